```python
import jax, jax.numpy as jnp
from jax import lax
import numpy as np

D_MODEL = 2048
BATCH = 4
SEQ = 2048
DEPTH = 1
DEC_BATCH = 32
DEC_SEQ = 1
PAST_LEN = 8192
PAGE_SIZE = 128

HEAD_DIM = 128
N_HEADS_ATTN = 6
N_HEADS_HGRN = 6
N_HEADS_MEM = 4
W_ATTN = N_HEADS_ATTN * HEAD_DIM
W_HGRN = N_HEADS_HGRN * HEAD_DIM
W_MEM = N_HEADS_MEM * HEAD_DIM
MIX_WIDTH = W_ATTN + W_HGRN + W_MEM
HGRN_EXPAND = HEAD_DIM
WINDOWS = (128, 512, 2048)
DILATIONS = (1, 4, 16)
MAX_WINDOW = max(WINDOWS)
N_MEM = 256
ROPE_THETA = 10000.0
HGRN_CHUNK = 64
LN_EPS = 1e-5
RMS_EPS = 1e-6
NEG_INF = -1e30
DEEPNORM_ALPHA = (2 * DEPTH) ** 0.25
DEEPNORM_BETA = (8 * DEPTH) ** -0.25
IN_WIDTHS = (W_ATTN,) * 4 + (W_HGRN,) * 4 + (W_MEM,) * 2
IN_PROJ = sum(IN_WIDTHS)

kernel_name = 'dilated_hgrn2_memory_hybrid_step'

F32 = jnp.float32


def _heads(a):
    return a.reshape(a.shape[:-1] + (-1, HEAD_DIM))


def _split_cols(proj):
    offsets = np.cumsum(IN_WIDTHS)[:-1].tolist()
    return jnp.split(proj, offsets, axis=-1)


def _rope(x, pos):
    half = HEAD_DIM // 2
    inv_freq = 1.0 / (ROPE_THETA ** (jnp.arange(half, dtype=F32) / half))
    ang = pos.astype(F32)[:, None] * inv_freq[None, :]
    cos = jnp.cos(ang)[None, :, None, :]
    sin = jnp.sin(ang)[None, :, None, :]
    xf = x.astype(F32)
    x1, x2 = xf[..., :half], xf[..., half:]
    return jnp.concatenate([x1 * cos - x2 * sin, x2 * cos + x1 * sin], axis=-1).astype(x.dtype)


def _dilated_prompt(q, k, v, window, dil):
    bsz, seq, nh, hd = q.shape
    blk = window // dil
    unit = dil * blk
    s_pad = -(-seq // unit) * unit
    m_len = s_pad // dil
    nb = m_len // blk
    pad = ((0, 0), (0, s_pad - seq), (0, 0), (0, 0))

    def to_blocks(a):
        a = jnp.pad(a.astype(F32), pad).reshape(bsz, m_len, dil, nh, hd)
        return a.transpose(0, 2, 3, 1, 4).reshape(bsz, dil, nh, nb, blk, hd)

    def with_prev(a):
        prev = jnp.pad(a, ((0, 0), (0, 0), (0, 0), (1, 0), (0, 0), (0, 0)))[:, :, :, :-1]
        return jnp.concatenate([prev, a], axis=4)

    qb = to_blocks(q)
    kb = with_prev(to_blocks(k))
    vb = with_prev(to_blocks(v))
    s = jnp.einsum('brhnqd,brhnkd->brhnqk', qb, kb) * hd ** -0.5
    qi = jnp.arange(blk)[:, None]
    ki = jnp.arange(2 * blk)[None, :]
    dist = qi + blk - ki
    band = (dist >= 0) & (dist <= blk)
    valid = band[None] & ((jnp.arange(nb)[:, None, None] > 0) | (ki[None] >= blk))
    s = jnp.where(valid, s, NEG_INF)
    m = jnp.max(s, axis=-1, keepdims=True)
    p = jnp.exp(s - m)
    den = jnp.sum(p, axis=-1)
    o = jnp.einsum('brhnqk,brhnkd->brhnqd', p, vb) / den[..., None]
    lse = m[..., 0] + jnp.log(den)
    o = o.reshape(bsz, dil, nh, m_len, hd).transpose(0, 3, 1, 2, 4).reshape(bsz, s_pad, nh, hd)[:, :seq]
    lse = lse.reshape(bsz, dil, nh, m_len).transpose(0, 3, 1, 2).reshape(bsz, s_pad, nh)[:, :seq]
    return o, lse


def _dilated_sample(q, k_all, v_all, window, dil):
    n_new = q.shape[1]
    past = k_all.shape[1] - n_new
    steps = window // dil
    idx = (past + jnp.arange(n_new))[:, None] - dil * jnp.arange(steps + 1)[None, :]
    valid = idx >= 0
    idx = jnp.maximum(idx, 0)
    kg = k_all[:, idx].astype(F32)
    vg = v_all[:, idx].astype(F32)
    s = jnp.einsum('blhd,blnhd->blhn', q.astype(F32), kg) * q.shape[-1] ** -0.5
    s = jnp.where(valid[None, :, None, :], s, NEG_INF)
    m = jnp.max(s, axis=-1, keepdims=True)
    p = jnp.exp(s - m)
    den = jnp.sum(p, axis=-1)
    o = jnp.einsum('blhn,blnhd->blhd', p, vg) / den[..., None]
    return o, m[..., 0] + jnp.log(den)


def _combine_dilations(outs, lses):
    wts = jax.nn.softmax(jnp.stack(lses), axis=0)
    return jnp.sum(wts[..., None] * jnp.stack(outs), axis=0)


def _hgrn_inputs(qb, fb, ib, lb):
    f = lb + (1.0 - lb) * jax.nn.sigmoid(fb.astype(F32))
    return (_heads(qb.astype(F32)), _heads(1.0 - f), _heads(ib.astype(F32)), _heads(jnp.log(f)))


def _hgrn_chunked(q, k, v, log_f, s0):
    bsz, n_tok, nh, kd = q.shape
    vd = v.shape[-1]
    c = min(HGRN_CHUNK, n_tok)
    n_chunks = -(-n_tok // c)
    l_pad = n_chunks * c
    pad = ((0, 0), (0, l_pad - n_tok), (0, 0), (0, 0))

    def chunks(a):
        a = jnp.pad(a, pad).reshape(bsz, n_chunks, c, nh, a.shape[-1])
        return a.transpose(1, 0, 3, 2, 4)

    tri = jnp.tril(jnp.ones((c, c), dtype=bool))

    def step(state, inp):
        qc, kc, vc, gc = inp
        b = jnp.cumsum(gc, axis=2)
        rel = b[:, :, :, None, :] - b[:, :, None, :, :]
        decay = jnp.where(tri[None, None, :, :, None], jnp.exp(jnp.minimum(rel, 0.0)), 0.0)
        att = jnp.einsum('bhtk,bhsk,bhtsk->bhts', qc, kc, decay)
        o = jnp.einsum('bhts,bhsv->bhtv', att, vc) + jnp.einsum('bhtk,bhkv->bhtv', qc * jnp.exp(b), state)
        b_last = b[:, :, -1, :]
        new_state = jnp.exp(b_last)[..., None] * state + jnp.einsum(
            'bhsk,bhsv->bhkv', kc * jnp.exp(b_last[:, :, None, :] - b), vc)
        return new_state, o

    s_fin, o = lax.scan(step, s0, (chunks(q), chunks(k), chunks(v), chunks(log_f)))
    o = o.transpose(1, 0, 3, 2, 4).reshape(bsz, l_pad, nh, vd)[:, :n_tok]
    return o, s_fin


def _mem_attend(q, mk, mv):
    s = jnp.einsum('blhd,bmhd->bhlm', q.astype(F32), mk.astype(F32)) * HEAD_DIM ** -0.5
    p = jax.nn.softmax(s, axis=-1)
    return jnp.einsum('bhlm,bmhd->blhd', p, mv.astype(F32))


def _merge(x, o_attn, ga, o_hgrn, gb, o_mem, gm, norm_g, w_out, ln_g, ln_b):
    bsz, n_tok = x.shape[:2]
    o_h = o_hgrn * lax.rsqrt(jnp.mean(o_hgrn * o_hgrn, axis=-1, keepdims=True) + RMS_EPS)
    o_h = o_h.reshape(bsz, n_tok, W_HGRN) * norm_g.astype(F32)
    z = jnp.concatenate([
        o_attn.reshape(bsz, n_tok, W_ATTN) * jax.nn.silu(ga.astype(F32)),
        o_h * jax.nn.silu(gb.astype(F32)),
        o_mem.reshape(bsz, n_tok, W_MEM) * jax.nn.silu(gm.astype(F32)),
    ], axis=-1).astype(x.dtype)
    y = z @ w_out
    r = DEEPNORM_ALPHA * x.astype(F32) + y.astype(F32)
    mu = jnp.mean(r, axis=-1, keepdims=True)
    var = jnp.mean((r - mu) ** 2, axis=-1, keepdims=True)
    out = (r - mu) * lax.rsqrt(var + LN_EPS) * ln_g.astype(F32) + ln_b.astype(F32)
    return out.astype(x.dtype)


def _prompt_layer(x, mem, w_in, w_mem_kv, lb, norm_g, w_out, ln_g, ln_b):
    bsz, seq, _ = x.shape
    pos = jnp.arange(seq)
    qa, ka, va, ga, qb, fb, ib, gb, qm, gm = _split_cols(x @ w_in)
    qa, ka, va = _rope(_heads(qa), pos), _rope(_heads(ka), pos), _heads(va)
    branches = [_dilated_prompt(qa, ka, va, w, d) for w, d in zip(WINDOWS, DILATIONS)]
    o_attn = _combine_dilations([br[0] for br in branches], [br[1] for br in branches])
    qh, kh, vh, logf = _hgrn_inputs(qb, fb, ib, lb)
    s0 = jnp.zeros((bsz, N_HEADS_HGRN, HGRN_EXPAND, HEAD_DIM), F32)
    o_hgrn, s_fin = _hgrn_chunked(qh, kh, vh, logf, s0)
    mk, mv = jnp.split(mem @ w_mem_kv, 2, axis=-1)
    mk, mv = _heads(mk), _heads(mv)
    o_mem = _mem_attend(_heads(qm), mk, mv)
    y = _merge(x, o_attn, ga, o_hgrn, gb, o_mem, gm, norm_g, w_out, ln_g, ln_b)
    keep = min(MAX_WINDOW, seq)
    return y, ka[:, seq - keep:], va[:, seq - keep:], s_fin, mk, mv


def _sample_layer(x, win_k, win_v, s_prev, mem_k, mem_v, w_in, lb, norm_g, w_out, ln_g, ln_b):
    n_new = x.shape[1]
    pos = PAST_LEN + jnp.arange(n_new)
    qa, ka, va, ga, qb, fb, ib, gb, qm, gm = _split_cols(x @ w_in)
    qa, ka, va = _rope(_heads(qa), pos), _rope(_heads(ka), pos), _heads(va)
    k_all = jnp.concatenate([win_k.astype(ka.dtype), ka], axis=1)
    v_all = jnp.concatenate([win_v.astype(va.dtype), va], axis=1)
    branches = [_dilated_sample(qa, k_all, v_all, w, d) for w, d in zip(WINDOWS, DILATIONS)]
    o_attn = _combine_dilations([br[0] for br in branches], [br[1] for br in branches])
    qh, kh, vh, logf = _hgrn_inputs(qb, fb, ib, lb)
    o_hgrn, s_new = _hgrn_chunked(qh, kh, vh, logf, s_prev.astype(F32))
    o_mem = _mem_attend(_heads(qm), mem_k, mem_v)
    y = _merge(x, o_attn, ga, o_hgrn, gb, o_mem, gm, norm_g, w_out, ln_g, ln_b)
    return y, ka, va, s_new.astype(s_prev.dtype)


def setup_inputs(seed: int = 0) -> dict:
    key = jax.random.key(seed)
    ks = jax.random.split(key, 16)
    w_buf = min(MAX_WINDOW, PAST_LEN)
    nrm = jax.random.normal
    return {
        'x_prompt': nrm(ks[0], (BATCH, SEQ, D_MODEL), F32),
        'x_sample': nrm(ks[1], (DEC_BATCH, DEC_SEQ, D_MODEL), F32),
        'cache_win_k': nrm(ks[2], (DEPTH, DEC_BATCH, w_buf, N_HEADS_ATTN, HEAD_DIM), F32),
        'cache_win_v': nrm(ks[3], (DEPTH, DEC_BATCH, w_buf, N_HEADS_ATTN, HEAD_DIM), F32),
        'state_hgrn': 0.5 * nrm(ks[4], (DEPTH, DEC_BATCH, N_HEADS_HGRN, HGRN_EXPAND, HEAD_DIM), F32),
        'cache_mem_k': nrm(ks[5], (DEPTH, DEC_BATCH, N_MEM, N_HEADS_MEM, HEAD_DIM), F32),
        'cache_mem_v': nrm(ks[6], (DEPTH, DEC_BATCH, N_MEM, N_HEADS_MEM, HEAD_DIM), F32),
        'mem_prompt': nrm(ks[7], (BATCH, N_MEM, D_MODEL), F32),
        'w_in': nrm(ks[8], (DEPTH, D_MODEL, IN_PROJ), F32) * D_MODEL ** -0.5,
        'w_mem_kv': nrm(ks[9], (DEPTH, D_MODEL, 2 * W_MEM), F32) * D_MODEL ** -0.5,
        'hgrn_lb_raw': 0.1 * nrm(ks[10], (DEPTH + 1, W_HGRN), F32),
        'hgrn_norm_g': 1.0 + 0.02 * nrm(ks[11], (DEPTH, W_HGRN), F32),
        'w_out': nrm(ks[12], (DEPTH, MIX_WIDTH, D_MODEL), F32) * (MIX_WIDTH ** -0.5 * DEEPNORM_BETA),
        'ln_g': 1.0 + 0.02 * nrm(ks[13], (DEPTH, D_MODEL), F32),
        'ln_b': 0.02 * nrm(ks[14], (DEPTH, D_MODEL), F32),
    }


def reference(x_prompt, x_sample, cache_win_k, cache_win_v, state_hgrn, cache_mem_k, cache_mem_v,
              mem_prompt, w_in, w_mem_kv, hgrn_lb_raw, hgrn_norm_g, w_out, ln_g, ln_b):
    lb_all = jnp.cumsum(jax.nn.softmax(hgrn_lb_raw.astype(F32), axis=0), axis=0)
    hp, hs = x_prompt, x_sample
    pk, pv, ps, pmk, pmv, sk, sv, ss = [], [], [], [], [], [], [], []
    for layer in range(DEPTH):
        hp, k1, v1, s1, mk1, mv1 = _prompt_layer(
            hp, mem_prompt, w_in[layer], w_mem_kv[layer], lb_all[layer], hgrn_norm_g[layer],
            w_out[layer], ln_g[layer], ln_b[layer])
        hs, k2, v2, s2 = _sample_layer(
            hs, cache_win_k[layer], cache_win_v[layer], state_hgrn[layer], cache_mem_k[layer],
            cache_mem_v[layer], w_in[layer], lb_all[layer], hgrn_norm_g[layer], w_out[layer],
            ln_g[layer], ln_b[layer])
        pk.append(k1); pv.append(v1); ps.append(s1); pmk.append(mk1); pmv.append(mv1)
        sk.append(k2); sv.append(v2); ss.append(s2)
    return (hp, hs, jnp.stack(pk), jnp.stack(pv), jnp.stack(ps), jnp.stack(pmk), jnp.stack(pmv),
            jnp.stack(sk), jnp.stack(sv), jnp.stack(ss))
```

```python
import functools

import jax
import jax.numpy as jnp
import numpy as np
from jax import lax
from jax.experimental import pallas as pl
from jax.experimental.pallas import tpu as pltpu

F32 = jnp.float32
BF16 = jnp.bfloat16

D_MODEL = 2048
DEPTH = 1
PAST_LEN = 8192
HEAD_DIM = 128
N_HEADS_ATTN = 6
N_HEADS_HGRN = 6
N_HEADS_MEM = 4
W_ATTN = N_HEADS_ATTN * HEAD_DIM
W_HGRN = N_HEADS_HGRN * HEAD_DIM
W_MEM = N_HEADS_MEM * HEAD_DIM
MIX_WIDTH = W_ATTN + W_HGRN + W_MEM
WINDOWS = (128, 512, 2048)
DILATIONS = (1, 4, 16)
N_MEM = 256
ROPE_THETA = 10000.0
LN_EPS = 1e-5
RMS_EPS = 1e-6
NEG_INF = -1e30
DEEPNORM_ALPHA = (2 * DEPTH) ** 0.25
ATTN_SCALE = HEAD_DIM ** -0.5

G_QA, G_KA, G_VA, G_GA = 0, 6, 12, 18
G_QB, G_FB, G_IB, G_GB = 24, 30, 36, 42
G_QM, G_GM = 48, 52
N_GROUPS = 56

LANES = 128
BAND = 128
HGRN_CHUNK = 64
HGRN_SUB = 16
VMEM_LIMIT = 48 * 1024 * 1024


def _params(sem, vmem=VMEM_LIMIT):
    return pltpu.CompilerParams(dimension_semantics=sem, vmem_limit_bytes=vmem)


def _sigmoid(x):
    return 1.0 / (1.0 + jnp.exp(-x))


def _silu(x):
    return x * _sigmoid(x)


def _dot_nt(a, b):
    return lax.dot_general(a, b, (((1,), (1,)), ((), ())), preferred_element_type=F32)


def _proj_kernel(x_ref, w_ref, o_ref, xb_ref):
    @pl.when(pl.program_id(1) == 0)
    def _():
        xb_ref[...] = x_ref[...].astype(BF16)

    acc = jnp.dot(xb_ref[...], w_ref[...], preferred_element_type=F32)
    for j in range(o_ref.shape[0]):
        o_ref[j] = acc[:, j * LANES:(j + 1) * LANES]


def _project(x, w_bf16, tm, tn):
    m, k = x.shape
    n = w_bf16.shape[1]
    return pl.pallas_call(
        _proj_kernel,
        grid=(m // tm, n // tn),
        in_specs=[pl.BlockSpec((tm, k), lambda i, j: (i, 0)),
                  pl.BlockSpec((k, tn), lambda i, j: (0, j))],
        out_specs=pl.BlockSpec((tn // LANES, tm, LANES), lambda i, j: (j, i, 0)),
        out_shape=jax.ShapeDtypeStruct((n // LANES, m, LANES), F32),
        scratch_shapes=[pltpu.VMEM((tm, k), BF16)],
        compiler_params=_params(("parallel", "arbitrary")),
        name="in_proj",
    )(x, w_bf16)


def _rope(x, cos, sin_signed):
    return x * cos + pltpu.roll(x, HEAD_DIM // 2, 1) * sin_signed


def _band_bias():
    qi = lax.broadcasted_iota(jnp.int32, (BAND, BAND), 0)
    ki = lax.broadcasted_iota(jnp.int32, (BAND, BAND), 1)
    prev = jnp.where(ki >= qi, 0.0, NEG_INF).astype(F32)
    cur = jnp.where(ki <= qi, 0.0, NEG_INF).astype(F32)
    return prev, cur


def _softmax_part(qb, kb, vb, bias):
    s = _dot_nt(qb.astype(BF16), kb.astype(BF16)) + bias
    m = jnp.max(s, axis=-1, keepdims=True)
    p = jnp.exp(s - m)
    l = jnp.sum(p, axis=-1, keepdims=True)
    acc = jnp.dot(p.astype(BF16), vb.astype(BF16), preferred_element_type=F32)
    return acc, m, l


def _attn_kernel(q_ref, k_ref, v_ref, g_ref, cos_ref, sin_ref, z_ref, ko_ref, vo_ref,
                 qs_ref, acc_ref, m_ref, l_ref):
    seq = q_ref.shape[0]
    n_blk = seq // BAND
    cos = cos_ref[...]
    sin = sin_ref[...]
    qs_ref[...] = _rope(q_ref[...], cos, sin) * ATTN_SCALE
    ko_ref[...] = _rope(k_ref[...], cos, sin)
    vo_ref[...] = v_ref[...]
    bias_prev, bias_cur = _band_bias()
    bias_both = jnp.concatenate([bias_prev, bias_cur], axis=1)

    for br, dil in enumerate(DILATIONS):
        n_cls_blk = n_blk // dil
        for r in range(dil):
            for n in range(n_cls_blk):
                base = r + dil * BAND * n
                rows = pl.ds(base, BAND, stride=dil) if dil > 1 else pl.ds(base, BAND)
                if n == 0:
                    keys, bias = rows, bias_cur
                else:
                    kbase = base - dil * BAND
                    keys = pl.ds(kbase, 2 * BAND, stride=dil) if dil > 1 else pl.ds(kbase, 2 * BAND)
                    bias = bias_both
                acc, m, l = _softmax_part(qs_ref[rows, :], ko_ref[keys, :], v_ref[keys, :], bias)
                acc_ref[br, rows, :] = acc
                m_ref[br, rows, :] = m
                l_ref[br, rows, :] = l

    for n in range(n_blk):
        rows = pl.ds(n * BAND, BAND)
        ms = [m_ref[br, rows, :] for br in range(len(DILATIONS))]
        m_all = functools.reduce(jnp.maximum, ms)
        num = jnp.zeros((BAND, HEAD_DIM), F32)
        den = jnp.zeros((BAND, 1), F32)
        for br in range(len(DILATIONS)):
            w = jnp.exp(ms[br] - m_all)
            num = num + w * acc_ref[br, rows, :]
            den = den + w * l_ref[br, rows, :]
        z_ref[rows, :] = ((num / den) * _silu(g_ref[rows, :])).astype(z_ref.dtype)


def _prompt_attention(proj, cos, sin, bsz, seq):
    def col(g0):
        return pl.BlockSpec((None, seq, HEAD_DIM), lambda b, h: (g0 + h, b, 0))

    table = pl.BlockSpec((seq, HEAD_DIM), lambda b, h: (0, 0))
    kv_out = pl.BlockSpec((None, seq, HEAD_DIM), lambda b, h: (b, 0, h))
    n_br = len(DILATIONS)
    return pl.pallas_call(
        _attn_kernel,
        grid=(bsz, N_HEADS_ATTN),
        in_specs=[col(G_QA), col(G_KA), col(G_VA), col(G_GA), table, table],
        out_specs=[pl.BlockSpec((None, seq, HEAD_DIM), lambda b, h: (h, b, 0)), kv_out, kv_out],
        out_shape=[jax.ShapeDtypeStruct((N_HEADS_ATTN, bsz * seq, HEAD_DIM), BF16),
                   jax.ShapeDtypeStruct((bsz, seq, W_ATTN), F32),
                   jax.ShapeDtypeStruct((bsz, seq, W_ATTN), F32)],
        scratch_shapes=[pltpu.VMEM((seq, HEAD_DIM), F32),
                        pltpu.VMEM((n_br, seq, HEAD_DIM), F32),
                        pltpu.VMEM((n_br, seq, 1), F32),
                        pltpu.VMEM((n_br, seq, 1), F32)],
        compiler_params=_params(("parallel", "parallel")),
        name="prompt_attn",
    )(proj, proj, proj, proj, cos, sin)


def _lower_bound(lb_raw, layer):
    e = jnp.exp(lb_raw - jnp.max(lb_raw, axis=0, keepdims=True))
    sm = e / jnp.sum(e, axis=0, keepdims=True)
    return jnp.sum(sm[:layer + 1], axis=0, keepdims=True)


def _split3(x):
    hi = x.astype(BF16)
    r1 = x - hi.astype(F32)
    mid = r1.astype(BF16)
    lo = (r1 - mid.astype(F32)).astype(BF16)
    return hi, mid, lo


def _rms_gate(o, norm_g, gate):
    o = o * lax.rsqrt(jnp.mean(o * o, axis=-1, keepdims=True) + RMS_EPS)
    return o * norm_g * _silu(gate)


def _hgrn_chunk(q, fb, v, lb, st, tril):
    c = q.shape[0]
    f = lb + (1.0 - lb) * _sigmoid(fb)
    kk = 1.0 - f
    g = jnp.log(f)
    b = sum(jnp.dot(tril, piece, preferred_element_type=F32) for piece in _split3(g))
    o = _dot_nt((q * jnp.exp(b)).astype(BF16), st.astype(BF16))

    s_idx = lax.broadcasted_iota(jnp.int32, (c, 1), 0)
    lane = lax.broadcasted_iota(jnp.int32, (HGRN_SUB, c), 1)
    row = lax.broadcasted_iota(jnp.int32, (HGRN_SUB, c), 0)
    att_rows = []
    for i0 in range(0, c, HGRN_SUB):
        qi = q[i0:i0 + HGRN_SUB]
        bi = b[i0:i0 + HGRN_SUB]
        if i0 > 0:
            bref = b[i0 - 1:i0]
            kt = jnp.where(s_idx < i0, kk * jnp.exp(jnp.minimum(bref - b, 0.0)), 0.0)
            att = _dot_nt((qi * jnp.exp(bi - bref)).astype(BF16), kt.astype(BF16))
        else:
            att = jnp.zeros((HGRN_SUB, c), F32)
        for j in range(HGRN_SUB):
            s = i0 + j
            e = jnp.exp(jnp.minimum(bi - b[s:s + 1], 0.0))
            colv = jnp.sum(qi * kk[s:s + 1] * e, axis=-1, keepdims=True)
            att = jnp.where((lane == s) & (row >= j), colv, att)
        att_rows.append(att)
    att = jnp.concatenate(att_rows, axis=0)
    vb = v.astype(BF16)
    o = o + jnp.dot(att.astype(BF16), vb, preferred_element_type=F32)

    b_last = b[c - 1:c]
    kd = (kk * jnp.exp(b_last - b)).astype(BF16)
    st_new = st * jnp.exp(b_last) + lax.dot_general(vb, kd, (((0,), (0,)), ((), ())),
                                                    preferred_element_type=F32)
    return o, st_new


def _tril_ones(c):
    return (lax.broadcasted_iota(jnp.int32, (c, c), 0) >= lax.broadcasted_iota(jnp.int32, (c, c), 1)
            ).astype(BF16)


def _hgrn_kernel(layer, q_ref, f_ref, i_ref, g_ref, lb_ref, ng_ref, z_ref, s_ref, st_ref):
    seq = q_ref.shape[0]
    lb = _lower_bound(lb_ref[...], layer)
    norm_g = ng_ref[...]
    tril = _tril_ones(HGRN_CHUNK)
    st_ref[...] = jnp.zeros_like(st_ref)

    def body(ci, carry):
        rows = pl.ds(pl.multiple_of(ci * HGRN_CHUNK, HGRN_CHUNK), HGRN_CHUNK)
        o, st = _hgrn_chunk(q_ref[rows, :], f_ref[rows, :], i_ref[rows, :], lb, st_ref[...], tril)
        st_ref[...] = st
        z_ref[rows, :] = _rms_gate(o, norm_g, g_ref[rows, :]).astype(z_ref.dtype)
        return carry

    lax.fori_loop(0, seq // HGRN_CHUNK, body, 0)
    s_ref[...] = st_ref[...].T


def _prompt_hgrn(proj, lb_raw, norm_g, layer, bsz, seq):
    def col(g0):
        return pl.BlockSpec((None, seq, HEAD_DIM), lambda b, h: (g0 + h, b, 0))

    return pl.pallas_call(
        functools.partial(_hgrn_kernel, layer),
        grid=(bsz, N_HEADS_HGRN),
        in_specs=[col(G_QB), col(G_FB), col(G_IB), col(G_GB),
                  pl.BlockSpec((DEPTH + 1, HEAD_DIM), lambda b, h: (0, h)),
                  pl.BlockSpec((1, HEAD_DIM), lambda b, h: (0, h))],
        out_specs=[pl.BlockSpec((None, seq, HEAD_DIM), lambda b, h: (h, b, 0)),
                   pl.BlockSpec((None, None, HEAD_DIM, HEAD_DIM), lambda b, h: (b, h, 0, 0))],
        out_shape=[jax.ShapeDtypeStruct((N_HEADS_HGRN, bsz * seq, HEAD_DIM), BF16),
                   jax.ShapeDtypeStruct((bsz, N_HEADS_HGRN, HEAD_DIM, HEAD_DIM), F32)],
        scratch_shapes=[pltpu.VMEM((HEAD_DIM, HEAD_DIM), F32)],
        compiler_params=_params(("parallel", "parallel")),
        name="prompt_hgrn",
    )(proj, proj, proj, proj, lb_raw, norm_g)


MEM_ROWS = 512


def _mem_kernel(q_ref, g_ref, mk_ref, mv_ref, z_ref):
    seq = q_ref.shape[0]
    mk = mk_ref[...].astype(BF16)
    mv = mv_ref[...].astype(BF16)
    for n in range(seq // MEM_ROWS):
        rows = pl.ds(n * MEM_ROWS, MEM_ROWS)
        s = _dot_nt((q_ref[rows, :] * ATTN_SCALE).astype(BF16), mk)
        p = jnp.exp(s - jnp.max(s, axis=-1, keepdims=True))
        den = jnp.sum(p, axis=-1, keepdims=True)
        o = jnp.dot(p.astype(BF16), mv, preferred_element_type=F32) / den
        z_ref[rows, :] = (o * _silu(g_ref[rows, :])).astype(z_ref.dtype)


def _prompt_mem(proj, mkv, bsz, seq):
    def col(g0):
        return pl.BlockSpec((None, seq, HEAD_DIM), lambda b, h: (g0 + h, b, 0))

    return pl.pallas_call(
        _mem_kernel,
        grid=(bsz, N_HEADS_MEM),
        in_specs=[col(G_QM), col(G_GM),
                  pl.BlockSpec((None, N_MEM, HEAD_DIM), lambda b, h: (h, b, 0)),
                  pl.BlockSpec((None, N_MEM, HEAD_DIM), lambda b, h: (N_HEADS_MEM + h, b, 0))],
        out_specs=pl.BlockSpec((None, seq, HEAD_DIM), lambda b, h: (h, b, 0)),
        out_shape=jax.ShapeDtypeStruct((N_HEADS_MEM, bsz * seq, HEAD_DIM), BF16),
        compiler_params=_params(("parallel", "parallel")),
        name="prompt_mem",
    )(proj, proj, mkv, mkv)


def _merge_kernel(za_ref, zh_ref, zm_ref, x_ref, w_ref, lg_ref, lb_ref, o_ref, z_ref):
    c0 = 0
    for ref in (za_ref, zh_ref, zm_ref):
        for c in range(ref.shape[0]):
            z_ref[:, (c0 + c) * LANES:(c0 + c + 1) * LANES] = ref[c].astype(BF16)
        c0 += ref.shape[0]
    y = jnp.dot(z_ref[...], w_ref[...], preferred_element_type=F32)
    r = DEEPNORM_ALPHA * x_ref[...] + y
    mu = jnp.mean(r, axis=-1, keepdims=True)
    d = r - mu
    var = jnp.mean(d * d, axis=-1, keepdims=True)
    o_ref[...] = d * lax.rsqrt(var + LN_EPS) * lg_ref[...] + lb_ref[...]


def _merge(za, zh, zm, x, w_out_bf16, ln_g, ln_b, tm):
    m = x.shape[0]

    def slab(a):
        return pl.BlockSpec((a.shape[0], tm, LANES), lambda i: (0, i, 0))

    const = lambda shape: pl.BlockSpec(shape, lambda i: (0, 0))
    return pl.pallas_call(
        _merge_kernel,
        grid=(m // tm,),
        in_specs=[slab(za), slab(zh), slab(zm), pl.BlockSpec((tm, D_MODEL), lambda i: (i, 0)),
                  const((MIX_WIDTH, D_MODEL)), const((1, D_MODEL)), const((1, D_MODEL))],
        out_specs=pl.BlockSpec((tm, D_MODEL), lambda i: (i, 0)),
        out_shape=jax.ShapeDtypeStruct((m, D_MODEL), F32),
        scratch_shapes=[pltpu.VMEM((tm, MIX_WIDTH), BF16)],
        compiler_params=_params(("parallel",)),
        name="merge",
    )(za, zh, zm, x, w_out_bf16, ln_g, ln_b)


def _column(row):
    return jnp.broadcast_to(row, (HEAD_DIM, HEAD_DIM)).T


def _sample_kernel(layer, p_ref, cos_ref, sin_ref, k1_ref, k4_ref, k16_ref, v1_ref, v4_ref, v16_ref,
                   st_ref, mk_ref, mv_ref, lb_ref, ng_ref, z_ref, ko_ref, vo_ref, so_ref):
    cos = cos_ref[...]
    sin = sin_ref[...]
    q_all = _rope(p_ref[G_QA:G_QA + N_HEADS_ATTN, :], cos, sin) * ATTN_SCALE
    k_all = _rope(p_ref[G_KA:G_KA + N_HEADS_ATTN, :], cos, sin)
    v_all = p_ref[G_VA:G_VA + N_HEADS_ATTN, :]
    ko_ref[...] = k_all
    vo_ref[...] = v_all
    caches = ((k1_ref, v1_ref), (k4_ref, v4_ref), (k16_ref, v16_ref))

    for h in range(N_HEADS_ATTN):
        lanes = slice(h * HEAD_DIM, (h + 1) * HEAD_DIM)
        q = q_all[h:h + 1]
        s_new = jnp.sum(q * k_all[h:h + 1], axis=-1, keepdims=True)
        scores = [jnp.sum(kr[:, lanes] * q, axis=-1, keepdims=True) for kr, _ in caches]
        m = s_new
        for s in scores:
            m = jnp.maximum(m, jnp.max(s, axis=0, keepdims=True))
        p_new = jnp.exp(s_new - m) * len(caches)
        num = p_new * v_all[h:h + 1]
        den = p_new
        for s, (_, vr) in zip(scores, caches):
            p = jnp.exp(s - m)
            den = den + jnp.sum(p, axis=0, keepdims=True)
            num = num + jnp.sum(p * vr[:, lanes], axis=0, keepdims=True)
        z_ref[h:h + 1, :] = (num / den) * _silu(p_ref[G_GA + h:G_GA + h + 1, :])

    lb_all = lb_ref[...]
    for h in range(N_HEADS_HGRN):
        lanes = slice(h * HEAD_DIM, (h + 1) * HEAD_DIM)
        lb = _lower_bound(lb_all[:, lanes], layer)
        f = lb + (1.0 - lb) * _sigmoid(p_ref[G_FB + h:G_FB + h + 1, :])
        f_col = _column(f)
        q_col = _column(p_ref[G_QB + h:G_QB + h + 1, :])
        s_new = f_col * st_ref[h] + (1.0 - f_col) * p_ref[G_IB + h:G_IB + h + 1, :]
        so_ref[h] = s_new
        o = jnp.sum(s_new * q_col, axis=0, keepdims=True)
        z_ref[N_HEADS_ATTN + h:N_HEADS_ATTN + h + 1, :] = _rms_gate(
            o, ng_ref[:, lanes], p_ref[G_GB + h:G_GB + h + 1, :])

    for h in range(N_HEADS_MEM):
        lanes = slice(h * HEAD_DIM, (h + 1) * HEAD_DIM)
        q = p_ref[G_QM + h:G_QM + h + 1, :] * ATTN_SCALE
        s = jnp.sum(mk_ref[:, lanes] * q, axis=-1, keepdims=True)
        p = jnp.exp(s - jnp.max(s, axis=0, keepdims=True))
        o = jnp.sum(p * mv_ref[:, lanes], axis=0, keepdims=True) / jnp.sum(p, axis=0, keepdims=True)
        row = N_HEADS_ATTN + N_HEADS_HGRN + h
        z_ref[row:row + 1, :] = o * _silu(p_ref[G_GM + h:G_GM + h + 1, :])


def _sample_mixers(proj_rows, cos, sin, win_k, win_v, state, mem_k, mem_v, lb_raw, norm_g, layer):
    bsz, past = win_k.shape[:2]
    assert all(past % d == 0 and past >= w for w, d in zip(WINDOWS, DILATIONS))
    assert all(w // d == BAND for w, d in zip(WINDOWS, DILATIONS))

    def strided_rows(a, dil):
        view = a.reshape(bsz, past // dil, dil * W_ATTN)
        return view, pl.BlockSpec((None, BAND, W_ATTN), lambda b: (b, past // dil // BAND - 1, 0))

    views, specs = zip(*[strided_rows(a, d) for a in (win_k, win_v) for d in DILATIONS])
    vec = lambda n: pl.BlockSpec((n, W_HGRN), lambda b: (0, 0))
    heads = lambda n: pl.BlockSpec((None, n, HEAD_DIM), lambda b: (b, 0, 0))
    return pl.pallas_call(
        functools.partial(_sample_kernel, layer),
        grid=(bsz,),
        in_specs=[heads(N_GROUPS), pl.BlockSpec((1, HEAD_DIM), lambda b: (0, 0)),
                  pl.BlockSpec((1, HEAD_DIM), lambda b: (0, 0)), *specs,
                  pl.BlockSpec((None, N_HEADS_HGRN, HEAD_DIM, HEAD_DIM), lambda b: (b, 0, 0, 0)),
                  pl.BlockSpec((None, N_MEM, W_MEM), lambda b: (b, 0, 0)),
                  pl.BlockSpec((None, N_MEM, W_MEM), lambda b: (b, 0, 0)),
                  vec(DEPTH + 1), vec(1)],
        out_specs=[heads(MIX_WIDTH // HEAD_DIM), heads(N_HEADS_ATTN), heads(N_HEADS_ATTN),
                   pl.BlockSpec((None, N_HEADS_HGRN, HEAD_DIM, HEAD_DIM), lambda b: (b, 0, 0, 0))],
        out_shape=[jax.ShapeDtypeStruct((bsz, MIX_WIDTH // HEAD_DIM, HEAD_DIM), F32),
                   jax.ShapeDtypeStruct((bsz, N_HEADS_ATTN, HEAD_DIM), F32),
                   jax.ShapeDtypeStruct((bsz, N_HEADS_ATTN, HEAD_DIM), F32),
                   jax.ShapeDtypeStruct((bsz, N_HEADS_HGRN, HEAD_DIM, HEAD_DIM), F32)],
        compiler_params=_params(("parallel",)),
        name="sample_mixers",
    )(proj_rows, cos, sin, *views, state, mem_k.reshape(bsz, N_MEM, W_MEM), mem_v.reshape(bsz, N_MEM, W_MEM),
      lb_raw, norm_g)


def _rope_tables(pos):
    half = HEAD_DIM // 2
    inv_freq = 1.0 / (ROPE_THETA ** (jnp.arange(half, dtype=F32) / half))
    ang = pos.astype(F32)[:, None] * inv_freq[None, :]
    cos, sin = jnp.cos(ang), jnp.sin(ang)
    return jnp.concatenate([cos, cos], axis=-1), jnp.concatenate([-sin, sin], axis=-1)


def kernel(x_prompt, x_sample, cache_win_k, cache_win_v, state_hgrn, cache_mem_k, cache_mem_v, mem_prompt,
           w_in, w_mem_kv, hgrn_lb_raw, hgrn_norm_g, w_out, ln_g, ln_b):
    bsz, seq, _ = x_prompt.shape
    dbsz, n_new, _ = x_sample.shape
    assert n_new == 1 and seq % (BAND * max(DILATIONS)) == 0
    cos_p, sin_p = _rope_tables(jnp.arange(seq))
    cos_s, sin_s = _rope_tables(PAST_LEN + jnp.arange(n_new))

    hp = x_prompt.reshape(bsz * seq, D_MODEL)
    hs = x_sample.reshape(dbsz * n_new, D_MODEL)
    mem = mem_prompt.reshape(bsz * N_MEM, D_MODEL)
    outs = [[] for _ in range(8)]
    for layer in range(DEPTH):
        w_in_b = w_in[layer].astype(BF16)
        w_out_b = w_out[layer].astype(BF16)
        lb_raw = hgrn_lb_raw
        norm_g = hgrn_norm_g[layer][None]
        lg, lbias = ln_g[layer][None], ln_b[layer][None]

        proj = _project(hp, w_in_b, 1024, 1024)
        mkv = _project(mem, w_mem_kv[layer].astype(BF16), bsz * N_MEM, 2 * W_MEM)
        za, k1, v1 = _prompt_attention(proj, cos_p, sin_p, bsz, seq)
        zh, s1 = _prompt_hgrn(proj, lb_raw, norm_g, layer, bsz, seq)
        zm = _prompt_mem(proj, mkv, bsz, seq)
        hp = _merge(za, zh, zm, hp, w_out_b, lg, lbias, 256)
        mkv_rows = mkv.reshape(2, N_HEADS_MEM, bsz, N_MEM, HEAD_DIM).transpose(0, 2, 3, 1, 4)

        proj_s = _project(hs, w_in_b, dbsz, 1024).transpose(1, 0, 2)
        zs, k2, v2, s2 = _sample_mixers(proj_s, cos_s, sin_s, cache_win_k[layer], cache_win_v[layer],
                                        state_hgrn[layer], cache_mem_k[layer], cache_mem_v[layer],
                                        lb_raw, norm_g, layer)
        zs = zs.transpose(1, 0, 2)
        hs = _merge(zs[:N_HEADS_ATTN], zs[N_HEADS_ATTN:N_HEADS_ATTN + N_HEADS_HGRN],
                    zs[N_HEADS_ATTN + N_HEADS_HGRN:], hs, w_out_b, lg, lbias, dbsz)

        new = (k1.reshape(bsz, seq, N_HEADS_ATTN, HEAD_DIM), v1.reshape(bsz, seq, N_HEADS_ATTN, HEAD_DIM), s1,
               mkv_rows[0], mkv_rows[1],
               k2.reshape(dbsz, n_new, N_HEADS_ATTN, HEAD_DIM), v2.reshape(dbsz, n_new, N_HEADS_ATTN, HEAD_DIM),
               s2.astype(state_hgrn.dtype))
        for acc, val in zip(outs, new):
            acc.append(val)

    return (hp.reshape(bsz, seq, D_MODEL), hs.reshape(dbsz, n_new, D_MODEL), *[jnp.stack(o) for o in outs])
```

```python
import functools

import jax
import jax.numpy as jnp
import numpy as np
from jax import lax
from jax.experimental import pallas as pl
from jax.experimental.pallas import tpu as pltpu

F32 = jnp.float32
BF16 = jnp.bfloat16

D_MODEL = 2048
DEPTH = 1
PAST_LEN = 8192
HEAD_DIM = 128
N_HEADS_ATTN = 6
N_HEADS_HGRN = 6
N_HEADS_MEM = 4
W_ATTN = N_HEADS_ATTN * HEAD_DIM
W_HGRN = N_HEADS_HGRN * HEAD_DIM
W_MEM = N_HEADS_MEM * HEAD_DIM
MIX_WIDTH = W_ATTN + W_HGRN + W_MEM
WINDOWS = (128, 512, 2048)
DILATIONS = (1, 4, 16)
N_MEM = 256
ROPE_THETA = 10000.0
LN_EPS = 1e-5
RMS_EPS = 1e-6
NEG_INF = -1e30
DEEPNORM_ALPHA = (2 * DEPTH) ** 0.25
ATTN_SCALE = HEAD_DIM ** -0.5

G_QA, G_KA, G_VA, G_GA = 0, 6, 12, 18
G_QB, G_FB, G_IB, G_GB = 24, 30, 36, 42
G_QM, G_GM = 48, 52
N_GROUPS = 56

LANES = 128
BAND = 128
HGRN_CHUNK = 64
HGRN_SUB = 16
VMEM_LIMIT = 48 * 1024 * 1024


def _params(sem, vmem=VMEM_LIMIT):
    return pltpu.CompilerParams(dimension_semantics=sem, vmem_limit_bytes=vmem)


def _sigmoid(x):
    return 1.0 / (1.0 + jnp.exp(-x))


def _silu(x):
    return x * _sigmoid(x)


def _dot_nt(a, b):
    return lax.dot_general(a, b, (((1,), (1,)), ((), ())), preferred_element_type=F32)


def _proj_kernel(x_ref, w_ref, o_ref, xb_ref):
    @pl.when(pl.program_id(1) == 0)
    def _():
        xb_ref[...] = x_ref[...].astype(BF16)

    acc = jnp.dot(xb_ref[...], w_ref[...], preferred_element_type=F32)
    for j in range(o_ref.shape[0]):
        o_ref[j] = acc[:, j * LANES:(j + 1) * LANES]


def _project(x, w_bf16, tm, tn):
    m, k = x.shape
    n = w_bf16.shape[1]
    return pl.pallas_call(
        _proj_kernel,
        grid=(m // tm, n // tn),
        in_specs=[pl.BlockSpec((tm, k), lambda i, j: (i, 0)),
                  pl.BlockSpec((k, tn), lambda i, j: (0, j))],
        out_specs=pl.BlockSpec((tn // LANES, tm, LANES), lambda i, j: (j, i, 0)),
        out_shape=jax.ShapeDtypeStruct((n // LANES, m, LANES), F32),
        scratch_shapes=[pltpu.VMEM((tm, k), BF16)],
        compiler_params=_params(("parallel", "arbitrary")),
        name="in_proj",
    )(x, w_bf16)


def _rope(x, cos, sin_signed):
    return x * cos + pltpu.roll(x, HEAD_DIM // 2, 1) * sin_signed


def _band_bias():
    qi = lax.broadcasted_iota(jnp.int32, (BAND, BAND), 0)
    ki = lax.broadcasted_iota(jnp.int32, (BAND, BAND), 1)
    prev = jnp.where(ki >= qi, 0.0, NEG_INF).astype(F32)
    cur = jnp.where(ki <= qi, 0.0, NEG_INF).astype(F32)
    return prev, cur


def _softmax_part(qb, kb, vb, bias):
    s = _dot_nt(qb.astype(BF16), kb.astype(BF16)) + bias
    m = jnp.max(s, axis=-1, keepdims=True)
    p = jnp.exp(s - m)
    l = jnp.sum(p, axis=-1, keepdims=True)
    acc = jnp.dot(p.astype(BF16), vb.astype(BF16), preferred_element_type=F32)
    return acc, m, l


def _attn_kernel(q_ref, k_ref, vin_ref, g_ref, cos_ref, sin_ref, z_ref, ko_hbm, vo_hbm,
                 qs_ref, ko_ref, v_ref, acc_ref, m_ref, l_ref, sem):
    seq = q_ref.shape[0]
    n_blk = seq // BAND
    cos = cos_ref[...]
    sin = sin_ref[...]
    qs_ref[...] = _rope(q_ref[...], cos, sin) * ATTN_SCALE
    ko_ref[...] = _rope(k_ref[...], cos, sin)
    v_ref[...] = vin_ref[...]
    b, h = pl.program_id(0), pl.program_id(1)
    k_copy = pltpu.make_async_copy(ko_ref, ko_hbm.at[b, :, h, :], sem.at[0])
    v_copy = pltpu.make_async_copy(v_ref, vo_hbm.at[b, :, h, :], sem.at[1])
    k_copy.start()
    v_copy.start()
    bias_prev, bias_cur = _band_bias()
    bias_both = jnp.concatenate([bias_prev, bias_cur], axis=1)

    for br, dil in enumerate(DILATIONS):
        n_cls_blk = n_blk // dil
        for r in range(dil):
            for n in range(n_cls_blk):
                base = r + dil * BAND * n
                rows = pl.ds(base, BAND, stride=dil) if dil > 1 else pl.ds(base, BAND)
                if n == 0:
                    keys, bias = rows, bias_cur
                else:
                    kbase = base - dil * BAND
                    keys = pl.ds(kbase, 2 * BAND, stride=dil) if dil > 1 else pl.ds(kbase, 2 * BAND)
                    bias = bias_both
                acc, m, l = _softmax_part(qs_ref[rows, :], ko_ref[keys, :], v_ref[keys, :], bias)
                acc_ref[br, rows, :] = acc
                m_ref[br, rows, :] = m
                l_ref[br, rows, :] = l

    for n in range(n_blk):
        rows = pl.ds(n * BAND, BAND)
        ms = [m_ref[br, rows, :] for br in range(len(DILATIONS))]
        m_all = functools.reduce(jnp.maximum, ms)
        num = jnp.zeros((BAND, HEAD_DIM), F32)
        den = jnp.zeros((BAND, 1), F32)
        for br in range(len(DILATIONS)):
            w = jnp.exp(ms[br] - m_all)
            num = num + w * acc_ref[br, rows, :]
            den = den + w * l_ref[br, rows, :]
        z_ref[rows, :] = ((num / den) * _silu(g_ref[rows, :])).astype(z_ref.dtype)
    k_copy.wait()
    v_copy.wait()


def _prompt_attention(proj, cos, sin, bsz, seq):
    def col(g0):
        return pl.BlockSpec((None, seq, HEAD_DIM), lambda b, h: (g0 + h, b, 0))

    table = pl.BlockSpec((seq, HEAD_DIM), lambda b, h: (0, 0))
    kv_out = pl.BlockSpec(memory_space=pl.ANY)
    n_br = len(DILATIONS)
    return pl.pallas_call(
        _attn_kernel,
        grid=(bsz, N_HEADS_ATTN),
        in_specs=[col(G_QA), col(G_KA), col(G_VA), col(G_GA), table, table],
        out_specs=[pl.BlockSpec((None, seq, HEAD_DIM), lambda b, h: (h, b, 0)), kv_out, kv_out],
        out_shape=[jax.ShapeDtypeStruct((N_HEADS_ATTN, bsz * seq, HEAD_DIM), BF16),
                   jax.ShapeDtypeStruct((bsz, seq, N_HEADS_ATTN, HEAD_DIM), F32),
                   jax.ShapeDtypeStruct((bsz, seq, N_HEADS_ATTN, HEAD_DIM), F32)],
        scratch_shapes=[pltpu.VMEM((seq, HEAD_DIM), F32),
                        pltpu.VMEM((seq, HEAD_DIM), F32),
                        pltpu.VMEM((seq, HEAD_DIM), F32),
                        pltpu.VMEM((n_br, seq, HEAD_DIM), F32),
                        pltpu.VMEM((n_br, seq, 1), F32),
                        pltpu.VMEM((n_br, seq, 1), F32),
                        pltpu.SemaphoreType.DMA((2,))],
        compiler_params=_params(("parallel", "parallel")),
        name="prompt_attn",
    )(proj, proj, proj, proj, cos, sin)


def _lower_bound(lb_raw, layer):
    e = jnp.exp(lb_raw - jnp.max(lb_raw, axis=0, keepdims=True))
    sm = e / jnp.sum(e, axis=0, keepdims=True)
    return jnp.sum(sm[:layer + 1], axis=0, keepdims=True)


def _split3(x):
    hi = x.astype(BF16)
    r1 = x - hi.astype(F32)
    mid = r1.astype(BF16)
    lo = (r1 - mid.astype(F32)).astype(BF16)
    return hi, mid, lo


def _rms_gate(o, norm_g, gate):
    o = o * lax.rsqrt(jnp.mean(o * o, axis=-1, keepdims=True) + RMS_EPS)
    return o * norm_g * _silu(gate)


def _hgrn_chunk(q, fb, v, lb, st, tril):
    c = q.shape[0]
    f = lb + (1.0 - lb) * _sigmoid(fb)
    kk = 1.0 - f
    g = jnp.log(f)
    b = sum(jnp.dot(tril, piece, preferred_element_type=F32) for piece in _split3(g))
    o = _dot_nt((q * jnp.exp(b)).astype(BF16), st.astype(BF16))

    s_idx = lax.broadcasted_iota(jnp.int32, (c, 1), 0)
    lane = lax.broadcasted_iota(jnp.int32, (HGRN_SUB, c), 1)
    row = lax.broadcasted_iota(jnp.int32, (HGRN_SUB, c), 0)
    att_rows = []
    for i0 in range(0, c, HGRN_SUB):
        qi = q[i0:i0 + HGRN_SUB]
        bi = b[i0:i0 + HGRN_SUB]
        if i0 > 0:
            bref = b[i0 - 1:i0]
            kt = jnp.where(s_idx < i0, kk * jnp.exp(jnp.minimum(bref - b, 0.0)), 0.0)
            att = _dot_nt((qi * jnp.exp(bi - bref)).astype(BF16), kt.astype(BF16))
        else:
            att = jnp.zeros((HGRN_SUB, c), F32)
        for j in range(HGRN_SUB):
            s = i0 + j
            e = jnp.exp(jnp.minimum(bi - b[s:s + 1], 0.0))
            colv = jnp.sum(qi * kk[s:s + 1] * e, axis=-1, keepdims=True)
            att = jnp.where((lane == s) & (row >= j), colv, att)
        att_rows.append(att)
    att = jnp.concatenate(att_rows, axis=0)
    vb = v.astype(BF16)
    o = o + jnp.dot(att.astype(BF16), vb, preferred_element_type=F32)

    b_last = b[c - 1:c]
    kd = (kk * jnp.exp(b_last - b)).astype(BF16)
    st_new = st * jnp.exp(b_last) + lax.dot_general(vb, kd, (((0,), (0,)), ((), ())),
                                                    preferred_element_type=F32)
    return o, st_new


def _tril_ones(c):
    return (lax.broadcasted_iota(jnp.int32, (c, c), 0) >= lax.broadcasted_iota(jnp.int32, (c, c), 1)
            ).astype(BF16)


def _hgrn_kernel(layer, q_ref, f_ref, i_ref, g_ref, lb_ref, ng_ref, z_ref, s_ref, st_ref):
    seq = q_ref.shape[0]
    lb = _lower_bound(lb_ref[...], layer)
    norm_g = ng_ref[...]
    tril = _tril_ones(HGRN_CHUNK)
    st_ref[...] = jnp.zeros_like(st_ref)

    def body(ci, carry):
        rows = pl.ds(pl.multiple_of(ci * HGRN_CHUNK, HGRN_CHUNK), HGRN_CHUNK)
        o, st = _hgrn_chunk(q_ref[rows, :], f_ref[rows, :], i_ref[rows, :], lb, st_ref[...], tril)
        st_ref[...] = st
        z_ref[rows, :] = _rms_gate(o, norm_g, g_ref[rows, :]).astype(z_ref.dtype)
        return carry

    lax.fori_loop(0, seq // HGRN_CHUNK, body, 0)
    s_ref[...] = st_ref[...].T


def _prompt_hgrn(proj, lb_raw, norm_g, layer, bsz, seq):
    def col(g0):
        return pl.BlockSpec((None, seq, HEAD_DIM), lambda b, h: (g0 + h, b, 0))

    return pl.pallas_call(
        functools.partial(_hgrn_kernel, layer),
        grid=(bsz, N_HEADS_HGRN),
        in_specs=[col(G_QB), col(G_FB), col(G_IB), col(G_GB),
                  pl.BlockSpec((DEPTH + 1, HEAD_DIM), lambda b, h: (0, h)),
                  pl.BlockSpec((1, HEAD_DIM), lambda b, h: (0, h))],
        out_specs=[pl.BlockSpec((None, seq, HEAD_DIM), lambda b, h: (h, b, 0)),
                   pl.BlockSpec((None, None, HEAD_DIM, HEAD_DIM), lambda b, h: (b, h, 0, 0))],
        out_shape=[jax.ShapeDtypeStruct((N_HEADS_HGRN, bsz * seq, HEAD_DIM), BF16),
                   jax.ShapeDtypeStruct((bsz, N_HEADS_HGRN, HEAD_DIM, HEAD_DIM), F32)],
        scratch_shapes=[pltpu.VMEM((HEAD_DIM, HEAD_DIM), F32)],
        compiler_params=_params(("parallel", "parallel")),
        name="prompt_hgrn",
    )(proj, proj, proj, proj, lb_raw, norm_g)


MEM_ROWS = 512


def _mem_kernel(q_ref, g_ref, mem_ref, w_ref, z_ref, mk_ref, mv_ref):
    seq = q_ref.shape[1]
    kv = jnp.dot(mem_ref[...].astype(BF16), w_ref[...], preferred_element_type=F32)
    for h in range(N_HEADS_MEM):
        mk = kv[:, h * HEAD_DIM:(h + 1) * HEAD_DIM]
        mv = kv[:, W_MEM + h * HEAD_DIM:W_MEM + (h + 1) * HEAD_DIM]
        mk_ref[:, h, :] = mk
        mv_ref[:, h, :] = mv
        mk, mv = mk.astype(BF16), mv.astype(BF16)
        for n in range(seq // MEM_ROWS):
            rows = pl.ds(n * MEM_ROWS, MEM_ROWS)
            s = _dot_nt((q_ref[h, rows, :] * ATTN_SCALE).astype(BF16), mk)
            p = jnp.exp(s - jnp.max(s, axis=-1, keepdims=True))
            den = jnp.sum(p, axis=-1, keepdims=True)
            o = jnp.dot(p.astype(BF16), mv, preferred_element_type=F32) / den
            z_ref[h, rows, :] = (o * _silu(g_ref[h, rows, :])).astype(z_ref.dtype)


def _prompt_mem(proj, mem, w_bf16, bsz, seq):
    assert G_QM % N_HEADS_MEM == 0 and G_GM % N_HEADS_MEM == 0

    def cols(g0):
        return pl.BlockSpec((N_HEADS_MEM, seq, HEAD_DIM), lambda b: (g0 // N_HEADS_MEM, b, 0))

    kv_out = pl.BlockSpec((None, N_MEM, N_HEADS_MEM, HEAD_DIM), lambda b: (b, 0, 0, 0))
    kv_shape = jax.ShapeDtypeStruct((bsz, N_MEM, N_HEADS_MEM, HEAD_DIM), F32)
    return pl.pallas_call(
        _mem_kernel,
        grid=(bsz,),
        in_specs=[cols(G_QM), cols(G_GM), pl.BlockSpec((N_MEM, D_MODEL), lambda b: (b, 0)),
                  pl.BlockSpec((D_MODEL, 2 * W_MEM), lambda b: (0, 0))],
        out_specs=[pl.BlockSpec((N_HEADS_MEM, seq, HEAD_DIM), lambda b: (0, b, 0)), kv_out, kv_out],
        out_shape=[jax.ShapeDtypeStruct((N_HEADS_MEM, bsz * seq, HEAD_DIM), BF16), kv_shape, kv_shape],
        compiler_params=_params(("parallel",)),
        name="prompt_mem",
    )(proj, proj, mem, w_bf16)


def _merge_kernel(za_ref, zh_ref, zm_ref, x_ref, w_ref, lg_ref, lb_ref, o_ref, z_ref):
    c0 = 0
    for ref in (za_ref, zh_ref, zm_ref):
        for c in range(ref.shape[0]):
            z_ref[:, (c0 + c) * LANES:(c0 + c + 1) * LANES] = ref[c].astype(BF16)
        c0 += ref.shape[0]
    y = jnp.dot(z_ref[...], w_ref[...], preferred_element_type=F32)
    r = DEEPNORM_ALPHA * x_ref[...] + y
    mu = jnp.mean(r, axis=-1, keepdims=True)
    d = r - mu
    var = jnp.mean(d * d, axis=-1, keepdims=True)
    o_ref[...] = d * lax.rsqrt(var + LN_EPS) * lg_ref[...] + lb_ref[...]


def _merge(za, zh, zm, x, w_out_bf16, ln_g, ln_b, tm):
    m = x.shape[0]

    def slab(a):
        return pl.BlockSpec((a.shape[0], tm, LANES), lambda i: (0, i, 0))

    const = lambda shape: pl.BlockSpec(shape, lambda i: (0, 0))
    return pl.pallas_call(
        _merge_kernel,
        grid=(m // tm,),
        in_specs=[slab(za), slab(zh), slab(zm), pl.BlockSpec((tm, D_MODEL), lambda i: (i, 0)),
                  const((MIX_WIDTH, D_MODEL)), const((1, D_MODEL)), const((1, D_MODEL))],
        out_specs=pl.BlockSpec((tm, D_MODEL), lambda i: (i, 0)),
        out_shape=jax.ShapeDtypeStruct((m, D_MODEL), F32),
        scratch_shapes=[pltpu.VMEM((tm, MIX_WIDTH), BF16)],
        compiler_params=_params(("parallel",)),
        name="merge",
    )(za, zh, zm, x, w_out_bf16, ln_g, ln_b)


def _column(row):
    return jnp.broadcast_to(row, (HEAD_DIM, HEAD_DIM)).T


def _sample_kernel(layer, p_ref, cos_ref, sin_ref, k1_ref, k4_ref, k16_ref, v1_ref, v4_ref, v16_ref,
                   st_ref, mk_ref, mv_ref, lb_ref, ng_ref, z_ref, ko_ref, vo_ref, so_ref):
    cos = cos_ref[...]
    sin = sin_ref[...]
    q_all = _rope(p_ref[G_QA:G_QA + N_HEADS_ATTN, :], cos, sin) * ATTN_SCALE
    k_all = _rope(p_ref[G_KA:G_KA + N_HEADS_ATTN, :], cos, sin)
    v_all = p_ref[G_VA:G_VA + N_HEADS_ATTN, :]
    ko_ref[...] = k_all
    vo_ref[...] = v_all
    caches = ((k1_ref, v1_ref), (k4_ref, v4_ref), (k16_ref, v16_ref))

    s_new = jnp.sum(q_all * k_all, axis=-1, keepdims=True)
    scores = [jnp.sum(kr[...] * q_all[None], axis=-1, keepdims=True) for kr, _ in caches]
    m = s_new
    for s in scores:
        m = jnp.maximum(m, jnp.max(s, axis=0))
    p_new = jnp.exp(s_new - m) * len(caches)
    num = p_new * v_all
    den = p_new
    for s, (_, vr) in zip(scores, caches):
        p = jnp.exp(s - m[None])
        den = den + jnp.sum(p, axis=0)
        num = num + jnp.sum(p * vr[...], axis=0)
    z_ref[0:N_HEADS_ATTN, :] = (num / den) * _silu(p_ref[G_GA:G_GA + N_HEADS_ATTN, :])

    lb_all = lb_ref[...]
    for h in range(N_HEADS_HGRN):
        lanes = slice(h * HEAD_DIM, (h + 1) * HEAD_DIM)
        lb = _lower_bound(lb_all[:, lanes], layer)
        f = lb + (1.0 - lb) * _sigmoid(p_ref[G_FB + h:G_FB + h + 1, :])
        f_col = _column(f)
        q_col = _column(p_ref[G_QB + h:G_QB + h + 1, :])
        s_new = f_col * st_ref[h] + (1.0 - f_col) * p_ref[G_IB + h:G_IB + h + 1, :]
        so_ref[h] = s_new
        o = jnp.sum(s_new * q_col, axis=0, keepdims=True)
        z_ref[N_HEADS_ATTN + h:N_HEADS_ATTN + h + 1, :] = _rms_gate(
            o, ng_ref[:, lanes], p_ref[G_GB + h:G_GB + h + 1, :])

    q_m = p_ref[G_QM:G_QM + N_HEADS_MEM, :] * ATTN_SCALE
    s = jnp.sum(mk_ref[...] * q_m[None], axis=-1, keepdims=True)
    p = jnp.exp(s - jnp.max(s, axis=0)[None])
    o = jnp.sum(p * mv_ref[...], axis=0) / jnp.sum(p, axis=0)
    row = N_HEADS_ATTN + N_HEADS_HGRN
    z_ref[row:row + N_HEADS_MEM, :] = o * _silu(p_ref[G_GM:G_GM + N_HEADS_MEM, :])


def _sample_mixers(proj_rows, cos, sin, win_k, win_v, state, mem_k, mem_v, lb_raw, norm_g, layer):
    bsz, past = win_k.shape[1:3]
    assert all(past % d == 0 and past >= w for w, d in zip(WINDOWS, DILATIONS))
    assert all(w // d == BAND for w, d in zip(WINDOWS, DILATIONS))

    def strided_rows(a, dil):
        view = a.reshape(a.shape[0], bsz, past // dil, dil, N_HEADS_ATTN, HEAD_DIM)
        return view, pl.BlockSpec((None, None, BAND, None, N_HEADS_ATTN, HEAD_DIM),
                                  lambda b: (layer, b, past // dil // BAND - 1, 0, 0, 0))

    views, specs = zip(*[strided_rows(a, d) for a in (win_k, win_v) for d in DILATIONS])
    vec = lambda n: pl.BlockSpec((n, W_HGRN), lambda b: (0, 0))
    heads = lambda n: pl.BlockSpec((None, n, HEAD_DIM), lambda b: (b, 0, 0))
    mem_spec = pl.BlockSpec((None, None, N_MEM, N_HEADS_MEM, HEAD_DIM), lambda b: (layer, b, 0, 0, 0))
    return pl.pallas_call(
        functools.partial(_sample_kernel, layer),
        grid=(bsz,),
        in_specs=[heads(N_GROUPS), pl.BlockSpec((1, HEAD_DIM), lambda b: (0, 0)),
                  pl.BlockSpec((1, HEAD_DIM), lambda b: (0, 0)), *specs,
                  pl.BlockSpec((None, None, N_HEADS_HGRN, HEAD_DIM, HEAD_DIM), lambda b: (layer, b, 0, 0, 0)),
                  mem_spec, mem_spec, vec(DEPTH + 1), vec(1)],
        out_specs=[heads(MIX_WIDTH // HEAD_DIM), heads(N_HEADS_ATTN), heads(N_HEADS_ATTN),
                   pl.BlockSpec((None, N_HEADS_HGRN, HEAD_DIM, HEAD_DIM), lambda b: (b, 0, 0, 0))],
        out_shape=[jax.ShapeDtypeStruct((bsz, MIX_WIDTH // HEAD_DIM, HEAD_DIM), F32),
                   jax.ShapeDtypeStruct((bsz, N_HEADS_ATTN, HEAD_DIM), F32),
                   jax.ShapeDtypeStruct((bsz, N_HEADS_ATTN, HEAD_DIM), F32),
                   jax.ShapeDtypeStruct((bsz, N_HEADS_HGRN, HEAD_DIM, HEAD_DIM), F32)],
        compiler_params=_params(("parallel",)),
        name="sample_mixers",
    )(proj_rows, cos, sin, *views, state, mem_k, mem_v, lb_raw, norm_g)


def _rope_tables(pos):
    half = HEAD_DIM // 2
    inv_freq = 1.0 / (ROPE_THETA ** (jnp.arange(half, dtype=F32) / half))
    ang = pos.astype(F32)[:, None] * inv_freq[None, :]
    cos, sin = jnp.cos(ang), jnp.sin(ang)
    return jnp.concatenate([cos, cos], axis=-1), jnp.concatenate([-sin, sin], axis=-1)


def kernel(x_prompt, x_sample, cache_win_k, cache_win_v, state_hgrn, cache_mem_k, cache_mem_v, mem_prompt,
           w_in, w_mem_kv, hgrn_lb_raw, hgrn_norm_g, w_out, ln_g, ln_b):
    bsz, seq, _ = x_prompt.shape
    dbsz, n_new, _ = x_sample.shape
    assert n_new == 1 and seq % (BAND * max(DILATIONS)) == 0
    cos_p, sin_p = _rope_tables(jnp.arange(seq))
    cos_s, sin_s = _rope_tables(PAST_LEN + jnp.arange(n_new))

    hp = x_prompt.reshape(bsz * seq, D_MODEL)
    hs = x_sample.reshape(dbsz * n_new, D_MODEL)
    mem = mem_prompt.reshape(bsz * N_MEM, D_MODEL)
    outs = [[] for _ in range(8)]
    for layer in range(DEPTH):
        w_in_b = w_in[layer].astype(BF16)
        w_out_b = w_out[layer].astype(BF16)
        lb_raw = hgrn_lb_raw
        norm_g = hgrn_norm_g[layer][None]
        lg, lbias = ln_g[layer][None], ln_b[layer][None]

        proj = _project(hp, w_in_b, 1024, 1024)
        za, k1, v1 = _prompt_attention(proj, cos_p, sin_p, bsz, seq)
        zh, s1 = _prompt_hgrn(proj, lb_raw, norm_g, layer, bsz, seq)
        zm, mk1, mv1 = _prompt_mem(proj, mem, w_mem_kv[layer].astype(BF16), bsz, seq)
        hp = _merge(za, zh, zm, hp, w_out_b, lg, lbias, 256)

        proj_s = _project(hs, w_in_b, dbsz, 1024).transpose(1, 0, 2)
        zs, k2, v2, s2 = _sample_mixers(proj_s, cos_s, sin_s, cache_win_k, cache_win_v, state_hgrn,
                                        cache_mem_k, cache_mem_v, lb_raw, norm_g, layer)
        zs = zs.transpose(1, 0, 2)
        hs = _merge(zs[:N_HEADS_ATTN], zs[N_HEADS_ATTN:N_HEADS_ATTN + N_HEADS_HGRN],
                    zs[N_HEADS_ATTN + N_HEADS_HGRN:], hs, w_out_b, lg, lbias, dbsz)

        new = (k1, v1, s1, mk1, mv1,
               k2.reshape(dbsz, n_new, N_HEADS_ATTN, HEAD_DIM), v2.reshape(dbsz, n_new, N_HEADS_ATTN, HEAD_DIM),
               s2.astype(state_hgrn.dtype))
        for acc, val in zip(outs, new):
            acc.append(val)

    return (hp.reshape(bsz, seq, D_MODEL), hs.reshape(dbsz, n_new, D_MODEL), *[jnp.stack(o) for o in outs])
```

```python
import functools

import jax
import jax.numpy as jnp
import numpy as np
from jax import lax
from jax.experimental import pallas as pl
from jax.experimental.pallas import tpu as pltpu

F32 = jnp.float32
BF16 = jnp.bfloat16

D_MODEL = 2048
DEPTH = 1
PAST_LEN = 8192
HEAD_DIM = 128
N_HEADS_ATTN = 6
N_HEADS_HGRN = 6
N_HEADS_MEM = 4
W_ATTN = N_HEADS_ATTN * HEAD_DIM
W_HGRN = N_HEADS_HGRN * HEAD_DIM
W_MEM = N_HEADS_MEM * HEAD_DIM
MIX_WIDTH = W_ATTN + W_HGRN + W_MEM
WINDOWS = (128, 512, 2048)
DILATIONS = (1, 4, 16)
N_MEM = 256
ROPE_THETA = 10000.0
LN_EPS = 1e-5
RMS_EPS = 1e-6
NEG_INF = -1e30
DEEPNORM_ALPHA = (2 * DEPTH) ** 0.25
ATTN_SCALE = HEAD_DIM ** -0.5

G_QA, G_KA, G_VA, G_GA = 0, 6, 12, 18
G_QB, G_FB, G_IB, G_GB = 24, 30, 36, 42
G_QM, G_GM = 48, 52
N_GROUPS = 56

LANES = 128
BAND = 128
HGRN_CHUNK = 64
HGRN_SUB = 16
VMEM_LIMIT = 48 * 1024 * 1024


def _params(sem, vmem=VMEM_LIMIT):
    return pltpu.CompilerParams(dimension_semantics=sem, vmem_limit_bytes=vmem)


def _sigmoid(x):
    return 1.0 / (1.0 + jnp.exp(-x))


def _silu(x):
    return x * _sigmoid(x)


def _dot_nt(a, b):
    return lax.dot_general(a, b, (((1,), (1,)), ((), ())), preferred_element_type=F32)


def _proj_kernel(x_ref, w_ref, o_ref, xb_ref):
    @pl.when(pl.program_id(1) == 0)
    def _():
        xb_ref[...] = x_ref[...].astype(BF16)

    acc = jnp.dot(xb_ref[...], w_ref[...], preferred_element_type=F32)
    for j in range(o_ref.shape[0]):
        o_ref[j] = acc[:, j * LANES:(j + 1) * LANES]


def _project(x, w_bf16, tm, tn):
    m, k = x.shape
    n = w_bf16.shape[1]
    return pl.pallas_call(
        _proj_kernel,
        grid=(m // tm, n // tn),
        in_specs=[pl.BlockSpec((tm, k), lambda i, j: (i, 0)),
                  pl.BlockSpec((k, tn), lambda i, j: (0, j))],
        out_specs=pl.BlockSpec((tn // LANES, tm, LANES), lambda i, j: (j, i, 0)),
        out_shape=jax.ShapeDtypeStruct((n // LANES, m, LANES), F32),
        scratch_shapes=[pltpu.VMEM((tm, k), BF16)],
        compiler_params=_params(("parallel", "arbitrary")),
        name="in_proj",
    )(x, w_bf16)


def _rope(x, cos, sin_signed):
    return x * cos + pltpu.roll(x, HEAD_DIM // 2, 1) * sin_signed


def _band_bias():
    qi = lax.broadcasted_iota(jnp.int32, (BAND, BAND), 0)
    ki = lax.broadcasted_iota(jnp.int32, (BAND, BAND), 1)
    prev = jnp.where(ki >= qi, 0.0, NEG_INF).astype(F32)
    cur = jnp.where(ki <= qi, 0.0, NEG_INF).astype(F32)
    return prev, cur


def _softmax_part(qb, kb, vb, bias):
    s = _dot_nt(qb.astype(BF16), kb.astype(BF16)) + bias
    m = jnp.max(s, axis=-1, keepdims=True)
    p = jnp.exp(s - m)
    l = jnp.sum(p, axis=-1, keepdims=True)
    acc = jnp.dot(p.astype(BF16), vb.astype(BF16), preferred_element_type=F32)
    return acc, m, l


def _attn_kernel(q_ref, k_ref, v_ref, g_ref, cos_ref, sin_ref, z_ref, ko_ref, vo_ref,
                 qs_ref, acc_ref, m_ref, l_ref):
    seq = q_ref.shape[0]
    n_blk = seq // BAND
    cos = cos_ref[...]
    sin = sin_ref[...]
    qs_ref[...] = _rope(q_ref[...], cos, sin) * ATTN_SCALE
    ko_ref[...] = _rope(k_ref[...], cos, sin)
    vo_ref[...] = v_ref[...]
    bias_prev, bias_cur = _band_bias()
    bias_both = jnp.concatenate([bias_prev, bias_cur], axis=1)

    for br, dil in enumerate(DILATIONS):
        n_cls_blk = n_blk // dil
        for r in range(dil):
            for n in range(n_cls_blk):
                base = r + dil * BAND * n
                rows = pl.ds(base, BAND, stride=dil) if dil > 1 else pl.ds(base, BAND)
                if n == 0:
                    keys, bias = rows, bias_cur
                else:
                    kbase = base - dil * BAND
                    keys = pl.ds(kbase, 2 * BAND, stride=dil) if dil > 1 else pl.ds(kbase, 2 * BAND)
                    bias = bias_both
                acc, m, l = _softmax_part(qs_ref[rows, :], ko_ref[keys, :], v_ref[keys, :], bias)
                acc_ref[br, rows, :] = acc
                m_ref[br, rows, :] = m
                l_ref[br, rows, :] = l

    for n in range(n_blk):
        rows = pl.ds(n * BAND, BAND)
        ms = [m_ref[br, rows, :] for br in range(len(DILATIONS))]
        m_all = functools.reduce(jnp.maximum, ms)
        num = jnp.zeros((BAND, HEAD_DIM), F32)
        den = jnp.zeros((BAND, 1), F32)
        for br in range(len(DILATIONS)):
            w = jnp.exp(ms[br] - m_all)
            num = num + w * acc_ref[br, rows, :]
            den = den + w * l_ref[br, rows, :]
        z_ref[rows, :] = ((num / den) * _silu(g_ref[rows, :])).astype(z_ref.dtype)


def _prompt_attention(proj, cos, sin, bsz, seq):
    def col(g0):
        return pl.BlockSpec((None, seq, HEAD_DIM), lambda b, h: (g0 + h, b, 0))

    table = pl.BlockSpec((seq, HEAD_DIM), lambda b, h: (0, 0))
    kv_out = pl.BlockSpec((None, None, seq, HEAD_DIM), lambda b, h: (b, h, 0, 0))
    n_br = len(DILATIONS)
    return pl.pallas_call(
        _attn_kernel,
        grid=(bsz, N_HEADS_ATTN),
        in_specs=[col(G_QA), col(G_KA), col(G_VA), col(G_GA), table, table],
        out_specs=[pl.BlockSpec((None, seq, HEAD_DIM), lambda b, h: (h, b, 0)), kv_out, kv_out],
        out_shape=[jax.ShapeDtypeStruct((N_HEADS_ATTN, bsz * seq, HEAD_DIM), BF16),
                   jax.ShapeDtypeStruct((bsz, N_HEADS_ATTN, seq, HEAD_DIM), F32),
                   jax.ShapeDtypeStruct((bsz, N_HEADS_ATTN, seq, HEAD_DIM), F32)],
        scratch_shapes=[pltpu.VMEM((seq, HEAD_DIM), F32),
                        pltpu.VMEM((n_br, seq, HEAD_DIM), F32),
                        pltpu.VMEM((n_br, seq, 1), F32),
                        pltpu.VMEM((n_br, seq, 1), F32)],
        compiler_params=_params(("parallel", "parallel")),
        name="prompt_attn",
    )(proj, proj, proj, proj, cos, sin)


def _lower_bound(lb_raw, layer):
    e = jnp.exp(lb_raw - jnp.max(lb_raw, axis=0, keepdims=True))
    sm = e / jnp.sum(e, axis=0, keepdims=True)
    return jnp.sum(sm[:layer + 1], axis=0, keepdims=True)


def _split3(x):
    hi = x.astype(BF16)
    r1 = x - hi.astype(F32)
    mid = r1.astype(BF16)
    lo = (r1 - mid.astype(F32)).astype(BF16)
    return hi, mid, lo


def _rms_gate(o, norm_g, gate):
    o = o * lax.rsqrt(jnp.mean(o * o, axis=-1, keepdims=True) + RMS_EPS)
    return o * norm_g * _silu(gate)


def _hgrn_chunk(q, fb, v, lb, st, tril):
    c = q.shape[0]
    f = lb + (1.0 - lb) * _sigmoid(fb)
    kk = 1.0 - f
    g = jnp.log(f)
    b = sum(jnp.dot(tril, piece, preferred_element_type=F32) for piece in _split3(g))
    o = _dot_nt((q * jnp.exp(b)).astype(BF16), st.astype(BF16))

    s_idx = lax.broadcasted_iota(jnp.int32, (c, 1), 0)
    lane = lax.broadcasted_iota(jnp.int32, (HGRN_SUB, c), 1)
    row = lax.broadcasted_iota(jnp.int32, (HGRN_SUB, c), 0)
    att_rows = []
    for i0 in range(0, c, HGRN_SUB):
        qi = q[i0:i0 + HGRN_SUB]
        bi = b[i0:i0 + HGRN_SUB]
        if i0 > 0:
            bref = b[i0 - 1:i0]
            kt = jnp.where(s_idx < i0, kk * jnp.exp(jnp.minimum(bref - b, 0.0)), 0.0)
            att = _dot_nt((qi * jnp.exp(bi - bref)).astype(BF16), kt.astype(BF16))
        else:
            att = jnp.zeros((HGRN_SUB, c), F32)
        for j in range(HGRN_SUB):
            s = i0 + j
            e = jnp.exp(jnp.minimum(bi - b[s:s + 1], 0.0))
            colv = jnp.sum(qi * kk[s:s + 1] * e, axis=-1, keepdims=True)
            att = jnp.where((lane == s) & (row >= j), colv, att)
        att_rows.append(att)
    att = jnp.concatenate(att_rows, axis=0)
    vb = v.astype(BF16)
    o = o + jnp.dot(att.astype(BF16), vb, preferred_element_type=F32)

    b_last = b[c - 1:c]
    kd = (kk * jnp.exp(b_last - b)).astype(BF16)
    st_new = st * jnp.exp(b_last) + lax.dot_general(vb, kd, (((0,), (0,)), ((), ())),
                                                    preferred_element_type=F32)
    return o, st_new


def _tril_ones(c):
    return (lax.broadcasted_iota(jnp.int32, (c, c), 0) >= lax.broadcasted_iota(jnp.int32, (c, c), 1)
            ).astype(BF16)


def _hgrn_kernel(layer, q_ref, f_ref, i_ref, g_ref, lb_ref, ng_ref, z_ref, s_ref, st_ref):
    seq = q_ref.shape[0]
    lb = _lower_bound(lb_ref[...], layer)
    norm_g = ng_ref[...]
    tril = _tril_ones(HGRN_CHUNK)
    st_ref[...] = jnp.zeros_like(st_ref)

    def body(ci, carry):
        rows = pl.ds(pl.multiple_of(ci * HGRN_CHUNK, HGRN_CHUNK), HGRN_CHUNK)
        o, st = _hgrn_chunk(q_ref[rows, :], f_ref[rows, :], i_ref[rows, :], lb, st_ref[...], tril)
        st_ref[...] = st
        z_ref[rows, :] = _rms_gate(o, norm_g, g_ref[rows, :]).astype(z_ref.dtype)
        return carry

    lax.fori_loop(0, seq // HGRN_CHUNK, body, 0)
    s_ref[...] = st_ref[...].T


def _prompt_hgrn(proj, lb_raw, norm_g, layer, bsz, seq):
    def col(g0):
        return pl.BlockSpec((None, seq, HEAD_DIM), lambda b, h: (g0 + h, b, 0))

    return pl.pallas_call(
        functools.partial(_hgrn_kernel, layer),
        grid=(bsz, N_HEADS_HGRN),
        in_specs=[col(G_QB), col(G_FB), col(G_IB), col(G_GB),
                  pl.BlockSpec((DEPTH + 1, HEAD_DIM), lambda b, h: (0, h)),
                  pl.BlockSpec((1, HEAD_DIM), lambda b, h: (0, h))],
        out_specs=[pl.BlockSpec((None, seq, HEAD_DIM), lambda b, h: (h, b, 0)),
                   pl.BlockSpec((None, None, HEAD_DIM, HEAD_DIM), lambda b, h: (b, h, 0, 0))],
        out_shape=[jax.ShapeDtypeStruct((N_HEADS_HGRN, bsz * seq, HEAD_DIM), BF16),
                   jax.ShapeDtypeStruct((bsz, N_HEADS_HGRN, HEAD_DIM, HEAD_DIM), F32)],
        scratch_shapes=[pltpu.VMEM((HEAD_DIM, HEAD_DIM), F32)],
        compiler_params=_params(("parallel", "parallel")),
        name="prompt_hgrn",
    )(proj, proj, proj, proj, lb_raw, norm_g)


MEM_ROWS = 512


def _mem_kernel(q_ref, g_ref, mem_ref, w_ref, z_ref, mk_ref, mv_ref):
    seq = q_ref.shape[1]
    kv = jnp.dot(mem_ref[...].astype(BF16), w_ref[...], preferred_element_type=F32)
    for h in range(N_HEADS_MEM):
        mk = kv[:, h * HEAD_DIM:(h + 1) * HEAD_DIM]
        mv = kv[:, W_MEM + h * HEAD_DIM:W_MEM + (h + 1) * HEAD_DIM]
        mk_ref[:, h, :] = mk
        mv_ref[:, h, :] = mv
        mk, mv = mk.astype(BF16), mv.astype(BF16)
        for n in range(seq // MEM_ROWS):
            rows = pl.ds(n * MEM_ROWS, MEM_ROWS)
            s = _dot_nt((q_ref[h, rows, :] * ATTN_SCALE).astype(BF16), mk)
            p = jnp.exp(s - jnp.max(s, axis=-1, keepdims=True))
            den = jnp.sum(p, axis=-1, keepdims=True)
            o = jnp.dot(p.astype(BF16), mv, preferred_element_type=F32) / den
            z_ref[h, rows, :] = (o * _silu(g_ref[h, rows, :])).astype(z_ref.dtype)


def _prompt_mem(proj, mem, w_bf16, bsz, seq):
    assert G_QM % N_HEADS_MEM == 0 and G_GM % N_HEADS_MEM == 0

    def cols(g0):
        return pl.BlockSpec((N_HEADS_MEM, seq, HEAD_DIM), lambda b: (g0 // N_HEADS_MEM, b, 0))

    kv_out = pl.BlockSpec((None, N_MEM, N_HEADS_MEM, HEAD_DIM), lambda b: (b, 0, 0, 0))
    kv_shape = jax.ShapeDtypeStruct((bsz, N_MEM, N_HEADS_MEM, HEAD_DIM), F32)
    return pl.pallas_call(
        _mem_kernel,
        grid=(bsz,),
        in_specs=[cols(G_QM), cols(G_GM), pl.BlockSpec((N_MEM, D_MODEL), lambda b: (b, 0)),
                  pl.BlockSpec((D_MODEL, 2 * W_MEM), lambda b: (0, 0))],
        out_specs=[pl.BlockSpec((N_HEADS_MEM, seq, HEAD_DIM), lambda b: (0, b, 0)), kv_out, kv_out],
        out_shape=[jax.ShapeDtypeStruct((N_HEADS_MEM, bsz * seq, HEAD_DIM), BF16), kv_shape, kv_shape],
        compiler_params=_params(("parallel",)),
        name="prompt_mem",
    )(proj, proj, mem, w_bf16)


def _merge_kernel(za_ref, zh_ref, zm_ref, x_ref, w_ref, lg_ref, lb_ref, o_ref, z_ref):
    c0 = 0
    for ref in (za_ref, zh_ref, zm_ref):
        for c in range(ref.shape[0]):
            z_ref[:, (c0 + c) * LANES:(c0 + c + 1) * LANES] = ref[c].astype(BF16)
        c0 += ref.shape[0]
    y = jnp.dot(z_ref[...], w_ref[...], preferred_element_type=F32)
    r = DEEPNORM_ALPHA * x_ref[...] + y
    mu = jnp.mean(r, axis=-1, keepdims=True)
    d = r - mu
    var = jnp.mean(d * d, axis=-1, keepdims=True)
    o_ref[...] = d * lax.rsqrt(var + LN_EPS) * lg_ref[...] + lb_ref[...]


def _merge(za, zh, zm, x, w_out_bf16, ln_g, ln_b, tm):
    m = x.shape[0]

    def slab(a):
        return pl.BlockSpec((a.shape[0], tm, LANES), lambda i: (0, i, 0))

    const = lambda shape: pl.BlockSpec(shape, lambda i: (0, 0))
    return pl.pallas_call(
        _merge_kernel,
        grid=(m // tm,),
        in_specs=[slab(za), slab(zh), slab(zm), pl.BlockSpec((tm, D_MODEL), lambda i: (i, 0)),
                  const((MIX_WIDTH, D_MODEL)), const((1, D_MODEL)), const((1, D_MODEL))],
        out_specs=pl.BlockSpec((tm, D_MODEL), lambda i: (i, 0)),
        out_shape=jax.ShapeDtypeStruct((m, D_MODEL), F32),
        scratch_shapes=[pltpu.VMEM((tm, MIX_WIDTH), BF16)],
        compiler_params=_params(("parallel",)),
        name="merge",
    )(za, zh, zm, x, w_out_bf16, ln_g, ln_b)


def _column(row):
    return jnp.broadcast_to(row, (HEAD_DIM, HEAD_DIM)).T


HBM_TILE_ROWS = 8


def _window_pieces(past):
    pieces = []
    for dil in DILATIONS:
        if dil == 1:
            pieces.append((0, past - BAND, BAND, None))
        elif dil < HBM_TILE_ROWS:
            n = BAND * dil // HBM_TILE_ROWS
            pieces += [(1, past // HBM_TILE_ROWS - n, n, r) for r in range(0, HBM_TILE_ROWS, dil)]
        else:
            pieces.append((2, past // dil - BAND, BAND, 0))
    return pieces


def _cache_views(cache):
    depth, bsz, past, nh, hd = cache.shape
    assert all(d == 1 or HBM_TILE_ROWS % d == 0 or d % HBM_TILE_ROWS == 0 for d in DILATIONS)
    big = max(DILATIONS)
    hm = cache.transpose(0, 1, 3, 2, 4)
    return (hm, hm.reshape(depth, bsz, nh, past // HBM_TILE_ROWS, HBM_TILE_ROWS, hd),
            hm.reshape(depth, bsz, nh, past // big, big, hd))


def _sample_kernel(layer, past, p_ref, cos_ref, sin_ref, k0_hbm, k1_hbm, k2_hbm, v0_hbm, v1_hbm, v2_hbm,
                   st_ref, mk_ref, mv_ref, lb_ref, ng_ref, z_ref, ko_ref, vo_ref, so_ref, kbuf, vbuf, sem):
    b = pl.program_id(0)
    slot = b % 2
    pieces = _window_pieces(past)

    def window_copies(row, sl):
        out = []
        for ci, (views, buf) in enumerate((((k0_hbm, k1_hbm, k2_hbm), kbuf), ((v0_hbm, v1_hbm, v2_hbm), vbuf))):
            off = 0
            for pi, (vi, start, count, res) in enumerate(pieces):
                view = views[vi]
                src = (view.at[layer, row, :, pl.ds(start, count), :] if res is None
                       else view.at[layer, row, :, pl.ds(start, count), res, :])
                out.append(pltpu.make_async_copy(src, buf.at[sl, :, pl.ds(off, count), :], sem.at[sl, ci, pi]))
                off += count
        return out

    @pl.when(b == 0)
    def _():
        for c in window_copies(b, slot):
            c.start()

    @pl.when(b + 1 < pl.num_programs(0))
    def _():
        for c in window_copies(b + 1, 1 - slot):
            c.start()

    cos = cos_ref[...]
    sin = sin_ref[...]
    q_all = _rope(p_ref[G_QA:G_QA + N_HEADS_ATTN, :], cos, sin) * ATTN_SCALE
    k_all = _rope(p_ref[G_KA:G_KA + N_HEADS_ATTN, :], cos, sin)
    v_all = p_ref[G_VA:G_VA + N_HEADS_ATTN, :]
    ko_ref[...] = k_all
    vo_ref[...] = v_all

    for c in window_copies(b, slot):
        c.wait()

    for h in range(N_HEADS_ATTN):
        q = q_all[h:h + 1]
        s_new = jnp.sum(q * k_all[h:h + 1], axis=-1, keepdims=True)
        s = jnp.sum(kbuf[slot, h] * q, axis=-1, keepdims=True)
        m = jnp.maximum(jnp.max(s, axis=0, keepdims=True), s_new)
        p = jnp.exp(s - m)
        p_new = jnp.exp(s_new - m) * len(DILATIONS)
        den = jnp.sum(p, axis=0, keepdims=True) + p_new
        num = jnp.sum(p * vbuf[slot, h], axis=0, keepdims=True) + p_new * v_all[h:h + 1]
        z_ref[h:h + 1, :] = (num / den) * _silu(p_ref[G_GA + h:G_GA + h + 1, :])

    lb_all = lb_ref[...]
    for h in range(N_HEADS_HGRN):
        lanes = slice(h * HEAD_DIM, (h + 1) * HEAD_DIM)
        lb = _lower_bound(lb_all[:, lanes], layer)
        f = lb + (1.0 - lb) * _sigmoid(p_ref[G_FB + h:G_FB + h + 1, :])
        f_col = _column(f)
        q_col = _column(p_ref[G_QB + h:G_QB + h + 1, :])
        s_new = f_col * st_ref[h] + (1.0 - f_col) * p_ref[G_IB + h:G_IB + h + 1, :]
        so_ref[h] = s_new
        o = jnp.sum(s_new * q_col, axis=0, keepdims=True)
        z_ref[N_HEADS_ATTN + h:N_HEADS_ATTN + h + 1, :] = _rms_gate(
            o, ng_ref[:, lanes], p_ref[G_GB + h:G_GB + h + 1, :])

    q_m = p_ref[G_QM:G_QM + N_HEADS_MEM, :] * ATTN_SCALE
    s = jnp.sum(mk_ref[...] * q_m[None], axis=-1, keepdims=True)
    p = jnp.exp(s - jnp.max(s, axis=0)[None])
    o = jnp.sum(p * mv_ref[...], axis=0) / jnp.sum(p, axis=0)
    row = N_HEADS_ATTN + N_HEADS_HGRN
    z_ref[row:row + N_HEADS_MEM, :] = o * _silu(p_ref[G_GM:G_GM + N_HEADS_MEM, :])


def _sample_mixers(proj_rows, cos, sin, win_k, win_v, state, mem_k, mem_v, lb_raw, norm_g, layer):
    bsz, past = win_k.shape[1:3]
    assert all(past % d == 0 and past >= w for w, d in zip(WINDOWS, DILATIONS))
    assert all(w // d == BAND for w, d in zip(WINDOWS, DILATIONS)) and past % HBM_TILE_ROWS == 0
    n_keys = BAND * len(DILATIONS)
    n_pieces = len(_window_pieces(past))
    any_spec = pl.BlockSpec(memory_space=pl.ANY)
    vec = lambda n: pl.BlockSpec((n, W_HGRN), lambda b: (0, 0))
    heads = lambda n: pl.BlockSpec((None, n, HEAD_DIM), lambda b: (b, 0, 0))
    mem_spec = pl.BlockSpec((None, None, N_MEM, N_HEADS_MEM, HEAD_DIM), lambda b: (layer, b, 0, 0, 0))
    return pl.pallas_call(
        functools.partial(_sample_kernel, layer, past),
        grid=(bsz,),
        in_specs=[heads(N_GROUPS), pl.BlockSpec((1, HEAD_DIM), lambda b: (0, 0)),
                  pl.BlockSpec((1, HEAD_DIM), lambda b: (0, 0)), *([any_spec] * 6),
                  pl.BlockSpec((None, None, N_HEADS_HGRN, HEAD_DIM, HEAD_DIM), lambda b: (layer, b, 0, 0, 0)),
                  mem_spec, mem_spec, vec(DEPTH + 1), vec(1)],
        out_specs=[heads(MIX_WIDTH // HEAD_DIM), heads(N_HEADS_ATTN), heads(N_HEADS_ATTN),
                   pl.BlockSpec((None, N_HEADS_HGRN, HEAD_DIM, HEAD_DIM), lambda b: (b, 0, 0, 0))],
        out_shape=[jax.ShapeDtypeStruct((bsz, MIX_WIDTH // HEAD_DIM, HEAD_DIM), F32),
                   jax.ShapeDtypeStruct((bsz, N_HEADS_ATTN, HEAD_DIM), F32),
                   jax.ShapeDtypeStruct((bsz, N_HEADS_ATTN, HEAD_DIM), F32),
                   jax.ShapeDtypeStruct((bsz, N_HEADS_HGRN, HEAD_DIM, HEAD_DIM), F32)],
        scratch_shapes=[pltpu.VMEM((2, N_HEADS_ATTN, n_keys, HEAD_DIM), F32),
                        pltpu.VMEM((2, N_HEADS_ATTN, n_keys, HEAD_DIM), F32),
                        pltpu.SemaphoreType.DMA((2, 2, n_pieces))],
        compiler_params=_params(("arbitrary",)),
        name="sample_mixers",
    )(proj_rows, cos, sin, *_cache_views(win_k), *_cache_views(win_v), state, mem_k, mem_v, lb_raw, norm_g)


def _rope_tables(pos):
    half = HEAD_DIM // 2
    inv_freq = 1.0 / (ROPE_THETA ** (jnp.arange(half, dtype=F32) / half))
    ang = pos.astype(F32)[:, None] * inv_freq[None, :]
    cos, sin = jnp.cos(ang), jnp.sin(ang)
    return jnp.concatenate([cos, cos], axis=-1), jnp.concatenate([-sin, sin], axis=-1)


def kernel(x_prompt, x_sample, cache_win_k, cache_win_v, state_hgrn, cache_mem_k, cache_mem_v, mem_prompt,
           w_in, w_mem_kv, hgrn_lb_raw, hgrn_norm_g, w_out, ln_g, ln_b):
    bsz, seq, _ = x_prompt.shape
    dbsz, n_new, _ = x_sample.shape
    assert n_new == 1 and seq % (BAND * max(DILATIONS)) == 0
    cos_p, sin_p = _rope_tables(jnp.arange(seq))
    cos_s, sin_s = _rope_tables(PAST_LEN + jnp.arange(n_new))

    hp = x_prompt.reshape(bsz * seq, D_MODEL)
    hs = x_sample.reshape(dbsz * n_new, D_MODEL)
    mem = mem_prompt.reshape(bsz * N_MEM, D_MODEL)
    outs = [[] for _ in range(8)]
    for layer in range(DEPTH):
        w_in_b = w_in[layer].astype(BF16)
        w_out_b = w_out[layer].astype(BF16)
        lb_raw = hgrn_lb_raw
        norm_g = hgrn_norm_g[layer][None]
        lg, lbias = ln_g[layer][None], ln_b[layer][None]

        proj = _project(hp, w_in_b, 1024, 1024)
        za, k1, v1 = _prompt_attention(proj, cos_p, sin_p, bsz, seq)
        zh, s1 = _prompt_hgrn(proj, lb_raw, norm_g, layer, bsz, seq)
        zm, mk1, mv1 = _prompt_mem(proj, mem, w_mem_kv[layer].astype(BF16), bsz, seq)
        hp = _merge(za, zh, zm, hp, w_out_b, lg, lbias, 256)

        proj_s = _project(hs, w_in_b, dbsz, 1024).transpose(1, 0, 2)
        zs, k2, v2, s2 = _sample_mixers(proj_s, cos_s, sin_s, cache_win_k, cache_win_v, state_hgrn,
                                        cache_mem_k, cache_mem_v, lb_raw, norm_g, layer)
        zs = zs.transpose(1, 0, 2)
        hs = _merge(zs[:N_HEADS_ATTN], zs[N_HEADS_ATTN:N_HEADS_ATTN + N_HEADS_HGRN],
                    zs[N_HEADS_ATTN + N_HEADS_HGRN:], hs, w_out_b, lg, lbias, dbsz)

        new = (k1.transpose(0, 2, 1, 3), v1.transpose(0, 2, 1, 3), s1, mk1, mv1,
               k2.reshape(dbsz, n_new, N_HEADS_ATTN, HEAD_DIM), v2.reshape(dbsz, n_new, N_HEADS_ATTN, HEAD_DIM),
               s2.astype(state_hgrn.dtype))
        for acc, val in zip(outs, new):
            acc.append(val)

    return (hp.reshape(bsz, seq, D_MODEL), hs.reshape(dbsz, n_new, D_MODEL), *[jnp.stack(o) for o in outs])
```

```python
import functools

import jax
import jax.numpy as jnp
import numpy as np
from jax import lax
from jax.experimental import pallas as pl
from jax.experimental.pallas import tpu as pltpu

F32 = jnp.float32
BF16 = jnp.bfloat16

D_MODEL = 2048
DEPTH = 1
PAST_LEN = 8192
HEAD_DIM = 128
N_HEADS_ATTN = 6
N_HEADS_HGRN = 6
N_HEADS_MEM = 4
W_ATTN = N_HEADS_ATTN * HEAD_DIM
W_HGRN = N_HEADS_HGRN * HEAD_DIM
W_MEM = N_HEADS_MEM * HEAD_DIM
MIX_WIDTH = W_ATTN + W_HGRN + W_MEM
WINDOWS = (128, 512, 2048)
DILATIONS = (1, 4, 16)
N_MEM = 256
ROPE_THETA = 10000.0
LN_EPS = 1e-5
RMS_EPS = 1e-6
NEG_INF = -1e30
DEEPNORM_ALPHA = (2 * DEPTH) ** 0.25
ATTN_SCALE = HEAD_DIM ** -0.5

G_QA, G_KA, G_VA, G_GA = 0, 6, 12, 18
G_QB, G_FB, G_IB, G_GB = 24, 30, 36, 42
G_QM, G_GM = 48, 52
N_GROUPS = 56

LANES = 128
BAND = 128
HGRN_CHUNK = 64
HGRN_SUB = 16
HGRN_HEADS_PER_STEP = 3
HGRN_PREP_ROWS = 256
HGRN_SCAN_CHUNKS = 4
HGRN_SAFE_LOG_RANGE = 70.0
VMEM_LIMIT = 48 * 1024 * 1024


def _params(sem, vmem=VMEM_LIMIT):
    return pltpu.CompilerParams(dimension_semantics=sem, vmem_limit_bytes=vmem)


def _sigmoid(x):
    return 1.0 / (1.0 + jnp.exp(-x))


def _silu(x):
    return x * _sigmoid(x)


def _dot_nt(a, b):
    return lax.dot_general(a, b, (((1,), (1,)), ((), ())), preferred_element_type=F32)


def _proj_kernel(x_ref, w_ref, o_ref, xb_ref):
    @pl.when(pl.program_id(1) == 0)
    def _():
        xb_ref[...] = x_ref[...].astype(BF16)

    acc = jnp.dot(xb_ref[...], w_ref[...], preferred_element_type=F32)
    for j in range(o_ref.shape[0]):
        o_ref[j] = acc[:, j * LANES:(j + 1) * LANES]


def _project(x, w_bf16, tm, tn):
    m, k = x.shape
    n = w_bf16.shape[1]
    return pl.pallas_call(
        _proj_kernel,
        grid=(m // tm, n // tn),
        in_specs=[pl.BlockSpec((tm, k), lambda i, j: (i, 0)),
                  pl.BlockSpec((k, tn), lambda i, j: (0, j))],
        out_specs=pl.BlockSpec((tn // LANES, tm, LANES), lambda i, j: (j, i, 0)),
        out_shape=jax.ShapeDtypeStruct((n // LANES, m, LANES), F32),
        scratch_shapes=[pltpu.VMEM((tm, k), BF16)],
        compiler_params=_params(("parallel", "arbitrary")),
        name="in_proj",
    )(x, w_bf16)


def _rope(x, cos, sin_signed):
    return x * cos + pltpu.roll(x, HEAD_DIM // 2, 1) * sin_signed


def _band_bias():
    qi = lax.broadcasted_iota(jnp.int32, (BAND, BAND), 0)
    ki = lax.broadcasted_iota(jnp.int32, (BAND, BAND), 1)
    prev = jnp.where(ki >= qi, 0.0, NEG_INF).astype(F32)
    cur = jnp.where(ki <= qi, 0.0, NEG_INF).astype(F32)
    return prev, cur


def _softmax_part(qb, kb, vb, bias):
    s = _dot_nt(qb.astype(BF16), kb.astype(BF16)) + bias
    m = jnp.max(s, axis=-1, keepdims=True)
    p = jnp.exp(s - m)
    l = jnp.sum(p, axis=-1, keepdims=True)
    acc = jnp.dot(p.astype(BF16), vb.astype(BF16), preferred_element_type=F32)
    return acc, m, l


def _attn_kernel(q_ref, k_ref, v_ref, g_ref, cos_ref, sin_ref, z_ref, ko_ref, vo_ref,
                 qs_ref, acc_ref, m_ref, l_ref):
    seq = q_ref.shape[0]
    n_blk = seq // BAND
    cos = cos_ref[...]
    sin = sin_ref[...]
    qs_ref[...] = _rope(q_ref[...], cos, sin) * ATTN_SCALE
    ko_ref[...] = _rope(k_ref[...], cos, sin)
    vo_ref[...] = v_ref[...]
    bias_prev, bias_cur = _band_bias()
    bias_both = jnp.concatenate([bias_prev, bias_cur], axis=1)

    for br, dil in enumerate(DILATIONS):
        n_cls_blk = n_blk // dil
        for r in range(dil):
            for n in range(n_cls_blk):
                base = r + dil * BAND * n
                rows = pl.ds(base, BAND, stride=dil) if dil > 1 else pl.ds(base, BAND)
                if n == 0:
                    keys, bias = rows, bias_cur
                else:
                    kbase = base - dil * BAND
                    keys = pl.ds(kbase, 2 * BAND, stride=dil) if dil > 1 else pl.ds(kbase, 2 * BAND)
                    bias = bias_both
                acc, m, l = _softmax_part(qs_ref[rows, :], ko_ref[keys, :], v_ref[keys, :], bias)
                acc_ref[br, rows, :] = acc
                m_ref[br, rows, :] = m
                l_ref[br, rows, :] = l

    for n in range(n_blk):
        rows = pl.ds(n * BAND, BAND)
        ms = [m_ref[br, rows, :] for br in range(len(DILATIONS))]
        m_all = functools.reduce(jnp.maximum, ms)
        num = jnp.zeros((BAND, HEAD_DIM), F32)
        den = jnp.zeros((BAND, 1), F32)
        for br in range(len(DILATIONS)):
            w = jnp.exp(ms[br] - m_all)
            num = num + w * acc_ref[br, rows, :]
            den = den + w * l_ref[br, rows, :]
        z_ref[rows, :] = ((num / den) * _silu(g_ref[rows, :])).astype(z_ref.dtype)


def _prompt_attention(proj, cos, sin, bsz, seq):
    def col(g0):
        return pl.BlockSpec((None, seq, HEAD_DIM), lambda b, h: (g0 + h, b, 0))

    table = pl.BlockSpec((seq, HEAD_DIM), lambda b, h: (0, 0))
    kv_out = pl.BlockSpec((None, None, seq, HEAD_DIM), lambda b, h: (b, h, 0, 0))
    n_br = len(DILATIONS)
    return pl.pallas_call(
        _attn_kernel,
        grid=(bsz, N_HEADS_ATTN),
        in_specs=[col(G_QA), col(G_KA), col(G_VA), col(G_GA), table, table],
        out_specs=[pl.BlockSpec((None, seq, HEAD_DIM), lambda b, h: (h, b, 0)), kv_out, kv_out],
        out_shape=[jax.ShapeDtypeStruct((N_HEADS_ATTN, bsz * seq, HEAD_DIM), BF16),
                   jax.ShapeDtypeStruct((bsz, N_HEADS_ATTN, seq, HEAD_DIM), F32),
                   jax.ShapeDtypeStruct((bsz, N_HEADS_ATTN, seq, HEAD_DIM), F32)],
        scratch_shapes=[pltpu.VMEM((seq, HEAD_DIM), F32),
                        pltpu.VMEM((n_br, seq, HEAD_DIM), F32),
                        pltpu.VMEM((n_br, seq, 1), F32),
                        pltpu.VMEM((n_br, seq, 1), F32)],
        compiler_params=_params(("parallel", "parallel")),
        name="prompt_attn",
    )(proj, proj, proj, proj, cos, sin)


def _lower_bound(lb_raw, layer):
    e = jnp.exp(lb_raw - jnp.max(lb_raw, axis=0, keepdims=True))
    sm = e / jnp.sum(e, axis=0, keepdims=True)
    return jnp.sum(sm[:layer + 1], axis=0, keepdims=True)


def _split3(x):
    hi = x.astype(BF16)
    r1 = x - hi.astype(F32)
    mid = r1.astype(BF16)
    lo = (r1 - mid.astype(F32)).astype(BF16)
    return hi, mid, lo


def _rms_gate(o, norm_g, gate):
    o = o * lax.rsqrt(jnp.mean(o * o, axis=-1, keepdims=True) + RMS_EPS)
    return o * norm_g * _silu(gate)


def _hgrn_gates(fb, lb, tril):
    f = lb + (1.0 - lb) * _sigmoid(fb)
    g = jnp.log(f)
    b = sum(jnp.dot(tril, piece, preferred_element_type=F32) for piece in _split3(g))
    return 1.0 - f, b


def _hgrn_fast_prepare(q_ref, f_ref, lb, qh_ref, kh_ref, el_ref):
    n_heads, seq, _ = q_ref.shape
    blk = HGRN_PREP_ROWS
    tril = _tril_ones(blk)

    def body(bi, carry):
        r0 = pl.multiple_of(bi * blk, blk)
        fb = jnp.concatenate([f_ref[h, pl.ds(r0, blk), :] for h in range(n_heads)], axis=1)
        f = lb + (1.0 - lb) * _sigmoid(fb)
        b_blk = sum(jnp.dot(tril, piece, preferred_element_type=F32) for piece in _split3(jnp.log(f)))
        kk = 1.0 - f
        for c0 in range(0, blk, HGRN_CHUNK):
            rs = slice(c0, c0 + HGRN_CHUNK)
            b = b_blk[rs] - b_blk[c0 - 1:c0] if c0 else b_blk[rs]
            rows = pl.ds(r0 + c0, HGRN_CHUNK)
            q = jnp.concatenate([q_ref[h, rows, :] for h in range(n_heads)], axis=1)
            qh_ref[rows, :] = (q * jnp.exp(b)).astype(BF16)
            kh_ref[rows, :] = (kk[rs] * jnp.exp(-b)).astype(BF16)
            el_ref[pl.ds(bi * (blk // HGRN_CHUNK) + c0 // HGRN_CHUNK, 1), :] = jnp.exp(b[HGRN_CHUNK - 1:])
        return carry

    lax.fori_loop(0, seq // blk, body, 0)


def _hgrn_fast_scan(i_ref, g_ref, norm_g, qh_ref, kh_ref, el_ref, st_ref, z_ref):
    n_heads, seq, _ = i_ref.shape
    c = HGRN_CHUNK
    heads = range(n_heads)
    chunks = range(HGRN_SCAN_CHUNKS)
    lanes = [slice(h * HEAD_DIM, (h + 1) * HEAD_DIM) for h in heads]
    causal = lax.broadcasted_iota(jnp.int32, (c, c), 0) >= lax.broadcasted_iota(jnp.int32, (c, c), 1)
    tn = (((0,), (0,)), ((), ()))

    def body(ti, carry):
        rows = [pl.ds(pl.multiple_of((ti * HGRN_SCAN_CHUNKS + k) * c, c), c) for k in chunks]
        qh = [[qh_ref[rows[k], lanes[h]] for h in heads] for k in chunks]
        kh = [[kh_ref[rows[k], lanes[h]] for h in heads] for k in chunks]
        vb = [[i_ref[h, rows[k], :].astype(BF16) for h in heads] for k in chunks]
        att = [[jnp.where(causal, _dot_nt(qh[k][h], kh[k][h]), 0.0).astype(BF16) for h in heads] for k in chunks]
        ds = [[lax.dot_general(vb[k][h], kh[k][h], tn, preferred_element_type=F32) for h in heads] for k in chunks]
        st = [st_ref[h] for h in heads]
        o = []
        for k in chunks:
            el = el_ref[pl.ds(ti * HGRN_SCAN_CHUNKS + k, 1), :]
            o.append([_dot_nt(qh[k][h], st[h].astype(BF16)) for h in heads])
            st = [(st[h] + ds[k][h]) * el[:, lanes[h]] for h in heads]
        for h in heads:
            st_ref[h] = st[h]
        for k in chunks:
            for h in heads:
                o_kh = o[k][h] + jnp.dot(att[k][h], vb[k][h], preferred_element_type=F32)
                z_ref[h, rows[k], :] = _rms_gate(o_kh, norm_g[:, lanes[h]], g_ref[h, rows[k], :]).astype(z_ref.dtype)
        return carry

    lax.fori_loop(0, seq // (c * HGRN_SCAN_CHUNKS), body, 0)


def _hgrn_chunk(q, fb, v, lb, st, tril):
    c = q.shape[0]
    kk, b = _hgrn_gates(fb, lb, tril)
    o = _dot_nt((q * jnp.exp(b)).astype(BF16), st.astype(BF16))

    s_idx = lax.broadcasted_iota(jnp.int32, (c, 1), 0)
    lane = lax.broadcasted_iota(jnp.int32, (HGRN_SUB, c), 1)
    row = lax.broadcasted_iota(jnp.int32, (HGRN_SUB, c), 0)
    att_rows = []
    for i0 in range(0, c, HGRN_SUB):
        qi = q[i0:i0 + HGRN_SUB]
        bi = b[i0:i0 + HGRN_SUB]
        if i0 > 0:
            bref = b[i0 - 1:i0]
            kt = jnp.where(s_idx < i0, kk * jnp.exp(jnp.minimum(bref - b, 0.0)), 0.0)
            att = _dot_nt((qi * jnp.exp(bi - bref)).astype(BF16), kt.astype(BF16))
        else:
            att = jnp.zeros((HGRN_SUB, c), F32)
        for j in range(HGRN_SUB):
            s = i0 + j
            e = jnp.exp(jnp.minimum(bi - b[s:s + 1], 0.0))
            colv = jnp.sum(qi * kk[s:s + 1] * e, axis=-1, keepdims=True)
            att = jnp.where((lane == s) & (row >= j), colv, att)
        att_rows.append(att)
    att = jnp.concatenate(att_rows, axis=0)
    vb = v.astype(BF16)
    o = o + jnp.dot(att.astype(BF16), vb, preferred_element_type=F32)

    b_last = b[c - 1:c]
    kd = (kk * jnp.exp(b_last - b)).astype(BF16)
    st_new = st * jnp.exp(b_last) + lax.dot_general(vb, kd, (((0,), (0,)), ((), ())),
                                                    preferred_element_type=F32)
    return o, st_new


def _tril_ones(c):
    return (lax.broadcasted_iota(jnp.int32, (c, c), 0) >= lax.broadcasted_iota(jnp.int32, (c, c), 1)
            ).astype(BF16)


def _hgrn_kernel(layer, q_ref, f_ref, i_ref, g_ref, lb_ref, ng_ref, z_ref, s_ref, st_ref, qh_ref, kh_ref, el_ref):
    n_heads, seq, _ = q_ref.shape
    lb = _lower_bound(lb_ref[...], layer)
    norm_g = ng_ref[...]
    st_ref[...] = jnp.zeros_like(st_ref)

    def fast():
        _hgrn_fast_prepare(q_ref, f_ref, lb, qh_ref, kh_ref, el_ref)
        _hgrn_fast_scan(i_ref, g_ref, norm_g, qh_ref, kh_ref, el_ref, st_ref, z_ref)

    def safe():
        tril = _tril_ones(HGRN_CHUNK)

        def body(ci, carry):
            rows = pl.ds(pl.multiple_of(ci * HGRN_CHUNK, HGRN_CHUNK), HGRN_CHUNK)
            for h in range(n_heads):
                lanes = slice(h * HEAD_DIM, (h + 1) * HEAD_DIM)
                o, st = _hgrn_chunk(q_ref[h, rows, :], f_ref[h, rows, :], i_ref[h, rows, :], lb[:, lanes],
                                    st_ref[h], tril)
                st_ref[h] = st
                z_ref[h, rows, :] = _rms_gate(o, norm_g[:, lanes], g_ref[h, rows, :]).astype(z_ref.dtype)
            return carry

        lax.fori_loop(0, seq // HGRN_CHUNK, body, 0)

    fast_ok = HGRN_CHUNK * -jnp.log(jnp.min(lb)) <= HGRN_SAFE_LOG_RANGE
    lax.cond(fast_ok, fast, safe)
    for h in range(n_heads):
        s_ref[h] = st_ref[h].T


def _prompt_hgrn(proj, lb_raw, norm_g, layer, bsz, seq):
    hps = HGRN_HEADS_PER_STEP
    assert N_HEADS_HGRN % hps == 0 and all(g % hps == 0 for g in (G_QB, G_FB, G_IB, G_GB))

    def cols(g0):
        return pl.BlockSpec((hps, seq, HEAD_DIM), lambda b, h: (g0 // hps + h, b, 0))

    return pl.pallas_call(
        functools.partial(_hgrn_kernel, layer),
        grid=(bsz, N_HEADS_HGRN // hps),
        in_specs=[cols(G_QB), cols(G_FB), cols(G_IB), cols(G_GB),
                  pl.BlockSpec((DEPTH + 1, hps * HEAD_DIM), lambda b, h: (0, h)),
                  pl.BlockSpec((1, hps * HEAD_DIM), lambda b, h: (0, h))],
        out_specs=[pl.BlockSpec((hps, seq, HEAD_DIM), lambda b, h: (h, b, 0)),
                   pl.BlockSpec((None, hps, HEAD_DIM, HEAD_DIM), lambda b, h: (b, h, 0, 0))],
        out_shape=[jax.ShapeDtypeStruct((N_HEADS_HGRN, bsz * seq, HEAD_DIM), BF16),
                   jax.ShapeDtypeStruct((bsz, N_HEADS_HGRN, HEAD_DIM, HEAD_DIM), F32)],
        scratch_shapes=[pltpu.VMEM((hps, HEAD_DIM, HEAD_DIM), F32),
                        pltpu.VMEM((seq, hps * HEAD_DIM), BF16),
                        pltpu.VMEM((seq, hps * HEAD_DIM), BF16),
                        pltpu.VMEM((seq // HGRN_CHUNK, hps * HEAD_DIM), F32)],
        compiler_params=_params(("parallel", "parallel")),
        name="prompt_hgrn",
    )(proj, proj, proj, proj, lb_raw, norm_g)


MEM_ROWS = 512


def _mem_kernel(q_ref, g_ref, mem_ref, w_ref, z_ref, mk_ref, mv_ref):
    seq = q_ref.shape[1]
    kv = jnp.dot(mem_ref[...].astype(BF16), w_ref[...], preferred_element_type=F32)
    for h in range(N_HEADS_MEM):
        mk = kv[:, h * HEAD_DIM:(h + 1) * HEAD_DIM]
        mv = kv[:, W_MEM + h * HEAD_DIM:W_MEM + (h + 1) * HEAD_DIM]
        mk_ref[:, h, :] = mk
        mv_ref[:, h, :] = mv
        mk, mv = mk.astype(BF16), mv.astype(BF16)
        for n in range(seq // MEM_ROWS):
            rows = pl.ds(n * MEM_ROWS, MEM_ROWS)
            s = _dot_nt((q_ref[h, rows, :] * ATTN_SCALE).astype(BF16), mk)
            p = jnp.exp(s - jnp.max(s, axis=-1, keepdims=True))
            den = jnp.sum(p, axis=-1, keepdims=True)
            o = jnp.dot(p.astype(BF16), mv, preferred_element_type=F32) / den
            z_ref[h, rows, :] = (o * _silu(g_ref[h, rows, :])).astype(z_ref.dtype)


def _prompt_mem(proj, mem, w_bf16, bsz, seq):
    assert G_QM % N_HEADS_MEM == 0 and G_GM % N_HEADS_MEM == 0

    def cols(g0):
        return pl.BlockSpec((N_HEADS_MEM, seq, HEAD_DIM), lambda b: (g0 // N_HEADS_MEM, b, 0))

    kv_out = pl.BlockSpec((None, N_MEM, N_HEADS_MEM, HEAD_DIM), lambda b: (b, 0, 0, 0))
    kv_shape = jax.ShapeDtypeStruct((bsz, N_MEM, N_HEADS_MEM, HEAD_DIM), F32)
    return pl.pallas_call(
        _mem_kernel,
        grid=(bsz,),
        in_specs=[cols(G_QM), cols(G_GM), pl.BlockSpec((N_MEM, D_MODEL), lambda b: (b, 0)),
                  pl.BlockSpec((D_MODEL, 2 * W_MEM), lambda b: (0, 0))],
        out_specs=[pl.BlockSpec((N_HEADS_MEM, seq, HEAD_DIM), lambda b: (0, b, 0)), kv_out, kv_out],
        out_shape=[jax.ShapeDtypeStruct((N_HEADS_MEM, bsz * seq, HEAD_DIM), BF16), kv_shape, kv_shape],
        compiler_params=_params(("parallel",)),
        name="prompt_mem",
    )(proj, proj, mem, w_bf16)


def _merge_kernel(za_ref, zh_ref, zm_ref, x_ref, w_ref, lg_ref, lb_ref, o_ref, z_ref):
    c0 = 0
    for ref in (za_ref, zh_ref, zm_ref):
        for c in range(ref.shape[0]):
            z_ref[:, (c0 + c) * LANES:(c0 + c + 1) * LANES] = ref[c].astype(BF16)
        c0 += ref.shape[0]
    y = jnp.dot(z_ref[...], w_ref[...], preferred_element_type=F32)
    r = DEEPNORM_ALPHA * x_ref[...] + y
    mu = jnp.mean(r, axis=-1, keepdims=True)
    d = r - mu
    var = jnp.mean(d * d, axis=-1, keepdims=True)
    o_ref[...] = d * lax.rsqrt(var + LN_EPS) * lg_ref[...] + lb_ref[...]


def _merge(za, zh, zm, x, w_out_bf16, ln_g, ln_b, tm):
    m = x.shape[0]

    def slab(a):
        return pl.BlockSpec((a.shape[0], tm, LANES), lambda i: (0, i, 0))

    const = lambda shape: pl.BlockSpec(shape, lambda i: (0, 0))
    return pl.pallas_call(
        _merge_kernel,
        grid=(m // tm,),
        in_specs=[slab(za), slab(zh), slab(zm), pl.BlockSpec((tm, D_MODEL), lambda i: (i, 0)),
                  const((MIX_WIDTH, D_MODEL)), const((1, D_MODEL)), const((1, D_MODEL))],
        out_specs=pl.BlockSpec((tm, D_MODEL), lambda i: (i, 0)),
        out_shape=jax.ShapeDtypeStruct((m, D_MODEL), F32),
        scratch_shapes=[pltpu.VMEM((tm, MIX_WIDTH), BF16)],
        compiler_params=_params(("parallel",)),
        name="merge",
    )(za, zh, zm, x, w_out_bf16, ln_g, ln_b)


def _column(row):
    return jnp.broadcast_to(row, (HEAD_DIM, HEAD_DIM)).T


HBM_TILE_ROWS = 8


def _window_pieces(past):
    pieces = []
    for dil in DILATIONS:
        if dil == 1:
            pieces.append((0, past - BAND, BAND, None))
        elif dil < HBM_TILE_ROWS:
            n = BAND * dil // HBM_TILE_ROWS
            pieces += [(1, past // HBM_TILE_ROWS - n, n, r) for r in range(0, HBM_TILE_ROWS, dil)]
        else:
            pieces.append((2, past // dil - BAND, BAND, 0))
    return pieces


def _cache_views(cache):
    depth, bsz, past, nh, hd = cache.shape
    assert all(d == 1 or HBM_TILE_ROWS % d == 0 or d % HBM_TILE_ROWS == 0 for d in DILATIONS)
    big = max(DILATIONS)
    hm = cache.transpose(0, 1, 3, 2, 4)
    return (hm, hm.reshape(depth, bsz, nh, past // HBM_TILE_ROWS, HBM_TILE_ROWS, hd),
            hm.reshape(depth, bsz, nh, past // big, big, hd))


def _sample_kernel(layer, past, p_ref, cos_ref, sin_ref, k0_hbm, k1_hbm, k2_hbm, v0_hbm, v1_hbm, v2_hbm,
                   st_ref, mk_ref, mv_ref, lb_ref, ng_ref, z_ref, ko_ref, vo_ref, so_ref, kbuf, vbuf, sem):
    b = pl.program_id(0)
    slot = b % 2
    pieces = _window_pieces(past)

    def window_copies(row, sl):
        out = []
        for ci, (views, buf) in enumerate((((k0_hbm, k1_hbm, k2_hbm), kbuf), ((v0_hbm, v1_hbm, v2_hbm), vbuf))):
            off = 0
            for pi, (vi, start, count, res) in enumerate(pieces):
                view = views[vi]
                src = (view.at[layer, row, :, pl.ds(start, count), :] if res is None
                       else view.at[layer, row, :, pl.ds(start, count), res, :])
                out.append(pltpu.make_async_copy(src, buf.at[sl, :, pl.ds(off, count), :], sem.at[sl, ci, pi]))
                off += count
        return out

    @pl.when(b == 0)
    def _():
        for c in window_copies(b, slot):
            c.start()

    @pl.when(b + 1 < pl.num_programs(0))
    def _():
        for c in window_copies(b + 1, 1 - slot):
            c.start()

    cos = cos_ref[...]
    sin = sin_ref[...]
    q_all = _rope(p_ref[G_QA:G_QA + N_HEADS_ATTN, :], cos, sin) * ATTN_SCALE
    k_all = _rope(p_ref[G_KA:G_KA + N_HEADS_ATTN, :], cos, sin)
    v_all = p_ref[G_VA:G_VA + N_HEADS_ATTN, :]
    ko_ref[...] = k_all
    vo_ref[...] = v_all

    for c in window_copies(b, slot):
        c.wait()

    for h in range(N_HEADS_ATTN):
        q = q_all[h:h + 1]
        s_new = jnp.sum(q * k_all[h:h + 1], axis=-1, keepdims=True)
        s = jnp.sum(kbuf[slot, h] * q, axis=-1, keepdims=True)
        m = jnp.maximum(jnp.max(s, axis=0, keepdims=True), s_new)
        p = jnp.exp(s - m)
        p_new = jnp.exp(s_new - m) * len(DILATIONS)
        den = jnp.sum(p, axis=0, keepdims=True) + p_new
        num = jnp.sum(p * vbuf[slot, h], axis=0, keepdims=True) + p_new * v_all[h:h + 1]
        z_ref[h:h + 1, :] = (num / den) * _silu(p_ref[G_GA + h:G_GA + h + 1, :])

    lb_all = lb_ref[...]
    for h in range(N_HEADS_HGRN):
        lanes = slice(h * HEAD_DIM, (h + 1) * HEAD_DIM)
        lb = _lower_bound(lb_all[:, lanes], layer)
        f = lb + (1.0 - lb) * _sigmoid(p_ref[G_FB + h:G_FB + h + 1, :])
        f_col = _column(f)
        q_col = _column(p_ref[G_QB + h:G_QB + h + 1, :])
        s_new = f_col * st_ref[h] + (1.0 - f_col) * p_ref[G_IB + h:G_IB + h + 1, :]
        so_ref[h] = s_new
        o = jnp.sum(s_new * q_col, axis=0, keepdims=True)
        z_ref[N_HEADS_ATTN + h:N_HEADS_ATTN + h + 1, :] = _rms_gate(
            o, ng_ref[:, lanes], p_ref[G_GB + h:G_GB + h + 1, :])

    q_m = p_ref[G_QM:G_QM + N_HEADS_MEM, :] * ATTN_SCALE
    s = jnp.sum(mk_ref[...] * q_m[None], axis=-1, keepdims=True)
    p = jnp.exp(s - jnp.max(s, axis=0)[None])
    o = jnp.sum(p * mv_ref[...], axis=0) / jnp.sum(p, axis=0)
    row = N_HEADS_ATTN + N_HEADS_HGRN
    z_ref[row:row + N_HEADS_MEM, :] = o * _silu(p_ref[G_GM:G_GM + N_HEADS_MEM, :])


def _sample_mixers(proj_rows, cos, sin, win_k, win_v, state, mem_k, mem_v, lb_raw, norm_g, layer):
    bsz, past = win_k.shape[1:3]
    assert all(past % d == 0 and past >= w for w, d in zip(WINDOWS, DILATIONS))
    assert all(w // d == BAND for w, d in zip(WINDOWS, DILATIONS)) and past % HBM_TILE_ROWS == 0
    n_keys = BAND * len(DILATIONS)
    n_pieces = len(_window_pieces(past))
    any_spec = pl.BlockSpec(memory_space=pl.ANY)
    vec = lambda n: pl.BlockSpec((n, W_HGRN), lambda b: (0, 0))
    heads = lambda n: pl.BlockSpec((None, n, HEAD_DIM), lambda b: (b, 0, 0))
    mem_spec = pl.BlockSpec((None, None, N_MEM, N_HEADS_MEM, HEAD_DIM), lambda b: (layer, b, 0, 0, 0))
    return pl.pallas_call(
        functools.partial(_sample_kernel, layer, past),
        grid=(bsz,),
        in_specs=[heads(N_GROUPS), pl.BlockSpec((1, HEAD_DIM), lambda b: (0, 0)),
                  pl.BlockSpec((1, HEAD_DIM), lambda b: (0, 0)), *([any_spec] * 6),
                  pl.BlockSpec((None, None, N_HEADS_HGRN, HEAD_DIM, HEAD_DIM), lambda b: (layer, b, 0, 0, 0)),
                  mem_spec, mem_spec, vec(DEPTH + 1), vec(1)],
        out_specs=[heads(MIX_WIDTH // HEAD_DIM), heads(N_HEADS_ATTN), heads(N_HEADS_ATTN),
                   pl.BlockSpec((None, N_HEADS_HGRN, HEAD_DIM, HEAD_DIM), lambda b: (b, 0, 0, 0))],
        out_shape=[jax.ShapeDtypeStruct((bsz, MIX_WIDTH // HEAD_DIM, HEAD_DIM), F32),
                   jax.ShapeDtypeStruct((bsz, N_HEADS_ATTN, HEAD_DIM), F32),
                   jax.ShapeDtypeStruct((bsz, N_HEADS_ATTN, HEAD_DIM), F32),
                   jax.ShapeDtypeStruct((bsz, N_HEADS_HGRN, HEAD_DIM, HEAD_DIM), F32)],
        scratch_shapes=[pltpu.VMEM((2, N_HEADS_ATTN, n_keys, HEAD_DIM), F32),
                        pltpu.VMEM((2, N_HEADS_ATTN, n_keys, HEAD_DIM), F32),
                        pltpu.SemaphoreType.DMA((2, 2, n_pieces))],
        compiler_params=_params(("arbitrary",)),
        name="sample_mixers",
    )(proj_rows, cos, sin, *_cache_views(win_k), *_cache_views(win_v), state, mem_k, mem_v, lb_raw, norm_g)


def _rope_tables(pos):
    half = HEAD_DIM // 2
    inv_freq = 1.0 / (ROPE_THETA ** (jnp.arange(half, dtype=F32) / half))
    ang = pos.astype(F32)[:, None] * inv_freq[None, :]
    cos, sin = jnp.cos(ang), jnp.sin(ang)
    return jnp.concatenate([cos, cos], axis=-1), jnp.concatenate([-sin, sin], axis=-1)


def kernel(x_prompt, x_sample, cache_win_k, cache_win_v, state_hgrn, cache_mem_k, cache_mem_v, mem_prompt,
           w_in, w_mem_kv, hgrn_lb_raw, hgrn_norm_g, w_out, ln_g, ln_b):
    bsz, seq, _ = x_prompt.shape
    dbsz, n_new, _ = x_sample.shape
    assert n_new == 1 and seq % (BAND * max(DILATIONS)) == 0
    cos_p, sin_p = _rope_tables(jnp.arange(seq))
    cos_s, sin_s = _rope_tables(PAST_LEN + jnp.arange(n_new))

    hp = x_prompt.reshape(bsz * seq, D_MODEL)
    hs = x_sample.reshape(dbsz * n_new, D_MODEL)
    mem = mem_prompt.reshape(bsz * N_MEM, D_MODEL)
    outs = [[] for _ in range(8)]
    for layer in range(DEPTH):
        w_in_b = w_in[layer].astype(BF16)
        w_out_b = w_out[layer].astype(BF16)
        lb_raw = hgrn_lb_raw
        norm_g = hgrn_norm_g[layer][None]
        lg, lbias = ln_g[layer][None], ln_b[layer][None]

        proj = _project(hp, w_in_b, 1024, 1024)
        za, k1, v1 = _prompt_attention(proj, cos_p, sin_p, bsz, seq)
        zh, s1 = _prompt_hgrn(proj, lb_raw, norm_g, layer, bsz, seq)
        zm, mk1, mv1 = _prompt_mem(proj, mem, w_mem_kv[layer].astype(BF16), bsz, seq)
        hp = _merge(za, zh, zm, hp, w_out_b, lg, lbias, 256)

        proj_s = _project(hs, w_in_b, dbsz, 1024).transpose(1, 0, 2)
        zs, k2, v2, s2 = _sample_mixers(proj_s, cos_s, sin_s, cache_win_k, cache_win_v, state_hgrn,
                                        cache_mem_k, cache_mem_v, lb_raw, norm_g, layer)
        zs = zs.transpose(1, 0, 2)
        hs = _merge(zs[:N_HEADS_ATTN], zs[N_HEADS_ATTN:N_HEADS_ATTN + N_HEADS_HGRN],
                    zs[N_HEADS_ATTN + N_HEADS_HGRN:], hs, w_out_b, lg, lbias, dbsz)

        new = (k1.transpose(0, 2, 1, 3), v1.transpose(0, 2, 1, 3), s1, mk1, mv1,
               k2.reshape(dbsz, n_new, N_HEADS_ATTN, HEAD_DIM), v2.reshape(dbsz, n_new, N_HEADS_ATTN, HEAD_DIM),
               s2.astype(state_hgrn.dtype))
        for acc, val in zip(outs, new):
            acc.append(val)

    return (hp.reshape(bsz, seq, D_MODEL), hs.reshape(dbsz, n_new, D_MODEL), *[jnp.stack(o) for o in outs])
```

```python
import functools

import jax
import jax.numpy as jnp
import numpy as np
from jax import lax
from jax.experimental import pallas as pl
from jax.experimental.pallas import tpu as pltpu

F32 = jnp.float32
BF16 = jnp.bfloat16

D_MODEL = 2048
DEPTH = 1
PAST_LEN = 8192
HEAD_DIM = 128
N_HEADS_ATTN = 6
N_HEADS_HGRN = 6
N_HEADS_MEM = 4
W_ATTN = N_HEADS_ATTN * HEAD_DIM
W_HGRN = N_HEADS_HGRN * HEAD_DIM
W_MEM = N_HEADS_MEM * HEAD_DIM
MIX_WIDTH = W_ATTN + W_HGRN + W_MEM
WINDOWS = (128, 512, 2048)
DILATIONS = (1, 4, 16)
N_MEM = 256
ROPE_THETA = 10000.0
LN_EPS = 1e-5
RMS_EPS = 1e-6
NEG_INF = -1e30
DEEPNORM_ALPHA = (2 * DEPTH) ** 0.25
ATTN_SCALE = HEAD_DIM ** -0.5
ATTN_SCALE_LOG2 = ATTN_SCALE * 1.4426950408889634

G_QA, G_KA, G_VA, G_GA = 0, 6, 12, 18
G_QB, G_FB, G_IB, G_GB = 24, 30, 36, 42
G_QM, G_GM = 48, 52
N_GROUPS = 56

LANES = 128
BAND = 128
ATTN_GROUP = 16
HGRN_CHUNK = 64
HGRN_SUB = 16
HGRN_HEADS_PER_STEP = 3
HGRN_PREP_ROWS = 256
HGRN_SCAN_CHUNKS = 4
HGRN_SAFE_LOG_RANGE = 70.0
VMEM_LIMIT = 48 * 1024 * 1024


def _params(sem, vmem=VMEM_LIMIT):
    return pltpu.CompilerParams(dimension_semantics=sem, vmem_limit_bytes=vmem)


def _sigmoid(x):
    return 1.0 / (1.0 + jnp.exp(-x))


def _silu(x):
    return x * _sigmoid(x)


def _dot_nt(a, b):
    return lax.dot_general(a, b, (((1,), (1,)), ((), ())), preferred_element_type=F32)


def _proj_kernel(x_ref, w_ref, o_ref, xb_ref):
    @pl.when(pl.program_id(1) == 0)
    def _():
        xb_ref[...] = x_ref[...].astype(BF16)

    acc = jnp.dot(xb_ref[...], w_ref[...], preferred_element_type=F32)
    for j in range(o_ref.shape[0]):
        o_ref[j] = acc[:, j * LANES:(j + 1) * LANES]


def _project(x, w_bf16, tm, tn):
    m, k = x.shape
    n = w_bf16.shape[1]
    return pl.pallas_call(
        _proj_kernel,
        grid=(m // tm, n // tn),
        in_specs=[pl.BlockSpec((tm, k), lambda i, j: (i, 0)),
                  pl.BlockSpec((k, tn), lambda i, j: (0, j))],
        out_specs=pl.BlockSpec((tn // LANES, tm, LANES), lambda i, j: (j, i, 0)),
        out_shape=jax.ShapeDtypeStruct((n // LANES, m, LANES), F32),
        scratch_shapes=[pltpu.VMEM((tm, k), BF16)],
        compiler_params=_params(("parallel", "arbitrary")),
        name="in_proj",
    )(x, w_bf16)


def _rope(x, cos, sin_signed):
    return x * cos + pltpu.roll(x, HEAD_DIM // 2, 1) * sin_signed


def _band_bias():
    qi = lax.broadcasted_iota(jnp.int32, (BAND, BAND), 0)
    ki = lax.broadcasted_iota(jnp.int32, (BAND, BAND), 1)
    prev = jnp.where(ki >= qi, 0.0, NEG_INF).astype(F32)
    cur = jnp.where(ki <= qi, 0.0, NEG_INF).astype(F32)
    return prev, cur


def _attn_kernel(q_ref, k_ref, v_ref, g_ref, cos_ref, sin_ref, z_ref, ko_ref, vo_ref,
                 qs_ref, acc_ref, m_ref, l_ref):
    seq = q_ref.shape[0]
    n_blk = seq // BAND
    cos = cos_ref[...]
    sin = sin_ref[...]
    qs_ref[...] = _rope(q_ref[...], cos, sin) * ATTN_SCALE_LOG2
    ko_ref[...] = _rope(k_ref[...], cos, sin)
    vo_ref[...] = v_ref[...]
    bias_prev, bias_cur = _band_bias()
    bias_both = jnp.concatenate([bias_prev, bias_cur], axis=1)

    items = []
    for br, dil in enumerate(DILATIONS):
        for r in range(dil):
            for n in range(n_blk // dil):
                base = r + dil * BAND * n
                first = n == 0
                kbase = base if first else base - dil * BAND
                n_keys = BAND if first else 2 * BAND
                rows = pl.ds(base, BAND, stride=dil) if dil > 1 else pl.ds(base, BAND)
                keys = pl.ds(kbase, n_keys, stride=dil) if dil > 1 else pl.ds(kbase, n_keys)
                items.append((br, rows, keys, bias_cur if first else bias_both))

    ones = jnp.ones((2 * BAND, HEAD_DIM), BF16)
    for g0 in range(0, len(items), ATTN_GROUP):
        group = items[g0:g0 + ATTN_GROUP]
        scores = [_dot_nt(qs_ref[rows, :].astype(BF16), ko_ref[keys, :].astype(BF16)) + bias
                  for _, rows, keys, bias in group]
        probs = []
        for s in scores:
            m = jnp.max(s, axis=-1, keepdims=True)
            probs.append((jnp.broadcast_to(m, (BAND, HEAD_DIM)), jnp.exp2(s - m).astype(BF16)))
        for (br, rows, keys, _), (m, p) in zip(group, probs):
            acc_ref[br, rows, :] = jnp.dot(p, v_ref[keys, :].astype(BF16), preferred_element_type=F32)
            l_ref[br, rows, :] = jnp.dot(p, ones[:p.shape[1]], preferred_element_type=F32)
            m_ref[br, rows, :] = m

    for n in range(n_blk):
        rows = pl.ds(n * BAND, BAND)
        ms = [m_ref[br, rows, :] for br in range(len(DILATIONS))]
        m_all = functools.reduce(jnp.maximum, ms)
        num = jnp.zeros((BAND, HEAD_DIM), F32)
        den = jnp.zeros((BAND, HEAD_DIM), F32)
        for br in range(len(DILATIONS)):
            w = jnp.exp2(ms[br] - m_all)
            num = num + w * acc_ref[br, rows, :]
            den = den + w * l_ref[br, rows, :]
        z_ref[rows, :] = ((num / den) * _silu(g_ref[rows, :])).astype(z_ref.dtype)


def _prompt_attention(proj, cos, sin, bsz, seq):
    def col(g0):
        return pl.BlockSpec((None, seq, HEAD_DIM), lambda b, h: (g0 + h, b, 0))

    table = pl.BlockSpec((seq, HEAD_DIM), lambda b, h: (0, 0))
    kv_out = pl.BlockSpec((None, None, seq, HEAD_DIM), lambda b, h: (b, h, 0, 0))
    n_br = len(DILATIONS)
    return pl.pallas_call(
        _attn_kernel,
        grid=(bsz, N_HEADS_ATTN),
        in_specs=[col(G_QA), col(G_KA), col(G_VA), col(G_GA), table, table],
        out_specs=[pl.BlockSpec((None, seq, HEAD_DIM), lambda b, h: (h, b, 0)), kv_out, kv_out],
        out_shape=[jax.ShapeDtypeStruct((N_HEADS_ATTN, bsz * seq, HEAD_DIM), BF16),
                   jax.ShapeDtypeStruct((bsz, N_HEADS_ATTN, seq, HEAD_DIM), F32),
                   jax.ShapeDtypeStruct((bsz, N_HEADS_ATTN, seq, HEAD_DIM), F32)],
        scratch_shapes=[pltpu.VMEM((seq, HEAD_DIM), F32),
                        pltpu.VMEM((n_br, seq, HEAD_DIM), F32),
                        pltpu.VMEM((n_br, seq, HEAD_DIM), F32),
                        pltpu.VMEM((n_br, seq, HEAD_DIM), F32)],
        compiler_params=_params(("parallel", "parallel")),
        name="prompt_attn",
    )(proj, proj, proj, proj, cos, sin)


def _lower_bound(lb_raw, layer):
    e = jnp.exp(lb_raw - jnp.max(lb_raw, axis=0, keepdims=True))
    sm = e / jnp.sum(e, axis=0, keepdims=True)
    return jnp.sum(sm[:layer + 1], axis=0, keepdims=True)


def _split3(x):
    hi = x.astype(BF16)
    r1 = x - hi.astype(F32)
    mid = r1.astype(BF16)
    lo = (r1 - mid.astype(F32)).astype(BF16)
    return hi, mid, lo


def _rms_gate(o, norm_g, gate):
    o = o * lax.rsqrt(jnp.mean(o * o, axis=-1, keepdims=True) + RMS_EPS)
    return o * norm_g * _silu(gate)


def _hgrn_gates(fb, lb, tril):
    f = lb + (1.0 - lb) * _sigmoid(fb)
    g = jnp.log(f)
    b = sum(jnp.dot(tril, piece, preferred_element_type=F32) for piece in _split3(g))
    return 1.0 - f, b


def _hgrn_fast_prepare(q_ref, f_ref, lb, qh_ref, kh_ref, el_ref):
    n_heads, seq, _ = q_ref.shape
    blk = HGRN_PREP_ROWS
    tril = _tril_ones(blk)

    def body(bi, carry):
        r0 = pl.multiple_of(bi * blk, blk)
        fb = jnp.concatenate([f_ref[h, pl.ds(r0, blk), :] for h in range(n_heads)], axis=1)
        f = lb + (1.0 - lb) * _sigmoid(fb)
        b_blk = sum(jnp.dot(tril, piece, preferred_element_type=F32) for piece in _split3(jnp.log(f)))
        kk = 1.0 - f
        for c0 in range(0, blk, HGRN_CHUNK):
            rs = slice(c0, c0 + HGRN_CHUNK)
            b = b_blk[rs] - b_blk[c0 - 1:c0] if c0 else b_blk[rs]
            rows = pl.ds(r0 + c0, HGRN_CHUNK)
            q = jnp.concatenate([q_ref[h, rows, :] for h in range(n_heads)], axis=1)
            qh_ref[rows, :] = (q * jnp.exp(b)).astype(BF16)
            kh_ref[rows, :] = (kk[rs] * jnp.exp(-b)).astype(BF16)
            el_ref[pl.ds(bi * (blk // HGRN_CHUNK) + c0 // HGRN_CHUNK, 1), :] = jnp.exp(b[HGRN_CHUNK - 1:])
        return carry

    lax.fori_loop(0, seq // blk, body, 0)


def _hgrn_fast_scan(i_ref, g_ref, norm_g, qh_ref, kh_ref, el_ref, st_ref, z_ref):
    n_heads, seq, _ = i_ref.shape
    c = HGRN_CHUNK
    heads = range(n_heads)
    chunks = range(HGRN_SCAN_CHUNKS)
    lanes = [slice(h * HEAD_DIM, (h + 1) * HEAD_DIM) for h in heads]
    causal = lax.broadcasted_iota(jnp.int32, (c, c), 0) >= lax.broadcasted_iota(jnp.int32, (c, c), 1)
    tn = (((0,), (0,)), ((), ()))

    def body(ti, carry):
        rows = [pl.ds(pl.multiple_of((ti * HGRN_SCAN_CHUNKS + k) * c, c), c) for k in chunks]
        qh = [[qh_ref[rows[k], lanes[h]] for h in heads] for k in chunks]
        kh = [[kh_ref[rows[k], lanes[h]] for h in heads] for k in chunks]
        vb = [[i_ref[h, rows[k], :].astype(BF16) for h in heads] for k in chunks]
        att = [[jnp.where(causal, _dot_nt(qh[k][h], kh[k][h]), 0.0).astype(BF16) for h in heads] for k in chunks]
        ds = [[lax.dot_general(vb[k][h], kh[k][h], tn, preferred_element_type=F32) for h in heads] for k in chunks]
        st = [st_ref[h] for h in heads]
        o = []
        for k in chunks:
            el = el_ref[pl.ds(ti * HGRN_SCAN_CHUNKS + k, 1), :]
            o.append([_dot_nt(qh[k][h], st[h].astype(BF16)) for h in heads])
            st = [(st[h] + ds[k][h]) * el[:, lanes[h]] for h in heads]
        for h in heads:
            st_ref[h] = st[h]
        for k in chunks:
            for h in heads:
                o_kh = o[k][h] + jnp.dot(att[k][h], vb[k][h], preferred_element_type=F32)
                z_ref[h, rows[k], :] = _rms_gate(o_kh, norm_g[:, lanes[h]], g_ref[h, rows[k], :]).astype(z_ref.dtype)
        return carry

    lax.fori_loop(0, seq // (c * HGRN_SCAN_CHUNKS), body, 0)


def _hgrn_chunk(q, fb, v, lb, st, tril):
    c = q.shape[0]
    kk, b = _hgrn_gates(fb, lb, tril)
    o = _dot_nt((q * jnp.exp(b)).astype(BF16), st.astype(BF16))

    s_idx = lax.broadcasted_iota(jnp.int32, (c, 1), 0)
    lane = lax.broadcasted_iota(jnp.int32, (HGRN_SUB, c), 1)
    row = lax.broadcasted_iota(jnp.int32, (HGRN_SUB, c), 0)
    att_rows = []
    for i0 in range(0, c, HGRN_SUB):
        qi = q[i0:i0 + HGRN_SUB]
        bi = b[i0:i0 + HGRN_SUB]
        if i0 > 0:
            bref = b[i0 - 1:i0]
            kt = jnp.where(s_idx < i0, kk * jnp.exp(jnp.minimum(bref - b, 0.0)), 0.0)
            att = _dot_nt((qi * jnp.exp(bi - bref)).astype(BF16), kt.astype(BF16))
        else:
            att = jnp.zeros((HGRN_SUB, c), F32)
        for j in range(HGRN_SUB):
            s = i0 + j
            e = jnp.exp(jnp.minimum(bi - b[s:s + 1], 0.0))
            colv = jnp.sum(qi * kk[s:s + 1] * e, axis=-1, keepdims=True)
            att = jnp.where((lane == s) & (row >= j), colv, att)
        att_rows.append(att)
    att = jnp.concatenate(att_rows, axis=0)
    vb = v.astype(BF16)
    o = o + jnp.dot(att.astype(BF16), vb, preferred_element_type=F32)

    b_last = b[c - 1:c]
    kd = (kk * jnp.exp(b_last - b)).astype(BF16)
    st_new = st * jnp.exp(b_last) + lax.dot_general(vb, kd, (((0,), (0,)), ((), ())),
                                                    preferred_element_type=F32)
    return o, st_new


def _tril_ones(c):
    return (lax.broadcasted_iota(jnp.int32, (c, c), 0) >= lax.broadcasted_iota(jnp.int32, (c, c), 1)
            ).astype(BF16)


def _hgrn_kernel(layer, q_ref, f_ref, i_ref, g_ref, lb_ref, ng_ref, z_ref, s_ref, st_ref, qh_ref, kh_ref, el_ref):
    n_heads, seq, _ = q_ref.shape
    lb = _lower_bound(lb_ref[...], layer)
    norm_g = ng_ref[...]
    st_ref[...] = jnp.zeros_like(st_ref)

    def fast():
        _hgrn_fast_prepare(q_ref, f_ref, lb, qh_ref, kh_ref, el_ref)
        _hgrn_fast_scan(i_ref, g_ref, norm_g, qh_ref, kh_ref, el_ref, st_ref, z_ref)

    def safe():
        tril = _tril_ones(HGRN_CHUNK)

        def body(ci, carry):
            rows = pl.ds(pl.multiple_of(ci * HGRN_CHUNK, HGRN_CHUNK), HGRN_CHUNK)
            for h in range(n_heads):
                lanes = slice(h * HEAD_DIM, (h + 1) * HEAD_DIM)
                o, st = _hgrn_chunk(q_ref[h, rows, :], f_ref[h, rows, :], i_ref[h, rows, :], lb[:, lanes],
                                    st_ref[h], tril)
                st_ref[h] = st
                z_ref[h, rows, :] = _rms_gate(o, norm_g[:, lanes], g_ref[h, rows, :]).astype(z_ref.dtype)
            return carry

        lax.fori_loop(0, seq // HGRN_CHUNK, body, 0)

    fast_ok = HGRN_CHUNK * -jnp.log(jnp.min(lb)) <= HGRN_SAFE_LOG_RANGE
    lax.cond(fast_ok, fast, safe)
    for h in range(n_heads):
        s_ref[h] = st_ref[h].T


def _prompt_hgrn(proj, lb_raw, norm_g, layer, bsz, seq):
    hps = HGRN_HEADS_PER_STEP
    assert N_HEADS_HGRN % hps == 0 and all(g % hps == 0 for g in (G_QB, G_FB, G_IB, G_GB))

    def cols(g0):
        return pl.BlockSpec((hps, seq, HEAD_DIM), lambda b, h: (g0 // hps + h, b, 0))

    return pl.pallas_call(
        functools.partial(_hgrn_kernel, layer),
        grid=(bsz, N_HEADS_HGRN // hps),
        in_specs=[cols(G_QB), cols(G_FB), cols(G_IB), cols(G_GB),
                  pl.BlockSpec((DEPTH + 1, hps * HEAD_DIM), lambda b, h: (0, h)),
                  pl.BlockSpec((1, hps * HEAD_DIM), lambda b, h: (0, h))],
        out_specs=[pl.BlockSpec((hps, seq, HEAD_DIM), lambda b, h: (h, b, 0)),
                   pl.BlockSpec((None, hps, HEAD_DIM, HEAD_DIM), lambda b, h: (b, h, 0, 0))],
        out_shape=[jax.ShapeDtypeStruct((N_HEADS_HGRN, bsz * seq, HEAD_DIM), BF16),
                   jax.ShapeDtypeStruct((bsz, N_HEADS_HGRN, HEAD_DIM, HEAD_DIM), F32)],
        scratch_shapes=[pltpu.VMEM((hps, HEAD_DIM, HEAD_DIM), F32),
                        pltpu.VMEM((seq, hps * HEAD_DIM), BF16),
                        pltpu.VMEM((seq, hps * HEAD_DIM), BF16),
                        pltpu.VMEM((seq // HGRN_CHUNK, hps * HEAD_DIM), F32)],
        compiler_params=_params(("parallel", "parallel")),
        name="prompt_hgrn",
    )(proj, proj, proj, proj, lb_raw, norm_g)


MEM_ROWS = 512


def _mem_kernel(q_ref, g_ref, mem_ref, w_ref, z_ref, mk_ref, mv_ref):
    seq = q_ref.shape[1]
    kv = jnp.dot(mem_ref[...].astype(BF16), w_ref[...], preferred_element_type=F32)
    for h in range(N_HEADS_MEM):
        mk = kv[:, h * HEAD_DIM:(h + 1) * HEAD_DIM]
        mv = kv[:, W_MEM + h * HEAD_DIM:W_MEM + (h + 1) * HEAD_DIM]
        mk_ref[:, h, :] = mk
        mv_ref[:, h, :] = mv
        mk, mv = mk.astype(BF16), mv.astype(BF16)
        for n in range(seq // MEM_ROWS):
            rows = pl.ds(n * MEM_ROWS, MEM_ROWS)
            s = _dot_nt((q_ref[h, rows, :] * ATTN_SCALE).astype(BF16), mk)
            p = jnp.exp(s - jnp.max(s, axis=-1, keepdims=True))
            den = jnp.sum(p, axis=-1, keepdims=True)
            o = jnp.dot(p.astype(BF16), mv, preferred_element_type=F32) / den
            z_ref[h, rows, :] = (o * _silu(g_ref[h, rows, :])).astype(z_ref.dtype)


def _prompt_mem(proj, mem, w_bf16, bsz, seq):
    assert G_QM % N_HEADS_MEM == 0 and G_GM % N_HEADS_MEM == 0

    def cols(g0):
        return pl.BlockSpec((N_HEADS_MEM, seq, HEAD_DIM), lambda b: (g0 // N_HEADS_MEM, b, 0))

    kv_out = pl.BlockSpec((None, N_MEM, N_HEADS_MEM, HEAD_DIM), lambda b: (b, 0, 0, 0))
    kv_shape = jax.ShapeDtypeStruct((bsz, N_MEM, N_HEADS_MEM, HEAD_DIM), F32)
    return pl.pallas_call(
        _mem_kernel,
        grid=(bsz,),
        in_specs=[cols(G_QM), cols(G_GM), pl.BlockSpec((N_MEM, D_MODEL), lambda b: (b, 0)),
                  pl.BlockSpec((D_MODEL, 2 * W_MEM), lambda b: (0, 0))],
        out_specs=[pl.BlockSpec((N_HEADS_MEM, seq, HEAD_DIM), lambda b: (0, b, 0)), kv_out, kv_out],
        out_shape=[jax.ShapeDtypeStruct((N_HEADS_MEM, bsz * seq, HEAD_DIM), BF16), kv_shape, kv_shape],
        compiler_params=_params(("parallel",)),
        name="prompt_mem",
    )(proj, proj, mem, w_bf16)


def _merge_kernel(za_ref, zh_ref, zm_ref, x_ref, w_ref, lg_ref, lb_ref, o_ref, z_ref):
    c0 = 0
    for ref in (za_ref, zh_ref, zm_ref):
        for c in range(ref.shape[0]):
            z_ref[:, (c0 + c) * LANES:(c0 + c + 1) * LANES] = ref[c].astype(BF16)
        c0 += ref.shape[0]
    y = jnp.dot(z_ref[...], w_ref[...], preferred_element_type=F32)
    r = DEEPNORM_ALPHA * x_ref[...] + y
    mu = jnp.mean(r, axis=-1, keepdims=True)
    d = r - mu
    var = jnp.mean(d * d, axis=-1, keepdims=True)
    o_ref[...] = d * lax.rsqrt(var + LN_EPS) * lg_ref[...] + lb_ref[...]


def _merge(za, zh, zm, x, w_out_bf16, ln_g, ln_b, tm):
    m = x.shape[0]

    def slab(a):
        return pl.BlockSpec((a.shape[0], tm, LANES), lambda i: (0, i, 0))

    const = lambda shape: pl.BlockSpec(shape, lambda i: (0, 0))
    return pl.pallas_call(
        _merge_kernel,
        grid=(m // tm,),
        in_specs=[slab(za), slab(zh), slab(zm), pl.BlockSpec((tm, D_MODEL), lambda i: (i, 0)),
                  const((MIX_WIDTH, D_MODEL)), const((1, D_MODEL)), const((1, D_MODEL))],
        out_specs=pl.BlockSpec((tm, D_MODEL), lambda i: (i, 0)),
        out_shape=jax.ShapeDtypeStruct((m, D_MODEL), F32),
        scratch_shapes=[pltpu.VMEM((tm, MIX_WIDTH), BF16)],
        compiler_params=_params(("parallel",)),
        name="merge",
    )(za, zh, zm, x, w_out_bf16, ln_g, ln_b)


def _column(row):
    return jnp.broadcast_to(row, (HEAD_DIM, HEAD_DIM)).T


HBM_TILE_ROWS = 8


def _window_pieces(past):
    pieces = []
    for dil in DILATIONS:
        if dil == 1:
            pieces.append((0, past - BAND, BAND, None))
        elif dil < HBM_TILE_ROWS:
            n = BAND * dil // HBM_TILE_ROWS
            pieces += [(1, past // HBM_TILE_ROWS - n, n, r) for r in range(0, HBM_TILE_ROWS, dil)]
        else:
            pieces.append((2, past // dil - BAND, BAND, 0))
    return pieces


def _cache_views(cache):
    depth, bsz, past, nh, hd = cache.shape
    assert all(d == 1 or HBM_TILE_ROWS % d == 0 or d % HBM_TILE_ROWS == 0 for d in DILATIONS)
    big = max(DILATIONS)
    hm = cache.transpose(0, 1, 3, 2, 4)
    return (hm, hm.reshape(depth, bsz, nh, past // HBM_TILE_ROWS, HBM_TILE_ROWS, hd),
            hm.reshape(depth, bsz, nh, past // big, big, hd))


def _sample_kernel(layer, past, p_ref, cos_ref, sin_ref, k0_hbm, k1_hbm, k2_hbm, v0_hbm, v1_hbm, v2_hbm,
                   st_ref, mk_ref, mv_ref, lb_ref, ng_ref, z_ref, ko_ref, vo_ref, so_ref, kbuf, vbuf, sem):
    b = pl.program_id(0)
    slot = b % 2
    pieces = _window_pieces(past)

    def window_copies(row, sl):
        out = []
        for ci, (views, buf) in enumerate((((k0_hbm, k1_hbm, k2_hbm), kbuf), ((v0_hbm, v1_hbm, v2_hbm), vbuf))):
            off = 0
            for pi, (vi, start, count, res) in enumerate(pieces):
                view = views[vi]
                src = (view.at[layer, row, :, pl.ds(start, count), :] if res is None
                       else view.at[layer, row, :, pl.ds(start, count), res, :])
                out.append(pltpu.make_async_copy(src, buf.at[sl, :, pl.ds(off, count), :], sem.at[sl, ci, pi]))
                off += count
        return out

    @pl.when(b == 0)
    def _():
        for c in window_copies(b, slot):
            c.start()

    @pl.when(b + 1 < pl.num_programs(0))
    def _():
        for c in window_copies(b + 1, 1 - slot):
            c.start()

    cos = cos_ref[...]
    sin = sin_ref[...]
    q_all = _rope(p_ref[G_QA:G_QA + N_HEADS_ATTN, :], cos, sin) * ATTN_SCALE
    k_all = _rope(p_ref[G_KA:G_KA + N_HEADS_ATTN, :], cos, sin)
    v_all = p_ref[G_VA:G_VA + N_HEADS_ATTN, :]
    ko_ref[...] = k_all
    vo_ref[...] = v_all

    for c in window_copies(b, slot):
        c.wait()

    for h in range(N_HEADS_ATTN):
        q = q_all[h:h + 1]
        s_new = jnp.sum(q * k_all[h:h + 1], axis=-1, keepdims=True)
        s = jnp.sum(kbuf[slot, h] * q, axis=-1, keepdims=True)
        m = jnp.maximum(jnp.max(s, axis=0, keepdims=True), s_new)
        p = jnp.exp(s - m)
        p_new = jnp.exp(s_new - m) * len(DILATIONS)
        den = jnp.sum(p, axis=0, keepdims=True) + p_new
        num = jnp.sum(p * vbuf[slot, h], axis=0, keepdims=True) + p_new * v_all[h:h + 1]
        z_ref[h:h + 1, :] = (num / den) * _silu(p_ref[G_GA + h:G_GA + h + 1, :])

    lb_all = lb_ref[...]
    for h in range(N_HEADS_HGRN):
        lanes = slice(h * HEAD_DIM, (h + 1) * HEAD_DIM)
        lb = _lower_bound(lb_all[:, lanes], layer)
        f = lb + (1.0 - lb) * _sigmoid(p_ref[G_FB + h:G_FB + h + 1, :])
        f_col = _column(f)
        q_col = _column(p_ref[G_QB + h:G_QB + h + 1, :])
        s_new = f_col * st_ref[h] + (1.0 - f_col) * p_ref[G_IB + h:G_IB + h + 1, :]
        so_ref[h] = s_new
        o = jnp.sum(s_new * q_col, axis=0, keepdims=True)
        z_ref[N_HEADS_ATTN + h:N_HEADS_ATTN + h + 1, :] = _rms_gate(
            o, ng_ref[:, lanes], p_ref[G_GB + h:G_GB + h + 1, :])

    q_m = p_ref[G_QM:G_QM + N_HEADS_MEM, :] * ATTN_SCALE
    s = jnp.sum(mk_ref[...] * q_m[None], axis=-1, keepdims=True)
    p = jnp.exp(s - jnp.max(s, axis=0)[None])
    o = jnp.sum(p * mv_ref[...], axis=0) / jnp.sum(p, axis=0)
    row = N_HEADS_ATTN + N_HEADS_HGRN
    z_ref[row:row + N_HEADS_MEM, :] = o * _silu(p_ref[G_GM:G_GM + N_HEADS_MEM, :])


def _sample_mixers(proj_rows, cos, sin, win_k, win_v, state, mem_k, mem_v, lb_raw, norm_g, layer):
    bsz, past = win_k.shape[1:3]
    assert all(past % d == 0 and past >= w for w, d in zip(WINDOWS, DILATIONS))
    assert all(w // d == BAND for w, d in zip(WINDOWS, DILATIONS)) and past % HBM_TILE_ROWS == 0
    n_keys = BAND * len(DILATIONS)
    n_pieces = len(_window_pieces(past))
    any_spec = pl.BlockSpec(memory_space=pl.ANY)
    vec = lambda n: pl.BlockSpec((n, W_HGRN), lambda b: (0, 0))
    heads = lambda n: pl.BlockSpec((None, n, HEAD_DIM), lambda b: (b, 0, 0))
    mem_spec = pl.BlockSpec((None, None, N_MEM, N_HEADS_MEM, HEAD_DIM), lambda b: (layer, b, 0, 0, 0))
    return pl.pallas_call(
        functools.partial(_sample_kernel, layer, past),
        grid=(bsz,),
        in_specs=[heads(N_GROUPS), pl.BlockSpec((1, HEAD_DIM), lambda b: (0, 0)),
                  pl.BlockSpec((1, HEAD_DIM), lambda b: (0, 0)), *([any_spec] * 6),
                  pl.BlockSpec((None, None, N_HEADS_HGRN, HEAD_DIM, HEAD_DIM), lambda b: (layer, b, 0, 0, 0)),
                  mem_spec, mem_spec, vec(DEPTH + 1), vec(1)],
        out_specs=[heads(MIX_WIDTH // HEAD_DIM), heads(N_HEADS_ATTN), heads(N_HEADS_ATTN),
                   pl.BlockSpec((None, N_HEADS_HGRN, HEAD_DIM, HEAD_DIM), lambda b: (b, 0, 0, 0))],
        out_shape=[jax.ShapeDtypeStruct((bsz, MIX_WIDTH // HEAD_DIM, HEAD_DIM), F32),
                   jax.ShapeDtypeStruct((bsz, N_HEADS_ATTN, HEAD_DIM), F32),
                   jax.ShapeDtypeStruct((bsz, N_HEADS_ATTN, HEAD_DIM), F32),
                   jax.ShapeDtypeStruct((bsz, N_HEADS_HGRN, HEAD_DIM, HEAD_DIM), F32)],
        scratch_shapes=[pltpu.VMEM((2, N_HEADS_ATTN, n_keys, HEAD_DIM), F32),
                        pltpu.VMEM((2, N_HEADS_ATTN, n_keys, HEAD_DIM), F32),
                        pltpu.SemaphoreType.DMA((2, 2, n_pieces))],
        compiler_params=_params(("arbitrary",)),
        name="sample_mixers",
    )(proj_rows, cos, sin, *_cache_views(win_k), *_cache_views(win_v), state, mem_k, mem_v, lb_raw, norm_g)


def _rope_tables(pos):
    half = HEAD_DIM // 2
    inv_freq = 1.0 / (ROPE_THETA ** (jnp.arange(half, dtype=F32) / half))
    ang = pos.astype(F32)[:, None] * inv_freq[None, :]
    cos, sin = jnp.cos(ang), jnp.sin(ang)
    return jnp.concatenate([cos, cos], axis=-1), jnp.concatenate([-sin, sin], axis=-1)


def kernel(x_prompt, x_sample, cache_win_k, cache_win_v, state_hgrn, cache_mem_k, cache_mem_v, mem_prompt,
           w_in, w_mem_kv, hgrn_lb_raw, hgrn_norm_g, w_out, ln_g, ln_b):
    bsz, seq, _ = x_prompt.shape
    dbsz, n_new, _ = x_sample.shape
    assert n_new == 1 and seq % (BAND * max(DILATIONS)) == 0
    cos_p, sin_p = _rope_tables(jnp.arange(seq))
    cos_s, sin_s = _rope_tables(PAST_LEN + jnp.arange(n_new))

    hp = x_prompt.reshape(bsz * seq, D_MODEL)
    hs = x_sample.reshape(dbsz * n_new, D_MODEL)
    mem = mem_prompt.reshape(bsz * N_MEM, D_MODEL)
    outs = [[] for _ in range(8)]
    for layer in range(DEPTH):
        w_in_b = w_in[layer].astype(BF16)
        w_out_b = w_out[layer].astype(BF16)
        lb_raw = hgrn_lb_raw
        norm_g = hgrn_norm_g[layer][None]
        lg, lbias = ln_g[layer][None], ln_b[layer][None]

        proj = _project(hp, w_in_b, 1024, 1024)
        za, k1, v1 = _prompt_attention(proj, cos_p, sin_p, bsz, seq)
        zh, s1 = _prompt_hgrn(proj, lb_raw, norm_g, layer, bsz, seq)
        zm, mk1, mv1 = _prompt_mem(proj, mem, w_mem_kv[layer].astype(BF16), bsz, seq)
        hp = _merge(za, zh, zm, hp, w_out_b, lg, lbias, 512)

        proj_s = _project(hs, w_in_b, dbsz, 1024).transpose(1, 0, 2)
        zs, k2, v2, s2 = _sample_mixers(proj_s, cos_s, sin_s, cache_win_k, cache_win_v, state_hgrn,
                                        cache_mem_k, cache_mem_v, lb_raw, norm_g, layer)
        zs = zs.transpose(1, 0, 2)
        hs = _merge(zs[:N_HEADS_ATTN], zs[N_HEADS_ATTN:N_HEADS_ATTN + N_HEADS_HGRN],
                    zs[N_HEADS_ATTN + N_HEADS_HGRN:], hs, w_out_b, lg, lbias, dbsz)

        new = (k1.transpose(0, 2, 1, 3), v1.transpose(0, 2, 1, 3), s1, mk1, mv1,
               k2.reshape(dbsz, n_new, N_HEADS_ATTN, HEAD_DIM), v2.reshape(dbsz, n_new, N_HEADS_ATTN, HEAD_DIM),
               s2.astype(state_hgrn.dtype))
        for acc, val in zip(outs, new):
            acc.append(val)

    return (hp.reshape(bsz, seq, D_MODEL), hs.reshape(dbsz, n_new, D_MODEL), *[jnp.stack(o) for o in outs])
```

```python
import functools

import jax
import jax.numpy as jnp
import numpy as np
from jax import lax
from jax.experimental import pallas as pl
from jax.experimental.pallas import tpu as pltpu

F32 = jnp.float32
BF16 = jnp.bfloat16

D_MODEL = 2048
DEPTH = 1
PAST_LEN = 8192
HEAD_DIM = 128
N_HEADS_ATTN = 6
N_HEADS_HGRN = 6
N_HEADS_MEM = 4
W_ATTN = N_HEADS_ATTN * HEAD_DIM
W_HGRN = N_HEADS_HGRN * HEAD_DIM
W_MEM = N_HEADS_MEM * HEAD_DIM
MIX_WIDTH = W_ATTN + W_HGRN + W_MEM
WINDOWS = (128, 512, 2048)
DILATIONS = (1, 4, 16)
N_MEM = 256
ROPE_THETA = 10000.0
LN_EPS = 1e-5
RMS_EPS = 1e-6
NEG_INF = -1e30
DEEPNORM_ALPHA = (2 * DEPTH) ** 0.25
ATTN_SCALE = HEAD_DIM ** -0.5
ATTN_SCALE_LOG2 = ATTN_SCALE * 1.4426950408889634

G_QA, G_KA, G_VA, G_GA = 0, 6, 12, 18
G_QB, G_FB, G_IB, G_GB = 24, 30, 36, 42
G_QM, G_GM = 48, 52
N_GROUPS = 56

LANES = 128
BAND = 128
ATTN_GROUP = 16
HGRN_CHUNK = 64
HGRN_SUB = 16
HGRN_HEADS_PER_STEP = 3
HGRN_PREP_ROWS = 256
HGRN_SCAN_CHUNKS = 4
HGRN_SAFE_LOG_RANGE = 70.0
VMEM_LIMIT = 48 * 1024 * 1024


def _params(sem, vmem=VMEM_LIMIT):
    return pltpu.CompilerParams(dimension_semantics=sem, vmem_limit_bytes=vmem)


def _sigmoid(x):
    return 1.0 / (1.0 + jnp.exp(-x))


def _silu(x):
    return x * _sigmoid(x)


def _dot_nt(a, b):
    return lax.dot_general(a, b, (((1,), (1,)), ((), ())), preferred_element_type=F32)


def _proj_kernel(x_ref, w_ref, o_ref, xb_ref):
    @pl.when(pl.program_id(1) == 0)
    def _():
        xb_ref[...] = x_ref[...].astype(BF16)

    acc = jnp.dot(xb_ref[...], w_ref[...], preferred_element_type=F32)
    for j in range(o_ref.shape[0]):
        o_ref[j] = acc[:, j * LANES:(j + 1) * LANES]


def _project(x, w_bf16, tm, tn):
    m, k = x.shape
    n = w_bf16.shape[1]
    return pl.pallas_call(
        _proj_kernel,
        grid=(m // tm, n // tn),
        in_specs=[pl.BlockSpec((tm, k), lambda i, j: (i, 0)),
                  pl.BlockSpec((k, tn), lambda i, j: (0, j))],
        out_specs=pl.BlockSpec((tn // LANES, tm, LANES), lambda i, j: (j, i, 0)),
        out_shape=jax.ShapeDtypeStruct((n // LANES, m, LANES), F32),
        scratch_shapes=[pltpu.VMEM((tm, k), BF16)],
        compiler_params=_params(("parallel", "arbitrary")),
        name="in_proj",
    )(x, w_bf16)


def _rope(x, cos, sin_signed):
    return x * cos + pltpu.roll(x, HEAD_DIM // 2, 1) * sin_signed


def _bias_from_count(count):
    return jnp.where(count > 1.5, 1.0, jnp.where(count > 0.5, 0.0, NEG_INF)).astype(F32)


def _block_deltas(block_gap):
    qi = lax.broadcasted_iota(jnp.int32, (BAND, BAND), 0)
    ki = lax.broadcasted_iota(jnp.int32, (BAND, BAND), 1)
    return block_gap * BAND + qi - ki


def _near_bias(block_gap):
    d = _block_deltas(block_gap)
    return _bias_from_count(((d >= 0) & (d <= BAND)).astype(F32))


def _class_bias(block_gap, far_step):
    d = _block_deltas(block_gap)
    near = (d >= 0) & (d <= BAND)
    far = (d >= 0) & ((d & (far_step - 1)) == 0)
    return _bias_from_count(near.astype(F32) + far.astype(F32))


def _softmax_stage(scores, floors=None):
    out = []
    for idx, s in enumerate(scores):
        m = jnp.broadcast_to(jnp.max(s, axis=-1, keepdims=True), (BAND, HEAD_DIM))
        if floors is not None:
            m = jnp.maximum(m, floors[idx])
        m_wide = jnp.concatenate([m] * (s.shape[1] // HEAD_DIM), axis=1)
        out.append((m, jnp.exp2(s - m_wide).astype(BF16)))
    return out


def _attn_kernel(q_ref, k_ref, v_ref, g_ref, cos_ref, sin_ref, z_ref, ko_ref, vo_ref,
                 qs_ref, qb_ref, kb_ref, vb_ref, qc_ref, kc_ref, vc_ref, acc_ref, m_ref, l_ref):
    seq = q_ref.shape[0]
    d_mid, d_far = DILATIONS[1], DILATIONS[2]
    far_step = d_far // d_mid
    cls_rows = seq // d_mid
    cos = cos_ref[...]
    sin = sin_ref[...]
    qs_ref[...] = _rope(q_ref[...], cos, sin) * ATTN_SCALE_LOG2
    ko_ref[...] = _rope(k_ref[...], cos, sin)
    vo_ref[...] = v_ref[...]
    qb_ref[...] = qs_ref[...].astype(BF16)
    kb_ref[...] = ko_ref[...].astype(BF16)
    vb_ref[...] = v_ref[...].astype(BF16)
    for r in range(d_mid):
        cls, dst = pl.ds(r, cls_rows, stride=d_mid), pl.ds(r * cls_rows, cls_rows)
        qc_ref[dst, :] = qs_ref[cls, :].astype(BF16)
        kc_ref[dst, :] = ko_ref[cls, :].astype(BF16)
        vc_ref[dst, :] = v_ref[cls, :].astype(BF16)
    ones = jnp.ones((cls_rows, HEAD_DIM), BF16)

    n_cls_blk = cls_rows // BAND
    cls_bias = [_class_bias(gap, far_step) for gap in range(min(n_cls_blk, 3))]
    items = [(r, n) for r in range(d_mid) for n in range(n_cls_blk)]
    for g0 in range(0, len(items), ATTN_GROUP):
        group = items[g0:g0 + ATTN_GROUP]
        keys = [pl.ds(r * cls_rows, (n + 1) * BAND) for r, n in group]
        scores = [_dot_nt(qc_ref[pl.ds(r * cls_rows + n * BAND, BAND), :], kc_ref[kr, :])
                  + jnp.concatenate([cls_bias[min(n - nk, 2)] for nk in range(n + 1)], axis=1)
                  for (r, n), kr in zip(group, keys)]
        for (r, n), kr, (m, p) in zip(group, keys, _softmax_stage(scores)):
            rows = pl.ds(r + d_mid * BAND * n, BAND, stride=d_mid)
            acc_ref[rows, :] = jnp.dot(p, vc_ref[kr, :], preferred_element_type=F32)
            l_ref[rows, :] = jnp.dot(p, ones[:p.shape[1]], preferred_element_type=F32)
            m_ref[rows, :] = m

    near_bias = [_near_bias(0), jnp.concatenate([_near_bias(1), _near_bias(0)], axis=1)]
    items = list(range(seq // BAND))
    for g0 in range(0, len(items), ATTN_GROUP):
        group = items[g0:g0 + ATTN_GROUP]
        rows = [pl.ds(n * BAND, BAND) for n in group]
        keys = [pl.ds(max(n - 1, 0) * BAND, BAND * min(n + 1, 2)) for n in group]
        scores = [_dot_nt(qb_ref[rw, :], kb_ref[kr, :]) + near_bias[min(n, 1)]
                  for n, rw, kr in zip(group, rows, keys)]
        probs = _softmax_stage(scores, floors=[m_ref[rw, :] for rw in rows])
        for rw, kr, (m, p) in zip(rows, keys, probs):
            w = jnp.exp2(m_ref[rw, :] - m)
            num = jnp.dot(p, vb_ref[kr, :], preferred_element_type=F32) + w * acc_ref[rw, :]
            den = jnp.dot(p, ones[:p.shape[1]], preferred_element_type=F32) + w * l_ref[rw, :]
            z_ref[rw, :] = ((num / den) * _silu(g_ref[rw, :])).astype(z_ref.dtype)


def _prompt_attention(proj, cos, sin, bsz, seq):
    def col(g0):
        return pl.BlockSpec((None, seq, HEAD_DIM), lambda b, h: (g0 + h, b, 0))

    table = pl.BlockSpec((seq, HEAD_DIM), lambda b, h: (0, 0))
    kv_out = pl.BlockSpec((None, None, seq, HEAD_DIM), lambda b, h: (b, h, 0, 0))
    d_near, d_mid, d_far = DILATIONS
    far_step = d_far // d_mid
    assert d_near == 1 and d_far % d_mid == 0 and far_step & (far_step - 1) == 0
    assert all(w // d == BAND for w, d in zip(WINDOWS, DILATIONS)) and seq % (d_mid * BAND) == 0
    assert seq // d_mid <= far_step * BAND
    return pl.pallas_call(
        _attn_kernel,
        grid=(bsz, N_HEADS_ATTN),
        in_specs=[col(G_QA), col(G_KA), col(G_VA), col(G_GA), table, table],
        out_specs=[pl.BlockSpec((None, seq, HEAD_DIM), lambda b, h: (h, b, 0)), kv_out, kv_out],
        out_shape=[jax.ShapeDtypeStruct((N_HEADS_ATTN, bsz * seq, HEAD_DIM), BF16),
                   jax.ShapeDtypeStruct((bsz, N_HEADS_ATTN, seq, HEAD_DIM), F32),
                   jax.ShapeDtypeStruct((bsz, N_HEADS_ATTN, seq, HEAD_DIM), F32)],
        scratch_shapes=[pltpu.VMEM((seq, HEAD_DIM), F32)] + [pltpu.VMEM((seq, HEAD_DIM), BF16)] * 6
                       + [pltpu.VMEM((seq, HEAD_DIM), F32)] * 3,
        compiler_params=_params(("parallel", "parallel")),
        name="prompt_attn",
    )(proj, proj, proj, proj, cos, sin)


def _lower_bound(lb_raw, layer):
    e = jnp.exp(lb_raw - jnp.max(lb_raw, axis=0, keepdims=True))
    sm = e / jnp.sum(e, axis=0, keepdims=True)
    return jnp.sum(sm[:layer + 1], axis=0, keepdims=True)


def _split3(x):
    hi = x.astype(BF16)
    r1 = x - hi.astype(F32)
    mid = r1.astype(BF16)
    lo = (r1 - mid.astype(F32)).astype(BF16)
    return hi, mid, lo


def _rms_gate(o, norm_g, gate):
    o = o * lax.rsqrt(jnp.mean(o * o, axis=-1, keepdims=True) + RMS_EPS)
    return o * norm_g * _silu(gate)


def _hgrn_gates(fb, lb, tril):
    f = lb + (1.0 - lb) * _sigmoid(fb)
    g = jnp.log(f)
    b = sum(jnp.dot(tril, piece, preferred_element_type=F32) for piece in _split3(g))
    return 1.0 - f, b


def _hgrn_fast_prepare(q_ref, f_ref, lb, qh_ref, kh_ref, el_ref):
    n_heads, seq, _ = q_ref.shape
    blk = HGRN_PREP_ROWS
    tril = _tril_ones(blk)

    def body(bi, carry):
        r0 = pl.multiple_of(bi * blk, blk)
        fb = jnp.concatenate([f_ref[h, pl.ds(r0, blk), :] for h in range(n_heads)], axis=1)
        f = lb + (1.0 - lb) * _sigmoid(fb)
        b_blk = sum(jnp.dot(tril, piece, preferred_element_type=F32) for piece in _split3(jnp.log(f)))
        kk = 1.0 - f
        for c0 in range(0, blk, HGRN_CHUNK):
            rs = slice(c0, c0 + HGRN_CHUNK)
            b = b_blk[rs] - b_blk[c0 - 1:c0] if c0 else b_blk[rs]
            rows = pl.ds(r0 + c0, HGRN_CHUNK)
            q = jnp.concatenate([q_ref[h, rows, :] for h in range(n_heads)], axis=1)
            qh_ref[rows, :] = (q * jnp.exp(b)).astype(BF16)
            kh_ref[rows, :] = (kk[rs] * jnp.exp(-b)).astype(BF16)
            el_ref[pl.ds(bi * (blk // HGRN_CHUNK) + c0 // HGRN_CHUNK, 1), :] = jnp.exp(b[HGRN_CHUNK - 1:])
        return carry

    lax.fori_loop(0, seq // blk, body, 0)


def _hgrn_fast_scan(i_ref, g_ref, norm_g, qh_ref, kh_ref, el_ref, st_ref, z_ref):
    n_heads, seq, _ = i_ref.shape
    c = HGRN_CHUNK
    heads = range(n_heads)
    chunks = range(HGRN_SCAN_CHUNKS)
    lanes = [slice(h * HEAD_DIM, (h + 1) * HEAD_DIM) for h in heads]
    causal = lax.broadcasted_iota(jnp.int32, (c, c), 0) >= lax.broadcasted_iota(jnp.int32, (c, c), 1)
    tn = (((0,), (0,)), ((), ()))

    def body(ti, carry):
        rows = [pl.ds(pl.multiple_of((ti * HGRN_SCAN_CHUNKS + k) * c, c), c) for k in chunks]
        qh = [[qh_ref[rows[k], lanes[h]] for h in heads] for k in chunks]
        kh = [[kh_ref[rows[k], lanes[h]] for h in heads] for k in chunks]
        vb = [[i_ref[h, rows[k], :].astype(BF16) for h in heads] for k in chunks]
        att = [[jnp.where(causal, _dot_nt(qh[k][h], kh[k][h]), 0.0).astype(BF16) for h in heads] for k in chunks]
        ds = [[lax.dot_general(vb[k][h], kh[k][h], tn, preferred_element_type=F32) for h in heads] for k in chunks]
        st = [st_ref[h] for h in heads]
        o = []
        for k in chunks:
            el = el_ref[pl.ds(ti * HGRN_SCAN_CHUNKS + k, 1), :]
            o.append([_dot_nt(qh[k][h], st[h].astype(BF16)) for h in heads])
            st = [(st[h] + ds[k][h]) * el[:, lanes[h]] for h in heads]
        for h in heads:
            st_ref[h] = st[h]
        for k in chunks:
            for h in heads:
                o_kh = o[k][h] + jnp.dot(att[k][h], vb[k][h], preferred_element_type=F32)
                z_ref[h, rows[k], :] = _rms_gate(o_kh, norm_g[:, lanes[h]], g_ref[h, rows[k], :]).astype(z_ref.dtype)
        return carry

    lax.fori_loop(0, seq // (c * HGRN_SCAN_CHUNKS), body, 0)


def _hgrn_chunk(q, fb, v, lb, st, tril):
    c = q.shape[0]
    kk, b = _hgrn_gates(fb, lb, tril)
    o = _dot_nt((q * jnp.exp(b)).astype(BF16), st.astype(BF16))

    s_idx = lax.broadcasted_iota(jnp.int32, (c, 1), 0)
    lane = lax.broadcasted_iota(jnp.int32, (HGRN_SUB, c), 1)
    row = lax.broadcasted_iota(jnp.int32, (HGRN_SUB, c), 0)
    att_rows = []
    for i0 in range(0, c, HGRN_SUB):
        qi = q[i0:i0 + HGRN_SUB]
        bi = b[i0:i0 + HGRN_SUB]
        if i0 > 0:
            bref = b[i0 - 1:i0]
            kt = jnp.where(s_idx < i0, kk * jnp.exp(jnp.minimum(bref - b, 0.0)), 0.0)
            att = _dot_nt((qi * jnp.exp(bi - bref)).astype(BF16), kt.astype(BF16))
        else:
            att = jnp.zeros((HGRN_SUB, c), F32)
        for j in range(HGRN_SUB):
            s = i0 + j
            e = jnp.exp(jnp.minimum(bi - b[s:s + 1], 0.0))
            colv = jnp.sum(qi * kk[s:s + 1] * e, axis=-1, keepdims=True)
            att = jnp.where((lane == s) & (row >= j), colv, att)
        att_rows.append(att)
    att = jnp.concatenate(att_rows, axis=0)
    vb = v.astype(BF16)
    o = o + jnp.dot(att.astype(BF16), vb, preferred_element_type=F32)

    b_last = b[c - 1:c]
    kd = (kk * jnp.exp(b_last - b)).astype(BF16)
    st_new = st * jnp.exp(b_last) + lax.dot_general(vb, kd, (((0,), (0,)), ((), ())),
                                                    preferred_element_type=F32)
    return o, st_new


def _tril_ones(c):
    return (lax.broadcasted_iota(jnp.int32, (c, c), 0) >= lax.broadcasted_iota(jnp.int32, (c, c), 1)
            ).astype(BF16)


def _hgrn_kernel(layer, q_ref, f_ref, i_ref, g_ref, lb_ref, ng_ref, z_ref, s_ref, st_ref, qh_ref, kh_ref, el_ref):
    n_heads, seq, _ = q_ref.shape
    lb = _lower_bound(lb_ref[...], layer)
    norm_g = ng_ref[...]
    st_ref[...] = jnp.zeros_like(st_ref)

    def fast():
        _hgrn_fast_prepare(q_ref, f_ref, lb, qh_ref, kh_ref, el_ref)
        _hgrn_fast_scan(i_ref, g_ref, norm_g, qh_ref, kh_ref, el_ref, st_ref, z_ref)

    def safe():
        tril = _tril_ones(HGRN_CHUNK)

        def body(ci, carry):
            rows = pl.ds(pl.multiple_of(ci * HGRN_CHUNK, HGRN_CHUNK), HGRN_CHUNK)
            for h in range(n_heads):
                lanes = slice(h * HEAD_DIM, (h + 1) * HEAD_DIM)
                o, st = _hgrn_chunk(q_ref[h, rows, :], f_ref[h, rows, :], i_ref[h, rows, :], lb[:, lanes],
                                    st_ref[h], tril)
                st_ref[h] = st
                z_ref[h, rows, :] = _rms_gate(o, norm_g[:, lanes], g_ref[h, rows, :]).astype(z_ref.dtype)
            return carry

        lax.fori_loop(0, seq // HGRN_CHUNK, body, 0)

    fast_ok = HGRN_CHUNK * -jnp.log(jnp.min(lb)) <= HGRN_SAFE_LOG_RANGE
    lax.cond(fast_ok, fast, safe)
    for h in range(n_heads):
        s_ref[h] = st_ref[h].T


def _prompt_hgrn(proj, lb_raw, norm_g, layer, bsz, seq):
    hps = HGRN_HEADS_PER_STEP
    assert N_HEADS_HGRN % hps == 0 and all(g % hps == 0 for g in (G_QB, G_FB, G_IB, G_GB))

    def cols(g0):
        return pl.BlockSpec((hps, seq, HEAD_DIM), lambda b, h: (g0 // hps + h, b, 0))

    return pl.pallas_call(
        functools.partial(_hgrn_kernel, layer),
        grid=(bsz, N_HEADS_HGRN // hps),
        in_specs=[cols(G_QB), cols(G_FB), cols(G_IB), cols(G_GB),
                  pl.BlockSpec((DEPTH + 1, hps * HEAD_DIM), lambda b, h: (0, h)),
                  pl.BlockSpec((1, hps * HEAD_DIM), lambda b, h: (0, h))],
        out_specs=[pl.BlockSpec((hps, seq, HEAD_DIM), lambda b, h: (h, b, 0)),
                   pl.BlockSpec((None, hps, HEAD_DIM, HEAD_DIM), lambda b, h: (b, h, 0, 0))],
        out_shape=[jax.ShapeDtypeStruct((N_HEADS_HGRN, bsz * seq, HEAD_DIM), BF16),
                   jax.ShapeDtypeStruct((bsz, N_HEADS_HGRN, HEAD_DIM, HEAD_DIM), F32)],
        scratch_shapes=[pltpu.VMEM((hps, HEAD_DIM, HEAD_DIM), F32),
                        pltpu.VMEM((seq, hps * HEAD_DIM), BF16),
                        pltpu.VMEM((seq, hps * HEAD_DIM), BF16),
                        pltpu.VMEM((seq // HGRN_CHUNK, hps * HEAD_DIM), F32)],
        compiler_params=_params(("parallel", "parallel")),
        name="prompt_hgrn",
    )(proj, proj, proj, proj, lb_raw, norm_g)


MEM_ROWS = 512


def _mem_kernel(q_ref, g_ref, mem_ref, w_ref, z_ref, mk_ref, mv_ref):
    seq = q_ref.shape[1]
    kv = jnp.dot(mem_ref[...].astype(BF16), w_ref[...], preferred_element_type=F32)
    for h in range(N_HEADS_MEM):
        mk = kv[:, h * HEAD_DIM:(h + 1) * HEAD_DIM]
        mv = kv[:, W_MEM + h * HEAD_DIM:W_MEM + (h + 1) * HEAD_DIM]
        mk_ref[:, h, :] = mk
        mv_ref[:, h, :] = mv
        mk, mv = mk.astype(BF16), mv.astype(BF16)
        for n in range(seq // MEM_ROWS):
            rows = pl.ds(n * MEM_ROWS, MEM_ROWS)
            s = _dot_nt((q_ref[h, rows, :] * ATTN_SCALE).astype(BF16), mk)
            p = jnp.exp(s - jnp.max(s, axis=-1, keepdims=True))
            den = jnp.sum(p, axis=-1, keepdims=True)
            o = jnp.dot(p.astype(BF16), mv, preferred_element_type=F32) / den
            z_ref[h, rows, :] = (o * _silu(g_ref[h, rows, :])).astype(z_ref.dtype)


def _prompt_mem(proj, mem, w_bf16, bsz, seq):
    assert G_QM % N_HEADS_MEM == 0 and G_GM % N_HEADS_MEM == 0

    def cols(g0):
        return pl.BlockSpec((N_HEADS_MEM, seq, HEAD_DIM), lambda b: (g0 // N_HEADS_MEM, b, 0))

    kv_out = pl.BlockSpec((None, N_MEM, N_HEADS_MEM, HEAD_DIM), lambda b: (b, 0, 0, 0))
    kv_shape = jax.ShapeDtypeStruct((bsz, N_MEM, N_HEADS_MEM, HEAD_DIM), F32)
    return pl.pallas_call(
        _mem_kernel,
        grid=(bsz,),
        in_specs=[cols(G_QM), cols(G_GM), pl.BlockSpec((N_MEM, D_MODEL), lambda b: (b, 0)),
                  pl.BlockSpec((D_MODEL, 2 * W_MEM), lambda b: (0, 0))],
        out_specs=[pl.BlockSpec((N_HEADS_MEM, seq, HEAD_DIM), lambda b: (0, b, 0)), kv_out, kv_out],
        out_shape=[jax.ShapeDtypeStruct((N_HEADS_MEM, bsz * seq, HEAD_DIM), BF16), kv_shape, kv_shape],
        compiler_params=_params(("parallel",)),
        name="prompt_mem",
    )(proj, proj, mem, w_bf16)


def _merge_kernel(za_ref, zh_ref, zm_ref, x_ref, w_ref, lg_ref, lb_ref, o_ref, z_ref):
    c0 = 0
    for ref in (za_ref, zh_ref, zm_ref):
        for c in range(ref.shape[0]):
            z_ref[:, (c0 + c) * LANES:(c0 + c + 1) * LANES] = ref[c].astype(BF16)
        c0 += ref.shape[0]
    y = jnp.dot(z_ref[...], w_ref[...], preferred_element_type=F32)
    r = DEEPNORM_ALPHA * x_ref[...] + y
    mu = jnp.mean(r, axis=-1, keepdims=True)
    d = r - mu
    var = jnp.mean(d * d, axis=-1, keepdims=True)
    o_ref[...] = d * lax.rsqrt(var + LN_EPS) * lg_ref[...] + lb_ref[...]


def _merge(za, zh, zm, x, w_out_bf16, ln_g, ln_b, tm):
    m = x.shape[0]

    def slab(a):
        return pl.BlockSpec((a.shape[0], tm, LANES), lambda i: (0, i, 0))

    const = lambda shape: pl.BlockSpec(shape, lambda i: (0, 0))
    return pl.pallas_call(
        _merge_kernel,
        grid=(m // tm,),
        in_specs=[slab(za), slab(zh), slab(zm), pl.BlockSpec((tm, D_MODEL), lambda i: (i, 0)),
                  const((MIX_WIDTH, D_MODEL)), const((1, D_MODEL)), const((1, D_MODEL))],
        out_specs=pl.BlockSpec((tm, D_MODEL), lambda i: (i, 0)),
        out_shape=jax.ShapeDtypeStruct((m, D_MODEL), F32),
        scratch_shapes=[pltpu.VMEM((tm, MIX_WIDTH), BF16)],
        compiler_params=_params(("parallel",)),
        name="merge",
    )(za, zh, zm, x, w_out_bf16, ln_g, ln_b)


def _column(row):
    return jnp.broadcast_to(row, (HEAD_DIM, HEAD_DIM)).T


HBM_TILE_ROWS = 8


def _window_pieces(past):
    pieces = []
    for dil in DILATIONS:
        if dil == 1:
            pieces.append((0, past - BAND, BAND, None))
        elif dil < HBM_TILE_ROWS:
            n = BAND * dil // HBM_TILE_ROWS
            pieces += [(1, past // HBM_TILE_ROWS - n, n, r) for r in range(0, HBM_TILE_ROWS, dil)]
        else:
            pieces.append((2, past // dil - BAND, BAND, 0))
    return pieces


def _cache_views(cache):
    depth, bsz, past, nh, hd = cache.shape
    assert all(d == 1 or HBM_TILE_ROWS % d == 0 or d % HBM_TILE_ROWS == 0 for d in DILATIONS)
    big = max(DILATIONS)
    hm = cache.transpose(0, 1, 3, 2, 4)
    return (hm, hm.reshape(depth, bsz, nh, past // HBM_TILE_ROWS, HBM_TILE_ROWS, hd),
            hm.reshape(depth, bsz, nh, past // big, big, hd))


def _sample_kernel(layer, past, p_ref, cos_ref, sin_ref, k0_hbm, k1_hbm, k2_hbm, v0_hbm, v1_hbm, v2_hbm,
                   st_ref, mk_ref, mv_ref, lb_ref, ng_ref, z_ref, ko_ref, vo_ref, so_ref, kbuf, vbuf, sem):
    b = pl.program_id(0)
    slot = b % 2
    pieces = _window_pieces(past)

    def window_copies(row, sl):
        out = []
        for ci, (views, buf) in enumerate((((k0_hbm, k1_hbm, k2_hbm), kbuf), ((v0_hbm, v1_hbm, v2_hbm), vbuf))):
            off = 0
            for pi, (vi, start, count, res) in enumerate(pieces):
                view = views[vi]
                src = (view.at[layer, row, :, pl.ds(start, count), :] if res is None
                       else view.at[layer, row, :, pl.ds(start, count), res, :])
                out.append(pltpu.make_async_copy(src, buf.at[sl, :, pl.ds(off, count), :], sem.at[sl, ci, pi]))
                off += count
        return out

    @pl.when(b == 0)
    def _():
        for c in window_copies(b, slot):
            c.start()

    @pl.when(b + 1 < pl.num_programs(0))
    def _():
        for c in window_copies(b + 1, 1 - slot):
            c.start()

    cos = cos_ref[...]
    sin = sin_ref[...]
    q_all = _rope(p_ref[G_QA:G_QA + N_HEADS_ATTN, :], cos, sin) * ATTN_SCALE
    k_all = _rope(p_ref[G_KA:G_KA + N_HEADS_ATTN, :], cos, sin)
    v_all = p_ref[G_VA:G_VA + N_HEADS_ATTN, :]
    ko_ref[...] = k_all
    vo_ref[...] = v_all

    for c in window_copies(b, slot):
        c.wait()

    for h in range(N_HEADS_ATTN):
        q = q_all[h:h + 1]
        s_new = jnp.sum(q * k_all[h:h + 1], axis=-1, keepdims=True)
        s = jnp.sum(kbuf[slot, h] * q, axis=-1, keepdims=True)
        m = jnp.maximum(jnp.max(s, axis=0, keepdims=True), s_new)
        p = jnp.exp(s - m)
        p_new = jnp.exp(s_new - m) * len(DILATIONS)
        den = jnp.sum(p, axis=0, keepdims=True) + p_new
        num = jnp.sum(p * vbuf[slot, h], axis=0, keepdims=True) + p_new * v_all[h:h + 1]
        z_ref[h:h + 1, :] = (num / den) * _silu(p_ref[G_GA + h:G_GA + h + 1, :])

    lb_all = lb_ref[...]
    for h in range(N_HEADS_HGRN):
        lanes = slice(h * HEAD_DIM, (h + 1) * HEAD_DIM)
        lb = _lower_bound(lb_all[:, lanes], layer)
        f = lb + (1.0 - lb) * _sigmoid(p_ref[G_FB + h:G_FB + h + 1, :])
        f_col = _column(f)
        q_col = _column(p_ref[G_QB + h:G_QB + h + 1, :])
        s_new = f_col * st_ref[h] + (1.0 - f_col) * p_ref[G_IB + h:G_IB + h + 1, :]
        so_ref[h] = s_new
        o = jnp.sum(s_new * q_col, axis=0, keepdims=True)
        z_ref[N_HEADS_ATTN + h:N_HEADS_ATTN + h + 1, :] = _rms_gate(
            o, ng_ref[:, lanes], p_ref[G_GB + h:G_GB + h + 1, :])

    q_m = p_ref[G_QM:G_QM + N_HEADS_MEM, :] * ATTN_SCALE
    s = jnp.sum(mk_ref[...] * q_m[None], axis=-1, keepdims=True)
    p = jnp.exp(s - jnp.max(s, axis=0)[None])
    o = jnp.sum(p * mv_ref[...], axis=0) / jnp.sum(p, axis=0)
    row = N_HEADS_ATTN + N_HEADS_HGRN
    z_ref[row:row + N_HEADS_MEM, :] = o * _silu(p_ref[G_GM:G_GM + N_HEADS_MEM, :])


def _sample_mixers(proj_rows, cos, sin, win_k, win_v, state, mem_k, mem_v, lb_raw, norm_g, layer):
    bsz, past = win_k.shape[1:3]
    assert all(past % d == 0 and past >= w for w, d in zip(WINDOWS, DILATIONS))
    assert all(w // d == BAND for w, d in zip(WINDOWS, DILATIONS)) and past % HBM_TILE_ROWS == 0
    n_keys = BAND * len(DILATIONS)
    n_pieces = len(_window_pieces(past))
    any_spec = pl.BlockSpec(memory_space=pl.ANY)
    vec = lambda n: pl.BlockSpec((n, W_HGRN), lambda b: (0, 0))
    heads = lambda n: pl.BlockSpec((None, n, HEAD_DIM), lambda b: (b, 0, 0))
    mem_spec = pl.BlockSpec((None, None, N_MEM, N_HEADS_MEM, HEAD_DIM), lambda b: (layer, b, 0, 0, 0))
    return pl.pallas_call(
        functools.partial(_sample_kernel, layer, past),
        grid=(bsz,),
        in_specs=[heads(N_GROUPS), pl.BlockSpec((1, HEAD_DIM), lambda b: (0, 0)),
                  pl.BlockSpec((1, HEAD_DIM), lambda b: (0, 0)), *([any_spec] * 6),
                  pl.BlockSpec((None, None, N_HEADS_HGRN, HEAD_DIM, HEAD_DIM), lambda b: (layer, b, 0, 0, 0)),
                  mem_spec, mem_spec, vec(DEPTH + 1), vec(1)],
        out_specs=[heads(MIX_WIDTH // HEAD_DIM), heads(N_HEADS_ATTN), heads(N_HEADS_ATTN),
                   pl.BlockSpec((None, N_HEADS_HGRN, HEAD_DIM, HEAD_DIM), lambda b: (b, 0, 0, 0))],
        out_shape=[jax.ShapeDtypeStruct((bsz, MIX_WIDTH // HEAD_DIM, HEAD_DIM), F32),
                   jax.ShapeDtypeStruct((bsz, N_HEADS_ATTN, HEAD_DIM), F32),
                   jax.ShapeDtypeStruct((bsz, N_HEADS_ATTN, HEAD_DIM), F32),
                   jax.ShapeDtypeStruct((bsz, N_HEADS_HGRN, HEAD_DIM, HEAD_DIM), F32)],
        scratch_shapes=[pltpu.VMEM((2, N_HEADS_ATTN, n_keys, HEAD_DIM), F32),
                        pltpu.VMEM((2, N_HEADS_ATTN, n_keys, HEAD_DIM), F32),
                        pltpu.SemaphoreType.DMA((2, 2, n_pieces))],
        compiler_params=_params(("arbitrary",)),
        name="sample_mixers",
    )(proj_rows, cos, sin, *_cache_views(win_k), *_cache_views(win_v), state, mem_k, mem_v, lb_raw, norm_g)


def _rope_tables(pos):
    half = HEAD_DIM // 2
    inv_freq = 1.0 / (ROPE_THETA ** (jnp.arange(half, dtype=F32) / half))
    ang = pos.astype(F32)[:, None] * inv_freq[None, :]
    cos, sin = jnp.cos(ang), jnp.sin(ang)
    return jnp.concatenate([cos, cos], axis=-1), jnp.concatenate([-sin, sin], axis=-1)


def kernel(x_prompt, x_sample, cache_win_k, cache_win_v, state_hgrn, cache_mem_k, cache_mem_v, mem_prompt,
           w_in, w_mem_kv, hgrn_lb_raw, hgrn_norm_g, w_out, ln_g, ln_b):
    bsz, seq, _ = x_prompt.shape
    dbsz, n_new, _ = x_sample.shape
    assert n_new == 1 and seq % (BAND * max(DILATIONS)) == 0
    cos_p, sin_p = _rope_tables(jnp.arange(seq))
    cos_s, sin_s = _rope_tables(PAST_LEN + jnp.arange(n_new))

    hp = x_prompt.reshape(bsz * seq, D_MODEL)
    hs = x_sample.reshape(dbsz * n_new, D_MODEL)
    mem = mem_prompt.reshape(bsz * N_MEM, D_MODEL)
    outs = [[] for _ in range(8)]
    for layer in range(DEPTH):
        w_in_b = w_in[layer].astype(BF16)
        w_out_b = w_out[layer].astype(BF16)
        lb_raw = hgrn_lb_raw
        norm_g = hgrn_norm_g[layer][None]
        lg, lbias = ln_g[layer][None], ln_b[layer][None]

        proj = _project(hp, w_in_b, 1024, 1024)
        za, k1, v1 = _prompt_attention(proj, cos_p, sin_p, bsz, seq)
        zh, s1 = _prompt_hgrn(proj, lb_raw, norm_g, layer, bsz, seq)
        zm, mk1, mv1 = _prompt_mem(proj, mem, w_mem_kv[layer].astype(BF16), bsz, seq)
        hp = _merge(za, zh, zm, hp, w_out_b, lg, lbias, 512)

        proj_s = _project(hs, w_in_b, dbsz, 1024).transpose(1, 0, 2)
        zs, k2, v2, s2 = _sample_mixers(proj_s, cos_s, sin_s, cache_win_k, cache_win_v, state_hgrn,
                                        cache_mem_k, cache_mem_v, lb_raw, norm_g, layer)
        zs = zs.transpose(1, 0, 2)
        hs = _merge(zs[:N_HEADS_ATTN], zs[N_HEADS_ATTN:N_HEADS_ATTN + N_HEADS_HGRN],
                    zs[N_HEADS_ATTN + N_HEADS_HGRN:], hs, w_out_b, lg, lbias, dbsz)

        new = (k1.transpose(0, 2, 1, 3), v1.transpose(0, 2, 1, 3), s1, mk1, mv1,
               k2.reshape(dbsz, n_new, N_HEADS_ATTN, HEAD_DIM), v2.reshape(dbsz, n_new, N_HEADS_ATTN, HEAD_DIM),
               s2.astype(state_hgrn.dtype))
        for acc, val in zip(outs, new):
            acc.append(val)

    return (hp.reshape(bsz, seq, D_MODEL), hs.reshape(dbsz, n_new, D_MODEL), *[jnp.stack(o) for o in outs])
```

```python
import functools

import jax
import jax.numpy as jnp
import numpy as np
from jax import lax
from jax.experimental import pallas as pl
from jax.experimental.pallas import tpu as pltpu

F32 = jnp.float32
BF16 = jnp.bfloat16

D_MODEL = 2048
DEPTH = 1
PAST_LEN = 8192
HEAD_DIM = 128
N_HEADS_ATTN = 6
N_HEADS_HGRN = 6
N_HEADS_MEM = 4
W_ATTN = N_HEADS_ATTN * HEAD_DIM
W_HGRN = N_HEADS_HGRN * HEAD_DIM
W_MEM = N_HEADS_MEM * HEAD_DIM
MIX_WIDTH = W_ATTN + W_HGRN + W_MEM
WINDOWS = (128, 512, 2048)
DILATIONS = (1, 4, 16)
N_MEM = 256
ROPE_THETA = 10000.0
LN_EPS = 1e-5
RMS_EPS = 1e-6
NEG_INF = -1e30
DEEPNORM_ALPHA = (2 * DEPTH) ** 0.25
ATTN_SCALE = HEAD_DIM ** -0.5
ATTN_SCALE_LOG2 = ATTN_SCALE * 1.4426950408889634

G_QA, G_KA, G_VA, G_GA = 0, 6, 12, 18
G_QB, G_FB, G_IB, G_GB = 24, 30, 36, 42
G_QM, G_GM = 48, 52
N_GROUPS = 56

LANES = 128
BAND = 128
ATTN_GROUP = 16
HGRN_CHUNK = 64
HGRN_SUB = 16
HGRN_HEADS_PER_STEP = 3
HGRN_PREP_ROWS = 256
HGRN_SCAN_CHUNKS = 4
HGRN_SAFE_LOG_RANGE = 70.0
VMEM_LIMIT = 48 * 1024 * 1024


def _params(sem, vmem=VMEM_LIMIT):
    return pltpu.CompilerParams(dimension_semantics=sem, vmem_limit_bytes=vmem)


def _sigmoid(x):
    return 1.0 / (1.0 + jnp.exp(-x))


def _silu(x):
    return x * _sigmoid(x)


def _dot_nt(a, b):
    return lax.dot_general(a, b, (((1,), (1,)), ((), ())), preferred_element_type=F32)


def _proj_kernel(x_ref, xs_ref, w_ref, o_ref, os_ref, xb_ref):
    i, j = pl.program_id(0), pl.program_id(1)
    tm = x_ref.shape[0]

    def store(ref, acc):
        for c in range(ref.shape[0]):
            ref[c] = acc[:, c * LANES:(c + 1) * LANES]

    @pl.when(j == 0)
    def _():
        xb_ref[:tm, :] = x_ref[...].astype(BF16)

    @pl.when((i == 0) & (j == 0))
    def _():
        xb_ref[tm:, :] = xs_ref[...].astype(BF16)

    @pl.when(i == 0)
    def _():
        acc = jnp.dot(xb_ref[...], w_ref[...], preferred_element_type=F32)
        store(o_ref, acc[:tm])
        store(os_ref, acc[tm:])

    @pl.when(i > 0)
    def _():
        store(o_ref, jnp.dot(xb_ref[:tm, :], w_ref[...], preferred_element_type=F32))


def _project(x, xs, w_bf16, tm, tn):
    m, k = x.shape
    ms = xs.shape[0]
    n = w_bf16.shape[1]
    n_col = n // tn
    return pl.pallas_call(
        _proj_kernel,
        grid=(m // tm, n_col),
        in_specs=[pl.BlockSpec((tm, k), lambda i, j: (i, 0)),
                  pl.BlockSpec((ms, k), lambda i, j: (0, 0)),
                  pl.BlockSpec((k, tn), lambda i, j: (0, j))],
        out_specs=[pl.BlockSpec((tn // LANES, tm, LANES), lambda i, j: (j, i, 0)),
                   pl.BlockSpec((tn // LANES, ms, LANES), lambda i, j: (jnp.where(i == 0, j, n_col - 1), 0, 0))],
        out_shape=[jax.ShapeDtypeStruct((n // LANES, m, LANES), F32),
                   jax.ShapeDtypeStruct((n // LANES, ms, LANES), F32)],
        scratch_shapes=[pltpu.VMEM((tm + ms, k), BF16)],
        compiler_params=_params(("arbitrary", "arbitrary")),
        name="in_proj",
    )(x, xs, w_bf16)


def _rope(x, cos, sin_signed):
    return x * cos + pltpu.roll(x, HEAD_DIM // 2, 1) * sin_signed


def _bias_from_count(count):
    return jnp.where(count > 1.5, 1.0, jnp.where(count > 0.5, 0.0, NEG_INF)).astype(F32)


def _block_deltas(block_gap):
    qi = lax.broadcasted_iota(jnp.int32, (BAND, BAND), 0)
    ki = lax.broadcasted_iota(jnp.int32, (BAND, BAND), 1)
    return block_gap * BAND + qi - ki


def _near_bias(block_gap):
    d = _block_deltas(block_gap)
    return _bias_from_count(((d >= 0) & (d <= BAND)).astype(F32))


def _class_bias(block_gap, far_step):
    d = _block_deltas(block_gap)
    near = (d >= 0) & (d <= BAND)
    far = (d >= 0) & ((d & (far_step - 1)) == 0)
    return _bias_from_count(near.astype(F32) + far.astype(F32))


def _softmax_stage(scores, floors=None):
    out = []
    for idx, s in enumerate(scores):
        m = jnp.broadcast_to(jnp.max(s, axis=-1, keepdims=True), (BAND, HEAD_DIM))
        if floors is not None:
            m = jnp.maximum(m, floors[idx])
        m_wide = jnp.concatenate([m] * (s.shape[1] // HEAD_DIM), axis=1)
        out.append((m, jnp.exp2(s - m_wide).astype(BF16)))
    return out


def _attn_kernel(q_ref, k_ref, v_ref, g_ref, cos_ref, sin_ref, z_ref, ko_ref, vo_ref,
                 qs_ref, qb_ref, kb_ref, vb_ref, qc_ref, kc_ref, vc_ref, acc_ref, m_ref, l_ref):
    seq = q_ref.shape[0]
    d_mid, d_far = DILATIONS[1], DILATIONS[2]
    far_step = d_far // d_mid
    cls_rows = seq // d_mid
    cos = cos_ref[...]
    sin = sin_ref[...]
    qs_ref[...] = _rope(q_ref[...], cos, sin) * ATTN_SCALE_LOG2
    ko_ref[...] = _rope(k_ref[...], cos, sin)
    vo_ref[...] = v_ref[...]
    qb_ref[...] = qs_ref[...].astype(BF16)
    kb_ref[...] = ko_ref[...].astype(BF16)
    vb_ref[...] = v_ref[...].astype(BF16)
    for r in range(d_mid):
        cls, dst = pl.ds(r, cls_rows, stride=d_mid), pl.ds(r * cls_rows, cls_rows)
        qc_ref[dst, :] = qs_ref[cls, :].astype(BF16)
        kc_ref[dst, :] = ko_ref[cls, :].astype(BF16)
        vc_ref[dst, :] = v_ref[cls, :].astype(BF16)
    ones = jnp.ones((cls_rows, HEAD_DIM), BF16)

    n_cls_blk = cls_rows // BAND
    cls_bias = [_class_bias(gap, far_step) for gap in range(min(n_cls_blk, 3))]
    items = [(r, n) for r in range(d_mid) for n in range(n_cls_blk)]
    for g0 in range(0, len(items), ATTN_GROUP):
        group = items[g0:g0 + ATTN_GROUP]
        keys = [pl.ds(r * cls_rows, (n + 1) * BAND) for r, n in group]
        scores = [_dot_nt(qc_ref[pl.ds(r * cls_rows + n * BAND, BAND), :], kc_ref[kr, :])
                  + jnp.concatenate([cls_bias[min(n - nk, 2)] for nk in range(n + 1)], axis=1)
                  for (r, n), kr in zip(group, keys)]
        for (r, n), kr, (m, p) in zip(group, keys, _softmax_stage(scores)):
            rows = pl.ds(r + d_mid * BAND * n, BAND, stride=d_mid)
            acc_ref[rows, :] = jnp.dot(p, vc_ref[kr, :], preferred_element_type=F32)
            l_ref[rows, :] = jnp.dot(p, ones[:p.shape[1]], preferred_element_type=F32)
            m_ref[rows, :] = m

    near_bias = [_near_bias(0), jnp.concatenate([_near_bias(1), _near_bias(0)], axis=1)]
    items = list(range(seq // BAND))
    for g0 in range(0, len(items), ATTN_GROUP):
        group = items[g0:g0 + ATTN_GROUP]
        rows = [pl.ds(n * BAND, BAND) for n in group]
        keys = [pl.ds(max(n - 1, 0) * BAND, BAND * min(n + 1, 2)) for n in group]
        scores = [_dot_nt(qb_ref[rw, :], kb_ref[kr, :]) + near_bias[min(n, 1)]
                  for n, rw, kr in zip(group, rows, keys)]
        probs = _softmax_stage(scores, floors=[m_ref[rw, :] for rw in rows])
        for rw, kr, (m, p) in zip(rows, keys, probs):
            w = jnp.exp2(m_ref[rw, :] - m)
            num = jnp.dot(p, vb_ref[kr, :], preferred_element_type=F32) + w * acc_ref[rw, :]
            den = jnp.dot(p, ones[:p.shape[1]], preferred_element_type=F32) + w * l_ref[rw, :]
            z_ref[rw, :] = ((num / den) * _silu(g_ref[rw, :])).astype(z_ref.dtype)


def _prompt_attention(proj, cos, sin, bsz, seq):
    def col(g0):
        return pl.BlockSpec((None, seq, HEAD_DIM), lambda b, h: (g0 + h, b, 0))

    table = pl.BlockSpec((seq, HEAD_DIM), lambda b, h: (0, 0))
    kv_out = pl.BlockSpec((None, None, seq, HEAD_DIM), lambda b, h: (b, h, 0, 0))
    d_near, d_mid, d_far = DILATIONS
    far_step = d_far // d_mid
    assert d_near == 1 and d_far % d_mid == 0 and far_step & (far_step - 1) == 0
    assert all(w // d == BAND for w, d in zip(WINDOWS, DILATIONS)) and seq % (d_mid * BAND) == 0
    assert seq // d_mid <= far_step * BAND
    return pl.pallas_call(
        _attn_kernel,
        grid=(bsz, N_HEADS_ATTN),
        in_specs=[col(G_QA), col(G_KA), col(G_VA), col(G_GA), table, table],
        out_specs=[pl.BlockSpec((None, seq, HEAD_DIM), lambda b, h: (h, b, 0)), kv_out, kv_out],
        out_shape=[jax.ShapeDtypeStruct((N_HEADS_ATTN, bsz * seq, HEAD_DIM), BF16),
                   jax.ShapeDtypeStruct((bsz, N_HEADS_ATTN, seq, HEAD_DIM), F32),
                   jax.ShapeDtypeStruct((bsz, N_HEADS_ATTN, seq, HEAD_DIM), F32)],
        scratch_shapes=[pltpu.VMEM((seq, HEAD_DIM), F32)] + [pltpu.VMEM((seq, HEAD_DIM), BF16)] * 6
                       + [pltpu.VMEM((seq, HEAD_DIM), F32)] * 3,
        compiler_params=_params(("parallel", "parallel")),
        name="prompt_attn",
    )(proj, proj, proj, proj, cos, sin)


def _lower_bound(lb_raw, layer):
    e = jnp.exp(lb_raw - jnp.max(lb_raw, axis=0, keepdims=True))
    sm = e / jnp.sum(e, axis=0, keepdims=True)
    return jnp.sum(sm[:layer + 1], axis=0, keepdims=True)


def _split2(x):
    hi = x.astype(BF16)
    return hi, (x - hi.astype(F32)).astype(BF16)


def _rms_gate(o, norm_g, gate):
    o = o * lax.rsqrt(jnp.mean(o * o, axis=-1, keepdims=True) + RMS_EPS)
    return o * norm_g * _silu(gate)


def _hgrn_gates(fb, lb, tril):
    f = lb + (1.0 - lb) * _sigmoid(fb)
    g = jnp.log(f)
    b = sum(jnp.dot(tril, piece, preferred_element_type=F32) for piece in _split2(g))
    return 1.0 - f, b


def _hgrn_fast_prepare(q_ref, f_ref, lb, qh_ref, kh_ref, el_ref):
    n_heads, seq, _ = q_ref.shape
    blk = HGRN_PREP_ROWS
    tril = _tril_ones(blk)

    def body(bi, carry):
        r0 = pl.multiple_of(bi * blk, blk)
        fb = jnp.concatenate([f_ref[h, pl.ds(r0, blk), :] for h in range(n_heads)], axis=1)
        f = lb + (1.0 - lb) * _sigmoid(fb)
        b_blk = sum(jnp.dot(tril, piece, preferred_element_type=F32) for piece in _split2(jnp.log(f)))
        kk = 1.0 - f
        for c0 in range(0, blk, HGRN_CHUNK):
            rs = slice(c0, c0 + HGRN_CHUNK)
            b = b_blk[rs] - b_blk[c0 - 1:c0] if c0 else b_blk[rs]
            rows = pl.ds(r0 + c0, HGRN_CHUNK)
            q = jnp.concatenate([q_ref[h, rows, :] for h in range(n_heads)], axis=1)
            qh_ref[rows, :] = (q * jnp.exp(b)).astype(BF16)
            kh_ref[rows, :] = (kk[rs] * jnp.exp(-b)).astype(BF16)
            el_ref[pl.ds(bi * (blk // HGRN_CHUNK) + c0 // HGRN_CHUNK, 1), :] = jnp.exp(b[HGRN_CHUNK - 1:])
        return carry

    lax.fori_loop(0, seq // blk, body, 0)


def _hgrn_fast_scan(i_ref, g_ref, norm_g, qh_ref, kh_ref, el_ref, st_ref, z_ref):
    n_heads, seq, _ = i_ref.shape
    c = HGRN_CHUNK
    heads = range(n_heads)
    chunks = range(HGRN_SCAN_CHUNKS)
    lanes = [slice(h * HEAD_DIM, (h + 1) * HEAD_DIM) for h in heads]
    causal = lax.broadcasted_iota(jnp.int32, (c, c), 0) >= lax.broadcasted_iota(jnp.int32, (c, c), 1)
    tn = (((0,), (0,)), ((), ()))

    def body(ti, carry):
        rows = [pl.ds(pl.multiple_of((ti * HGRN_SCAN_CHUNKS + k) * c, c), c) for k in chunks]
        qh = [[qh_ref[rows[k], lanes[h]] for h in heads] for k in chunks]
        kh = [[kh_ref[rows[k], lanes[h]] for h in heads] for k in chunks]
        vb = [[i_ref[h, rows[k], :].astype(BF16) for h in heads] for k in chunks]
        att = [[jnp.where(causal, _dot_nt(qh[k][h], kh[k][h]), 0.0).astype(BF16) for h in heads] for k in chunks]
        ds = [[lax.dot_general(vb[k][h], kh[k][h], tn, preferred_element_type=F32) for h in heads] for k in chunks]
        st = [st_ref[h] for h in heads]
        o = []
        for k in chunks:
            el = el_ref[pl.ds(ti * HGRN_SCAN_CHUNKS + k, 1), :]
            o.append([_dot_nt(qh[k][h], st[h].astype(BF16)) for h in heads])
            st = [(st[h] + ds[k][h]) * el[:, lanes[h]] for h in heads]
        for h in heads:
            st_ref[h] = st[h]
        for k in chunks:
            for h in heads:
                o_kh = o[k][h] + jnp.dot(att[k][h], vb[k][h], preferred_element_type=F32)
                z_ref[h, rows[k], :] = _rms_gate(o_kh, norm_g[:, lanes[h]], g_ref[h, rows[k], :]).astype(z_ref.dtype)
        return carry

    lax.fori_loop(0, seq // (c * HGRN_SCAN_CHUNKS), body, 0)


def _hgrn_chunk(q, fb, v, lb, st, tril):
    c = q.shape[0]
    kk, b = _hgrn_gates(fb, lb, tril)
    o = _dot_nt((q * jnp.exp(b)).astype(BF16), st.astype(BF16))

    s_idx = lax.broadcasted_iota(jnp.int32, (c, 1), 0)
    lane = lax.broadcasted_iota(jnp.int32, (HGRN_SUB, c), 1)
    row = lax.broadcasted_iota(jnp.int32, (HGRN_SUB, c), 0)
    att_rows = []
    for i0 in range(0, c, HGRN_SUB):
        qi = q[i0:i0 + HGRN_SUB]
        bi = b[i0:i0 + HGRN_SUB]
        if i0 > 0:
            bref = b[i0 - 1:i0]
            kt = jnp.where(s_idx < i0, kk * jnp.exp(jnp.minimum(bref - b, 0.0)), 0.0)
            att = _dot_nt((qi * jnp.exp(bi - bref)).astype(BF16), kt.astype(BF16))
        else:
            att = jnp.zeros((HGRN_SUB, c), F32)
        for j in range(HGRN_SUB):
            s = i0 + j
            e = jnp.exp(jnp.minimum(bi - b[s:s + 1], 0.0))
            colv = jnp.sum(qi * kk[s:s + 1] * e, axis=-1, keepdims=True)
            att = jnp.where((lane == s) & (row >= j), colv, att)
        att_rows.append(att)
    att = jnp.concatenate(att_rows, axis=0)
    vb = v.astype(BF16)
    o = o + jnp.dot(att.astype(BF16), vb, preferred_element_type=F32)

    b_last = b[c - 1:c]
    kd = (kk * jnp.exp(b_last - b)).astype(BF16)
    st_new = st * jnp.exp(b_last) + lax.dot_general(vb, kd, (((0,), (0,)), ((), ())),
                                                    preferred_element_type=F32)
    return o, st_new


def _tril_ones(c):
    return (lax.broadcasted_iota(jnp.int32, (c, c), 0) >= lax.broadcasted_iota(jnp.int32, (c, c), 1)
            ).astype(BF16)


def _hgrn_kernel(layer, q_ref, f_ref, i_ref, g_ref, lb_ref, ng_ref, z_ref, s_ref, st_ref, qh_ref, kh_ref, el_ref):
    n_heads, seq, _ = q_ref.shape
    lb = _lower_bound(lb_ref[...], layer)
    norm_g = ng_ref[...]
    st_ref[...] = jnp.zeros_like(st_ref)

    def fast():
        _hgrn_fast_prepare(q_ref, f_ref, lb, qh_ref, kh_ref, el_ref)
        _hgrn_fast_scan(i_ref, g_ref, norm_g, qh_ref, kh_ref, el_ref, st_ref, z_ref)

    def safe():
        tril = _tril_ones(HGRN_CHUNK)

        def body(ci, carry):
            rows = pl.ds(pl.multiple_of(ci * HGRN_CHUNK, HGRN_CHUNK), HGRN_CHUNK)
            for h in range(n_heads):
                lanes = slice(h * HEAD_DIM, (h + 1) * HEAD_DIM)
                o, st = _hgrn_chunk(q_ref[h, rows, :], f_ref[h, rows, :], i_ref[h, rows, :], lb[:, lanes],
                                    st_ref[h], tril)
                st_ref[h] = st
                z_ref[h, rows, :] = _rms_gate(o, norm_g[:, lanes], g_ref[h, rows, :]).astype(z_ref.dtype)
            return carry

        lax.fori_loop(0, seq // HGRN_CHUNK, body, 0)

    fast_ok = HGRN_CHUNK * -jnp.log(jnp.min(lb)) <= HGRN_SAFE_LOG_RANGE
    lax.cond(fast_ok, fast, safe)
    for h in range(n_heads):
        s_ref[h] = st_ref[h].T


def _prompt_hgrn(proj, lb_raw, norm_g, layer, bsz, seq):
    hps = HGRN_HEADS_PER_STEP
    assert N_HEADS_HGRN % hps == 0 and all(g % hps == 0 for g in (G_QB, G_FB, G_IB, G_GB))

    def cols(g0):
        return pl.BlockSpec((hps, seq, HEAD_DIM), lambda b, h: (g0 // hps + h, b, 0))

    return pl.pallas_call(
        functools.partial(_hgrn_kernel, layer),
        grid=(bsz, N_HEADS_HGRN // hps),
        in_specs=[cols(G_QB), cols(G_FB), cols(G_IB), cols(G_GB),
                  pl.BlockSpec((DEPTH + 1, hps * HEAD_DIM), lambda b, h: (0, h)),
                  pl.BlockSpec((1, hps * HEAD_DIM), lambda b, h: (0, h))],
        out_specs=[pl.BlockSpec((hps, seq, HEAD_DIM), lambda b, h: (h, b, 0)),
                   pl.BlockSpec((None, hps, HEAD_DIM, HEAD_DIM), lambda b, h: (b, h, 0, 0))],
        out_shape=[jax.ShapeDtypeStruct((N_HEADS_HGRN, bsz * seq, HEAD_DIM), BF16),
                   jax.ShapeDtypeStruct((bsz, N_HEADS_HGRN, HEAD_DIM, HEAD_DIM), F32)],
        scratch_shapes=[pltpu.VMEM((hps, HEAD_DIM, HEAD_DIM), F32),
                        pltpu.VMEM((seq, hps * HEAD_DIM), BF16),
                        pltpu.VMEM((seq, hps * HEAD_DIM), BF16),
                        pltpu.VMEM((seq // HGRN_CHUNK, hps * HEAD_DIM), F32)],
        compiler_params=_params(("parallel", "parallel")),
        name="prompt_hgrn",
    )(proj, proj, proj, proj, lb_raw, norm_g)


MEM_ROWS = 512


def _mem_kernel(q_ref, g_ref, mem_ref, w_ref, z_ref, mk_ref, mv_ref):
    seq = q_ref.shape[1]
    kv = jnp.dot(mem_ref[...].astype(BF16), w_ref[...].astype(BF16), preferred_element_type=F32)
    for h in range(N_HEADS_MEM):
        mk = kv[:, h * HEAD_DIM:(h + 1) * HEAD_DIM]
        mv = kv[:, W_MEM + h * HEAD_DIM:W_MEM + (h + 1) * HEAD_DIM]
        mk_ref[:, h, :] = mk
        mv_ref[:, h, :] = mv
        mk, mv = mk.astype(BF16), mv.astype(BF16)
        for n in range(seq // MEM_ROWS):
            rows = pl.ds(n * MEM_ROWS, MEM_ROWS)
            s = _dot_nt((q_ref[h, rows, :] * ATTN_SCALE).astype(BF16), mk)
            p = jnp.exp(s - jnp.max(s, axis=-1, keepdims=True))
            den = jnp.sum(p, axis=-1, keepdims=True)
            o = jnp.dot(p.astype(BF16), mv, preferred_element_type=F32) / den
            z_ref[h, rows, :] = (o * _silu(g_ref[h, rows, :])).astype(z_ref.dtype)


def _prompt_mem(proj, mem, w_kv, bsz, seq):
    assert G_QM % N_HEADS_MEM == 0 and G_GM % N_HEADS_MEM == 0

    def cols(g0):
        return pl.BlockSpec((N_HEADS_MEM, seq, HEAD_DIM), lambda b: (g0 // N_HEADS_MEM, b, 0))

    kv_out = pl.BlockSpec((None, N_MEM, N_HEADS_MEM, HEAD_DIM), lambda b: (b, 0, 0, 0))
    kv_shape = jax.ShapeDtypeStruct((bsz, N_MEM, N_HEADS_MEM, HEAD_DIM), F32)
    return pl.pallas_call(
        _mem_kernel,
        grid=(bsz,),
        in_specs=[cols(G_QM), cols(G_GM), pl.BlockSpec((N_MEM, D_MODEL), lambda b: (b, 0)),
                  pl.BlockSpec((D_MODEL, 2 * W_MEM), lambda b: (0, 0))],
        out_specs=[pl.BlockSpec((N_HEADS_MEM, seq, HEAD_DIM), lambda b: (0, b, 0)), kv_out, kv_out],
        out_shape=[jax.ShapeDtypeStruct((N_HEADS_MEM, bsz * seq, HEAD_DIM), BF16), kv_shape, kv_shape],
        compiler_params=_params(("parallel",)),
        name="prompt_mem",
    )(proj, proj, mem, w_kv)


def _merge_kernel(za_ref, zh_ref, zm_ref, x_ref, w_ref, lg_ref, lb_ref, o_ref, z_ref):
    c0 = 0
    for ref in (za_ref, zh_ref, zm_ref):
        for c in range(ref.shape[0]):
            z_ref[:, (c0 + c) * LANES:(c0 + c + 1) * LANES] = ref[c].astype(BF16)
        c0 += ref.shape[0]
    y = jnp.dot(z_ref[...], w_ref[...], preferred_element_type=F32)
    r = DEEPNORM_ALPHA * x_ref[...] + y
    mu = jnp.mean(r, axis=-1, keepdims=True)
    d = r - mu
    var = jnp.mean(d * d, axis=-1, keepdims=True)
    o_ref[...] = d * lax.rsqrt(var + LN_EPS) * lg_ref[...] + lb_ref[...]


def _merge(za, zh, zm, x, w_out_bf16, ln_g, ln_b, tm):
    m = x.shape[0]

    def slab(a):
        return pl.BlockSpec((a.shape[0], tm, LANES), lambda i: (0, i, 0))

    const = lambda shape: pl.BlockSpec(shape, lambda i: (0, 0))
    return pl.pallas_call(
        _merge_kernel,
        grid=(m // tm,),
        in_specs=[slab(za), slab(zh), slab(zm), pl.BlockSpec((tm, D_MODEL), lambda i: (i, 0)),
                  const((MIX_WIDTH, D_MODEL)), const((1, D_MODEL)), const((1, D_MODEL))],
        out_specs=pl.BlockSpec((tm, D_MODEL), lambda i: (i, 0)),
        out_shape=jax.ShapeDtypeStruct((m, D_MODEL), F32),
        scratch_shapes=[pltpu.VMEM((tm, MIX_WIDTH), BF16)],
        compiler_params=_params(("parallel",)),
        name="merge",
    )(za, zh, zm, x, w_out_bf16, ln_g, ln_b)


def _column(row):
    return jnp.broadcast_to(row, (HEAD_DIM, HEAD_DIM)).T


HBM_TILE_ROWS = 8


def _window_pieces(past):
    pieces = []
    for dil in DILATIONS:
        if dil == 1:
            pieces.append((0, past - BAND, BAND, None))
        elif dil < HBM_TILE_ROWS:
            n = BAND * dil // HBM_TILE_ROWS
            pieces += [(1, past // HBM_TILE_ROWS - n, n, r) for r in range(0, HBM_TILE_ROWS, dil)]
        else:
            pieces.append((2, past // dil - BAND, BAND, 0))
    return pieces


def _cache_views(cache):
    depth, bsz, past, nh, hd = cache.shape
    assert all(d == 1 or HBM_TILE_ROWS % d == 0 or d % HBM_TILE_ROWS == 0 for d in DILATIONS)
    big = max(DILATIONS)
    hm = cache.transpose(0, 1, 3, 2, 4)
    return (hm, hm.reshape(depth, bsz, nh, past // HBM_TILE_ROWS, HBM_TILE_ROWS, hd),
            hm.reshape(depth, bsz, nh, past // big, big, hd))


def _sample_kernel(layer, past, p_ref, cos_ref, sin_ref, k0_hbm, k1_hbm, k2_hbm, v0_hbm, v1_hbm, v2_hbm,
                   st_ref, mk_ref, mv_ref, lb_ref, ng_ref, z_ref, ko_ref, vo_ref, so_ref, kbuf, vbuf, sem):
    b = pl.program_id(0)
    slot = b % 2
    pieces = _window_pieces(past)

    def window_copies(row, sl):
        out = []
        for ci, (views, buf) in enumerate((((k0_hbm, k1_hbm, k2_hbm), kbuf), ((v0_hbm, v1_hbm, v2_hbm), vbuf))):
            off = 0
            for pi, (vi, start, count, res) in enumerate(pieces):
                view = views[vi]
                src = (view.at[layer, row, :, pl.ds(start, count), :] if res is None
                       else view.at[layer, row, :, pl.ds(start, count), res, :])
                out.append(pltpu.make_async_copy(src, buf.at[sl, :, pl.ds(off, count), :], sem.at[sl, ci, pi]))
                off += count
        return out

    @pl.when(b == 0)
    def _():
        for c in window_copies(b, slot):
            c.start()

    @pl.when(b + 1 < pl.num_programs(0))
    def _():
        for c in window_copies(b + 1, 1 - slot):
            c.start()

    cos = cos_ref[...]
    sin = sin_ref[...]
    q_all = _rope(p_ref[G_QA:G_QA + N_HEADS_ATTN, :], cos, sin) * ATTN_SCALE
    k_all = _rope(p_ref[G_KA:G_KA + N_HEADS_ATTN, :], cos, sin)
    v_all = p_ref[G_VA:G_VA + N_HEADS_ATTN, :]
    ko_ref[...] = k_all
    vo_ref[...] = v_all

    for c in window_copies(b, slot):
        c.wait()

    for h in range(N_HEADS_ATTN):
        q = q_all[h:h + 1]
        s_new = jnp.sum(q * k_all[h:h + 1], axis=-1, keepdims=True)
        s = jnp.sum(kbuf[slot, h] * q, axis=-1, keepdims=True)
        m = jnp.maximum(jnp.max(s, axis=0, keepdims=True), s_new)
        p = jnp.exp(s - m)
        p_new = jnp.exp(s_new - m) * len(DILATIONS)
        den = jnp.sum(p, axis=0, keepdims=True) + p_new
        num = jnp.sum(p * vbuf[slot, h], axis=0, keepdims=True) + p_new * v_all[h:h + 1]
        z_ref[h:h + 1, :] = (num / den) * _silu(p_ref[G_GA + h:G_GA + h + 1, :])

    lb_all = lb_ref[...]
    for h in range(N_HEADS_HGRN):
        lanes = slice(h * HEAD_DIM, (h + 1) * HEAD_DIM)
        lb = _lower_bound(lb_all[:, lanes], layer)
        f = lb + (1.0 - lb) * _sigmoid(p_ref[G_FB + h:G_FB + h + 1, :])
        f_col = _column(f)
        q_col = _column(p_ref[G_QB + h:G_QB + h + 1, :])
        s_new = f_col * st_ref[h] + (1.0 - f_col) * p_ref[G_IB + h:G_IB + h + 1, :]
        so_ref[h] = s_new
        o = jnp.sum(s_new * q_col, axis=0, keepdims=True)
        z_ref[N_HEADS_ATTN + h:N_HEADS_ATTN + h + 1, :] = _rms_gate(
            o, ng_ref[:, lanes], p_ref[G_GB + h:G_GB + h + 1, :])

    q_m = p_ref[G_QM:G_QM + N_HEADS_MEM, :] * ATTN_SCALE
    s = jnp.sum(mk_ref[...] * q_m[None], axis=-1, keepdims=True)
    p = jnp.exp(s - jnp.max(s, axis=0)[None])
    o = jnp.sum(p * mv_ref[...], axis=0) / jnp.sum(p, axis=0)
    row = N_HEADS_ATTN + N_HEADS_HGRN
    z_ref[row:row + N_HEADS_MEM, :] = o * _silu(p_ref[G_GM:G_GM + N_HEADS_MEM, :])


def _sample_mixers(proj_rows, cos, sin, win_k, win_v, state, mem_k, mem_v, lb_raw, norm_g, layer):
    bsz, past = win_k.shape[1:3]
    assert all(past % d == 0 and past >= w for w, d in zip(WINDOWS, DILATIONS))
    assert all(w // d == BAND for w, d in zip(WINDOWS, DILATIONS)) and past % HBM_TILE_ROWS == 0
    n_keys = BAND * len(DILATIONS)
    n_pieces = len(_window_pieces(past))
    any_spec = pl.BlockSpec(memory_space=pl.ANY)
    vec = lambda n: pl.BlockSpec((n, W_HGRN), lambda b: (0, 0))
    heads = lambda n: pl.BlockSpec((None, n, HEAD_DIM), lambda b: (b, 0, 0))
    mem_spec = pl.BlockSpec((None, None, N_MEM, N_HEADS_MEM, HEAD_DIM), lambda b: (layer, b, 0, 0, 0))
    return pl.pallas_call(
        functools.partial(_sample_kernel, layer, past),
        grid=(bsz,),
        in_specs=[heads(N_GROUPS), pl.BlockSpec((1, HEAD_DIM), lambda b: (0, 0)),
                  pl.BlockSpec((1, HEAD_DIM), lambda b: (0, 0)), *([any_spec] * 6),
                  pl.BlockSpec((None, None, N_HEADS_HGRN, HEAD_DIM, HEAD_DIM), lambda b: (layer, b, 0, 0, 0)),
                  mem_spec, mem_spec, vec(DEPTH + 1), vec(1)],
        out_specs=[heads(MIX_WIDTH // HEAD_DIM), heads(N_HEADS_ATTN), heads(N_HEADS_ATTN),
                   pl.BlockSpec((None, N_HEADS_HGRN, HEAD_DIM, HEAD_DIM), lambda b: (b, 0, 0, 0))],
        out_shape=[jax.ShapeDtypeStruct((bsz, MIX_WIDTH // HEAD_DIM, HEAD_DIM), F32),
                   jax.ShapeDtypeStruct((bsz, N_HEADS_ATTN, HEAD_DIM), F32),
                   jax.ShapeDtypeStruct((bsz, N_HEADS_ATTN, HEAD_DIM), F32),
                   jax.ShapeDtypeStruct((bsz, N_HEADS_HGRN, HEAD_DIM, HEAD_DIM), F32)],
        scratch_shapes=[pltpu.VMEM((2, N_HEADS_ATTN, n_keys, HEAD_DIM), F32),
                        pltpu.VMEM((2, N_HEADS_ATTN, n_keys, HEAD_DIM), F32),
                        pltpu.SemaphoreType.DMA((2, 2, n_pieces))],
        compiler_params=_params(("arbitrary",)),
        name="sample_mixers",
    )(proj_rows, cos, sin, *_cache_views(win_k), *_cache_views(win_v), state, mem_k, mem_v, lb_raw, norm_g)


def _rope_tables(pos):
    half = HEAD_DIM // 2
    inv_freq = 1.0 / (ROPE_THETA ** (jnp.arange(half, dtype=F32) / half))
    ang = pos.astype(F32)[:, None] * inv_freq[None, :]
    cos, sin = jnp.cos(ang), jnp.sin(ang)
    return jnp.concatenate([cos, cos], axis=-1), jnp.concatenate([-sin, sin], axis=-1)


def kernel(x_prompt, x_sample, cache_win_k, cache_win_v, state_hgrn, cache_mem_k, cache_mem_v, mem_prompt,
           w_in, w_mem_kv, hgrn_lb_raw, hgrn_norm_g, w_out, ln_g, ln_b):
    bsz, seq, _ = x_prompt.shape
    dbsz, n_new, _ = x_sample.shape
    assert n_new == 1 and seq % (BAND * max(DILATIONS)) == 0
    cos_p, sin_p = _rope_tables(jnp.arange(seq))
    cos_s, sin_s = _rope_tables(PAST_LEN + jnp.arange(n_new))

    hp = x_prompt.reshape(bsz * seq, D_MODEL)
    hs = x_sample.reshape(dbsz * n_new, D_MODEL)
    mem = mem_prompt.reshape(bsz * N_MEM, D_MODEL)
    outs = [[] for _ in range(8)]
    for layer in range(DEPTH):
        w_in_b = w_in[layer].astype(BF16)
        w_out_b = w_out[layer].astype(BF16)
        lb_raw = hgrn_lb_raw
        norm_g = hgrn_norm_g[layer][None]
        lg, lbias = ln_g[layer][None], ln_b[layer][None]

        proj, proj_s = _project(hp, hs, w_in_b, 1024, 1024)
        za, k1, v1 = _prompt_attention(proj, cos_p, sin_p, bsz, seq)
        zh, s1 = _prompt_hgrn(proj, lb_raw, norm_g, layer, bsz, seq)
        zm, mk1, mv1 = _prompt_mem(proj, mem, w_mem_kv[layer], bsz, seq)
        hp = _merge(za, zh, zm, hp, w_out_b, lg, lbias, 512)

        proj_s = proj_s.transpose(1, 0, 2)
        zs, k2, v2, s2 = _sample_mixers(proj_s, cos_s, sin_s, cache_win_k, cache_win_v, state_hgrn,
                                        cache_mem_k, cache_mem_v, lb_raw, norm_g, layer)
        zs = zs.transpose(1, 0, 2)
        hs = _merge(zs[:N_HEADS_ATTN], zs[N_HEADS_ATTN:N_HEADS_ATTN + N_HEADS_HGRN],
                    zs[N_HEADS_ATTN + N_HEADS_HGRN:], hs, w_out_b, lg, lbias, dbsz)

        new = (k1.transpose(0, 2, 1, 3), v1.transpose(0, 2, 1, 3), s1, mk1, mv1,
               k2.reshape(dbsz, n_new, N_HEADS_ATTN, HEAD_DIM), v2.reshape(dbsz, n_new, N_HEADS_ATTN, HEAD_DIM),
               s2.astype(state_hgrn.dtype))
        for acc, val in zip(outs, new):
            acc.append(val)

    return (hp.reshape(bsz, seq, D_MODEL), hs.reshape(dbsz, n_new, D_MODEL), *[jnp.stack(o) for o in outs])
```

```python
import functools

import jax
import jax.numpy as jnp
import numpy as np
from jax import lax
from jax.experimental import pallas as pl
from jax.experimental.pallas import tpu as pltpu

F32 = jnp.float32
BF16 = jnp.bfloat16

D_MODEL = 2048
DEPTH = 1
PAST_LEN = 8192
HEAD_DIM = 128
N_HEADS_ATTN = 6
N_HEADS_HGRN = 6
N_HEADS_MEM = 4
W_ATTN = N_HEADS_ATTN * HEAD_DIM
W_HGRN = N_HEADS_HGRN * HEAD_DIM
W_MEM = N_HEADS_MEM * HEAD_DIM
MIX_WIDTH = W_ATTN + W_HGRN + W_MEM
WINDOWS = (128, 512, 2048)
DILATIONS = (1, 4, 16)
N_MEM = 256
ROPE_THETA = 10000.0
LN_EPS = 1e-5
RMS_EPS = 1e-6
NEG_INF = -1e30
DEEPNORM_ALPHA = (2 * DEPTH) ** 0.25
ATTN_SCALE = HEAD_DIM ** -0.5
ATTN_SCALE_LOG2 = ATTN_SCALE * 1.4426950408889634

G_QA, G_KA, G_VA, G_GA = 0, 6, 12, 18
G_QB, G_FB, G_IB, G_GB = 24, 30, 36, 42
G_QM, G_GM = 48, 52
N_GROUPS = 56

LANES = 128
BAND = 128
ATTN_GROUP = 16
HGRN_CHUNK = 64
HGRN_SUB = 16
HGRN_HEADS_PER_STEP = 3
HGRN_PREP_ROWS = 256
HGRN_SCAN_CHUNKS = 16
HGRN_SAFE_LOG_RANGE = 70.0
VMEM_LIMIT = 48 * 1024 * 1024


def _params(sem, vmem=VMEM_LIMIT):
    return pltpu.CompilerParams(dimension_semantics=sem, vmem_limit_bytes=vmem)


def _sigmoid(x):
    return 1.0 / (1.0 + jnp.exp(-x))


def _silu(x):
    return x * _sigmoid(x)


def _dot_nt(a, b):
    return lax.dot_general(a, b, (((1,), (1,)), ((), ())), preferred_element_type=F32)


def _proj_kernel(x_ref, xs_ref, w_ref, o_ref, os_ref, xb_ref):
    i, j = pl.program_id(0), pl.program_id(1)
    tm = x_ref.shape[0]

    def store(ref, acc):
        for c in range(ref.shape[0]):
            ref[c] = acc[:, c * LANES:(c + 1) * LANES]

    @pl.when(j == 0)
    def _():
        xb_ref[:tm, :] = x_ref[...].astype(BF16)

    @pl.when((i == 0) & (j == 0))
    def _():
        xb_ref[tm:, :] = xs_ref[...].astype(BF16)

    @pl.when(i == 0)
    def _():
        acc = jnp.dot(xb_ref[...], w_ref[...], preferred_element_type=F32)
        store(o_ref, acc[:tm])
        store(os_ref, acc[tm:])

    @pl.when(i > 0)
    def _():
        store(o_ref, jnp.dot(xb_ref[:tm, :], w_ref[...], preferred_element_type=F32))


def _project(x, xs, w_bf16, tm, tn):
    m, k = x.shape
    ms = xs.shape[0]
    n = w_bf16.shape[1]
    n_col = n // tn
    return pl.pallas_call(
        _proj_kernel,
        grid=(m // tm, n_col),
        in_specs=[pl.BlockSpec((tm, k), lambda i, j: (i, 0)),
                  pl.BlockSpec((ms, k), lambda i, j: (0, 0)),
                  pl.BlockSpec((k, tn), lambda i, j: (0, j))],
        out_specs=[pl.BlockSpec((tn // LANES, tm, LANES), lambda i, j: (j, i, 0)),
                   pl.BlockSpec((tn // LANES, ms, LANES), lambda i, j: (jnp.where(i == 0, j, n_col - 1), 0, 0))],
        out_shape=[jax.ShapeDtypeStruct((n // LANES, m, LANES), F32),
                   jax.ShapeDtypeStruct((n // LANES, ms, LANES), F32)],
        scratch_shapes=[pltpu.VMEM((tm + ms, k), BF16)],
        compiler_params=_params(("arbitrary", "arbitrary")),
        name="in_proj",
    )(x, xs, w_bf16)


def _rope(x, cos, sin_signed):
    return x * cos + pltpu.roll(x, HEAD_DIM // 2, 1) * sin_signed


def _bias_from_count(count):
    return jnp.where(count > 1.5, 1.0, jnp.where(count > 0.5, 0.0, NEG_INF)).astype(F32)


def _block_deltas(block_gap):
    qi = lax.broadcasted_iota(jnp.int32, (BAND, BAND), 0)
    ki = lax.broadcasted_iota(jnp.int32, (BAND, BAND), 1)
    return block_gap * BAND + qi - ki


def _near_bias(block_gap):
    d = _block_deltas(block_gap)
    return _bias_from_count(((d >= 0) & (d <= BAND)).astype(F32))


def _class_bias(block_gap, far_step):
    d = _block_deltas(block_gap)
    near = (d >= 0) & (d <= BAND)
    far = (d >= 0) & ((d & (far_step - 1)) == 0)
    return _bias_from_count(near.astype(F32) + far.astype(F32))


def _softmax_stage(scores, floors=None):
    out = []
    for idx, s in enumerate(scores):
        m = jnp.broadcast_to(jnp.max(s, axis=-1, keepdims=True), (BAND, HEAD_DIM))
        if floors is not None:
            m = jnp.maximum(m, floors[idx])
        m_wide = jnp.concatenate([m] * (s.shape[1] // HEAD_DIM), axis=1)
        out.append((m, jnp.exp2(s - m_wide).astype(BF16)))
    return out


def _attn_kernel(q_ref, k_ref, v_ref, g_ref, cos_ref, sin_ref, z_ref, ko_ref, vo_ref,
                 qs_ref, qb_ref, kb_ref, vb_ref, qc_ref, kc_ref, vc_ref, acc_ref, m_ref, l_ref):
    seq = q_ref.shape[0]
    d_mid, d_far = DILATIONS[1], DILATIONS[2]
    far_step = d_far // d_mid
    cls_rows = seq // d_mid
    cos = cos_ref[...]
    sin = sin_ref[...]
    qs_ref[...] = _rope(q_ref[...], cos, sin) * ATTN_SCALE_LOG2
    ko_ref[...] = _rope(k_ref[...], cos, sin)
    vo_ref[...] = v_ref[...]
    qb_ref[...] = qs_ref[...].astype(BF16)
    kb_ref[...] = ko_ref[...].astype(BF16)
    ones = jnp.ones((seq, HEAD_DIM), BF16)
    vb_ref[:, :HEAD_DIM] = v_ref[...].astype(BF16)
    vb_ref[:, HEAD_DIM:] = ones
    vc_ref[:, HEAD_DIM:] = ones
    for r in range(d_mid):
        cls, dst = pl.ds(r, cls_rows, stride=d_mid), pl.ds(r * cls_rows, cls_rows)
        qc_ref[dst, :] = qs_ref[cls, :].astype(BF16)
        kc_ref[dst, :] = ko_ref[cls, :].astype(BF16)
        vc_ref[dst, :HEAD_DIM] = v_ref[cls, :].astype(BF16)

    n_cls_blk = cls_rows // BAND
    cls_bias = [_class_bias(gap, far_step) for gap in range(min(n_cls_blk, 3))]
    items = [(r, n) for r in range(d_mid) for n in range(n_cls_blk)]
    for g0 in range(0, len(items), ATTN_GROUP):
        group = items[g0:g0 + ATTN_GROUP]
        keys = [pl.ds(r * cls_rows, (n + 1) * BAND) for r, n in group]
        scores = [_dot_nt(qc_ref[pl.ds(r * cls_rows + n * BAND, BAND), :], kc_ref[kr, :])
                  + jnp.concatenate([cls_bias[min(n - nk, 2)] for nk in range(n + 1)], axis=1)
                  for (r, n), kr in zip(group, keys)]
        for (r, n), kr, (m, p) in zip(group, keys, _softmax_stage(scores)):
            rows = pl.ds(r + d_mid * BAND * n, BAND, stride=d_mid)
            pv = jnp.dot(p, vc_ref[kr, :], preferred_element_type=F32)
            acc_ref[rows, :] = pv[:, :HEAD_DIM]
            l_ref[rows, :] = pv[:, HEAD_DIM:]
            m_ref[rows, :] = m

    near_bias = [_near_bias(0), jnp.concatenate([_near_bias(1), _near_bias(0)], axis=1)]
    items = list(range(seq // BAND))
    for g0 in range(0, len(items), ATTN_GROUP):
        group = items[g0:g0 + ATTN_GROUP]
        rows = [pl.ds(n * BAND, BAND) for n in group]
        keys = [pl.ds(max(n - 1, 0) * BAND, BAND * min(n + 1, 2)) for n in group]
        scores = [_dot_nt(qb_ref[rw, :], kb_ref[kr, :]) + near_bias[min(n, 1)]
                  for n, rw, kr in zip(group, rows, keys)]
        probs = _softmax_stage(scores, floors=[m_ref[rw, :] for rw in rows])
        for rw, kr, (m, p) in zip(rows, keys, probs):
            w = jnp.exp2(m_ref[rw, :] - m)
            pv = jnp.dot(p, vb_ref[kr, :], preferred_element_type=F32)
            num = pv[:, :HEAD_DIM] + w * acc_ref[rw, :]
            den = pv[:, HEAD_DIM:] + w * l_ref[rw, :]
            z_ref[rw, :] = ((num / den) * _silu(g_ref[rw, :])).astype(z_ref.dtype)


def _prompt_attention(proj, cos, sin, bsz, seq):
    def col(g0):
        return pl.BlockSpec((None, seq, HEAD_DIM), lambda b, h: (g0 + h, b, 0))

    table = pl.BlockSpec((seq, HEAD_DIM), lambda b, h: (0, 0))
    kv_out = pl.BlockSpec((None, None, seq, HEAD_DIM), lambda b, h: (b, h, 0, 0))
    d_near, d_mid, d_far = DILATIONS
    far_step = d_far // d_mid
    assert d_near == 1 and d_far % d_mid == 0 and far_step & (far_step - 1) == 0
    assert all(w // d == BAND for w, d in zip(WINDOWS, DILATIONS)) and seq % (d_mid * BAND) == 0
    assert seq // d_mid <= far_step * BAND
    return pl.pallas_call(
        _attn_kernel,
        grid=(bsz, N_HEADS_ATTN),
        in_specs=[col(G_QA), col(G_KA), col(G_VA), col(G_GA), table, table],
        out_specs=[pl.BlockSpec((None, seq, HEAD_DIM), lambda b, h: (h, b, 0)), kv_out, kv_out],
        out_shape=[jax.ShapeDtypeStruct((N_HEADS_ATTN, bsz * seq, HEAD_DIM), BF16),
                   jax.ShapeDtypeStruct((bsz, N_HEADS_ATTN, seq, HEAD_DIM), F32),
                   jax.ShapeDtypeStruct((bsz, N_HEADS_ATTN, seq, HEAD_DIM), F32)],
        scratch_shapes=[pltpu.VMEM((seq, HEAD_DIM), F32)]
                       + [pltpu.VMEM((seq, HEAD_DIM), BF16), pltpu.VMEM((seq, HEAD_DIM), BF16),
                          pltpu.VMEM((seq, 2 * HEAD_DIM), BF16)] * 2
                       + [pltpu.VMEM((seq, HEAD_DIM), F32)] * 3,
        compiler_params=_params(("parallel", "parallel")),
        name="prompt_attn",
    )(proj, proj, proj, proj, cos, sin)


def _lower_bound(lb_raw, layer):
    e = jnp.exp(lb_raw - jnp.max(lb_raw, axis=0, keepdims=True))
    sm = e / jnp.sum(e, axis=0, keepdims=True)
    return jnp.sum(sm[:layer + 1], axis=0, keepdims=True)


def _split2(x):
    hi = x.astype(BF16)
    return hi, (x - hi.astype(F32)).astype(BF16)


def _rms_gate(o, norm_g, gate):
    o = o * lax.rsqrt(jnp.mean(o * o, axis=-1, keepdims=True) + RMS_EPS)
    return o * norm_g * _silu(gate)


def _hgrn_gates(fb, lb, tril):
    f = lb + (1.0 - lb) * _sigmoid(fb)
    g = jnp.log(f)
    b = sum(jnp.dot(tril, piece, preferred_element_type=F32) for piece in _split2(g))
    return 1.0 - f, b


def _hgrn_fast_prepare(q_ref, f_ref, lb, qh_ref, kh_ref, el_ref):
    n_heads, seq, _ = q_ref.shape
    blk = HGRN_PREP_ROWS
    tril = _tril_ones(blk)

    def body(bi, carry):
        r0 = pl.multiple_of(bi * blk, blk)
        fb = jnp.concatenate([f_ref[h, pl.ds(r0, blk), :] for h in range(n_heads)], axis=1)
        f = lb + (1.0 - lb) * _sigmoid(fb)
        b_blk = sum(jnp.dot(tril, piece, preferred_element_type=F32) for piece in _split2(jnp.log(f)))
        kk = 1.0 - f
        for c0 in range(0, blk, HGRN_CHUNK):
            rs = slice(c0, c0 + HGRN_CHUNK)
            b = b_blk[rs] - b_blk[c0 - 1:c0] if c0 else b_blk[rs]
            rows = pl.ds(r0 + c0, HGRN_CHUNK)
            q = jnp.concatenate([q_ref[h, rows, :] for h in range(n_heads)], axis=1)
            qh_ref[rows, :] = (q * jnp.exp(b)).astype(BF16)
            kh_ref[rows, :] = (kk[rs] * jnp.exp(-b)).astype(BF16)
            el_ref[pl.ds(bi * (blk // HGRN_CHUNK) + c0 // HGRN_CHUNK, 1), :] = jnp.exp(b[HGRN_CHUNK - 1:])
        return carry

    lax.fori_loop(0, seq // blk, body, 0, unroll=2)


def _hgrn_fast_scan(i_ref, g_ref, norm_g, qh_ref, kh_ref, el_ref, st_ref, z_ref):
    n_heads, seq, _ = i_ref.shape
    c = HGRN_CHUNK
    heads = range(n_heads)
    chunks = range(HGRN_SCAN_CHUNKS)
    lanes = [slice(h * HEAD_DIM, (h + 1) * HEAD_DIM) for h in heads]
    causal = lax.broadcasted_iota(jnp.int32, (c, c), 0) >= lax.broadcasted_iota(jnp.int32, (c, c), 1)
    tn = (((0,), (0,)), ((), ()))

    def body(ti, carry):
        rows = [pl.ds(pl.multiple_of((ti * HGRN_SCAN_CHUNKS + k) * c, c), c) for k in chunks]
        qh = [[qh_ref[rows[k], lanes[h]] for h in heads] for k in chunks]
        kh = [[kh_ref[rows[k], lanes[h]] for h in heads] for k in chunks]
        vb = [[i_ref[h, rows[k], :].astype(BF16) for h in heads] for k in chunks]
        att = [[jnp.where(causal, _dot_nt(qh[k][h], kh[k][h]), 0.0).astype(BF16) for h in heads] for k in chunks]
        ds = [[lax.dot_general(vb[k][h], kh[k][h], tn, preferred_element_type=F32) for h in heads] for k in chunks]
        st = [st_ref[h] for h in heads]
        o = []
        for k in chunks:
            el = el_ref[pl.ds(ti * HGRN_SCAN_CHUNKS + k, 1), :]
            o.append([_dot_nt(qh[k][h], st[h].astype(BF16)) for h in heads])
            st = [(st[h] + ds[k][h]) * el[:, lanes[h]] for h in heads]
        for h in heads:
            st_ref[h] = st[h]
        for k in chunks:
            for h in heads:
                o_kh = o[k][h] + jnp.dot(att[k][h], vb[k][h], preferred_element_type=F32)
                z_ref[h, rows[k], :] = _rms_gate(o_kh, norm_g[:, lanes[h]], g_ref[h, rows[k], :]).astype(z_ref.dtype)
        return carry

    lax.fori_loop(0, seq // (c * HGRN_SCAN_CHUNKS), body, 0)


def _hgrn_chunk(q, fb, v, lb, st, tril):
    c = q.shape[0]
    kk, b = _hgrn_gates(fb, lb, tril)
    o = _dot_nt((q * jnp.exp(b)).astype(BF16), st.astype(BF16))

    s_idx = lax.broadcasted_iota(jnp.int32, (c, 1), 0)
    lane = lax.broadcasted_iota(jnp.int32, (HGRN_SUB, c), 1)
    row = lax.broadcasted_iota(jnp.int32, (HGRN_SUB, c), 0)
    att_rows = []
    for i0 in range(0, c, HGRN_SUB):
        qi = q[i0:i0 + HGRN_SUB]
        bi = b[i0:i0 + HGRN_SUB]
        if i0 > 0:
            bref = b[i0 - 1:i0]
            kt = jnp.where(s_idx < i0, kk * jnp.exp(jnp.minimum(bref - b, 0.0)), 0.0)
            att = _dot_nt((qi * jnp.exp(bi - bref)).astype(BF16), kt.astype(BF16))
        else:
            att = jnp.zeros((HGRN_SUB, c), F32)
        for j in range(HGRN_SUB):
            s = i0 + j
            e = jnp.exp(jnp.minimum(bi - b[s:s + 1], 0.0))
            colv = jnp.sum(qi * kk[s:s + 1] * e, axis=-1, keepdims=True)
            att = jnp.where((lane == s) & (row >= j), colv, att)
        att_rows.append(att)
    att = jnp.concatenate(att_rows, axis=0)
    vb = v.astype(BF16)
    o = o + jnp.dot(att.astype(BF16), vb, preferred_element_type=F32)

    b_last = b[c - 1:c]
    kd = (kk * jnp.exp(b_last - b)).astype(BF16)
    st_new = st * jnp.exp(b_last) + lax.dot_general(vb, kd, (((0,), (0,)), ((), ())),
                                                    preferred_element_type=F32)
    return o, st_new


def _tril_ones(c):
    return (lax.broadcasted_iota(jnp.int32, (c, c), 0) >= lax.broadcasted_iota(jnp.int32, (c, c), 1)
            ).astype(BF16)


def _hgrn_kernel(layer, q_ref, f_ref, i_ref, g_ref, lb_ref, ng_ref, z_ref, s_ref, st_ref, qh_ref, kh_ref, el_ref):
    n_heads, seq, _ = q_ref.shape
    lb = _lower_bound(lb_ref[...], layer)
    norm_g = ng_ref[...]
    st_ref[...] = jnp.zeros_like(st_ref)

    def fast():
        _hgrn_fast_prepare(q_ref, f_ref, lb, qh_ref, kh_ref, el_ref)
        _hgrn_fast_scan(i_ref, g_ref, norm_g, qh_ref, kh_ref, el_ref, st_ref, z_ref)

    def safe():
        tril = _tril_ones(HGRN_CHUNK)

        def body(ci, carry):
            rows = pl.ds(pl.multiple_of(ci * HGRN_CHUNK, HGRN_CHUNK), HGRN_CHUNK)
            for h in range(n_heads):
                lanes = slice(h * HEAD_DIM, (h + 1) * HEAD_DIM)
                o, st = _hgrn_chunk(q_ref[h, rows, :], f_ref[h, rows, :], i_ref[h, rows, :], lb[:, lanes],
                                    st_ref[h], tril)
                st_ref[h] = st
                z_ref[h, rows, :] = _rms_gate(o, norm_g[:, lanes], g_ref[h, rows, :]).astype(z_ref.dtype)
            return carry

        lax.fori_loop(0, seq // HGRN_CHUNK, body, 0)

    fast_ok = HGRN_CHUNK * -jnp.log(jnp.min(lb)) <= HGRN_SAFE_LOG_RANGE
    lax.cond(fast_ok, fast, safe)
    for h in range(n_heads):
        s_ref[h] = st_ref[h].T


def _prompt_hgrn(proj, lb_raw, norm_g, layer, bsz, seq):
    hps = HGRN_HEADS_PER_STEP
    assert N_HEADS_HGRN % hps == 0 and all(g % hps == 0 for g in (G_QB, G_FB, G_IB, G_GB))
    assert seq % (HGRN_CHUNK * HGRN_SCAN_CHUNKS) == 0 and seq % HGRN_PREP_ROWS == 0 and HGRN_PREP_ROWS % HGRN_CHUNK == 0

    def cols(g0):
        return pl.BlockSpec((hps, seq, HEAD_DIM), lambda b, h: (g0 // hps + h, b, 0))

    return pl.pallas_call(
        functools.partial(_hgrn_kernel, layer),
        grid=(bsz, N_HEADS_HGRN // hps),
        in_specs=[cols(G_QB), cols(G_FB), cols(G_IB), cols(G_GB),
                  pl.BlockSpec((DEPTH + 1, hps * HEAD_DIM), lambda b, h: (0, h)),
                  pl.BlockSpec((1, hps * HEAD_DIM), lambda b, h: (0, h))],
        out_specs=[pl.BlockSpec((hps, seq, HEAD_DIM), lambda b, h: (h, b, 0)),
                   pl.BlockSpec((None, hps, HEAD_DIM, HEAD_DIM), lambda b, h: (b, h, 0, 0))],
        out_shape=[jax.ShapeDtypeStruct((N_HEADS_HGRN, bsz * seq, HEAD_DIM), BF16),
                   jax.ShapeDtypeStruct((bsz, N_HEADS_HGRN, HEAD_DIM, HEAD_DIM), F32)],
        scratch_shapes=[pltpu.VMEM((hps, HEAD_DIM, HEAD_DIM), F32),
                        pltpu.VMEM((seq, hps * HEAD_DIM), BF16),
                        pltpu.VMEM((seq, hps * HEAD_DIM), BF16),
                        pltpu.VMEM((seq // HGRN_CHUNK, hps * HEAD_DIM), F32)],
        compiler_params=_params(("parallel", "parallel")),
        name="prompt_hgrn",
    )(proj, proj, proj, proj, lb_raw, norm_g)


MEM_ROWS = 512


def _mem_kernel(q_ref, g_ref, mem_ref, w_ref, z_ref, mk_ref, mv_ref):
    seq = q_ref.shape[1]
    kv = jnp.dot(mem_ref[...].astype(BF16), w_ref[...].astype(BF16), preferred_element_type=F32)
    for h in range(N_HEADS_MEM):
        mk = kv[:, h * HEAD_DIM:(h + 1) * HEAD_DIM]
        mv = kv[:, W_MEM + h * HEAD_DIM:W_MEM + (h + 1) * HEAD_DIM]
        mk_ref[:, h, :] = mk
        mv_ref[:, h, :] = mv
        mk, mv = mk.astype(BF16), mv.astype(BF16)
        for n in range(seq // MEM_ROWS):
            rows = pl.ds(n * MEM_ROWS, MEM_ROWS)
            s = _dot_nt((q_ref[h, rows, :] * ATTN_SCALE).astype(BF16), mk)
            p = jnp.exp(s - jnp.max(s, axis=-1, keepdims=True))
            den = jnp.sum(p, axis=-1, keepdims=True)
            o = jnp.dot(p.astype(BF16), mv, preferred_element_type=F32) / den
            z_ref[h, rows, :] = (o * _silu(g_ref[h, rows, :])).astype(z_ref.dtype)


def _prompt_mem(proj, mem, w_kv, bsz, seq):
    assert G_QM % N_HEADS_MEM == 0 and G_GM % N_HEADS_MEM == 0

    def cols(g0):
        return pl.BlockSpec((N_HEADS_MEM, seq, HEAD_DIM), lambda b: (g0 // N_HEADS_MEM, b, 0))

    kv_out = pl.BlockSpec((None, N_MEM, N_HEADS_MEM, HEAD_DIM), lambda b: (b, 0, 0, 0))
    kv_shape = jax.ShapeDtypeStruct((bsz, N_MEM, N_HEADS_MEM, HEAD_DIM), F32)
    return pl.pallas_call(
        _mem_kernel,
        grid=(bsz,),
        in_specs=[cols(G_QM), cols(G_GM), pl.BlockSpec((N_MEM, D_MODEL), lambda b: (b, 0)),
                  pl.BlockSpec((D_MODEL, 2 * W_MEM), lambda b: (0, 0))],
        out_specs=[pl.BlockSpec((N_HEADS_MEM, seq, HEAD_DIM), lambda b: (0, b, 0)), kv_out, kv_out],
        out_shape=[jax.ShapeDtypeStruct((N_HEADS_MEM, bsz * seq, HEAD_DIM), BF16), kv_shape, kv_shape],
        compiler_params=_params(("parallel",)),
        name="prompt_mem",
    )(proj, proj, mem, w_kv)


def _merge_kernel(za_ref, zh_ref, zm_ref, x_ref, w_ref, lg_ref, lb_ref, o_ref, z_ref):
    c0 = 0
    for ref in (za_ref, zh_ref, zm_ref):
        for c in range(ref.shape[0]):
            z_ref[:, (c0 + c) * LANES:(c0 + c + 1) * LANES] = ref[c].astype(BF16)
        c0 += ref.shape[0]
    y = jnp.dot(z_ref[...], w_ref[...], preferred_element_type=F32)
    r = DEEPNORM_ALPHA * x_ref[...] + y
    mu = jnp.mean(r, axis=-1, keepdims=True)
    d = r - mu
    var = jnp.mean(d * d, axis=-1, keepdims=True)
    o_ref[...] = d * lax.rsqrt(var + LN_EPS) * lg_ref[...] + lb_ref[...]


def _merge(za, zh, zm, x, w_out_bf16, ln_g, ln_b, tm):
    m = x.shape[0]

    def slab(a):
        return pl.BlockSpec((a.shape[0], tm, LANES), lambda i: (0, i, 0))

    const = lambda shape: pl.BlockSpec(shape, lambda i: (0, 0))
    return pl.pallas_call(
        _merge_kernel,
        grid=(m // tm,),
        in_specs=[slab(za), slab(zh), slab(zm), pl.BlockSpec((tm, D_MODEL), lambda i: (i, 0)),
                  const((MIX_WIDTH, D_MODEL)), const((1, D_MODEL)), const((1, D_MODEL))],
        out_specs=pl.BlockSpec((tm, D_MODEL), lambda i: (i, 0)),
        out_shape=jax.ShapeDtypeStruct((m, D_MODEL), F32),
        scratch_shapes=[pltpu.VMEM((tm, MIX_WIDTH), BF16)],
        compiler_params=_params(("parallel",)),
        name="merge",
    )(za, zh, zm, x, w_out_bf16, ln_g, ln_b)


def _column(row):
    return jnp.broadcast_to(row, (HEAD_DIM, HEAD_DIM)).T


HBM_TILE_ROWS = 8


def _window_pieces(past):
    pieces = []
    for dil in DILATIONS:
        if dil == 1:
            pieces.append((0, past - BAND, BAND, None))
        elif dil < HBM_TILE_ROWS:
            n = BAND * dil // HBM_TILE_ROWS
            pieces += [(1, past // HBM_TILE_ROWS - n, n, r) for r in range(0, HBM_TILE_ROWS, dil)]
        else:
            pieces.append((2, past // dil - BAND, BAND, 0))
    return pieces


def _cache_views(cache):
    depth, bsz, past, nh, hd = cache.shape
    assert all(d == 1 or HBM_TILE_ROWS % d == 0 or d % HBM_TILE_ROWS == 0 for d in DILATIONS)
    big = max(DILATIONS)
    hm = cache.transpose(0, 1, 3, 2, 4)
    return (hm, hm.reshape(depth, bsz, nh, past // HBM_TILE_ROWS, HBM_TILE_ROWS, hd),
            hm.reshape(depth, bsz, nh, past // big, big, hd))


def _sample_kernel(layer, past, p_ref, cos_ref, sin_ref, k0_hbm, k1_hbm, k2_hbm, v0_hbm, v1_hbm, v2_hbm,
                   st_ref, mk_ref, mv_ref, lb_ref, ng_ref, z_ref, ko_ref, vo_ref, so_ref, kbuf, vbuf, sem):
    b = pl.program_id(0)
    slot = b % 2
    pieces = _window_pieces(past)

    def window_copies(row, sl):
        out = []
        for ci, (views, buf) in enumerate((((k0_hbm, k1_hbm, k2_hbm), kbuf), ((v0_hbm, v1_hbm, v2_hbm), vbuf))):
            off = 0
            for pi, (vi, start, count, res) in enumerate(pieces):
                view = views[vi]
                src = (view.at[layer, row, :, pl.ds(start, count), :] if res is None
                       else view.at[layer, row, :, pl.ds(start, count), res, :])
                out.append(pltpu.make_async_copy(src, buf.at[sl, :, pl.ds(off, count), :], sem.at[sl, ci, pi]))
                off += count
        return out

    @pl.when(b == 0)
    def _():
        for c in window_copies(b, slot):
            c.start()

    @pl.when(b + 1 < pl.num_programs(0))
    def _():
        for c in window_copies(b + 1, 1 - slot):
            c.start()

    cos = cos_ref[...]
    sin = sin_ref[...]
    q_all = _rope(p_ref[G_QA:G_QA + N_HEADS_ATTN, :], cos, sin) * ATTN_SCALE
    k_all = _rope(p_ref[G_KA:G_KA + N_HEADS_ATTN, :], cos, sin)
    v_all = p_ref[G_VA:G_VA + N_HEADS_ATTN, :]
    ko_ref[...] = k_all
    vo_ref[...] = v_all

    for c in window_copies(b, slot):
        c.wait()

    for h in range(N_HEADS_ATTN):
        q = q_all[h:h + 1]
        s_new = jnp.sum(q * k_all[h:h + 1], axis=-1, keepdims=True)
        s = jnp.sum(kbuf[slot, h] * q, axis=-1, keepdims=True)
        m = jnp.maximum(jnp.max(s, axis=0, keepdims=True), s_new)
        p = jnp.exp(s - m)
        p_new = jnp.exp(s_new - m) * len(DILATIONS)
        den = jnp.sum(p, axis=0, keepdims=True) + p_new
        num = jnp.sum(p * vbuf[slot, h], axis=0, keepdims=True) + p_new * v_all[h:h + 1]
        z_ref[h:h + 1, :] = (num / den) * _silu(p_ref[G_GA + h:G_GA + h + 1, :])

    lb_all = lb_ref[...]
    for h in range(N_HEADS_HGRN):
        lanes = slice(h * HEAD_DIM, (h + 1) * HEAD_DIM)
        lb = _lower_bound(lb_all[:, lanes], layer)
        f = lb + (1.0 - lb) * _sigmoid(p_ref[G_FB + h:G_FB + h + 1, :])
        f_col = _column(f)
        q_col = _column(p_ref[G_QB + h:G_QB + h + 1, :])
        s_new = f_col * st_ref[h] + (1.0 - f_col) * p_ref[G_IB + h:G_IB + h + 1, :]
        so_ref[h] = s_new
        o = jnp.sum(s_new * q_col, axis=0, keepdims=True)
        z_ref[N_HEADS_ATTN + h:N_HEADS_ATTN + h + 1, :] = _rms_gate(
            o, ng_ref[:, lanes], p_ref[G_GB + h:G_GB + h + 1, :])

    fold = mk_ref.shape[1] // N_HEADS_MEM
    unfold = lambda a: sum(a[i * N_HEADS_MEM:(i + 1) * N_HEADS_MEM] for i in range(fold))
    q_m = p_ref[G_QM:G_QM + N_HEADS_MEM, :] * ATTN_SCALE
    s = jnp.sum(mk_ref[...] * jnp.concatenate([q_m] * fold, axis=0)[None], axis=-1, keepdims=True)
    m = jnp.max(s, axis=0)
    m = functools.reduce(jnp.maximum, [m[i * N_HEADS_MEM:(i + 1) * N_HEADS_MEM] for i in range(fold)])
    p = jnp.exp(s - jnp.concatenate([m] * fold, axis=0)[None])
    o = unfold(jnp.sum(p * mv_ref[...], axis=0)) / unfold(jnp.sum(p, axis=0))
    row = N_HEADS_ATTN + N_HEADS_HGRN
    z_ref[row:row + N_HEADS_MEM, :] = o * _silu(p_ref[G_GM:G_GM + N_HEADS_MEM, :])


def _sample_mixers(proj_rows, cos, sin, win_k, win_v, state, mem_k, mem_v, lb_raw, norm_g, layer):
    bsz, past = win_k.shape[1:3]
    assert all(past % d == 0 and past >= w for w, d in zip(WINDOWS, DILATIONS))
    assert all(w // d == BAND for w, d in zip(WINDOWS, DILATIONS)) and past % HBM_TILE_ROWS == 0
    n_keys = BAND * len(DILATIONS)
    n_pieces = len(_window_pieces(past))
    any_spec = pl.BlockSpec(memory_space=pl.ANY)
    vec = lambda n: pl.BlockSpec((n, W_HGRN), lambda b: (0, 0))
    heads = lambda n: pl.BlockSpec((None, n, HEAD_DIM), lambda b: (b, 0, 0))
    fold = HBM_TILE_ROWS // N_HEADS_MEM
    mem_k, mem_v = (a.reshape(a.shape[0], bsz, N_MEM // fold, fold * N_HEADS_MEM, HEAD_DIM) for a in (mem_k, mem_v))
    mem_spec = pl.BlockSpec((None, None, N_MEM // fold, fold * N_HEADS_MEM, HEAD_DIM), lambda b: (layer, b, 0, 0, 0))
    return pl.pallas_call(
        functools.partial(_sample_kernel, layer, past),
        grid=(bsz,),
        in_specs=[heads(N_GROUPS), pl.BlockSpec((1, HEAD_DIM), lambda b: (0, 0)),
                  pl.BlockSpec((1, HEAD_DIM), lambda b: (0, 0)), *([any_spec] * 6),
                  pl.BlockSpec((None, None, N_HEADS_HGRN, HEAD_DIM, HEAD_DIM), lambda b: (layer, b, 0, 0, 0)),
                  mem_spec, mem_spec, vec(DEPTH + 1), vec(1)],
        out_specs=[heads(MIX_WIDTH // HEAD_DIM), heads(N_HEADS_ATTN), heads(N_HEADS_ATTN),
                   pl.BlockSpec((None, N_HEADS_HGRN, HEAD_DIM, HEAD_DIM), lambda b: (b, 0, 0, 0))],
        out_shape=[jax.ShapeDtypeStruct((bsz, MIX_WIDTH // HEAD_DIM, HEAD_DIM), F32),
                   jax.ShapeDtypeStruct((bsz, N_HEADS_ATTN, HEAD_DIM), F32),
                   jax.ShapeDtypeStruct((bsz, N_HEADS_ATTN, HEAD_DIM), F32),
                   jax.ShapeDtypeStruct((bsz, N_HEADS_HGRN, HEAD_DIM, HEAD_DIM), F32)],
        scratch_shapes=[pltpu.VMEM((2, N_HEADS_ATTN, n_keys, HEAD_DIM), F32),
                        pltpu.VMEM((2, N_HEADS_ATTN, n_keys, HEAD_DIM), F32),
                        pltpu.SemaphoreType.DMA((2, 2, n_pieces))],
        compiler_params=_params(("arbitrary",)),
        name="sample_mixers",
    )(proj_rows, cos, sin, *_cache_views(win_k), *_cache_views(win_v), state, mem_k, mem_v, lb_raw, norm_g)


def _rope_tables(pos):
    half = HEAD_DIM // 2
    inv_freq = 1.0 / (ROPE_THETA ** (np.arange(half, dtype=np.float64) / half))
    ang = np.asarray(pos, np.float64)[:, None] * inv_freq[None, :]
    cos, sin = np.cos(ang), np.sin(ang)
    return (jnp.asarray(np.concatenate([cos, cos], axis=-1), F32),
            jnp.asarray(np.concatenate([-sin, sin], axis=-1), F32))


def kernel(x_prompt, x_sample, cache_win_k, cache_win_v, state_hgrn, cache_mem_k, cache_mem_v, mem_prompt,
           w_in, w_mem_kv, hgrn_lb_raw, hgrn_norm_g, w_out, ln_g, ln_b):
    bsz, seq, _ = x_prompt.shape
    dbsz, n_new, _ = x_sample.shape
    assert n_new == 1 and seq % (BAND * max(DILATIONS)) == 0
    cos_p, sin_p = _rope_tables(np.arange(seq))
    cos_s, sin_s = _rope_tables(PAST_LEN + np.arange(n_new))

    hp = x_prompt.reshape(bsz * seq, D_MODEL)
    hs = x_sample.reshape(dbsz * n_new, D_MODEL)
    mem = mem_prompt.reshape(bsz * N_MEM, D_MODEL)
    outs = [[] for _ in range(8)]
    for layer in range(DEPTH):
        w_in_b = w_in[layer].astype(BF16)
        w_out_b = w_out[layer].astype(BF16)
        lb_raw = hgrn_lb_raw
        norm_g = hgrn_norm_g[layer][None]
        lg, lbias = ln_g[layer][None], ln_b[layer][None]

        proj, proj_s = _project(hp, hs, w_in_b, 1024, 1024)
        za, k1, v1 = _prompt_attention(proj, cos_p, sin_p, bsz, seq)
        zh, s1 = _prompt_hgrn(proj, lb_raw, norm_g, layer, bsz, seq)
        zm, mk1, mv1 = _prompt_mem(proj, mem, w_mem_kv[layer], bsz, seq)
        hp = _merge(za, zh, zm, hp, w_out_b, lg, lbias, 512)

        proj_s = proj_s.transpose(1, 0, 2)
        zs, k2, v2, s2 = _sample_mixers(proj_s, cos_s, sin_s, cache_win_k, cache_win_v, state_hgrn,
                                        cache_mem_k, cache_mem_v, lb_raw, norm_g, layer)
        zs = zs.transpose(1, 0, 2)
        hs = _merge(zs[:N_HEADS_ATTN], zs[N_HEADS_ATTN:N_HEADS_ATTN + N_HEADS_HGRN],
                    zs[N_HEADS_ATTN + N_HEADS_HGRN:], hs, w_out_b, lg, lbias, dbsz)

        new = (k1.transpose(0, 2, 1, 3), v1.transpose(0, 2, 1, 3), s1, mk1, mv1,
               k2.reshape(dbsz, n_new, N_HEADS_ATTN, HEAD_DIM), v2.reshape(dbsz, n_new, N_HEADS_ATTN, HEAD_DIM),
               s2.astype(state_hgrn.dtype))
        for acc, val in zip(outs, new):
            acc.append(val)

    return (hp.reshape(bsz, seq, D_MODEL), hs.reshape(dbsz, n_new, D_MODEL), *[jnp.stack(o) for o in outs])
```

```python
import functools

import jax
import jax.numpy as jnp
import numpy as np
from jax import lax
from jax.experimental import pallas as pl
from jax.experimental.pallas import tpu as pltpu

F32 = jnp.float32
BF16 = jnp.bfloat16

D_MODEL = 2048
DEPTH = 1
PAST_LEN = 8192
HEAD_DIM = 128
N_HEADS_ATTN = 6
N_HEADS_HGRN = 6
N_HEADS_MEM = 4
W_ATTN = N_HEADS_ATTN * HEAD_DIM
W_HGRN = N_HEADS_HGRN * HEAD_DIM
W_MEM = N_HEADS_MEM * HEAD_DIM
MIX_WIDTH = W_ATTN + W_HGRN + W_MEM
WINDOWS = (128, 512, 2048)
DILATIONS = (1, 4, 16)
N_MEM = 256
ROPE_THETA = 10000.0
LN_EPS = 1e-5
RMS_EPS = 1e-6
NEG_INF = -1e30
DEEPNORM_ALPHA = (2 * DEPTH) ** 0.25
ATTN_SCALE = HEAD_DIM ** -0.5
ATTN_SCALE_LOG2 = ATTN_SCALE * 1.4426950408889634

G_QA, G_KA, G_VA, G_GA = 0, 6, 12, 18
G_QB, G_FB, G_IB, G_GB = 24, 30, 36, 42
G_QM, G_GM = 48, 52
N_GROUPS = 56

LANES = 128
BAND = 128
ATTN_GROUP = 16
HGRN_CHUNK = 64
HGRN_SUB = 16
HGRN_HEADS_PER_STEP = 3
HGRN_PREP_ROWS = 256
HGRN_SCAN_CHUNKS = 16
HGRN_SAFE_LOG_RANGE = 70.0
VMEM_LIMIT = 48 * 1024 * 1024


def _params(sem, vmem=VMEM_LIMIT):
    return pltpu.CompilerParams(dimension_semantics=sem, vmem_limit_bytes=vmem)


def _sigmoid(x):
    return 1.0 / (1.0 + jnp.exp(-x))


def _silu(x):
    return x * _sigmoid(x)


def _dot_nt(a, b):
    return lax.dot_general(a, b, (((1,), (1,)), ((), ())), preferred_element_type=F32)


def _proj_kernel(x_ref, xs_ref, w_ref, o_ref, os_ref, xb_ref):
    i, j = pl.program_id(0), pl.program_id(1)
    tm = x_ref.shape[0]

    def store(ref, acc):
        for c in range(ref.shape[0]):
            ref[c] = acc[:, c * LANES:(c + 1) * LANES]

    @pl.when(j == 0)
    def _():
        xb_ref[:tm, :] = x_ref[...].astype(BF16)

    @pl.when((i == 0) & (j == 0))
    def _():
        xb_ref[tm:, :] = xs_ref[...].astype(BF16)

    @pl.when(i == 0)
    def _():
        acc = jnp.dot(xb_ref[...], w_ref[...], preferred_element_type=F32)
        store(o_ref, acc[:tm])
        store(os_ref, acc[tm:])

    @pl.when(i > 0)
    def _():
        store(o_ref, jnp.dot(xb_ref[:tm, :], w_ref[...], preferred_element_type=F32))


def _project(x, xs, w_bf16, tm, tn):
    m, k = x.shape
    ms = xs.shape[0]
    n = w_bf16.shape[1]
    n_col = n // tn
    return pl.pallas_call(
        _proj_kernel,
        grid=(m // tm, n_col),
        in_specs=[pl.BlockSpec((tm, k), lambda i, j: (i, 0)),
                  pl.BlockSpec((ms, k), lambda i, j: (0, 0)),
                  pl.BlockSpec((k, tn), lambda i, j: (0, j))],
        out_specs=[pl.BlockSpec((tn // LANES, tm, LANES), lambda i, j: (j, i, 0)),
                   pl.BlockSpec((tn // LANES, ms, LANES), lambda i, j: (jnp.where(i == 0, j, n_col - 1), 0, 0))],
        out_shape=[jax.ShapeDtypeStruct((n // LANES, m, LANES), F32),
                   jax.ShapeDtypeStruct((n // LANES, ms, LANES), F32)],
        scratch_shapes=[pltpu.VMEM((tm + ms, k), BF16)],
        compiler_params=_params(("arbitrary", "arbitrary")),
        name="in_proj",
    )(x, xs, w_bf16)


def _rope(x, cos, sin_signed):
    return x * cos + pltpu.roll(x, HEAD_DIM // 2, 1) * sin_signed


def _bias_from_count(count):
    return jnp.where(count > 1.5, 1.0, jnp.where(count > 0.5, 0.0, NEG_INF)).astype(F32)


def _block_deltas(block_gap):
    qi = lax.broadcasted_iota(jnp.int32, (BAND, BAND), 0)
    ki = lax.broadcasted_iota(jnp.int32, (BAND, BAND), 1)
    return block_gap * BAND + qi - ki


def _near_bias(block_gap):
    d = _block_deltas(block_gap)
    return _bias_from_count(((d >= 0) & (d <= BAND)).astype(F32))


def _class_bias(block_gap, far_step):
    d = _block_deltas(block_gap)
    near = (d >= 0) & (d <= BAND)
    far = (d >= 0) & ((d & (far_step - 1)) == 0)
    return _bias_from_count(near.astype(F32) + far.astype(F32))


def _softmax_stage(scores, floors=None):
    out = []
    for idx, s in enumerate(scores):
        m = jnp.broadcast_to(jnp.max(s, axis=-1, keepdims=True), (BAND, HEAD_DIM))
        if floors is not None:
            m = jnp.maximum(m, floors[idx])
        m_wide = jnp.concatenate([m] * (s.shape[1] // HEAD_DIM), axis=1)
        out.append((m, jnp.exp2(s - m_wide).astype(BF16)))
    return out


def _attn_kernel(q_ref, k_ref, v_ref, g_ref, cos_ref, sin_ref, z_ref, ko_ref, vo_ref,
                 qs_ref, qb_ref, kb_ref, vb_ref, qc_ref, kc_ref, vc_ref, acc_ref, m_ref, l_ref):
    seq = q_ref.shape[0]
    d_mid, d_far = DILATIONS[1], DILATIONS[2]
    far_step = d_far // d_mid
    cls_rows = seq // d_mid
    cos = cos_ref[...]
    sin = sin_ref[...]
    qs_ref[...] = _rope(q_ref[...], cos, sin) * ATTN_SCALE_LOG2
    ko_ref[...] = _rope(k_ref[...], cos, sin)
    vo_ref[...] = v_ref[...]
    qb_ref[...] = qs_ref[...].astype(BF16)
    kb_ref[...] = ko_ref[...].astype(BF16)
    ones = jnp.ones((seq, HEAD_DIM), BF16)
    vb_ref[:, :HEAD_DIM] = v_ref[...].astype(BF16)
    vb_ref[:, HEAD_DIM:] = ones
    vc_ref[:, HEAD_DIM:] = ones
    for r in range(d_mid):
        cls, dst = pl.ds(r, cls_rows, stride=d_mid), pl.ds(r * cls_rows, cls_rows)
        qc_ref[dst, :] = qs_ref[cls, :].astype(BF16)
        kc_ref[dst, :] = ko_ref[cls, :].astype(BF16)
        vc_ref[dst, :HEAD_DIM] = v_ref[cls, :].astype(BF16)

    n_cls_blk = cls_rows // BAND
    cls_bias = [_class_bias(gap, far_step) for gap in range(min(n_cls_blk, 3))]
    items = [(r, n) for r in range(d_mid) for n in range(n_cls_blk)]
    for g0 in range(0, len(items), ATTN_GROUP):
        group = items[g0:g0 + ATTN_GROUP]
        keys = [pl.ds(r * cls_rows, (n + 1) * BAND) for r, n in group]
        scores = [_dot_nt(qc_ref[pl.ds(r * cls_rows + n * BAND, BAND), :], kc_ref[kr, :])
                  + jnp.concatenate([cls_bias[min(n - nk, 2)] for nk in range(n + 1)], axis=1)
                  for (r, n), kr in zip(group, keys)]
        for (r, n), kr, (m, p) in zip(group, keys, _softmax_stage(scores)):
            rows = pl.ds(r + d_mid * BAND * n, BAND, stride=d_mid)
            pv = jnp.dot(p, vc_ref[kr, :], preferred_element_type=F32)
            acc_ref[rows, :] = pv[:, :HEAD_DIM]
            l_ref[rows, :] = pv[:, HEAD_DIM:]
            m_ref[rows, :] = m

    near_bias = [_near_bias(0), jnp.concatenate([_near_bias(1), _near_bias(0)], axis=1)]
    items = list(range(seq // BAND))
    for g0 in range(0, len(items), ATTN_GROUP):
        group = items[g0:g0 + ATTN_GROUP]
        rows = [pl.ds(n * BAND, BAND) for n in group]
        keys = [pl.ds(max(n - 1, 0) * BAND, BAND * min(n + 1, 2)) for n in group]
        scores = [_dot_nt(qb_ref[rw, :], kb_ref[kr, :]) + near_bias[min(n, 1)]
                  for n, rw, kr in zip(group, rows, keys)]
        probs = _softmax_stage(scores, floors=[m_ref[rw, :] for rw in rows])
        for rw, kr, (m, p) in zip(rows, keys, probs):
            w = jnp.exp2(m_ref[rw, :] - m)
            pv = jnp.dot(p, vb_ref[kr, :], preferred_element_type=F32)
            num = pv[:, :HEAD_DIM] + w * acc_ref[rw, :]
            den = pv[:, HEAD_DIM:] + w * l_ref[rw, :]
            z_ref[rw, :] = ((num / den) * _silu(g_ref[rw, :])).astype(z_ref.dtype)


def _prompt_attention(proj, cos, sin, bsz, seq):
    def col(g0):
        return pl.BlockSpec((None, seq, HEAD_DIM), lambda b, h: (g0 + h, b, 0))

    table = pl.BlockSpec((seq, HEAD_DIM), lambda b, h: (0, 0))
    kv_out = pl.BlockSpec((None, None, seq, HEAD_DIM), lambda b, h: (b, h, 0, 0))
    d_near, d_mid, d_far = DILATIONS
    far_step = d_far // d_mid
    assert d_near == 1 and d_far % d_mid == 0 and far_step & (far_step - 1) == 0
    assert all(w // d == BAND for w, d in zip(WINDOWS, DILATIONS)) and seq % (d_mid * BAND) == 0
    assert seq // d_mid <= far_step * BAND
    return pl.pallas_call(
        _attn_kernel,
        grid=(bsz, N_HEADS_ATTN),
        in_specs=[col(G_QA), col(G_KA), col(G_VA), col(G_GA), table, table],
        out_specs=[pl.BlockSpec((None, seq, HEAD_DIM), lambda b, h: (h, b, 0)), kv_out, kv_out],
        out_shape=[jax.ShapeDtypeStruct((N_HEADS_ATTN, bsz * seq, HEAD_DIM), BF16),
                   jax.ShapeDtypeStruct((bsz, N_HEADS_ATTN, seq, HEAD_DIM), F32),
                   jax.ShapeDtypeStruct((bsz, N_HEADS_ATTN, seq, HEAD_DIM), F32)],
        scratch_shapes=[pltpu.VMEM((seq, HEAD_DIM), F32)]
                       + [pltpu.VMEM((seq, HEAD_DIM), BF16), pltpu.VMEM((seq, HEAD_DIM), BF16),
                          pltpu.VMEM((seq, 2 * HEAD_DIM), BF16)] * 2
                       + [pltpu.VMEM((seq, HEAD_DIM), F32)] * 3,
        compiler_params=_params(("parallel", "parallel")),
        name="prompt_attn",
    )(proj, proj, proj, proj, cos, sin)


def _lower_bound(lb_raw, layer):
    e = jnp.exp(lb_raw - jnp.max(lb_raw, axis=0, keepdims=True))
    sm = e / jnp.sum(e, axis=0, keepdims=True)
    return jnp.sum(sm[:layer + 1], axis=0, keepdims=True)


def _split2(x):
    hi = x.astype(BF16)
    return hi, (x - hi.astype(F32)).astype(BF16)


def _rms_gate(o, norm_g, gate):
    o = o * lax.rsqrt(jnp.mean(o * o, axis=-1, keepdims=True) + RMS_EPS)
    return o * norm_g * _silu(gate)


def _hgrn_gates(fb, lb, tril):
    f = lb + (1.0 - lb) * _sigmoid(fb)
    g = jnp.log(f)
    b = sum(jnp.dot(tril, piece, preferred_element_type=F32) for piece in _split2(g))
    return 1.0 - f, b


def _hgrn_fast_prepare(q_ref, f_ref, lb, qh_ref, kh_ref, el_ref):
    n_heads, seq, _ = q_ref.shape
    blk = HGRN_PREP_ROWS
    tril = _tril_ones(blk)

    def body(bi, carry):
        r0 = pl.multiple_of(bi * blk, blk)
        fb = jnp.concatenate([f_ref[h, pl.ds(r0, blk), :] for h in range(n_heads)], axis=1)
        f = lb + (1.0 - lb) * _sigmoid(fb)
        b_blk = sum(jnp.dot(tril, piece, preferred_element_type=F32) for piece in _split2(jnp.log(f)))
        kk = 1.0 - f
        for c0 in range(0, blk, HGRN_CHUNK):
            rs = slice(c0, c0 + HGRN_CHUNK)
            b = b_blk[rs] - b_blk[c0 - 1:c0] if c0 else b_blk[rs]
            rows = pl.ds(r0 + c0, HGRN_CHUNK)
            q = jnp.concatenate([q_ref[h, rows, :] for h in range(n_heads)], axis=1)
            qh_ref[rows, :] = (q * jnp.exp(b)).astype(BF16)
            kh_ref[rows, :] = (kk[rs] * jnp.exp(-b)).astype(BF16)
            el_ref[pl.ds(bi * (blk // HGRN_CHUNK) + c0 // HGRN_CHUNK, 1), :] = jnp.exp(b[HGRN_CHUNK - 1:])
        return carry

    lax.fori_loop(0, seq // blk, body, 0, unroll=4)


def _hgrn_fast_scan(i_ref, g_ref, norm_g, qh_ref, kh_ref, el_ref, st_ref, z_ref):
    n_heads, seq, _ = i_ref.shape
    c = HGRN_CHUNK
    heads = range(n_heads)
    chunks = range(HGRN_SCAN_CHUNKS)
    lanes = [slice(h * HEAD_DIM, (h + 1) * HEAD_DIM) for h in heads]
    causal = lax.broadcasted_iota(jnp.int32, (c, c), 0) >= lax.broadcasted_iota(jnp.int32, (c, c), 1)
    tn = (((0,), (0,)), ((), ()))

    def body(ti, carry):
        rows = [pl.ds(pl.multiple_of((ti * HGRN_SCAN_CHUNKS + k) * c, c), c) for k in chunks]
        qh = [[qh_ref[rows[k], lanes[h]] for h in heads] for k in chunks]
        kh = [[kh_ref[rows[k], lanes[h]] for h in heads] for k in chunks]
        vb = [[i_ref[h, rows[k], :].astype(BF16) for h in heads] for k in chunks]
        att = [[jnp.where(causal, _dot_nt(qh[k][h], kh[k][h]), 0.0).astype(BF16) for h in heads] for k in chunks]
        ds = [[lax.dot_general(vb[k][h], kh[k][h], tn, preferred_element_type=F32) for h in heads] for k in chunks]
        st = [st_ref[h] for h in heads]
        o = []
        for k in chunks:
            el = el_ref[pl.ds(ti * HGRN_SCAN_CHUNKS + k, 1), :]
            o.append([_dot_nt(qh[k][h], st[h].astype(BF16)) for h in heads])
            st = [(st[h] + ds[k][h]) * el[:, lanes[h]] for h in heads]
        for h in heads:
            st_ref[h] = st[h]
        for k in chunks:
            for h in heads:
                o_kh = o[k][h] + jnp.dot(att[k][h], vb[k][h], preferred_element_type=F32)
                z_ref[h, rows[k], :] = _rms_gate(o_kh, norm_g[:, lanes[h]], g_ref[h, rows[k], :]).astype(z_ref.dtype)
        return carry

    lax.fori_loop(0, seq // (c * HGRN_SCAN_CHUNKS), body, 0)


def _hgrn_chunk(q, fb, v, lb, st, tril):
    c = q.shape[0]
    kk, b = _hgrn_gates(fb, lb, tril)
    o = _dot_nt((q * jnp.exp(b)).astype(BF16), st.astype(BF16))

    s_idx = lax.broadcasted_iota(jnp.int32, (c, 1), 0)
    lane = lax.broadcasted_iota(jnp.int32, (HGRN_SUB, c), 1)
    row = lax.broadcasted_iota(jnp.int32, (HGRN_SUB, c), 0)
    att_rows = []
    for i0 in range(0, c, HGRN_SUB):
        qi = q[i0:i0 + HGRN_SUB]
        bi = b[i0:i0 + HGRN_SUB]
        if i0 > 0:
            bref = b[i0 - 1:i0]
            kt = jnp.where(s_idx < i0, kk * jnp.exp(jnp.minimum(bref - b, 0.0)), 0.0)
            att = _dot_nt((qi * jnp.exp(bi - bref)).astype(BF16), kt.astype(BF16))
        else:
            att = jnp.zeros((HGRN_SUB, c), F32)
        for j in range(HGRN_SUB):
            s = i0 + j
            e = jnp.exp(jnp.minimum(bi - b[s:s + 1], 0.0))
            colv = jnp.sum(qi * kk[s:s + 1] * e, axis=-1, keepdims=True)
            att = jnp.where((lane == s) & (row >= j), colv, att)
        att_rows.append(att)
    att = jnp.concatenate(att_rows, axis=0)
    vb = v.astype(BF16)
    o = o + jnp.dot(att.astype(BF16), vb, preferred_element_type=F32)

    b_last = b[c - 1:c]
    kd = (kk * jnp.exp(b_last - b)).astype(BF16)
    st_new = st * jnp.exp(b_last) + lax.dot_general(vb, kd, (((0,), (0,)), ((), ())),
                                                    preferred_element_type=F32)
    return o, st_new


def _tril_ones(c):
    return (lax.broadcasted_iota(jnp.int32, (c, c), 0) >= lax.broadcasted_iota(jnp.int32, (c, c), 1)
            ).astype(BF16)


def _hgrn_kernel(layer, q_ref, f_ref, i_ref, g_ref, lb_ref, ng_ref, z_ref, s_ref, st_ref, qh_ref, kh_ref, el_ref):
    n_heads, seq, _ = q_ref.shape
    lb = _lower_bound(lb_ref[...], layer)
    norm_g = ng_ref[...]
    st_ref[...] = jnp.zeros_like(st_ref)

    def fast():
        _hgrn_fast_prepare(q_ref, f_ref, lb, qh_ref, kh_ref, el_ref)
        _hgrn_fast_scan(i_ref, g_ref, norm_g, qh_ref, kh_ref, el_ref, st_ref, z_ref)

    def safe():
        tril = _tril_ones(HGRN_CHUNK)

        def body(ci, carry):
            rows = pl.ds(pl.multiple_of(ci * HGRN_CHUNK, HGRN_CHUNK), HGRN_CHUNK)
            for h in range(n_heads):
                lanes = slice(h * HEAD_DIM, (h + 1) * HEAD_DIM)
                o, st = _hgrn_chunk(q_ref[h, rows, :], f_ref[h, rows, :], i_ref[h, rows, :], lb[:, lanes],
                                    st_ref[h], tril)
                st_ref[h] = st
                z_ref[h, rows, :] = _rms_gate(o, norm_g[:, lanes], g_ref[h, rows, :]).astype(z_ref.dtype)
            return carry

        lax.fori_loop(0, seq // HGRN_CHUNK, body, 0)

    fast_ok = HGRN_CHUNK * -jnp.log(jnp.min(lb)) <= HGRN_SAFE_LOG_RANGE
    lax.cond(fast_ok, fast, safe)
    for h in range(n_heads):
        s_ref[h] = st_ref[h].T


def _prompt_hgrn(proj, lb_raw, norm_g, layer, bsz, seq):
    hps = HGRN_HEADS_PER_STEP
    assert N_HEADS_HGRN % hps == 0 and all(g % hps == 0 for g in (G_QB, G_FB, G_IB, G_GB))
    assert seq % (HGRN_CHUNK * HGRN_SCAN_CHUNKS) == 0 and seq % HGRN_PREP_ROWS == 0 and HGRN_PREP_ROWS % HGRN_CHUNK == 0

    def cols(g0):
        return pl.BlockSpec((hps, seq, HEAD_DIM), lambda b, h: (g0 // hps + h, b, 0))

    return pl.pallas_call(
        functools.partial(_hgrn_kernel, layer),
        grid=(bsz, N_HEADS_HGRN // hps),
        in_specs=[cols(G_QB), cols(G_FB), cols(G_IB), cols(G_GB),
                  pl.BlockSpec((DEPTH + 1, hps * HEAD_DIM), lambda b, h: (0, h)),
                  pl.BlockSpec((1, hps * HEAD_DIM), lambda b, h: (0, h))],
        out_specs=[pl.BlockSpec((hps, seq, HEAD_DIM), lambda b, h: (h, b, 0)),
                   pl.BlockSpec((None, hps, HEAD_DIM, HEAD_DIM), lambda b, h: (b, h, 0, 0))],
        out_shape=[jax.ShapeDtypeStruct((N_HEADS_HGRN, bsz * seq, HEAD_DIM), BF16),
                   jax.ShapeDtypeStruct((bsz, N_HEADS_HGRN, HEAD_DIM, HEAD_DIM), F32)],
        scratch_shapes=[pltpu.VMEM((hps, HEAD_DIM, HEAD_DIM), F32),
                        pltpu.VMEM((seq, hps * HEAD_DIM), BF16),
                        pltpu.VMEM((seq, hps * HEAD_DIM), BF16),
                        pltpu.VMEM((seq // HGRN_CHUNK, hps * HEAD_DIM), F32)],
        compiler_params=_params(("parallel", "parallel")),
        name="prompt_hgrn",
    )(proj, proj, proj, proj, lb_raw, norm_g)


MEM_ROWS = 256
MEM_GROUP = 8


def _mem_kernel(q_ref, g_ref, mem_ref, w_ref, z_ref, mk_ref, mv_ref):
    seq = q_ref.shape[1]
    kv = jnp.dot(mem_ref[...].astype(BF16), w_ref[...].astype(BF16), preferred_element_type=F32)
    ones = jnp.ones((N_MEM, HEAD_DIM), BF16)
    mk_b, mv_b = [], []
    for h in range(N_HEADS_MEM):
        mk = kv[:, h * HEAD_DIM:(h + 1) * HEAD_DIM]
        mv = kv[:, W_MEM + h * HEAD_DIM:W_MEM + (h + 1) * HEAD_DIM]
        mk_ref[:, h, :] = mk
        mv_ref[:, h, :] = mv
        mk_b.append(mk.astype(BF16))
        mv_b.append(jnp.concatenate([mv.astype(BF16), ones], axis=1))

    items = [(h, pl.ds(n * MEM_ROWS, MEM_ROWS)) for h in range(N_HEADS_MEM) for n in range(seq // MEM_ROWS)]
    for g0 in range(0, len(items), MEM_GROUP):
        group = items[g0:g0 + MEM_GROUP]
        scores = [_dot_nt((q_ref[h, rows, :] * ATTN_SCALE_LOG2).astype(BF16), mk_b[h]) for h, rows in group]
        probs = [jnp.exp2(s - jnp.max(s, axis=-1, keepdims=True)).astype(BF16) for s in scores]
        for (h, rows), p in zip(group, probs):
            pv = jnp.dot(p, mv_b[h], preferred_element_type=F32)
            o = pv[:, :HEAD_DIM] / pv[:, HEAD_DIM:]
            z_ref[h, rows, :] = (o * _silu(g_ref[h, rows, :])).astype(z_ref.dtype)


def _prompt_mem(proj, mem, w_kv, bsz, seq):
    assert G_QM % N_HEADS_MEM == 0 and G_GM % N_HEADS_MEM == 0

    def cols(g0):
        return pl.BlockSpec((N_HEADS_MEM, seq, HEAD_DIM), lambda b: (g0 // N_HEADS_MEM, b, 0))

    kv_out = pl.BlockSpec((None, N_MEM, N_HEADS_MEM, HEAD_DIM), lambda b: (b, 0, 0, 0))
    kv_shape = jax.ShapeDtypeStruct((bsz, N_MEM, N_HEADS_MEM, HEAD_DIM), F32)
    return pl.pallas_call(
        _mem_kernel,
        grid=(bsz,),
        in_specs=[cols(G_QM), cols(G_GM), pl.BlockSpec((N_MEM, D_MODEL), lambda b: (b, 0)),
                  pl.BlockSpec((D_MODEL, 2 * W_MEM), lambda b: (0, 0))],
        out_specs=[pl.BlockSpec((N_HEADS_MEM, seq, HEAD_DIM), lambda b: (0, b, 0)), kv_out, kv_out],
        out_shape=[jax.ShapeDtypeStruct((N_HEADS_MEM, bsz * seq, HEAD_DIM), BF16), kv_shape, kv_shape],
        compiler_params=_params(("parallel",)),
        name="prompt_mem",
    )(proj, proj, mem, w_kv)


def _merge_kernel(za_ref, zh_ref, zm_ref, x_ref, w_ref, lg_ref, lb_ref, o_ref, z_ref):
    c0 = 0
    for ref in (za_ref, zh_ref, zm_ref):
        for c in range(ref.shape[0]):
            z_ref[:, (c0 + c) * LANES:(c0 + c + 1) * LANES] = ref[c].astype(BF16)
        c0 += ref.shape[0]
    y = jnp.dot(z_ref[...], w_ref[...], preferred_element_type=F32)
    r = DEEPNORM_ALPHA * x_ref[...] + y
    mu = jnp.mean(r, axis=-1, keepdims=True)
    d = r - mu
    var = jnp.mean(d * d, axis=-1, keepdims=True)
    o_ref[...] = d * lax.rsqrt(var + LN_EPS) * lg_ref[...] + lb_ref[...]


def _merge(za, zh, zm, x, w_out_bf16, ln_g, ln_b, tm):
    m = x.shape[0]

    def slab(a):
        return pl.BlockSpec((a.shape[0], tm, LANES), lambda i: (0, i, 0))

    const = lambda shape: pl.BlockSpec(shape, lambda i: (0, 0))
    return pl.pallas_call(
        _merge_kernel,
        grid=(m // tm,),
        in_specs=[slab(za), slab(zh), slab(zm), pl.BlockSpec((tm, D_MODEL), lambda i: (i, 0)),
                  const((MIX_WIDTH, D_MODEL)), const((1, D_MODEL)), const((1, D_MODEL))],
        out_specs=pl.BlockSpec((tm, D_MODEL), lambda i: (i, 0)),
        out_shape=jax.ShapeDtypeStruct((m, D_MODEL), F32),
        scratch_shapes=[pltpu.VMEM((tm, MIX_WIDTH), BF16)],
        compiler_params=_params(("parallel",)),
        name="merge",
    )(za, zh, zm, x, w_out_bf16, ln_g, ln_b)


def _column(row):
    return jnp.broadcast_to(row, (HEAD_DIM, HEAD_DIM)).T


HBM_TILE_ROWS = 8
SAMPLE_ROWS = 2


def _window_pieces(past):
    pieces = []
    for dil in DILATIONS:
        if dil == 1:
            pieces.append((0, past - BAND, BAND, None))
        elif dil < HBM_TILE_ROWS:
            n = BAND * dil // HBM_TILE_ROWS
            pieces += [(1, past // HBM_TILE_ROWS - n, n, r) for r in range(0, HBM_TILE_ROWS, dil)]
        else:
            pieces.append((2, past // dil - BAND, BAND, 0))
    return pieces


def _cache_views(cache):
    depth, bsz, past, nh, hd = cache.shape
    assert all(d == 1 or HBM_TILE_ROWS % d == 0 or d % HBM_TILE_ROWS == 0 for d in DILATIONS)
    big = max(DILATIONS)
    hm = cache.transpose(0, 1, 3, 2, 4)
    return (hm, hm.reshape(depth, bsz, nh, past // HBM_TILE_ROWS, HBM_TILE_ROWS, hd),
            hm.reshape(depth, bsz, nh, past // big, big, hd))


def _sample_row(layer, p_ref, cos, sin, k_win, v_win, st_ref, mk_ref, mv_ref, lb_all, ng_ref, z_ref, ko_ref, vo_ref,
                so_ref):
    q_all = _rope(p_ref[G_QA:G_QA + N_HEADS_ATTN, :], cos, sin) * ATTN_SCALE
    k_all = _rope(p_ref[G_KA:G_KA + N_HEADS_ATTN, :], cos, sin)
    v_all = p_ref[G_VA:G_VA + N_HEADS_ATTN, :]
    ko_ref[...] = k_all
    vo_ref[...] = v_all

    for h in range(N_HEADS_ATTN):
        q = q_all[h:h + 1]
        s_new = jnp.sum(q * k_all[h:h + 1], axis=-1, keepdims=True)
        s = jnp.sum(k_win[h] * q, axis=-1, keepdims=True)
        m = jnp.maximum(jnp.max(s, axis=0, keepdims=True), s_new)
        p = jnp.exp(s - m)
        p_new = jnp.exp(s_new - m) * len(DILATIONS)
        den = jnp.sum(p, axis=0, keepdims=True) + p_new
        num = jnp.sum(p * v_win[h], axis=0, keepdims=True) + p_new * v_all[h:h + 1]
        z_ref[h:h + 1, :] = (num / den) * _silu(p_ref[G_GA + h:G_GA + h + 1, :])

    for h in range(N_HEADS_HGRN):
        lanes = slice(h * HEAD_DIM, (h + 1) * HEAD_DIM)
        lb = _lower_bound(lb_all[:, lanes], layer)
        f = lb + (1.0 - lb) * _sigmoid(p_ref[G_FB + h:G_FB + h + 1, :])
        f_col = _column(f)
        q_col = _column(p_ref[G_QB + h:G_QB + h + 1, :])
        s_new = f_col * st_ref[h] + (1.0 - f_col) * p_ref[G_IB + h:G_IB + h + 1, :]
        so_ref[h] = s_new
        o = jnp.sum(s_new * q_col, axis=0, keepdims=True)
        z_ref[N_HEADS_ATTN + h:N_HEADS_ATTN + h + 1, :] = _rms_gate(
            o, ng_ref[:, lanes], p_ref[G_GB + h:G_GB + h + 1, :])

    fold = mk_ref.shape[1] // N_HEADS_MEM
    unfold = lambda a: sum(a[i * N_HEADS_MEM:(i + 1) * N_HEADS_MEM] for i in range(fold))
    q_m = p_ref[G_QM:G_QM + N_HEADS_MEM, :] * ATTN_SCALE
    s = jnp.sum(mk_ref[...] * jnp.concatenate([q_m] * fold, axis=0)[None], axis=-1, keepdims=True)
    m = jnp.max(s, axis=0)
    m = functools.reduce(jnp.maximum, [m[i * N_HEADS_MEM:(i + 1) * N_HEADS_MEM] for i in range(fold)])
    p = jnp.exp(s - jnp.concatenate([m] * fold, axis=0)[None])
    o = unfold(jnp.sum(p * mv_ref[...], axis=0)) / unfold(jnp.sum(p, axis=0))
    row = N_HEADS_ATTN + N_HEADS_HGRN
    z_ref[row:row + N_HEADS_MEM, :] = o * _silu(p_ref[G_GM:G_GM + N_HEADS_MEM, :])


def _sample_kernel(layer, past, p_ref, cos_ref, sin_ref, k0_hbm, k1_hbm, k2_hbm, v0_hbm, v1_hbm, v2_hbm,
                   st_ref, mk_ref, mv_ref, lb_ref, ng_ref, z_ref, ko_ref, vo_ref, so_ref, kbuf, vbuf, sem):
    step = pl.program_id(0)
    slot = step % 2
    n_rows = p_ref.shape[0]
    pieces = _window_pieces(past)

    def window_copies(stp, sl):
        out = []
        for rr in range(n_rows):
            for ci, (views, buf) in enumerate((((k0_hbm, k1_hbm, k2_hbm), kbuf), ((v0_hbm, v1_hbm, v2_hbm), vbuf))):
                off = 0
                for pi, (vi, start, count, res) in enumerate(pieces):
                    view, row = views[vi], stp * n_rows + rr
                    src = (view.at[layer, row, :, pl.ds(start, count), :] if res is None
                           else view.at[layer, row, :, pl.ds(start, count), res, :])
                    out.append(pltpu.make_async_copy(src, buf.at[sl, rr, :, pl.ds(off, count), :],
                                                     sem.at[sl, rr, ci, pi]))
                    off += count
        return out

    @pl.when(step == 0)
    def _():
        for c in window_copies(step, slot):
            c.start()

    @pl.when(step + 1 < pl.num_programs(0))
    def _():
        for c in window_copies(step + 1, 1 - slot):
            c.start()

    for c in window_copies(step, slot):
        c.wait()

    cos, sin, lb_all = cos_ref[...], sin_ref[...], lb_ref[...]
    for rr in range(n_rows):
        _sample_row(layer, p_ref.at[rr], cos, sin, kbuf.at[slot, rr], vbuf.at[slot, rr], st_ref.at[rr], mk_ref.at[rr],
                    mv_ref.at[rr], lb_all, ng_ref, z_ref.at[rr], ko_ref.at[rr], vo_ref.at[rr], so_ref.at[rr])


def _sample_mixers(proj_rows, cos, sin, win_k, win_v, state, mem_k, mem_v, lb_raw, norm_g, layer):
    bsz, past = win_k.shape[1:3]
    rps = SAMPLE_ROWS
    assert all(past % d == 0 and past >= w for w, d in zip(WINDOWS, DILATIONS)) and bsz % rps == 0
    assert all(w // d == BAND for w, d in zip(WINDOWS, DILATIONS)) and past % HBM_TILE_ROWS == 0
    n_keys = BAND * len(DILATIONS)
    n_pieces = len(_window_pieces(past))
    any_spec = pl.BlockSpec(memory_space=pl.ANY)
    vec = lambda n: pl.BlockSpec((n, W_HGRN), lambda b: (0, 0))
    heads = lambda n: pl.BlockSpec((rps, n, HEAD_DIM), lambda b: (b, 0, 0))
    fold = HBM_TILE_ROWS // N_HEADS_MEM
    mem_k, mem_v = (a.reshape(a.shape[0], bsz, N_MEM // fold, fold * N_HEADS_MEM, HEAD_DIM) for a in (mem_k, mem_v))
    mem_spec = pl.BlockSpec((None, rps, N_MEM // fold, fold * N_HEADS_MEM, HEAD_DIM), lambda b: (layer, b, 0, 0, 0))
    return pl.pallas_call(
        functools.partial(_sample_kernel, layer, past),
        grid=(bsz // rps,),
        in_specs=[heads(N_GROUPS), pl.BlockSpec((1, HEAD_DIM), lambda b: (0, 0)),
                  pl.BlockSpec((1, HEAD_DIM), lambda b: (0, 0)), *([any_spec] * 6),
                  pl.BlockSpec((None, rps, N_HEADS_HGRN, HEAD_DIM, HEAD_DIM), lambda b: (layer, b, 0, 0, 0)),
                  mem_spec, mem_spec, vec(DEPTH + 1), vec(1)],
        out_specs=[heads(MIX_WIDTH // HEAD_DIM), heads(N_HEADS_ATTN), heads(N_HEADS_ATTN),
                   pl.BlockSpec((rps, N_HEADS_HGRN, HEAD_DIM, HEAD_DIM), lambda b: (b, 0, 0, 0))],
        out_shape=[jax.ShapeDtypeStruct((bsz, MIX_WIDTH // HEAD_DIM, HEAD_DIM), F32),
                   jax.ShapeDtypeStruct((bsz, N_HEADS_ATTN, HEAD_DIM), F32),
                   jax.ShapeDtypeStruct((bsz, N_HEADS_ATTN, HEAD_DIM), F32),
                   jax.ShapeDtypeStruct((bsz, N_HEADS_HGRN, HEAD_DIM, HEAD_DIM), F32)],
        scratch_shapes=[pltpu.VMEM((2, rps, N_HEADS_ATTN, n_keys, HEAD_DIM), F32),
                        pltpu.VMEM((2, rps, N_HEADS_ATTN, n_keys, HEAD_DIM), F32),
                        pltpu.SemaphoreType.DMA((2, rps, 2, n_pieces))],
        compiler_params=_params(("arbitrary",)),
        name="sample_mixers",
    )(proj_rows, cos, sin, *_cache_views(win_k), *_cache_views(win_v), state, mem_k, mem_v, lb_raw, norm_g)


def _rope_tables(pos):
    half = HEAD_DIM // 2
    inv_freq = 1.0 / (ROPE_THETA ** (np.arange(half, dtype=np.float64) / half))
    ang = np.asarray(pos, np.float64)[:, None] * inv_freq[None, :]
    cos, sin = np.cos(ang), np.sin(ang)
    return (jnp.asarray(np.concatenate([cos, cos], axis=-1), F32),
            jnp.asarray(np.concatenate([-sin, sin], axis=-1), F32))


def kernel(x_prompt, x_sample, cache_win_k, cache_win_v, state_hgrn, cache_mem_k, cache_mem_v, mem_prompt,
           w_in, w_mem_kv, hgrn_lb_raw, hgrn_norm_g, w_out, ln_g, ln_b):
    bsz, seq, _ = x_prompt.shape
    dbsz, n_new, _ = x_sample.shape
    assert n_new == 1 and seq % (BAND * max(DILATIONS)) == 0
    cos_p, sin_p = _rope_tables(np.arange(seq))
    cos_s, sin_s = _rope_tables(PAST_LEN + np.arange(n_new))

    hp = x_prompt.reshape(bsz * seq, D_MODEL)
    hs = x_sample.reshape(dbsz * n_new, D_MODEL)
    mem = mem_prompt.reshape(bsz * N_MEM, D_MODEL)
    outs = [[] for _ in range(8)]
    for layer in range(DEPTH):
        w_in_b = w_in[layer].astype(BF16)
        w_out_b = w_out[layer].astype(BF16)
        lb_raw = hgrn_lb_raw
        norm_g = hgrn_norm_g[layer][None]
        lg, lbias = ln_g[layer][None], ln_b[layer][None]

        proj, proj_s = _project(hp, hs, w_in_b, 1024, 1024)
        za, k1, v1 = _prompt_attention(proj, cos_p, sin_p, bsz, seq)
        zh, s1 = _prompt_hgrn(proj, lb_raw, norm_g, layer, bsz, seq)
        zm, mk1, mv1 = _prompt_mem(proj, mem, w_mem_kv[layer], bsz, seq)
        hp = _merge(za, zh, zm, hp, w_out_b, lg, lbias, 512)

        proj_s = proj_s.transpose(1, 0, 2)
        zs, k2, v2, s2 = _sample_mixers(proj_s, cos_s, sin_s, cache_win_k, cache_win_v, state_hgrn,
                                        cache_mem_k, cache_mem_v, lb_raw, norm_g, layer)
        zs = zs.transpose(1, 0, 2)
        hs = _merge(zs[:N_HEADS_ATTN], zs[N_HEADS_ATTN:N_HEADS_ATTN + N_HEADS_HGRN],
                    zs[N_HEADS_ATTN + N_HEADS_HGRN:], hs, w_out_b, lg, lbias, dbsz)

        new = (k1.transpose(0, 2, 1, 3), v1.transpose(0, 2, 1, 3), s1, mk1, mv1,
               k2.reshape(dbsz, n_new, N_HEADS_ATTN, HEAD_DIM), v2.reshape(dbsz, n_new, N_HEADS_ATTN, HEAD_DIM),
               s2.astype(state_hgrn.dtype))
        for acc, val in zip(outs, new):
            acc.append(val)

    return (hp.reshape(bsz, seq, D_MODEL), hs.reshape(dbsz, n_new, D_MODEL), *[jnp.stack(o) for o in outs])
```

```python
import functools

import jax
import jax.numpy as jnp
import numpy as np
from jax import lax
from jax.experimental import pallas as pl
from jax.experimental.pallas import tpu as pltpu

F32 = jnp.float32
BF16 = jnp.bfloat16

D_MODEL = 2048
DEPTH = 1
PAST_LEN = 8192
HEAD_DIM = 128
N_HEADS_ATTN = 6
N_HEADS_HGRN = 6
N_HEADS_MEM = 4
W_ATTN = N_HEADS_ATTN * HEAD_DIM
W_HGRN = N_HEADS_HGRN * HEAD_DIM
W_MEM = N_HEADS_MEM * HEAD_DIM
MIX_WIDTH = W_ATTN + W_HGRN + W_MEM
WINDOWS = (128, 512, 2048)
DILATIONS = (1, 4, 16)
N_MEM = 256
ROPE_THETA = 10000.0
LN_EPS = 1e-5
RMS_EPS = 1e-6
NEG_INF = -1e30
DEEPNORM_ALPHA = (2 * DEPTH) ** 0.25
ATTN_SCALE = HEAD_DIM ** -0.5
ATTN_SCALE_LOG2 = ATTN_SCALE * 1.4426950408889634

G_QA, G_KA, G_VA, G_GA = 0, 6, 12, 18
G_QB, G_FB, G_IB, G_GB = 24, 30, 36, 42
G_QM, G_GM = 48, 52
N_GROUPS = 56

LANES = 128
BAND = 128
ATTN_GROUP = 16
HGRN_CHUNK = 64
HGRN_SUB = 16
HGRN_HEADS_PER_STEP = 3
HGRN_PREP_ROWS = 256
HGRN_SCAN_CHUNKS = 16
HGRN_SAFE_LOG_RANGE = 70.0
VMEM_LIMIT = 48 * 1024 * 1024


def _params(sem, vmem=VMEM_LIMIT):
    return pltpu.CompilerParams(dimension_semantics=sem, vmem_limit_bytes=vmem)


def _sigmoid(x):
    return 1.0 / (1.0 + jnp.exp(-x))


def _silu(x):
    return x * _sigmoid(x)


def _dot_nt(a, b):
    return lax.dot_general(a, b, (((1,), (1,)), ((), ())), preferred_element_type=F32)


def _store_slabs(ref, acc):
    for c in range(ref.shape[0]):
        ref[c] = acc[:, c * LANES:(c + 1) * LANES]


def _proj_first_kernel(x_ref, xs_ref, w_ref, o_ref, os_ref, wb_ref, xb_ref):
    tm = x_ref.shape[0]

    @pl.when(pl.program_id(0) == 0)
    def _():
        xb_ref[:tm, :] = x_ref[...].astype(BF16)
        xb_ref[tm:, :] = xs_ref[...].astype(BF16)

    wb_ref[...] = w_ref[...].astype(BF16)
    acc = jnp.dot(xb_ref[...], wb_ref[...], preferred_element_type=F32)
    _store_slabs(o_ref, acc[:tm])
    _store_slabs(os_ref, acc[tm:])


def _proj_rest_kernel(x_ref, w_ref, proj_hbm, o_ref, xb_ref):
    del proj_hbm

    @pl.when(pl.program_id(1) == 0)
    def _():
        xb_ref[...] = x_ref[...].astype(BF16)

    _store_slabs(o_ref, jnp.dot(xb_ref[...], w_ref[...], preferred_element_type=F32))


def _project(x, xs, w, tm, tn_first, tn):
    m, k = x.shape
    ms = xs.shape[0]
    n = w.shape[1]
    proj, proj_s, w_bf16 = pl.pallas_call(
        _proj_first_kernel,
        grid=(n // tn_first,),
        in_specs=[pl.BlockSpec((tm, k), lambda j: (0, 0)),
                  pl.BlockSpec((ms, k), lambda j: (0, 0)),
                  pl.BlockSpec((k, tn_first), lambda j: (0, j))],
        out_specs=[pl.BlockSpec((tn_first // LANES, tm, LANES), lambda j: (j, 0, 0)),
                   pl.BlockSpec((tn_first // LANES, ms, LANES), lambda j: (j, 0, 0)),
                   pl.BlockSpec((k, tn_first), lambda j: (0, j))],
        out_shape=[jax.ShapeDtypeStruct((n // LANES, m, LANES), F32),
                   jax.ShapeDtypeStruct((n // LANES, ms, LANES), F32),
                   jax.ShapeDtypeStruct((k, n), BF16)],
        scratch_shapes=[pltpu.VMEM((tm + ms, k), BF16)],
        compiler_params=_params(("arbitrary",)),
        name="in_proj_first",
    )(x, xs, w)
    proj = pl.pallas_call(
        _proj_rest_kernel,
        grid=(m // tm - 1, n // tn),
        in_specs=[pl.BlockSpec((tm, k), lambda i, j: (i + 1, 0)),
                  pl.BlockSpec((k, tn), lambda i, j: (0, j)),
                  pl.BlockSpec(memory_space=pl.ANY)],
        out_specs=pl.BlockSpec((tn // LANES, tm, LANES), lambda i, j: (j, i + 1, 0)),
        out_shape=jax.ShapeDtypeStruct((n // LANES, m, LANES), F32),
        input_output_aliases={2: 0},
        scratch_shapes=[pltpu.VMEM((tm, k), BF16)],
        compiler_params=_params(("arbitrary", "arbitrary")),
        name="in_proj",
    )(x, w_bf16, proj)
    return proj, proj_s


def _rope(x, cos, sin_signed):
    return x * cos + pltpu.roll(x, HEAD_DIM // 2, 1) * sin_signed


def _bias_from_count(count):
    return jnp.where(count > 1.5, 1.0, jnp.where(count > 0.5, 0.0, NEG_INF)).astype(F32)


def _block_deltas(block_gap):
    qi = lax.broadcasted_iota(jnp.int32, (BAND, BAND), 0)
    ki = lax.broadcasted_iota(jnp.int32, (BAND, BAND), 1)
    return block_gap * BAND + qi - ki


def _near_bias(block_gap):
    d = _block_deltas(block_gap)
    return _bias_from_count(((d >= 0) & (d <= BAND)).astype(F32))


def _class_bias(block_gap, far_step):
    d = _block_deltas(block_gap)
    near = (d >= 0) & (d <= BAND)
    far = (d >= 0) & ((d & (far_step - 1)) == 0)
    return _bias_from_count(near.astype(F32) + far.astype(F32))


def _softmax_stage(scores, floors=None):
    out = []
    for idx, s in enumerate(scores):
        m = jnp.broadcast_to(jnp.max(s, axis=-1, keepdims=True), (BAND, HEAD_DIM))
        if floors is not None:
            m = jnp.maximum(m, floors[idx])
        m_wide = jnp.concatenate([m] * (s.shape[1] // HEAD_DIM), axis=1)
        out.append((m, jnp.exp2(s - m_wide).astype(BF16)))
    return out


def _attn_kernel(q_ref, k_ref, v_ref, g_ref, cos_ref, sin_ref, z_ref, ko_ref, vo_ref,
                 qs_ref, qb_ref, kb_ref, vb_ref, qc_ref, kc_ref, vc_ref, acc_ref, m_ref, l_ref):
    seq = q_ref.shape[0]
    d_mid, d_far = DILATIONS[1], DILATIONS[2]
    far_step = d_far // d_mid
    cls_rows = seq // d_mid
    cos = cos_ref[...]
    sin = sin_ref[...]
    qs_ref[...] = _rope(q_ref[...], cos, sin) * ATTN_SCALE_LOG2
    ko_ref[...] = _rope(k_ref[...], cos, sin)
    vo_ref[...] = v_ref[...]
    qb_ref[...] = qs_ref[...].astype(BF16)
    kb_ref[...] = ko_ref[...].astype(BF16)
    ones = jnp.ones((seq, HEAD_DIM), BF16)
    vb_ref[:, :HEAD_DIM] = v_ref[...].astype(BF16)
    vb_ref[:, HEAD_DIM:] = ones
    vc_ref[:, HEAD_DIM:] = ones
    for r in range(d_mid):
        cls, dst = pl.ds(r, cls_rows, stride=d_mid), pl.ds(r * cls_rows, cls_rows)
        qc_ref[dst, :] = qs_ref[cls, :].astype(BF16)
        kc_ref[dst, :] = ko_ref[cls, :].astype(BF16)
        vc_ref[dst, :HEAD_DIM] = v_ref[cls, :].astype(BF16)

    n_cls_blk = cls_rows // BAND
    cls_bias = [_class_bias(gap, far_step) for gap in range(min(n_cls_blk, 3))]
    items = [(r, n) for r in range(d_mid) for n in range(n_cls_blk)]
    for g0 in range(0, len(items), ATTN_GROUP):
        group = items[g0:g0 + ATTN_GROUP]
        keys = [pl.ds(r * cls_rows, (n + 1) * BAND) for r, n in group]
        scores = [_dot_nt(qc_ref[pl.ds(r * cls_rows + n * BAND, BAND), :], kc_ref[kr, :])
                  + jnp.concatenate([cls_bias[min(n - nk, 2)] for nk in range(n + 1)], axis=1)
                  for (r, n), kr in zip(group, keys)]
        for (r, n), kr, (m, p) in zip(group, keys, _softmax_stage(scores)):
            rows = pl.ds(r + d_mid * BAND * n, BAND, stride=d_mid)
            pv = jnp.dot(p, vc_ref[kr, :], preferred_element_type=F32)
            acc_ref[rows, :] = pv[:, :HEAD_DIM]
            l_ref[rows, :] = pv[:, HEAD_DIM:]
            m_ref[rows, :] = m

    near_bias = [_near_bias(0), jnp.concatenate([_near_bias(1), _near_bias(0)], axis=1)]
    items = list(range(seq // BAND))
    for g0 in range(0, len(items), ATTN_GROUP):
        group = items[g0:g0 + ATTN_GROUP]
        rows = [pl.ds(n * BAND, BAND) for n in group]
        keys = [pl.ds(max(n - 1, 0) * BAND, BAND * min(n + 1, 2)) for n in group]
        scores = [_dot_nt(qb_ref[rw, :], kb_ref[kr, :]) + near_bias[min(n, 1)]
                  for n, rw, kr in zip(group, rows, keys)]
        probs = _softmax_stage(scores, floors=[m_ref[rw, :] for rw in rows])
        for rw, kr, (m, p) in zip(rows, keys, probs):
            w = jnp.exp2(m_ref[rw, :] - m)
            pv = jnp.dot(p, vb_ref[kr, :], preferred_element_type=F32)
            num = pv[:, :HEAD_DIM] + w * acc_ref[rw, :]
            den = pv[:, HEAD_DIM:] + w * l_ref[rw, :]
            z_ref[rw, :] = ((num / den) * _silu(g_ref[rw, :])).astype(z_ref.dtype)


def _prompt_attention(proj, cos, sin, bsz, seq):
    def col(g0):
        return pl.BlockSpec((None, seq, HEAD_DIM), lambda b, h: (g0 + h, b, 0))

    table = pl.BlockSpec((seq, HEAD_DIM), lambda b, h: (0, 0))
    kv_out = pl.BlockSpec((None, None, seq, HEAD_DIM), lambda b, h: (b, h, 0, 0))
    d_near, d_mid, d_far = DILATIONS
    far_step = d_far // d_mid
    assert d_near == 1 and d_far % d_mid == 0 and far_step & (far_step - 1) == 0
    assert all(w // d == BAND for w, d in zip(WINDOWS, DILATIONS)) and seq % (d_mid * BAND) == 0
    assert seq // d_mid <= far_step * BAND
    return pl.pallas_call(
        _attn_kernel,
        grid=(bsz, N_HEADS_ATTN),
        in_specs=[col(G_QA), col(G_KA), col(G_VA), col(G_GA), table, table],
        out_specs=[pl.BlockSpec((None, seq, HEAD_DIM), lambda b, h: (h, b, 0)), kv_out, kv_out],
        out_shape=[jax.ShapeDtypeStruct((N_HEADS_ATTN, bsz * seq, HEAD_DIM), BF16),
                   jax.ShapeDtypeStruct((bsz, N_HEADS_ATTN, seq, HEAD_DIM), F32),
                   jax.ShapeDtypeStruct((bsz, N_HEADS_ATTN, seq, HEAD_DIM), F32)],
        scratch_shapes=[pltpu.VMEM((seq, HEAD_DIM), F32)]
                       + [pltpu.VMEM((seq, HEAD_DIM), BF16), pltpu.VMEM((seq, HEAD_DIM), BF16),
                          pltpu.VMEM((seq, 2 * HEAD_DIM), BF16)] * 2
                       + [pltpu.VMEM((seq, HEAD_DIM), F32)] * 3,
        compiler_params=_params(("parallel", "parallel")),
        name="prompt_attn",
    )(proj, proj, proj, proj, cos, sin)


def _lower_bound(lb_raw, layer):
    e = jnp.exp(lb_raw - jnp.max(lb_raw, axis=0, keepdims=True))
    sm = e / jnp.sum(e, axis=0, keepdims=True)
    return jnp.sum(sm[:layer + 1], axis=0, keepdims=True)


def _split2(x):
    hi = x.astype(BF16)
    return hi, (x - hi.astype(F32)).astype(BF16)


def _rms_gate(o, norm_g, gate):
    o = o * lax.rsqrt(jnp.mean(o * o, axis=-1, keepdims=True) + RMS_EPS)
    return o * norm_g * _silu(gate)


def _hgrn_gates(fb, lb, tril):
    f = lb + (1.0 - lb) * _sigmoid(fb)
    g = jnp.log(f)
    b = sum(jnp.dot(tril, piece, preferred_element_type=F32) for piece in _split2(g))
    return 1.0 - f, b


def _hgrn_fast_prepare(q_ref, f_ref, lb, qh_ref, kh_ref, el_ref):
    n_heads, seq, _ = q_ref.shape
    blk = HGRN_PREP_ROWS
    tril = _tril_ones(blk)

    def body(bi, carry):
        r0 = pl.multiple_of(bi * blk, blk)
        fb = jnp.concatenate([f_ref[h, pl.ds(r0, blk), :] for h in range(n_heads)], axis=1)
        f = lb + (1.0 - lb) * _sigmoid(fb)
        b_blk = sum(jnp.dot(tril, piece, preferred_element_type=F32) for piece in _split2(jnp.log(f)))
        kk = 1.0 - f
        for c0 in range(0, blk, HGRN_CHUNK):
            rs = slice(c0, c0 + HGRN_CHUNK)
            b = b_blk[rs] - b_blk[c0 - 1:c0] if c0 else b_blk[rs]
            rows = pl.ds(r0 + c0, HGRN_CHUNK)
            q = jnp.concatenate([q_ref[h, rows, :] for h in range(n_heads)], axis=1)
            qh_ref[rows, :] = (q * jnp.exp(b)).astype(BF16)
            kh_ref[rows, :] = (kk[rs] * jnp.exp(-b)).astype(BF16)
            el_ref[pl.ds(bi * (blk // HGRN_CHUNK) + c0 // HGRN_CHUNK, 1), :] = jnp.exp(b[HGRN_CHUNK - 1:])
        return carry

    lax.fori_loop(0, seq // blk, body, 0, unroll=4)


def _hgrn_fast_scan(i_ref, g_ref, norm_g, qh_ref, kh_ref, el_ref, st_ref, z_ref):
    n_heads, seq, _ = i_ref.shape
    c = HGRN_CHUNK
    heads = range(n_heads)
    chunks = range(HGRN_SCAN_CHUNKS)
    lanes = [slice(h * HEAD_DIM, (h + 1) * HEAD_DIM) for h in heads]
    causal = lax.broadcasted_iota(jnp.int32, (c, c), 0) >= lax.broadcasted_iota(jnp.int32, (c, c), 1)
    tn = (((0,), (0,)), ((), ()))

    def body(ti, carry):
        rows = [pl.ds(pl.multiple_of((ti * HGRN_SCAN_CHUNKS + k) * c, c), c) for k in chunks]
        qh = [[qh_ref[rows[k], lanes[h]] for h in heads] for k in chunks]
        kh = [[kh_ref[rows[k], lanes[h]] for h in heads] for k in chunks]
        vb = [[i_ref[h, rows[k], :].astype(BF16) for h in heads] for k in chunks]
        att = [[jnp.where(causal, _dot_nt(qh[k][h], kh[k][h]), 0.0).astype(BF16) for h in heads] for k in chunks]
        ds = [[lax.dot_general(vb[k][h], kh[k][h], tn, preferred_element_type=F32) for h in heads] for k in chunks]
        st = [st_ref[h] for h in heads]
        o = []
        for k in chunks:
            el = el_ref[pl.ds(ti * HGRN_SCAN_CHUNKS + k, 1), :]
            o.append([_dot_nt(qh[k][h], st[h].astype(BF16)) for h in heads])
            st = [(st[h] + ds[k][h]) * el[:, lanes[h]] for h in heads]
        for h in heads:
            st_ref[h] = st[h]
        for k in chunks:
            for h in heads:
                o_kh = o[k][h] + jnp.dot(att[k][h], vb[k][h], preferred_element_type=F32)
                z_ref[h, rows[k], :] = _rms_gate(o_kh, norm_g[:, lanes[h]], g_ref[h, rows[k], :]).astype(z_ref.dtype)
        return carry

    lax.fori_loop(0, seq // (c * HGRN_SCAN_CHUNKS), body, 0)


def _hgrn_chunk(q, fb, v, lb, st, tril):
    c = q.shape[0]
    kk, b = _hgrn_gates(fb, lb, tril)
    o = _dot_nt((q * jnp.exp(b)).astype(BF16), st.astype(BF16))

    s_idx = lax.broadcasted_iota(jnp.int32, (c, 1), 0)
    lane = lax.broadcasted_iota(jnp.int32, (HGRN_SUB, c), 1)
    row = lax.broadcasted_iota(jnp.int32, (HGRN_SUB, c), 0)
    att_rows = []
    for i0 in range(0, c, HGRN_SUB):
        qi = q[i0:i0 + HGRN_SUB]
        bi = b[i0:i0 + HGRN_SUB]
        if i0 > 0:
            bref = b[i0 - 1:i0]
            kt = jnp.where(s_idx < i0, kk * jnp.exp(jnp.minimum(bref - b, 0.0)), 0.0)
            att = _dot_nt((qi * jnp.exp(bi - bref)).astype(BF16), kt.astype(BF16))
        else:
            att = jnp.zeros((HGRN_SUB, c), F32)
        for j in range(HGRN_SUB):
            s = i0 + j
            e = jnp.exp(jnp.minimum(bi - b[s:s + 1], 0.0))
            colv = jnp.sum(qi * kk[s:s + 1] * e, axis=-1, keepdims=True)
            att = jnp.where((lane == s) & (row >= j), colv, att)
        att_rows.append(att)
    att = jnp.concatenate(att_rows, axis=0)
    vb = v.astype(BF16)
    o = o + jnp.dot(att.astype(BF16), vb, preferred_element_type=F32)

    b_last = b[c - 1:c]
    kd = (kk * jnp.exp(b_last - b)).astype(BF16)
    st_new = st * jnp.exp(b_last) + lax.dot_general(vb, kd, (((0,), (0,)), ((), ())),
                                                    preferred_element_type=F32)
    return o, st_new


def _tril_ones(c):
    return (lax.broadcasted_iota(jnp.int32, (c, c), 0) >= lax.broadcasted_iota(jnp.int32, (c, c), 1)
            ).astype(BF16)


def _hgrn_kernel(layer, q_ref, f_ref, i_ref, g_ref, lb_ref, ng_ref, z_ref, s_ref, st_ref, qh_ref, kh_ref, el_ref):
    n_heads, seq, _ = q_ref.shape
    lb = _lower_bound(lb_ref[...], layer)
    norm_g = ng_ref[...]
    st_ref[...] = jnp.zeros_like(st_ref)

    def fast():
        _hgrn_fast_prepare(q_ref, f_ref, lb, qh_ref, kh_ref, el_ref)
        _hgrn_fast_scan(i_ref, g_ref, norm_g, qh_ref, kh_ref, el_ref, st_ref, z_ref)

    def safe():
        tril = _tril_ones(HGRN_CHUNK)

        def body(ci, carry):
            rows = pl.ds(pl.multiple_of(ci * HGRN_CHUNK, HGRN_CHUNK), HGRN_CHUNK)
            for h in range(n_heads):
                lanes = slice(h * HEAD_DIM, (h + 1) * HEAD_DIM)
                o, st = _hgrn_chunk(q_ref[h, rows, :], f_ref[h, rows, :], i_ref[h, rows, :], lb[:, lanes],
                                    st_ref[h], tril)
                st_ref[h] = st
                z_ref[h, rows, :] = _rms_gate(o, norm_g[:, lanes], g_ref[h, rows, :]).astype(z_ref.dtype)
            return carry

        lax.fori_loop(0, seq // HGRN_CHUNK, body, 0)

    fast_ok = HGRN_CHUNK * -jnp.log(jnp.min(lb)) <= HGRN_SAFE_LOG_RANGE
    lax.cond(fast_ok, fast, safe)
    for h in range(n_heads):
        s_ref[h] = st_ref[h].T


def _prompt_hgrn(proj, lb_raw, norm_g, layer, bsz, seq):
    hps = HGRN_HEADS_PER_STEP
    assert N_HEADS_HGRN % hps == 0 and all(g % hps == 0 for g in (G_QB, G_FB, G_IB, G_GB))
    assert seq % (HGRN_CHUNK * HGRN_SCAN_CHUNKS) == 0 and seq % HGRN_PREP_ROWS == 0 and HGRN_PREP_ROWS % HGRN_CHUNK == 0

    def cols(g0):
        return pl.BlockSpec((hps, seq, HEAD_DIM), lambda b, h: (g0 // hps + h, b, 0))

    return pl.pallas_call(
        functools.partial(_hgrn_kernel, layer),
        grid=(bsz, N_HEADS_HGRN // hps),
        in_specs=[cols(G_QB), cols(G_FB), cols(G_IB), cols(G_GB),
                  pl.BlockSpec((DEPTH + 1, hps * HEAD_DIM), lambda b, h: (0, h)),
                  pl.BlockSpec((1, hps * HEAD_DIM), lambda b, h: (0, h))],
        out_specs=[pl.BlockSpec((hps, seq, HEAD_DIM), lambda b, h: (h, b, 0)),
                   pl.BlockSpec((None, hps, HEAD_DIM, HEAD_DIM), lambda b, h: (b, h, 0, 0))],
        out_shape=[jax.ShapeDtypeStruct((N_HEADS_HGRN, bsz * seq, HEAD_DIM), BF16),
                   jax.ShapeDtypeStruct((bsz, N_HEADS_HGRN, HEAD_DIM, HEAD_DIM), F32)],
        scratch_shapes=[pltpu.VMEM((hps, HEAD_DIM, HEAD_DIM), F32),
                        pltpu.VMEM((seq, hps * HEAD_DIM), BF16),
                        pltpu.VMEM((seq, hps * HEAD_DIM), BF16),
                        pltpu.VMEM((seq // HGRN_CHUNK, hps * HEAD_DIM), F32)],
        compiler_params=_params(("parallel", "parallel")),
        name="prompt_hgrn",
    )(proj, proj, proj, proj, lb_raw, norm_g)


MEM_ROWS = 256
MEM_GROUP = 8


def _mem_kernel(q_ref, g_ref, mem_ref, w_ref, z_ref, mk_ref, mv_ref):
    seq = q_ref.shape[1]
    kv = jnp.dot(mem_ref[...].astype(BF16), w_ref[...].astype(BF16), preferred_element_type=F32)
    ones = jnp.ones((N_MEM, HEAD_DIM), BF16)
    mk_b, mv_b = [], []
    for h in range(N_HEADS_MEM):
        mk = kv[:, h * HEAD_DIM:(h + 1) * HEAD_DIM]
        mv = kv[:, W_MEM + h * HEAD_DIM:W_MEM + (h + 1) * HEAD_DIM]
        mk_ref[:, h, :] = mk
        mv_ref[:, h, :] = mv
        mk_b.append(mk.astype(BF16))
        mv_b.append(jnp.concatenate([mv.astype(BF16), ones], axis=1))

    items = [(h, pl.ds(n * MEM_ROWS, MEM_ROWS)) for h in range(N_HEADS_MEM) for n in range(seq // MEM_ROWS)]
    for g0 in range(0, len(items), MEM_GROUP):
        group = items[g0:g0 + MEM_GROUP]
        scores = [_dot_nt((q_ref[h, rows, :] * ATTN_SCALE_LOG2).astype(BF16), mk_b[h]) for h, rows in group]
        probs = [jnp.exp2(s - jnp.max(s, axis=-1, keepdims=True)).astype(BF16) for s in scores]
        for (h, rows), p in zip(group, probs):
            pv = jnp.dot(p, mv_b[h], preferred_element_type=F32)
            o = pv[:, :HEAD_DIM] / pv[:, HEAD_DIM:]
            z_ref[h, rows, :] = (o * _silu(g_ref[h, rows, :])).astype(z_ref.dtype)


def _prompt_mem(proj, mem, w_kv, bsz, seq):
    assert G_QM % N_HEADS_MEM == 0 and G_GM % N_HEADS_MEM == 0

    def cols(g0):
        return pl.BlockSpec((N_HEADS_MEM, seq, HEAD_DIM), lambda b: (g0 // N_HEADS_MEM, b, 0))

    kv_out = pl.BlockSpec((None, N_MEM, N_HEADS_MEM, HEAD_DIM), lambda b: (b, 0, 0, 0))
    kv_shape = jax.ShapeDtypeStruct((bsz, N_MEM, N_HEADS_MEM, HEAD_DIM), F32)
    return pl.pallas_call(
        _mem_kernel,
        grid=(bsz,),
        in_specs=[cols(G_QM), cols(G_GM), pl.BlockSpec((N_MEM, D_MODEL), lambda b: (b, 0)),
                  pl.BlockSpec((D_MODEL, 2 * W_MEM), lambda b: (0, 0))],
        out_specs=[pl.BlockSpec((N_HEADS_MEM, seq, HEAD_DIM), lambda b: (0, b, 0)), kv_out, kv_out],
        out_shape=[jax.ShapeDtypeStruct((N_HEADS_MEM, bsz * seq, HEAD_DIM), BF16), kv_shape, kv_shape],
        compiler_params=_params(("parallel",)),
        name="prompt_mem",
    )(proj, proj, mem, w_kv)


def _merge_kernel(za_ref, zh_ref, zm_ref, x_ref, w_ref, lg_ref, lb_ref, o_ref, z_ref):
    c0 = 0
    for ref in (za_ref, zh_ref, zm_ref):
        for c in range(ref.shape[0]):
            z_ref[:, (c0 + c) * LANES:(c0 + c + 1) * LANES] = ref[c].astype(BF16)
        c0 += ref.shape[0]
    y = jnp.dot(z_ref[...], w_ref[...], preferred_element_type=F32)
    r = DEEPNORM_ALPHA * x_ref[...] + y
    mu = jnp.mean(r, axis=-1, keepdims=True)
    d = r - mu
    var = jnp.mean(d * d, axis=-1, keepdims=True)
    o_ref[...] = d * lax.rsqrt(var + LN_EPS) * lg_ref[...] + lb_ref[...]


def _merge(za, zh, zm, x, w_out_bf16, ln_g, ln_b, tm):
    m = x.shape[0]

    def slab(a):
        return pl.BlockSpec((a.shape[0], tm, LANES), lambda i: (0, i, 0))

    const = lambda shape: pl.BlockSpec(shape, lambda i: (0, 0))
    return pl.pallas_call(
        _merge_kernel,
        grid=(m // tm,),
        in_specs=[slab(za), slab(zh), slab(zm), pl.BlockSpec((tm, D_MODEL), lambda i: (i, 0)),
                  const((MIX_WIDTH, D_MODEL)), const((1, D_MODEL)), const((1, D_MODEL))],
        out_specs=pl.BlockSpec((tm, D_MODEL), lambda i: (i, 0)),
        out_shape=jax.ShapeDtypeStruct((m, D_MODEL), F32),
        scratch_shapes=[pltpu.VMEM((tm, MIX_WIDTH), BF16)],
        compiler_params=_params(("parallel",)),
        name="merge",
    )(za, zh, zm, x, w_out_bf16, ln_g, ln_b)


def _column(row):
    return jnp.broadcast_to(row, (HEAD_DIM, HEAD_DIM)).T


HBM_TILE_ROWS = 8
SAMPLE_ROWS = 2


def _window_pieces(past):
    pieces = []
    for dil in DILATIONS:
        if dil == 1:
            pieces.append((0, past - BAND, BAND, None))
        elif dil < HBM_TILE_ROWS:
            n = BAND * dil // HBM_TILE_ROWS
            pieces += [(1, past // HBM_TILE_ROWS - n, n, r) for r in range(0, HBM_TILE_ROWS, dil)]
        else:
            pieces.append((2, past // dil - BAND, BAND, 0))
    return pieces


def _cache_views(cache):
    depth, bsz, past, nh, hd = cache.shape
    assert all(d == 1 or HBM_TILE_ROWS % d == 0 or d % HBM_TILE_ROWS == 0 for d in DILATIONS)
    big = max(DILATIONS)
    hm = cache.transpose(0, 1, 3, 2, 4)
    return (hm, hm.reshape(depth, bsz, nh, past // HBM_TILE_ROWS, HBM_TILE_ROWS, hd),
            hm.reshape(depth, bsz, nh, past // big, big, hd))


def _sample_row(layer, p_ref, cos, sin, k_win, v_win, st_ref, mk_ref, mv_ref, lb_all, ng_ref, z_ref, ko_ref, vo_ref,
                so_ref):
    q_all = _rope(p_ref[G_QA:G_QA + N_HEADS_ATTN, :], cos, sin) * ATTN_SCALE
    k_all = _rope(p_ref[G_KA:G_KA + N_HEADS_ATTN, :], cos, sin)
    v_all = p_ref[G_VA:G_VA + N_HEADS_ATTN, :]
    ko_ref[...] = k_all
    vo_ref[...] = v_all

    for h in range(N_HEADS_ATTN):
        q = q_all[h:h + 1]
        s_new = jnp.sum(q * k_all[h:h + 1], axis=-1, keepdims=True)
        s = jnp.sum(k_win[h] * q, axis=-1, keepdims=True)
        m = jnp.maximum(jnp.max(s, axis=0, keepdims=True), s_new)
        p = jnp.exp(s - m)
        p_new = jnp.exp(s_new - m) * len(DILATIONS)
        den = jnp.sum(p, axis=0, keepdims=True) + p_new
        num = jnp.sum(p * v_win[h], axis=0, keepdims=True) + p_new * v_all[h:h + 1]
        z_ref[h:h + 1, :] = (num / den) * _silu(p_ref[G_GA + h:G_GA + h + 1, :])

    for h in range(N_HEADS_HGRN):
        lanes = slice(h * HEAD_DIM, (h + 1) * HEAD_DIM)
        lb = _lower_bound(lb_all[:, lanes], layer)
        f = lb + (1.0 - lb) * _sigmoid(p_ref[G_FB + h:G_FB + h + 1, :])
        f_col = _column(f)
        q_col = _column(p_ref[G_QB + h:G_QB + h + 1, :])
        s_new = f_col * st_ref[h] + (1.0 - f_col) * p_ref[G_IB + h:G_IB + h + 1, :]
        so_ref[h] = s_new
        o = jnp.sum(s_new * q_col, axis=0, keepdims=True)
        z_ref[N_HEADS_ATTN + h:N_HEADS_ATTN + h + 1, :] = _rms_gate(
            o, ng_ref[:, lanes], p_ref[G_GB + h:G_GB + h + 1, :])

    fold = mk_ref.shape[1] // N_HEADS_MEM
    unfold = lambda a: sum(a[i * N_HEADS_MEM:(i + 1) * N_HEADS_MEM] for i in range(fold))
    q_m = p_ref[G_QM:G_QM + N_HEADS_MEM, :] * ATTN_SCALE
    s = jnp.sum(mk_ref[...] * jnp.concatenate([q_m] * fold, axis=0)[None], axis=-1, keepdims=True)
    m = jnp.max(s, axis=0)
    m = functools.reduce(jnp.maximum, [m[i * N_HEADS_MEM:(i + 1) * N_HEADS_MEM] for i in range(fold)])
    p = jnp.exp(s - jnp.concatenate([m] * fold, axis=0)[None])
    o = unfold(jnp.sum(p * mv_ref[...], axis=0)) / unfold(jnp.sum(p, axis=0))
    row = N_HEADS_ATTN + N_HEADS_HGRN
    z_ref[row:row + N_HEADS_MEM, :] = o * _silu(p_ref[G_GM:G_GM + N_HEADS_MEM, :])


def _sample_kernel(layer, past, p_ref, cos_ref, sin_ref, k0_hbm, k1_hbm, k2_hbm, v0_hbm, v1_hbm, v2_hbm,
                   st_ref, mk_ref, mv_ref, lb_ref, ng_ref, z_ref, ko_ref, vo_ref, so_ref, kbuf, vbuf, sem):
    step = pl.program_id(0)
    slot = step % 2
    n_rows = p_ref.shape[0]
    pieces = _window_pieces(past)

    def window_copies(stp, sl):
        out = []
        for rr in range(n_rows):
            for ci, (views, buf) in enumerate((((k0_hbm, k1_hbm, k2_hbm), kbuf), ((v0_hbm, v1_hbm, v2_hbm), vbuf))):
                off = 0
                for pi, (vi, start, count, res) in enumerate(pieces):
                    view, row = views[vi], stp * n_rows + rr
                    src = (view.at[layer, row, :, pl.ds(start, count), :] if res is None
                           else view.at[layer, row, :, pl.ds(start, count), res, :])
                    out.append(pltpu.make_async_copy(src, buf.at[sl, rr, :, pl.ds(off, count), :],
                                                     sem.at[sl, rr, ci, pi]))
                    off += count
        return out

    @pl.when(step == 0)
    def _():
        for c in window_copies(step, slot):
            c.start()

    @pl.when(step + 1 < pl.num_programs(0))
    def _():
        for c in window_copies(step + 1, 1 - slot):
            c.start()

    for c in window_copies(step, slot):
        c.wait()

    cos, sin, lb_all = cos_ref[...], sin_ref[...], lb_ref[...]
    for rr in range(n_rows):
        _sample_row(layer, p_ref.at[rr], cos, sin, kbuf.at[slot, rr], vbuf.at[slot, rr], st_ref.at[rr], mk_ref.at[rr],
                    mv_ref.at[rr], lb_all, ng_ref, z_ref.at[rr], ko_ref.at[rr], vo_ref.at[rr], so_ref.at[rr])


def _sample_mixers(proj_rows, cos, sin, win_k, win_v, state, mem_k, mem_v, lb_raw, norm_g, layer):
    bsz, past = win_k.shape[1:3]
    rps = SAMPLE_ROWS
    assert all(past % d == 0 and past >= w for w, d in zip(WINDOWS, DILATIONS)) and bsz % rps == 0
    assert all(w // d == BAND for w, d in zip(WINDOWS, DILATIONS)) and past % HBM_TILE_ROWS == 0
    n_keys = BAND * len(DILATIONS)
    n_pieces = len(_window_pieces(past))
    any_spec = pl.BlockSpec(memory_space=pl.ANY)
    vec = lambda n: pl.BlockSpec((n, W_HGRN), lambda b: (0, 0))
    heads = lambda n: pl.BlockSpec((rps, n, HEAD_DIM), lambda b: (b, 0, 0))
    fold = HBM_TILE_ROWS // N_HEADS_MEM
    mem_k, mem_v = (a.reshape(a.shape[0], bsz, N_MEM // fold, fold * N_HEADS_MEM, HEAD_DIM) for a in (mem_k, mem_v))
    mem_spec = pl.BlockSpec((None, rps, N_MEM // fold, fold * N_HEADS_MEM, HEAD_DIM), lambda b: (layer, b, 0, 0, 0))
    return pl.pallas_call(
        functools.partial(_sample_kernel, layer, past),
        grid=(bsz // rps,),
        in_specs=[heads(N_GROUPS), pl.BlockSpec((1, HEAD_DIM), lambda b: (0, 0)),
                  pl.BlockSpec((1, HEAD_DIM), lambda b: (0, 0)), *([any_spec] * 6),
                  pl.BlockSpec((None, rps, N_HEADS_HGRN, HEAD_DIM, HEAD_DIM), lambda b: (layer, b, 0, 0, 0)),
                  mem_spec, mem_spec, vec(DEPTH + 1), vec(1)],
        out_specs=[heads(MIX_WIDTH // HEAD_DIM), heads(N_HEADS_ATTN), heads(N_HEADS_ATTN),
                   pl.BlockSpec((rps, N_HEADS_HGRN, HEAD_DIM, HEAD_DIM), lambda b: (b, 0, 0, 0))],
        out_shape=[jax.ShapeDtypeStruct((bsz, MIX_WIDTH // HEAD_DIM, HEAD_DIM), F32),
                   jax.ShapeDtypeStruct((bsz, N_HEADS_ATTN, HEAD_DIM), F32),
                   jax.ShapeDtypeStruct((bsz, N_HEADS_ATTN, HEAD_DIM), F32),
                   jax.ShapeDtypeStruct((bsz, N_HEADS_HGRN, HEAD_DIM, HEAD_DIM), F32)],
        scratch_shapes=[pltpu.VMEM((2, rps, N_HEADS_ATTN, n_keys, HEAD_DIM), F32),
                        pltpu.VMEM((2, rps, N_HEADS_ATTN, n_keys, HEAD_DIM), F32),
                        pltpu.SemaphoreType.DMA((2, rps, 2, n_pieces))],
        compiler_params=_params(("arbitrary",)),
        name="sample_mixers",
    )(proj_rows, cos, sin, *_cache_views(win_k), *_cache_views(win_v), state, mem_k, mem_v, lb_raw, norm_g)


def _rope_tables(pos):
    half = HEAD_DIM // 2
    inv_freq = 1.0 / (ROPE_THETA ** (np.arange(half, dtype=np.float64) / half))
    ang = np.asarray(pos, np.float64)[:, None] * inv_freq[None, :]
    cos, sin = np.cos(ang), np.sin(ang)
    return (jnp.asarray(np.concatenate([cos, cos], axis=-1), F32),
            jnp.asarray(np.concatenate([-sin, sin], axis=-1), F32))


def kernel(x_prompt, x_sample, cache_win_k, cache_win_v, state_hgrn, cache_mem_k, cache_mem_v, mem_prompt,
           w_in, w_mem_kv, hgrn_lb_raw, hgrn_norm_g, w_out, ln_g, ln_b):
    bsz, seq, _ = x_prompt.shape
    dbsz, n_new, _ = x_sample.shape
    assert n_new == 1 and seq % (BAND * max(DILATIONS)) == 0
    cos_p, sin_p = _rope_tables(np.arange(seq))
    cos_s, sin_s = _rope_tables(PAST_LEN + np.arange(n_new))

    hp = x_prompt.reshape(bsz * seq, D_MODEL)
    hs = x_sample.reshape(dbsz * n_new, D_MODEL)
    mem = mem_prompt.reshape(bsz * N_MEM, D_MODEL)
    outs = [[] for _ in range(8)]
    for layer in range(DEPTH):
        w_out_b = w_out[layer].astype(BF16)
        lb_raw = hgrn_lb_raw
        norm_g = hgrn_norm_g[layer][None]
        lg, lbias = ln_g[layer][None], ln_b[layer][None]

        proj, proj_s = _project(hp, hs, w_in[layer], 1024, 512, 1024)
        za, k1, v1 = _prompt_attention(proj, cos_p, sin_p, bsz, seq)
        zh, s1 = _prompt_hgrn(proj, lb_raw, norm_g, layer, bsz, seq)
        zm, mk1, mv1 = _prompt_mem(proj, mem, w_mem_kv[layer], bsz, seq)
        hp = _merge(za, zh, zm, hp, w_out_b, lg, lbias, 512)

        proj_s = proj_s.transpose(1, 0, 2)
        zs, k2, v2, s2 = _sample_mixers(proj_s, cos_s, sin_s, cache_win_k, cache_win_v, state_hgrn,
                                        cache_mem_k, cache_mem_v, lb_raw, norm_g, layer)
        zs = zs.transpose(1, 0, 2)
        hs = _merge(zs[:N_HEADS_ATTN], zs[N_HEADS_ATTN:N_HEADS_ATTN + N_HEADS_HGRN],
                    zs[N_HEADS_ATTN + N_HEADS_HGRN:], hs, w_out_b, lg, lbias, dbsz)

        new = (k1.transpose(0, 2, 1, 3), v1.transpose(0, 2, 1, 3), s1, mk1, mv1,
               k2.reshape(dbsz, n_new, N_HEADS_ATTN, HEAD_DIM), v2.reshape(dbsz, n_new, N_HEADS_ATTN, HEAD_DIM),
               s2.astype(state_hgrn.dtype))
        for acc, val in zip(outs, new):
            acc.append(val)

    return (hp.reshape(bsz, seq, D_MODEL), hs.reshape(dbsz, n_new, D_MODEL), *[jnp.stack(o) for o in outs])
```

```python
import functools

import jax
import jax.numpy as jnp
import numpy as np
from jax import lax
from jax.experimental import pallas as pl
from jax.experimental.pallas import tpu as pltpu

F32 = jnp.float32
BF16 = jnp.bfloat16

D_MODEL = 2048
DEPTH = 1
PAST_LEN = 8192
HEAD_DIM = 128
N_HEADS_ATTN = 6
N_HEADS_HGRN = 6
N_HEADS_MEM = 4
W_ATTN = N_HEADS_ATTN * HEAD_DIM
W_HGRN = N_HEADS_HGRN * HEAD_DIM
W_MEM = N_HEADS_MEM * HEAD_DIM
MIX_WIDTH = W_ATTN + W_HGRN + W_MEM
WINDOWS = (128, 512, 2048)
DILATIONS = (1, 4, 16)
N_MEM = 256
ROPE_THETA = 10000.0
LN_EPS = 1e-5
RMS_EPS = 1e-6
NEG_INF = -1e30
DEEPNORM_ALPHA = (2 * DEPTH) ** 0.25
ATTN_SCALE = HEAD_DIM ** -0.5
ATTN_SCALE_LOG2 = ATTN_SCALE * 1.4426950408889634

G_QA, G_KA, G_VA, G_GA = 0, 6, 12, 18
G_QB, G_FB, G_IB, G_GB = 24, 30, 36, 42
G_QM, G_GM = 48, 52
N_GROUPS = 56

LANES = 128
BAND = 128
ATTN_GROUP = 16
HGRN_CHUNK = 64
HGRN_SUB = 16
HGRN_HEADS_PER_STEP = 3
HGRN_PREP_ROWS = 256
HGRN_SCAN_CHUNKS = 16
HGRN_SAFE_LOG_RANGE = 70.0
VMEM_LIMIT = 48 * 1024 * 1024


def _params(sem, vmem=VMEM_LIMIT):
    return pltpu.CompilerParams(dimension_semantics=sem, vmem_limit_bytes=vmem)


def _sigmoid(x):
    return 1.0 / (1.0 + jnp.exp(-x))


def _silu(x):
    return x * _sigmoid(x)


def _dot_nt(a, b):
    return lax.dot_general(a, b, (((1,), (1,)), ((), ())), preferred_element_type=F32)


def _store_slabs(ref, acc):
    for c in range(ref.shape[0]):
        ref[c] = acc[:, c * LANES:(c + 1) * LANES]


def _proj_first_kernel(x_ref, xs_ref, w_ref, o_ref, os_ref, wb_ref, xb_ref):
    tm = x_ref.shape[0]

    @pl.when(pl.program_id(0) == 0)
    def _():
        xb_ref[:tm, :] = x_ref[...].astype(BF16)
        xb_ref[tm:, :] = xs_ref[...].astype(BF16)

    wb_ref[...] = w_ref[...].astype(BF16)
    acc = jnp.dot(xb_ref[...], wb_ref[...], preferred_element_type=F32)
    _store_slabs(o_ref, acc[:tm])
    _store_slabs(os_ref, acc[tm:])


def _proj_rest_kernel(x_ref, w_ref, first_hbm, o_hbm, xb_ref, obuf, sem_out, sem_first):
    i, j = pl.program_id(0), pl.program_id(1)
    n_col = pl.num_programs(1)
    step = i * n_col + j
    last = pl.num_programs(0) * n_col - 1
    slot = step % 2
    tm = x_ref.shape[0]
    n_slab = obuf.shape[1]

    def tile_copy(stp):
        row_tile, col_tile = stp // n_col + 1, stp % n_col
        return pltpu.make_async_copy(
            obuf.at[stp % 2], o_hbm.at[pl.ds(col_tile * n_slab, n_slab), pl.ds(row_tile * tm, tm), :],
            sem_out.at[stp % 2])

    first_copy = pltpu.make_async_copy(first_hbm, o_hbm.at[:, pl.ds(0, tm), :], sem_first.at[0])

    @pl.when(step == 0)
    def _():
        first_copy.start()

    @pl.when(j == 0)
    def _():
        xb_ref[...] = x_ref[...].astype(BF16)

    @pl.when(step >= 2)
    def _():
        tile_copy(step - 2).wait()

    _store_slabs(obuf.at[slot], jnp.dot(xb_ref[...], w_ref[...], preferred_element_type=F32))
    tile_copy(step).start()

    @pl.when(step == last)
    def _():
        tile_copy(step - 1).wait()
        tile_copy(step).wait()
        first_copy.wait()


def _project(x, xs, w, tm, tn_first, tn):
    m, k = x.shape
    ms = xs.shape[0]
    n = w.shape[1]
    assert (m // tm - 1) * (n // tn) >= 2
    proj_first, proj_s, w_bf16 = pl.pallas_call(
        _proj_first_kernel,
        grid=(n // tn_first,),
        in_specs=[pl.BlockSpec((tm, k), lambda j: (0, 0)),
                  pl.BlockSpec((ms, k), lambda j: (0, 0)),
                  pl.BlockSpec((k, tn_first), lambda j: (0, j))],
        out_specs=[pl.BlockSpec((tn_first // LANES, tm, LANES), lambda j: (j, 0, 0)),
                   pl.BlockSpec((tn_first // LANES, ms, LANES), lambda j: (j, 0, 0)),
                   pl.BlockSpec((k, tn_first), lambda j: (0, j))],
        out_shape=[jax.ShapeDtypeStruct((n // LANES, tm, LANES), F32),
                   jax.ShapeDtypeStruct((n // LANES, ms, LANES), F32),
                   jax.ShapeDtypeStruct((k, n), BF16)],
        scratch_shapes=[pltpu.VMEM((tm + ms, k), BF16)],
        compiler_params=_params(("arbitrary",)),
        name="in_proj_first",
    )(x, xs, w)
    proj = pl.pallas_call(
        _proj_rest_kernel,
        grid=(m // tm - 1, n // tn),
        in_specs=[pl.BlockSpec((tm, k), lambda i, j: (i + 1, 0)),
                  pl.BlockSpec((k, tn), lambda i, j: (0, j)),
                  pl.BlockSpec(memory_space=pl.ANY)],
        out_specs=pl.BlockSpec(memory_space=pl.ANY),
        out_shape=jax.ShapeDtypeStruct((n // LANES, m, LANES), F32),
        scratch_shapes=[pltpu.VMEM((tm, k), BF16), pltpu.VMEM((2, tn // LANES, tm, LANES), F32),
                        pltpu.SemaphoreType.DMA((2,)), pltpu.SemaphoreType.DMA((1,))],
        compiler_params=_params(("arbitrary", "arbitrary")),
        name="in_proj",
    )(x, w_bf16, proj_first)
    return proj, proj_s


def _rope(x, cos, sin_signed):
    return x * cos + pltpu.roll(x, HEAD_DIM // 2, 1) * sin_signed


def _bias_from_count(count):
    return jnp.where(count > 1.5, 1.0, jnp.where(count > 0.5, 0.0, NEG_INF)).astype(F32)


def _block_deltas(block_gap):
    qi = lax.broadcasted_iota(jnp.int32, (BAND, BAND), 0)
    ki = lax.broadcasted_iota(jnp.int32, (BAND, BAND), 1)
    return block_gap * BAND + qi - ki


def _near_bias(block_gap):
    d = _block_deltas(block_gap)
    return _bias_from_count(((d >= 0) & (d <= BAND)).astype(F32))


def _class_bias(block_gap, far_step):
    d = _block_deltas(block_gap)
    near = (d >= 0) & (d <= BAND)
    far = (d >= 0) & ((d & (far_step - 1)) == 0)
    return _bias_from_count(near.astype(F32) + far.astype(F32))


def _softmax_stage(scores, floors=None):
    out = []
    for idx, s in enumerate(scores):
        m = jnp.broadcast_to(jnp.max(s, axis=-1, keepdims=True), (BAND, HEAD_DIM))
        if floors is not None:
            m = jnp.maximum(m, floors[idx])
        m_wide = jnp.concatenate([m] * (s.shape[1] // HEAD_DIM), axis=1)
        out.append((m, jnp.exp2(s - m_wide).astype(BF16)))
    return out


def _attn_kernel(q_ref, k_ref, v_ref, g_ref, cos_ref, sin_ref, z_ref, ko_ref, vo_ref,
                 qs_ref, qb_ref, kb_ref, vb_ref, qc_ref, kc_ref, vc_ref, acc_ref, m_ref, l_ref):
    seq = q_ref.shape[0]
    d_mid, d_far = DILATIONS[1], DILATIONS[2]
    far_step = d_far // d_mid
    cls_rows = seq // d_mid
    cos = cos_ref[...]
    sin = sin_ref[...]
    qs_ref[...] = _rope(q_ref[...], cos, sin) * ATTN_SCALE_LOG2
    ko_ref[...] = _rope(k_ref[...], cos, sin)
    vo_ref[...] = v_ref[...]
    qb_ref[...] = qs_ref[...].astype(BF16)
    kb_ref[...] = ko_ref[...].astype(BF16)
    ones = jnp.ones((seq, HEAD_DIM), BF16)
    vb_ref[:, :HEAD_DIM] = v_ref[...].astype(BF16)
    vb_ref[:, HEAD_DIM:] = ones
    vc_ref[:, HEAD_DIM:] = ones
    for r in range(d_mid):
        cls, dst = pl.ds(r, cls_rows, stride=d_mid), pl.ds(r * cls_rows, cls_rows)
        qc_ref[dst, :] = qs_ref[cls, :].astype(BF16)
        kc_ref[dst, :] = ko_ref[cls, :].astype(BF16)
        vc_ref[dst, :HEAD_DIM] = v_ref[cls, :].astype(BF16)

    n_cls_blk = cls_rows // BAND
    cls_bias = [_class_bias(gap, far_step) for gap in range(min(n_cls_blk, 3))]
    items = [(r, n) for r in range(d_mid) for n in range(n_cls_blk)]
    for g0 in range(0, len(items), ATTN_GROUP):
        group = items[g0:g0 + ATTN_GROUP]
        keys = [pl.ds(r * cls_rows, (n + 1) * BAND) for r, n in group]
        scores = [_dot_nt(qc_ref[pl.ds(r * cls_rows + n * BAND, BAND), :], kc_ref[kr, :])
                  + jnp.concatenate([cls_bias[min(n - nk, 2)] for nk in range(n + 1)], axis=1)
                  for (r, n), kr in zip(group, keys)]
        for (r, n), kr, (m, p) in zip(group, keys, _softmax_stage(scores)):
            rows = pl.ds(r + d_mid * BAND * n, BAND, stride=d_mid)
            pv = jnp.dot(p, vc_ref[kr, :], preferred_element_type=F32)
            acc_ref[rows, :] = pv[:, :HEAD_DIM]
            l_ref[rows, :] = pv[:, HEAD_DIM:]
            m_ref[rows, :] = m

    near_bias = [_near_bias(0), jnp.concatenate([_near_bias(1), _near_bias(0)], axis=1)]
    items = list(range(seq // BAND))
    for g0 in range(0, len(items), ATTN_GROUP):
        group = items[g0:g0 + ATTN_GROUP]
        rows = [pl.ds(n * BAND, BAND) for n in group]
        keys = [pl.ds(max(n - 1, 0) * BAND, BAND * min(n + 1, 2)) for n in group]
        scores = [_dot_nt(qb_ref[rw, :], kb_ref[kr, :]) + near_bias[min(n, 1)]
                  for n, rw, kr in zip(group, rows, keys)]
        probs = _softmax_stage(scores, floors=[m_ref[rw, :] for rw in rows])
        for rw, kr, (m, p) in zip(rows, keys, probs):
            w = jnp.exp2(m_ref[rw, :] - m)
            pv = jnp.dot(p, vb_ref[kr, :], preferred_element_type=F32)
            num = pv[:, :HEAD_DIM] + w * acc_ref[rw, :]
            den = pv[:, HEAD_DIM:] + w * l_ref[rw, :]
            z_ref[rw, :] = ((num / den) * _silu(g_ref[rw, :])).astype(z_ref.dtype)


def _prompt_attention(proj, cos, sin, bsz, seq):
    def col(g0):
        return pl.BlockSpec((None, seq, HEAD_DIM), lambda b, h: (g0 + h, b, 0))

    table = pl.BlockSpec((seq, HEAD_DIM), lambda b, h: (0, 0))
    kv_out = pl.BlockSpec((None, None, seq, HEAD_DIM), lambda b, h: (b, h, 0, 0))
    d_near, d_mid, d_far = DILATIONS
    far_step = d_far // d_mid
    assert d_near == 1 and d_far % d_mid == 0 and far_step & (far_step - 1) == 0
    assert all(w // d == BAND for w, d in zip(WINDOWS, DILATIONS)) and seq % (d_mid * BAND) == 0
    assert seq // d_mid <= far_step * BAND
    return pl.pallas_call(
        _attn_kernel,
        grid=(bsz, N_HEADS_ATTN),
        in_specs=[col(G_QA), col(G_KA), col(G_VA), col(G_GA), table, table],
        out_specs=[pl.BlockSpec((None, seq, HEAD_DIM), lambda b, h: (h, b, 0)), kv_out, kv_out],
        out_shape=[jax.ShapeDtypeStruct((N_HEADS_ATTN, bsz * seq, HEAD_DIM), BF16),
                   jax.ShapeDtypeStruct((bsz, N_HEADS_ATTN, seq, HEAD_DIM), F32),
                   jax.ShapeDtypeStruct((bsz, N_HEADS_ATTN, seq, HEAD_DIM), F32)],
        scratch_shapes=[pltpu.VMEM((seq, HEAD_DIM), F32)]
                       + [pltpu.VMEM((seq, HEAD_DIM), BF16), pltpu.VMEM((seq, HEAD_DIM), BF16),
                          pltpu.VMEM((seq, 2 * HEAD_DIM), BF16)] * 2
                       + [pltpu.VMEM((seq, HEAD_DIM), F32)] * 3,
        compiler_params=_params(("parallel", "parallel")),
        name="prompt_attn",
    )(proj, proj, proj, proj, cos, sin)


def _lower_bound(lb_raw, layer):
    e = jnp.exp(lb_raw - jnp.max(lb_raw, axis=0, keepdims=True))
    sm = e / jnp.sum(e, axis=0, keepdims=True)
    return jnp.sum(sm[:layer + 1], axis=0, keepdims=True)


def _split2(x):
    hi = x.astype(BF16)
    return hi, (x - hi.astype(F32)).astype(BF16)


def _rms_gate(o, norm_g, gate):
    o = o * lax.rsqrt(jnp.mean(o * o, axis=-1, keepdims=True) + RMS_EPS)
    return o * norm_g * _silu(gate)


def _hgrn_gates(fb, lb, tril):
    f = lb + (1.0 - lb) * _sigmoid(fb)
    g = jnp.log(f)
    b = sum(jnp.dot(tril, piece, preferred_element_type=F32) for piece in _split2(g))
    return 1.0 - f, b


def _hgrn_fast_prepare(q_ref, f_ref, lb, qh_ref, kh_ref, el_ref):
    n_heads, seq, _ = q_ref.shape
    blk = HGRN_PREP_ROWS
    tril = _tril_ones(blk)

    def body(bi, carry):
        r0 = pl.multiple_of(bi * blk, blk)
        fb = jnp.concatenate([f_ref[h, pl.ds(r0, blk), :] for h in range(n_heads)], axis=1)
        f = lb + (1.0 - lb) * _sigmoid(fb)
        b_blk = sum(jnp.dot(tril, piece, preferred_element_type=F32) for piece in _split2(jnp.log(f)))
        kk = 1.0 - f
        for c0 in range(0, blk, HGRN_CHUNK):
            rs = slice(c0, c0 + HGRN_CHUNK)
            b = b_blk[rs] - b_blk[c0 - 1:c0] if c0 else b_blk[rs]
            rows = pl.ds(r0 + c0, HGRN_CHUNK)
            q = jnp.concatenate([q_ref[h, rows, :] for h in range(n_heads)], axis=1)
            qh_ref[rows, :] = (q * jnp.exp(b)).astype(BF16)
            kh_ref[rows, :] = (kk[rs] * jnp.exp(-b)).astype(BF16)
            el_ref[pl.ds(bi * (blk // HGRN_CHUNK) + c0 // HGRN_CHUNK, 1), :] = jnp.exp(b[HGRN_CHUNK - 1:])
        return carry

    lax.fori_loop(0, seq // blk, body, 0, unroll=4)


def _hgrn_fast_scan(i_ref, g_ref, norm_g, qh_ref, kh_ref, el_ref, st_ref, z_ref):
    n_heads, seq, _ = i_ref.shape
    c = HGRN_CHUNK
    heads = range(n_heads)
    chunks = range(HGRN_SCAN_CHUNKS)
    lanes = [slice(h * HEAD_DIM, (h + 1) * HEAD_DIM) for h in heads]
    causal = lax.broadcasted_iota(jnp.int32, (c, c), 0) >= lax.broadcasted_iota(jnp.int32, (c, c), 1)
    tn = (((0,), (0,)), ((), ()))

    def body(ti, carry):
        rows = [pl.ds(pl.multiple_of((ti * HGRN_SCAN_CHUNKS + k) * c, c), c) for k in chunks]
        qh = [[qh_ref[rows[k], lanes[h]] for h in heads] for k in chunks]
        kh = [[kh_ref[rows[k], lanes[h]] for h in heads] for k in chunks]
        vb = [[i_ref[h, rows[k], :].astype(BF16) for h in heads] for k in chunks]
        att = [[jnp.where(causal, _dot_nt(qh[k][h], kh[k][h]), 0.0).astype(BF16) for h in heads] for k in chunks]
        ds = [[lax.dot_general(vb[k][h], kh[k][h], tn, preferred_element_type=F32) for h in heads] for k in chunks]
        st = [st_ref[h] for h in heads]
        o = []
        for k in chunks:
            el = el_ref[pl.ds(ti * HGRN_SCAN_CHUNKS + k, 1), :]
            o.append([_dot_nt(qh[k][h], st[h].astype(BF16)) for h in heads])
            st = [(st[h] + ds[k][h]) * el[:, lanes[h]] for h in heads]
        for h in heads:
            st_ref[h] = st[h]
        for k in chunks:
            for h in heads:
                o_kh = o[k][h] + jnp.dot(att[k][h], vb[k][h], preferred_element_type=F32)
                z_ref[h, rows[k], :] = _rms_gate(o_kh, norm_g[:, lanes[h]], g_ref[h, rows[k], :]).astype(z_ref.dtype)
        return carry

    lax.fori_loop(0, seq // (c * HGRN_SCAN_CHUNKS), body, 0)


def _hgrn_chunk(q, fb, v, lb, st, tril):
    c = q.shape[0]
    kk, b = _hgrn_gates(fb, lb, tril)
    o = _dot_nt((q * jnp.exp(b)).astype(BF16), st.astype(BF16))

    s_idx = lax.broadcasted_iota(jnp.int32, (c, 1), 0)
    lane = lax.broadcasted_iota(jnp.int32, (HGRN_SUB, c), 1)
    row = lax.broadcasted_iota(jnp.int32, (HGRN_SUB, c), 0)
    att_rows = []
    for i0 in range(0, c, HGRN_SUB):
        qi = q[i0:i0 + HGRN_SUB]
        bi = b[i0:i0 + HGRN_SUB]
        if i0 > 0:
            bref = b[i0 - 1:i0]
            kt = jnp.where(s_idx < i0, kk * jnp.exp(jnp.minimum(bref - b, 0.0)), 0.0)
            att = _dot_nt((qi * jnp.exp(bi - bref)).astype(BF16), kt.astype(BF16))
        else:
            att = jnp.zeros((HGRN_SUB, c), F32)
        for j in range(HGRN_SUB):
            s = i0 + j
            e = jnp.exp(jnp.minimum(bi - b[s:s + 1], 0.0))
            colv = jnp.sum(qi * kk[s:s + 1] * e, axis=-1, keepdims=True)
            att = jnp.where((lane == s) & (row >= j), colv, att)
        att_rows.append(att)
    att = jnp.concatenate(att_rows, axis=0)
    vb = v.astype(BF16)
    o = o + jnp.dot(att.astype(BF16), vb, preferred_element_type=F32)

    b_last = b[c - 1:c]
    kd = (kk * jnp.exp(b_last - b)).astype(BF16)
    st_new = st * jnp.exp(b_last) + lax.dot_general(vb, kd, (((0,), (0,)), ((), ())),
                                                    preferred_element_type=F32)
    return o, st_new


def _tril_ones(c):
    return (lax.broadcasted_iota(jnp.int32, (c, c), 0) >= lax.broadcasted_iota(jnp.int32, (c, c), 1)
            ).astype(BF16)


def _hgrn_kernel(layer, q_ref, f_ref, i_ref, g_ref, lb_ref, ng_ref, z_ref, s_ref, st_ref, qh_ref, kh_ref, el_ref):
    n_heads, seq, _ = q_ref.shape
    lb = _lower_bound(lb_ref[...], layer)
    norm_g = ng_ref[...]
    st_ref[...] = jnp.zeros_like(st_ref)

    def fast():
        _hgrn_fast_prepare(q_ref, f_ref, lb, qh_ref, kh_ref, el_ref)
        _hgrn_fast_scan(i_ref, g_ref, norm_g, qh_ref, kh_ref, el_ref, st_ref, z_ref)

    def safe():
        tril = _tril_ones(HGRN_CHUNK)

        def body(ci, carry):
            rows = pl.ds(pl.multiple_of(ci * HGRN_CHUNK, HGRN_CHUNK), HGRN_CHUNK)
            for h in range(n_heads):
                lanes = slice(h * HEAD_DIM, (h + 1) * HEAD_DIM)
                o, st = _hgrn_chunk(q_ref[h, rows, :], f_ref[h, rows, :], i_ref[h, rows, :], lb[:, lanes],
                                    st_ref[h], tril)
                st_ref[h] = st
                z_ref[h, rows, :] = _rms_gate(o, norm_g[:, lanes], g_ref[h, rows, :]).astype(z_ref.dtype)
            return carry

        lax.fori_loop(0, seq // HGRN_CHUNK, body, 0)

    fast_ok = HGRN_CHUNK * -jnp.log(jnp.min(lb)) <= HGRN_SAFE_LOG_RANGE
    lax.cond(fast_ok, fast, safe)
    for h in range(n_heads):
        s_ref[h] = st_ref[h].T


def _prompt_hgrn(proj, lb_raw, norm_g, layer, bsz, seq):
    hps = HGRN_HEADS_PER_STEP
    assert N_HEADS_HGRN % hps == 0 and all(g % hps == 0 for g in (G_QB, G_FB, G_IB, G_GB))
    assert seq % (HGRN_CHUNK * HGRN_SCAN_CHUNKS) == 0 and seq % HGRN_PREP_ROWS == 0 and HGRN_PREP_ROWS % HGRN_CHUNK == 0

    def cols(g0):
        return pl.BlockSpec((hps, seq, HEAD_DIM), lambda b, h: (g0 // hps + h, b, 0))

    return pl.pallas_call(
        functools.partial(_hgrn_kernel, layer),
        grid=(bsz, N_HEADS_HGRN // hps),
        in_specs=[cols(G_QB), cols(G_FB), cols(G_IB), cols(G_GB),
                  pl.BlockSpec((DEPTH + 1, hps * HEAD_DIM), lambda b, h: (0, h)),
                  pl.BlockSpec((1, hps * HEAD_DIM), lambda b, h: (0, h))],
        out_specs=[pl.BlockSpec((hps, seq, HEAD_DIM), lambda b, h: (h, b, 0)),
                   pl.BlockSpec((None, hps, HEAD_DIM, HEAD_DIM), lambda b, h: (b, h, 0, 0))],
        out_shape=[jax.ShapeDtypeStruct((N_HEADS_HGRN, bsz * seq, HEAD_DIM), BF16),
                   jax.ShapeDtypeStruct((bsz, N_HEADS_HGRN, HEAD_DIM, HEAD_DIM), F32)],
        scratch_shapes=[pltpu.VMEM((hps, HEAD_DIM, HEAD_DIM), F32),
                        pltpu.VMEM((seq, hps * HEAD_DIM), BF16),
                        pltpu.VMEM((seq, hps * HEAD_DIM), BF16),
                        pltpu.VMEM((seq // HGRN_CHUNK, hps * HEAD_DIM), F32)],
        compiler_params=_params(("parallel", "parallel")),
        name="prompt_hgrn",
    )(proj, proj, proj, proj, lb_raw, norm_g)


MEM_ROWS = 256
MEM_GROUP = 8


def _mem_kernel(q_ref, g_ref, mem_ref, w_ref, z_ref, mk_ref, mv_ref):
    seq = q_ref.shape[1]
    kv = jnp.dot(mem_ref[...].astype(BF16), w_ref[...].astype(BF16), preferred_element_type=F32)
    ones = jnp.ones((N_MEM, HEAD_DIM), BF16)
    mk_b, mv_b = [], []
    for h in range(N_HEADS_MEM):
        mk = kv[:, h * HEAD_DIM:(h + 1) * HEAD_DIM]
        mv = kv[:, W_MEM + h * HEAD_DIM:W_MEM + (h + 1) * HEAD_DIM]
        mk_ref[:, h, :] = mk
        mv_ref[:, h, :] = mv
        mk_b.append(mk.astype(BF16))
        mv_b.append(jnp.concatenate([mv.astype(BF16), ones], axis=1))

    items = [(h, pl.ds(n * MEM_ROWS, MEM_ROWS)) for h in range(N_HEADS_MEM) for n in range(seq // MEM_ROWS)]
    for g0 in range(0, len(items), MEM_GROUP):
        group = items[g0:g0 + MEM_GROUP]
        scores = [_dot_nt((q_ref[h, rows, :] * ATTN_SCALE_LOG2).astype(BF16), mk_b[h]) for h, rows in group]
        probs = [jnp.exp2(s - jnp.max(s, axis=-1, keepdims=True)).astype(BF16) for s in scores]
        for (h, rows), p in zip(group, probs):
            pv = jnp.dot(p, mv_b[h], preferred_element_type=F32)
            o = pv[:, :HEAD_DIM] / pv[:, HEAD_DIM:]
            z_ref[h, rows, :] = (o * _silu(g_ref[h, rows, :])).astype(z_ref.dtype)


def _prompt_mem(proj, mem, w_kv, bsz, seq):
    assert G_QM % N_HEADS_MEM == 0 and G_GM % N_HEADS_MEM == 0

    def cols(g0):
        return pl.BlockSpec((N_HEADS_MEM, seq, HEAD_DIM), lambda b: (g0 // N_HEADS_MEM, b, 0))

    kv_out = pl.BlockSpec((None, N_MEM, N_HEADS_MEM, HEAD_DIM), lambda b: (b, 0, 0, 0))
    kv_shape = jax.ShapeDtypeStruct((bsz, N_MEM, N_HEADS_MEM, HEAD_DIM), F32)
    return pl.pallas_call(
        _mem_kernel,
        grid=(bsz,),
        in_specs=[cols(G_QM), cols(G_GM), pl.BlockSpec((N_MEM, D_MODEL), lambda b: (b, 0)),
                  pl.BlockSpec((D_MODEL, 2 * W_MEM), lambda b: (0, 0))],
        out_specs=[pl.BlockSpec((N_HEADS_MEM, seq, HEAD_DIM), lambda b: (0, b, 0)), kv_out, kv_out],
        out_shape=[jax.ShapeDtypeStruct((N_HEADS_MEM, bsz * seq, HEAD_DIM), BF16), kv_shape, kv_shape],
        compiler_params=_params(("parallel",)),
        name="prompt_mem",
    )(proj, proj, mem, w_kv)


def _merge_kernel(za_ref, zh_ref, zm_ref, x_ref, w_ref, lg_ref, lb_ref, o_ref, z_ref):
    c0 = 0
    for ref in (za_ref, zh_ref, zm_ref):
        for c in range(ref.shape[0]):
            z_ref[:, (c0 + c) * LANES:(c0 + c + 1) * LANES] = ref[c].astype(BF16)
        c0 += ref.shape[0]
    y = jnp.dot(z_ref[...], w_ref[...], preferred_element_type=F32)
    r = DEEPNORM_ALPHA * x_ref[...] + y
    mu = jnp.mean(r, axis=-1, keepdims=True)
    d = r - mu
    var = jnp.mean(d * d, axis=-1, keepdims=True)
    o_ref[...] = d * lax.rsqrt(var + LN_EPS) * lg_ref[...] + lb_ref[...]


def _merge(za, zh, zm, x, w_out_bf16, ln_g, ln_b, tm):
    m = x.shape[0]

    def slab(a):
        return pl.BlockSpec((a.shape[0], tm, LANES), lambda i: (0, i, 0))

    const = lambda shape: pl.BlockSpec(shape, lambda i: (0, 0))
    return pl.pallas_call(
        _merge_kernel,
        grid=(m // tm,),
        in_specs=[slab(za), slab(zh), slab(zm), pl.BlockSpec((tm, D_MODEL), lambda i: (i, 0)),
                  const((MIX_WIDTH, D_MODEL)), const((1, D_MODEL)), const((1, D_MODEL))],
        out_specs=pl.BlockSpec((tm, D_MODEL), lambda i: (i, 0)),
        out_shape=jax.ShapeDtypeStruct((m, D_MODEL), F32),
        scratch_shapes=[pltpu.VMEM((tm, MIX_WIDTH), BF16)],
        compiler_params=_params(("parallel",)),
        name="merge",
    )(za, zh, zm, x, w_out_bf16, ln_g, ln_b)


def _column(row):
    return jnp.broadcast_to(row, (HEAD_DIM, HEAD_DIM)).T


HBM_TILE_ROWS = 8
SAMPLE_ROWS = 2


def _window_pieces(past):
    pieces = []
    for dil in DILATIONS:
        if dil == 1:
            pieces.append((0, past - BAND, BAND, None))
        elif dil < HBM_TILE_ROWS:
            n = BAND * dil // HBM_TILE_ROWS
            pieces += [(1, past // HBM_TILE_ROWS - n, n, r) for r in range(0, HBM_TILE_ROWS, dil)]
        else:
            pieces.append((2, past // dil - BAND, BAND, 0))
    return pieces


def _cache_views(cache):
    depth, bsz, past, nh, hd = cache.shape
    assert all(d == 1 or HBM_TILE_ROWS % d == 0 or d % HBM_TILE_ROWS == 0 for d in DILATIONS)
    big = max(DILATIONS)
    hm = cache.transpose(0, 1, 3, 2, 4)
    return (hm, hm.reshape(depth, bsz, nh, past // HBM_TILE_ROWS, HBM_TILE_ROWS, hd),
            hm.reshape(depth, bsz, nh, past // big, big, hd))


def _sample_row(layer, p_ref, cos, sin, k_win, v_win, st_ref, mk_ref, mv_ref, lb_all, ng_ref, z_ref, ko_ref, vo_ref,
                so_ref):
    q_all = _rope(p_ref[G_QA:G_QA + N_HEADS_ATTN, :], cos, sin) * ATTN_SCALE
    k_all = _rope(p_ref[G_KA:G_KA + N_HEADS_ATTN, :], cos, sin)
    v_all = p_ref[G_VA:G_VA + N_HEADS_ATTN, :]
    ko_ref[...] = k_all
    vo_ref[...] = v_all

    for h in range(N_HEADS_ATTN):
        q = q_all[h:h + 1]
        s_new = jnp.sum(q * k_all[h:h + 1], axis=-1, keepdims=True)
        s = jnp.sum(k_win[h] * q, axis=-1, keepdims=True)
        m = jnp.maximum(jnp.max(s, axis=0, keepdims=True), s_new)
        p = jnp.exp(s - m)
        p_new = jnp.exp(s_new - m) * len(DILATIONS)
        den = jnp.sum(p, axis=0, keepdims=True) + p_new
        num = jnp.sum(p * v_win[h], axis=0, keepdims=True) + p_new * v_all[h:h + 1]
        z_ref[h:h + 1, :] = (num / den) * _silu(p_ref[G_GA + h:G_GA + h + 1, :])

    for h in range(N_HEADS_HGRN):
        lanes = slice(h * HEAD_DIM, (h + 1) * HEAD_DIM)
        lb = _lower_bound(lb_all[:, lanes], layer)
        f = lb + (1.0 - lb) * _sigmoid(p_ref[G_FB + h:G_FB + h + 1, :])
        f_col = _column(f)
        q_col = _column(p_ref[G_QB + h:G_QB + h + 1, :])
        s_new = f_col * st_ref[h] + (1.0 - f_col) * p_ref[G_IB + h:G_IB + h + 1, :]
        so_ref[h] = s_new
        o = jnp.sum(s_new * q_col, axis=0, keepdims=True)
        z_ref[N_HEADS_ATTN + h:N_HEADS_ATTN + h + 1, :] = _rms_gate(
            o, ng_ref[:, lanes], p_ref[G_GB + h:G_GB + h + 1, :])

    fold = mk_ref.shape[1] // N_HEADS_MEM
    unfold = lambda a: sum(a[i * N_HEADS_MEM:(i + 1) * N_HEADS_MEM] for i in range(fold))
    q_m = p_ref[G_QM:G_QM + N_HEADS_MEM, :] * ATTN_SCALE
    s = jnp.sum(mk_ref[...] * jnp.concatenate([q_m] * fold, axis=0)[None], axis=-1, keepdims=True)
    m = jnp.max(s, axis=0)
    m = functools.reduce(jnp.maximum, [m[i * N_HEADS_MEM:(i + 1) * N_HEADS_MEM] for i in range(fold)])
    p = jnp.exp(s - jnp.concatenate([m] * fold, axis=0)[None])
    o = unfold(jnp.sum(p * mv_ref[...], axis=0)) / unfold(jnp.sum(p, axis=0))
    row = N_HEADS_ATTN + N_HEADS_HGRN
    z_ref[row:row + N_HEADS_MEM, :] = o * _silu(p_ref[G_GM:G_GM + N_HEADS_MEM, :])


def _sample_kernel(layer, past, p_ref, cos_ref, sin_ref, k0_hbm, k1_hbm, k2_hbm, v0_hbm, v1_hbm, v2_hbm,
                   st_ref, mk_ref, mv_ref, lb_ref, ng_ref, z_ref, ko_ref, vo_ref, so_ref, kbuf, vbuf, sem):
    step = pl.program_id(0)
    slot = step % 2
    n_rows = p_ref.shape[0]
    pieces = _window_pieces(past)

    def window_copies(stp, sl):
        out = []
        for rr in range(n_rows):
            for ci, (views, buf) in enumerate((((k0_hbm, k1_hbm, k2_hbm), kbuf), ((v0_hbm, v1_hbm, v2_hbm), vbuf))):
                off = 0
                for pi, (vi, start, count, res) in enumerate(pieces):
                    view, row = views[vi], stp * n_rows + rr
                    src = (view.at[layer, row, :, pl.ds(start, count), :] if res is None
                           else view.at[layer, row, :, pl.ds(start, count), res, :])
                    out.append(pltpu.make_async_copy(src, buf.at[sl, rr, :, pl.ds(off, count), :],
                                                     sem.at[sl, rr, ci, pi]))
                    off += count
        return out

    @pl.when(step == 0)
    def _():
        for c in window_copies(step, slot):
            c.start()

    @pl.when(step + 1 < pl.num_programs(0))
    def _():
        for c in window_copies(step + 1, 1 - slot):
            c.start()

    for c in window_copies(step, slot):
        c.wait()

    cos, sin, lb_all = cos_ref[...], sin_ref[...], lb_ref[...]
    for rr in range(n_rows):
        _sample_row(layer, p_ref.at[rr], cos, sin, kbuf.at[slot, rr], vbuf.at[slot, rr], st_ref.at[rr], mk_ref.at[rr],
                    mv_ref.at[rr], lb_all, ng_ref, z_ref.at[rr], ko_ref.at[rr], vo_ref.at[rr], so_ref.at[rr])


def _sample_mixers(proj_rows, cos, sin, win_k, win_v, state, mem_k, mem_v, lb_raw, norm_g, layer):
    bsz, past = win_k.shape[1:3]
    rps = SAMPLE_ROWS
    assert all(past % d == 0 and past >= w for w, d in zip(WINDOWS, DILATIONS)) and bsz % rps == 0
    assert all(w // d == BAND for w, d in zip(WINDOWS, DILATIONS)) and past % HBM_TILE_ROWS == 0
    n_keys = BAND * len(DILATIONS)
    n_pieces = len(_window_pieces(past))
    any_spec = pl.BlockSpec(memory_space=pl.ANY)
    vec = lambda n: pl.BlockSpec((n, W_HGRN), lambda b: (0, 0))
    heads = lambda n: pl.BlockSpec((rps, n, HEAD_DIM), lambda b: (b, 0, 0))
    fold = HBM_TILE_ROWS // N_HEADS_MEM
    mem_k, mem_v = (a.reshape(a.shape[0], bsz, N_MEM // fold, fold * N_HEADS_MEM, HEAD_DIM) for a in (mem_k, mem_v))
    mem_spec = pl.BlockSpec((None, rps, N_MEM // fold, fold * N_HEADS_MEM, HEAD_DIM), lambda b: (layer, b, 0, 0, 0))
    return pl.pallas_call(
        functools.partial(_sample_kernel, layer, past),
        grid=(bsz // rps,),
        in_specs=[heads(N_GROUPS), pl.BlockSpec((1, HEAD_DIM), lambda b: (0, 0)),
                  pl.BlockSpec((1, HEAD_DIM), lambda b: (0, 0)), *([any_spec] * 6),
                  pl.BlockSpec((None, rps, N_HEADS_HGRN, HEAD_DIM, HEAD_DIM), lambda b: (layer, b, 0, 0, 0)),
                  mem_spec, mem_spec, vec(DEPTH + 1), vec(1)],
        out_specs=[heads(MIX_WIDTH // HEAD_DIM), heads(N_HEADS_ATTN), heads(N_HEADS_ATTN),
                   pl.BlockSpec((rps, N_HEADS_HGRN, HEAD_DIM, HEAD_DIM), lambda b: (b, 0, 0, 0))],
        out_shape=[jax.ShapeDtypeStruct((bsz, MIX_WIDTH // HEAD_DIM, HEAD_DIM), F32),
                   jax.ShapeDtypeStruct((bsz, N_HEADS_ATTN, HEAD_DIM), F32),
                   jax.ShapeDtypeStruct((bsz, N_HEADS_ATTN, HEAD_DIM), F32),
                   jax.ShapeDtypeStruct((bsz, N_HEADS_HGRN, HEAD_DIM, HEAD_DIM), F32)],
        scratch_shapes=[pltpu.VMEM((2, rps, N_HEADS_ATTN, n_keys, HEAD_DIM), F32),
                        pltpu.VMEM((2, rps, N_HEADS_ATTN, n_keys, HEAD_DIM), F32),
                        pltpu.SemaphoreType.DMA((2, rps, 2, n_pieces))],
        compiler_params=_params(("arbitrary",)),
        name="sample_mixers",
    )(proj_rows, cos, sin, *_cache_views(win_k), *_cache_views(win_v), state, mem_k, mem_v, lb_raw, norm_g)


def _rope_tables(pos):
    half = HEAD_DIM // 2
    inv_freq = 1.0 / (ROPE_THETA ** (np.arange(half, dtype=np.float64) / half))
    ang = np.asarray(pos, np.float64)[:, None] * inv_freq[None, :]
    cos, sin = np.cos(ang), np.sin(ang)
    return (jnp.asarray(np.concatenate([cos, cos], axis=-1), F32),
            jnp.asarray(np.concatenate([-sin, sin], axis=-1), F32))


def kernel(x_prompt, x_sample, cache_win_k, cache_win_v, state_hgrn, cache_mem_k, cache_mem_v, mem_prompt,
           w_in, w_mem_kv, hgrn_lb_raw, hgrn_norm_g, w_out, ln_g, ln_b):
    bsz, seq, _ = x_prompt.shape
    dbsz, n_new, _ = x_sample.shape
    assert n_new == 1 and seq % (BAND * max(DILATIONS)) == 0
    cos_p, sin_p = _rope_tables(np.arange(seq))
    cos_s, sin_s = _rope_tables(PAST_LEN + np.arange(n_new))

    hp = x_prompt.reshape(bsz * seq, D_MODEL)
    hs = x_sample.reshape(dbsz * n_new, D_MODEL)
    mem = mem_prompt.reshape(bsz * N_MEM, D_MODEL)
    outs = [[] for _ in range(8)]
    for layer in range(DEPTH):
        w_out_b = w_out[layer].astype(BF16)
        lb_raw = hgrn_lb_raw
        norm_g = hgrn_norm_g[layer][None]
        lg, lbias = ln_g[layer][None], ln_b[layer][None]

        proj, proj_s = _project(hp, hs, w_in[layer], 1024, 512, 1024)
        za, k1, v1 = _prompt_attention(proj, cos_p, sin_p, bsz, seq)
        zh, s1 = _prompt_hgrn(proj, lb_raw, norm_g, layer, bsz, seq)
        zm, mk1, mv1 = _prompt_mem(proj, mem, w_mem_kv[layer], bsz, seq)
        hp = _merge(za, zh, zm, hp, w_out_b, lg, lbias, 512)

        proj_s = proj_s.transpose(1, 0, 2)
        zs, k2, v2, s2 = _sample_mixers(proj_s, cos_s, sin_s, cache_win_k, cache_win_v, state_hgrn,
                                        cache_mem_k, cache_mem_v, lb_raw, norm_g, layer)
        zs = zs.transpose(1, 0, 2)
        hs = _merge(zs[:N_HEADS_ATTN], zs[N_HEADS_ATTN:N_HEADS_ATTN + N_HEADS_HGRN],
                    zs[N_HEADS_ATTN + N_HEADS_HGRN:], hs, w_out_b, lg, lbias, dbsz)

        new = (k1.transpose(0, 2, 1, 3), v1.transpose(0, 2, 1, 3), s1, mk1, mv1,
               k2.reshape(dbsz, n_new, N_HEADS_ATTN, HEAD_DIM), v2.reshape(dbsz, n_new, N_HEADS_ATTN, HEAD_DIM),
               s2.astype(state_hgrn.dtype))
        for acc, val in zip(outs, new):
            acc.append(val)

    return (hp.reshape(bsz, seq, D_MODEL), hs.reshape(dbsz, n_new, D_MODEL), *[jnp.stack(o) for o in outs])
```

```python
import functools

import jax
import jax.numpy as jnp
import numpy as np
from jax import lax
from jax.experimental import pallas as pl
from jax.experimental.pallas import tpu as pltpu

F32 = jnp.float32
BF16 = jnp.bfloat16

D_MODEL = 2048
DEPTH = 1
PAST_LEN = 8192
HEAD_DIM = 128
N_HEADS_ATTN = 6
N_HEADS_HGRN = 6
N_HEADS_MEM = 4
W_ATTN = N_HEADS_ATTN * HEAD_DIM
W_HGRN = N_HEADS_HGRN * HEAD_DIM
W_MEM = N_HEADS_MEM * HEAD_DIM
MIX_WIDTH = W_ATTN + W_HGRN + W_MEM
WINDOWS = (128, 512, 2048)
DILATIONS = (1, 4, 16)
N_MEM = 256
ROPE_THETA = 10000.0
LN_EPS = 1e-5
RMS_EPS = 1e-6
NEG_INF = -1e30
DEEPNORM_ALPHA = (2 * DEPTH) ** 0.25
ATTN_SCALE = HEAD_DIM ** -0.5
ATTN_SCALE_LOG2 = ATTN_SCALE * 1.4426950408889634

G_QA, G_KA, G_VA, G_GA = 0, 6, 12, 18
G_QB, G_FB, G_IB, G_GB = 24, 30, 36, 42
G_QM, G_GM = 48, 52
N_GROUPS = 56

LANES = 128
BAND = 128
ATTN_GROUP = 16
HGRN_CHUNK = 64
HGRN_SUB = 16
HGRN_HEADS_PER_STEP = 3
HGRN_PREP_ROWS = 256
HGRN_SCAN_CHUNKS = 16
HGRN_SAFE_LOG_RANGE = 70.0
VMEM_LIMIT = 48 * 1024 * 1024


def _params(sem, vmem=VMEM_LIMIT):
    return pltpu.CompilerParams(dimension_semantics=sem, vmem_limit_bytes=vmem)


def _sigmoid(x):
    return 1.0 / (1.0 + jnp.exp(-x))


def _silu(x):
    return x * _sigmoid(x)


def _dot_nt(a, b):
    return lax.dot_general(a, b, (((1,), (1,)), ((), ())), preferred_element_type=F32)


def _store_slabs(ref, acc):
    for c in range(ref.shape[0]):
        ref[c] = acc[:, c * LANES:(c + 1) * LANES]


def _proj_first_kernel(x_ref, xs_ref, w_ref, o_ref, os_ref, wb_ref, xb_ref):
    tm = x_ref.shape[0]

    @pl.when(pl.program_id(0) == 0)
    def _():
        xb_ref[:tm, :] = x_ref[...].astype(BF16)
        xb_ref[tm:, :] = xs_ref[...].astype(BF16)

    wb_ref[...] = w_ref[...].astype(BF16)
    acc = jnp.dot(xb_ref[...], wb_ref[...], preferred_element_type=F32)
    _store_slabs(o_ref, acc[:tm])
    _store_slabs(os_ref, acc[tm:])


def _proj_rest_kernel(x_ref, w_ref, first_ref, o_hbm, xb_ref, obuf, sem_out, sem_first):
    i, j = pl.program_id(0), pl.program_id(1)
    n_col = pl.num_programs(1)
    step = i * n_col + j
    last = pl.num_programs(0) * n_col - 1
    slot = step % 2
    tm = x_ref.shape[0]
    n_slab = obuf.shape[1]

    def tile_copy(stp):
        row_tile, col_tile = stp // n_col + 1, stp % n_col
        return pltpu.make_async_copy(
            obuf.at[stp % 2], o_hbm.at[pl.ds(col_tile * n_slab, n_slab), pl.ds(row_tile * tm, tm), :],
            sem_out.at[stp % 2])

    first_copy = pltpu.make_async_copy(first_ref, o_hbm.at[pl.ds(j * n_slab, n_slab), pl.ds(0, tm), :], sem_first.at[0])

    @pl.when(i == 0)
    def _():
        first_copy.start()

    @pl.when(j == 0)
    def _():
        xb_ref[...] = x_ref[...].astype(BF16)

    @pl.when(step >= 2)
    def _():
        tile_copy(step - 2).wait()

    _store_slabs(obuf.at[slot], jnp.dot(xb_ref[...], w_ref[...], preferred_element_type=F32))
    tile_copy(step).start()

    @pl.when(i == 0)
    def _():
        first_copy.wait()

    @pl.when(step == last)
    def _():
        tile_copy(step - 1).wait()
        tile_copy(step).wait()


def _project(x, xs, w, tm, tn_first, tn):
    m, k = x.shape
    ms = xs.shape[0]
    n = w.shape[1]
    assert (m // tm - 1) * (n // tn) >= 2
    proj_first, proj_s, w_bf16 = pl.pallas_call(
        _proj_first_kernel,
        grid=(n // tn_first,),
        in_specs=[pl.BlockSpec((tm, k), lambda j: (0, 0)),
                  pl.BlockSpec((ms, k), lambda j: (0, 0)),
                  pl.BlockSpec((k, tn_first), lambda j: (0, j))],
        out_specs=[pl.BlockSpec((tn_first // LANES, tm, LANES), lambda j: (j, 0, 0)),
                   pl.BlockSpec((tn_first // LANES, ms, LANES), lambda j: (j, 0, 0)),
                   pl.BlockSpec((k, tn_first), lambda j: (0, j))],
        out_shape=[jax.ShapeDtypeStruct((n // LANES, tm, LANES), F32),
                   jax.ShapeDtypeStruct((n // LANES, ms, LANES), F32),
                   jax.ShapeDtypeStruct((k, n), BF16)],
        scratch_shapes=[pltpu.VMEM((tm + ms, k), BF16)],
        compiler_params=_params(("arbitrary",)),
        name="in_proj_first",
    )(x, xs, w)
    proj = pl.pallas_call(
        _proj_rest_kernel,
        grid=(m // tm - 1, n // tn),
        in_specs=[pl.BlockSpec((tm, k), lambda i, j: (i + 1, 0)),
                  pl.BlockSpec((k, tn), lambda i, j: (0, j)),
                  pl.BlockSpec((tn // LANES, tm, LANES), lambda i, j: (jnp.where(i == 0, j, n // tn - 1), 0, 0))],
        out_specs=pl.BlockSpec(memory_space=pl.ANY),
        out_shape=jax.ShapeDtypeStruct((n // LANES, m, LANES), F32),
        scratch_shapes=[pltpu.VMEM((tm, k), BF16), pltpu.VMEM((2, tn // LANES, tm, LANES), F32),
                        pltpu.SemaphoreType.DMA((2,)), pltpu.SemaphoreType.DMA((1,))],
        compiler_params=_params(("arbitrary", "arbitrary")),
        name="in_proj",
    )(x, w_bf16, proj_first)
    return proj, proj_s


def _rope(x, cos, sin_signed):
    return x * cos + pltpu.roll(x, HEAD_DIM // 2, 1) * sin_signed


def _bias_from_count(count):
    return jnp.where(count > 1.5, 1.0, jnp.where(count > 0.5, 0.0, NEG_INF)).astype(F32)


def _block_deltas(block_gap):
    qi = lax.broadcasted_iota(jnp.int32, (BAND, BAND), 0)
    ki = lax.broadcasted_iota(jnp.int32, (BAND, BAND), 1)
    return block_gap * BAND + qi - ki


def _near_bias(block_gap):
    d = _block_deltas(block_gap)
    return _bias_from_count(((d >= 0) & (d <= BAND)).astype(F32))


def _class_bias(block_gap, far_step):
    d = _block_deltas(block_gap)
    near = (d >= 0) & (d <= BAND)
    far = (d >= 0) & ((d & (far_step - 1)) == 0)
    return _bias_from_count(near.astype(F32) + far.astype(F32))


def _softmax_stage(scores, floors=None):
    out = []
    for idx, s in enumerate(scores):
        m = jnp.broadcast_to(jnp.max(s, axis=-1, keepdims=True), (BAND, HEAD_DIM))
        if floors is not None:
            m = jnp.maximum(m, floors[idx])
        m_wide = jnp.concatenate([m] * (s.shape[1] // HEAD_DIM), axis=1)
        out.append((m, jnp.exp2(s - m_wide).astype(BF16)))
    return out


def _attn_kernel(q_ref, k_ref, v_ref, g_ref, cos_ref, sin_ref, z_ref, ko_ref, vo_ref,
                 qs_ref, qb_ref, kb_ref, vb_ref, qc_ref, kc_ref, vc_ref, acc_ref, m_ref, l_ref):
    seq = q_ref.shape[0]
    d_mid, d_far = DILATIONS[1], DILATIONS[2]
    far_step = d_far // d_mid
    cls_rows = seq // d_mid
    cos = cos_ref[...]
    sin = sin_ref[...]
    qs_ref[...] = _rope(q_ref[...], cos, sin) * ATTN_SCALE_LOG2
    ko_ref[...] = _rope(k_ref[...], cos, sin)
    vo_ref[...] = v_ref[...]
    qb_ref[...] = qs_ref[...].astype(BF16)
    kb_ref[...] = ko_ref[...].astype(BF16)
    ones = jnp.ones((seq, HEAD_DIM), BF16)
    vb_ref[:, :HEAD_DIM] = v_ref[...].astype(BF16)
    vb_ref[:, HEAD_DIM:] = ones
    vc_ref[:, HEAD_DIM:] = ones
    for r in range(d_mid):
        cls, dst = pl.ds(r, cls_rows, stride=d_mid), pl.ds(r * cls_rows, cls_rows)
        qc_ref[dst, :] = qs_ref[cls, :].astype(BF16)
        kc_ref[dst, :] = ko_ref[cls, :].astype(BF16)
        vc_ref[dst, :HEAD_DIM] = v_ref[cls, :].astype(BF16)

    n_cls_blk = cls_rows // BAND
    cls_bias = [_class_bias(gap, far_step) for gap in range(min(n_cls_blk, 3))]
    items = [(r, n) for r in range(d_mid) for n in range(n_cls_blk)]
    for g0 in range(0, len(items), ATTN_GROUP):
        group = items[g0:g0 + ATTN_GROUP]
        keys = [pl.ds(r * cls_rows, (n + 1) * BAND) for r, n in group]
        scores = [_dot_nt(qc_ref[pl.ds(r * cls_rows + n * BAND, BAND), :], kc_ref[kr, :])
                  + jnp.concatenate([cls_bias[min(n - nk, 2)] for nk in range(n + 1)], axis=1)
                  for (r, n), kr in zip(group, keys)]
        for (r, n), kr, (m, p) in zip(group, keys, _softmax_stage(scores)):
            rows = pl.ds(r + d_mid * BAND * n, BAND, stride=d_mid)
            pv = jnp.dot(p, vc_ref[kr, :], preferred_element_type=F32)
            acc_ref[rows, :] = pv[:, :HEAD_DIM]
            l_ref[rows, :] = pv[:, HEAD_DIM:]
            m_ref[rows, :] = m

    near_bias = [_near_bias(0), jnp.concatenate([_near_bias(1), _near_bias(0)], axis=1)]
    items = list(range(seq // BAND))
    for g0 in range(0, len(items), ATTN_GROUP):
        group = items[g0:g0 + ATTN_GROUP]
        rows = [pl.ds(n * BAND, BAND) for n in group]
        keys = [pl.ds(max(n - 1, 0) * BAND, BAND * min(n + 1, 2)) for n in group]
        scores = [_dot_nt(qb_ref[rw, :], kb_ref[kr, :]) + near_bias[min(n, 1)]
                  for n, rw, kr in zip(group, rows, keys)]
        probs = _softmax_stage(scores, floors=[m_ref[rw, :] for rw in rows])
        for rw, kr, (m, p) in zip(rows, keys, probs):
            w = jnp.exp2(m_ref[rw, :] - m)
            pv = jnp.dot(p, vb_ref[kr, :], preferred_element_type=F32)
            num = pv[:, :HEAD_DIM] + w * acc_ref[rw, :]
            den = pv[:, HEAD_DIM:] + w * l_ref[rw, :]
            z_ref[rw, :] = ((num / den) * _silu(g_ref[rw, :])).astype(z_ref.dtype)


def _prompt_attention(proj, cos, sin, bsz, seq):
    def col(g0):
        return pl.BlockSpec((None, seq, HEAD_DIM), lambda b, h: (g0 + h, b, 0))

    table = pl.BlockSpec((seq, HEAD_DIM), lambda b, h: (0, 0))
    kv_out = pl.BlockSpec((None, None, seq, HEAD_DIM), lambda b, h: (b, h, 0, 0))
    d_near, d_mid, d_far = DILATIONS
    far_step = d_far // d_mid
    assert d_near == 1 and d_far % d_mid == 0 and far_step & (far_step - 1) == 0
    assert all(w // d == BAND for w, d in zip(WINDOWS, DILATIONS)) and seq % (d_mid * BAND) == 0
    assert seq // d_mid <= far_step * BAND
    return pl.pallas_call(
        _attn_kernel,
        grid=(bsz, N_HEADS_ATTN),
        in_specs=[col(G_QA), col(G_KA), col(G_VA), col(G_GA), table, table],
        out_specs=[pl.BlockSpec((None, seq, HEAD_DIM), lambda b, h: (h, b, 0)), kv_out, kv_out],
        out_shape=[jax.ShapeDtypeStruct((N_HEADS_ATTN, bsz * seq, HEAD_DIM), BF16),
                   jax.ShapeDtypeStruct((bsz, N_HEADS_ATTN, seq, HEAD_DIM), F32),
                   jax.ShapeDtypeStruct((bsz, N_HEADS_ATTN, seq, HEAD_DIM), F32)],
        scratch_shapes=[pltpu.VMEM((seq, HEAD_DIM), F32)]
                       + [pltpu.VMEM((seq, HEAD_DIM), BF16), pltpu.VMEM((seq, HEAD_DIM), BF16),
                          pltpu.VMEM((seq, 2 * HEAD_DIM), BF16)] * 2
                       + [pltpu.VMEM((seq, HEAD_DIM), F32)] * 3,
        compiler_params=_params(("parallel", "parallel")),
        name="prompt_attn",
    )(proj, proj, proj, proj, cos, sin)


def _lower_bound(lb_raw, layer):
    e = jnp.exp(lb_raw - jnp.max(lb_raw, axis=0, keepdims=True))
    sm = e / jnp.sum(e, axis=0, keepdims=True)
    return jnp.sum(sm[:layer + 1], axis=0, keepdims=True)


def _split2(x):
    hi = x.astype(BF16)
    return hi, (x - hi.astype(F32)).astype(BF16)


def _rms_gate(o, norm_g, gate):
    o = o * lax.rsqrt(jnp.mean(o * o, axis=-1, keepdims=True) + RMS_EPS)
    return o * norm_g * _silu(gate)


def _hgrn_gates(fb, lb, tril):
    f = lb + (1.0 - lb) * _sigmoid(fb)
    g = jnp.log(f)
    b = sum(jnp.dot(tril, piece, preferred_element_type=F32) for piece in _split2(g))
    return 1.0 - f, b


def _hgrn_fast_prepare(q_ref, f_ref, lb, qh_ref, kh_ref, el_ref):
    n_heads, seq, _ = q_ref.shape
    blk = HGRN_PREP_ROWS
    tril = _tril_ones(blk)

    def body(bi, carry):
        r0 = pl.multiple_of(bi * blk, blk)
        fb = jnp.concatenate([f_ref[h, pl.ds(r0, blk), :] for h in range(n_heads)], axis=1)
        f = lb + (1.0 - lb) * _sigmoid(fb)
        b_blk = sum(jnp.dot(tril, piece, preferred_element_type=F32) for piece in _split2(jnp.log(f)))
        kk = 1.0 - f
        for c0 in range(0, blk, HGRN_CHUNK):
            rs = slice(c0, c0 + HGRN_CHUNK)
            b = b_blk[rs] - b_blk[c0 - 1:c0] if c0 else b_blk[rs]
            rows = pl.ds(r0 + c0, HGRN_CHUNK)
            q = jnp.concatenate([q_ref[h, rows, :] for h in range(n_heads)], axis=1)
            qh_ref[rows, :] = (q * jnp.exp(b)).astype(BF16)
            kh_ref[rows, :] = (kk[rs] * jnp.exp(-b)).astype(BF16)
            el_ref[pl.ds(bi * (blk // HGRN_CHUNK) + c0 // HGRN_CHUNK, 1), :] = jnp.exp(b[HGRN_CHUNK - 1:])
        return carry

    lax.fori_loop(0, seq // blk, body, 0, unroll=4)


def _hgrn_fast_scan(i_ref, g_ref, norm_g, qh_ref, kh_ref, el_ref, st_ref, z_ref):
    n_heads, seq, _ = i_ref.shape
    c = HGRN_CHUNK
    heads = range(n_heads)
    chunks = range(HGRN_SCAN_CHUNKS)
    lanes = [slice(h * HEAD_DIM, (h + 1) * HEAD_DIM) for h in heads]
    causal = lax.broadcasted_iota(jnp.int32, (c, c), 0) >= lax.broadcasted_iota(jnp.int32, (c, c), 1)
    tn = (((0,), (0,)), ((), ()))

    def body(ti, carry):
        rows = [pl.ds(pl.multiple_of((ti * HGRN_SCAN_CHUNKS + k) * c, c), c) for k in chunks]
        qh = [[qh_ref[rows[k], lanes[h]] for h in heads] for k in chunks]
        kh = [[kh_ref[rows[k], lanes[h]] for h in heads] for k in chunks]
        vb = [[i_ref[h, rows[k], :].astype(BF16) for h in heads] for k in chunks]
        att = [[jnp.where(causal, _dot_nt(qh[k][h], kh[k][h]), 0.0).astype(BF16) for h in heads] for k in chunks]
        ds = [[lax.dot_general(vb[k][h], kh[k][h], tn, preferred_element_type=F32) for h in heads] for k in chunks]
        st = [st_ref[h] for h in heads]
        o = []
        for k in chunks:
            el = el_ref[pl.ds(ti * HGRN_SCAN_CHUNKS + k, 1), :]
            o.append([_dot_nt(qh[k][h], st[h].astype(BF16)) for h in heads])
            st = [(st[h] + ds[k][h]) * el[:, lanes[h]] for h in heads]
        for h in heads:
            st_ref[h] = st[h]
        for k in chunks:
            for h in heads:
                o_kh = o[k][h] + jnp.dot(att[k][h], vb[k][h], preferred_element_type=F32)
                z_ref[h, rows[k], :] = _rms_gate(o_kh, norm_g[:, lanes[h]], g_ref[h, rows[k], :]).astype(z_ref.dtype)
        return carry

    lax.fori_loop(0, seq // (c * HGRN_SCAN_CHUNKS), body, 0)


def _hgrn_chunk(q, fb, v, lb, st, tril):
    c = q.shape[0]
    kk, b = _hgrn_gates(fb, lb, tril)
    o = _dot_nt((q * jnp.exp(b)).astype(BF16), st.astype(BF16))

    s_idx = lax.broadcasted_iota(jnp.int32, (c, 1), 0)
    lane = lax.broadcasted_iota(jnp.int32, (HGRN_SUB, c), 1)
    row = lax.broadcasted_iota(jnp.int32, (HGRN_SUB, c), 0)
    att_rows = []
    for i0 in range(0, c, HGRN_SUB):
        qi = q[i0:i0 + HGRN_SUB]
        bi = b[i0:i0 + HGRN_SUB]
        if i0 > 0:
            bref = b[i0 - 1:i0]
            kt = jnp.where(s_idx < i0, kk * jnp.exp(jnp.minimum(bref - b, 0.0)), 0.0)
            att = _dot_nt((qi * jnp.exp(bi - bref)).astype(BF16), kt.astype(BF16))
        else:
            att = jnp.zeros((HGRN_SUB, c), F32)
        for j in range(HGRN_SUB):
            s = i0 + j
            e = jnp.exp(jnp.minimum(bi - b[s:s + 1], 0.0))
            colv = jnp.sum(qi * kk[s:s + 1] * e, axis=-1, keepdims=True)
            att = jnp.where((lane == s) & (row >= j), colv, att)
        att_rows.append(att)
    att = jnp.concatenate(att_rows, axis=0)
    vb = v.astype(BF16)
    o = o + jnp.dot(att.astype(BF16), vb, preferred_element_type=F32)

    b_last = b[c - 1:c]
    kd = (kk * jnp.exp(b_last - b)).astype(BF16)
    st_new = st * jnp.exp(b_last) + lax.dot_general(vb, kd, (((0,), (0,)), ((), ())),
                                                    preferred_element_type=F32)
    return o, st_new


def _tril_ones(c):
    return (lax.broadcasted_iota(jnp.int32, (c, c), 0) >= lax.broadcasted_iota(jnp.int32, (c, c), 1)
            ).astype(BF16)


def _hgrn_kernel(layer, q_ref, f_ref, i_ref, g_ref, lb_ref, ng_ref, z_ref, s_ref, st_ref, qh_ref, kh_ref, el_ref):
    n_heads, seq, _ = q_ref.shape
    lb = _lower_bound(lb_ref[...], layer)
    norm_g = ng_ref[...]
    st_ref[...] = jnp.zeros_like(st_ref)

    def fast():
        _hgrn_fast_prepare(q_ref, f_ref, lb, qh_ref, kh_ref, el_ref)
        _hgrn_fast_scan(i_ref, g_ref, norm_g, qh_ref, kh_ref, el_ref, st_ref, z_ref)

    def safe():
        tril = _tril_ones(HGRN_CHUNK)

        def body(ci, carry):
            rows = pl.ds(pl.multiple_of(ci * HGRN_CHUNK, HGRN_CHUNK), HGRN_CHUNK)
            for h in range(n_heads):
                lanes = slice(h * HEAD_DIM, (h + 1) * HEAD_DIM)
                o, st = _hgrn_chunk(q_ref[h, rows, :], f_ref[h, rows, :], i_ref[h, rows, :], lb[:, lanes],
                                    st_ref[h], tril)
                st_ref[h] = st
                z_ref[h, rows, :] = _rms_gate(o, norm_g[:, lanes], g_ref[h, rows, :]).astype(z_ref.dtype)
            return carry

        lax.fori_loop(0, seq // HGRN_CHUNK, body, 0)

    fast_ok = HGRN_CHUNK * -jnp.log(jnp.min(lb)) <= HGRN_SAFE_LOG_RANGE
    lax.cond(fast_ok, fast, safe)
    for h in range(n_heads):
        s_ref[h] = st_ref[h].T


def _prompt_hgrn(proj, lb_raw, norm_g, layer, bsz, seq):
    hps = HGRN_HEADS_PER_STEP
    assert N_HEADS_HGRN % hps == 0 and all(g % hps == 0 for g in (G_QB, G_FB, G_IB, G_GB))
    assert seq % (HGRN_CHUNK * HGRN_SCAN_CHUNKS) == 0 and seq % HGRN_PREP_ROWS == 0 and HGRN_PREP_ROWS % HGRN_CHUNK == 0

    def cols(g0):
        return pl.BlockSpec((hps, seq, HEAD_DIM), lambda b, h: (g0 // hps + h, b, 0))

    return pl.pallas_call(
        functools.partial(_hgrn_kernel, layer),
        grid=(bsz, N_HEADS_HGRN // hps),
        in_specs=[cols(G_QB), cols(G_FB), cols(G_IB), cols(G_GB),
                  pl.BlockSpec((DEPTH + 1, hps * HEAD_DIM), lambda b, h: (0, h)),
                  pl.BlockSpec((1, hps * HEAD_DIM), lambda b, h: (0, h))],
        out_specs=[pl.BlockSpec((hps, seq, HEAD_DIM), lambda b, h: (h, b, 0)),
                   pl.BlockSpec((None, hps, HEAD_DIM, HEAD_DIM), lambda b, h: (b, h, 0, 0))],
        out_shape=[jax.ShapeDtypeStruct((N_HEADS_HGRN, bsz * seq, HEAD_DIM), BF16),
                   jax.ShapeDtypeStruct((bsz, N_HEADS_HGRN, HEAD_DIM, HEAD_DIM), F32)],
        scratch_shapes=[pltpu.VMEM((hps, HEAD_DIM, HEAD_DIM), F32),
                        pltpu.VMEM((seq, hps * HEAD_DIM), BF16),
                        pltpu.VMEM((seq, hps * HEAD_DIM), BF16),
                        pltpu.VMEM((seq // HGRN_CHUNK, hps * HEAD_DIM), F32)],
        compiler_params=_params(("parallel", "parallel")),
        name="prompt_hgrn",
    )(proj, proj, proj, proj, lb_raw, norm_g)


MEM_ROWS = 256
MEM_GROUP = 8


def _mem_kernel(q_ref, g_ref, mem_ref, w_ref, z_ref, mk_ref, mv_ref):
    seq = q_ref.shape[1]
    kv = jnp.dot(mem_ref[...].astype(BF16), w_ref[...].astype(BF16), preferred_element_type=F32)
    ones = jnp.ones((N_MEM, HEAD_DIM), BF16)
    mk_b, mv_b = [], []
    for h in range(N_HEADS_MEM):
        mk = kv[:, h * HEAD_DIM:(h + 1) * HEAD_DIM]
        mv = kv[:, W_MEM + h * HEAD_DIM:W_MEM + (h + 1) * HEAD_DIM]
        mk_ref[:, h, :] = mk
        mv_ref[:, h, :] = mv
        mk_b.append(mk.astype(BF16))
        mv_b.append(jnp.concatenate([mv.astype(BF16), ones], axis=1))

    items = [(h, pl.ds(n * MEM_ROWS, MEM_ROWS)) for h in range(N_HEADS_MEM) for n in range(seq // MEM_ROWS)]
    for g0 in range(0, len(items), MEM_GROUP):
        group = items[g0:g0 + MEM_GROUP]
        scores = [_dot_nt((q_ref[h, rows, :] * ATTN_SCALE_LOG2).astype(BF16), mk_b[h]) for h, rows in group]
        probs = [jnp.exp2(s - jnp.max(s, axis=-1, keepdims=True)).astype(BF16) for s in scores]
        for (h, rows), p in zip(group, probs):
            pv = jnp.dot(p, mv_b[h], preferred_element_type=F32)
            o = pv[:, :HEAD_DIM] / pv[:, HEAD_DIM:]
            z_ref[h, rows, :] = (o * _silu(g_ref[h, rows, :])).astype(z_ref.dtype)


def _prompt_mem(proj, mem, w_kv, bsz, seq):
    assert G_QM % N_HEADS_MEM == 0 and G_GM % N_HEADS_MEM == 0

    def cols(g0):
        return pl.BlockSpec((N_HEADS_MEM, seq, HEAD_DIM), lambda b: (g0 // N_HEADS_MEM, b, 0))

    kv_out = pl.BlockSpec((None, N_MEM, N_HEADS_MEM, HEAD_DIM), lambda b: (b, 0, 0, 0))
    kv_shape = jax.ShapeDtypeStruct((bsz, N_MEM, N_HEADS_MEM, HEAD_DIM), F32)
    return pl.pallas_call(
        _mem_kernel,
        grid=(bsz,),
        in_specs=[cols(G_QM), cols(G_GM), pl.BlockSpec((N_MEM, D_MODEL), lambda b: (b, 0)),
                  pl.BlockSpec((D_MODEL, 2 * W_MEM), lambda b: (0, 0))],
        out_specs=[pl.BlockSpec((N_HEADS_MEM, seq, HEAD_DIM), lambda b: (0, b, 0)), kv_out, kv_out],
        out_shape=[jax.ShapeDtypeStruct((N_HEADS_MEM, bsz * seq, HEAD_DIM), BF16), kv_shape, kv_shape],
        compiler_params=_params(("parallel",)),
        name="prompt_mem",
    )(proj, proj, mem, w_kv)


def _merge_kernel(za_ref, zh_ref, zm_ref, x_ref, w_ref, lg_ref, lb_ref, o_ref, z_ref):
    c0 = 0
    for ref in (za_ref, zh_ref, zm_ref):
        for c in range(ref.shape[0]):
            z_ref[:, (c0 + c) * LANES:(c0 + c + 1) * LANES] = ref[c].astype(BF16)
        c0 += ref.shape[0]
    y = jnp.dot(z_ref[...], w_ref[...], preferred_element_type=F32)
    r = DEEPNORM_ALPHA * x_ref[...] + y
    mu = jnp.mean(r, axis=-1, keepdims=True)
    d = r - mu
    var = jnp.mean(d * d, axis=-1, keepdims=True)
    o_ref[...] = d * lax.rsqrt(var + LN_EPS) * lg_ref[...] + lb_ref[...]


def _merge(za, zh, zm, x, w_out_bf16, ln_g, ln_b, tm):
    m = x.shape[0]

    def slab(a):
        return pl.BlockSpec((a.shape[0], tm, LANES), lambda i: (0, i, 0))

    const = lambda shape: pl.BlockSpec(shape, lambda i: (0, 0))
    return pl.pallas_call(
        _merge_kernel,
        grid=(m // tm,),
        in_specs=[slab(za), slab(zh), slab(zm), pl.BlockSpec((tm, D_MODEL), lambda i: (i, 0)),
                  const((MIX_WIDTH, D_MODEL)), const((1, D_MODEL)), const((1, D_MODEL))],
        out_specs=pl.BlockSpec((tm, D_MODEL), lambda i: (i, 0)),
        out_shape=jax.ShapeDtypeStruct((m, D_MODEL), F32),
        scratch_shapes=[pltpu.VMEM((tm, MIX_WIDTH), BF16)],
        compiler_params=_params(("parallel",)),
        name="merge",
    )(za, zh, zm, x, w_out_bf16, ln_g, ln_b)


def _column(row):
    return jnp.broadcast_to(row, (HEAD_DIM, HEAD_DIM)).T


HBM_TILE_ROWS = 8
SAMPLE_ROWS = 2


def _window_pieces(past):
    pieces = []
    for dil in DILATIONS:
        if dil == 1:
            pieces.append((0, past - BAND, BAND, None))
        elif dil < HBM_TILE_ROWS:
            n = BAND * dil // HBM_TILE_ROWS
            pieces += [(1, past // HBM_TILE_ROWS - n, n, r) for r in range(0, HBM_TILE_ROWS, dil)]
        else:
            pieces.append((2, past // dil - BAND, BAND, 0))
    return pieces


def _cache_views(cache):
    depth, bsz, past, nh, hd = cache.shape
    assert all(d == 1 or HBM_TILE_ROWS % d == 0 or d % HBM_TILE_ROWS == 0 for d in DILATIONS)
    big = max(DILATIONS)
    hm = cache.transpose(0, 1, 3, 2, 4)
    return (hm, hm.reshape(depth, bsz, nh, past // HBM_TILE_ROWS, HBM_TILE_ROWS, hd),
            hm.reshape(depth, bsz, nh, past // big, big, hd))


def _sample_row(layer, p_ref, cos, sin, k_win, v_win, st_ref, mk_ref, mv_ref, lb_all, ng_ref, z_ref, ko_ref, vo_ref,
                so_ref):
    q_all = _rope(p_ref[G_QA:G_QA + N_HEADS_ATTN, :], cos, sin) * ATTN_SCALE
    k_all = _rope(p_ref[G_KA:G_KA + N_HEADS_ATTN, :], cos, sin)
    v_all = p_ref[G_VA:G_VA + N_HEADS_ATTN, :]
    ko_ref[...] = k_all
    vo_ref[...] = v_all

    for h in range(N_HEADS_ATTN):
        q = q_all[h:h + 1]
        s_new = jnp.sum(q * k_all[h:h + 1], axis=-1, keepdims=True)
        s = jnp.sum(k_win[h] * q, axis=-1, keepdims=True)
        m = jnp.maximum(jnp.max(s, axis=0, keepdims=True), s_new)
        p = jnp.exp(s - m)
        p_new = jnp.exp(s_new - m) * len(DILATIONS)
        den = jnp.sum(p, axis=0, keepdims=True) + p_new
        num = jnp.sum(p * v_win[h], axis=0, keepdims=True) + p_new * v_all[h:h + 1]
        z_ref[h:h + 1, :] = (num / den) * _silu(p_ref[G_GA + h:G_GA + h + 1, :])

    for h in range(N_HEADS_HGRN):
        lanes = slice(h * HEAD_DIM, (h + 1) * HEAD_DIM)
        lb = _lower_bound(lb_all[:, lanes], layer)
        f = lb + (1.0 - lb) * _sigmoid(p_ref[G_FB + h:G_FB + h + 1, :])
        f_col = _column(f)
        q_col = _column(p_ref[G_QB + h:G_QB + h + 1, :])
        s_new = f_col * st_ref[h] + (1.0 - f_col) * p_ref[G_IB + h:G_IB + h + 1, :]
        so_ref[h] = s_new
        o = jnp.sum(s_new * q_col, axis=0, keepdims=True)
        z_ref[N_HEADS_ATTN + h:N_HEADS_ATTN + h + 1, :] = _rms_gate(
            o, ng_ref[:, lanes], p_ref[G_GB + h:G_GB + h + 1, :])

    fold = mk_ref.shape[1] // N_HEADS_MEM
    unfold = lambda a: sum(a[i * N_HEADS_MEM:(i + 1) * N_HEADS_MEM] for i in range(fold))
    q_m = p_ref[G_QM:G_QM + N_HEADS_MEM, :] * ATTN_SCALE
    s = jnp.sum(mk_ref[...] * jnp.concatenate([q_m] * fold, axis=0)[None], axis=-1, keepdims=True)
    m = jnp.max(s, axis=0)
    m = functools.reduce(jnp.maximum, [m[i * N_HEADS_MEM:(i + 1) * N_HEADS_MEM] for i in range(fold)])
    p = jnp.exp(s - jnp.concatenate([m] * fold, axis=0)[None])
    o = unfold(jnp.sum(p * mv_ref[...], axis=0)) / unfold(jnp.sum(p, axis=0))
    row = N_HEADS_ATTN + N_HEADS_HGRN
    z_ref[row:row + N_HEADS_MEM, :] = o * _silu(p_ref[G_GM:G_GM + N_HEADS_MEM, :])


def _sample_kernel(layer, past, p_ref, cos_ref, sin_ref, k0_hbm, k1_hbm, k2_hbm, v0_hbm, v1_hbm, v2_hbm,
                   st_ref, mk_ref, mv_ref, lb_ref, ng_ref, z_ref, ko_ref, vo_ref, so_ref, kbuf, vbuf, sem):
    step = pl.program_id(0)
    slot = step % 2
    n_rows = p_ref.shape[0]
    pieces = _window_pieces(past)

    def window_copies(stp, sl):
        out = []
        for rr in range(n_rows):
            for ci, (views, buf) in enumerate((((k0_hbm, k1_hbm, k2_hbm), kbuf), ((v0_hbm, v1_hbm, v2_hbm), vbuf))):
                off = 0
                for pi, (vi, start, count, res) in enumerate(pieces):
                    view, row = views[vi], stp * n_rows + rr
                    src = (view.at[layer, row, :, pl.ds(start, count), :] if res is None
                           else view.at[layer, row, :, pl.ds(start, count), res, :])
                    out.append(pltpu.make_async_copy(src, buf.at[sl, rr, :, pl.ds(off, count), :],
                                                     sem.at[sl, rr, ci, pi]))
                    off += count
        return out

    @pl.when(step == 0)
    def _():
        for c in window_copies(step, slot):
            c.start()

    @pl.when(step + 1 < pl.num_programs(0))
    def _():
        for c in window_copies(step + 1, 1 - slot):
            c.start()

    for c in window_copies(step, slot):
        c.wait()

    cos, sin, lb_all = cos_ref[...], sin_ref[...], lb_ref[...]
    for rr in range(n_rows):
        _sample_row(layer, p_ref.at[rr], cos, sin, kbuf.at[slot, rr], vbuf.at[slot, rr], st_ref.at[rr], mk_ref.at[rr],
                    mv_ref.at[rr], lb_all, ng_ref, z_ref.at[rr], ko_ref.at[rr], vo_ref.at[rr], so_ref.at[rr])


def _sample_mixers(proj_rows, cos, sin, win_k, win_v, state, mem_k, mem_v, lb_raw, norm_g, layer):
    bsz, past = win_k.shape[1:3]
    rps = SAMPLE_ROWS
    assert all(past % d == 0 and past >= w for w, d in zip(WINDOWS, DILATIONS)) and bsz % rps == 0
    assert all(w // d == BAND for w, d in zip(WINDOWS, DILATIONS)) and past % HBM_TILE_ROWS == 0
    n_keys = BAND * len(DILATIONS)
    n_pieces = len(_window_pieces(past))
    any_spec = pl.BlockSpec(memory_space=pl.ANY)
    vec = lambda n: pl.BlockSpec((n, W_HGRN), lambda b: (0, 0))
    heads = lambda n: pl.BlockSpec((rps, n, HEAD_DIM), lambda b: (b, 0, 0))
    fold = HBM_TILE_ROWS // N_HEADS_MEM
    mem_k, mem_v = (a.reshape(a.shape[0], bsz, N_MEM // fold, fold * N_HEADS_MEM, HEAD_DIM) for a in (mem_k, mem_v))
    mem_spec = pl.BlockSpec((None, rps, N_MEM // fold, fold * N_HEADS_MEM, HEAD_DIM), lambda b: (layer, b, 0, 0, 0))
    return pl.pallas_call(
        functools.partial(_sample_kernel, layer, past),
        grid=(bsz // rps,),
        in_specs=[heads(N_GROUPS), pl.BlockSpec((1, HEAD_DIM), lambda b: (0, 0)),
                  pl.BlockSpec((1, HEAD_DIM), lambda b: (0, 0)), *([any_spec] * 6),
                  pl.BlockSpec((None, rps, N_HEADS_HGRN, HEAD_DIM, HEAD_DIM), lambda b: (layer, b, 0, 0, 0)),
                  mem_spec, mem_spec, vec(DEPTH + 1), vec(1)],
        out_specs=[heads(MIX_WIDTH // HEAD_DIM), heads(N_HEADS_ATTN), heads(N_HEADS_ATTN),
                   pl.BlockSpec((rps, N_HEADS_HGRN, HEAD_DIM, HEAD_DIM), lambda b: (b, 0, 0, 0))],
        out_shape=[jax.ShapeDtypeStruct((bsz, MIX_WIDTH // HEAD_DIM, HEAD_DIM), F32),
                   jax.ShapeDtypeStruct((bsz, N_HEADS_ATTN, HEAD_DIM), F32),
                   jax.ShapeDtypeStruct((bsz, N_HEADS_ATTN, HEAD_DIM), F32),
                   jax.ShapeDtypeStruct((bsz, N_HEADS_HGRN, HEAD_DIM, HEAD_DIM), F32)],
        scratch_shapes=[pltpu.VMEM((2, rps, N_HEADS_ATTN, n_keys, HEAD_DIM), F32),
                        pltpu.VMEM((2, rps, N_HEADS_ATTN, n_keys, HEAD_DIM), F32),
                        pltpu.SemaphoreType.DMA((2, rps, 2, n_pieces))],
        compiler_params=_params(("arbitrary",)),
        name="sample_mixers",
    )(proj_rows, cos, sin, *_cache_views(win_k), *_cache_views(win_v), state, mem_k, mem_v, lb_raw, norm_g)


def _rope_tables(pos):
    half = HEAD_DIM // 2
    inv_freq = 1.0 / (ROPE_THETA ** (np.arange(half, dtype=np.float64) / half))
    ang = np.asarray(pos, np.float64)[:, None] * inv_freq[None, :]
    cos, sin = np.cos(ang), np.sin(ang)
    return (jnp.asarray(np.concatenate([cos, cos], axis=-1), F32),
            jnp.asarray(np.concatenate([-sin, sin], axis=-1), F32))


def kernel(x_prompt, x_sample, cache_win_k, cache_win_v, state_hgrn, cache_mem_k, cache_mem_v, mem_prompt,
           w_in, w_mem_kv, hgrn_lb_raw, hgrn_norm_g, w_out, ln_g, ln_b):
    bsz, seq, _ = x_prompt.shape
    dbsz, n_new, _ = x_sample.shape
    assert n_new == 1 and seq % (BAND * max(DILATIONS)) == 0
    cos_p, sin_p = _rope_tables(np.arange(seq))
    cos_s, sin_s = _rope_tables(PAST_LEN + np.arange(n_new))

    hp = x_prompt.reshape(bsz * seq, D_MODEL)
    hs = x_sample.reshape(dbsz * n_new, D_MODEL)
    mem = mem_prompt.reshape(bsz * N_MEM, D_MODEL)
    outs = [[] for _ in range(8)]
    for layer in range(DEPTH):
        w_out_b = w_out[layer].astype(BF16)
        lb_raw = hgrn_lb_raw
        norm_g = hgrn_norm_g[layer][None]
        lg, lbias = ln_g[layer][None], ln_b[layer][None]

        proj, proj_s = _project(hp, hs, w_in[layer], 1024, 512, 1024)
        za, k1, v1 = _prompt_attention(proj, cos_p, sin_p, bsz, seq)
        zh, s1 = _prompt_hgrn(proj, lb_raw, norm_g, layer, bsz, seq)
        zm, mk1, mv1 = _prompt_mem(proj, mem, w_mem_kv[layer], bsz, seq)
        hp = _merge(za, zh, zm, hp, w_out_b, lg, lbias, 512)

        proj_s = proj_s.transpose(1, 0, 2)
        zs, k2, v2, s2 = _sample_mixers(proj_s, cos_s, sin_s, cache_win_k, cache_win_v, state_hgrn,
                                        cache_mem_k, cache_mem_v, lb_raw, norm_g, layer)
        zs = zs.transpose(1, 0, 2)
        hs = _merge(zs[:N_HEADS_ATTN], zs[N_HEADS_ATTN:N_HEADS_ATTN + N_HEADS_HGRN],
                    zs[N_HEADS_ATTN + N_HEADS_HGRN:], hs, w_out_b, lg, lbias, dbsz)

        new = (k1.transpose(0, 2, 1, 3), v1.transpose(0, 2, 1, 3), s1, mk1, mv1,
               k2.reshape(dbsz, n_new, N_HEADS_ATTN, HEAD_DIM), v2.reshape(dbsz, n_new, N_HEADS_ATTN, HEAD_DIM),
               s2.astype(state_hgrn.dtype))
        for acc, val in zip(outs, new):
            acc.append(val)

    return (hp.reshape(bsz, seq, D_MODEL), hs.reshape(dbsz, n_new, D_MODEL), *[jnp.stack(o) for o in outs])
```

```python
import functools

import jax
import jax.numpy as jnp
import numpy as np
from jax import lax
from jax.experimental import pallas as pl
from jax.experimental.pallas import tpu as pltpu

F32 = jnp.float32
BF16 = jnp.bfloat16

D_MODEL = 2048
DEPTH = 1
PAST_LEN = 8192
HEAD_DIM = 128
N_HEADS_ATTN = 6
N_HEADS_HGRN = 6
N_HEADS_MEM = 4
W_ATTN = N_HEADS_ATTN * HEAD_DIM
W_HGRN = N_HEADS_HGRN * HEAD_DIM
W_MEM = N_HEADS_MEM * HEAD_DIM
MIX_WIDTH = W_ATTN + W_HGRN + W_MEM
WINDOWS = (128, 512, 2048)
DILATIONS = (1, 4, 16)
N_MEM = 256
ROPE_THETA = 10000.0
LN_EPS = 1e-5
RMS_EPS = 1e-6
NEG_INF = -1e30
DEEPNORM_ALPHA = (2 * DEPTH) ** 0.25
ATTN_SCALE = HEAD_DIM ** -0.5
ATTN_SCALE_LOG2 = ATTN_SCALE * 1.4426950408889634

G_QA, G_KA, G_VA, G_GA = 0, 6, 12, 18
G_QB, G_FB, G_IB, G_GB = 24, 30, 36, 42
G_QM, G_GM = 48, 52
N_GROUPS = 56

LANES = 128
BAND = 128
ATTN_GROUP = 16
HGRN_CHUNK = 64
HGRN_SUB = 16
HGRN_HEADS_PER_STEP = 3
HGRN_PREP_ROWS = 256
HGRN_SCAN_CHUNKS = 16
HGRN_SAFE_LOG_RANGE = 70.0
VMEM_LIMIT = 48 * 1024 * 1024
VMEM_LIMIT_LARGE = 56 * 1024 * 1024


def _params(sem, vmem=VMEM_LIMIT):
    return pltpu.CompilerParams(dimension_semantics=sem, vmem_limit_bytes=vmem)


def _sigmoid(x):
    return 1.0 / (1.0 + jnp.exp(-x))


def _silu(x):
    return x * _sigmoid(x)


def _dot_nt(a, b):
    return lax.dot_general(a, b, (((1,), (1,)), ((), ())), preferred_element_type=F32)


def _store_slabs(ref, acc):
    for c in range(ref.shape[0]):
        ref[c] = acc[:, c * LANES:(c + 1) * LANES]


def _proj_first_kernel(x_ref, xs_ref, w_ref, o_ref, os_ref, wb_ref, xb_ref):
    tm = x_ref.shape[0]

    @pl.when(pl.program_id(0) == 0)
    def _():
        xb_ref[:tm, :] = x_ref[...].astype(BF16)
        xb_ref[tm:, :] = xs_ref[...].astype(BF16)

    wb_ref[...] = w_ref[...].astype(BF16)
    acc = jnp.dot(xb_ref[...], wb_ref[...], preferred_element_type=F32)
    _store_slabs(o_ref, acc[:tm])
    _store_slabs(os_ref, acc[tm:])


def _proj_rest_kernel(x_ref, w_ref, first_ref, o_hbm, xb_ref, obuf, sem_out, sem_first):
    i, j = pl.program_id(0), pl.program_id(1)
    n_col = pl.num_programs(1)
    step = i * n_col + j
    last = pl.num_programs(0) * n_col - 1
    slot = step % 2
    tm = x_ref.shape[0]
    n_slab = obuf.shape[1]

    def tile_copy(stp):
        row_tile, col_tile = stp // n_col + 1, stp % n_col
        return pltpu.make_async_copy(
            obuf.at[stp % 2], o_hbm.at[pl.ds(col_tile * n_slab, n_slab), pl.ds(row_tile * tm, tm), :],
            sem_out.at[stp % 2])

    first_copy = pltpu.make_async_copy(first_ref, o_hbm.at[pl.ds(j * n_slab, n_slab), pl.ds(0, tm), :], sem_first.at[0])

    @pl.when(i == 0)
    def _():
        first_copy.start()

    @pl.when(j == 0)
    def _():
        xb_ref[...] = x_ref[...].astype(BF16)

    @pl.when(step >= 2)
    def _():
        tile_copy(step - 2).wait()

    _store_slabs(obuf.at[slot], jnp.dot(xb_ref[...], w_ref[...], preferred_element_type=F32))
    tile_copy(step).start()

    @pl.when(i == 0)
    def _():
        first_copy.wait()

    @pl.when(step == last)
    def _():
        tile_copy(step - 1).wait()
        tile_copy(step).wait()


def _project(x, xs, w, tm, tn_first, tn):
    m, k = x.shape
    ms = xs.shape[0]
    n = w.shape[1]
    assert (m // tm - 1) * (n // tn) >= 2
    proj_first, proj_s, w_bf16 = pl.pallas_call(
        _proj_first_kernel,
        grid=(n // tn_first,),
        in_specs=[pl.BlockSpec((tm, k), lambda j: (0, 0), pipeline_mode=pl.Buffered(1)),
                  pl.BlockSpec((ms, k), lambda j: (0, 0), pipeline_mode=pl.Buffered(1)),
                  pl.BlockSpec((k, tn_first), lambda j: (0, j))],
        out_specs=[pl.BlockSpec((tn_first // LANES, tm, LANES), lambda j: (j, 0, 0)),
                   pl.BlockSpec((tn_first // LANES, ms, LANES), lambda j: (j, 0, 0)),
                   pl.BlockSpec((k, tn_first), lambda j: (0, j))],
        out_shape=[jax.ShapeDtypeStruct((n // LANES, tm, LANES), F32),
                   jax.ShapeDtypeStruct((n // LANES, ms, LANES), F32),
                   jax.ShapeDtypeStruct((k, n), BF16)],
        scratch_shapes=[pltpu.VMEM((tm + ms, k), BF16)],
        compiler_params=_params(("arbitrary",), vmem=VMEM_LIMIT_LARGE),
        name="in_proj_first",
    )(x, xs, w)
    proj = pl.pallas_call(
        _proj_rest_kernel,
        grid=(m // tm - 1, n // tn),
        in_specs=[pl.BlockSpec((tm, k), lambda i, j: (i + 1, 0)),
                  pl.BlockSpec((k, tn), lambda i, j: (0, j)),
                  pl.BlockSpec((tn // LANES, tm, LANES), lambda i, j: (jnp.where(i == 0, j, n // tn - 1), 0, 0))],
        out_specs=pl.BlockSpec(memory_space=pl.ANY),
        out_shape=jax.ShapeDtypeStruct((n // LANES, m, LANES), F32),
        scratch_shapes=[pltpu.VMEM((tm, k), BF16), pltpu.VMEM((2, tn // LANES, tm, LANES), F32),
                        pltpu.SemaphoreType.DMA((2,)), pltpu.SemaphoreType.DMA((1,))],
        compiler_params=_params(("arbitrary", "arbitrary")),
        name="in_proj",
    )(x, w_bf16, proj_first)
    return proj, proj_s


def _rope(x, cos, sin_signed):
    return x * cos + pltpu.roll(x, HEAD_DIM // 2, 1) * sin_signed


def _bias_from_count(count):
    return jnp.where(count > 1.5, 1.0, jnp.where(count > 0.5, 0.0, NEG_INF)).astype(F32)


def _block_deltas(block_gap):
    qi = lax.broadcasted_iota(jnp.int32, (BAND, BAND), 0)
    ki = lax.broadcasted_iota(jnp.int32, (BAND, BAND), 1)
    return block_gap * BAND + qi - ki


def _near_bias(block_gap):
    d = _block_deltas(block_gap)
    return _bias_from_count(((d >= 0) & (d <= BAND)).astype(F32))


def _class_bias(block_gap, far_step):
    d = _block_deltas(block_gap)
    near = (d >= 0) & (d <= BAND)
    far = (d >= 0) & ((d & (far_step - 1)) == 0)
    return _bias_from_count(near.astype(F32) + far.astype(F32))


def _softmax_stage(scores, floors=None):
    out = []
    for idx, s in enumerate(scores):
        m = jnp.broadcast_to(jnp.max(s, axis=-1, keepdims=True), (BAND, HEAD_DIM))
        if floors is not None:
            m = jnp.maximum(m, floors[idx])
        m_wide = jnp.concatenate([m] * (s.shape[1] // HEAD_DIM), axis=1)
        out.append((m, jnp.exp2(s - m_wide).astype(BF16)))
    return out


def _attn_kernel(q_ref, k_ref, v_ref, g_ref, cos_ref, sin_ref, z_ref, ko_ref, vo_ref,
                 qs_ref, qb_ref, kb_ref, vb_ref, qc_ref, kc_ref, vc_ref, acc_ref, m_ref, l_ref):
    seq = q_ref.shape[0]
    d_mid, d_far = DILATIONS[1], DILATIONS[2]
    far_step = d_far // d_mid
    cls_rows = seq // d_mid
    cos = cos_ref[...]
    sin = sin_ref[...]
    qs_ref[...] = _rope(q_ref[...], cos, sin) * ATTN_SCALE_LOG2
    ko_ref[...] = _rope(k_ref[...], cos, sin)
    vo_ref[...] = v_ref[...]
    qb_ref[...] = qs_ref[...].astype(BF16)
    kb_ref[...] = ko_ref[...].astype(BF16)
    ones = jnp.ones((seq, HEAD_DIM), BF16)
    vb_ref[:, :HEAD_DIM] = v_ref[...].astype(BF16)
    vb_ref[:, HEAD_DIM:] = ones
    vc_ref[:, HEAD_DIM:] = ones
    for r in range(d_mid):
        cls, dst = pl.ds(r, cls_rows, stride=d_mid), pl.ds(r * cls_rows, cls_rows)
        qc_ref[dst, :] = qs_ref[cls, :].astype(BF16)
        kc_ref[dst, :] = ko_ref[cls, :].astype(BF16)
        vc_ref[dst, :HEAD_DIM] = v_ref[cls, :].astype(BF16)

    n_cls_blk = cls_rows // BAND
    cls_bias = [_class_bias(gap, far_step) for gap in range(min(n_cls_blk, 3))]
    items = [(r, n) for r in range(d_mid) for n in range(n_cls_blk)]
    for g0 in range(0, len(items), ATTN_GROUP):
        group = items[g0:g0 + ATTN_GROUP]
        keys = [pl.ds(r * cls_rows, (n + 1) * BAND) for r, n in group]
        scores = [_dot_nt(qc_ref[pl.ds(r * cls_rows + n * BAND, BAND), :], kc_ref[kr, :])
                  + jnp.concatenate([cls_bias[min(n - nk, 2)] for nk in range(n + 1)], axis=1)
                  for (r, n), kr in zip(group, keys)]
        for (r, n), kr, (m, p) in zip(group, keys, _softmax_stage(scores)):
            rows = pl.ds(r + d_mid * BAND * n, BAND, stride=d_mid)
            pv = jnp.dot(p, vc_ref[kr, :], preferred_element_type=F32)
            acc_ref[rows, :] = pv[:, :HEAD_DIM]
            l_ref[rows, :] = pv[:, HEAD_DIM:]
            m_ref[rows, :] = m

    near_bias = [_near_bias(0), jnp.concatenate([_near_bias(1), _near_bias(0)], axis=1)]
    items = list(range(seq // BAND))
    for g0 in range(0, len(items), ATTN_GROUP):
        group = items[g0:g0 + ATTN_GROUP]
        rows = [pl.ds(n * BAND, BAND) for n in group]
        keys = [pl.ds(max(n - 1, 0) * BAND, BAND * min(n + 1, 2)) for n in group]
        scores = [_dot_nt(qb_ref[rw, :], kb_ref[kr, :]) + near_bias[min(n, 1)]
                  for n, rw, kr in zip(group, rows, keys)]
        probs = _softmax_stage(scores, floors=[m_ref[rw, :] for rw in rows])
        for rw, kr, (m, p) in zip(rows, keys, probs):
            w = jnp.exp2(m_ref[rw, :] - m)
            pv = jnp.dot(p, vb_ref[kr, :], preferred_element_type=F32)
            num = pv[:, :HEAD_DIM] + w * acc_ref[rw, :]
            den = pv[:, HEAD_DIM:] + w * l_ref[rw, :]
            z_ref[rw, :] = ((num / den) * _silu(g_ref[rw, :])).astype(z_ref.dtype)


def _prompt_attention(proj, cos, sin, bsz, seq):
    def col(g0):
        return pl.BlockSpec((None, seq, HEAD_DIM), lambda b, h: (g0 + h, b, 0))

    table = pl.BlockSpec((seq, HEAD_DIM), lambda b, h: (0, 0))
    kv_out = pl.BlockSpec((None, None, seq, HEAD_DIM), lambda b, h: (b, h, 0, 0))
    d_near, d_mid, d_far = DILATIONS
    far_step = d_far // d_mid
    assert d_near == 1 and d_far % d_mid == 0 and far_step & (far_step - 1) == 0
    assert all(w // d == BAND for w, d in zip(WINDOWS, DILATIONS)) and seq % (d_mid * BAND) == 0
    assert seq // d_mid <= far_step * BAND
    return pl.pallas_call(
        _attn_kernel,
        grid=(bsz, N_HEADS_ATTN),
        in_specs=[col(G_QA), col(G_KA), col(G_VA), col(G_GA), table, table],
        out_specs=[pl.BlockSpec((None, seq, HEAD_DIM), lambda b, h: (h, b, 0)), kv_out, kv_out],
        out_shape=[jax.ShapeDtypeStruct((N_HEADS_ATTN, bsz * seq, HEAD_DIM), BF16),
                   jax.ShapeDtypeStruct((bsz, N_HEADS_ATTN, seq, HEAD_DIM), F32),
                   jax.ShapeDtypeStruct((bsz, N_HEADS_ATTN, seq, HEAD_DIM), F32)],
        scratch_shapes=[pltpu.VMEM((seq, HEAD_DIM), F32)]
                       + [pltpu.VMEM((seq, HEAD_DIM), BF16), pltpu.VMEM((seq, HEAD_DIM), BF16),
                          pltpu.VMEM((seq, 2 * HEAD_DIM), BF16)] * 2
                       + [pltpu.VMEM((seq, HEAD_DIM), F32)] * 3,
        compiler_params=_params(("parallel", "parallel")),
        name="prompt_attn",
    )(proj, proj, proj, proj, cos, sin)


def _lower_bound(lb_raw, layer):
    e = jnp.exp(lb_raw - jnp.max(lb_raw, axis=0, keepdims=True))
    sm = e / jnp.sum(e, axis=0, keepdims=True)
    return jnp.sum(sm[:layer + 1], axis=0, keepdims=True)


def _split2(x):
    hi = x.astype(BF16)
    return hi, (x - hi.astype(F32)).astype(BF16)


def _rms_gate(o, norm_g, gate):
    o = o * lax.rsqrt(jnp.mean(o * o, axis=-1, keepdims=True) + RMS_EPS)
    return o * norm_g * _silu(gate)


def _hgrn_gates(fb, lb, tril):
    f = lb + (1.0 - lb) * _sigmoid(fb)
    g = jnp.log(f)
    b = sum(jnp.dot(tril, piece, preferred_element_type=F32) for piece in _split2(g))
    return 1.0 - f, b


def _hgrn_fast_prepare(q_ref, f_ref, lb, qh_ref, kh_ref, el_ref):
    n_heads, seq, _ = q_ref.shape
    blk = HGRN_PREP_ROWS
    tril = _tril_ones(blk)

    def body(bi, carry):
        r0 = pl.multiple_of(bi * blk, blk)
        fb = jnp.concatenate([f_ref[h, pl.ds(r0, blk), :] for h in range(n_heads)], axis=1)
        f = lb + (1.0 - lb) * _sigmoid(fb)
        b_blk = sum(jnp.dot(tril, piece, preferred_element_type=F32) for piece in _split2(jnp.log(f)))
        kk = 1.0 - f
        for c0 in range(0, blk, HGRN_CHUNK):
            rs = slice(c0, c0 + HGRN_CHUNK)
            b = b_blk[rs] - b_blk[c0 - 1:c0] if c0 else b_blk[rs]
            rows = pl.ds(r0 + c0, HGRN_CHUNK)
            q = jnp.concatenate([q_ref[h, rows, :] for h in range(n_heads)], axis=1)
            qh_ref[rows, :] = (q * jnp.exp(b)).astype(BF16)
            kh_ref[rows, :] = (kk[rs] * jnp.exp(-b)).astype(BF16)
            el_ref[pl.ds(bi * (blk // HGRN_CHUNK) + c0 // HGRN_CHUNK, 1), :] = jnp.exp(b[HGRN_CHUNK - 1:])
        return carry

    lax.fori_loop(0, seq // blk, body, 0, unroll=4)


def _hgrn_fast_scan(i_ref, g_ref, norm_g, qh_ref, kh_ref, el_ref, st_ref, z_ref):
    n_heads, seq, _ = i_ref.shape
    c = HGRN_CHUNK
    heads = range(n_heads)
    chunks = range(HGRN_SCAN_CHUNKS)
    lanes = [slice(h * HEAD_DIM, (h + 1) * HEAD_DIM) for h in heads]
    causal = lax.broadcasted_iota(jnp.int32, (c, c), 0) >= lax.broadcasted_iota(jnp.int32, (c, c), 1)
    tn = (((0,), (0,)), ((), ()))

    def body(ti, carry):
        rows = [pl.ds(pl.multiple_of((ti * HGRN_SCAN_CHUNKS + k) * c, c), c) for k in chunks]
        qh = [[qh_ref[rows[k], lanes[h]] for h in heads] for k in chunks]
        kh = [[kh_ref[rows[k], lanes[h]] for h in heads] for k in chunks]
        vb = [[i_ref[h, rows[k], :].astype(BF16) for h in heads] for k in chunks]
        att = [[jnp.where(causal, _dot_nt(qh[k][h], kh[k][h]), 0.0).astype(BF16) for h in heads] for k in chunks]
        ds = [[lax.dot_general(vb[k][h], kh[k][h], tn, preferred_element_type=F32) for h in heads] for k in chunks]
        st = [st_ref[h] for h in heads]
        o = []
        for k in chunks:
            el = el_ref[pl.ds(ti * HGRN_SCAN_CHUNKS + k, 1), :]
            o.append([_dot_nt(qh[k][h], st[h].astype(BF16)) for h in heads])
            st = [(st[h] + ds[k][h]) * el[:, lanes[h]] for h in heads]
        for h in heads:
            st_ref[h] = st[h]
        for k in chunks:
            for h in heads:
                o_kh = o[k][h] + jnp.dot(att[k][h], vb[k][h], preferred_element_type=F32)
                z_ref[h, rows[k], :] = _rms_gate(o_kh, norm_g[:, lanes[h]], g_ref[h, rows[k], :]).astype(z_ref.dtype)
        return carry

    lax.fori_loop(0, seq // (c * HGRN_SCAN_CHUNKS), body, 0)


def _hgrn_chunk(q, fb, v, lb, st, tril):
    c = q.shape[0]
    kk, b = _hgrn_gates(fb, lb, tril)
    o = _dot_nt((q * jnp.exp(b)).astype(BF16), st.astype(BF16))

    s_idx = lax.broadcasted_iota(jnp.int32, (c, 1), 0)
    lane = lax.broadcasted_iota(jnp.int32, (HGRN_SUB, c), 1)
    row = lax.broadcasted_iota(jnp.int32, (HGRN_SUB, c), 0)
    att_rows = []
    for i0 in range(0, c, HGRN_SUB):
        qi = q[i0:i0 + HGRN_SUB]
        bi = b[i0:i0 + HGRN_SUB]
        if i0 > 0:
            bref = b[i0 - 1:i0]
            kt = jnp.where(s_idx < i0, kk * jnp.exp(jnp.minimum(bref - b, 0.0)), 0.0)
            att = _dot_nt((qi * jnp.exp(bi - bref)).astype(BF16), kt.astype(BF16))
        else:
            att = jnp.zeros((HGRN_SUB, c), F32)
        for j in range(HGRN_SUB):
            s = i0 + j
            e = jnp.exp(jnp.minimum(bi - b[s:s + 1], 0.0))
            colv = jnp.sum(qi * kk[s:s + 1] * e, axis=-1, keepdims=True)
            att = jnp.where((lane == s) & (row >= j), colv, att)
        att_rows.append(att)
    att = jnp.concatenate(att_rows, axis=0)
    vb = v.astype(BF16)
    o = o + jnp.dot(att.astype(BF16), vb, preferred_element_type=F32)

    b_last = b[c - 1:c]
    kd = (kk * jnp.exp(b_last - b)).astype(BF16)
    st_new = st * jnp.exp(b_last) + lax.dot_general(vb, kd, (((0,), (0,)), ((), ())),
                                                    preferred_element_type=F32)
    return o, st_new


def _tril_ones(c):
    return (lax.broadcasted_iota(jnp.int32, (c, c), 0) >= lax.broadcasted_iota(jnp.int32, (c, c), 1)
            ).astype(BF16)


def _hgrn_kernel(layer, q_ref, f_ref, i_ref, g_ref, lb_ref, ng_ref, z_ref, s_ref, st_ref, qh_ref, kh_ref, el_ref):
    n_heads, seq, _ = q_ref.shape
    lb = _lower_bound(lb_ref[...], layer)
    norm_g = ng_ref[...]
    st_ref[...] = jnp.zeros_like(st_ref)

    def fast():
        _hgrn_fast_prepare(q_ref, f_ref, lb, qh_ref, kh_ref, el_ref)
        _hgrn_fast_scan(i_ref, g_ref, norm_g, qh_ref, kh_ref, el_ref, st_ref, z_ref)

    def safe():
        tril = _tril_ones(HGRN_CHUNK)

        def body(ci, carry):
            rows = pl.ds(pl.multiple_of(ci * HGRN_CHUNK, HGRN_CHUNK), HGRN_CHUNK)
            for h in range(n_heads):
                lanes = slice(h * HEAD_DIM, (h + 1) * HEAD_DIM)
                o, st = _hgrn_chunk(q_ref[h, rows, :], f_ref[h, rows, :], i_ref[h, rows, :], lb[:, lanes],
                                    st_ref[h], tril)
                st_ref[h] = st
                z_ref[h, rows, :] = _rms_gate(o, norm_g[:, lanes], g_ref[h, rows, :]).astype(z_ref.dtype)
            return carry

        lax.fori_loop(0, seq // HGRN_CHUNK, body, 0)

    fast_ok = HGRN_CHUNK * -jnp.log(jnp.min(lb)) <= HGRN_SAFE_LOG_RANGE
    lax.cond(fast_ok, fast, safe)
    for h in range(n_heads):
        s_ref[h] = st_ref[h].T


def _prompt_hgrn(proj, lb_raw, norm_g, layer, bsz, seq):
    hps = HGRN_HEADS_PER_STEP
    assert N_HEADS_HGRN % hps == 0 and all(g % hps == 0 for g in (G_QB, G_FB, G_IB, G_GB))
    assert seq % (HGRN_CHUNK * HGRN_SCAN_CHUNKS) == 0 and seq % HGRN_PREP_ROWS == 0 and HGRN_PREP_ROWS % HGRN_CHUNK == 0

    def cols(g0):
        return pl.BlockSpec((hps, seq, HEAD_DIM), lambda b, h: (g0 // hps + h, b, 0))

    return pl.pallas_call(
        functools.partial(_hgrn_kernel, layer),
        grid=(bsz, N_HEADS_HGRN // hps),
        in_specs=[cols(G_QB), cols(G_FB), cols(G_IB), cols(G_GB),
                  pl.BlockSpec((DEPTH + 1, hps * HEAD_DIM), lambda b, h: (0, h)),
                  pl.BlockSpec((1, hps * HEAD_DIM), lambda b, h: (0, h))],
        out_specs=[pl.BlockSpec((hps, seq, HEAD_DIM), lambda b, h: (h, b, 0)),
                   pl.BlockSpec((None, hps, HEAD_DIM, HEAD_DIM), lambda b, h: (b, h, 0, 0))],
        out_shape=[jax.ShapeDtypeStruct((N_HEADS_HGRN, bsz * seq, HEAD_DIM), BF16),
                   jax.ShapeDtypeStruct((bsz, N_HEADS_HGRN, HEAD_DIM, HEAD_DIM), F32)],
        scratch_shapes=[pltpu.VMEM((hps, HEAD_DIM, HEAD_DIM), F32),
                        pltpu.VMEM((seq, hps * HEAD_DIM), BF16),
                        pltpu.VMEM((seq, hps * HEAD_DIM), BF16),
                        pltpu.VMEM((seq // HGRN_CHUNK, hps * HEAD_DIM), F32)],
        compiler_params=_params(("parallel", "parallel")),
        name="prompt_hgrn",
    )(proj, proj, proj, proj, lb_raw, norm_g)


MEM_ROWS = 256
MEM_GROUP = 8


def _mem_kernel(q_ref, g_ref, mem_ref, w_ref, z_ref, mk_ref, mv_ref):
    seq = q_ref.shape[1]
    kv = jnp.dot(mem_ref[...].astype(BF16), w_ref[...].astype(BF16), preferred_element_type=F32)
    ones = jnp.ones((N_MEM, HEAD_DIM), BF16)
    mk_b, mv_b = [], []
    for h in range(N_HEADS_MEM):
        mk = kv[:, h * HEAD_DIM:(h + 1) * HEAD_DIM]
        mv = kv[:, W_MEM + h * HEAD_DIM:W_MEM + (h + 1) * HEAD_DIM]
        mk_ref[:, h, :] = mk
        mv_ref[:, h, :] = mv
        mk_b.append(mk.astype(BF16))
        mv_b.append(jnp.concatenate([mv.astype(BF16), ones], axis=1))

    items = [(h, pl.ds(n * MEM_ROWS, MEM_ROWS)) for h in range(N_HEADS_MEM) for n in range(seq // MEM_ROWS)]
    for g0 in range(0, len(items), MEM_GROUP):
        group = items[g0:g0 + MEM_GROUP]
        scores = [_dot_nt((q_ref[h, rows, :] * ATTN_SCALE_LOG2).astype(BF16), mk_b[h]) for h, rows in group]
        probs = [jnp.exp2(s - jnp.max(s, axis=-1, keepdims=True)).astype(BF16) for s in scores]
        for (h, rows), p in zip(group, probs):
            pv = jnp.dot(p, mv_b[h], preferred_element_type=F32)
            o = pv[:, :HEAD_DIM] / pv[:, HEAD_DIM:]
            z_ref[h, rows, :] = (o * _silu(g_ref[h, rows, :])).astype(z_ref.dtype)


def _prompt_mem(proj, mem, w_kv, bsz, seq):
    assert G_QM % N_HEADS_MEM == 0 and G_GM % N_HEADS_MEM == 0

    def cols(g0):
        return pl.BlockSpec((N_HEADS_MEM, seq, HEAD_DIM), lambda b: (g0 // N_HEADS_MEM, b, 0))

    kv_out = pl.BlockSpec((None, N_MEM, N_HEADS_MEM, HEAD_DIM), lambda b: (b, 0, 0, 0))
    kv_shape = jax.ShapeDtypeStruct((bsz, N_MEM, N_HEADS_MEM, HEAD_DIM), F32)
    return pl.pallas_call(
        _mem_kernel,
        grid=(bsz,),
        in_specs=[cols(G_QM), cols(G_GM), pl.BlockSpec((N_MEM, D_MODEL), lambda b: (b, 0)),
                  pl.BlockSpec((D_MODEL, 2 * W_MEM), lambda b: (0, 0))],
        out_specs=[pl.BlockSpec((N_HEADS_MEM, seq, HEAD_DIM), lambda b: (0, b, 0)), kv_out, kv_out],
        out_shape=[jax.ShapeDtypeStruct((N_HEADS_MEM, bsz * seq, HEAD_DIM), BF16), kv_shape, kv_shape],
        compiler_params=_params(("parallel",)),
        name="prompt_mem",
    )(proj, proj, mem, w_kv)


def _merge_kernel(za_ref, zh_ref, zm_ref, x_ref, w_ref, lg_ref, lb_ref, o_ref, z_ref):
    c0 = 0
    for ref in (za_ref, zh_ref, zm_ref):
        for c in range(ref.shape[0]):
            z_ref[:, (c0 + c) * LANES:(c0 + c + 1) * LANES] = ref[c].astype(BF16)
        c0 += ref.shape[0]
    y = jnp.dot(z_ref[...], w_ref[...], preferred_element_type=F32)
    r = DEEPNORM_ALPHA * x_ref[...] + y
    mu = jnp.mean(r, axis=-1, keepdims=True)
    d = r - mu
    var = jnp.mean(d * d, axis=-1, keepdims=True)
    o_ref[...] = d * lax.rsqrt(var + LN_EPS) * lg_ref[...] + lb_ref[...]


def _merge(za, zh, zm, x, w_out_bf16, ln_g, ln_b, tm):
    m = x.shape[0]

    def slab(a):
        return pl.BlockSpec((a.shape[0], tm, LANES), lambda i: (0, i, 0))

    const = lambda shape: pl.BlockSpec(shape, lambda i: (0, 0))
    return pl.pallas_call(
        _merge_kernel,
        grid=(m // tm,),
        in_specs=[slab(za), slab(zh), slab(zm), pl.BlockSpec((tm, D_MODEL), lambda i: (i, 0)),
                  const((MIX_WIDTH, D_MODEL)), const((1, D_MODEL)), const((1, D_MODEL))],
        out_specs=pl.BlockSpec((tm, D_MODEL), lambda i: (i, 0)),
        out_shape=jax.ShapeDtypeStruct((m, D_MODEL), F32),
        scratch_shapes=[pltpu.VMEM((tm, MIX_WIDTH), BF16)],
        compiler_params=_params(("parallel",)),
        name="merge",
    )(za, zh, zm, x, w_out_bf16, ln_g, ln_b)


def _column(row):
    return jnp.broadcast_to(row, (HEAD_DIM, HEAD_DIM)).T


HBM_TILE_ROWS = 8
SAMPLE_ROWS = 2


def _window_pieces(past):
    pieces = []
    for dil in DILATIONS:
        if dil == 1:
            pieces.append((0, past - BAND, BAND, None))
        elif dil < HBM_TILE_ROWS:
            n = BAND * dil // HBM_TILE_ROWS
            pieces += [(1, past // HBM_TILE_ROWS - n, n, r) for r in range(0, HBM_TILE_ROWS, dil)]
        else:
            pieces.append((2, past // dil - BAND, BAND, 0))
    return pieces


def _cache_views(cache):
    depth, bsz, past, nh, hd = cache.shape
    assert all(d == 1 or HBM_TILE_ROWS % d == 0 or d % HBM_TILE_ROWS == 0 for d in DILATIONS)
    big = max(DILATIONS)
    hm = cache.transpose(0, 1, 3, 2, 4)
    return (hm, hm.reshape(depth, bsz, nh, past // HBM_TILE_ROWS, HBM_TILE_ROWS, hd),
            hm.reshape(depth, bsz, nh, past // big, big, hd))


def _sample_row(layer, p_ref, cos, sin, k_win, v_win, st_ref, mk_ref, mv_ref, lb_all, ng_ref, z_ref, ko_ref, vo_ref,
                so_ref):
    q_all = _rope(p_ref[G_QA:G_QA + N_HEADS_ATTN, :], cos, sin) * ATTN_SCALE
    k_all = _rope(p_ref[G_KA:G_KA + N_HEADS_ATTN, :], cos, sin)
    v_all = p_ref[G_VA:G_VA + N_HEADS_ATTN, :]
    ko_ref[...] = k_all
    vo_ref[...] = v_all

    for h in range(N_HEADS_ATTN):
        q = q_all[h:h + 1]
        s_new = jnp.sum(q * k_all[h:h + 1], axis=-1, keepdims=True)
        s = jnp.sum(k_win[h] * q, axis=-1, keepdims=True)
        m = jnp.maximum(jnp.max(s, axis=0, keepdims=True), s_new)
        p = jnp.exp(s - m)
        p_new = jnp.exp(s_new - m) * len(DILATIONS)
        den = jnp.sum(p, axis=0, keepdims=True) + p_new
        num = jnp.sum(p * v_win[h], axis=0, keepdims=True) + p_new * v_all[h:h + 1]
        z_ref[h:h + 1, :] = (num / den) * _silu(p_ref[G_GA + h:G_GA + h + 1, :])

    for h in range(N_HEADS_HGRN):
        lanes = slice(h * HEAD_DIM, (h + 1) * HEAD_DIM)
        lb = _lower_bound(lb_all[:, lanes], layer)
        f = lb + (1.0 - lb) * _sigmoid(p_ref[G_FB + h:G_FB + h + 1, :])
        f_col = _column(f)
        q_col = _column(p_ref[G_QB + h:G_QB + h + 1, :])
        s_new = f_col * st_ref[h] + (1.0 - f_col) * p_ref[G_IB + h:G_IB + h + 1, :]
        so_ref[h] = s_new
        o = jnp.sum(s_new * q_col, axis=0, keepdims=True)
        z_ref[N_HEADS_ATTN + h:N_HEADS_ATTN + h + 1, :] = _rms_gate(
            o, ng_ref[:, lanes], p_ref[G_GB + h:G_GB + h + 1, :])

    fold = mk_ref.shape[1] // N_HEADS_MEM
    unfold = lambda a: sum(a[i * N_HEADS_MEM:(i + 1) * N_HEADS_MEM] for i in range(fold))
    q_m = p_ref[G_QM:G_QM + N_HEADS_MEM, :] * ATTN_SCALE
    s = jnp.sum(mk_ref[...] * jnp.concatenate([q_m] * fold, axis=0)[None], axis=-1, keepdims=True)
    m = jnp.max(s, axis=0)
    m = functools.reduce(jnp.maximum, [m[i * N_HEADS_MEM:(i + 1) * N_HEADS_MEM] for i in range(fold)])
    p = jnp.exp(s - jnp.concatenate([m] * fold, axis=0)[None])
    o = unfold(jnp.sum(p * mv_ref[...], axis=0)) / unfold(jnp.sum(p, axis=0))
    row = N_HEADS_ATTN + N_HEADS_HGRN
    z_ref[row:row + N_HEADS_MEM, :] = o * _silu(p_ref[G_GM:G_GM + N_HEADS_MEM, :])


def _sample_kernel(layer, past, p_ref, cos_ref, sin_ref, k0_hbm, k1_hbm, k2_hbm, v0_hbm, v1_hbm, v2_hbm,
                   st_ref, mk_ref, mv_ref, lb_ref, ng_ref, z_ref, ko_ref, vo_ref, so_ref, kbuf, vbuf, sem):
    step = pl.program_id(0)
    slot = step % 2
    n_rows = p_ref.shape[0]
    pieces = _window_pieces(past)

    def window_copies(stp, sl):
        out = []
        for rr in range(n_rows):
            for ci, (views, buf) in enumerate((((k0_hbm, k1_hbm, k2_hbm), kbuf), ((v0_hbm, v1_hbm, v2_hbm), vbuf))):
                off = 0
                for pi, (vi, start, count, res) in enumerate(pieces):
                    view, row = views[vi], stp * n_rows + rr
                    src = (view.at[layer, row, :, pl.ds(start, count), :] if res is None
                           else view.at[layer, row, :, pl.ds(start, count), res, :])
                    out.append(pltpu.make_async_copy(src, buf.at[sl, rr, :, pl.ds(off, count), :],
                                                     sem.at[sl, rr, ci, pi]))
                    off += count
        return out

    @pl.when(step == 0)
    def _():
        for c in window_copies(step, slot):
            c.start()

    @pl.when(step + 1 < pl.num_programs(0))
    def _():
        for c in window_copies(step + 1, 1 - slot):
            c.start()

    for c in window_copies(step, slot):
        c.wait()

    cos, sin, lb_all = cos_ref[...], sin_ref[...], lb_ref[...]
    for rr in range(n_rows):
        _sample_row(layer, p_ref.at[rr], cos, sin, kbuf.at[slot, rr], vbuf.at[slot, rr], st_ref.at[rr], mk_ref.at[rr],
                    mv_ref.at[rr], lb_all, ng_ref, z_ref.at[rr], ko_ref.at[rr], vo_ref.at[rr], so_ref.at[rr])


def _sample_mixers(proj_rows, cos, sin, win_k, win_v, state, mem_k, mem_v, lb_raw, norm_g, layer):
    bsz, past = win_k.shape[1:3]
    rps = SAMPLE_ROWS
    assert all(past % d == 0 and past >= w for w, d in zip(WINDOWS, DILATIONS)) and bsz % rps == 0
    assert all(w // d == BAND for w, d in zip(WINDOWS, DILATIONS)) and past % HBM_TILE_ROWS == 0
    n_keys = BAND * len(DILATIONS)
    n_pieces = len(_window_pieces(past))
    any_spec = pl.BlockSpec(memory_space=pl.ANY)
    vec = lambda n: pl.BlockSpec((n, W_HGRN), lambda b: (0, 0))
    heads = lambda n: pl.BlockSpec((rps, n, HEAD_DIM), lambda b: (b, 0, 0))
    fold = HBM_TILE_ROWS // N_HEADS_MEM
    mem_k, mem_v = (a.reshape(a.shape[0], bsz, N_MEM // fold, fold * N_HEADS_MEM, HEAD_DIM) for a in (mem_k, mem_v))
    mem_spec = pl.BlockSpec((None, rps, N_MEM // fold, fold * N_HEADS_MEM, HEAD_DIM), lambda b: (layer, b, 0, 0, 0))
    return pl.pallas_call(
        functools.partial(_sample_kernel, layer, past),
        grid=(bsz // rps,),
        in_specs=[heads(N_GROUPS), pl.BlockSpec((1, HEAD_DIM), lambda b: (0, 0)),
                  pl.BlockSpec((1, HEAD_DIM), lambda b: (0, 0)), *([any_spec] * 6),
                  pl.BlockSpec((None, rps, N_HEADS_HGRN, HEAD_DIM, HEAD_DIM), lambda b: (layer, b, 0, 0, 0)),
                  mem_spec, mem_spec, vec(DEPTH + 1), vec(1)],
        out_specs=[heads(MIX_WIDTH // HEAD_DIM), heads(N_HEADS_ATTN), heads(N_HEADS_ATTN),
                   pl.BlockSpec((rps, N_HEADS_HGRN, HEAD_DIM, HEAD_DIM), lambda b: (b, 0, 0, 0))],
        out_shape=[jax.ShapeDtypeStruct((bsz, MIX_WIDTH // HEAD_DIM, HEAD_DIM), F32),
                   jax.ShapeDtypeStruct((bsz, N_HEADS_ATTN, HEAD_DIM), F32),
                   jax.ShapeDtypeStruct((bsz, N_HEADS_ATTN, HEAD_DIM), F32),
                   jax.ShapeDtypeStruct((bsz, N_HEADS_HGRN, HEAD_DIM, HEAD_DIM), F32)],
        scratch_shapes=[pltpu.VMEM((2, rps, N_HEADS_ATTN, n_keys, HEAD_DIM), F32),
                        pltpu.VMEM((2, rps, N_HEADS_ATTN, n_keys, HEAD_DIM), F32),
                        pltpu.SemaphoreType.DMA((2, rps, 2, n_pieces))],
        compiler_params=_params(("arbitrary",)),
        name="sample_mixers",
    )(proj_rows, cos, sin, *_cache_views(win_k), *_cache_views(win_v), state, mem_k, mem_v, lb_raw, norm_g)


def _rope_tables(pos):
    half = HEAD_DIM // 2
    inv_freq = 1.0 / (ROPE_THETA ** (np.arange(half, dtype=np.float64) / half))
    ang = np.asarray(pos, np.float64)[:, None] * inv_freq[None, :]
    cos, sin = np.cos(ang), np.sin(ang)
    return (jnp.asarray(np.concatenate([cos, cos], axis=-1), F32),
            jnp.asarray(np.concatenate([-sin, sin], axis=-1), F32))


def kernel(x_prompt, x_sample, cache_win_k, cache_win_v, state_hgrn, cache_mem_k, cache_mem_v, mem_prompt,
           w_in, w_mem_kv, hgrn_lb_raw, hgrn_norm_g, w_out, ln_g, ln_b):
    bsz, seq, _ = x_prompt.shape
    dbsz, n_new, _ = x_sample.shape
    assert n_new == 1 and seq % (BAND * max(DILATIONS)) == 0
    cos_p, sin_p = _rope_tables(np.arange(seq))
    cos_s, sin_s = _rope_tables(PAST_LEN + np.arange(n_new))

    hp = x_prompt.reshape(bsz * seq, D_MODEL)
    hs = x_sample.reshape(dbsz * n_new, D_MODEL)
    mem = mem_prompt.reshape(bsz * N_MEM, D_MODEL)
    outs = [[] for _ in range(8)]
    for layer in range(DEPTH):
        w_out_b = w_out[layer].astype(BF16)
        lb_raw = hgrn_lb_raw
        norm_g = hgrn_norm_g[layer][None]
        lg, lbias = ln_g[layer][None], ln_b[layer][None]

        proj, proj_s = _project(hp, hs, w_in[layer], 1024, 1024, 1024)
        za, k1, v1 = _prompt_attention(proj, cos_p, sin_p, bsz, seq)
        zh, s1 = _prompt_hgrn(proj, lb_raw, norm_g, layer, bsz, seq)
        zm, mk1, mv1 = _prompt_mem(proj, mem, w_mem_kv[layer], bsz, seq)
        hp = _merge(za, zh, zm, hp, w_out_b, lg, lbias, 512)

        proj_s = proj_s.transpose(1, 0, 2)
        zs, k2, v2, s2 = _sample_mixers(proj_s, cos_s, sin_s, cache_win_k, cache_win_v, state_hgrn,
                                        cache_mem_k, cache_mem_v, lb_raw, norm_g, layer)
        zs = zs.transpose(1, 0, 2)
        hs = _merge(zs[:N_HEADS_ATTN], zs[N_HEADS_ATTN:N_HEADS_ATTN + N_HEADS_HGRN],
                    zs[N_HEADS_ATTN + N_HEADS_HGRN:], hs, w_out_b, lg, lbias, dbsz)

        new = (k1.transpose(0, 2, 1, 3), v1.transpose(0, 2, 1, 3), s1, mk1, mv1,
               k2.reshape(dbsz, n_new, N_HEADS_ATTN, HEAD_DIM), v2.reshape(dbsz, n_new, N_HEADS_ATTN, HEAD_DIM),
               s2.astype(state_hgrn.dtype))
        for acc, val in zip(outs, new):
            acc.append(val)

    return (hp.reshape(bsz, seq, D_MODEL), hs.reshape(dbsz, n_new, D_MODEL), *[jnp.stack(o) for o in outs])
```

```python
import functools

import jax
import jax.numpy as jnp
import numpy as np
from jax import lax
from jax.experimental import pallas as pl
from jax.experimental.pallas import tpu as pltpu

F32 = jnp.float32
BF16 = jnp.bfloat16

D_MODEL = 2048
DEPTH = 1
PAST_LEN = 8192
HEAD_DIM = 128
N_HEADS_ATTN = 6
N_HEADS_HGRN = 6
N_HEADS_MEM = 4
W_ATTN = N_HEADS_ATTN * HEAD_DIM
W_HGRN = N_HEADS_HGRN * HEAD_DIM
W_MEM = N_HEADS_MEM * HEAD_DIM
MIX_WIDTH = W_ATTN + W_HGRN + W_MEM
WINDOWS = (128, 512, 2048)
DILATIONS = (1, 4, 16)
N_MEM = 256
ROPE_THETA = 10000.0
LN_EPS = 1e-5
RMS_EPS = 1e-6
NEG_INF = -1e30
DEEPNORM_ALPHA = (2 * DEPTH) ** 0.25
ATTN_SCALE = HEAD_DIM ** -0.5
ATTN_SCALE_LOG2 = ATTN_SCALE * 1.4426950408889634

G_QA, G_KA, G_VA, G_GA = 0, 6, 12, 18
G_QB, G_FB, G_IB, G_GB = 24, 30, 36, 42
G_QM, G_GM = 48, 52
N_GROUPS = 56

LANES = 128
BAND = 128
ATTN_GROUP = 16
HGRN_CHUNK = 64
HGRN_SUB = 16
HGRN_HEADS_PER_STEP = 3
HGRN_PREP_ROWS = 256
HGRN_SCAN_CHUNKS = 16
HGRN_SAFE_LOG_RANGE = 70.0
VMEM_LIMIT = 48 * 1024 * 1024
PROJ_ROWS, PROJ_COLS, PROJ_FIRST_COLS = 1024, 1024, 512
MERGE_ROWS = 512
MERGE_CAST_ROWS = 256


def _params(sem, vmem=VMEM_LIMIT):
    return pltpu.CompilerParams(dimension_semantics=sem, vmem_limit_bytes=vmem)


def _sigmoid(x):
    return 1.0 / (1.0 + jnp.exp(-x))


def _silu(x):
    return x * _sigmoid(x)


def _dot_nt(a, b):
    return lax.dot_general(a, b, (((1,), (1,)), ((), ())), preferred_element_type=F32)


def _store_slabs(ref, acc):
    for c in range(ref.shape[0]):
        ref[c] = acc[:, c * LANES:(c + 1) * LANES]


def _proj_first_kernel(x_ref, xs_ref, w_ref, o_ref, os_ref, wb_ref, xb_ref):
    tm = x_ref.shape[0]

    @pl.when(pl.program_id(0) == 0)
    def _():
        xb_ref[:tm, :] = x_ref[...].astype(BF16)
        xb_ref[tm:, :] = xs_ref[...].astype(BF16)

    wb_ref[...] = w_ref[...].astype(BF16)
    acc = jnp.dot(xb_ref[...], wb_ref[...], preferred_element_type=F32)
    _store_slabs(o_ref, acc[:tm])
    _store_slabs(os_ref, acc[tm:])


def _proj_rest_kernel(x_ref, w_ref, first_ref, o_hbm, xb_ref, obuf, sem_out, sem_first):
    i, j = pl.program_id(0), pl.program_id(1)
    n_col = pl.num_programs(1)
    step = i * n_col + j
    last = pl.num_programs(0) * n_col - 1
    slot = step % 2
    tm = x_ref.shape[0]
    n_slab = obuf.shape[1]

    def tile_copy(stp):
        row_tile, col_tile = stp // n_col + 1, stp % n_col
        return pltpu.make_async_copy(
            obuf.at[stp % 2], o_hbm.at[pl.ds(col_tile * n_slab, n_slab), pl.ds(row_tile * tm, tm), :],
            sem_out.at[stp % 2])

    first_copy = pltpu.make_async_copy(first_ref, o_hbm.at[pl.ds(j * n_slab, n_slab), pl.ds(0, tm), :], sem_first.at[0])

    @pl.when(i == 0)
    def _():
        first_copy.start()

    @pl.when(j == 0)
    def _():
        xb_ref[...] = x_ref[...].astype(BF16)

    @pl.when(step >= 2)
    def _():
        tile_copy(step - 2).wait()

    _store_slabs(obuf.at[slot], jnp.dot(xb_ref[...], w_ref[...], preferred_element_type=F32))
    tile_copy(step).start()

    @pl.when(i == 0)
    def _():
        first_copy.wait()

    @pl.when(step == last)
    def _():
        tile_copy(step - 1).wait()
        tile_copy(step).wait()


def _project(x, xs, w, tm, tn_first, tn):
    m, k = x.shape
    ms = xs.shape[0]
    n = w.shape[1]
    assert (m // tm - 1) * (n // tn) >= 2
    proj_first, proj_s, w_bf16 = pl.pallas_call(
        _proj_first_kernel,
        grid=(n // tn_first,),
        in_specs=[pl.BlockSpec((tm, k), lambda j: (0, 0)),
                  pl.BlockSpec((ms, k), lambda j: (0, 0)),
                  pl.BlockSpec((k, tn_first), lambda j: (0, j))],
        out_specs=[pl.BlockSpec((tn_first // LANES, tm, LANES), lambda j: (j, 0, 0)),
                   pl.BlockSpec((tn_first // LANES, ms, LANES), lambda j: (j, 0, 0)),
                   pl.BlockSpec((k, tn_first), lambda j: (0, j))],
        out_shape=[jax.ShapeDtypeStruct((n // LANES, tm, LANES), F32),
                   jax.ShapeDtypeStruct((n // LANES, ms, LANES), F32),
                   jax.ShapeDtypeStruct((k, n), BF16)],
        scratch_shapes=[pltpu.VMEM((tm + ms, k), BF16)],
        compiler_params=_params(("arbitrary",)),
        name="in_proj_first",
    )(x, xs, w)
    proj = pl.pallas_call(
        _proj_rest_kernel,
        grid=(m // tm - 1, n // tn),
        in_specs=[pl.BlockSpec((tm, k), lambda i, j: (i + 1, 0)),
                  pl.BlockSpec((k, tn), lambda i, j: (0, j)),
                  pl.BlockSpec((tn // LANES, tm, LANES), lambda i, j: (jnp.where(i == 0, j, n // tn - 1), 0, 0))],
        out_specs=pl.BlockSpec(memory_space=pl.ANY),
        out_shape=jax.ShapeDtypeStruct((n // LANES, m, LANES), F32),
        scratch_shapes=[pltpu.VMEM((tm, k), BF16), pltpu.VMEM((2, tn // LANES, tm, LANES), F32),
                        pltpu.SemaphoreType.DMA((2,)), pltpu.SemaphoreType.DMA((1,))],
        compiler_params=_params(("arbitrary", "arbitrary")),
        name="in_proj",
    )(x, w_bf16, proj_first)
    return proj, proj_s


def _rope(x, cos, sin_signed):
    return x * cos + pltpu.roll(x, HEAD_DIM // 2, 1) * sin_signed


def _bias_from_count(count):
    return jnp.where(count > 1.5, 1.0, jnp.where(count > 0.5, 0.0, NEG_INF)).astype(F32)


def _block_deltas(block_gap):
    qi = lax.broadcasted_iota(jnp.int32, (BAND, BAND), 0)
    ki = lax.broadcasted_iota(jnp.int32, (BAND, BAND), 1)
    return block_gap * BAND + qi - ki


def _near_bias(block_gap):
    d = _block_deltas(block_gap)
    return _bias_from_count(((d >= 0) & (d <= BAND)).astype(F32))


def _class_bias(block_gap, far_step):
    d = _block_deltas(block_gap)
    near = (d >= 0) & (d <= BAND)
    far = (d >= 0) & ((d & (far_step - 1)) == 0)
    return _bias_from_count(near.astype(F32) + far.astype(F32))


def _softmax_stage(scores, floors=None):
    out = []
    for idx, s in enumerate(scores):
        m = jnp.broadcast_to(jnp.max(s, axis=-1, keepdims=True), (BAND, HEAD_DIM))
        if floors is not None:
            m = jnp.maximum(m, floors[idx])
        m_wide = jnp.concatenate([m] * (s.shape[1] // HEAD_DIM), axis=1)
        out.append((m, jnp.exp2(s - m_wide).astype(BF16)))
    return out


def _attn_kernel(q_ref, k_ref, v_ref, g_ref, cos_ref, sin_ref, z_ref, ko_ref, vo_ref,
                 qs_ref, qb_ref, kb_ref, vb_ref, qc_ref, kc_ref, vc_ref, acc_ref, m_ref, l_ref):
    seq = q_ref.shape[0]
    d_mid, d_far = DILATIONS[1], DILATIONS[2]
    far_step = d_far // d_mid
    cls_rows = seq // d_mid
    cos = cos_ref[...]
    sin = sin_ref[...]
    qs_ref[...] = _rope(q_ref[...], cos, sin) * ATTN_SCALE_LOG2
    ko_ref[...] = _rope(k_ref[...], cos, sin)
    vo_ref[...] = v_ref[...]
    qb_ref[...] = qs_ref[...].astype(BF16)
    kb_ref[...] = ko_ref[...].astype(BF16)
    ones = jnp.ones((seq, HEAD_DIM), BF16)
    vb_ref[:, :HEAD_DIM] = v_ref[...].astype(BF16)
    vb_ref[:, HEAD_DIM:] = ones
    vc_ref[:, HEAD_DIM:] = ones
    for r in range(d_mid):
        cls, dst = pl.ds(r, cls_rows, stride=d_mid), pl.ds(r * cls_rows, cls_rows)
        qc_ref[dst, :] = qs_ref[cls, :].astype(BF16)
        kc_ref[dst, :] = ko_ref[cls, :].astype(BF16)
        vc_ref[dst, :HEAD_DIM] = v_ref[cls, :].astype(BF16)

    n_cls_blk = cls_rows // BAND
    cls_bias = [_class_bias(gap, far_step) for gap in range(min(n_cls_blk, 3))]
    items = [(r, n) for r in range(d_mid) for n in range(n_cls_blk)]
    for g0 in range(0, len(items), ATTN_GROUP):
        group = items[g0:g0 + ATTN_GROUP]
        keys = [pl.ds(r * cls_rows, (n + 1) * BAND) for r, n in group]
        scores = [_dot_nt(qc_ref[pl.ds(r * cls_rows + n * BAND, BAND), :], kc_ref[kr, :])
                  + jnp.concatenate([cls_bias[min(n - nk, 2)] for nk in range(n + 1)], axis=1)
                  for (r, n), kr in zip(group, keys)]
        for (r, n), kr, (m, p) in zip(group, keys, _softmax_stage(scores)):
            rows = pl.ds(r + d_mid * BAND * n, BAND, stride=d_mid)
            pv = jnp.dot(p, vc_ref[kr, :], preferred_element_type=F32)
            acc_ref[rows, :] = pv[:, :HEAD_DIM]
            l_ref[rows, :] = pv[:, HEAD_DIM:]
            m_ref[rows, :] = m

    near_bias = [_near_bias(0), jnp.concatenate([_near_bias(1), _near_bias(0)], axis=1)]
    items = list(range(seq // BAND))
    for g0 in range(0, len(items), ATTN_GROUP):
        group = items[g0:g0 + ATTN_GROUP]
        rows = [pl.ds(n * BAND, BAND) for n in group]
        keys = [pl.ds(max(n - 1, 0) * BAND, BAND * min(n + 1, 2)) for n in group]
        scores = [_dot_nt(qb_ref[rw, :], kb_ref[kr, :]) + near_bias[min(n, 1)]
                  for n, rw, kr in zip(group, rows, keys)]
        probs = _softmax_stage(scores, floors=[m_ref[rw, :] for rw in rows])
        for rw, kr, (m, p) in zip(rows, keys, probs):
            w = jnp.exp2(m_ref[rw, :] - m)
            pv = jnp.dot(p, vb_ref[kr, :], preferred_element_type=F32)
            num = pv[:, :HEAD_DIM] + w * acc_ref[rw, :]
            den = pv[:, HEAD_DIM:] + w * l_ref[rw, :]
            z_ref[rw, :] = ((num / den) * _silu(g_ref[rw, :])).astype(z_ref.dtype)


def _prompt_attention(proj, cos, sin, bsz, seq):
    def col(g0):
        return pl.BlockSpec((None, seq, HEAD_DIM), lambda b, h: (g0 + h, b, 0))

    table = pl.BlockSpec((seq, HEAD_DIM), lambda b, h: (0, 0))
    kv_out = pl.BlockSpec((None, None, seq, HEAD_DIM), lambda b, h: (b, h, 0, 0))
    d_near, d_mid, d_far = DILATIONS
    far_step = d_far // d_mid
    assert d_near == 1 and d_far % d_mid == 0 and far_step & (far_step - 1) == 0
    assert all(w // d == BAND for w, d in zip(WINDOWS, DILATIONS)) and seq % (d_mid * BAND) == 0
    assert seq // d_mid <= far_step * BAND
    return pl.pallas_call(
        _attn_kernel,
        grid=(bsz, N_HEADS_ATTN),
        in_specs=[col(G_QA), col(G_KA), col(G_VA), col(G_GA), table, table],
        out_specs=[pl.BlockSpec((None, seq, HEAD_DIM), lambda b, h: (h, b, 0)), kv_out, kv_out],
        out_shape=[jax.ShapeDtypeStruct((N_HEADS_ATTN, bsz * seq, HEAD_DIM), BF16),
                   jax.ShapeDtypeStruct((bsz, N_HEADS_ATTN, seq, HEAD_DIM), F32),
                   jax.ShapeDtypeStruct((bsz, N_HEADS_ATTN, seq, HEAD_DIM), F32)],
        scratch_shapes=[pltpu.VMEM((seq, HEAD_DIM), F32)]
                       + [pltpu.VMEM((seq, HEAD_DIM), BF16), pltpu.VMEM((seq, HEAD_DIM), BF16),
                          pltpu.VMEM((seq, 2 * HEAD_DIM), BF16)] * 2
                       + [pltpu.VMEM((seq, HEAD_DIM), F32)] * 3,
        compiler_params=_params(("parallel", "parallel")),
        name="prompt_attn",
    )(proj, proj, proj, proj, cos, sin)


def _lower_bound(lb_raw, layer):
    e = jnp.exp(lb_raw - jnp.max(lb_raw, axis=0, keepdims=True))
    sm = e / jnp.sum(e, axis=0, keepdims=True)
    return jnp.sum(sm[:layer + 1], axis=0, keepdims=True)


def _split2(x):
    hi = x.astype(BF16)
    return hi, (x - hi.astype(F32)).astype(BF16)


def _rms_gate(o, norm_g, gate):
    o = o * lax.rsqrt(jnp.mean(o * o, axis=-1, keepdims=True) + RMS_EPS)
    return o * norm_g * _silu(gate)


def _hgrn_gates(fb, lb, tril):
    f = lb + (1.0 - lb) * _sigmoid(fb)
    g = jnp.log(f)
    b = sum(jnp.dot(tril, piece, preferred_element_type=F32) for piece in _split2(g))
    return 1.0 - f, b


def _hgrn_fast_prepare(q_ref, f_ref, lb, qh_ref, kh_ref, el_ref):
    n_heads, seq, _ = q_ref.shape
    blk = HGRN_PREP_ROWS
    tril = _tril_ones(blk)

    def body(bi, carry):
        r0 = pl.multiple_of(bi * blk, blk)
        fb = jnp.concatenate([f_ref[h, pl.ds(r0, blk), :] for h in range(n_heads)], axis=1)
        f = lb + (1.0 - lb) * _sigmoid(fb)
        b_blk = sum(jnp.dot(tril, piece, preferred_element_type=F32) for piece in _split2(jnp.log(f)))
        kk = 1.0 - f
        for c0 in range(0, blk, HGRN_CHUNK):
            rs = slice(c0, c0 + HGRN_CHUNK)
            b = b_blk[rs] - b_blk[c0 - 1:c0] if c0 else b_blk[rs]
            rows = pl.ds(r0 + c0, HGRN_CHUNK)
            q = jnp.concatenate([q_ref[h, rows, :] for h in range(n_heads)], axis=1)
            qh_ref[rows, :] = (q * jnp.exp(b)).astype(BF16)
            kh_ref[rows, :] = (kk[rs] * jnp.exp(-b)).astype(BF16)
            el_ref[pl.ds(bi * (blk // HGRN_CHUNK) + c0 // HGRN_CHUNK, 1), :] = jnp.exp(b[HGRN_CHUNK - 1:])
        return carry

    lax.fori_loop(0, seq // blk, body, 0, unroll=4)


def _hgrn_fast_scan(i_ref, g_ref, norm_g, qh_ref, kh_ref, el_ref, st_ref, z_ref):
    n_heads, seq, _ = i_ref.shape
    c = HGRN_CHUNK
    heads = range(n_heads)
    chunks = range(HGRN_SCAN_CHUNKS)
    lanes = [slice(h * HEAD_DIM, (h + 1) * HEAD_DIM) for h in heads]
    causal = lax.broadcasted_iota(jnp.int32, (c, c), 0) >= lax.broadcasted_iota(jnp.int32, (c, c), 1)
    tn = (((0,), (0,)), ((), ()))

    def body(ti, carry):
        rows = [pl.ds(pl.multiple_of((ti * HGRN_SCAN_CHUNKS + k) * c, c), c) for k in chunks]
        qh = [[qh_ref[rows[k], lanes[h]] for h in heads] for k in chunks]
        kh = [[kh_ref[rows[k], lanes[h]] for h in heads] for k in chunks]
        vb = [[i_ref[h, rows[k], :].astype(BF16) for h in heads] for k in chunks]
        att = [[jnp.where(causal, _dot_nt(qh[k][h], kh[k][h]), 0.0).astype(BF16) for h in heads] for k in chunks]
        ds = [[lax.dot_general(vb[k][h], kh[k][h], tn, preferred_element_type=F32) for h in heads] for k in chunks]
        st = [st_ref[h] for h in heads]
        o = []
        for k in chunks:
            el = el_ref[pl.ds(ti * HGRN_SCAN_CHUNKS + k, 1), :]
            o.append([_dot_nt(qh[k][h], st[h].astype(BF16)) for h in heads])
            st = [(st[h] + ds[k][h]) * el[:, lanes[h]] for h in heads]
        for h in heads:
            st_ref[h] = st[h]
        for k in chunks:
            for h in heads:
                o_kh = o[k][h] + jnp.dot(att[k][h], vb[k][h], preferred_element_type=F32)
                z_ref[h, rows[k], :] = _rms_gate(o_kh, norm_g[:, lanes[h]], g_ref[h, rows[k], :]).astype(z_ref.dtype)
        return carry

    lax.fori_loop(0, seq // (c * HGRN_SCAN_CHUNKS), body, 0)


def _hgrn_chunk(q, fb, v, lb, st, tril):
    c = q.shape[0]
    kk, b = _hgrn_gates(fb, lb, tril)
    o = _dot_nt((q * jnp.exp(b)).astype(BF16), st.astype(BF16))

    s_idx = lax.broadcasted_iota(jnp.int32, (c, 1), 0)
    lane = lax.broadcasted_iota(jnp.int32, (HGRN_SUB, c), 1)
    row = lax.broadcasted_iota(jnp.int32, (HGRN_SUB, c), 0)
    att_rows = []
    for i0 in range(0, c, HGRN_SUB):
        qi = q[i0:i0 + HGRN_SUB]
        bi = b[i0:i0 + HGRN_SUB]
        if i0 > 0:
            bref = b[i0 - 1:i0]
            kt = jnp.where(s_idx < i0, kk * jnp.exp(jnp.minimum(bref - b, 0.0)), 0.0)
            att = _dot_nt((qi * jnp.exp(bi - bref)).astype(BF16), kt.astype(BF16))
        else:
            att = jnp.zeros((HGRN_SUB, c), F32)
        for j in range(HGRN_SUB):
            s = i0 + j
            e = jnp.exp(jnp.minimum(bi - b[s:s + 1], 0.0))
            colv = jnp.sum(qi * kk[s:s + 1] * e, axis=-1, keepdims=True)
            att = jnp.where((lane == s) & (row >= j), colv, att)
        att_rows.append(att)
    att = jnp.concatenate(att_rows, axis=0)
    vb = v.astype(BF16)
    o = o + jnp.dot(att.astype(BF16), vb, preferred_element_type=F32)

    b_last = b[c - 1:c]
    kd = (kk * jnp.exp(b_last - b)).astype(BF16)
    st_new = st * jnp.exp(b_last) + lax.dot_general(vb, kd, (((0,), (0,)), ((), ())),
                                                    preferred_element_type=F32)
    return o, st_new


def _tril_ones(c):
    return (lax.broadcasted_iota(jnp.int32, (c, c), 0) >= lax.broadcasted_iota(jnp.int32, (c, c), 1)
            ).astype(BF16)


def _hgrn_kernel(layer, q_ref, f_ref, i_ref, g_ref, lb_ref, ng_ref, z_ref, s_ref, st_ref, qh_ref, kh_ref, el_ref):
    n_heads, seq, _ = q_ref.shape
    lb = _lower_bound(lb_ref[...], layer)
    norm_g = ng_ref[...]
    st_ref[...] = jnp.zeros_like(st_ref)

    def fast():
        _hgrn_fast_prepare(q_ref, f_ref, lb, qh_ref, kh_ref, el_ref)
        _hgrn_fast_scan(i_ref, g_ref, norm_g, qh_ref, kh_ref, el_ref, st_ref, z_ref)

    def safe():
        tril = _tril_ones(HGRN_CHUNK)

        def body(ci, carry):
            rows = pl.ds(pl.multiple_of(ci * HGRN_CHUNK, HGRN_CHUNK), HGRN_CHUNK)
            for h in range(n_heads):
                lanes = slice(h * HEAD_DIM, (h + 1) * HEAD_DIM)
                o, st = _hgrn_chunk(q_ref[h, rows, :], f_ref[h, rows, :], i_ref[h, rows, :], lb[:, lanes],
                                    st_ref[h], tril)
                st_ref[h] = st
                z_ref[h, rows, :] = _rms_gate(o, norm_g[:, lanes], g_ref[h, rows, :]).astype(z_ref.dtype)
            return carry

        lax.fori_loop(0, seq // HGRN_CHUNK, body, 0)

    fast_ok = HGRN_CHUNK * -jnp.log(jnp.min(lb)) <= HGRN_SAFE_LOG_RANGE
    lax.cond(fast_ok, fast, safe)
    for h in range(n_heads):
        s_ref[h] = st_ref[h].T


def _prompt_hgrn(proj, lb_raw, norm_g, layer, bsz, seq):
    hps = HGRN_HEADS_PER_STEP
    assert N_HEADS_HGRN % hps == 0 and all(g % hps == 0 for g in (G_QB, G_FB, G_IB, G_GB))
    assert seq % (HGRN_CHUNK * HGRN_SCAN_CHUNKS) == 0 and seq % HGRN_PREP_ROWS == 0 and HGRN_PREP_ROWS % HGRN_CHUNK == 0

    def cols(g0):
        return pl.BlockSpec((hps, seq, HEAD_DIM), lambda b, h: (g0 // hps + h, b, 0))

    return pl.pallas_call(
        functools.partial(_hgrn_kernel, layer),
        grid=(bsz, N_HEADS_HGRN // hps),
        in_specs=[cols(G_QB), cols(G_FB), cols(G_IB), cols(G_GB),
                  pl.BlockSpec((DEPTH + 1, hps * HEAD_DIM), lambda b, h: (0, h)),
                  pl.BlockSpec((1, hps * HEAD_DIM), lambda b, h: (0, h))],
        out_specs=[pl.BlockSpec((hps, seq, HEAD_DIM), lambda b, h: (h, b, 0)),
                   pl.BlockSpec((None, hps, HEAD_DIM, HEAD_DIM), lambda b, h: (b, h, 0, 0))],
        out_shape=[jax.ShapeDtypeStruct((N_HEADS_HGRN, bsz * seq, HEAD_DIM), BF16),
                   jax.ShapeDtypeStruct((bsz, N_HEADS_HGRN, HEAD_DIM, HEAD_DIM), F32)],
        scratch_shapes=[pltpu.VMEM((hps, HEAD_DIM, HEAD_DIM), F32),
                        pltpu.VMEM((seq, hps * HEAD_DIM), BF16),
                        pltpu.VMEM((seq, hps * HEAD_DIM), BF16),
                        pltpu.VMEM((seq // HGRN_CHUNK, hps * HEAD_DIM), F32)],
        compiler_params=_params(("parallel", "parallel")),
        name="prompt_hgrn",
    )(proj, proj, proj, proj, lb_raw, norm_g)


MEM_ROWS = 256
MEM_GROUP = 8


def _mem_kernel(q_ref, g_ref, mem_ref, w_ref, z_ref, mk_ref, mv_ref):
    seq = q_ref.shape[1]
    kv = jnp.dot(mem_ref[...].astype(BF16), w_ref[...].astype(BF16), preferred_element_type=F32)
    ones = jnp.ones((N_MEM, HEAD_DIM), BF16)
    mk_b, mv_b = [], []
    for h in range(N_HEADS_MEM):
        mk = kv[:, h * HEAD_DIM:(h + 1) * HEAD_DIM]
        mv = kv[:, W_MEM + h * HEAD_DIM:W_MEM + (h + 1) * HEAD_DIM]
        mk_ref[:, h, :] = mk
        mv_ref[:, h, :] = mv
        mk_b.append(mk.astype(BF16))
        mv_b.append(jnp.concatenate([mv.astype(BF16), ones], axis=1))

    items = [(h, pl.ds(n * MEM_ROWS, MEM_ROWS)) for h in range(N_HEADS_MEM) for n in range(seq // MEM_ROWS)]
    for g0 in range(0, len(items), MEM_GROUP):
        group = items[g0:g0 + MEM_GROUP]
        scores = [_dot_nt((q_ref[h, rows, :] * ATTN_SCALE_LOG2).astype(BF16), mk_b[h]) for h, rows in group]
        probs = [jnp.exp2(s - jnp.max(s, axis=-1, keepdims=True)).astype(BF16) for s in scores]
        for (h, rows), p in zip(group, probs):
            pv = jnp.dot(p, mv_b[h], preferred_element_type=F32)
            o = pv[:, :HEAD_DIM] / pv[:, HEAD_DIM:]
            z_ref[h, rows, :] = (o * _silu(g_ref[h, rows, :])).astype(z_ref.dtype)


def _prompt_mem(proj, mem, w_kv, bsz, seq):
    assert G_QM % N_HEADS_MEM == 0 and G_GM % N_HEADS_MEM == 0

    def cols(g0):
        return pl.BlockSpec((N_HEADS_MEM, seq, HEAD_DIM), lambda b: (g0 // N_HEADS_MEM, b, 0))

    kv_out = pl.BlockSpec((None, N_MEM, N_HEADS_MEM, HEAD_DIM), lambda b: (b, 0, 0, 0))
    kv_shape = jax.ShapeDtypeStruct((bsz, N_MEM, N_HEADS_MEM, HEAD_DIM), F32)
    return pl.pallas_call(
        _mem_kernel,
        grid=(bsz,),
        in_specs=[cols(G_QM), cols(G_GM), pl.BlockSpec((N_MEM, D_MODEL), lambda b: (b, 0)),
                  pl.BlockSpec((D_MODEL, 2 * W_MEM), lambda b: (0, 0))],
        out_specs=[pl.BlockSpec((N_HEADS_MEM, seq, HEAD_DIM), lambda b: (0, b, 0)), kv_out, kv_out],
        out_shape=[jax.ShapeDtypeStruct((N_HEADS_MEM, bsz * seq, HEAD_DIM), BF16), kv_shape, kv_shape],
        compiler_params=_params(("parallel",)),
        name="prompt_mem",
    )(proj, proj, mem, w_kv)


def _merge_kernel(cast_w, za_ref, zh_ref, zm_ref, x_ref, w_ref, lg_ref, lb_ref, o_ref, *rest):
    if cast_w:
        wb_ref, z_ref, stage, sem = rest
        n_rows = stage.shape[1]
        n_chunks = w_ref.shape[0] // n_rows

        def chunk_copy(c):
            return pltpu.make_async_copy(w_ref.at[pl.ds(c * n_rows, n_rows), :], stage.at[c % 2], sem.at[c % 2])

        @pl.when(pl.program_id(0) == 0)
        def _():
            chunk_copy(0).start()
            for c in range(n_chunks):
                if c + 1 < n_chunks:
                    chunk_copy(c + 1).start()
                chunk_copy(c).wait()
                wb_ref[c * n_rows:(c + 1) * n_rows, :] = stage[c % 2].astype(BF16)
    else:
        wb_ref, (z_ref,) = w_ref, rest

    c0 = 0
    for ref in (za_ref, zh_ref, zm_ref):
        for c in range(ref.shape[0]):
            z_ref[:, (c0 + c) * LANES:(c0 + c + 1) * LANES] = ref[c].astype(BF16)
        c0 += ref.shape[0]
    y = jnp.dot(z_ref[...], wb_ref[...], preferred_element_type=F32)
    r = DEEPNORM_ALPHA * x_ref[...] + y
    mu = jnp.mean(r, axis=-1, keepdims=True)
    d = r - mu
    var = jnp.mean(d * d, axis=-1, keepdims=True)
    o_ref[...] = d * lax.rsqrt(var + LN_EPS) * lg_ref[...] + lb_ref[...]


def _merge(za, zh, zm, x, w_out, ln_g, ln_b, tm):
    m = x.shape[0]
    cast_w = w_out.dtype != BF16

    def slab(a):
        return pl.BlockSpec((a.shape[0], tm, LANES), lambda i: (0, i, 0))

    const = lambda shape: pl.BlockSpec(shape, lambda i: (0, 0))
    y_spec, y_shape = pl.BlockSpec((tm, D_MODEL), lambda i: (i, 0)), jax.ShapeDtypeStruct((m, D_MODEL), F32)
    w_block = const((MIX_WIDTH, D_MODEL))
    z_scratch = pltpu.VMEM((tm, MIX_WIDTH), BF16)
    if cast_w:
        w_spec = pl.BlockSpec(memory_space=pl.ANY)
        out_specs, out_shape = [y_spec, w_block], [y_shape, jax.ShapeDtypeStruct((MIX_WIDTH, D_MODEL), BF16)]
        scratch = [z_scratch, pltpu.VMEM((2, MERGE_CAST_ROWS, D_MODEL), F32), pltpu.SemaphoreType.DMA((2,))]
    else:
        w_spec, out_specs, out_shape, scratch = w_block, y_spec, y_shape, [z_scratch]
    return pl.pallas_call(
        functools.partial(_merge_kernel, cast_w),
        grid=(m // tm,),
        in_specs=[slab(za), slab(zh), slab(zm), pl.BlockSpec((tm, D_MODEL), lambda i: (i, 0)),
                  w_spec, const((1, D_MODEL)), const((1, D_MODEL))],
        out_specs=out_specs,
        out_shape=out_shape,
        scratch_shapes=scratch,
        compiler_params=_params(("arbitrary",)),
        name="merge",
    )(za, zh, zm, x, w_out, ln_g, ln_b)


def _column(row):
    return jnp.broadcast_to(row, (HEAD_DIM, HEAD_DIM)).T


HBM_TILE_ROWS = 8
SAMPLE_ROWS = 2


def _window_pieces(past):
    pieces = []
    for dil in DILATIONS:
        if dil == 1:
            pieces.append((0, past - BAND, BAND, None))
        elif dil < HBM_TILE_ROWS:
            n = BAND * dil // HBM_TILE_ROWS
            pieces += [(1, past // HBM_TILE_ROWS - n, n, r) for r in range(0, HBM_TILE_ROWS, dil)]
        else:
            pieces.append((2, past // dil - BAND, BAND, 0))
    return pieces


def _cache_views(cache):
    depth, bsz, past, nh, hd = cache.shape
    assert all(d == 1 or HBM_TILE_ROWS % d == 0 or d % HBM_TILE_ROWS == 0 for d in DILATIONS)
    big = max(DILATIONS)
    hm = cache.transpose(0, 1, 3, 2, 4)
    return (hm, hm.reshape(depth, bsz, nh, past // HBM_TILE_ROWS, HBM_TILE_ROWS, hd),
            hm.reshape(depth, bsz, nh, past // big, big, hd))


def _sample_row(layer, p_ref, cos, sin, k_win, v_win, st_ref, mk_ref, mv_ref, lb_all, ng_ref, z_ref, ko_ref, vo_ref,
                so_ref):
    q_all = _rope(p_ref[G_QA:G_QA + N_HEADS_ATTN, :], cos, sin) * ATTN_SCALE
    k_all = _rope(p_ref[G_KA:G_KA + N_HEADS_ATTN, :], cos, sin)
    v_all = p_ref[G_VA:G_VA + N_HEADS_ATTN, :]
    ko_ref[...] = k_all
    vo_ref[...] = v_all

    for h in range(N_HEADS_ATTN):
        q = q_all[h:h + 1]
        s_new = jnp.sum(q * k_all[h:h + 1], axis=-1, keepdims=True)
        s = jnp.sum(k_win[h] * q, axis=-1, keepdims=True)
        m = jnp.maximum(jnp.max(s, axis=0, keepdims=True), s_new)
        p = jnp.exp(s - m)
        p_new = jnp.exp(s_new - m) * len(DILATIONS)
        den = jnp.sum(p, axis=0, keepdims=True) + p_new
        num = jnp.sum(p * v_win[h], axis=0, keepdims=True) + p_new * v_all[h:h + 1]
        z_ref[h:h + 1, :] = (num / den) * _silu(p_ref[G_GA + h:G_GA + h + 1, :])

    for h in range(N_HEADS_HGRN):
        lanes = slice(h * HEAD_DIM, (h + 1) * HEAD_DIM)
        lb = _lower_bound(lb_all[:, lanes], layer)
        f = lb + (1.0 - lb) * _sigmoid(p_ref[G_FB + h:G_FB + h + 1, :])
        f_col = _column(f)
        q_col = _column(p_ref[G_QB + h:G_QB + h + 1, :])
        s_new = f_col * st_ref[h] + (1.0 - f_col) * p_ref[G_IB + h:G_IB + h + 1, :]
        so_ref[h] = s_new
        o = jnp.sum(s_new * q_col, axis=0, keepdims=True)
        z_ref[N_HEADS_ATTN + h:N_HEADS_ATTN + h + 1, :] = _rms_gate(
            o, ng_ref[:, lanes], p_ref[G_GB + h:G_GB + h + 1, :])

    fold = mk_ref.shape[1] // N_HEADS_MEM
    unfold = lambda a: sum(a[i * N_HEADS_MEM:(i + 1) * N_HEADS_MEM] for i in range(fold))
    q_m = p_ref[G_QM:G_QM + N_HEADS_MEM, :] * ATTN_SCALE
    s = jnp.sum(mk_ref[...] * jnp.concatenate([q_m] * fold, axis=0)[None], axis=-1, keepdims=True)
    m = jnp.max(s, axis=0)
    m = functools.reduce(jnp.maximum, [m[i * N_HEADS_MEM:(i + 1) * N_HEADS_MEM] for i in range(fold)])
    p = jnp.exp(s - jnp.concatenate([m] * fold, axis=0)[None])
    o = unfold(jnp.sum(p * mv_ref[...], axis=0)) / unfold(jnp.sum(p, axis=0))
    row = N_HEADS_ATTN + N_HEADS_HGRN
    z_ref[row:row + N_HEADS_MEM, :] = o * _silu(p_ref[G_GM:G_GM + N_HEADS_MEM, :])


def _sample_kernel(layer, past, p_ref, cos_ref, sin_ref, k0_hbm, k1_hbm, k2_hbm, v0_hbm, v1_hbm, v2_hbm,
                   st_ref, mk_ref, mv_ref, lb_ref, ng_ref, z_ref, ko_ref, vo_ref, so_ref, kbuf, vbuf, sem):
    step = pl.program_id(0)
    slot = step % 2
    n_rows = p_ref.shape[0]
    pieces = _window_pieces(past)

    def window_copies(stp, sl):
        out = []
        for rr in range(n_rows):
            for ci, (views, buf) in enumerate((((k0_hbm, k1_hbm, k2_hbm), kbuf), ((v0_hbm, v1_hbm, v2_hbm), vbuf))):
                off = 0
                for pi, (vi, start, count, res) in enumerate(pieces):
                    view, row = views[vi], stp * n_rows + rr
                    src = (view.at[layer, row, :, pl.ds(start, count), :] if res is None
                           else view.at[layer, row, :, pl.ds(start, count), res, :])
                    out.append(pltpu.make_async_copy(src, buf.at[sl, rr, :, pl.ds(off, count), :],
                                                     sem.at[sl, rr, ci, pi]))
                    off += count
        return out

    @pl.when(step == 0)
    def _():
        for c in window_copies(step, slot):
            c.start()

    @pl.when(step + 1 < pl.num_programs(0))
    def _():
        for c in window_copies(step + 1, 1 - slot):
            c.start()

    for c in window_copies(step, slot):
        c.wait()

    cos, sin, lb_all = cos_ref[...], sin_ref[...], lb_ref[...]
    for rr in range(n_rows):
        _sample_row(layer, p_ref.at[rr], cos, sin, kbuf.at[slot, rr], vbuf.at[slot, rr], st_ref.at[rr], mk_ref.at[rr],
                    mv_ref.at[rr], lb_all, ng_ref, z_ref.at[rr], ko_ref.at[rr], vo_ref.at[rr], so_ref.at[rr])


def _sample_mixers(proj_rows, cos, sin, win_k, win_v, state, mem_k, mem_v, lb_raw, norm_g, layer):
    bsz, past = win_k.shape[1:3]
    rps = SAMPLE_ROWS
    assert all(past % d == 0 and past >= w for w, d in zip(WINDOWS, DILATIONS)) and bsz % rps == 0
    assert all(w // d == BAND for w, d in zip(WINDOWS, DILATIONS)) and past % HBM_TILE_ROWS == 0
    n_keys = BAND * len(DILATIONS)
    n_pieces = len(_window_pieces(past))
    any_spec = pl.BlockSpec(memory_space=pl.ANY)
    vec = lambda n: pl.BlockSpec((n, W_HGRN), lambda b: (0, 0))
    heads = lambda n: pl.BlockSpec((rps, n, HEAD_DIM), lambda b: (b, 0, 0))
    fold = HBM_TILE_ROWS // N_HEADS_MEM
    mem_k, mem_v = (a.reshape(a.shape[0], bsz, N_MEM // fold, fold * N_HEADS_MEM, HEAD_DIM) for a in (mem_k, mem_v))
    mem_spec = pl.BlockSpec((None, rps, N_MEM // fold, fold * N_HEADS_MEM, HEAD_DIM), lambda b: (layer, b, 0, 0, 0))
    return pl.pallas_call(
        functools.partial(_sample_kernel, layer, past),
        grid=(bsz // rps,),
        in_specs=[heads(N_GROUPS), pl.BlockSpec((1, HEAD_DIM), lambda b: (0, 0)),
                  pl.BlockSpec((1, HEAD_DIM), lambda b: (0, 0)), *([any_spec] * 6),
                  pl.BlockSpec((None, rps, N_HEADS_HGRN, HEAD_DIM, HEAD_DIM), lambda b: (layer, b, 0, 0, 0)),
                  mem_spec, mem_spec, vec(DEPTH + 1), vec(1)],
        out_specs=[heads(MIX_WIDTH // HEAD_DIM), heads(N_HEADS_ATTN), heads(N_HEADS_ATTN),
                   pl.BlockSpec((rps, N_HEADS_HGRN, HEAD_DIM, HEAD_DIM), lambda b: (b, 0, 0, 0))],
        out_shape=[jax.ShapeDtypeStruct((bsz, MIX_WIDTH // HEAD_DIM, HEAD_DIM), F32),
                   jax.ShapeDtypeStruct((bsz, N_HEADS_ATTN, HEAD_DIM), F32),
                   jax.ShapeDtypeStruct((bsz, N_HEADS_ATTN, HEAD_DIM), F32),
                   jax.ShapeDtypeStruct((bsz, N_HEADS_HGRN, HEAD_DIM, HEAD_DIM), F32)],
        scratch_shapes=[pltpu.VMEM((2, rps, N_HEADS_ATTN, n_keys, HEAD_DIM), F32),
                        pltpu.VMEM((2, rps, N_HEADS_ATTN, n_keys, HEAD_DIM), F32),
                        pltpu.SemaphoreType.DMA((2, rps, 2, n_pieces))],
        compiler_params=_params(("arbitrary",)),
        name="sample_mixers",
    )(proj_rows, cos, sin, *_cache_views(win_k), *_cache_views(win_v), state, mem_k, mem_v, lb_raw, norm_g)


def _rope_tables(pos):
    half = HEAD_DIM // 2
    inv_freq = 1.0 / (ROPE_THETA ** (np.arange(half, dtype=np.float64) / half))
    ang = np.asarray(pos, np.float64)[:, None] * inv_freq[None, :]
    cos, sin = np.cos(ang), np.sin(ang)
    return (jnp.asarray(np.concatenate([cos, cos], axis=-1), F32),
            jnp.asarray(np.concatenate([-sin, sin], axis=-1), F32))


def kernel(x_prompt, x_sample, cache_win_k, cache_win_v, state_hgrn, cache_mem_k, cache_mem_v, mem_prompt,
           w_in, w_mem_kv, hgrn_lb_raw, hgrn_norm_g, w_out, ln_g, ln_b):
    bsz, seq, _ = x_prompt.shape
    dbsz, n_new, _ = x_sample.shape
    assert n_new == 1 and seq % (BAND * max(DILATIONS)) == 0
    cos_p, sin_p = _rope_tables(np.arange(seq))
    cos_s, sin_s = _rope_tables(PAST_LEN + np.arange(n_new))

    hp = x_prompt.reshape(bsz * seq, D_MODEL)
    hs = x_sample.reshape(dbsz * n_new, D_MODEL)
    mem = mem_prompt.reshape(bsz * N_MEM, D_MODEL)
    outs = [[] for _ in range(8)]
    for layer in range(DEPTH):
        lb_raw = hgrn_lb_raw
        norm_g = hgrn_norm_g[layer][None]
        lg, lbias = ln_g[layer][None], ln_b[layer][None]

        proj, proj_s = _project(hp, hs, w_in[layer], PROJ_ROWS, PROJ_FIRST_COLS, PROJ_COLS)
        za, k1, v1 = _prompt_attention(proj, cos_p, sin_p, bsz, seq)
        zh, s1 = _prompt_hgrn(proj, lb_raw, norm_g, layer, bsz, seq)
        zm, mk1, mv1 = _prompt_mem(proj, mem, w_mem_kv[layer], bsz, seq)
        hp, w_out_b = _merge(za, zh, zm, hp, w_out[layer], lg, lbias, MERGE_ROWS)

        proj_s = proj_s.transpose(1, 0, 2)
        zs, k2, v2, s2 = _sample_mixers(proj_s, cos_s, sin_s, cache_win_k, cache_win_v, state_hgrn,
                                        cache_mem_k, cache_mem_v, lb_raw, norm_g, layer)
        zs = zs.transpose(1, 0, 2)
        hs = _merge(zs[:N_HEADS_ATTN], zs[N_HEADS_ATTN:N_HEADS_ATTN + N_HEADS_HGRN],
                    zs[N_HEADS_ATTN + N_HEADS_HGRN:], hs, w_out_b, lg, lbias, dbsz)

        new = (k1.transpose(0, 2, 1, 3), v1.transpose(0, 2, 1, 3), s1, mk1, mv1,
               k2.reshape(dbsz, n_new, N_HEADS_ATTN, HEAD_DIM), v2.reshape(dbsz, n_new, N_HEADS_ATTN, HEAD_DIM),
               s2.astype(state_hgrn.dtype))
        for acc, val in zip(outs, new):
            acc.append(val)

    return (hp.reshape(bsz, seq, D_MODEL), hs.reshape(dbsz, n_new, D_MODEL), *[jnp.stack(o) for o in outs])
```

```python
import functools

import jax
import jax.numpy as jnp
import numpy as np
from jax import lax
from jax.experimental import pallas as pl
from jax.experimental.pallas import tpu as pltpu

F32 = jnp.float32
BF16 = jnp.bfloat16

D_MODEL = 2048
DEPTH = 1
PAST_LEN = 8192
HEAD_DIM = 128
N_HEADS_ATTN = 6
N_HEADS_HGRN = 6
N_HEADS_MEM = 4
W_ATTN = N_HEADS_ATTN * HEAD_DIM
W_HGRN = N_HEADS_HGRN * HEAD_DIM
W_MEM = N_HEADS_MEM * HEAD_DIM
MIX_WIDTH = W_ATTN + W_HGRN + W_MEM
WINDOWS = (128, 512, 2048)
DILATIONS = (1, 4, 16)
N_MEM = 256
ROPE_THETA = 10000.0
LN_EPS = 1e-5
RMS_EPS = 1e-6
NEG_INF = -1e30
DEEPNORM_ALPHA = (2 * DEPTH) ** 0.25
ATTN_SCALE = HEAD_DIM ** -0.5
ATTN_SCALE_LOG2 = ATTN_SCALE * 1.4426950408889634

G_QA, G_KA, G_VA, G_GA = 0, 6, 12, 18
G_QB, G_FB, G_IB, G_GB = 24, 30, 36, 42
G_QM, G_GM = 48, 52
N_GROUPS = 56

LANES = 128
BAND = 128
ATTN_GROUP = 16
HGRN_CHUNK = 64
HGRN_SUB = 16
HGRN_HEADS_PER_STEP = 3
HGRN_PREP_ROWS = 256
HGRN_SCAN_CHUNKS = 16
HGRN_SAFE_LOG_RANGE = 70.0
VMEM_LIMIT = 48 * 1024 * 1024
PROJ_ROWS, PROJ_COLS, PROJ_FIRST_COLS = 1024, 1024, 512
MERGE_ROWS = 512
MERGE_CAST_ROWS = 256


def _params(sem, vmem=VMEM_LIMIT):
    return pltpu.CompilerParams(dimension_semantics=sem, vmem_limit_bytes=vmem)


def _sigmoid(x):
    return 1.0 / (1.0 + jnp.exp(-x))


def _silu(x):
    return x * _sigmoid(x)


def _dot_nt(a, b):
    return lax.dot_general(a, b, (((1,), (1,)), ((), ())), preferred_element_type=F32)


def _store_slabs(ref, acc):
    for c in range(ref.shape[0]):
        ref[c] = acc[:, c * LANES:(c + 1) * LANES]


def _proj_first_kernel(x_ref, xs_ref, w_ref, o_ref, os_ref, wb_ref, xb_ref):
    tm = x_ref.shape[0]

    @pl.when(pl.program_id(0) == 0)
    def _():
        xb_ref[:tm, :] = x_ref[...].astype(BF16)
        xb_ref[tm:, :] = xs_ref[...].astype(BF16)

    wb_ref[...] = w_ref[...].astype(BF16)
    acc = jnp.dot(xb_ref[...], wb_ref[...], preferred_element_type=F32)
    _store_slabs(o_ref, acc[:tm])
    _store_slabs(os_ref, acc[tm:])


def _proj_rest_kernel(x_ref, w_ref, first_ref, o_hbm, xb_ref, obuf, sem_out, sem_first):
    i, j = pl.program_id(0), pl.program_id(1)
    n_col = pl.num_programs(1)
    step = i * n_col + j
    last = pl.num_programs(0) * n_col - 1
    slot = step % 2
    tm = x_ref.shape[0]
    n_slab = obuf.shape[1]

    def tile_copy(stp):
        row_tile, col_tile = stp // n_col + 1, stp % n_col
        return pltpu.make_async_copy(
            obuf.at[stp % 2], o_hbm.at[pl.ds(col_tile * n_slab, n_slab), pl.ds(row_tile * tm, tm), :],
            sem_out.at[stp % 2])

    first_copy = pltpu.make_async_copy(first_ref, o_hbm.at[pl.ds(j * n_slab, n_slab), pl.ds(0, tm), :], sem_first.at[0])

    @pl.when(i == 0)
    def _():
        first_copy.start()

    @pl.when(j == 0)
    def _():
        xb_ref[...] = x_ref[...].astype(BF16)

    @pl.when(step >= 2)
    def _():
        tile_copy(step - 2).wait()

    _store_slabs(obuf.at[slot], jnp.dot(xb_ref[...], w_ref[...], preferred_element_type=F32))
    tile_copy(step).start()

    @pl.when(i == 0)
    def _():
        first_copy.wait()

    @pl.when(step == last)
    def _():
        tile_copy(step - 1).wait()
        tile_copy(step).wait()


def _project(x, xs, w, tm, tn_first, tn):
    m, k = x.shape
    ms = xs.shape[0]
    n = w.shape[1]
    assert (m // tm - 1) * (n // tn) >= 2
    proj_first, proj_s, w_bf16 = pl.pallas_call(
        _proj_first_kernel,
        grid=(n // tn_first,),
        in_specs=[pl.BlockSpec((tm, k), lambda j: (0, 0)),
                  pl.BlockSpec((ms, k), lambda j: (0, 0)),
                  pl.BlockSpec((k, tn_first), lambda j: (0, j))],
        out_specs=[pl.BlockSpec((tn_first // LANES, tm, LANES), lambda j: (j, 0, 0)),
                   pl.BlockSpec((tn_first // LANES, ms, LANES), lambda j: (j, 0, 0)),
                   pl.BlockSpec((k, tn_first), lambda j: (0, j))],
        out_shape=[jax.ShapeDtypeStruct((n // LANES, tm, LANES), F32),
                   jax.ShapeDtypeStruct((n // LANES, ms, LANES), F32),
                   jax.ShapeDtypeStruct((k, n), BF16)],
        scratch_shapes=[pltpu.VMEM((tm + ms, k), BF16)],
        compiler_params=_params(("arbitrary",)),
        name="in_proj_first",
    )(x, xs, w)
    proj = pl.pallas_call(
        _proj_rest_kernel,
        grid=(m // tm - 1, n // tn),
        in_specs=[pl.BlockSpec((tm, k), lambda i, j: (i + 1, 0)),
                  pl.BlockSpec((k, tn), lambda i, j: (0, j)),
                  pl.BlockSpec((tn // LANES, tm, LANES), lambda i, j: (jnp.where(i == 0, j, n // tn - 1), 0, 0))],
        out_specs=pl.BlockSpec(memory_space=pl.ANY),
        out_shape=jax.ShapeDtypeStruct((n // LANES, m, LANES), F32),
        scratch_shapes=[pltpu.VMEM((tm, k), BF16), pltpu.VMEM((2, tn // LANES, tm, LANES), F32),
                        pltpu.SemaphoreType.DMA((2,)), pltpu.SemaphoreType.DMA((1,))],
        compiler_params=_params(("arbitrary", "arbitrary")),
        name="in_proj",
    )(x, w_bf16, proj_first)
    return proj, proj_s


def _rope(x, cos, sin_signed):
    return x * cos + pltpu.roll(x, HEAD_DIM // 2, 1) * sin_signed


def _bias_from_count(count):
    return jnp.where(count > 1.5, 1.0, jnp.where(count > 0.5, 0.0, NEG_INF)).astype(F32)


def _block_deltas(block_gap):
    qi = lax.broadcasted_iota(jnp.int32, (BAND, BAND), 0)
    ki = lax.broadcasted_iota(jnp.int32, (BAND, BAND), 1)
    return block_gap * BAND + qi - ki


def _near_bias(block_gap):
    d = _block_deltas(block_gap)
    return _bias_from_count(((d >= 0) & (d <= BAND)).astype(F32))


def _class_bias(block_gap, far_step):
    d = _block_deltas(block_gap)
    near = (d >= 0) & (d <= BAND)
    far = (d >= 0) & ((d & (far_step - 1)) == 0)
    return _bias_from_count(near.astype(F32) + far.astype(F32))


def _softmax_stage(scores, floors=None):
    out = []
    for idx, s in enumerate(scores):
        m = jnp.broadcast_to(jnp.max(s, axis=-1, keepdims=True), (BAND, HEAD_DIM))
        if floors is not None:
            m = jnp.maximum(m, floors[idx])
        m_wide = jnp.concatenate([m] * (s.shape[1] // HEAD_DIM), axis=1)
        out.append((m, jnp.exp2(s - m_wide).astype(BF16)))
    return out


def _attn_kernel(q_ref, k_ref, v_ref, g_ref, cos_ref, sin_ref, z_ref, ko_ref, vo_ref,
                 qs_ref, qb_ref, kb_ref, vb_ref, qc_ref, kc_ref, vc_ref, acc_ref, m_ref, l_ref):
    seq = q_ref.shape[0]
    d_mid, d_far = DILATIONS[1], DILATIONS[2]
    far_step = d_far // d_mid
    cls_rows = seq // d_mid
    cos = cos_ref[...]
    sin = sin_ref[...]
    qs_ref[...] = _rope(q_ref[...], cos, sin) * ATTN_SCALE_LOG2
    ko_ref[...] = _rope(k_ref[...], cos, sin)
    vo_ref[...] = v_ref[...]
    qb_ref[...] = qs_ref[...].astype(BF16)
    kb_ref[...] = ko_ref[...].astype(BF16)
    ones = jnp.ones((seq, HEAD_DIM), BF16)
    vb_ref[:, :HEAD_DIM] = v_ref[...].astype(BF16)
    vb_ref[:, HEAD_DIM:] = ones
    vc_ref[:, HEAD_DIM:] = ones
    for r in range(d_mid):
        cls, dst = pl.ds(r, cls_rows, stride=d_mid), pl.ds(r * cls_rows, cls_rows)
        qc_ref[dst, :] = qs_ref[cls, :].astype(BF16)
        kc_ref[dst, :] = ko_ref[cls, :].astype(BF16)
        vc_ref[dst, :HEAD_DIM] = v_ref[cls, :].astype(BF16)

    n_cls_blk = cls_rows // BAND
    cls_bias = [_class_bias(gap, far_step) for gap in range(min(n_cls_blk, 3))]
    items = [(r, n) for r in range(d_mid) for n in range(n_cls_blk)]
    for g0 in range(0, len(items), ATTN_GROUP):
        group = items[g0:g0 + ATTN_GROUP]
        keys = [pl.ds(r * cls_rows, (n + 1) * BAND) for r, n in group]
        scores = [_dot_nt(qc_ref[pl.ds(r * cls_rows + n * BAND, BAND), :], kc_ref[kr, :])
                  + jnp.concatenate([cls_bias[min(n - nk, 2)] for nk in range(n + 1)], axis=1)
                  for (r, n), kr in zip(group, keys)]
        for (r, n), kr, (m, p) in zip(group, keys, _softmax_stage(scores)):
            rows = pl.ds(r + d_mid * BAND * n, BAND, stride=d_mid)
            pv = jnp.dot(p, vc_ref[kr, :], preferred_element_type=F32)
            acc_ref[rows, :] = pv[:, :HEAD_DIM]
            l_ref[rows, :] = pv[:, HEAD_DIM:]
            m_ref[rows, :] = m

    near_bias = [_near_bias(0), jnp.concatenate([_near_bias(1), _near_bias(0)], axis=1)]
    items = list(range(seq // BAND))
    for g0 in range(0, len(items), ATTN_GROUP):
        group = items[g0:g0 + ATTN_GROUP]
        rows = [pl.ds(n * BAND, BAND) for n in group]
        keys = [pl.ds(max(n - 1, 0) * BAND, BAND * min(n + 1, 2)) for n in group]
        scores = [_dot_nt(qb_ref[rw, :], kb_ref[kr, :]) + near_bias[min(n, 1)]
                  for n, rw, kr in zip(group, rows, keys)]
        probs = _softmax_stage(scores, floors=[m_ref[rw, :] for rw in rows])
        for rw, kr, (m, p) in zip(rows, keys, probs):
            w = jnp.exp2(m_ref[rw, :] - m)
            pv = jnp.dot(p, vb_ref[kr, :], preferred_element_type=F32)
            num = pv[:, :HEAD_DIM] + w * acc_ref[rw, :]
            den = pv[:, HEAD_DIM:] + w * l_ref[rw, :]
            z_ref[rw, :] = ((num / den) * _silu(g_ref[rw, :])).astype(z_ref.dtype)


def _prompt_attention(proj, cos, sin, bsz, seq):
    def col(g0):
        return pl.BlockSpec((None, seq, HEAD_DIM), lambda b, h: (g0 + h, b, 0))

    table = pl.BlockSpec((seq, HEAD_DIM), lambda b, h: (0, 0))
    kv_out = pl.BlockSpec((None, None, seq, HEAD_DIM), lambda b, h: (b, h, 0, 0))
    d_near, d_mid, d_far = DILATIONS
    far_step = d_far // d_mid
    assert d_near == 1 and d_far % d_mid == 0 and far_step & (far_step - 1) == 0
    assert all(w // d == BAND for w, d in zip(WINDOWS, DILATIONS)) and seq % (d_mid * BAND) == 0
    assert seq // d_mid <= far_step * BAND
    return pl.pallas_call(
        _attn_kernel,
        grid=(bsz, N_HEADS_ATTN),
        in_specs=[col(G_QA), col(G_KA), col(G_VA), col(G_GA), table, table],
        out_specs=[pl.BlockSpec((None, seq, HEAD_DIM), lambda b, h: (h, b, 0)), kv_out, kv_out],
        out_shape=[jax.ShapeDtypeStruct((N_HEADS_ATTN, bsz * seq, HEAD_DIM), BF16),
                   jax.ShapeDtypeStruct((bsz, N_HEADS_ATTN, seq, HEAD_DIM), F32),
                   jax.ShapeDtypeStruct((bsz, N_HEADS_ATTN, seq, HEAD_DIM), F32)],
        scratch_shapes=[pltpu.VMEM((seq, HEAD_DIM), F32)]
                       + [pltpu.VMEM((seq, HEAD_DIM), BF16), pltpu.VMEM((seq, HEAD_DIM), BF16),
                          pltpu.VMEM((seq, 2 * HEAD_DIM), BF16)] * 2
                       + [pltpu.VMEM((seq, HEAD_DIM), F32)] * 3,
        compiler_params=_params(("parallel", "parallel")),
        name="prompt_attn",
    )(proj, proj, proj, proj, cos, sin)


def _lower_bound(lb_raw, layer):
    e = jnp.exp(lb_raw - jnp.max(lb_raw, axis=0, keepdims=True))
    sm = e / jnp.sum(e, axis=0, keepdims=True)
    return jnp.sum(sm[:layer + 1], axis=0, keepdims=True)


def _split2(x):
    hi = x.astype(BF16)
    return hi, (x - hi.astype(F32)).astype(BF16)


def _rms_gate(o, norm_g, gate):
    o = o * lax.rsqrt(jnp.mean(o * o, axis=-1, keepdims=True) + RMS_EPS)
    return o * norm_g * _silu(gate)


def _hgrn_gates(fb, lb, tril):
    f = lb + (1.0 - lb) * _sigmoid(fb)
    g = jnp.log(f)
    b = sum(jnp.dot(tril, piece, preferred_element_type=F32) for piece in _split2(g))
    return 1.0 - f, b


def _hgrn_fast_prepare(q_ref, f_ref, lb, qh_ref, kh_ref, el_ref):
    n_heads, seq, _ = q_ref.shape
    blk = HGRN_PREP_ROWS
    tril = _tril_ones(blk)

    def body(bi, carry):
        r0 = pl.multiple_of(bi * blk, blk)
        fb = jnp.concatenate([f_ref[h, pl.ds(r0, blk), :] for h in range(n_heads)], axis=1)
        f = lb + (1.0 - lb) * _sigmoid(fb)
        b_blk = sum(jnp.dot(tril, piece, preferred_element_type=F32) for piece in _split2(jnp.log(f)))
        kk = 1.0 - f
        for c0 in range(0, blk, HGRN_CHUNK):
            rs = slice(c0, c0 + HGRN_CHUNK)
            b = b_blk[rs] - b_blk[c0 - 1:c0] if c0 else b_blk[rs]
            rows = pl.ds(r0 + c0, HGRN_CHUNK)
            q = jnp.concatenate([q_ref[h, rows, :] for h in range(n_heads)], axis=1)
            qh_ref[rows, :] = (q * jnp.exp(b)).astype(BF16)
            kh_ref[rows, :] = (kk[rs] * jnp.exp(-b)).astype(BF16)
            el_ref[pl.ds(bi * (blk // HGRN_CHUNK) + c0 // HGRN_CHUNK, 1), :] = jnp.exp(b[HGRN_CHUNK - 1:])
        return carry

    lax.fori_loop(0, seq // blk, body, 0, unroll=4)


def _hgrn_fast_scan(i_ref, g_ref, norm_g, qh_ref, kh_ref, el_ref, st_ref, z_ref):
    n_heads, seq, _ = i_ref.shape
    c = HGRN_CHUNK
    heads = range(n_heads)
    chunks = range(HGRN_SCAN_CHUNKS)
    lanes = [slice(h * HEAD_DIM, (h + 1) * HEAD_DIM) for h in heads]
    causal = lax.broadcasted_iota(jnp.int32, (c, c), 0) >= lax.broadcasted_iota(jnp.int32, (c, c), 1)
    tn = (((0,), (0,)), ((), ()))

    def body(ti, carry):
        rows = [pl.ds(pl.multiple_of((ti * HGRN_SCAN_CHUNKS + k) * c, c), c) for k in chunks]
        qh = [[qh_ref[rows[k], lanes[h]] for h in heads] for k in chunks]
        kh = [[kh_ref[rows[k], lanes[h]] for h in heads] for k in chunks]
        vb = [[i_ref[h, rows[k], :].astype(BF16) for h in heads] for k in chunks]
        att = [[jnp.where(causal, _dot_nt(qh[k][h], kh[k][h]), 0.0).astype(BF16) for h in heads] for k in chunks]
        ds = [[lax.dot_general(vb[k][h], kh[k][h], tn, preferred_element_type=F32) for h in heads] for k in chunks]
        st = [st_ref[h] for h in heads]
        o = []
        for k in chunks:
            el = el_ref[pl.ds(ti * HGRN_SCAN_CHUNKS + k, 1), :]
            o.append([_dot_nt(qh[k][h], st[h].astype(BF16)) for h in heads])
            st = [(st[h] + ds[k][h]) * el[:, lanes[h]] for h in heads]
        for h in heads:
            st_ref[h] = st[h]
        for k in chunks:
            for h in heads:
                o_kh = o[k][h] + jnp.dot(att[k][h], vb[k][h], preferred_element_type=F32)
                z_ref[h, rows[k], :] = _rms_gate(o_kh, norm_g[:, lanes[h]], g_ref[h, rows[k], :]).astype(z_ref.dtype)
        return carry

    lax.fori_loop(0, seq // (c * HGRN_SCAN_CHUNKS), body, 0)


def _hgrn_chunk(q, fb, v, lb, st, tril):
    c = q.shape[0]
    kk, b = _hgrn_gates(fb, lb, tril)
    o = _dot_nt((q * jnp.exp(b)).astype(BF16), st.astype(BF16))

    s_idx = lax.broadcasted_iota(jnp.int32, (c, 1), 0)
    lane = lax.broadcasted_iota(jnp.int32, (HGRN_SUB, c), 1)
    row = lax.broadcasted_iota(jnp.int32, (HGRN_SUB, c), 0)
    att_rows = []
    for i0 in range(0, c, HGRN_SUB):
        qi = q[i0:i0 + HGRN_SUB]
        bi = b[i0:i0 + HGRN_SUB]
        if i0 > 0:
            bref = b[i0 - 1:i0]
            kt = jnp.where(s_idx < i0, kk * jnp.exp(jnp.minimum(bref - b, 0.0)), 0.0)
            att = _dot_nt((qi * jnp.exp(bi - bref)).astype(BF16), kt.astype(BF16))
        else:
            att = jnp.zeros((HGRN_SUB, c), F32)
        for j in range(HGRN_SUB):
            s = i0 + j
            e = jnp.exp(jnp.minimum(bi - b[s:s + 1], 0.0))
            colv = jnp.sum(qi * kk[s:s + 1] * e, axis=-1, keepdims=True)
            att = jnp.where((lane == s) & (row >= j), colv, att)
        att_rows.append(att)
    att = jnp.concatenate(att_rows, axis=0)
    vb = v.astype(BF16)
    o = o + jnp.dot(att.astype(BF16), vb, preferred_element_type=F32)

    b_last = b[c - 1:c]
    kd = (kk * jnp.exp(b_last - b)).astype(BF16)
    st_new = st * jnp.exp(b_last) + lax.dot_general(vb, kd, (((0,), (0,)), ((), ())),
                                                    preferred_element_type=F32)
    return o, st_new


def _tril_ones(c):
    return (lax.broadcasted_iota(jnp.int32, (c, c), 0) >= lax.broadcasted_iota(jnp.int32, (c, c), 1)
            ).astype(BF16)


def _hgrn_kernel(layer, q_ref, f_ref, i_ref, g_ref, lb_ref, ng_ref, z_ref, s_ref, st_ref, qh_ref, kh_ref, el_ref):
    n_heads, seq, _ = q_ref.shape
    lb = _lower_bound(lb_ref[...], layer)
    norm_g = ng_ref[...]
    st_ref[...] = jnp.zeros_like(st_ref)

    def fast():
        _hgrn_fast_prepare(q_ref, f_ref, lb, qh_ref, kh_ref, el_ref)
        _hgrn_fast_scan(i_ref, g_ref, norm_g, qh_ref, kh_ref, el_ref, st_ref, z_ref)

    def safe():
        tril = _tril_ones(HGRN_CHUNK)

        def body(ci, carry):
            rows = pl.ds(pl.multiple_of(ci * HGRN_CHUNK, HGRN_CHUNK), HGRN_CHUNK)
            for h in range(n_heads):
                lanes = slice(h * HEAD_DIM, (h + 1) * HEAD_DIM)
                o, st = _hgrn_chunk(q_ref[h, rows, :], f_ref[h, rows, :], i_ref[h, rows, :], lb[:, lanes],
                                    st_ref[h], tril)
                st_ref[h] = st
                z_ref[h, rows, :] = _rms_gate(o, norm_g[:, lanes], g_ref[h, rows, :]).astype(z_ref.dtype)
            return carry

        lax.fori_loop(0, seq // HGRN_CHUNK, body, 0)

    fast_ok = HGRN_CHUNK * -jnp.log(jnp.min(lb)) <= HGRN_SAFE_LOG_RANGE
    lax.cond(fast_ok, fast, safe)
    for h in range(n_heads):
        s_ref[h] = st_ref[h].T


def _prompt_hgrn(proj, lb_raw, norm_g, layer, bsz, seq):
    hps = HGRN_HEADS_PER_STEP
    assert N_HEADS_HGRN % hps == 0 and all(g % hps == 0 for g in (G_QB, G_FB, G_IB, G_GB))
    assert seq % (HGRN_CHUNK * HGRN_SCAN_CHUNKS) == 0 and seq % HGRN_PREP_ROWS == 0 and HGRN_PREP_ROWS % HGRN_CHUNK == 0

    def cols(g0):
        return pl.BlockSpec((hps, seq, HEAD_DIM), lambda b, h: (g0 // hps + h, b, 0))

    return pl.pallas_call(
        functools.partial(_hgrn_kernel, layer),
        grid=(bsz, N_HEADS_HGRN // hps),
        in_specs=[cols(G_QB), cols(G_FB), cols(G_IB), cols(G_GB),
                  pl.BlockSpec((DEPTH + 1, hps * HEAD_DIM), lambda b, h: (0, h)),
                  pl.BlockSpec((1, hps * HEAD_DIM), lambda b, h: (0, h))],
        out_specs=[pl.BlockSpec((hps, seq, HEAD_DIM), lambda b, h: (h, b, 0)),
                   pl.BlockSpec((None, hps, HEAD_DIM, HEAD_DIM), lambda b, h: (b, h, 0, 0))],
        out_shape=[jax.ShapeDtypeStruct((N_HEADS_HGRN, bsz * seq, HEAD_DIM), BF16),
                   jax.ShapeDtypeStruct((bsz, N_HEADS_HGRN, HEAD_DIM, HEAD_DIM), F32)],
        scratch_shapes=[pltpu.VMEM((hps, HEAD_DIM, HEAD_DIM), F32),
                        pltpu.VMEM((seq, hps * HEAD_DIM), BF16),
                        pltpu.VMEM((seq, hps * HEAD_DIM), BF16),
                        pltpu.VMEM((seq // HGRN_CHUNK, hps * HEAD_DIM), F32)],
        compiler_params=_params(("parallel", "parallel")),
        name="prompt_hgrn",
    )(proj, proj, proj, proj, lb_raw, norm_g)


MEM_ROWS = 256
MEM_GROUP = 8


def _mem_kernel(q_ref, g_ref, mem_ref, w_ref, z_ref, mk_ref, mv_ref):
    seq = q_ref.shape[1]
    kv = jnp.dot(mem_ref[...].astype(BF16), w_ref[...].astype(BF16), preferred_element_type=F32)
    ones = jnp.ones((N_MEM, HEAD_DIM), BF16)
    mk_b, mv_b = [], []
    for h in range(N_HEADS_MEM):
        mk = kv[:, h * HEAD_DIM:(h + 1) * HEAD_DIM]
        mv = kv[:, W_MEM + h * HEAD_DIM:W_MEM + (h + 1) * HEAD_DIM]
        mk_ref[:, h, :] = mk
        mv_ref[:, h, :] = mv
        mk_b.append(mk.astype(BF16))
        mv_b.append(jnp.concatenate([mv.astype(BF16), ones], axis=1))

    items = [(h, pl.ds(n * MEM_ROWS, MEM_ROWS)) for h in range(N_HEADS_MEM) for n in range(seq // MEM_ROWS)]
    for g0 in range(0, len(items), MEM_GROUP):
        group = items[g0:g0 + MEM_GROUP]
        scores = [_dot_nt((q_ref[h, rows, :] * ATTN_SCALE_LOG2).astype(BF16), mk_b[h]) for h, rows in group]
        probs = [jnp.exp2(s - jnp.max(s, axis=-1, keepdims=True)).astype(BF16) for s in scores]
        for (h, rows), p in zip(group, probs):
            pv = jnp.dot(p, mv_b[h], preferred_element_type=F32)
            o = pv[:, :HEAD_DIM] / pv[:, HEAD_DIM:]
            z_ref[h, rows, :] = (o * _silu(g_ref[h, rows, :])).astype(z_ref.dtype)


def _prompt_mem(proj, mem, w_kv, bsz, seq):
    assert G_QM % N_HEADS_MEM == 0 and G_GM % N_HEADS_MEM == 0

    def cols(g0):
        return pl.BlockSpec((N_HEADS_MEM, seq, HEAD_DIM), lambda b: (g0 // N_HEADS_MEM, b, 0))

    kv_out = pl.BlockSpec((None, N_MEM, N_HEADS_MEM, HEAD_DIM), lambda b: (b, 0, 0, 0))
    kv_shape = jax.ShapeDtypeStruct((bsz, N_MEM, N_HEADS_MEM, HEAD_DIM), F32)
    return pl.pallas_call(
        _mem_kernel,
        grid=(bsz,),
        in_specs=[cols(G_QM), cols(G_GM), pl.BlockSpec((N_MEM, D_MODEL), lambda b: (b, 0)),
                  pl.BlockSpec((D_MODEL, 2 * W_MEM), lambda b: (0, 0))],
        out_specs=[pl.BlockSpec((N_HEADS_MEM, seq, HEAD_DIM), lambda b: (0, b, 0)), kv_out, kv_out],
        out_shape=[jax.ShapeDtypeStruct((N_HEADS_MEM, bsz * seq, HEAD_DIM), BF16), kv_shape, kv_shape],
        compiler_params=_params(("parallel",)),
        name="prompt_mem",
    )(proj, proj, mem, w_kv)


def _layer_norm_residual(x, y, ln_g, ln_b):
    r = DEEPNORM_ALPHA * x + y
    mu = jnp.mean(r, axis=-1, keepdims=True)
    d = r - mu
    var = jnp.mean(d * d, axis=-1, keepdims=True)
    return d * lax.rsqrt(var + LN_EPS) * ln_g + ln_b


def _merge_kernel(za_ref, zh_ref, zm_ref, x_ref, zs_ref, xs_ref, w_ref, lg_ref, lb_ref, o_ref, os_ref,
                  wb_ref, z_ref, stage, sem):
    tm = x_ref.shape[0]
    n_rows = stage.shape[1]
    n_chunks = w_ref.shape[0] // n_rows
    first = pl.program_id(0) == 0

    def chunk_copy(c):
        return pltpu.make_async_copy(w_ref.at[pl.ds(c * n_rows, n_rows), :], stage.at[c % 2], sem.at[c % 2])

    @pl.when(first)
    def _():
        chunk_copy(0).start()
        for c in range(n_chunks):
            if c + 1 < n_chunks:
                chunk_copy(c + 1).start()
            chunk_copy(c).wait()
            wb_ref[c * n_rows:(c + 1) * n_rows, :] = stage[c % 2].astype(BF16)
        for c in range(zs_ref.shape[1]):
            z_ref[tm:, c * LANES:(c + 1) * LANES] = zs_ref[:, c, :].astype(BF16)

    c0 = 0
    for ref in (za_ref, zh_ref, zm_ref):
        for c in range(ref.shape[0]):
            z_ref[:tm, (c0 + c) * LANES:(c0 + c + 1) * LANES] = ref[c].astype(BF16)
        c0 += ref.shape[0]

    @pl.when(first)
    def _():
        y = jnp.dot(z_ref[...], wb_ref[...], preferred_element_type=F32)
        o_ref[...] = _layer_norm_residual(x_ref[...], y[:tm], lg_ref[...], lb_ref[...])
        os_ref[...] = _layer_norm_residual(xs_ref[...], y[tm:], lg_ref[...], lb_ref[...])

    @pl.when(jnp.logical_not(first))
    def _():
        y = jnp.dot(z_ref[:tm, :], wb_ref[...], preferred_element_type=F32)
        o_ref[...] = _layer_norm_residual(x_ref[...], y, lg_ref[...], lb_ref[...])


def _merge(za, zh, zm, x, zs, xs, w_out, ln_g, ln_b, tm):
    m = x.shape[0]
    ms = xs.shape[0]

    def slab(a):
        return pl.BlockSpec((a.shape[0], tm, LANES), lambda i: (0, i, 0))

    const = lambda shape: pl.BlockSpec(shape, lambda i: (0,) * len(shape))
    return pl.pallas_call(
        _merge_kernel,
        grid=(m // tm,),
        in_specs=[slab(za), slab(zh), slab(zm), pl.BlockSpec((tm, D_MODEL), lambda i: (i, 0)),
                  const(zs.shape), const((ms, D_MODEL)), pl.BlockSpec(memory_space=pl.ANY),
                  const((1, D_MODEL)), const((1, D_MODEL))],
        out_specs=[pl.BlockSpec((tm, D_MODEL), lambda i: (i, 0)), const((ms, D_MODEL))],
        out_shape=[jax.ShapeDtypeStruct((m, D_MODEL), F32), jax.ShapeDtypeStruct((ms, D_MODEL), F32)],
        scratch_shapes=[pltpu.VMEM((MIX_WIDTH, D_MODEL), BF16), pltpu.VMEM((tm + ms, MIX_WIDTH), BF16),
                        pltpu.VMEM((2, MERGE_CAST_ROWS, D_MODEL), F32), pltpu.SemaphoreType.DMA((2,))],
        compiler_params=_params(("arbitrary",)),
        name="merge",
    )(za, zh, zm, x, zs, xs, w_out, ln_g, ln_b)


def _column(row):
    return jnp.broadcast_to(row, (HEAD_DIM, HEAD_DIM)).T


HBM_TILE_ROWS = 8
SAMPLE_ROWS = 2


def _window_pieces(past):
    pieces = []
    for dil in DILATIONS:
        if dil == 1:
            pieces.append((0, past - BAND, BAND, None))
        elif dil < HBM_TILE_ROWS:
            n = BAND * dil // HBM_TILE_ROWS
            pieces += [(1, past // HBM_TILE_ROWS - n, n, r) for r in range(0, HBM_TILE_ROWS, dil)]
        else:
            pieces.append((2, past // dil - BAND, BAND, 0))
    return pieces


def _cache_views(cache):
    depth, bsz, past, nh, hd = cache.shape
    assert all(d == 1 or HBM_TILE_ROWS % d == 0 or d % HBM_TILE_ROWS == 0 for d in DILATIONS)
    big = max(DILATIONS)
    hm = cache.transpose(0, 1, 3, 2, 4)
    return (hm, hm.reshape(depth, bsz, nh, past // HBM_TILE_ROWS, HBM_TILE_ROWS, hd),
            hm.reshape(depth, bsz, nh, past // big, big, hd))


def _sample_row(layer, p_ref, cos, sin, k_win, v_win, st_ref, mk_ref, mv_ref, lb_all, ng_ref, z_ref, ko_ref, vo_ref,
                so_ref):
    q_all = _rope(p_ref[G_QA:G_QA + N_HEADS_ATTN, :], cos, sin) * ATTN_SCALE
    k_all = _rope(p_ref[G_KA:G_KA + N_HEADS_ATTN, :], cos, sin)
    v_all = p_ref[G_VA:G_VA + N_HEADS_ATTN, :]
    ko_ref[...] = k_all
    vo_ref[...] = v_all

    for h in range(N_HEADS_ATTN):
        q = q_all[h:h + 1]
        s_new = jnp.sum(q * k_all[h:h + 1], axis=-1, keepdims=True)
        s = jnp.sum(k_win[h] * q, axis=-1, keepdims=True)
        m = jnp.maximum(jnp.max(s, axis=0, keepdims=True), s_new)
        p = jnp.exp(s - m)
        p_new = jnp.exp(s_new - m) * len(DILATIONS)
        den = jnp.sum(p, axis=0, keepdims=True) + p_new
        num = jnp.sum(p * v_win[h], axis=0, keepdims=True) + p_new * v_all[h:h + 1]
        z_ref[h:h + 1, :] = (num / den) * _silu(p_ref[G_GA + h:G_GA + h + 1, :])

    for h in range(N_HEADS_HGRN):
        lanes = slice(h * HEAD_DIM, (h + 1) * HEAD_DIM)
        lb = _lower_bound(lb_all[:, lanes], layer)
        f = lb + (1.0 - lb) * _sigmoid(p_ref[G_FB + h:G_FB + h + 1, :])
        f_col = _column(f)
        q_col = _column(p_ref[G_QB + h:G_QB + h + 1, :])
        s_new = f_col * st_ref[h] + (1.0 - f_col) * p_ref[G_IB + h:G_IB + h + 1, :]
        so_ref[h] = s_new
        o = jnp.sum(s_new * q_col, axis=0, keepdims=True)
        z_ref[N_HEADS_ATTN + h:N_HEADS_ATTN + h + 1, :] = _rms_gate(
            o, ng_ref[:, lanes], p_ref[G_GB + h:G_GB + h + 1, :])

    fold = mk_ref.shape[1] // N_HEADS_MEM
    unfold = lambda a: sum(a[i * N_HEADS_MEM:(i + 1) * N_HEADS_MEM] for i in range(fold))
    q_m = p_ref[G_QM:G_QM + N_HEADS_MEM, :] * ATTN_SCALE
    s = jnp.sum(mk_ref[...] * jnp.concatenate([q_m] * fold, axis=0)[None], axis=-1, keepdims=True)
    m = jnp.max(s, axis=0)
    m = functools.reduce(jnp.maximum, [m[i * N_HEADS_MEM:(i + 1) * N_HEADS_MEM] for i in range(fold)])
    p = jnp.exp(s - jnp.concatenate([m] * fold, axis=0)[None])
    o = unfold(jnp.sum(p * mv_ref[...], axis=0)) / unfold(jnp.sum(p, axis=0))
    row = N_HEADS_ATTN + N_HEADS_HGRN
    z_ref[row:row + N_HEADS_MEM, :] = o * _silu(p_ref[G_GM:G_GM + N_HEADS_MEM, :])


def _sample_kernel(layer, past, p_ref, cos_ref, sin_ref, k0_hbm, k1_hbm, k2_hbm, v0_hbm, v1_hbm, v2_hbm,
                   st_ref, mk_ref, mv_ref, lb_ref, ng_ref, z_ref, ko_ref, vo_ref, so_ref, kbuf, vbuf, sem):
    step = pl.program_id(0)
    slot = step % 2
    n_rows = p_ref.shape[0]
    pieces = _window_pieces(past)

    def window_copies(stp, sl):
        out = []
        for rr in range(n_rows):
            for ci, (views, buf) in enumerate((((k0_hbm, k1_hbm, k2_hbm), kbuf), ((v0_hbm, v1_hbm, v2_hbm), vbuf))):
                off = 0
                for pi, (vi, start, count, res) in enumerate(pieces):
                    view, row = views[vi], stp * n_rows + rr
                    src = (view.at[layer, row, :, pl.ds(start, count), :] if res is None
                           else view.at[layer, row, :, pl.ds(start, count), res, :])
                    out.append(pltpu.make_async_copy(src, buf.at[sl, rr, :, pl.ds(off, count), :],
                                                     sem.at[sl, rr, ci, pi]))
                    off += count
        return out

    @pl.when(step == 0)
    def _():
        for c in window_copies(step, slot):
            c.start()

    @pl.when(step + 1 < pl.num_programs(0))
    def _():
        for c in window_copies(step + 1, 1 - slot):
            c.start()

    for c in window_copies(step, slot):
        c.wait()

    cos, sin, lb_all = cos_ref[...], sin_ref[...], lb_ref[...]
    for rr in range(n_rows):
        _sample_row(layer, p_ref.at[rr], cos, sin, kbuf.at[slot, rr], vbuf.at[slot, rr], st_ref.at[rr], mk_ref.at[rr],
                    mv_ref.at[rr], lb_all, ng_ref, z_ref.at[rr], ko_ref.at[rr], vo_ref.at[rr], so_ref.at[rr])


def _sample_mixers(proj_rows, cos, sin, win_k, win_v, state, mem_k, mem_v, lb_raw, norm_g, layer):
    bsz, past = win_k.shape[1:3]
    rps = SAMPLE_ROWS
    assert all(past % d == 0 and past >= w for w, d in zip(WINDOWS, DILATIONS)) and bsz % rps == 0
    assert all(w // d == BAND for w, d in zip(WINDOWS, DILATIONS)) and past % HBM_TILE_ROWS == 0
    n_keys = BAND * len(DILATIONS)
    n_pieces = len(_window_pieces(past))
    any_spec = pl.BlockSpec(memory_space=pl.ANY)
    vec = lambda n: pl.BlockSpec((n, W_HGRN), lambda b: (0, 0))
    heads = lambda n: pl.BlockSpec((rps, n, HEAD_DIM), lambda b: (b, 0, 0))
    fold = HBM_TILE_ROWS // N_HEADS_MEM
    mem_k, mem_v = (a.reshape(a.shape[0], bsz, N_MEM // fold, fold * N_HEADS_MEM, HEAD_DIM) for a in (mem_k, mem_v))
    mem_spec = pl.BlockSpec((None, rps, N_MEM // fold, fold * N_HEADS_MEM, HEAD_DIM), lambda b: (layer, b, 0, 0, 0))
    return pl.pallas_call(
        functools.partial(_sample_kernel, layer, past),
        grid=(bsz // rps,),
        in_specs=[heads(N_GROUPS), pl.BlockSpec((1, HEAD_DIM), lambda b: (0, 0)),
                  pl.BlockSpec((1, HEAD_DIM), lambda b: (0, 0)), *([any_spec] * 6),
                  pl.BlockSpec((None, rps, N_HEADS_HGRN, HEAD_DIM, HEAD_DIM), lambda b: (layer, b, 0, 0, 0)),
                  mem_spec, mem_spec, vec(DEPTH + 1), vec(1)],
        out_specs=[heads(MIX_WIDTH // HEAD_DIM), heads(N_HEADS_ATTN), heads(N_HEADS_ATTN),
                   pl.BlockSpec((rps, N_HEADS_HGRN, HEAD_DIM, HEAD_DIM), lambda b: (b, 0, 0, 0))],
        out_shape=[jax.ShapeDtypeStruct((bsz, MIX_WIDTH // HEAD_DIM, HEAD_DIM), F32),
                   jax.ShapeDtypeStruct((bsz, N_HEADS_ATTN, HEAD_DIM), F32),
                   jax.ShapeDtypeStruct((bsz, N_HEADS_ATTN, HEAD_DIM), F32),
                   jax.ShapeDtypeStruct((bsz, N_HEADS_HGRN, HEAD_DIM, HEAD_DIM), F32)],
        scratch_shapes=[pltpu.VMEM((2, rps, N_HEADS_ATTN, n_keys, HEAD_DIM), F32),
                        pltpu.VMEM((2, rps, N_HEADS_ATTN, n_keys, HEAD_DIM), F32),
                        pltpu.SemaphoreType.DMA((2, rps, 2, n_pieces))],
        compiler_params=_params(("arbitrary",)),
        name="sample_mixers",
    )(proj_rows, cos, sin, *_cache_views(win_k), *_cache_views(win_v), state, mem_k, mem_v, lb_raw, norm_g)


def _rope_tables(pos):
    half = HEAD_DIM // 2
    inv_freq = 1.0 / (ROPE_THETA ** (np.arange(half, dtype=np.float64) / half))
    ang = np.asarray(pos, np.float64)[:, None] * inv_freq[None, :]
    cos, sin = np.cos(ang), np.sin(ang)
    return (jnp.asarray(np.concatenate([cos, cos], axis=-1), F32),
            jnp.asarray(np.concatenate([-sin, sin], axis=-1), F32))


def kernel(x_prompt, x_sample, cache_win_k, cache_win_v, state_hgrn, cache_mem_k, cache_mem_v, mem_prompt,
           w_in, w_mem_kv, hgrn_lb_raw, hgrn_norm_g, w_out, ln_g, ln_b):
    bsz, seq, _ = x_prompt.shape
    dbsz, n_new, _ = x_sample.shape
    assert n_new == 1 and seq % (BAND * max(DILATIONS)) == 0
    cos_p, sin_p = _rope_tables(np.arange(seq))
    cos_s, sin_s = _rope_tables(PAST_LEN + np.arange(n_new))

    hp = x_prompt.reshape(bsz * seq, D_MODEL)
    hs = x_sample.reshape(dbsz * n_new, D_MODEL)
    mem = mem_prompt.reshape(bsz * N_MEM, D_MODEL)
    outs = [[] for _ in range(8)]
    for layer in range(DEPTH):
        lb_raw = hgrn_lb_raw
        norm_g = hgrn_norm_g[layer][None]
        lg, lbias = ln_g[layer][None], ln_b[layer][None]

        proj, proj_s = _project(hp, hs, w_in[layer], PROJ_ROWS, PROJ_FIRST_COLS, PROJ_COLS)
        za, k1, v1 = _prompt_attention(proj, cos_p, sin_p, bsz, seq)
        zh, s1 = _prompt_hgrn(proj, lb_raw, norm_g, layer, bsz, seq)
        zm, mk1, mv1 = _prompt_mem(proj, mem, w_mem_kv[layer], bsz, seq)
        zs, k2, v2, s2 = _sample_mixers(proj_s.transpose(1, 0, 2), cos_s, sin_s, cache_win_k, cache_win_v, state_hgrn,
                                        cache_mem_k, cache_mem_v, lb_raw, norm_g, layer)
        hp, hs = _merge(za, zh, zm, hp, zs, hs, w_out[layer], lg, lbias, MERGE_ROWS)

        new = (k1.transpose(0, 2, 1, 3), v1.transpose(0, 2, 1, 3), s1, mk1, mv1,
               k2.reshape(dbsz, n_new, N_HEADS_ATTN, HEAD_DIM), v2.reshape(dbsz, n_new, N_HEADS_ATTN, HEAD_DIM),
               s2.astype(state_hgrn.dtype))
        for acc, val in zip(outs, new):
            acc.append(val)

    return (hp.reshape(bsz, seq, D_MODEL), hs.reshape(dbsz, n_new, D_MODEL), *[jnp.stack(o) for o in outs])
```

```python
import functools

import jax
import jax.numpy as jnp
import numpy as np
from jax import lax
from jax.experimental import pallas as pl
from jax.experimental.pallas import tpu as pltpu

F32 = jnp.float32
BF16 = jnp.bfloat16

D_MODEL = 2048
DEPTH = 1
PAST_LEN = 8192
HEAD_DIM = 128
N_HEADS_ATTN = 6
N_HEADS_HGRN = 6
N_HEADS_MEM = 4
W_ATTN = N_HEADS_ATTN * HEAD_DIM
W_HGRN = N_HEADS_HGRN * HEAD_DIM
W_MEM = N_HEADS_MEM * HEAD_DIM
MIX_WIDTH = W_ATTN + W_HGRN + W_MEM
WINDOWS = (128, 512, 2048)
DILATIONS = (1, 4, 16)
N_MEM = 256
ROPE_THETA = 10000.0
LN_EPS = 1e-5
RMS_EPS = 1e-6
NEG_INF = -1e30
DEEPNORM_ALPHA = (2 * DEPTH) ** 0.25
ATTN_SCALE = HEAD_DIM ** -0.5
ATTN_SCALE_LOG2 = ATTN_SCALE * 1.4426950408889634

G_QA, G_KA, G_VA, G_GA = 0, 6, 12, 18
G_QB, G_FB, G_IB, G_GB = 24, 30, 36, 42
G_QM, G_GM = 48, 52
N_GROUPS = 56

LANES = 128
BAND = 128
ATTN_GROUP = 16
HGRN_CHUNK = 64
HGRN_SUB = 16
HGRN_HEADS_PER_STEP = 3
HGRN_PREP_ROWS = 256
HGRN_SCAN_CHUNKS = 16
HGRN_SAFE_LOG_RANGE = 70.0
VMEM_LIMIT = 48 * 1024 * 1024
PROJ_ROWS, PROJ_COLS, PROJ_FIRST_COLS = 1024, 1024, 512
MERGE_ROWS = 512
MERGE_CAST_ROWS = 256


def _params(sem, vmem=VMEM_LIMIT):
    return pltpu.CompilerParams(dimension_semantics=sem, vmem_limit_bytes=vmem)


def _sigmoid(x):
    return 1.0 / (1.0 + jnp.exp(-x))


def _silu(x):
    return x * _sigmoid(x)


def _dot_nt(a, b):
    return lax.dot_general(a, b, (((1,), (1,)), ((), ())), preferred_element_type=F32)


def _store_slabs(ref, acc):
    for c in range(ref.shape[0]):
        ref[c] = acc[:, c * LANES:(c + 1) * LANES]


def _proj_first_kernel(x_ref, xs_ref, w_ref, o_ref, os_ref, wb_ref, xb_ref):
    tm = x_ref.shape[0]

    @pl.when(pl.program_id(0) == 0)
    def _():
        xb_ref[:tm, :] = x_ref[...].astype(BF16)
        xb_ref[tm:, :] = xs_ref[...].astype(BF16)

    wb_ref[...] = w_ref[...].astype(BF16)
    acc = jnp.dot(xb_ref[...], wb_ref[...], preferred_element_type=F32)
    _store_slabs(o_ref, acc[:tm])
    _store_slabs(os_ref, acc[tm:])


def _proj_rest_kernel(x_ref, w_ref, first_ref, o_hbm, xb_ref, obuf, sem_out, sem_first):
    i, j = pl.program_id(0), pl.program_id(1)
    n_col = pl.num_programs(1)
    step = i * n_col + j
    last = pl.num_programs(0) * n_col - 1
    slot = step % 2
    tm = x_ref.shape[0]
    n_slab = obuf.shape[1]

    def tile_copy(stp):
        row_tile, col_tile = stp // n_col + 1, stp % n_col
        return pltpu.make_async_copy(
            obuf.at[stp % 2], o_hbm.at[pl.ds(col_tile * n_slab, n_slab), pl.ds(row_tile * tm, tm), :],
            sem_out.at[stp % 2])

    first_copy = pltpu.make_async_copy(first_ref, o_hbm.at[pl.ds(j * n_slab, n_slab), pl.ds(0, tm), :], sem_first.at[0])

    @pl.when(i == 0)
    def _():
        first_copy.start()

    @pl.when(j == 0)
    def _():
        xb_ref[...] = x_ref[...].astype(BF16)

    @pl.when(step >= 2)
    def _():
        tile_copy(step - 2).wait()

    _store_slabs(obuf.at[slot], jnp.dot(xb_ref[...], w_ref[...], preferred_element_type=F32))
    tile_copy(step).start()

    @pl.when(i == 0)
    def _():
        first_copy.wait()

    @pl.when(step == last)
    def _():
        tile_copy(step - 1).wait()
        tile_copy(step).wait()


def _project(x, xs, w, tm, tn_first, tn):
    m, k = x.shape
    ms = xs.shape[0]
    n = w.shape[1]
    assert (m // tm - 1) * (n // tn) >= 2
    proj_first, proj_s, w_bf16 = pl.pallas_call(
        _proj_first_kernel,
        grid=(n // tn_first,),
        in_specs=[pl.BlockSpec((tm, k), lambda j: (0, 0)),
                  pl.BlockSpec((ms, k), lambda j: (0, 0)),
                  pl.BlockSpec((k, tn_first), lambda j: (0, j))],
        out_specs=[pl.BlockSpec((tn_first // LANES, tm, LANES), lambda j: (j, 0, 0)),
                   pl.BlockSpec((tn_first // LANES, ms, LANES), lambda j: (j, 0, 0)),
                   pl.BlockSpec((k, tn_first), lambda j: (0, j))],
        out_shape=[jax.ShapeDtypeStruct((n // LANES, tm, LANES), F32),
                   jax.ShapeDtypeStruct((n // LANES, ms, LANES), F32),
                   jax.ShapeDtypeStruct((k, n), BF16)],
        scratch_shapes=[pltpu.VMEM((tm + ms, k), BF16)],
        compiler_params=_params(("arbitrary",)),
        name="in_proj_first",
    )(x, xs, w)
    proj = pl.pallas_call(
        _proj_rest_kernel,
        grid=(m // tm - 1, n // tn),
        in_specs=[pl.BlockSpec((tm, k), lambda i, j: (i + 1, 0)),
                  pl.BlockSpec((k, tn), lambda i, j: (0, j)),
                  pl.BlockSpec((tn // LANES, tm, LANES), lambda i, j: (jnp.where(i == 0, j, n // tn - 1), 0, 0))],
        out_specs=pl.BlockSpec(memory_space=pl.ANY),
        out_shape=jax.ShapeDtypeStruct((n // LANES, m, LANES), F32),
        scratch_shapes=[pltpu.VMEM((tm, k), BF16), pltpu.VMEM((2, tn // LANES, tm, LANES), F32),
                        pltpu.SemaphoreType.DMA((2,)), pltpu.SemaphoreType.DMA((1,))],
        compiler_params=_params(("arbitrary", "arbitrary")),
        name="in_proj",
    )(x, w_bf16, proj_first)
    return proj, proj_s


def _rope(x, cos, sin_signed):
    return x * cos + pltpu.roll(x, HEAD_DIM // 2, 1) * sin_signed


def _bias_from_count(count):
    return jnp.where(count > 1.5, 1.0, jnp.where(count > 0.5, 0.0, NEG_INF)).astype(F32)


def _block_deltas(block_gap):
    qi = lax.broadcasted_iota(jnp.int32, (BAND, BAND), 0)
    ki = lax.broadcasted_iota(jnp.int32, (BAND, BAND), 1)
    return block_gap * BAND + qi - ki


def _near_bias(block_gap):
    d = _block_deltas(block_gap)
    return _bias_from_count(((d >= 0) & (d <= BAND)).astype(F32))


def _class_bias(block_gap, far_step):
    d = _block_deltas(block_gap)
    near = (d >= 0) & (d <= BAND)
    far = (d >= 0) & ((d & (far_step - 1)) == 0)
    return _bias_from_count(near.astype(F32) + far.astype(F32))


def _softmax_stage(scores, floors=None):
    out = []
    for idx, s in enumerate(scores):
        m = jnp.broadcast_to(jnp.max(s, axis=-1, keepdims=True), (BAND, HEAD_DIM))
        if floors is not None:
            m = jnp.maximum(m, floors[idx])
        m_wide = jnp.concatenate([m] * (s.shape[1] // HEAD_DIM), axis=1)
        out.append((m, jnp.exp2(s - m_wide).astype(BF16)))
    return out


def _attn_kernel(q_ref, k_ref, v_ref, g_ref, cos_ref, sin_ref, z_ref, ko_ref, vo_ref,
                 qs_ref, qb_ref, kb_ref, vb_ref, qc_ref, kc_ref, vc_ref, acc_ref, m_ref, l_ref):
    seq = q_ref.shape[0]
    d_mid, d_far = DILATIONS[1], DILATIONS[2]
    far_step = d_far // d_mid
    cls_rows = seq // d_mid
    cos = cos_ref[...]
    sin = sin_ref[...]
    qs_ref[...] = _rope(q_ref[...], cos, sin) * ATTN_SCALE_LOG2
    ko_ref[...] = _rope(k_ref[...], cos, sin)
    vo_ref[...] = v_ref[...]
    qb_ref[...] = qs_ref[...].astype(BF16)
    kb_ref[...] = ko_ref[...].astype(BF16)
    ones = jnp.ones((seq, HEAD_DIM), BF16)
    vb_ref[:, :HEAD_DIM] = v_ref[...].astype(BF16)
    vb_ref[:, HEAD_DIM:] = ones
    vc_ref[:, HEAD_DIM:] = ones
    for r in range(d_mid):
        cls, dst = pl.ds(r, cls_rows, stride=d_mid), pl.ds(r * cls_rows, cls_rows)
        qc_ref[dst, :] = qs_ref[cls, :].astype(BF16)
        kc_ref[dst, :] = ko_ref[cls, :].astype(BF16)
        vc_ref[dst, :HEAD_DIM] = v_ref[cls, :].astype(BF16)

    n_cls_blk = cls_rows // BAND
    cls_bias = [_class_bias(gap, far_step) for gap in range(min(n_cls_blk, 3))]
    items = [(r, n) for r in range(d_mid) for n in range(n_cls_blk)]
    for g0 in range(0, len(items), ATTN_GROUP):
        group = items[g0:g0 + ATTN_GROUP]
        keys = [pl.ds(r * cls_rows, (n + 1) * BAND) for r, n in group]
        scores = [_dot_nt(qc_ref[pl.ds(r * cls_rows + n * BAND, BAND), :], kc_ref[kr, :])
                  + jnp.concatenate([cls_bias[min(n - nk, 2)] for nk in range(n + 1)], axis=1)
                  for (r, n), kr in zip(group, keys)]
        for (r, n), kr, (m, p) in zip(group, keys, _softmax_stage(scores)):
            rows = pl.ds(r + d_mid * BAND * n, BAND, stride=d_mid)
            pv = jnp.dot(p, vc_ref[kr, :], preferred_element_type=F32)
            acc_ref[rows, :] = pv[:, :HEAD_DIM]
            l_ref[rows, :] = pv[:, HEAD_DIM:]
            m_ref[rows, :] = m

    near_bias = [_near_bias(0), jnp.concatenate([_near_bias(1), _near_bias(0)], axis=1)]
    items = list(range(seq // BAND))
    for g0 in range(0, len(items), ATTN_GROUP):
        group = items[g0:g0 + ATTN_GROUP]
        rows = [pl.ds(n * BAND, BAND) for n in group]
        keys = [pl.ds(max(n - 1, 0) * BAND, BAND * min(n + 1, 2)) for n in group]
        scores = [_dot_nt(qb_ref[rw, :], kb_ref[kr, :]) + near_bias[min(n, 1)]
                  for n, rw, kr in zip(group, rows, keys)]
        probs = _softmax_stage(scores, floors=[m_ref[rw, :] for rw in rows])
        for rw, kr, (m, p) in zip(rows, keys, probs):
            w = jnp.exp2(m_ref[rw, :] - m)
            pv = jnp.dot(p, vb_ref[kr, :], preferred_element_type=F32)
            num = pv[:, :HEAD_DIM] + w * acc_ref[rw, :]
            den = pv[:, HEAD_DIM:] + w * l_ref[rw, :]
            z_ref[rw, :] = ((num / den) * _silu(g_ref[rw, :])).astype(z_ref.dtype)


def _prompt_attention(proj, cos, sin, bsz, seq):
    def col(g0):
        return pl.BlockSpec((None, seq, HEAD_DIM), lambda b, h: (g0 + h, b, 0))

    table = pl.BlockSpec((seq, HEAD_DIM), lambda b, h: (0, 0))
    kv_out = pl.BlockSpec((None, None, seq, HEAD_DIM), lambda b, h: (b, h, 0, 0))
    d_near, d_mid, d_far = DILATIONS
    far_step = d_far // d_mid
    assert d_near == 1 and d_far % d_mid == 0 and far_step & (far_step - 1) == 0
    assert all(w // d == BAND for w, d in zip(WINDOWS, DILATIONS)) and seq % (d_mid * BAND) == 0
    assert seq // d_mid <= far_step * BAND
    return pl.pallas_call(
        _attn_kernel,
        grid=(bsz, N_HEADS_ATTN),
        in_specs=[col(G_QA), col(G_KA), col(G_VA), col(G_GA), table, table],
        out_specs=[pl.BlockSpec((None, seq, HEAD_DIM), lambda b, h: (h, b, 0)), kv_out, kv_out],
        out_shape=[jax.ShapeDtypeStruct((N_HEADS_ATTN, bsz * seq, HEAD_DIM), BF16),
                   jax.ShapeDtypeStruct((bsz, N_HEADS_ATTN, seq, HEAD_DIM), F32),
                   jax.ShapeDtypeStruct((bsz, N_HEADS_ATTN, seq, HEAD_DIM), F32)],
        scratch_shapes=[pltpu.VMEM((seq, HEAD_DIM), F32)]
                       + [pltpu.VMEM((seq, HEAD_DIM), BF16), pltpu.VMEM((seq, HEAD_DIM), BF16),
                          pltpu.VMEM((seq, 2 * HEAD_DIM), BF16)] * 2
                       + [pltpu.VMEM((seq, HEAD_DIM), F32)] * 3,
        compiler_params=_params(("parallel", "parallel")),
        name="prompt_attn",
    )(proj, proj, proj, proj, cos, sin)


def _lower_bound(lb_raw, layer):
    e = jnp.exp(lb_raw - jnp.max(lb_raw, axis=0, keepdims=True))
    sm = e / jnp.sum(e, axis=0, keepdims=True)
    return jnp.sum(sm[:layer + 1], axis=0, keepdims=True)


def _split2(x):
    hi = x.astype(BF16)
    return hi, (x - hi.astype(F32)).astype(BF16)


def _rms_gate(o, norm_g, gate):
    o = o * lax.rsqrt(jnp.mean(o * o, axis=-1, keepdims=True) + RMS_EPS)
    return o * norm_g * _silu(gate)


def _hgrn_gates(fb, lb, tril):
    f = lb + (1.0 - lb) * _sigmoid(fb)
    g = jnp.log(f)
    b = sum(jnp.dot(tril, piece, preferred_element_type=F32) for piece in _split2(g))
    return 1.0 - f, b


def _hgrn_fast_prepare(q_ref, f_ref, lb, qh_ref, kh_ref, el_ref):
    n_heads, seq, _ = q_ref.shape
    blk = HGRN_PREP_ROWS
    tril = _tril_ones(blk)

    def body(bi, carry):
        r0 = pl.multiple_of(bi * blk, blk)
        fb = jnp.concatenate([f_ref[h, pl.ds(r0, blk), :] for h in range(n_heads)], axis=1)
        f = lb + (1.0 - lb) * _sigmoid(fb)
        b_blk = sum(jnp.dot(tril, piece, preferred_element_type=F32) for piece in _split2(jnp.log(f)))
        kk = 1.0 - f
        for c0 in range(0, blk, HGRN_CHUNK):
            rs = slice(c0, c0 + HGRN_CHUNK)
            b = b_blk[rs] - b_blk[c0 - 1:c0] if c0 else b_blk[rs]
            rows = pl.ds(r0 + c0, HGRN_CHUNK)
            q = jnp.concatenate([q_ref[h, rows, :] for h in range(n_heads)], axis=1)
            qh_ref[rows, :] = (q * jnp.exp(b)).astype(BF16)
            kh_ref[rows, :] = (kk[rs] * jnp.exp(-b)).astype(BF16)
            el_ref[pl.ds(bi * (blk // HGRN_CHUNK) + c0 // HGRN_CHUNK, 1), :] = jnp.exp(b[HGRN_CHUNK - 1:])
        return carry

    lax.fori_loop(0, seq // blk, body, 0, unroll=4)


def _hgrn_fast_scan(i_ref, g_ref, norm_g, qh_ref, kh_ref, el_ref, st_ref, z_ref):
    n_heads, seq, _ = i_ref.shape
    c = HGRN_CHUNK
    heads = range(n_heads)
    chunks = range(HGRN_SCAN_CHUNKS)
    lanes = [slice(h * HEAD_DIM, (h + 1) * HEAD_DIM) for h in heads]
    causal = lax.broadcasted_iota(jnp.int32, (c, c), 0) >= lax.broadcasted_iota(jnp.int32, (c, c), 1)
    tn = (((0,), (0,)), ((), ()))

    def body(ti, carry):
        rows = [pl.ds(pl.multiple_of((ti * HGRN_SCAN_CHUNKS + k) * c, c), c) for k in chunks]
        qh = [[qh_ref[rows[k], lanes[h]] for h in heads] for k in chunks]
        kh = [[kh_ref[rows[k], lanes[h]] for h in heads] for k in chunks]
        vb = [[i_ref[h, rows[k], :].astype(BF16) for h in heads] for k in chunks]
        att = [[jnp.where(causal, _dot_nt(qh[k][h], kh[k][h]), 0.0).astype(BF16) for h in heads] for k in chunks]
        ds = [[lax.dot_general(vb[k][h], kh[k][h], tn, preferred_element_type=F32) for h in heads] for k in chunks]
        st = [st_ref[h] for h in heads]
        o = []
        for k in chunks:
            el = el_ref[pl.ds(ti * HGRN_SCAN_CHUNKS + k, 1), :]
            o.append([_dot_nt(qh[k][h], st[h].astype(BF16)) for h in heads])
            st = [(st[h] + ds[k][h]) * el[:, lanes[h]] for h in heads]
        for h in heads:
            st_ref[h] = st[h]
        for k in chunks:
            for h in heads:
                o_kh = o[k][h] + jnp.dot(att[k][h], vb[k][h], preferred_element_type=F32)
                z_ref[h, rows[k], :] = _rms_gate(o_kh, norm_g[:, lanes[h]], g_ref[h, rows[k], :]).astype(z_ref.dtype)
        return carry

    lax.fori_loop(0, seq // (c * HGRN_SCAN_CHUNKS), body, 0)


def _hgrn_chunk(q, fb, v, lb, st, tril):
    c = q.shape[0]
    kk, b = _hgrn_gates(fb, lb, tril)
    o = _dot_nt((q * jnp.exp(b)).astype(BF16), st.astype(BF16))

    s_idx = lax.broadcasted_iota(jnp.int32, (c, 1), 0)
    lane = lax.broadcasted_iota(jnp.int32, (HGRN_SUB, c), 1)
    row = lax.broadcasted_iota(jnp.int32, (HGRN_SUB, c), 0)
    att_rows = []
    for i0 in range(0, c, HGRN_SUB):
        qi = q[i0:i0 + HGRN_SUB]
        bi = b[i0:i0 + HGRN_SUB]
        if i0 > 0:
            bref = b[i0 - 1:i0]
            kt = jnp.where(s_idx < i0, kk * jnp.exp(jnp.minimum(bref - b, 0.0)), 0.0)
            att = _dot_nt((qi * jnp.exp(bi - bref)).astype(BF16), kt.astype(BF16))
        else:
            att = jnp.zeros((HGRN_SUB, c), F32)
        for j in range(HGRN_SUB):
            s = i0 + j
            e = jnp.exp(jnp.minimum(bi - b[s:s + 1], 0.0))
            colv = jnp.sum(qi * kk[s:s + 1] * e, axis=-1, keepdims=True)
            att = jnp.where((lane == s) & (row >= j), colv, att)
        att_rows.append(att)
    att = jnp.concatenate(att_rows, axis=0)
    vb = v.astype(BF16)
    o = o + jnp.dot(att.astype(BF16), vb, preferred_element_type=F32)

    b_last = b[c - 1:c]
    kd = (kk * jnp.exp(b_last - b)).astype(BF16)
    st_new = st * jnp.exp(b_last) + lax.dot_general(vb, kd, (((0,), (0,)), ((), ())),
                                                    preferred_element_type=F32)
    return o, st_new


def _tril_ones(c):
    return (lax.broadcasted_iota(jnp.int32, (c, c), 0) >= lax.broadcasted_iota(jnp.int32, (c, c), 1)
            ).astype(BF16)


def _hgrn_kernel(layer, q_ref, f_ref, i_ref, g_ref, lb_ref, ng_ref, z_ref, s_ref, st_ref, qh_ref, kh_ref, el_ref):
    n_heads, seq, _ = q_ref.shape
    lb = _lower_bound(lb_ref[...], layer)
    norm_g = ng_ref[...]
    st_ref[...] = jnp.zeros_like(st_ref)

    def fast():
        _hgrn_fast_prepare(q_ref, f_ref, lb, qh_ref, kh_ref, el_ref)
        _hgrn_fast_scan(i_ref, g_ref, norm_g, qh_ref, kh_ref, el_ref, st_ref, z_ref)

    def safe():
        tril = _tril_ones(HGRN_CHUNK)

        def body(ci, carry):
            rows = pl.ds(pl.multiple_of(ci * HGRN_CHUNK, HGRN_CHUNK), HGRN_CHUNK)
            for h in range(n_heads):
                lanes = slice(h * HEAD_DIM, (h + 1) * HEAD_DIM)
                o, st = _hgrn_chunk(q_ref[h, rows, :], f_ref[h, rows, :], i_ref[h, rows, :], lb[:, lanes],
                                    st_ref[h], tril)
                st_ref[h] = st
                z_ref[h, rows, :] = _rms_gate(o, norm_g[:, lanes], g_ref[h, rows, :]).astype(z_ref.dtype)
            return carry

        lax.fori_loop(0, seq // HGRN_CHUNK, body, 0)

    fast_ok = HGRN_CHUNK * -jnp.log(jnp.min(lb)) <= HGRN_SAFE_LOG_RANGE
    lax.cond(fast_ok, fast, safe)
    for h in range(n_heads):
        s_ref[h] = st_ref[h].T


def _prompt_hgrn(proj, lb_raw, norm_g, layer, bsz, seq):
    hps = HGRN_HEADS_PER_STEP
    assert N_HEADS_HGRN % hps == 0 and all(g % hps == 0 for g in (G_QB, G_FB, G_IB, G_GB))
    assert seq % (HGRN_CHUNK * HGRN_SCAN_CHUNKS) == 0 and seq % HGRN_PREP_ROWS == 0 and HGRN_PREP_ROWS % HGRN_CHUNK == 0

    def cols(g0):
        return pl.BlockSpec((hps, seq, HEAD_DIM), lambda b, h: (g0 // hps + h, b, 0))

    return pl.pallas_call(
        functools.partial(_hgrn_kernel, layer),
        grid=(bsz, N_HEADS_HGRN // hps),
        in_specs=[cols(G_QB), cols(G_FB), cols(G_IB), cols(G_GB),
                  pl.BlockSpec((DEPTH + 1, hps * HEAD_DIM), lambda b, h: (0, h)),
                  pl.BlockSpec((1, hps * HEAD_DIM), lambda b, h: (0, h))],
        out_specs=[pl.BlockSpec((hps, seq, HEAD_DIM), lambda b, h: (h, b, 0)),
                   pl.BlockSpec((None, hps, HEAD_DIM, HEAD_DIM), lambda b, h: (b, h, 0, 0))],
        out_shape=[jax.ShapeDtypeStruct((N_HEADS_HGRN, bsz * seq, HEAD_DIM), BF16),
                   jax.ShapeDtypeStruct((bsz, N_HEADS_HGRN, HEAD_DIM, HEAD_DIM), F32)],
        scratch_shapes=[pltpu.VMEM((hps, HEAD_DIM, HEAD_DIM), F32),
                        pltpu.VMEM((seq, hps * HEAD_DIM), BF16),
                        pltpu.VMEM((seq, hps * HEAD_DIM), BF16),
                        pltpu.VMEM((seq // HGRN_CHUNK, hps * HEAD_DIM), F32)],
        compiler_params=_params(("parallel", "parallel")),
        name="prompt_hgrn",
    )(proj, proj, proj, proj, lb_raw, norm_g)


MEM_ROWS = 256
MEM_GROUP = 8


def _mem_kernel(q_ref, g_ref, mem_ref, w_ref, z_ref, mk_ref, mv_ref):
    seq = q_ref.shape[1]
    kv = jnp.dot(mem_ref[...].astype(BF16), w_ref[...].astype(BF16), preferred_element_type=F32)
    ones = jnp.ones((N_MEM, HEAD_DIM), BF16)
    mk_b, mv_b = [], []
    for h in range(N_HEADS_MEM):
        mk = kv[:, h * HEAD_DIM:(h + 1) * HEAD_DIM]
        mv = kv[:, W_MEM + h * HEAD_DIM:W_MEM + (h + 1) * HEAD_DIM]
        mk_ref[:, h, :] = mk
        mv_ref[:, h, :] = mv
        mk_b.append(mk.astype(BF16))
        mv_b.append(jnp.concatenate([mv.astype(BF16), ones], axis=1))

    items = [(h, pl.ds(n * MEM_ROWS, MEM_ROWS)) for h in range(N_HEADS_MEM) for n in range(seq // MEM_ROWS)]
    for g0 in range(0, len(items), MEM_GROUP):
        group = items[g0:g0 + MEM_GROUP]
        scores = [_dot_nt((q_ref[h, rows, :] * ATTN_SCALE_LOG2).astype(BF16), mk_b[h]) for h, rows in group]
        probs = [jnp.exp2(s - jnp.max(s, axis=-1, keepdims=True)).astype(BF16) for s in scores]
        for (h, rows), p in zip(group, probs):
            pv = jnp.dot(p, mv_b[h], preferred_element_type=F32)
            o = pv[:, :HEAD_DIM] / pv[:, HEAD_DIM:]
            z_ref[h, rows, :] = (o * _silu(g_ref[h, rows, :])).astype(z_ref.dtype)


def _prompt_mem(proj, mem, w_kv, bsz, seq):
    assert G_QM % N_HEADS_MEM == 0 and G_GM % N_HEADS_MEM == 0

    def cols(g0):
        return pl.BlockSpec((N_HEADS_MEM, seq, HEAD_DIM), lambda b: (g0 // N_HEADS_MEM, b, 0))

    kv_out = pl.BlockSpec((None, N_MEM, N_HEADS_MEM, HEAD_DIM), lambda b: (b, 0, 0, 0))
    kv_shape = jax.ShapeDtypeStruct((bsz, N_MEM, N_HEADS_MEM, HEAD_DIM), F32)
    return pl.pallas_call(
        _mem_kernel,
        grid=(bsz,),
        in_specs=[cols(G_QM), cols(G_GM), pl.BlockSpec((N_MEM, D_MODEL), lambda b: (b, 0)),
                  pl.BlockSpec((D_MODEL, 2 * W_MEM), lambda b: (0, 0))],
        out_specs=[pl.BlockSpec((N_HEADS_MEM, seq, HEAD_DIM), lambda b: (0, b, 0)), kv_out, kv_out],
        out_shape=[jax.ShapeDtypeStruct((N_HEADS_MEM, bsz * seq, HEAD_DIM), BF16), kv_shape, kv_shape],
        compiler_params=_params(("parallel",)),
        name="prompt_mem",
    )(proj, proj, mem, w_kv)


def _merge_kernel(cast_w, za_ref, zh_ref, zm_ref, x_ref, w_ref, lg_ref, lb_ref, o_ref, *rest):
    if cast_w:
        wb_ref, z_ref, stage, sem = rest
        n_rows = stage.shape[1]
        n_chunks = w_ref.shape[0] // n_rows

        def chunk_copy(c):
            return pltpu.make_async_copy(w_ref.at[pl.ds(c * n_rows, n_rows), :], stage.at[c % 2], sem.at[c % 2])

        @pl.when(pl.program_id(0) == 0)
        def _():
            chunk_copy(0).start()
            for c in range(n_chunks):
                if c + 1 < n_chunks:
                    chunk_copy(c + 1).start()
                chunk_copy(c).wait()
                wb_ref[c * n_rows:(c + 1) * n_rows, :] = stage[c % 2].astype(BF16)
    else:
        wb_ref, (z_ref,) = w_ref, rest

    c0 = 0
    for ref in (za_ref, zh_ref, zm_ref):
        for c in range(ref.shape[0]):
            z_ref[:, (c0 + c) * LANES:(c0 + c + 1) * LANES] = ref[c].astype(BF16)
        c0 += ref.shape[0]
    y = jnp.dot(z_ref[...], wb_ref[...], preferred_element_type=F32)
    r = DEEPNORM_ALPHA * x_ref[...] + y
    mu = jnp.mean(r, axis=-1, keepdims=True)
    d = r - mu
    var = jnp.mean(d * d, axis=-1, keepdims=True)
    o_ref[...] = d * lax.rsqrt(var + LN_EPS) * lg_ref[...] + lb_ref[...]


def _merge(za, zh, zm, x, w_out, ln_g, ln_b, tm):
    m = x.shape[0]
    cast_w = w_out.dtype != BF16

    def slab(a):
        return pl.BlockSpec((a.shape[0], tm, LANES), lambda i: (0, i, 0))

    const = lambda shape: pl.BlockSpec(shape, lambda i: (0, 0))
    y_spec, y_shape = pl.BlockSpec((tm, D_MODEL), lambda i: (i, 0)), jax.ShapeDtypeStruct((m, D_MODEL), F32)
    w_block = const((MIX_WIDTH, D_MODEL))
    z_scratch = pltpu.VMEM((tm, MIX_WIDTH), BF16)
    if cast_w:
        w_spec = pl.BlockSpec(memory_space=pl.ANY)
        out_specs, out_shape = [y_spec, w_block], [y_shape, jax.ShapeDtypeStruct((MIX_WIDTH, D_MODEL), BF16)]
        scratch = [z_scratch, pltpu.VMEM((2, MERGE_CAST_ROWS, D_MODEL), F32), pltpu.SemaphoreType.DMA((2,))]
    else:
        w_spec, out_specs, out_shape, scratch = w_block, y_spec, y_shape, [z_scratch]
    return pl.pallas_call(
        functools.partial(_merge_kernel, cast_w),
        grid=(m // tm,),
        in_specs=[slab(za), slab(zh), slab(zm), pl.BlockSpec((tm, D_MODEL), lambda i: (i, 0)),
                  w_spec, const((1, D_MODEL)), const((1, D_MODEL))],
        out_specs=out_specs,
        out_shape=out_shape,
        scratch_shapes=scratch,
        compiler_params=_params(("arbitrary",)),
        name="merge",
    )(za, zh, zm, x, w_out, ln_g, ln_b)


def _column(row):
    return jnp.broadcast_to(row, (HEAD_DIM, HEAD_DIM)).T


HBM_TILE_ROWS = 8
SAMPLE_ROWS = 2


def _window_pieces(past):
    pieces = []
    for dil in DILATIONS:
        if dil == 1:
            pieces.append((0, past - BAND, BAND, None))
        elif dil < HBM_TILE_ROWS:
            n = BAND * dil // HBM_TILE_ROWS
            pieces += [(1, past // HBM_TILE_ROWS - n, n, r) for r in range(0, HBM_TILE_ROWS, dil)]
        else:
            pieces.append((2, past // dil - BAND, BAND, 0))
    return pieces


def _cache_views(cache):
    depth, bsz, past, nh, hd = cache.shape
    assert all(d == 1 or HBM_TILE_ROWS % d == 0 or d % HBM_TILE_ROWS == 0 for d in DILATIONS)
    big = max(DILATIONS)
    hm = cache.transpose(0, 1, 3, 2, 4)
    return (hm, hm.reshape(depth, bsz, nh, past // HBM_TILE_ROWS, HBM_TILE_ROWS, hd),
            hm.reshape(depth, bsz, nh, past // big, big, hd))


def _sample_row(layer, p_ref, cos, sin, k_win, v_win, st_ref, mk_ref, mv_ref, lb_all, ng_ref, z_ref, ko_ref, vo_ref,
                so_ref):
    q_all = _rope(p_ref[G_QA:G_QA + N_HEADS_ATTN, :], cos, sin) * ATTN_SCALE
    k_all = _rope(p_ref[G_KA:G_KA + N_HEADS_ATTN, :], cos, sin)
    v_all = p_ref[G_VA:G_VA + N_HEADS_ATTN, :]
    ko_ref[...] = k_all
    vo_ref[...] = v_all

    for h in range(N_HEADS_ATTN):
        q = q_all[h:h + 1]
        s_new = jnp.sum(q * k_all[h:h + 1], axis=-1, keepdims=True)
        s = jnp.sum(k_win[h] * q, axis=-1, keepdims=True)
        m = jnp.maximum(jnp.max(s, axis=0, keepdims=True), s_new)
        p = jnp.exp(s - m)
        p_new = jnp.exp(s_new - m) * len(DILATIONS)
        den = jnp.sum(p, axis=0, keepdims=True) + p_new
        num = jnp.sum(p * v_win[h], axis=0, keepdims=True) + p_new * v_all[h:h + 1]
        z_ref[h:h + 1, :] = (num / den) * _silu(p_ref[G_GA + h:G_GA + h + 1, :])

    for h in range(N_HEADS_HGRN):
        lanes = slice(h * HEAD_DIM, (h + 1) * HEAD_DIM)
        lb = _lower_bound(lb_all[:, lanes], layer)
        f = lb + (1.0 - lb) * _sigmoid(p_ref[G_FB + h:G_FB + h + 1, :])
        f_col = _column(f)
        q_col = _column(p_ref[G_QB + h:G_QB + h + 1, :])
        s_new = f_col * st_ref[h] + (1.0 - f_col) * p_ref[G_IB + h:G_IB + h + 1, :]
        so_ref[h] = s_new
        o = jnp.sum(s_new * q_col, axis=0, keepdims=True)
        z_ref[N_HEADS_ATTN + h:N_HEADS_ATTN + h + 1, :] = _rms_gate(
            o, ng_ref[:, lanes], p_ref[G_GB + h:G_GB + h + 1, :])

    fold = mk_ref.shape[1] // N_HEADS_MEM
    unfold = lambda a: sum(a[i * N_HEADS_MEM:(i + 1) * N_HEADS_MEM] for i in range(fold))
    q_m = p_ref[G_QM:G_QM + N_HEADS_MEM, :] * ATTN_SCALE
    s = jnp.sum(mk_ref[...] * jnp.concatenate([q_m] * fold, axis=0)[None], axis=-1, keepdims=True)
    m = jnp.max(s, axis=0)
    m = functools.reduce(jnp.maximum, [m[i * N_HEADS_MEM:(i + 1) * N_HEADS_MEM] for i in range(fold)])
    p = jnp.exp(s - jnp.concatenate([m] * fold, axis=0)[None])
    o = unfold(jnp.sum(p * mv_ref[...], axis=0)) / unfold(jnp.sum(p, axis=0))
    row = N_HEADS_ATTN + N_HEADS_HGRN
    z_ref[row:row + N_HEADS_MEM, :] = o * _silu(p_ref[G_GM:G_GM + N_HEADS_MEM, :])


def _sample_kernel(layer, past, p_ref, cos_ref, sin_ref, k0_hbm, k1_hbm, k2_hbm, v0_hbm, v1_hbm, v2_hbm,
                   st_ref, mk_ref, mv_ref, lb_ref, ng_ref, z_ref, ko_ref, vo_ref, so_ref, kbuf, vbuf, sem):
    step = pl.program_id(0)
    slot = step % 2
    n_rows = p_ref.shape[0]
    pieces = _window_pieces(past)

    def window_copies(stp, sl):
        out = []
        for rr in range(n_rows):
            for ci, (views, buf) in enumerate((((k0_hbm, k1_hbm, k2_hbm), kbuf), ((v0_hbm, v1_hbm, v2_hbm), vbuf))):
                off = 0
                for pi, (vi, start, count, res) in enumerate(pieces):
                    view, row = views[vi], stp * n_rows + rr
                    src = (view.at[layer, row, :, pl.ds(start, count), :] if res is None
                           else view.at[layer, row, :, pl.ds(start, count), res, :])
                    out.append(pltpu.make_async_copy(src, buf.at[sl, rr, :, pl.ds(off, count), :],
                                                     sem.at[sl, rr, ci, pi]))
                    off += count
        return out

    @pl.when(step == 0)
    def _():
        for c in window_copies(step, slot):
            c.start()

    @pl.when(step + 1 < pl.num_programs(0))
    def _():
        for c in window_copies(step + 1, 1 - slot):
            c.start()

    for c in window_copies(step, slot):
        c.wait()

    cos, sin, lb_all = cos_ref[...], sin_ref[...], lb_ref[...]
    for rr in range(n_rows):
        _sample_row(layer, p_ref.at[rr], cos, sin, kbuf.at[slot, rr], vbuf.at[slot, rr], st_ref.at[rr], mk_ref.at[rr],
                    mv_ref.at[rr], lb_all, ng_ref, z_ref.at[rr], ko_ref.at[rr], vo_ref.at[rr], so_ref.at[rr])


def _sample_mixers(proj_rows, cos, sin, win_k, win_v, state, mem_k, mem_v, lb_raw, norm_g, layer):
    bsz, past = win_k.shape[1:3]
    rps = SAMPLE_ROWS
    assert all(past % d == 0 and past >= w for w, d in zip(WINDOWS, DILATIONS)) and bsz % rps == 0
    assert all(w // d == BAND for w, d in zip(WINDOWS, DILATIONS)) and past % HBM_TILE_ROWS == 0
    n_keys = BAND * len(DILATIONS)
    n_pieces = len(_window_pieces(past))
    any_spec = pl.BlockSpec(memory_space=pl.ANY)
    vec = lambda n: pl.BlockSpec((n, W_HGRN), lambda b: (0, 0))
    heads = lambda n: pl.BlockSpec((rps, n, HEAD_DIM), lambda b: (b, 0, 0))
    fold = HBM_TILE_ROWS // N_HEADS_MEM
    mem_k, mem_v = (a.reshape(a.shape[0], bsz, N_MEM // fold, fold * N_HEADS_MEM, HEAD_DIM) for a in (mem_k, mem_v))
    mem_spec = pl.BlockSpec((None, rps, N_MEM // fold, fold * N_HEADS_MEM, HEAD_DIM), lambda b: (layer, b, 0, 0, 0))
    return pl.pallas_call(
        functools.partial(_sample_kernel, layer, past),
        grid=(bsz // rps,),
        in_specs=[heads(N_GROUPS), pl.BlockSpec((1, HEAD_DIM), lambda b: (0, 0)),
                  pl.BlockSpec((1, HEAD_DIM), lambda b: (0, 0)), *([any_spec] * 6),
                  pl.BlockSpec((None, rps, N_HEADS_HGRN, HEAD_DIM, HEAD_DIM), lambda b: (layer, b, 0, 0, 0)),
                  mem_spec, mem_spec, vec(DEPTH + 1), vec(1)],
        out_specs=[heads(MIX_WIDTH // HEAD_DIM), heads(N_HEADS_ATTN), heads(N_HEADS_ATTN),
                   pl.BlockSpec((rps, N_HEADS_HGRN, HEAD_DIM, HEAD_DIM), lambda b: (b, 0, 0, 0))],
        out_shape=[jax.ShapeDtypeStruct((bsz, MIX_WIDTH // HEAD_DIM, HEAD_DIM), F32),
                   jax.ShapeDtypeStruct((bsz, N_HEADS_ATTN, HEAD_DIM), F32),
                   jax.ShapeDtypeStruct((bsz, N_HEADS_ATTN, HEAD_DIM), F32),
                   jax.ShapeDtypeStruct((bsz, N_HEADS_HGRN, HEAD_DIM, HEAD_DIM), F32)],
        scratch_shapes=[pltpu.VMEM((2, rps, N_HEADS_ATTN, n_keys, HEAD_DIM), F32),
                        pltpu.VMEM((2, rps, N_HEADS_ATTN, n_keys, HEAD_DIM), F32),
                        pltpu.SemaphoreType.DMA((2, rps, 2, n_pieces))],
        compiler_params=_params(("arbitrary",)),
        name="sample_mixers",
    )(proj_rows, cos, sin, *_cache_views(win_k), *_cache_views(win_v), state, mem_k, mem_v, lb_raw, norm_g)


def _rope_tables(pos):
    half = HEAD_DIM // 2
    inv_freq = 1.0 / (ROPE_THETA ** (np.arange(half, dtype=np.float64) / half))
    ang = np.asarray(pos, np.float64)[:, None] * inv_freq[None, :]
    cos, sin = np.cos(ang), np.sin(ang)
    return (jnp.asarray(np.concatenate([cos, cos], axis=-1), F32),
            jnp.asarray(np.concatenate([-sin, sin], axis=-1), F32))


def kernel(x_prompt, x_sample, cache_win_k, cache_win_v, state_hgrn, cache_mem_k, cache_mem_v, mem_prompt,
           w_in, w_mem_kv, hgrn_lb_raw, hgrn_norm_g, w_out, ln_g, ln_b):
    bsz, seq, _ = x_prompt.shape
    dbsz, n_new, _ = x_sample.shape
    assert n_new == 1 and seq % (BAND * max(DILATIONS)) == 0
    cos_p, sin_p = _rope_tables(np.arange(seq))
    cos_s, sin_s = _rope_tables(PAST_LEN + np.arange(n_new))

    hp = x_prompt.reshape(bsz * seq, D_MODEL)
    hs = x_sample.reshape(dbsz * n_new, D_MODEL)
    mem = mem_prompt.reshape(bsz * N_MEM, D_MODEL)
    outs = [[] for _ in range(8)]
    for layer in range(DEPTH):
        lb_raw = hgrn_lb_raw
        norm_g = hgrn_norm_g[layer][None]
        lg, lbias = ln_g[layer][None], ln_b[layer][None]

        proj, proj_s = _project(hp, hs, w_in[layer], PROJ_ROWS, PROJ_FIRST_COLS, PROJ_COLS)
        za, k1, v1 = _prompt_attention(proj, cos_p, sin_p, bsz, seq)
        zh, s1 = _prompt_hgrn(proj, lb_raw, norm_g, layer, bsz, seq)
        zm, mk1, mv1 = _prompt_mem(proj, mem, w_mem_kv[layer], bsz, seq)
        hp, w_out_b = _merge(za, zh, zm, hp, w_out[layer], lg, lbias, MERGE_ROWS)

        proj_s = proj_s.transpose(1, 0, 2)
        zs, k2, v2, s2 = _sample_mixers(proj_s, cos_s, sin_s, cache_win_k, cache_win_v, state_hgrn,
                                        cache_mem_k, cache_mem_v, lb_raw, norm_g, layer)
        zs = zs.transpose(1, 0, 2)
        hs = _merge(zs[:N_HEADS_ATTN], zs[N_HEADS_ATTN:N_HEADS_ATTN + N_HEADS_HGRN],
                    zs[N_HEADS_ATTN + N_HEADS_HGRN:], hs, w_out_b, lg, lbias, dbsz)

        new = (k1.transpose(0, 2, 1, 3), v1.transpose(0, 2, 1, 3), s1, mk1, mv1,
               k2.reshape(dbsz, n_new, N_HEADS_ATTN, HEAD_DIM), v2.reshape(dbsz, n_new, N_HEADS_ATTN, HEAD_DIM),
               s2.astype(state_hgrn.dtype))
        for acc, val in zip(outs, new):
            acc.append(val)

    return (hp.reshape(bsz, seq, D_MODEL), hs.reshape(dbsz, n_new, D_MODEL), *[jnp.stack(o) for o in outs])
```

```python
import functools

import jax
import jax.numpy as jnp
import numpy as np
from jax import lax
from jax.experimental import pallas as pl
from jax.experimental.pallas import tpu as pltpu

F32 = jnp.float32
BF16 = jnp.bfloat16

D_MODEL = 2048
DEPTH = 1
PAST_LEN = 8192
HEAD_DIM = 128
N_HEADS_ATTN = 6
N_HEADS_HGRN = 6
N_HEADS_MEM = 4
W_ATTN = N_HEADS_ATTN * HEAD_DIM
W_HGRN = N_HEADS_HGRN * HEAD_DIM
W_MEM = N_HEADS_MEM * HEAD_DIM
MIX_WIDTH = W_ATTN + W_HGRN + W_MEM
WINDOWS = (128, 512, 2048)
DILATIONS = (1, 4, 16)
N_MEM = 256
ROPE_THETA = 10000.0
LN_EPS = 1e-5
RMS_EPS = 1e-6
NEG_INF = -1e30
DEEPNORM_ALPHA = (2 * DEPTH) ** 0.25
ATTN_SCALE = HEAD_DIM ** -0.5
ATTN_SCALE_LOG2 = ATTN_SCALE * 1.4426950408889634

G_QA, G_KA, G_VA, G_GA = 0, 6, 12, 18
G_QB, G_FB, G_IB, G_GB = 24, 30, 36, 42
G_QM, G_GM = 48, 52
N_GROUPS = 56

LANES = 128
BAND = 128
ATTN_GROUP = 16
HGRN_CHUNK = 64
HGRN_SUB = 16
HGRN_HEADS_PER_STEP = 3
HGRN_PREP_ROWS = 256
HGRN_SCAN_CHUNKS = 16
HGRN_SAFE_LOG_RANGE = 70.0
VMEM_LIMIT = 48 * 1024 * 1024
PROJ_ROWS, PROJ_COLS, PROJ_FIRST_COLS = 1024, 1024, 512
MERGE_ROWS = 512
MERGE_CAST_ROWS = 256
WOUT_CAST_ROWS = 128


def _params(sem, vmem=VMEM_LIMIT):
    return pltpu.CompilerParams(dimension_semantics=sem, vmem_limit_bytes=vmem)


def _sigmoid(x):
    return 1.0 / (1.0 + jnp.exp(-x))


def _silu(x):
    return x * _sigmoid(x)


def _dot_nt(a, b):
    return lax.dot_general(a, b, (((1,), (1,)), ((), ())), preferred_element_type=F32)


def _store_slabs(ref, acc):
    for c in range(ref.shape[0]):
        ref[c] = acc[:, c * LANES:(c + 1) * LANES]


def _proj_first_kernel(x_ref, xs_ref, w_ref, o_ref, os_ref, wb_ref, xb_ref):
    tm = x_ref.shape[0]

    @pl.when(pl.program_id(0) == 0)
    def _():
        xb_ref[:tm, :] = x_ref[...].astype(BF16)
        xb_ref[tm:, :] = xs_ref[...].astype(BF16)

    wb_ref[...] = w_ref[...].astype(BF16)
    acc = jnp.dot(xb_ref[...], wb_ref[...], preferred_element_type=F32)
    _store_slabs(o_ref, acc[:tm])
    _store_slabs(os_ref, acc[tm:])


def _proj_rest_kernel(x_ref, w_ref, first_ref, o_hbm, xb_ref, obuf, sem_out, sem_first):
    i, j = pl.program_id(0), pl.program_id(1)
    n_col = pl.num_programs(1)
    step = i * n_col + j
    last = pl.num_programs(0) * n_col - 1
    slot = step % 2
    tm = x_ref.shape[0]
    n_slab = obuf.shape[1]

    def tile_copy(stp):
        row_tile, col_tile = stp // n_col + 1, stp % n_col
        return pltpu.make_async_copy(
            obuf.at[stp % 2], o_hbm.at[pl.ds(col_tile * n_slab, n_slab), pl.ds(row_tile * tm, tm), :],
            sem_out.at[stp % 2])

    first_copy = pltpu.make_async_copy(first_ref, o_hbm.at[pl.ds(j * n_slab, n_slab), pl.ds(0, tm), :], sem_first.at[0])

    @pl.when(i == 0)
    def _():
        first_copy.start()

    @pl.when(j == 0)
    def _():
        xb_ref[...] = x_ref[...].astype(BF16)

    @pl.when(step >= 2)
    def _():
        tile_copy(step - 2).wait()

    _store_slabs(obuf.at[slot], jnp.dot(xb_ref[...], w_ref[...], preferred_element_type=F32))
    tile_copy(step).start()

    @pl.when(i == 0)
    def _():
        first_copy.wait()

    @pl.when(step == last)
    def _():
        tile_copy(step - 1).wait()
        tile_copy(step).wait()


def _project(x, xs, w, tm, tn_first, tn):
    m, k = x.shape
    ms = xs.shape[0]
    n = w.shape[1]
    assert (m // tm - 1) * (n // tn) >= 2
    proj_first, proj_s, w_bf16 = pl.pallas_call(
        _proj_first_kernel,
        grid=(n // tn_first,),
        in_specs=[pl.BlockSpec((tm, k), lambda j: (0, 0)),
                  pl.BlockSpec((ms, k), lambda j: (0, 0)),
                  pl.BlockSpec((k, tn_first), lambda j: (0, j))],
        out_specs=[pl.BlockSpec((tn_first // LANES, tm, LANES), lambda j: (j, 0, 0)),
                   pl.BlockSpec((tn_first // LANES, ms, LANES), lambda j: (j, 0, 0)),
                   pl.BlockSpec((k, tn_first), lambda j: (0, j))],
        out_shape=[jax.ShapeDtypeStruct((n // LANES, tm, LANES), F32),
                   jax.ShapeDtypeStruct((n // LANES, ms, LANES), F32),
                   jax.ShapeDtypeStruct((k, n), BF16)],
        scratch_shapes=[pltpu.VMEM((tm + ms, k), BF16)],
        compiler_params=_params(("arbitrary",)),
        name="in_proj_first",
    )(x, xs, w)
    proj = pl.pallas_call(
        _proj_rest_kernel,
        grid=(m // tm - 1, n // tn),
        in_specs=[pl.BlockSpec((tm, k), lambda i, j: (i + 1, 0)),
                  pl.BlockSpec((k, tn), lambda i, j: (0, j)),
                  pl.BlockSpec((tn // LANES, tm, LANES), lambda i, j: (jnp.where(i == 0, j, n // tn - 1), 0, 0))],
        out_specs=pl.BlockSpec(memory_space=pl.ANY),
        out_shape=jax.ShapeDtypeStruct((n // LANES, m, LANES), F32),
        scratch_shapes=[pltpu.VMEM((tm, k), BF16), pltpu.VMEM((2, tn // LANES, tm, LANES), F32),
                        pltpu.SemaphoreType.DMA((2,)), pltpu.SemaphoreType.DMA((1,))],
        compiler_params=_params(("arbitrary", "arbitrary")),
        name="in_proj",
    )(x, w_bf16, proj_first)
    return proj, proj_s


def _rope(x, cos, sin_signed):
    return x * cos + pltpu.roll(x, HEAD_DIM // 2, 1) * sin_signed


def _bias_from_count(count):
    return jnp.where(count > 1.5, 1.0, jnp.where(count > 0.5, 0.0, NEG_INF)).astype(F32)


def _block_deltas(block_gap):
    qi = lax.broadcasted_iota(jnp.int32, (BAND, BAND), 0)
    ki = lax.broadcasted_iota(jnp.int32, (BAND, BAND), 1)
    return block_gap * BAND + qi - ki


def _near_bias(block_gap):
    d = _block_deltas(block_gap)
    return _bias_from_count(((d >= 0) & (d <= BAND)).astype(F32))


def _class_bias(block_gap, far_step):
    d = _block_deltas(block_gap)
    near = (d >= 0) & (d <= BAND)
    far = (d >= 0) & ((d & (far_step - 1)) == 0)
    return _bias_from_count(near.astype(F32) + far.astype(F32))


def _softmax_stage(scores, floors=None):
    out = []
    for idx, s in enumerate(scores):
        m = jnp.broadcast_to(jnp.max(s, axis=-1, keepdims=True), (BAND, HEAD_DIM))
        if floors is not None:
            m = jnp.maximum(m, floors[idx])
        m_wide = jnp.concatenate([m] * (s.shape[1] // HEAD_DIM), axis=1)
        out.append((m, jnp.exp2(s - m_wide).astype(BF16)))
    return out


def _attn_kernel(q_ref, k_ref, v_ref, g_ref, cos_ref, sin_ref, wo_hbm, z_ref, ko_ref, vo_ref, wob_ref,
                 qs_ref, qb_ref, kb_ref, vb_ref, qc_ref, kc_ref, vc_ref, acc_ref, m_ref, l_ref, wstage, wsem):
    seq = q_ref.shape[0]
    step = pl.program_id(0) * pl.num_programs(1) + pl.program_id(1)
    n_chunks = wo_hbm.shape[0] // wstage.shape[1]

    def chunk_copy(c):
        return pltpu.make_async_copy(wo_hbm.at[pl.ds(c * wstage.shape[1], wstage.shape[1]), :], wstage.at[c % 2],
                                     wsem.at[c % 2])

    @pl.when(step == 0)
    def _():
        chunk_copy(step).start()

    @pl.when(step + 1 < n_chunks)
    def _():
        chunk_copy(step + 1).start()

    d_mid, d_far = DILATIONS[1], DILATIONS[2]
    far_step = d_far // d_mid
    cls_rows = seq // d_mid
    cos = cos_ref[...]
    sin = sin_ref[...]
    qs_ref[...] = _rope(q_ref[...], cos, sin) * ATTN_SCALE_LOG2
    ko_ref[...] = _rope(k_ref[...], cos, sin)
    vo_ref[...] = v_ref[...]
    qb_ref[...] = qs_ref[...].astype(BF16)
    kb_ref[...] = ko_ref[...].astype(BF16)
    ones = jnp.ones((seq, HEAD_DIM), BF16)
    vb_ref[:, :HEAD_DIM] = v_ref[...].astype(BF16)
    vb_ref[:, HEAD_DIM:] = ones
    vc_ref[:, HEAD_DIM:] = ones
    for r in range(d_mid):
        cls, dst = pl.ds(r, cls_rows, stride=d_mid), pl.ds(r * cls_rows, cls_rows)
        qc_ref[dst, :] = qs_ref[cls, :].astype(BF16)
        kc_ref[dst, :] = ko_ref[cls, :].astype(BF16)
        vc_ref[dst, :HEAD_DIM] = v_ref[cls, :].astype(BF16)

    n_cls_blk = cls_rows // BAND
    cls_bias = [_class_bias(gap, far_step) for gap in range(min(n_cls_blk, 3))]
    items = [(r, n) for r in range(d_mid) for n in range(n_cls_blk)]
    for g0 in range(0, len(items), ATTN_GROUP):
        group = items[g0:g0 + ATTN_GROUP]
        keys = [pl.ds(r * cls_rows, (n + 1) * BAND) for r, n in group]
        scores = [_dot_nt(qc_ref[pl.ds(r * cls_rows + n * BAND, BAND), :], kc_ref[kr, :])
                  + jnp.concatenate([cls_bias[min(n - nk, 2)] for nk in range(n + 1)], axis=1)
                  for (r, n), kr in zip(group, keys)]
        for (r, n), kr, (m, p) in zip(group, keys, _softmax_stage(scores)):
            rows = pl.ds(r + d_mid * BAND * n, BAND, stride=d_mid)
            pv = jnp.dot(p, vc_ref[kr, :], preferred_element_type=F32)
            acc_ref[rows, :] = pv[:, :HEAD_DIM]
            l_ref[rows, :] = pv[:, HEAD_DIM:]
            m_ref[rows, :] = m

    near_bias = [_near_bias(0), jnp.concatenate([_near_bias(1), _near_bias(0)], axis=1)]
    items = list(range(seq // BAND))
    for g0 in range(0, len(items), ATTN_GROUP):
        group = items[g0:g0 + ATTN_GROUP]
        rows = [pl.ds(n * BAND, BAND) for n in group]
        keys = [pl.ds(max(n - 1, 0) * BAND, BAND * min(n + 1, 2)) for n in group]
        scores = [_dot_nt(qb_ref[rw, :], kb_ref[kr, :]) + near_bias[min(n, 1)]
                  for n, rw, kr in zip(group, rows, keys)]
        probs = _softmax_stage(scores, floors=[m_ref[rw, :] for rw in rows])
        for rw, kr, (m, p) in zip(rows, keys, probs):
            w = jnp.exp2(m_ref[rw, :] - m)
            pv = jnp.dot(p, vb_ref[kr, :], preferred_element_type=F32)
            num = pv[:, :HEAD_DIM] + w * acc_ref[rw, :]
            den = pv[:, HEAD_DIM:] + w * l_ref[rw, :]
            z_ref[rw, :] = ((num / den) * _silu(g_ref[rw, :])).astype(z_ref.dtype)

    @pl.when(step < n_chunks)
    def _():
        chunk_copy(step).wait()
        wob_ref[...] = wstage[step % 2].astype(BF16)


def _prompt_attention(proj, cos, sin, w_out, bsz, seq):
    def col(g0):
        return pl.BlockSpec((None, seq, HEAD_DIM), lambda b, h: (g0 + h, b, 0))

    table = pl.BlockSpec((seq, HEAD_DIM), lambda b, h: (0, 0))
    kv_out = pl.BlockSpec((None, None, seq, HEAD_DIM), lambda b, h: (b, h, 0, 0))
    d_near, d_mid, d_far = DILATIONS
    far_step = d_far // d_mid
    assert d_near == 1 and d_far % d_mid == 0 and far_step & (far_step - 1) == 0
    assert all(w // d == BAND for w, d in zip(WINDOWS, DILATIONS)) and seq % (d_mid * BAND) == 0
    assert seq // d_mid <= far_step * BAND
    n_chunks = w_out.shape[0] // WOUT_CAST_ROWS
    assert bsz * N_HEADS_ATTN >= n_chunks and w_out.shape[0] % WOUT_CAST_ROWS == 0
    wob_spec = pl.BlockSpec((WOUT_CAST_ROWS, w_out.shape[1]),
                            lambda b, h: (jnp.minimum(b * N_HEADS_ATTN + h, n_chunks - 1), 0))
    return pl.pallas_call(
        _attn_kernel,
        grid=(bsz, N_HEADS_ATTN),
        in_specs=[col(G_QA), col(G_KA), col(G_VA), col(G_GA), table, table, pl.BlockSpec(memory_space=pl.ANY)],
        out_specs=[pl.BlockSpec((None, seq, HEAD_DIM), lambda b, h: (h, b, 0)), kv_out, kv_out, wob_spec],
        out_shape=[jax.ShapeDtypeStruct((N_HEADS_ATTN, bsz * seq, HEAD_DIM), BF16),
                   jax.ShapeDtypeStruct((bsz, N_HEADS_ATTN, seq, HEAD_DIM), F32),
                   jax.ShapeDtypeStruct((bsz, N_HEADS_ATTN, seq, HEAD_DIM), F32),
                   jax.ShapeDtypeStruct(w_out.shape, BF16)],
        scratch_shapes=[pltpu.VMEM((seq, HEAD_DIM), F32)]
                       + [pltpu.VMEM((seq, HEAD_DIM), BF16), pltpu.VMEM((seq, HEAD_DIM), BF16),
                          pltpu.VMEM((seq, 2 * HEAD_DIM), BF16)] * 2
                       + [pltpu.VMEM((seq, HEAD_DIM), F32)] * 3
                       + [pltpu.VMEM((2, WOUT_CAST_ROWS, w_out.shape[1]), F32), pltpu.SemaphoreType.DMA((2,))],
        compiler_params=_params(("arbitrary", "arbitrary")),
        name="prompt_attn",
    )(proj, proj, proj, proj, cos, sin, w_out)


def _lower_bound(lb_raw, layer):
    e = jnp.exp(lb_raw - jnp.max(lb_raw, axis=0, keepdims=True))
    sm = e / jnp.sum(e, axis=0, keepdims=True)
    return jnp.sum(sm[:layer + 1], axis=0, keepdims=True)


def _split2(x):
    hi = x.astype(BF16)
    return hi, (x - hi.astype(F32)).astype(BF16)


def _rms_gate(o, norm_g, gate):
    o = o * lax.rsqrt(jnp.mean(o * o, axis=-1, keepdims=True) + RMS_EPS)
    return o * norm_g * _silu(gate)


def _hgrn_gates(fb, lb, tril):
    f = lb + (1.0 - lb) * _sigmoid(fb)
    g = jnp.log(f)
    b = sum(jnp.dot(tril, piece, preferred_element_type=F32) for piece in _split2(g))
    return 1.0 - f, b


def _hgrn_fast_prepare(q_ref, f_ref, lb, qh_ref, kh_ref, el_ref):
    n_heads, seq, _ = q_ref.shape
    blk = HGRN_PREP_ROWS
    tril = _tril_ones(blk)

    def body(bi, carry):
        r0 = pl.multiple_of(bi * blk, blk)
        fb = jnp.concatenate([f_ref[h, pl.ds(r0, blk), :] for h in range(n_heads)], axis=1)
        f = lb + (1.0 - lb) * _sigmoid(fb)
        b_blk = sum(jnp.dot(tril, piece, preferred_element_type=F32) for piece in _split2(jnp.log(f)))
        kk = 1.0 - f
        for c0 in range(0, blk, HGRN_CHUNK):
            rs = slice(c0, c0 + HGRN_CHUNK)
            b = b_blk[rs] - b_blk[c0 - 1:c0] if c0 else b_blk[rs]
            rows = pl.ds(r0 + c0, HGRN_CHUNK)
            q = jnp.concatenate([q_ref[h, rows, :] for h in range(n_heads)], axis=1)
            qh_ref[rows, :] = (q * jnp.exp(b)).astype(BF16)
            kh_ref[rows, :] = (kk[rs] * jnp.exp(-b)).astype(BF16)
            el_ref[pl.ds(bi * (blk // HGRN_CHUNK) + c0 // HGRN_CHUNK, 1), :] = jnp.exp(b[HGRN_CHUNK - 1:])
        return carry

    lax.fori_loop(0, seq // blk, body, 0, unroll=4)


def _hgrn_fast_scan(i_ref, g_ref, norm_g, qh_ref, kh_ref, el_ref, st_ref, z_ref):
    n_heads, seq, _ = i_ref.shape
    c = HGRN_CHUNK
    heads = range(n_heads)
    chunks = range(HGRN_SCAN_CHUNKS)
    lanes = [slice(h * HEAD_DIM, (h + 1) * HEAD_DIM) for h in heads]
    causal = lax.broadcasted_iota(jnp.int32, (c, c), 0) >= lax.broadcasted_iota(jnp.int32, (c, c), 1)
    tn = (((0,), (0,)), ((), ()))

    def body(ti, carry):
        rows = [pl.ds(pl.multiple_of((ti * HGRN_SCAN_CHUNKS + k) * c, c), c) for k in chunks]
        qh = [[qh_ref[rows[k], lanes[h]] for h in heads] for k in chunks]
        kh = [[kh_ref[rows[k], lanes[h]] for h in heads] for k in chunks]
        vb = [[i_ref[h, rows[k], :].astype(BF16) for h in heads] for k in chunks]
        att = [[jnp.where(causal, _dot_nt(qh[k][h], kh[k][h]), 0.0).astype(BF16) for h in heads] for k in chunks]
        ds = [[lax.dot_general(vb[k][h], kh[k][h], tn, preferred_element_type=F32) for h in heads] for k in chunks]
        st = [st_ref[h] for h in heads]
        o = []
        for k in chunks:
            el = el_ref[pl.ds(ti * HGRN_SCAN_CHUNKS + k, 1), :]
            o.append([_dot_nt(qh[k][h], st[h].astype(BF16)) for h in heads])
            st = [(st[h] + ds[k][h]) * el[:, lanes[h]] for h in heads]
        for h in heads:
            st_ref[h] = st[h]
        for k in chunks:
            for h in heads:
                o_kh = o[k][h] + jnp.dot(att[k][h], vb[k][h], preferred_element_type=F32)
                z_ref[h, rows[k], :] = _rms_gate(o_kh, norm_g[:, lanes[h]], g_ref[h, rows[k], :]).astype(z_ref.dtype)
        return carry

    lax.fori_loop(0, seq // (c * HGRN_SCAN_CHUNKS), body, 0)


def _hgrn_chunk(q, fb, v, lb, st, tril):
    c = q.shape[0]
    kk, b = _hgrn_gates(fb, lb, tril)
    o = _dot_nt((q * jnp.exp(b)).astype(BF16), st.astype(BF16))

    s_idx = lax.broadcasted_iota(jnp.int32, (c, 1), 0)
    lane = lax.broadcasted_iota(jnp.int32, (HGRN_SUB, c), 1)
    row = lax.broadcasted_iota(jnp.int32, (HGRN_SUB, c), 0)
    att_rows = []
    for i0 in range(0, c, HGRN_SUB):
        qi = q[i0:i0 + HGRN_SUB]
        bi = b[i0:i0 + HGRN_SUB]
        if i0 > 0:
            bref = b[i0 - 1:i0]
            kt = jnp.where(s_idx < i0, kk * jnp.exp(jnp.minimum(bref - b, 0.0)), 0.0)
            att = _dot_nt((qi * jnp.exp(bi - bref)).astype(BF16), kt.astype(BF16))
        else:
            att = jnp.zeros((HGRN_SUB, c), F32)
        for j in range(HGRN_SUB):
            s = i0 + j
            e = jnp.exp(jnp.minimum(bi - b[s:s + 1], 0.0))
            colv = jnp.sum(qi * kk[s:s + 1] * e, axis=-1, keepdims=True)
            att = jnp.where((lane == s) & (row >= j), colv, att)
        att_rows.append(att)
    att = jnp.concatenate(att_rows, axis=0)
    vb = v.astype(BF16)
    o = o + jnp.dot(att.astype(BF16), vb, preferred_element_type=F32)

    b_last = b[c - 1:c]
    kd = (kk * jnp.exp(b_last - b)).astype(BF16)
    st_new = st * jnp.exp(b_last) + lax.dot_general(vb, kd, (((0,), (0,)), ((), ())),
                                                    preferred_element_type=F32)
    return o, st_new


def _tril_ones(c):
    return (lax.broadcasted_iota(jnp.int32, (c, c), 0) >= lax.broadcasted_iota(jnp.int32, (c, c), 1)
            ).astype(BF16)


def _hgrn_kernel(layer, q_ref, f_ref, i_ref, g_ref, lb_ref, ng_ref, z_ref, s_ref, st_ref, qh_ref, kh_ref, el_ref):
    n_heads, seq, _ = q_ref.shape
    lb = _lower_bound(lb_ref[...], layer)
    norm_g = ng_ref[...]
    st_ref[...] = jnp.zeros_like(st_ref)

    def fast():
        _hgrn_fast_prepare(q_ref, f_ref, lb, qh_ref, kh_ref, el_ref)
        _hgrn_fast_scan(i_ref, g_ref, norm_g, qh_ref, kh_ref, el_ref, st_ref, z_ref)

    def safe():
        tril = _tril_ones(HGRN_CHUNK)

        def body(ci, carry):
            rows = pl.ds(pl.multiple_of(ci * HGRN_CHUNK, HGRN_CHUNK), HGRN_CHUNK)
            for h in range(n_heads):
                lanes = slice(h * HEAD_DIM, (h + 1) * HEAD_DIM)
                o, st = _hgrn_chunk(q_ref[h, rows, :], f_ref[h, rows, :], i_ref[h, rows, :], lb[:, lanes],
                                    st_ref[h], tril)
                st_ref[h] = st
                z_ref[h, rows, :] = _rms_gate(o, norm_g[:, lanes], g_ref[h, rows, :]).astype(z_ref.dtype)
            return carry

        lax.fori_loop(0, seq // HGRN_CHUNK, body, 0)

    fast_ok = HGRN_CHUNK * -jnp.log(jnp.min(lb)) <= HGRN_SAFE_LOG_RANGE
    lax.cond(fast_ok, fast, safe)
    for h in range(n_heads):
        s_ref[h] = st_ref[h].T


def _prompt_hgrn(proj, lb_raw, norm_g, layer, bsz, seq):
    hps = HGRN_HEADS_PER_STEP
    assert N_HEADS_HGRN % hps == 0 and all(g % hps == 0 for g in (G_QB, G_FB, G_IB, G_GB))
    assert seq % (HGRN_CHUNK * HGRN_SCAN_CHUNKS) == 0 and seq % HGRN_PREP_ROWS == 0 and HGRN_PREP_ROWS % HGRN_CHUNK == 0

    def cols(g0):
        return pl.BlockSpec((hps, seq, HEAD_DIM), lambda b, h: (g0 // hps + h, b, 0))

    return pl.pallas_call(
        functools.partial(_hgrn_kernel, layer),
        grid=(bsz, N_HEADS_HGRN // hps),
        in_specs=[cols(G_QB), cols(G_FB), cols(G_IB), cols(G_GB),
                  pl.BlockSpec((DEPTH + 1, hps * HEAD_DIM), lambda b, h: (0, h)),
                  pl.BlockSpec((1, hps * HEAD_DIM), lambda b, h: (0, h))],
        out_specs=[pl.BlockSpec((hps, seq, HEAD_DIM), lambda b, h: (h, b, 0)),
                   pl.BlockSpec((None, hps, HEAD_DIM, HEAD_DIM), lambda b, h: (b, h, 0, 0))],
        out_shape=[jax.ShapeDtypeStruct((N_HEADS_HGRN, bsz * seq, HEAD_DIM), BF16),
                   jax.ShapeDtypeStruct((bsz, N_HEADS_HGRN, HEAD_DIM, HEAD_DIM), F32)],
        scratch_shapes=[pltpu.VMEM((hps, HEAD_DIM, HEAD_DIM), F32),
                        pltpu.VMEM((seq, hps * HEAD_DIM), BF16),
                        pltpu.VMEM((seq, hps * HEAD_DIM), BF16),
                        pltpu.VMEM((seq // HGRN_CHUNK, hps * HEAD_DIM), F32)],
        compiler_params=_params(("parallel", "parallel")),
        name="prompt_hgrn",
    )(proj, proj, proj, proj, lb_raw, norm_g)


MEM_ROWS = 256
MEM_GROUP = 8


def _mem_kernel(q_ref, g_ref, mem_ref, w_ref, z_ref, mk_ref, mv_ref):
    seq = q_ref.shape[1]
    kv = jnp.dot(mem_ref[...].astype(BF16), w_ref[...].astype(BF16), preferred_element_type=F32)
    ones = jnp.ones((N_MEM, HEAD_DIM), BF16)
    mk_b, mv_b = [], []
    for h in range(N_HEADS_MEM):
        mk = kv[:, h * HEAD_DIM:(h + 1) * HEAD_DIM]
        mv = kv[:, W_MEM + h * HEAD_DIM:W_MEM + (h + 1) * HEAD_DIM]
        mk_ref[:, h, :] = mk
        mv_ref[:, h, :] = mv
        mk_b.append(mk.astype(BF16))
        mv_b.append(jnp.concatenate([mv.astype(BF16), ones], axis=1))

    items = [(h, pl.ds(n * MEM_ROWS, MEM_ROWS)) for h in range(N_HEADS_MEM) for n in range(seq // MEM_ROWS)]
    for g0 in range(0, len(items), MEM_GROUP):
        group = items[g0:g0 + MEM_GROUP]
        scores = [_dot_nt((q_ref[h, rows, :] * ATTN_SCALE_LOG2).astype(BF16), mk_b[h]) for h, rows in group]
        probs = [jnp.exp2(s - jnp.max(s, axis=-1, keepdims=True)).astype(BF16) for s in scores]
        for (h, rows), p in zip(group, probs):
            pv = jnp.dot(p, mv_b[h], preferred_element_type=F32)
            o = pv[:, :HEAD_DIM] / pv[:, HEAD_DIM:]
            z_ref[h, rows, :] = (o * _silu(g_ref[h, rows, :])).astype(z_ref.dtype)


def _prompt_mem(proj, mem, w_kv, bsz, seq):
    assert G_QM % N_HEADS_MEM == 0 and G_GM % N_HEADS_MEM == 0

    def cols(g0):
        return pl.BlockSpec((N_HEADS_MEM, seq, HEAD_DIM), lambda b: (g0 // N_HEADS_MEM, b, 0))

    kv_out = pl.BlockSpec((None, N_MEM, N_HEADS_MEM, HEAD_DIM), lambda b: (b, 0, 0, 0))
    kv_shape = jax.ShapeDtypeStruct((bsz, N_MEM, N_HEADS_MEM, HEAD_DIM), F32)
    return pl.pallas_call(
        _mem_kernel,
        grid=(bsz,),
        in_specs=[cols(G_QM), cols(G_GM), pl.BlockSpec((N_MEM, D_MODEL), lambda b: (b, 0)),
                  pl.BlockSpec((D_MODEL, 2 * W_MEM), lambda b: (0, 0))],
        out_specs=[pl.BlockSpec((N_HEADS_MEM, seq, HEAD_DIM), lambda b: (0, b, 0)), kv_out, kv_out],
        out_shape=[jax.ShapeDtypeStruct((N_HEADS_MEM, bsz * seq, HEAD_DIM), BF16), kv_shape, kv_shape],
        compiler_params=_params(("parallel",)),
        name="prompt_mem",
    )(proj, proj, mem, w_kv)


def _merge_kernel(cast_w, za_ref, zh_ref, zm_ref, x_ref, w_ref, lg_ref, lb_ref, o_ref, *rest):
    if cast_w:
        wb_ref, z_ref, stage, sem = rest
        n_rows = stage.shape[1]
        n_chunks = w_ref.shape[0] // n_rows

        def chunk_copy(c):
            return pltpu.make_async_copy(w_ref.at[pl.ds(c * n_rows, n_rows), :], stage.at[c % 2], sem.at[c % 2])

        @pl.when(pl.program_id(0) == 0)
        def _():
            chunk_copy(0).start()
            for c in range(n_chunks):
                if c + 1 < n_chunks:
                    chunk_copy(c + 1).start()
                chunk_copy(c).wait()
                wb_ref[c * n_rows:(c + 1) * n_rows, :] = stage[c % 2].astype(BF16)
    else:
        wb_ref, (z_ref,) = w_ref, rest

    c0 = 0
    for ref in (za_ref, zh_ref, zm_ref):
        for c in range(ref.shape[0]):
            z_ref[:, (c0 + c) * LANES:(c0 + c + 1) * LANES] = ref[c].astype(BF16)
        c0 += ref.shape[0]
    y = jnp.dot(z_ref[...], wb_ref[...], preferred_element_type=F32)
    r = DEEPNORM_ALPHA * x_ref[...] + y
    mu = jnp.mean(r, axis=-1, keepdims=True)
    d = r - mu
    var = jnp.mean(d * d, axis=-1, keepdims=True)
    o_ref[...] = d * lax.rsqrt(var + LN_EPS) * lg_ref[...] + lb_ref[...]


def _merge(za, zh, zm, x, w_out, ln_g, ln_b, tm):
    m = x.shape[0]
    cast_w = w_out.dtype != BF16

    def slab(a):
        return pl.BlockSpec((a.shape[0], tm, LANES), lambda i: (0, i, 0))

    const = lambda shape: pl.BlockSpec(shape, lambda i: (0, 0))
    y_spec, y_shape = pl.BlockSpec((tm, D_MODEL), lambda i: (i, 0)), jax.ShapeDtypeStruct((m, D_MODEL), F32)
    w_block = const((MIX_WIDTH, D_MODEL))
    z_scratch = pltpu.VMEM((tm, MIX_WIDTH), BF16)
    if cast_w:
        w_spec = pl.BlockSpec(memory_space=pl.ANY)
        out_specs, out_shape = [y_spec, w_block], [y_shape, jax.ShapeDtypeStruct((MIX_WIDTH, D_MODEL), BF16)]
        scratch = [z_scratch, pltpu.VMEM((2, MERGE_CAST_ROWS, D_MODEL), F32), pltpu.SemaphoreType.DMA((2,))]
    else:
        w_spec, out_specs, out_shape, scratch = w_block, y_spec, y_shape, [z_scratch]
    return pl.pallas_call(
        functools.partial(_merge_kernel, cast_w),
        grid=(m // tm,),
        in_specs=[slab(za), slab(zh), slab(zm), pl.BlockSpec((tm, D_MODEL), lambda i: (i, 0)),
                  w_spec, const((1, D_MODEL)), const((1, D_MODEL))],
        out_specs=out_specs,
        out_shape=out_shape,
        scratch_shapes=scratch,
        compiler_params=_params(("arbitrary",)),
        name="merge",
    )(za, zh, zm, x, w_out, ln_g, ln_b)


def _column(row):
    return jnp.broadcast_to(row, (HEAD_DIM, HEAD_DIM)).T


HBM_TILE_ROWS = 8
SAMPLE_ROWS = 2


def _window_pieces(past):
    pieces = []
    for dil in DILATIONS:
        if dil == 1:
            pieces.append((0, past - BAND, BAND, None))
        elif dil < HBM_TILE_ROWS:
            n = BAND * dil // HBM_TILE_ROWS
            pieces += [(1, past // HBM_TILE_ROWS - n, n, r) for r in range(0, HBM_TILE_ROWS, dil)]
        else:
            pieces.append((2, past // dil - BAND, BAND, 0))
    return pieces


def _cache_views(cache):
    depth, bsz, past, nh, hd = cache.shape
    assert all(d == 1 or HBM_TILE_ROWS % d == 0 or d % HBM_TILE_ROWS == 0 for d in DILATIONS)
    big = max(DILATIONS)
    hm = cache.transpose(0, 1, 3, 2, 4)
    return (hm, hm.reshape(depth, bsz, nh, past // HBM_TILE_ROWS, HBM_TILE_ROWS, hd),
            hm.reshape(depth, bsz, nh, past // big, big, hd))


def _sample_row(layer, p_ref, cos, sin, k_win, v_win, st_ref, mk_ref, mv_ref, lb_all, ng_ref, z_ref, ko_ref, vo_ref,
                so_ref):
    q_all = _rope(p_ref[G_QA:G_QA + N_HEADS_ATTN, :], cos, sin) * ATTN_SCALE
    k_all = _rope(p_ref[G_KA:G_KA + N_HEADS_ATTN, :], cos, sin)
    v_all = p_ref[G_VA:G_VA + N_HEADS_ATTN, :]
    ko_ref[...] = k_all
    vo_ref[...] = v_all

    for h in range(N_HEADS_ATTN):
        q = q_all[h:h + 1]
        s_new = jnp.sum(q * k_all[h:h + 1], axis=-1, keepdims=True)
        s = jnp.sum(k_win[h] * q, axis=-1, keepdims=True)
        m = jnp.maximum(jnp.max(s, axis=0, keepdims=True), s_new)
        p = jnp.exp(s - m)
        p_new = jnp.exp(s_new - m) * len(DILATIONS)
        den = jnp.sum(p, axis=0, keepdims=True) + p_new
        num = jnp.sum(p * v_win[h], axis=0, keepdims=True) + p_new * v_all[h:h + 1]
        z_ref[h:h + 1, :] = (num / den) * _silu(p_ref[G_GA + h:G_GA + h + 1, :])

    for h in range(N_HEADS_HGRN):
        lanes = slice(h * HEAD_DIM, (h + 1) * HEAD_DIM)
        lb = _lower_bound(lb_all[:, lanes], layer)
        f = lb + (1.0 - lb) * _sigmoid(p_ref[G_FB + h:G_FB + h + 1, :])
        f_col = _column(f)
        q_col = _column(p_ref[G_QB + h:G_QB + h + 1, :])
        s_new = f_col * st_ref[h] + (1.0 - f_col) * p_ref[G_IB + h:G_IB + h + 1, :]
        so_ref[h] = s_new
        o = jnp.sum(s_new * q_col, axis=0, keepdims=True)
        z_ref[N_HEADS_ATTN + h:N_HEADS_ATTN + h + 1, :] = _rms_gate(
            o, ng_ref[:, lanes], p_ref[G_GB + h:G_GB + h + 1, :])

    fold = mk_ref.shape[1] // N_HEADS_MEM
    unfold = lambda a: sum(a[i * N_HEADS_MEM:(i + 1) * N_HEADS_MEM] for i in range(fold))
    q_m = p_ref[G_QM:G_QM + N_HEADS_MEM, :] * ATTN_SCALE
    s = jnp.sum(mk_ref[...] * jnp.concatenate([q_m] * fold, axis=0)[None], axis=-1, keepdims=True)
    m = jnp.max(s, axis=0)
    m = functools.reduce(jnp.maximum, [m[i * N_HEADS_MEM:(i + 1) * N_HEADS_MEM] for i in range(fold)])
    p = jnp.exp(s - jnp.concatenate([m] * fold, axis=0)[None])
    o = unfold(jnp.sum(p * mv_ref[...], axis=0)) / unfold(jnp.sum(p, axis=0))
    row = N_HEADS_ATTN + N_HEADS_HGRN
    z_ref[row:row + N_HEADS_MEM, :] = o * _silu(p_ref[G_GM:G_GM + N_HEADS_MEM, :])


def _sample_kernel(layer, past, p_ref, cos_ref, sin_ref, k0_hbm, k1_hbm, k2_hbm, v0_hbm, v1_hbm, v2_hbm,
                   st_ref, mk_ref, mv_ref, lb_ref, ng_ref, z_ref, ko_ref, vo_ref, so_ref, kbuf, vbuf, sem):
    step = pl.program_id(0)
    slot = step % 2
    n_rows = p_ref.shape[0]
    pieces = _window_pieces(past)

    def window_copies(stp, sl):
        out = []
        for rr in range(n_rows):
            for ci, (views, buf) in enumerate((((k0_hbm, k1_hbm, k2_hbm), kbuf), ((v0_hbm, v1_hbm, v2_hbm), vbuf))):
                off = 0
                for pi, (vi, start, count, res) in enumerate(pieces):
                    view, row = views[vi], stp * n_rows + rr
                    src = (view.at[layer, row, :, pl.ds(start, count), :] if res is None
                           else view.at[layer, row, :, pl.ds(start, count), res, :])
                    out.append(pltpu.make_async_copy(src, buf.at[sl, rr, :, pl.ds(off, count), :],
                                                     sem.at[sl, rr, ci, pi]))
                    off += count
        return out

    @pl.when(step == 0)
    def _():
        for c in window_copies(step, slot):
            c.start()

    @pl.when(step + 1 < pl.num_programs(0))
    def _():
        for c in window_copies(step + 1, 1 - slot):
            c.start()

    for c in window_copies(step, slot):
        c.wait()

    cos, sin, lb_all = cos_ref[...], sin_ref[...], lb_ref[...]
    for rr in range(n_rows):
        _sample_row(layer, p_ref.at[rr], cos, sin, kbuf.at[slot, rr], vbuf.at[slot, rr], st_ref.at[rr], mk_ref.at[rr],
                    mv_ref.at[rr], lb_all, ng_ref, z_ref.at[rr], ko_ref.at[rr], vo_ref.at[rr], so_ref.at[rr])


def _sample_mixers(proj_rows, cos, sin, win_k, win_v, state, mem_k, mem_v, lb_raw, norm_g, layer):
    bsz, past = win_k.shape[1:3]
    rps = SAMPLE_ROWS
    assert all(past % d == 0 and past >= w for w, d in zip(WINDOWS, DILATIONS)) and bsz % rps == 0
    assert all(w // d == BAND for w, d in zip(WINDOWS, DILATIONS)) and past % HBM_TILE_ROWS == 0
    n_keys = BAND * len(DILATIONS)
    n_pieces = len(_window_pieces(past))
    any_spec = pl.BlockSpec(memory_space=pl.ANY)
    vec = lambda n: pl.BlockSpec((n, W_HGRN), lambda b: (0, 0))
    heads = lambda n: pl.BlockSpec((rps, n, HEAD_DIM), lambda b: (b, 0, 0))
    fold = HBM_TILE_ROWS // N_HEADS_MEM
    mem_k, mem_v = (a.reshape(a.shape[0], bsz, N_MEM // fold, fold * N_HEADS_MEM, HEAD_DIM) for a in (mem_k, mem_v))
    mem_spec = pl.BlockSpec((None, rps, N_MEM // fold, fold * N_HEADS_MEM, HEAD_DIM), lambda b: (layer, b, 0, 0, 0))
    return pl.pallas_call(
        functools.partial(_sample_kernel, layer, past),
        grid=(bsz // rps,),
        in_specs=[heads(N_GROUPS), pl.BlockSpec((1, HEAD_DIM), lambda b: (0, 0)),
                  pl.BlockSpec((1, HEAD_DIM), lambda b: (0, 0)), *([any_spec] * 6),
                  pl.BlockSpec((None, rps, N_HEADS_HGRN, HEAD_DIM, HEAD_DIM), lambda b: (layer, b, 0, 0, 0)),
                  mem_spec, mem_spec, vec(DEPTH + 1), vec(1)],
        out_specs=[heads(MIX_WIDTH // HEAD_DIM), heads(N_HEADS_ATTN), heads(N_HEADS_ATTN),
                   pl.BlockSpec((rps, N_HEADS_HGRN, HEAD_DIM, HEAD_DIM), lambda b: (b, 0, 0, 0))],
        out_shape=[jax.ShapeDtypeStruct((bsz, MIX_WIDTH // HEAD_DIM, HEAD_DIM), F32),
                   jax.ShapeDtypeStruct((bsz, N_HEADS_ATTN, HEAD_DIM), F32),
                   jax.ShapeDtypeStruct((bsz, N_HEADS_ATTN, HEAD_DIM), F32),
                   jax.ShapeDtypeStruct((bsz, N_HEADS_HGRN, HEAD_DIM, HEAD_DIM), F32)],
        scratch_shapes=[pltpu.VMEM((2, rps, N_HEADS_ATTN, n_keys, HEAD_DIM), F32),
                        pltpu.VMEM((2, rps, N_HEADS_ATTN, n_keys, HEAD_DIM), F32),
                        pltpu.SemaphoreType.DMA((2, rps, 2, n_pieces))],
        compiler_params=_params(("arbitrary",)),
        name="sample_mixers",
    )(proj_rows, cos, sin, *_cache_views(win_k), *_cache_views(win_v), state, mem_k, mem_v, lb_raw, norm_g)


def _rope_tables(pos):
    half = HEAD_DIM // 2
    inv_freq = 1.0 / (ROPE_THETA ** (np.arange(half, dtype=np.float64) / half))
    ang = np.asarray(pos, np.float64)[:, None] * inv_freq[None, :]
    cos, sin = np.cos(ang), np.sin(ang)
    return (jnp.asarray(np.concatenate([cos, cos], axis=-1), F32),
            jnp.asarray(np.concatenate([-sin, sin], axis=-1), F32))


def kernel(x_prompt, x_sample, cache_win_k, cache_win_v, state_hgrn, cache_mem_k, cache_mem_v, mem_prompt,
           w_in, w_mem_kv, hgrn_lb_raw, hgrn_norm_g, w_out, ln_g, ln_b):
    bsz, seq, _ = x_prompt.shape
    dbsz, n_new, _ = x_sample.shape
    assert n_new == 1 and seq % (BAND * max(DILATIONS)) == 0
    cos_p, sin_p = _rope_tables(np.arange(seq))
    cos_s, sin_s = _rope_tables(PAST_LEN + np.arange(n_new))

    hp = x_prompt.reshape(bsz * seq, D_MODEL)
    hs = x_sample.reshape(dbsz * n_new, D_MODEL)
    mem = mem_prompt.reshape(bsz * N_MEM, D_MODEL)
    outs = [[] for _ in range(8)]
    for layer in range(DEPTH):
        lb_raw = hgrn_lb_raw
        norm_g = hgrn_norm_g[layer][None]
        lg, lbias = ln_g[layer][None], ln_b[layer][None]

        proj, proj_s = _project(hp, hs, w_in[layer], PROJ_ROWS, PROJ_FIRST_COLS, PROJ_COLS)
        za, k1, v1, w_out_b = _prompt_attention(proj, cos_p, sin_p, w_out[layer], bsz, seq)
        zh, s1 = _prompt_hgrn(proj, lb_raw, norm_g, layer, bsz, seq)
        zm, mk1, mv1 = _prompt_mem(proj, mem, w_mem_kv[layer], bsz, seq)
        hp = _merge(za, zh, zm, hp, w_out_b, lg, lbias, MERGE_ROWS)

        proj_s = proj_s.transpose(1, 0, 2)
        zs, k2, v2, s2 = _sample_mixers(proj_s, cos_s, sin_s, cache_win_k, cache_win_v, state_hgrn,
                                        cache_mem_k, cache_mem_v, lb_raw, norm_g, layer)
        zs = zs.transpose(1, 0, 2)
        hs = _merge(zs[:N_HEADS_ATTN], zs[N_HEADS_ATTN:N_HEADS_ATTN + N_HEADS_HGRN],
                    zs[N_HEADS_ATTN + N_HEADS_HGRN:], hs, w_out_b, lg, lbias, dbsz)

        new = (k1.transpose(0, 2, 1, 3), v1.transpose(0, 2, 1, 3), s1, mk1, mv1,
               k2.reshape(dbsz, n_new, N_HEADS_ATTN, HEAD_DIM), v2.reshape(dbsz, n_new, N_HEADS_ATTN, HEAD_DIM),
               s2.astype(state_hgrn.dtype))
        for acc, val in zip(outs, new):
            acc.append(val)

    return (hp.reshape(bsz, seq, D_MODEL), hs.reshape(dbsz, n_new, D_MODEL), *[jnp.stack(o) for o in outs])
```

```python
import functools

import jax
import jax.numpy as jnp
import numpy as np
from jax import lax
from jax.experimental import pallas as pl
from jax.experimental.pallas import tpu as pltpu

F32 = jnp.float32
BF16 = jnp.bfloat16

D_MODEL = 2048
DEPTH = 1
PAST_LEN = 8192
HEAD_DIM = 128
N_HEADS_ATTN = 6
N_HEADS_HGRN = 6
N_HEADS_MEM = 4
W_ATTN = N_HEADS_ATTN * HEAD_DIM
W_HGRN = N_HEADS_HGRN * HEAD_DIM
W_MEM = N_HEADS_MEM * HEAD_DIM
MIX_WIDTH = W_ATTN + W_HGRN + W_MEM
WINDOWS = (128, 512, 2048)
DILATIONS = (1, 4, 16)
N_MEM = 256
ROPE_THETA = 10000.0
LN_EPS = 1e-5
RMS_EPS = 1e-6
NEG_INF = -1e30
DEEPNORM_ALPHA = (2 * DEPTH) ** 0.25
ATTN_SCALE = HEAD_DIM ** -0.5
ATTN_SCALE_LOG2 = ATTN_SCALE * 1.4426950408889634

G_QA, G_KA, G_VA, G_GA = 0, 6, 12, 18
G_QB, G_FB, G_IB, G_GB = 24, 30, 36, 42
G_QM, G_GM = 48, 52
N_GROUPS = 56

LANES = 128
BAND = 128
ATTN_GROUP = 16
HGRN_CHUNK = 64
HGRN_SUB = 16
HGRN_HEADS_PER_STEP = 3
HGRN_PREP_ROWS = 256
HGRN_SCAN_CHUNKS = 16
HGRN_SAFE_LOG_RANGE = 70.0
VMEM_LIMIT = 48 * 1024 * 1024
PROJ_ROWS, PROJ_COLS, PROJ_FIRST_COLS = 1024, 1024, 512
MERGE_ROWS = 512
MERGE_CAST_ROWS = 256


def _params(sem, vmem=VMEM_LIMIT):
    return pltpu.CompilerParams(dimension_semantics=sem, vmem_limit_bytes=vmem)


def _sigmoid(x):
    return 1.0 / (1.0 + jnp.exp(-x))


def _silu(x):
    return x * _sigmoid(x)


def _dot_nt(a, b):
    return lax.dot_general(a, b, (((1,), (1,)), ((), ())), preferred_element_type=F32)


def _store_slabs(ref, acc):
    for c in range(ref.shape[0]):
        ref[c] = acc[:, c * LANES:(c + 1) * LANES]


def _proj_first_kernel(x_ref, xs_ref, w_ref, o_ref, os_ref, wb_ref, xb_ref):
    tm = x_ref.shape[0]

    @pl.when(pl.program_id(0) == 0)
    def _():
        xb_ref[:tm, :] = x_ref[...].astype(BF16)
        xb_ref[tm:, :] = xs_ref[...].astype(BF16)

    wb_ref[...] = w_ref[...].astype(BF16)
    acc = jnp.dot(xb_ref[...], wb_ref[...], preferred_element_type=F32)
    _store_slabs(o_ref, acc[:tm])
    _store_slabs(os_ref, acc[tm:])


def _proj_rest_kernel(x_ref, w_ref, first_ref, o_hbm, xb_ref, obuf, sem_out, sem_first):
    i, j = pl.program_id(0), pl.program_id(1)
    n_col = pl.num_programs(1)
    step = i * n_col + j
    last = pl.num_programs(0) * n_col - 1
    slot = step % 2
    tm = x_ref.shape[0]
    n_slab = obuf.shape[1]

    def tile_copy(stp):
        row_tile, col_tile = stp // n_col + 1, stp % n_col
        return pltpu.make_async_copy(
            obuf.at[stp % 2], o_hbm.at[pl.ds(col_tile * n_slab, n_slab), pl.ds(row_tile * tm, tm), :],
            sem_out.at[stp % 2])

    first_copy = pltpu.make_async_copy(first_ref, o_hbm.at[pl.ds(j * n_slab, n_slab), pl.ds(0, tm), :], sem_first.at[0])

    @pl.when(i == 0)
    def _():
        first_copy.start()

    @pl.when(j == 0)
    def _():
        xb_ref[...] = x_ref[...].astype(BF16)

    @pl.when(step >= 2)
    def _():
        tile_copy(step - 2).wait()

    _store_slabs(obuf.at[slot], jnp.dot(xb_ref[...], w_ref[...], preferred_element_type=F32))
    tile_copy(step).start()

    @pl.when(i == 0)
    def _():
        first_copy.wait()

    @pl.when(step == last)
    def _():
        tile_copy(step - 1).wait()
        tile_copy(step).wait()


def _project(x, xs, w, tm, tn_first, tn):
    m, k = x.shape
    ms = xs.shape[0]
    n = w.shape[1]
    assert (m // tm - 1) * (n // tn) >= 2
    proj_first, proj_s, w_bf16 = pl.pallas_call(
        _proj_first_kernel,
        grid=(n // tn_first,),
        in_specs=[pl.BlockSpec((tm, k), lambda j: (0, 0)),
                  pl.BlockSpec((ms, k), lambda j: (0, 0)),
                  pl.BlockSpec((k, tn_first), lambda j: (0, j))],
        out_specs=[pl.BlockSpec((tn_first // LANES, tm, LANES), lambda j: (j, 0, 0)),
                   pl.BlockSpec((tn_first // LANES, ms, LANES), lambda j: (j, 0, 0)),
                   pl.BlockSpec((k, tn_first), lambda j: (0, j))],
        out_shape=[jax.ShapeDtypeStruct((n // LANES, tm, LANES), F32),
                   jax.ShapeDtypeStruct((n // LANES, ms, LANES), F32),
                   jax.ShapeDtypeStruct((k, n), BF16)],
        scratch_shapes=[pltpu.VMEM((tm + ms, k), BF16)],
        compiler_params=_params(("arbitrary",)),
        name="in_proj_first",
    )(x, xs, w)
    proj = pl.pallas_call(
        _proj_rest_kernel,
        grid=(m // tm - 1, n // tn),
        in_specs=[pl.BlockSpec((tm, k), lambda i, j: (i + 1, 0)),
                  pl.BlockSpec((k, tn), lambda i, j: (0, j)),
                  pl.BlockSpec((tn // LANES, tm, LANES), lambda i, j: (jnp.where(i == 0, j, n // tn - 1), 0, 0))],
        out_specs=pl.BlockSpec(memory_space=pl.ANY),
        out_shape=jax.ShapeDtypeStruct((n // LANES, m, LANES), F32),
        scratch_shapes=[pltpu.VMEM((tm, k), BF16), pltpu.VMEM((2, tn // LANES, tm, LANES), F32),
                        pltpu.SemaphoreType.DMA((2,)), pltpu.SemaphoreType.DMA((1,))],
        compiler_params=_params(("arbitrary", "arbitrary")),
        name="in_proj",
    )(x, w_bf16, proj_first)
    return proj, proj_s


def _rope(x, cos, sin_signed):
    return x * cos + pltpu.roll(x, HEAD_DIM // 2, 1) * sin_signed


def _bias_from_count(count):
    return jnp.where(count > 1.5, 1.0, jnp.where(count > 0.5, 0.0, NEG_INF)).astype(F32)


def _block_deltas(block_gap):
    qi = lax.broadcasted_iota(jnp.int32, (BAND, BAND), 0)
    ki = lax.broadcasted_iota(jnp.int32, (BAND, BAND), 1)
    return block_gap * BAND + qi - ki


def _near_bias(block_gap):
    d = _block_deltas(block_gap)
    return _bias_from_count(((d >= 0) & (d <= BAND)).astype(F32))


def _class_bias(block_gap, far_step):
    d = _block_deltas(block_gap)
    near = (d >= 0) & (d <= BAND)
    far = (d >= 0) & ((d & (far_step - 1)) == 0)
    return _bias_from_count(near.astype(F32) + far.astype(F32))


def _softmax_stage(scores, floors=None):
    out = []
    for idx, s in enumerate(scores):
        m = jnp.broadcast_to(jnp.max(s, axis=-1, keepdims=True), (BAND, HEAD_DIM))
        if floors is not None:
            m = jnp.maximum(m, floors[idx])
        m_wide = jnp.concatenate([m] * (s.shape[1] // HEAD_DIM), axis=1)
        out.append((m, jnp.exp2(s - m_wide).astype(BF16)))
    return out


def _attn_kernel(q_ref, k_ref, v_ref, g_ref, cos_ref, sin_ref, z_ref, ko_ref, vo_ref,
                 qs_ref, qb_ref, kb_ref, vb_ref, qc_ref, kc_ref, vc_ref, acc_ref, m_ref, l_ref):
    seq = q_ref.shape[0]
    d_mid, d_far = DILATIONS[1], DILATIONS[2]
    far_step = d_far // d_mid
    cls_rows = seq // d_mid
    cos = cos_ref[...]
    sin = sin_ref[...]
    qs_ref[...] = _rope(q_ref[...], cos, sin) * ATTN_SCALE_LOG2
    ko_ref[...] = _rope(k_ref[...], cos, sin)
    vo_ref[...] = v_ref[...]
    qb_ref[...] = qs_ref[...].astype(BF16)
    kb_ref[...] = ko_ref[...].astype(BF16)
    ones = jnp.ones((seq, HEAD_DIM), BF16)
    vb_ref[:, :HEAD_DIM] = v_ref[...].astype(BF16)
    vb_ref[:, HEAD_DIM:] = ones
    vc_ref[:, HEAD_DIM:] = ones
    for r in range(d_mid):
        cls, dst = pl.ds(r, cls_rows, stride=d_mid), pl.ds(r * cls_rows, cls_rows)
        qc_ref[dst, :] = qs_ref[cls, :].astype(BF16)
        kc_ref[dst, :] = ko_ref[cls, :].astype(BF16)
        vc_ref[dst, :HEAD_DIM] = v_ref[cls, :].astype(BF16)

    n_cls_blk = cls_rows // BAND
    cls_bias = [_class_bias(gap, far_step) for gap in range(min(n_cls_blk, 3))]
    items = [(r, n) for r in range(d_mid) for n in range(n_cls_blk)]
    for g0 in range(0, len(items), ATTN_GROUP):
        group = items[g0:g0 + ATTN_GROUP]
        keys = [pl.ds(r * cls_rows, (n + 1) * BAND) for r, n in group]
        scores = [_dot_nt(qc_ref[pl.ds(r * cls_rows + n * BAND, BAND), :], kc_ref[kr, :])
                  + jnp.concatenate([cls_bias[min(n - nk, 2)] for nk in range(n + 1)], axis=1)
                  for (r, n), kr in zip(group, keys)]
        for (r, n), kr, (m, p) in zip(group, keys, _softmax_stage(scores)):
            rows = pl.ds(r + d_mid * BAND * n, BAND, stride=d_mid)
            pv = jnp.dot(p, vc_ref[kr, :], preferred_element_type=F32)
            acc_ref[rows, :] = pv[:, :HEAD_DIM]
            l_ref[rows, :] = pv[:, HEAD_DIM:]
            m_ref[rows, :] = m

    near_bias = [_near_bias(0), jnp.concatenate([_near_bias(1), _near_bias(0)], axis=1)]
    items = list(range(seq // BAND))
    for g0 in range(0, len(items), ATTN_GROUP):
        group = items[g0:g0 + ATTN_GROUP]
        rows = [pl.ds(n * BAND, BAND) for n in group]
        keys = [pl.ds(max(n - 1, 0) * BAND, BAND * min(n + 1, 2)) for n in group]
        scores = [_dot_nt(qb_ref[rw, :], kb_ref[kr, :]) + near_bias[min(n, 1)]
                  for n, rw, kr in zip(group, rows, keys)]
        probs = _softmax_stage(scores, floors=[m_ref[rw, :] for rw in rows])
        for rw, kr, (m, p) in zip(rows, keys, probs):
            w = jnp.exp2(m_ref[rw, :] - m)
            pv = jnp.dot(p, vb_ref[kr, :], preferred_element_type=F32)
            num = pv[:, :HEAD_DIM] + w * acc_ref[rw, :]
            den = pv[:, HEAD_DIM:] + w * l_ref[rw, :]
            z_ref[rw, :] = ((num / den) * _silu(g_ref[rw, :])).astype(z_ref.dtype)


def _prompt_attention(proj, cos, sin, bsz, seq):
    def col(g0):
        return pl.BlockSpec((None, seq, HEAD_DIM), lambda b, h: (g0 + h, b, 0))

    table = pl.BlockSpec((seq, HEAD_DIM), lambda b, h: (0, 0))
    kv_out = pl.BlockSpec((None, None, seq, HEAD_DIM), lambda b, h: (b, h, 0, 0))
    d_near, d_mid, d_far = DILATIONS
    far_step = d_far // d_mid
    assert d_near == 1 and d_far % d_mid == 0 and far_step & (far_step - 1) == 0
    assert all(w // d == BAND for w, d in zip(WINDOWS, DILATIONS)) and seq % (d_mid * BAND) == 0
    assert seq // d_mid <= far_step * BAND
    return pl.pallas_call(
        _attn_kernel,
        grid=(bsz, N_HEADS_ATTN),
        in_specs=[col(G_QA), col(G_KA), col(G_VA), col(G_GA), table, table],
        out_specs=[pl.BlockSpec((None, seq, HEAD_DIM), lambda b, h: (h, b, 0)), kv_out, kv_out],
        out_shape=[jax.ShapeDtypeStruct((N_HEADS_ATTN, bsz * seq, HEAD_DIM), BF16),
                   jax.ShapeDtypeStruct((bsz, N_HEADS_ATTN, seq, HEAD_DIM), F32),
                   jax.ShapeDtypeStruct((bsz, N_HEADS_ATTN, seq, HEAD_DIM), F32)],
        scratch_shapes=[pltpu.VMEM((seq, HEAD_DIM), F32)]
                       + [pltpu.VMEM((seq, HEAD_DIM), BF16), pltpu.VMEM((seq, HEAD_DIM), BF16),
                          pltpu.VMEM((seq, 2 * HEAD_DIM), BF16)] * 2
                       + [pltpu.VMEM((seq, HEAD_DIM), F32)] * 3,
        compiler_params=_params(("parallel", "parallel")),
        name="prompt_attn",
    )(proj, proj, proj, proj, cos, sin)


def _lower_bound(lb_raw, layer):
    e = jnp.exp(lb_raw - jnp.max(lb_raw, axis=0, keepdims=True))
    sm = e / jnp.sum(e, axis=0, keepdims=True)
    return jnp.sum(sm[:layer + 1], axis=0, keepdims=True)


def _split2(x):
    hi = x.astype(BF16)
    return hi, (x - hi.astype(F32)).astype(BF16)


def _rms_gate(o, norm_g, gate):
    o = o * lax.rsqrt(jnp.mean(o * o, axis=-1, keepdims=True) + RMS_EPS)
    return o * norm_g * _silu(gate)


def _hgrn_gates(fb, lb, tril):
    f = lb + (1.0 - lb) * _sigmoid(fb)
    g = jnp.log(f)
    b = sum(jnp.dot(tril, piece, preferred_element_type=F32) for piece in _split2(g))
    return 1.0 - f, b


def _hgrn_fast_prepare(q_ref, f_ref, lb, qh_ref, kh_ref, el_ref):
    n_heads, seq, _ = q_ref.shape
    blk = HGRN_PREP_ROWS
    tril = _tril_ones(blk)

    def body(bi, carry):
        r0 = pl.multiple_of(bi * blk, blk)
        fb = jnp.concatenate([f_ref[h, pl.ds(r0, blk), :] for h in range(n_heads)], axis=1)
        f = lb + (1.0 - lb) * _sigmoid(fb)
        b_blk = sum(jnp.dot(tril, piece, preferred_element_type=F32) for piece in _split2(jnp.log(f)))
        kk = 1.0 - f
        for c0 in range(0, blk, HGRN_CHUNK):
            rs = slice(c0, c0 + HGRN_CHUNK)
            b = b_blk[rs] - b_blk[c0 - 1:c0] if c0 else b_blk[rs]
            rows = pl.ds(r0 + c0, HGRN_CHUNK)
            q = jnp.concatenate([q_ref[h, rows, :] for h in range(n_heads)], axis=1)
            qh_ref[rows, :] = (q * jnp.exp(b)).astype(BF16)
            kh_ref[rows, :] = (kk[rs] * jnp.exp(-b)).astype(BF16)
            el_ref[pl.ds(bi * (blk // HGRN_CHUNK) + c0 // HGRN_CHUNK, 1), :] = jnp.exp(b[HGRN_CHUNK - 1:])
        return carry

    lax.fori_loop(0, seq // blk, body, 0, unroll=4)


def _hgrn_fast_scan(i_ref, g_ref, norm_g, qh_ref, kh_ref, el_ref, st_ref, z_ref):
    n_heads, seq, _ = i_ref.shape
    c = HGRN_CHUNK
    heads = range(n_heads)
    chunks = range(HGRN_SCAN_CHUNKS)
    lanes = [slice(h * HEAD_DIM, (h + 1) * HEAD_DIM) for h in heads]
    causal = lax.broadcasted_iota(jnp.int32, (c, c), 0) >= lax.broadcasted_iota(jnp.int32, (c, c), 1)
    tn = (((0,), (0,)), ((), ()))

    def body(ti, carry):
        rows = [pl.ds(pl.multiple_of((ti * HGRN_SCAN_CHUNKS + k) * c, c), c) for k in chunks]
        qh = [[qh_ref[rows[k], lanes[h]] for h in heads] for k in chunks]
        kh = [[kh_ref[rows[k], lanes[h]] for h in heads] for k in chunks]
        vb = [[i_ref[h, rows[k], :].astype(BF16) for h in heads] for k in chunks]
        att = [[jnp.where(causal, _dot_nt(qh[k][h], kh[k][h]), 0.0).astype(BF16) for h in heads] for k in chunks]
        ds = [[lax.dot_general(vb[k][h], kh[k][h], tn, preferred_element_type=F32) for h in heads] for k in chunks]
        st = [st_ref[h] for h in heads]
        o = []
        for k in chunks:
            el = el_ref[pl.ds(ti * HGRN_SCAN_CHUNKS + k, 1), :]
            o.append([_dot_nt(qh[k][h], st[h].astype(BF16)) for h in heads])
            st = [(st[h] + ds[k][h]) * el[:, lanes[h]] for h in heads]
        for h in heads:
            st_ref[h] = st[h]
        for k in chunks:
            for h in heads:
                o_kh = o[k][h] + jnp.dot(att[k][h], vb[k][h], preferred_element_type=F32)
                z_ref[h, rows[k], :] = _rms_gate(o_kh, norm_g[:, lanes[h]], g_ref[h, rows[k], :]).astype(z_ref.dtype)
        return carry

    lax.fori_loop(0, seq // (c * HGRN_SCAN_CHUNKS), body, 0)


def _hgrn_chunk(q, fb, v, lb, st, tril):
    c = q.shape[0]
    kk, b = _hgrn_gates(fb, lb, tril)
    o = _dot_nt((q * jnp.exp(b)).astype(BF16), st.astype(BF16))

    s_idx = lax.broadcasted_iota(jnp.int32, (c, 1), 0)
    lane = lax.broadcasted_iota(jnp.int32, (HGRN_SUB, c), 1)
    row = lax.broadcasted_iota(jnp.int32, (HGRN_SUB, c), 0)
    att_rows = []
    for i0 in range(0, c, HGRN_SUB):
        qi = q[i0:i0 + HGRN_SUB]
        bi = b[i0:i0 + HGRN_SUB]
        if i0 > 0:
            bref = b[i0 - 1:i0]
            kt = jnp.where(s_idx < i0, kk * jnp.exp(jnp.minimum(bref - b, 0.0)), 0.0)
            att = _dot_nt((qi * jnp.exp(bi - bref)).astype(BF16), kt.astype(BF16))
        else:
            att = jnp.zeros((HGRN_SUB, c), F32)
        for j in range(HGRN_SUB):
            s = i0 + j
            e = jnp.exp(jnp.minimum(bi - b[s:s + 1], 0.0))
            colv = jnp.sum(qi * kk[s:s + 1] * e, axis=-1, keepdims=True)
            att = jnp.where((lane == s) & (row >= j), colv, att)
        att_rows.append(att)
    att = jnp.concatenate(att_rows, axis=0)
    vb = v.astype(BF16)
    o = o + jnp.dot(att.astype(BF16), vb, preferred_element_type=F32)

    b_last = b[c - 1:c]
    kd = (kk * jnp.exp(b_last - b)).astype(BF16)
    st_new = st * jnp.exp(b_last) + lax.dot_general(vb, kd, (((0,), (0,)), ((), ())),
                                                    preferred_element_type=F32)
    return o, st_new


def _tril_ones(c):
    return (lax.broadcasted_iota(jnp.int32, (c, c), 0) >= lax.broadcasted_iota(jnp.int32, (c, c), 1)
            ).astype(BF16)


def _hgrn_kernel(layer, q_ref, f_ref, i_ref, g_ref, lb_ref, ng_ref, z_ref, s_ref, st_ref, qh_ref, kh_ref, el_ref):
    n_heads, seq, _ = q_ref.shape
    lb = _lower_bound(lb_ref[...], layer)
    norm_g = ng_ref[...]
    st_ref[...] = jnp.zeros_like(st_ref)

    def fast():
        _hgrn_fast_prepare(q_ref, f_ref, lb, qh_ref, kh_ref, el_ref)
        _hgrn_fast_scan(i_ref, g_ref, norm_g, qh_ref, kh_ref, el_ref, st_ref, z_ref)

    def safe():
        tril = _tril_ones(HGRN_CHUNK)

        def body(ci, carry):
            rows = pl.ds(pl.multiple_of(ci * HGRN_CHUNK, HGRN_CHUNK), HGRN_CHUNK)
            for h in range(n_heads):
                lanes = slice(h * HEAD_DIM, (h + 1) * HEAD_DIM)
                o, st = _hgrn_chunk(q_ref[h, rows, :], f_ref[h, rows, :], i_ref[h, rows, :], lb[:, lanes],
                                    st_ref[h], tril)
                st_ref[h] = st
                z_ref[h, rows, :] = _rms_gate(o, norm_g[:, lanes], g_ref[h, rows, :]).astype(z_ref.dtype)
            return carry

        lax.fori_loop(0, seq // HGRN_CHUNK, body, 0)

    fast_ok = HGRN_CHUNK * -jnp.log(jnp.min(lb)) <= HGRN_SAFE_LOG_RANGE
    lax.cond(fast_ok, fast, safe)
    for h in range(n_heads):
        s_ref[h] = st_ref[h].T


def _prompt_hgrn(proj, lb_raw, norm_g, layer, bsz, seq):
    hps = HGRN_HEADS_PER_STEP
    assert N_HEADS_HGRN % hps == 0 and all(g % hps == 0 for g in (G_QB, G_FB, G_IB, G_GB))
    assert seq % (HGRN_CHUNK * HGRN_SCAN_CHUNKS) == 0 and seq % HGRN_PREP_ROWS == 0 and HGRN_PREP_ROWS % HGRN_CHUNK == 0

    def cols(g0):
        return pl.BlockSpec((hps, seq, HEAD_DIM), lambda b, h: (g0 // hps + h, b, 0))

    return pl.pallas_call(
        functools.partial(_hgrn_kernel, layer),
        grid=(bsz, N_HEADS_HGRN // hps),
        in_specs=[cols(G_QB), cols(G_FB), cols(G_IB), cols(G_GB),
                  pl.BlockSpec((DEPTH + 1, hps * HEAD_DIM), lambda b, h: (0, h)),
                  pl.BlockSpec((1, hps * HEAD_DIM), lambda b, h: (0, h))],
        out_specs=[pl.BlockSpec((hps, seq, HEAD_DIM), lambda b, h: (h, b, 0)),
                   pl.BlockSpec((None, hps, HEAD_DIM, HEAD_DIM), lambda b, h: (b, h, 0, 0))],
        out_shape=[jax.ShapeDtypeStruct((N_HEADS_HGRN, bsz * seq, HEAD_DIM), BF16),
                   jax.ShapeDtypeStruct((bsz, N_HEADS_HGRN, HEAD_DIM, HEAD_DIM), F32)],
        scratch_shapes=[pltpu.VMEM((hps, HEAD_DIM, HEAD_DIM), F32),
                        pltpu.VMEM((seq, hps * HEAD_DIM), BF16),
                        pltpu.VMEM((seq, hps * HEAD_DIM), BF16),
                        pltpu.VMEM((seq // HGRN_CHUNK, hps * HEAD_DIM), F32)],
        compiler_params=_params(("parallel", "parallel")),
        name="prompt_hgrn",
    )(proj, proj, proj, proj, lb_raw, norm_g)


MEM_ROWS = 256
MEM_GROUP = 8


def _mem_kernel(q_ref, g_ref, mem_ref, w_ref, z_ref, mk_ref, mv_ref):
    seq = q_ref.shape[1]
    kv = jnp.dot(mem_ref[...].astype(BF16), w_ref[...].astype(BF16), preferred_element_type=F32)
    ones = jnp.ones((N_MEM, HEAD_DIM), BF16)
    mk_b, mv_b = [], []
    for h in range(N_HEADS_MEM):
        mk = kv[:, h * HEAD_DIM:(h + 1) * HEAD_DIM]
        mv = kv[:, W_MEM + h * HEAD_DIM:W_MEM + (h + 1) * HEAD_DIM]
        mk_ref[:, h, :] = mk
        mv_ref[:, h, :] = mv
        mk_b.append(mk.astype(BF16))
        mv_b.append(jnp.concatenate([mv.astype(BF16), ones], axis=1))

    items = [(h, pl.ds(n * MEM_ROWS, MEM_ROWS)) for h in range(N_HEADS_MEM) for n in range(seq // MEM_ROWS)]
    for g0 in range(0, len(items), MEM_GROUP):
        group = items[g0:g0 + MEM_GROUP]
        scores = [_dot_nt((q_ref[h, rows, :] * ATTN_SCALE_LOG2).astype(BF16), mk_b[h]) for h, rows in group]
        probs = [jnp.exp2(s - jnp.max(s, axis=-1, keepdims=True)).astype(BF16) for s in scores]
        for (h, rows), p in zip(group, probs):
            pv = jnp.dot(p, mv_b[h], preferred_element_type=F32)
            o = pv[:, :HEAD_DIM] / pv[:, HEAD_DIM:]
            z_ref[h, rows, :] = (o * _silu(g_ref[h, rows, :])).astype(z_ref.dtype)


def _prompt_mem(proj, mem, w_kv, bsz, seq):
    assert G_QM % N_HEADS_MEM == 0 and G_GM % N_HEADS_MEM == 0

    def cols(g0):
        return pl.BlockSpec((N_HEADS_MEM, seq, HEAD_DIM), lambda b: (g0 // N_HEADS_MEM, b, 0))

    kv_out = pl.BlockSpec((None, N_MEM, N_HEADS_MEM, HEAD_DIM), lambda b: (b, 0, 0, 0))
    kv_shape = jax.ShapeDtypeStruct((bsz, N_MEM, N_HEADS_MEM, HEAD_DIM), F32)
    return pl.pallas_call(
        _mem_kernel,
        grid=(bsz,),
        in_specs=[cols(G_QM), cols(G_GM), pl.BlockSpec((N_MEM, D_MODEL), lambda b: (b, 0)),
                  pl.BlockSpec((D_MODEL, 2 * W_MEM), lambda b: (0, 0))],
        out_specs=[pl.BlockSpec((N_HEADS_MEM, seq, HEAD_DIM), lambda b: (0, b, 0)), kv_out, kv_out],
        out_shape=[jax.ShapeDtypeStruct((N_HEADS_MEM, bsz * seq, HEAD_DIM), BF16), kv_shape, kv_shape],
        compiler_params=_params(("parallel",)),
        name="prompt_mem",
    )(proj, proj, mem, w_kv)


def _merge_kernel(cast_w, za_ref, zh_ref, zm_ref, x_ref, w_ref, lg_ref, lb_ref, o_ref, *rest):
    if cast_w:
        wb_ref, z_ref, stage, sem = rest
        n_rows = stage.shape[1]
        n_chunks = w_ref.shape[0] // n_rows

        def chunk_copy(c):
            return pltpu.make_async_copy(w_ref.at[pl.ds(c * n_rows, n_rows), :], stage.at[c % 2], sem.at[c % 2])

        @pl.when(pl.program_id(0) == 0)
        def _():
            chunk_copy(0).start()
            for c in range(n_chunks):
                if c + 1 < n_chunks:
                    chunk_copy(c + 1).start()
                chunk_copy(c).wait()
                wb_ref[c * n_rows:(c + 1) * n_rows, :] = stage[c % 2].astype(BF16)
    else:
        wb_ref, (z_ref,) = w_ref, rest

    c0 = 0
    for ref in (za_ref, zh_ref, zm_ref):
        for c in range(ref.shape[0]):
            z_ref[:, (c0 + c) * LANES:(c0 + c + 1) * LANES] = ref[c].astype(BF16)
        c0 += ref.shape[0]
    y = jnp.dot(z_ref[...], wb_ref[...], preferred_element_type=F32)
    r = DEEPNORM_ALPHA * x_ref[...] + y
    mu = jnp.mean(r, axis=-1, keepdims=True)
    d = r - mu
    var = jnp.mean(d * d, axis=-1, keepdims=True)
    o_ref[...] = d * lax.rsqrt(var + LN_EPS) * lg_ref[...] + lb_ref[...]


def _merge(za, zh, zm, x, w_out, ln_g, ln_b, tm):
    m = x.shape[0]
    cast_w = w_out.dtype != BF16

    def slab(a):
        return pl.BlockSpec((a.shape[0], tm, LANES), lambda i: (0, i, 0))

    const = lambda shape: pl.BlockSpec(shape, lambda i: (0, 0))
    y_spec, y_shape = pl.BlockSpec((tm, D_MODEL), lambda i: (i, 0)), jax.ShapeDtypeStruct((m, D_MODEL), F32)
    w_block = const((MIX_WIDTH, D_MODEL))
    z_scratch = pltpu.VMEM((tm, MIX_WIDTH), BF16)
    if cast_w:
        w_spec = pl.BlockSpec(memory_space=pl.ANY)
        out_specs, out_shape = [y_spec, w_block], [y_shape, jax.ShapeDtypeStruct((MIX_WIDTH, D_MODEL), BF16)]
        scratch = [z_scratch, pltpu.VMEM((2, MERGE_CAST_ROWS, D_MODEL), F32), pltpu.SemaphoreType.DMA((2,))]
    else:
        w_spec, out_specs, out_shape, scratch = w_block, y_spec, y_shape, [z_scratch]
    return pl.pallas_call(
        functools.partial(_merge_kernel, cast_w),
        grid=(m // tm,),
        in_specs=[slab(za), slab(zh), slab(zm), pl.BlockSpec((tm, D_MODEL), lambda i: (i, 0)),
                  w_spec, const((1, D_MODEL)), const((1, D_MODEL))],
        out_specs=out_specs,
        out_shape=out_shape,
        scratch_shapes=scratch,
        compiler_params=_params(("arbitrary",)),
        name="merge",
    )(za, zh, zm, x, w_out, ln_g, ln_b)


def _column(row):
    return jnp.broadcast_to(row, (HEAD_DIM, HEAD_DIM)).T


HBM_TILE_ROWS = 8
SAMPLE_ROWS = 4


def _window_pieces(past):
    pieces = []
    for dil in DILATIONS:
        if dil == 1:
            pieces.append((0, past - BAND, BAND, None))
        elif dil < HBM_TILE_ROWS:
            n = BAND * dil // HBM_TILE_ROWS
            pieces += [(1, past // HBM_TILE_ROWS - n, n, r) for r in range(0, HBM_TILE_ROWS, dil)]
        else:
            pieces.append((2, past // dil - BAND, BAND, 0))
    return pieces


def _cache_views(cache):
    depth, bsz, past, nh, hd = cache.shape
    assert all(d == 1 or HBM_TILE_ROWS % d == 0 or d % HBM_TILE_ROWS == 0 for d in DILATIONS)
    big = max(DILATIONS)
    hm = cache.transpose(0, 1, 3, 2, 4)
    return (hm, hm.reshape(depth, bsz, nh, past // HBM_TILE_ROWS, HBM_TILE_ROWS, hd),
            hm.reshape(depth, bsz, nh, past // big, big, hd))


def _sample_row(layer, p_ref, cos, sin, k_win, v_win, st_ref, mk_ref, mv_ref, lb_all, ng_ref, z_ref, ko_ref, vo_ref,
                so_ref):
    q_all = _rope(p_ref[G_QA:G_QA + N_HEADS_ATTN, :], cos, sin) * ATTN_SCALE
    k_all = _rope(p_ref[G_KA:G_KA + N_HEADS_ATTN, :], cos, sin)
    v_all = p_ref[G_VA:G_VA + N_HEADS_ATTN, :]
    ko_ref[...] = k_all
    vo_ref[...] = v_all

    for h in range(N_HEADS_ATTN):
        q = q_all[h:h + 1]
        s_new = jnp.sum(q * k_all[h:h + 1], axis=-1, keepdims=True)
        s = jnp.sum(k_win[h] * q, axis=-1, keepdims=True)
        m = jnp.maximum(jnp.max(s, axis=0, keepdims=True), s_new)
        p = jnp.exp(s - m)
        p_new = jnp.exp(s_new - m) * len(DILATIONS)
        den = jnp.sum(p, axis=0, keepdims=True) + p_new
        num = jnp.sum(p * v_win[h], axis=0, keepdims=True) + p_new * v_all[h:h + 1]
        z_ref[h:h + 1, :] = (num / den) * _silu(p_ref[G_GA + h:G_GA + h + 1, :])

    for h in range(N_HEADS_HGRN):
        lanes = slice(h * HEAD_DIM, (h + 1) * HEAD_DIM)
        lb = _lower_bound(lb_all[:, lanes], layer)
        f = lb + (1.0 - lb) * _sigmoid(p_ref[G_FB + h:G_FB + h + 1, :])
        f_col = _column(f)
        q_col = _column(p_ref[G_QB + h:G_QB + h + 1, :])
        s_new = f_col * st_ref[h] + (1.0 - f_col) * p_ref[G_IB + h:G_IB + h + 1, :]
        so_ref[h] = s_new
        o = jnp.sum(s_new * q_col, axis=0, keepdims=True)
        z_ref[N_HEADS_ATTN + h:N_HEADS_ATTN + h + 1, :] = _rms_gate(
            o, ng_ref[:, lanes], p_ref[G_GB + h:G_GB + h + 1, :])

    fold = mk_ref.shape[1] // N_HEADS_MEM
    unfold = lambda a: sum(a[i * N_HEADS_MEM:(i + 1) * N_HEADS_MEM] for i in range(fold))
    q_m = p_ref[G_QM:G_QM + N_HEADS_MEM, :] * ATTN_SCALE
    s = jnp.sum(mk_ref[...] * jnp.concatenate([q_m] * fold, axis=0)[None], axis=-1, keepdims=True)
    m = jnp.max(s, axis=0)
    m = functools.reduce(jnp.maximum, [m[i * N_HEADS_MEM:(i + 1) * N_HEADS_MEM] for i in range(fold)])
    p = jnp.exp(s - jnp.concatenate([m] * fold, axis=0)[None])
    o = unfold(jnp.sum(p * mv_ref[...], axis=0)) / unfold(jnp.sum(p, axis=0))
    row = N_HEADS_ATTN + N_HEADS_HGRN
    z_ref[row:row + N_HEADS_MEM, :] = o * _silu(p_ref[G_GM:G_GM + N_HEADS_MEM, :])


def _sample_kernel(layer, past, p_ref, cos_ref, sin_ref, k0_hbm, k1_hbm, k2_hbm, v0_hbm, v1_hbm, v2_hbm,
                   st_ref, mk_ref, mv_ref, lb_ref, ng_ref, z_ref, ko_ref, vo_ref, so_ref, kbuf, vbuf, sem):
    step = pl.program_id(0)
    slot = step % 2
    n_rows = p_ref.shape[0]
    pieces = _window_pieces(past)

    def window_copies(stp, sl):
        out = []
        for rr in range(n_rows):
            for ci, (views, buf) in enumerate((((k0_hbm, k1_hbm, k2_hbm), kbuf), ((v0_hbm, v1_hbm, v2_hbm), vbuf))):
                off = 0
                for pi, (vi, start, count, res) in enumerate(pieces):
                    view, row = views[vi], stp * n_rows + rr
                    src = (view.at[layer, row, :, pl.ds(start, count), :] if res is None
                           else view.at[layer, row, :, pl.ds(start, count), res, :])
                    out.append(pltpu.make_async_copy(src, buf.at[sl, rr, :, pl.ds(off, count), :],
                                                     sem.at[sl, rr, ci, pi]))
                    off += count
        return out

    @pl.when(step == 0)
    def _():
        for c in window_copies(step, slot):
            c.start()

    @pl.when(step + 1 < pl.num_programs(0))
    def _():
        for c in window_copies(step + 1, 1 - slot):
            c.start()

    for c in window_copies(step, slot):
        c.wait()

    cos, sin, lb_all = cos_ref[...], sin_ref[...], lb_ref[...]
    for rr in range(n_rows):
        _sample_row(layer, p_ref.at[rr], cos, sin, kbuf.at[slot, rr], vbuf.at[slot, rr], st_ref.at[rr], mk_ref.at[rr],
                    mv_ref.at[rr], lb_all, ng_ref, z_ref.at[rr], ko_ref.at[rr], vo_ref.at[rr], so_ref.at[rr])


def _sample_mixers(proj_rows, cos, sin, win_k, win_v, state, mem_k, mem_v, lb_raw, norm_g, layer):
    bsz, past = win_k.shape[1:3]
    rps = SAMPLE_ROWS
    assert all(past % d == 0 and past >= w for w, d in zip(WINDOWS, DILATIONS)) and bsz % rps == 0
    assert all(w // d == BAND for w, d in zip(WINDOWS, DILATIONS)) and past % HBM_TILE_ROWS == 0
    n_keys = BAND * len(DILATIONS)
    n_pieces = len(_window_pieces(past))
    any_spec = pl.BlockSpec(memory_space=pl.ANY)
    vec = lambda n: pl.BlockSpec((n, W_HGRN), lambda b: (0, 0))
    heads = lambda n: pl.BlockSpec((rps, n, HEAD_DIM), lambda b: (b, 0, 0))
    fold = HBM_TILE_ROWS // N_HEADS_MEM
    mem_k, mem_v = (a.reshape(a.shape[0], bsz, N_MEM // fold, fold * N_HEADS_MEM, HEAD_DIM) for a in (mem_k, mem_v))
    mem_spec = pl.BlockSpec((None, rps, N_MEM // fold, fold * N_HEADS_MEM, HEAD_DIM), lambda b: (layer, b, 0, 0, 0))
    return pl.pallas_call(
        functools.partial(_sample_kernel, layer, past),
        grid=(bsz // rps,),
        in_specs=[heads(N_GROUPS), pl.BlockSpec((1, HEAD_DIM), lambda b: (0, 0)),
                  pl.BlockSpec((1, HEAD_DIM), lambda b: (0, 0)), *([any_spec] * 6),
                  pl.BlockSpec((None, rps, N_HEADS_HGRN, HEAD_DIM, HEAD_DIM), lambda b: (layer, b, 0, 0, 0)),
                  mem_spec, mem_spec, vec(DEPTH + 1), vec(1)],
        out_specs=[heads(MIX_WIDTH // HEAD_DIM), heads(N_HEADS_ATTN), heads(N_HEADS_ATTN),
                   pl.BlockSpec((rps, N_HEADS_HGRN, HEAD_DIM, HEAD_DIM), lambda b: (b, 0, 0, 0))],
        out_shape=[jax.ShapeDtypeStruct((bsz, MIX_WIDTH // HEAD_DIM, HEAD_DIM), F32),
                   jax.ShapeDtypeStruct((bsz, N_HEADS_ATTN, HEAD_DIM), F32),
                   jax.ShapeDtypeStruct((bsz, N_HEADS_ATTN, HEAD_DIM), F32),
                   jax.ShapeDtypeStruct((bsz, N_HEADS_HGRN, HEAD_DIM, HEAD_DIM), F32)],
        scratch_shapes=[pltpu.VMEM((2, rps, N_HEADS_ATTN, n_keys, HEAD_DIM), F32),
                        pltpu.VMEM((2, rps, N_HEADS_ATTN, n_keys, HEAD_DIM), F32),
                        pltpu.SemaphoreType.DMA((2, rps, 2, n_pieces))],
        compiler_params=_params(("arbitrary",)),
        name="sample_mixers",
    )(proj_rows, cos, sin, *_cache_views(win_k), *_cache_views(win_v), state, mem_k, mem_v, lb_raw, norm_g)


def _rope_tables(pos):
    half = HEAD_DIM // 2
    inv_freq = 1.0 / (ROPE_THETA ** (np.arange(half, dtype=np.float64) / half))
    ang = np.asarray(pos, np.float64)[:, None] * inv_freq[None, :]
    cos, sin = np.cos(ang), np.sin(ang)
    return (jnp.asarray(np.concatenate([cos, cos], axis=-1), F32),
            jnp.asarray(np.concatenate([-sin, sin], axis=-1), F32))


def kernel(x_prompt, x_sample, cache_win_k, cache_win_v, state_hgrn, cache_mem_k, cache_mem_v, mem_prompt,
           w_in, w_mem_kv, hgrn_lb_raw, hgrn_norm_g, w_out, ln_g, ln_b):
    bsz, seq, _ = x_prompt.shape
    dbsz, n_new, _ = x_sample.shape
    assert n_new == 1 and seq % (BAND * max(DILATIONS)) == 0
    cos_p, sin_p = _rope_tables(np.arange(seq))
    cos_s, sin_s = _rope_tables(PAST_LEN + np.arange(n_new))

    hp = x_prompt.reshape(bsz * seq, D_MODEL)
    hs = x_sample.reshape(dbsz * n_new, D_MODEL)
    mem = mem_prompt.reshape(bsz * N_MEM, D_MODEL)
    outs = [[] for _ in range(8)]
    for layer in range(DEPTH):
        lb_raw = hgrn_lb_raw
        norm_g = hgrn_norm_g[layer][None]
        lg, lbias = ln_g[layer][None], ln_b[layer][None]

        proj, proj_s = _project(hp, hs, w_in[layer], PROJ_ROWS, PROJ_FIRST_COLS, PROJ_COLS)
        za, k1, v1 = _prompt_attention(proj, cos_p, sin_p, bsz, seq)
        zh, s1 = _prompt_hgrn(proj, lb_raw, norm_g, layer, bsz, seq)
        zm, mk1, mv1 = _prompt_mem(proj, mem, w_mem_kv[layer], bsz, seq)
        hp, w_out_b = _merge(za, zh, zm, hp, w_out[layer], lg, lbias, MERGE_ROWS)

        proj_s = proj_s.transpose(1, 0, 2)
        zs, k2, v2, s2 = _sample_mixers(proj_s, cos_s, sin_s, cache_win_k, cache_win_v, state_hgrn,
                                        cache_mem_k, cache_mem_v, lb_raw, norm_g, layer)
        zs = zs.transpose(1, 0, 2)
        hs = _merge(zs[:N_HEADS_ATTN], zs[N_HEADS_ATTN:N_HEADS_ATTN + N_HEADS_HGRN],
                    zs[N_HEADS_ATTN + N_HEADS_HGRN:], hs, w_out_b, lg, lbias, dbsz)

        new = (k1.transpose(0, 2, 1, 3), v1.transpose(0, 2, 1, 3), s1, mk1, mv1,
               k2.reshape(dbsz, n_new, N_HEADS_ATTN, HEAD_DIM), v2.reshape(dbsz, n_new, N_HEADS_ATTN, HEAD_DIM),
               s2.astype(state_hgrn.dtype))
        for acc, val in zip(outs, new):
            acc.append(val)

    return (hp.reshape(bsz, seq, D_MODEL), hs.reshape(dbsz, n_new, D_MODEL), *[jnp.stack(o) for o in outs])
```

```python
import functools

import jax
import jax.numpy as jnp
import numpy as np
from jax import lax
from jax.experimental import pallas as pl
from jax.experimental.pallas import tpu as pltpu

F32 = jnp.float32
BF16 = jnp.bfloat16

D_MODEL = 2048
DEPTH = 1
PAST_LEN = 8192
HEAD_DIM = 128
N_HEADS_ATTN = 6
N_HEADS_HGRN = 6
N_HEADS_MEM = 4
W_ATTN = N_HEADS_ATTN * HEAD_DIM
W_HGRN = N_HEADS_HGRN * HEAD_DIM
W_MEM = N_HEADS_MEM * HEAD_DIM
MIX_WIDTH = W_ATTN + W_HGRN + W_MEM
WINDOWS = (128, 512, 2048)
DILATIONS = (1, 4, 16)
N_MEM = 256
ROPE_THETA = 10000.0
LN_EPS = 1e-5
RMS_EPS = 1e-6
NEG_INF = -1e30
DEEPNORM_ALPHA = (2 * DEPTH) ** 0.25
ATTN_SCALE = HEAD_DIM ** -0.5
ATTN_SCALE_LOG2 = ATTN_SCALE * 1.4426950408889634

G_QA, G_KA, G_VA, G_GA = 0, 6, 12, 18
G_QB, G_FB, G_IB, G_GB = 24, 30, 36, 42
G_QM, G_GM = 48, 52
N_GROUPS = 56

LANES = 128
BAND = 128
ATTN_GROUP = 8
HGRN_CHUNK = 64
HGRN_SUB = 16
HGRN_HEADS_PER_STEP = 3
HGRN_PREP_ROWS = 256
HGRN_SCAN_CHUNKS = 16
HGRN_SAFE_LOG_RANGE = 70.0
VMEM_LIMIT = 48 * 1024 * 1024
PROJ_ROWS, PROJ_COLS, PROJ_FIRST_COLS = 1024, 1024, 512
MERGE_ROWS = 512
MERGE_CAST_ROWS = 256


def _params(sem, vmem=VMEM_LIMIT):
    return pltpu.CompilerParams(dimension_semantics=sem, vmem_limit_bytes=vmem)


def _sigmoid(x):
    return 1.0 / (1.0 + jnp.exp(-x))


def _silu(x):
    return x * _sigmoid(x)


def _dot_nt(a, b):
    return lax.dot_general(a, b, (((1,), (1,)), ((), ())), preferred_element_type=F32)


def _store_slabs(ref, acc):
    for c in range(ref.shape[0]):
        ref[c] = acc[:, c * LANES:(c + 1) * LANES]


def _proj_first_kernel(x_ref, xs_ref, w_ref, o_ref, os_ref, wb_ref, xb_ref):
    tm = x_ref.shape[0]

    @pl.when(pl.program_id(0) == 0)
    def _():
        xb_ref[:tm, :] = x_ref[...].astype(BF16)
        xb_ref[tm:, :] = xs_ref[...].astype(BF16)

    wb_ref[...] = w_ref[...].astype(BF16)
    acc = jnp.dot(xb_ref[...], wb_ref[...], preferred_element_type=F32)
    _store_slabs(o_ref, acc[:tm])
    _store_slabs(os_ref, acc[tm:])


def _proj_rest_kernel(x_ref, w_ref, first_ref, o_hbm, xb_ref, obuf, sem_out, sem_first):
    i, j = pl.program_id(0), pl.program_id(1)
    n_col = pl.num_programs(1)
    step = i * n_col + j
    last = pl.num_programs(0) * n_col - 1
    slot = step % 2
    tm = x_ref.shape[0]
    n_slab = obuf.shape[1]

    def tile_copy(stp):
        row_tile, col_tile = stp // n_col + 1, stp % n_col
        return pltpu.make_async_copy(
            obuf.at[stp % 2], o_hbm.at[pl.ds(col_tile * n_slab, n_slab), pl.ds(row_tile * tm, tm), :],
            sem_out.at[stp % 2])

    first_copy = pltpu.make_async_copy(first_ref, o_hbm.at[pl.ds(j * n_slab, n_slab), pl.ds(0, tm), :], sem_first.at[0])

    @pl.when(i == 0)
    def _():
        first_copy.start()

    @pl.when(j == 0)
    def _():
        xb_ref[...] = x_ref[...].astype(BF16)

    @pl.when(step >= 2)
    def _():
        tile_copy(step - 2).wait()

    _store_slabs(obuf.at[slot], jnp.dot(xb_ref[...], w_ref[...], preferred_element_type=F32))
    tile_copy(step).start()

    @pl.when(i == 0)
    def _():
        first_copy.wait()

    @pl.when(step == last)
    def _():
        tile_copy(step - 1).wait()
        tile_copy(step).wait()


def _project(x, xs, w, tm, tn_first, tn):
    m, k = x.shape
    ms = xs.shape[0]
    n = w.shape[1]
    assert (m // tm - 1) * (n // tn) >= 2
    proj_first, proj_s, w_bf16 = pl.pallas_call(
        _proj_first_kernel,
        grid=(n // tn_first,),
        in_specs=[pl.BlockSpec((tm, k), lambda j: (0, 0)),
                  pl.BlockSpec((ms, k), lambda j: (0, 0)),
                  pl.BlockSpec((k, tn_first), lambda j: (0, j))],
        out_specs=[pl.BlockSpec((tn_first // LANES, tm, LANES), lambda j: (j, 0, 0)),
                   pl.BlockSpec((tn_first // LANES, ms, LANES), lambda j: (j, 0, 0)),
                   pl.BlockSpec((k, tn_first), lambda j: (0, j))],
        out_shape=[jax.ShapeDtypeStruct((n // LANES, tm, LANES), F32),
                   jax.ShapeDtypeStruct((n // LANES, ms, LANES), F32),
                   jax.ShapeDtypeStruct((k, n), BF16)],
        scratch_shapes=[pltpu.VMEM((tm + ms, k), BF16)],
        compiler_params=_params(("arbitrary",)),
        name="in_proj_first",
    )(x, xs, w)
    proj = pl.pallas_call(
        _proj_rest_kernel,
        grid=(m // tm - 1, n // tn),
        in_specs=[pl.BlockSpec((tm, k), lambda i, j: (i + 1, 0)),
                  pl.BlockSpec((k, tn), lambda i, j: (0, j)),
                  pl.BlockSpec((tn // LANES, tm, LANES), lambda i, j: (jnp.where(i == 0, j, n // tn - 1), 0, 0))],
        out_specs=pl.BlockSpec(memory_space=pl.ANY),
        out_shape=jax.ShapeDtypeStruct((n // LANES, m, LANES), F32),
        scratch_shapes=[pltpu.VMEM((tm, k), BF16), pltpu.VMEM((2, tn // LANES, tm, LANES), F32),
                        pltpu.SemaphoreType.DMA((2,)), pltpu.SemaphoreType.DMA((1,))],
        compiler_params=_params(("arbitrary", "arbitrary")),
        name="in_proj",
    )(x, w_bf16, proj_first)
    return proj, proj_s


def _rope(x, cos, sin_signed):
    return x * cos + pltpu.roll(x, HEAD_DIM // 2, 1) * sin_signed


def _bias_from_count(count):
    return jnp.where(count > 1.5, 1.0, jnp.where(count > 0.5, 0.0, NEG_INF)).astype(F32)


def _block_deltas(block_gap):
    qi = lax.broadcasted_iota(jnp.int32, (BAND, BAND), 0)
    ki = lax.broadcasted_iota(jnp.int32, (BAND, BAND), 1)
    return block_gap * BAND + qi - ki


def _near_bias(block_gap):
    d = _block_deltas(block_gap)
    return _bias_from_count(((d >= 0) & (d <= BAND)).astype(F32))


def _class_bias(block_gap, far_step):
    d = _block_deltas(block_gap)
    near = (d >= 0) & (d <= BAND)
    far = (d >= 0) & ((d & (far_step - 1)) == 0)
    return _bias_from_count(near.astype(F32) + far.astype(F32))


def _softmax_stage(scores, floors=None):
    out = []
    for idx, s in enumerate(scores):
        m = jnp.broadcast_to(jnp.max(s, axis=-1, keepdims=True), (BAND, HEAD_DIM))
        if floors is not None:
            m = jnp.maximum(m, floors[idx])
        m_wide = jnp.concatenate([m] * (s.shape[1] // HEAD_DIM), axis=1)
        out.append((m, jnp.exp2(s - m_wide).astype(BF16)))
    return out


def _attn_kernel(q_ref, k_ref, v_ref, g_ref, cos_ref, sin_ref, z_ref, ko_ref, vo_ref,
                 qs_ref, qb_ref, kb_ref, vb_ref, qc_ref, kc_ref, vc_ref, acc_ref, m_ref, l_ref):
    seq = q_ref.shape[0]
    d_mid, d_far = DILATIONS[1], DILATIONS[2]
    far_step = d_far // d_mid
    cls_rows = seq // d_mid
    cos = cos_ref[...]
    sin = sin_ref[...]
    qs_ref[...] = _rope(q_ref[...], cos, sin) * ATTN_SCALE_LOG2
    ko_ref[...] = _rope(k_ref[...], cos, sin)
    vo_ref[...] = v_ref[...]
    qb_ref[...] = qs_ref[...].astype(BF16)
    kb_ref[...] = ko_ref[...].astype(BF16)
    ones = jnp.ones((seq, HEAD_DIM), BF16)
    vb_ref[:, :HEAD_DIM] = v_ref[...].astype(BF16)
    vb_ref[:, HEAD_DIM:] = ones
    vc_ref[:, HEAD_DIM:] = ones
    for r in range(d_mid):
        cls, dst = pl.ds(r, cls_rows, stride=d_mid), pl.ds(r * cls_rows, cls_rows)
        qc_ref[dst, :] = qs_ref[cls, :].astype(BF16)
        kc_ref[dst, :] = ko_ref[cls, :].astype(BF16)
        vc_ref[dst, :HEAD_DIM] = v_ref[cls, :].astype(BF16)

    n_cls_blk = cls_rows // BAND
    cls_bias = [_class_bias(gap, far_step) for gap in range(min(n_cls_blk, 3))]
    items = [(r, n) for r in range(d_mid) for n in range(n_cls_blk)]
    for g0 in range(0, len(items), ATTN_GROUP):
        group = items[g0:g0 + ATTN_GROUP]
        keys = [pl.ds(r * cls_rows, (n + 1) * BAND) for r, n in group]
        scores = [_dot_nt(qc_ref[pl.ds(r * cls_rows + n * BAND, BAND), :], kc_ref[kr, :])
                  + jnp.concatenate([cls_bias[min(n - nk, 2)] for nk in range(n + 1)], axis=1)
                  for (r, n), kr in zip(group, keys)]
        for (r, n), kr, (m, p) in zip(group, keys, _softmax_stage(scores)):
            rows = pl.ds(r + d_mid * BAND * n, BAND, stride=d_mid)
            pv = jnp.dot(p, vc_ref[kr, :], preferred_element_type=F32)
            acc_ref[rows, :] = pv[:, :HEAD_DIM]
            l_ref[rows, :] = pv[:, HEAD_DIM:]
            m_ref[rows, :] = m

    near_bias = [_near_bias(0), jnp.concatenate([_near_bias(1), _near_bias(0)], axis=1)]
    items = list(range(seq // BAND))
    for g0 in range(0, len(items), ATTN_GROUP):
        group = items[g0:g0 + ATTN_GROUP]
        rows = [pl.ds(n * BAND, BAND) for n in group]
        keys = [pl.ds(max(n - 1, 0) * BAND, BAND * min(n + 1, 2)) for n in group]
        scores = [_dot_nt(qb_ref[rw, :], kb_ref[kr, :]) + near_bias[min(n, 1)]
                  for n, rw, kr in zip(group, rows, keys)]
        probs = _softmax_stage(scores, floors=[m_ref[rw, :] for rw in rows])
        for rw, kr, (m, p) in zip(rows, keys, probs):
            w = jnp.exp2(m_ref[rw, :] - m)
            pv = jnp.dot(p, vb_ref[kr, :], preferred_element_type=F32)
            num = pv[:, :HEAD_DIM] + w * acc_ref[rw, :]
            den = pv[:, HEAD_DIM:] + w * l_ref[rw, :]
            z_ref[rw, :] = ((num / den) * _silu(g_ref[rw, :])).astype(z_ref.dtype)


def _prompt_attention(proj, cos, sin, bsz, seq):
    def col(g0):
        return pl.BlockSpec((None, seq, HEAD_DIM), lambda b, h: (g0 + h, b, 0))

    table = pl.BlockSpec((seq, HEAD_DIM), lambda b, h: (0, 0))
    kv_out = pl.BlockSpec((None, None, seq, HEAD_DIM), lambda b, h: (b, h, 0, 0))
    d_near, d_mid, d_far = DILATIONS
    far_step = d_far // d_mid
    assert d_near == 1 and d_far % d_mid == 0 and far_step & (far_step - 1) == 0
    assert all(w // d == BAND for w, d in zip(WINDOWS, DILATIONS)) and seq % (d_mid * BAND) == 0
    assert seq // d_mid <= far_step * BAND
    return pl.pallas_call(
        _attn_kernel,
        grid=(bsz, N_HEADS_ATTN),
        in_specs=[col(G_QA), col(G_KA), col(G_VA), col(G_GA), table, table],
        out_specs=[pl.BlockSpec((None, seq, HEAD_DIM), lambda b, h: (h, b, 0)), kv_out, kv_out],
        out_shape=[jax.ShapeDtypeStruct((N_HEADS_ATTN, bsz * seq, HEAD_DIM), BF16),
                   jax.ShapeDtypeStruct((bsz, N_HEADS_ATTN, seq, HEAD_DIM), F32),
                   jax.ShapeDtypeStruct((bsz, N_HEADS_ATTN, seq, HEAD_DIM), F32)],
        scratch_shapes=[pltpu.VMEM((seq, HEAD_DIM), F32)]
                       + [pltpu.VMEM((seq, HEAD_DIM), BF16), pltpu.VMEM((seq, HEAD_DIM), BF16),
                          pltpu.VMEM((seq, 2 * HEAD_DIM), BF16)] * 2
                       + [pltpu.VMEM((seq, HEAD_DIM), F32)] * 3,
        compiler_params=_params(("parallel", "parallel")),
        name="prompt_attn",
    )(proj, proj, proj, proj, cos, sin)


def _lower_bound(lb_raw, layer):
    e = jnp.exp(lb_raw - jnp.max(lb_raw, axis=0, keepdims=True))
    sm = e / jnp.sum(e, axis=0, keepdims=True)
    return jnp.sum(sm[:layer + 1], axis=0, keepdims=True)


def _split2(x):
    hi = x.astype(BF16)
    return hi, (x - hi.astype(F32)).astype(BF16)


def _rms_gate(o, norm_g, gate):
    o = o * lax.rsqrt(jnp.mean(o * o, axis=-1, keepdims=True) + RMS_EPS)
    return o * norm_g * _silu(gate)


def _hgrn_gates(fb, lb, tril):
    f = lb + (1.0 - lb) * _sigmoid(fb)
    g = jnp.log(f)
    b = sum(jnp.dot(tril, piece, preferred_element_type=F32) for piece in _split2(g))
    return 1.0 - f, b


def _hgrn_fast_prepare(q_ref, f_ref, lb, qh_ref, kh_ref, el_ref):
    n_heads, seq, _ = q_ref.shape
    blk = HGRN_PREP_ROWS
    tril = _tril_ones(blk)

    def body(bi, carry):
        r0 = pl.multiple_of(bi * blk, blk)
        fb = jnp.concatenate([f_ref[h, pl.ds(r0, blk), :] for h in range(n_heads)], axis=1)
        f = lb + (1.0 - lb) * _sigmoid(fb)
        b_blk = sum(jnp.dot(tril, piece, preferred_element_type=F32) for piece in _split2(jnp.log(f)))
        kk = 1.0 - f
        for c0 in range(0, blk, HGRN_CHUNK):
            rs = slice(c0, c0 + HGRN_CHUNK)
            b = b_blk[rs] - b_blk[c0 - 1:c0] if c0 else b_blk[rs]
            rows = pl.ds(r0 + c0, HGRN_CHUNK)
            q = jnp.concatenate([q_ref[h, rows, :] for h in range(n_heads)], axis=1)
            qh_ref[rows, :] = (q * jnp.exp(b)).astype(BF16)
            kh_ref[rows, :] = (kk[rs] * jnp.exp(-b)).astype(BF16)
            el_ref[pl.ds(bi * (blk // HGRN_CHUNK) + c0 // HGRN_CHUNK, 1), :] = jnp.exp(b[HGRN_CHUNK - 1:])
        return carry

    lax.fori_loop(0, seq // blk, body, 0, unroll=4)


def _hgrn_fast_scan(i_ref, g_ref, norm_g, qh_ref, kh_ref, el_ref, st_ref, z_ref):
    n_heads, seq, _ = i_ref.shape
    c = HGRN_CHUNK
    heads = range(n_heads)
    chunks = range(HGRN_SCAN_CHUNKS)
    lanes = [slice(h * HEAD_DIM, (h + 1) * HEAD_DIM) for h in heads]
    causal = lax.broadcasted_iota(jnp.int32, (c, c), 0) >= lax.broadcasted_iota(jnp.int32, (c, c), 1)
    tn = (((0,), (0,)), ((), ()))

    def body(ti, carry):
        rows = [pl.ds(pl.multiple_of((ti * HGRN_SCAN_CHUNKS + k) * c, c), c) for k in chunks]
        qh = [[qh_ref[rows[k], lanes[h]] for h in heads] for k in chunks]
        kh = [[kh_ref[rows[k], lanes[h]] for h in heads] for k in chunks]
        vb = [[i_ref[h, rows[k], :].astype(BF16) for h in heads] for k in chunks]
        att = [[jnp.where(causal, _dot_nt(qh[k][h], kh[k][h]), 0.0).astype(BF16) for h in heads] for k in chunks]
        ds = [[lax.dot_general(vb[k][h], kh[k][h], tn, preferred_element_type=F32) for h in heads] for k in chunks]
        st = [st_ref[h] for h in heads]
        o = []
        for k in chunks:
            el = el_ref[pl.ds(ti * HGRN_SCAN_CHUNKS + k, 1), :]
            o.append([_dot_nt(qh[k][h], st[h].astype(BF16)) for h in heads])
            st = [(st[h] + ds[k][h]) * el[:, lanes[h]] for h in heads]
        for h in heads:
            st_ref[h] = st[h]
        for k in chunks:
            for h in heads:
                o_kh = o[k][h] + jnp.dot(att[k][h], vb[k][h], preferred_element_type=F32)
                z_ref[h, rows[k], :] = _rms_gate(o_kh, norm_g[:, lanes[h]], g_ref[h, rows[k], :]).astype(z_ref.dtype)
        return carry

    lax.fori_loop(0, seq // (c * HGRN_SCAN_CHUNKS), body, 0)


def _hgrn_chunk(q, fb, v, lb, st, tril):
    c = q.shape[0]
    kk, b = _hgrn_gates(fb, lb, tril)
    o = _dot_nt((q * jnp.exp(b)).astype(BF16), st.astype(BF16))

    s_idx = lax.broadcasted_iota(jnp.int32, (c, 1), 0)
    lane = lax.broadcasted_iota(jnp.int32, (HGRN_SUB, c), 1)
    row = lax.broadcasted_iota(jnp.int32, (HGRN_SUB, c), 0)
    att_rows = []
    for i0 in range(0, c, HGRN_SUB):
        qi = q[i0:i0 + HGRN_SUB]
        bi = b[i0:i0 + HGRN_SUB]
        if i0 > 0:
            bref = b[i0 - 1:i0]
            kt = jnp.where(s_idx < i0, kk * jnp.exp(jnp.minimum(bref - b, 0.0)), 0.0)
            att = _dot_nt((qi * jnp.exp(bi - bref)).astype(BF16), kt.astype(BF16))
        else:
            att = jnp.zeros((HGRN_SUB, c), F32)
        for j in range(HGRN_SUB):
            s = i0 + j
            e = jnp.exp(jnp.minimum(bi - b[s:s + 1], 0.0))
            colv = jnp.sum(qi * kk[s:s + 1] * e, axis=-1, keepdims=True)
            att = jnp.where((lane == s) & (row >= j), colv, att)
        att_rows.append(att)
    att = jnp.concatenate(att_rows, axis=0)
    vb = v.astype(BF16)
    o = o + jnp.dot(att.astype(BF16), vb, preferred_element_type=F32)

    b_last = b[c - 1:c]
    kd = (kk * jnp.exp(b_last - b)).astype(BF16)
    st_new = st * jnp.exp(b_last) + lax.dot_general(vb, kd, (((0,), (0,)), ((), ())),
                                                    preferred_element_type=F32)
    return o, st_new


def _tril_ones(c):
    return (lax.broadcasted_iota(jnp.int32, (c, c), 0) >= lax.broadcasted_iota(jnp.int32, (c, c), 1)
            ).astype(BF16)


def _hgrn_kernel(layer, q_ref, f_ref, i_ref, g_ref, lb_ref, ng_ref, z_ref, s_ref, st_ref, qh_ref, kh_ref, el_ref):
    n_heads, seq, _ = q_ref.shape
    lb = _lower_bound(lb_ref[...], layer)
    norm_g = ng_ref[...]
    st_ref[...] = jnp.zeros_like(st_ref)

    def fast():
        _hgrn_fast_prepare(q_ref, f_ref, lb, qh_ref, kh_ref, el_ref)
        _hgrn_fast_scan(i_ref, g_ref, norm_g, qh_ref, kh_ref, el_ref, st_ref, z_ref)

    def safe():
        tril = _tril_ones(HGRN_CHUNK)

        def body(ci, carry):
            rows = pl.ds(pl.multiple_of(ci * HGRN_CHUNK, HGRN_CHUNK), HGRN_CHUNK)
            for h in range(n_heads):
                lanes = slice(h * HEAD_DIM, (h + 1) * HEAD_DIM)
                o, st = _hgrn_chunk(q_ref[h, rows, :], f_ref[h, rows, :], i_ref[h, rows, :], lb[:, lanes],
                                    st_ref[h], tril)
                st_ref[h] = st
                z_ref[h, rows, :] = _rms_gate(o, norm_g[:, lanes], g_ref[h, rows, :]).astype(z_ref.dtype)
            return carry

        lax.fori_loop(0, seq // HGRN_CHUNK, body, 0)

    fast_ok = HGRN_CHUNK * -jnp.log(jnp.min(lb)) <= HGRN_SAFE_LOG_RANGE
    lax.cond(fast_ok, fast, safe)
    for h in range(n_heads):
        s_ref[h] = st_ref[h].T


def _prompt_hgrn(proj, lb_raw, norm_g, layer, bsz, seq):
    hps = HGRN_HEADS_PER_STEP
    assert N_HEADS_HGRN % hps == 0 and all(g % hps == 0 for g in (G_QB, G_FB, G_IB, G_GB))
    assert seq % (HGRN_CHUNK * HGRN_SCAN_CHUNKS) == 0 and seq % HGRN_PREP_ROWS == 0 and HGRN_PREP_ROWS % HGRN_CHUNK == 0

    def cols(g0):
        return pl.BlockSpec((hps, seq, HEAD_DIM), lambda b, h: (g0 // hps + h, b, 0))

    return pl.pallas_call(
        functools.partial(_hgrn_kernel, layer),
        grid=(bsz, N_HEADS_HGRN // hps),
        in_specs=[cols(G_QB), cols(G_FB), cols(G_IB), cols(G_GB),
                  pl.BlockSpec((DEPTH + 1, hps * HEAD_DIM), lambda b, h: (0, h)),
                  pl.BlockSpec((1, hps * HEAD_DIM), lambda b, h: (0, h))],
        out_specs=[pl.BlockSpec((hps, seq, HEAD_DIM), lambda b, h: (h, b, 0)),
                   pl.BlockSpec((None, hps, HEAD_DIM, HEAD_DIM), lambda b, h: (b, h, 0, 0))],
        out_shape=[jax.ShapeDtypeStruct((N_HEADS_HGRN, bsz * seq, HEAD_DIM), BF16),
                   jax.ShapeDtypeStruct((bsz, N_HEADS_HGRN, HEAD_DIM, HEAD_DIM), F32)],
        scratch_shapes=[pltpu.VMEM((hps, HEAD_DIM, HEAD_DIM), F32),
                        pltpu.VMEM((seq, hps * HEAD_DIM), BF16),
                        pltpu.VMEM((seq, hps * HEAD_DIM), BF16),
                        pltpu.VMEM((seq // HGRN_CHUNK, hps * HEAD_DIM), F32)],
        compiler_params=_params(("parallel", "parallel")),
        name="prompt_hgrn",
    )(proj, proj, proj, proj, lb_raw, norm_g)


MEM_ROWS = 256
MEM_GROUP = 8


def _mem_kernel(q_ref, g_ref, mem_ref, w_ref, z_ref, mk_ref, mv_ref):
    seq = q_ref.shape[1]
    kv = jnp.dot(mem_ref[...].astype(BF16), w_ref[...].astype(BF16), preferred_element_type=F32)
    ones = jnp.ones((N_MEM, HEAD_DIM), BF16)
    mk_b, mv_b = [], []
    for h in range(N_HEADS_MEM):
        mk = kv[:, h * HEAD_DIM:(h + 1) * HEAD_DIM]
        mv = kv[:, W_MEM + h * HEAD_DIM:W_MEM + (h + 1) * HEAD_DIM]
        mk_ref[:, h, :] = mk
        mv_ref[:, h, :] = mv
        mk_b.append(mk.astype(BF16))
        mv_b.append(jnp.concatenate([mv.astype(BF16), ones], axis=1))

    items = [(h, pl.ds(n * MEM_ROWS, MEM_ROWS)) for h in range(N_HEADS_MEM) for n in range(seq // MEM_ROWS)]
    for g0 in range(0, len(items), MEM_GROUP):
        group = items[g0:g0 + MEM_GROUP]
        scores = [_dot_nt((q_ref[h, rows, :] * ATTN_SCALE_LOG2).astype(BF16), mk_b[h]) for h, rows in group]
        probs = [jnp.exp2(s - jnp.max(s, axis=-1, keepdims=True)).astype(BF16) for s in scores]
        for (h, rows), p in zip(group, probs):
            pv = jnp.dot(p, mv_b[h], preferred_element_type=F32)
            o = pv[:, :HEAD_DIM] / pv[:, HEAD_DIM:]
            z_ref[h, rows, :] = (o * _silu(g_ref[h, rows, :])).astype(z_ref.dtype)


def _prompt_mem(proj, mem, w_kv, bsz, seq):
    assert G_QM % N_HEADS_MEM == 0 and G_GM % N_HEADS_MEM == 0

    def cols(g0):
        return pl.BlockSpec((N_HEADS_MEM, seq, HEAD_DIM), lambda b: (g0 // N_HEADS_MEM, b, 0))

    kv_out = pl.BlockSpec((None, N_MEM, N_HEADS_MEM, HEAD_DIM), lambda b: (b, 0, 0, 0))
    kv_shape = jax.ShapeDtypeStruct((bsz, N_MEM, N_HEADS_MEM, HEAD_DIM), F32)
    return pl.pallas_call(
        _mem_kernel,
        grid=(bsz,),
        in_specs=[cols(G_QM), cols(G_GM), pl.BlockSpec((N_MEM, D_MODEL), lambda b: (b, 0)),
                  pl.BlockSpec((D_MODEL, 2 * W_MEM), lambda b: (0, 0))],
        out_specs=[pl.BlockSpec((N_HEADS_MEM, seq, HEAD_DIM), lambda b: (0, b, 0)), kv_out, kv_out],
        out_shape=[jax.ShapeDtypeStruct((N_HEADS_MEM, bsz * seq, HEAD_DIM), BF16), kv_shape, kv_shape],
        compiler_params=_params(("parallel",)),
        name="prompt_mem",
    )(proj, proj, mem, w_kv)


def _merge_kernel(cast_w, za_ref, zh_ref, zm_ref, x_ref, w_ref, lg_ref, lb_ref, o_ref, *rest):
    if cast_w:
        wb_ref, z_ref, stage, sem = rest
        n_rows = stage.shape[1]
        n_chunks = w_ref.shape[0] // n_rows

        def chunk_copy(c):
            return pltpu.make_async_copy(w_ref.at[pl.ds(c * n_rows, n_rows), :], stage.at[c % 2], sem.at[c % 2])

        @pl.when(pl.program_id(0) == 0)
        def _():
            chunk_copy(0).start()
            for c in range(n_chunks):
                if c + 1 < n_chunks:
                    chunk_copy(c + 1).start()
                chunk_copy(c).wait()
                wb_ref[c * n_rows:(c + 1) * n_rows, :] = stage[c % 2].astype(BF16)
    else:
        wb_ref, (z_ref,) = w_ref, rest

    c0 = 0
    for ref in (za_ref, zh_ref, zm_ref):
        for c in range(ref.shape[0]):
            z_ref[:, (c0 + c) * LANES:(c0 + c + 1) * LANES] = ref[c].astype(BF16)
        c0 += ref.shape[0]
    y = jnp.dot(z_ref[...], wb_ref[...], preferred_element_type=F32)
    r = DEEPNORM_ALPHA * x_ref[...] + y
    mu = jnp.mean(r, axis=-1, keepdims=True)
    d = r - mu
    var = jnp.mean(d * d, axis=-1, keepdims=True)
    o_ref[...] = d * lax.rsqrt(var + LN_EPS) * lg_ref[...] + lb_ref[...]


def _merge(za, zh, zm, x, w_out, ln_g, ln_b, tm):
    m = x.shape[0]
    cast_w = w_out.dtype != BF16

    def slab(a):
        return pl.BlockSpec((a.shape[0], tm, LANES), lambda i: (0, i, 0))

    const = lambda shape: pl.BlockSpec(shape, lambda i: (0, 0))
    y_spec, y_shape = pl.BlockSpec((tm, D_MODEL), lambda i: (i, 0)), jax.ShapeDtypeStruct((m, D_MODEL), F32)
    w_block = const((MIX_WIDTH, D_MODEL))
    z_scratch = pltpu.VMEM((tm, MIX_WIDTH), BF16)
    if cast_w:
        w_spec = pl.BlockSpec(memory_space=pl.ANY)
        out_specs, out_shape = [y_spec, w_block], [y_shape, jax.ShapeDtypeStruct((MIX_WIDTH, D_MODEL), BF16)]
        scratch = [z_scratch, pltpu.VMEM((2, MERGE_CAST_ROWS, D_MODEL), F32), pltpu.SemaphoreType.DMA((2,))]
    else:
        w_spec, out_specs, out_shape, scratch = w_block, y_spec, y_shape, [z_scratch]
    return pl.pallas_call(
        functools.partial(_merge_kernel, cast_w),
        grid=(m // tm,),
        in_specs=[slab(za), slab(zh), slab(zm), pl.BlockSpec((tm, D_MODEL), lambda i: (i, 0)),
                  w_spec, const((1, D_MODEL)), const((1, D_MODEL))],
        out_specs=out_specs,
        out_shape=out_shape,
        scratch_shapes=scratch,
        compiler_params=_params(("arbitrary",)),
        name="merge",
    )(za, zh, zm, x, w_out, ln_g, ln_b)


def _column(row):
    return jnp.broadcast_to(row, (HEAD_DIM, HEAD_DIM)).T


HBM_TILE_ROWS = 8
SAMPLE_ROWS = 4


def _window_pieces(past):
    pieces = []
    for dil in DILATIONS:
        if dil == 1:
            pieces.append((0, past - BAND, BAND, None))
        elif dil < HBM_TILE_ROWS:
            n = BAND * dil // HBM_TILE_ROWS
            pieces += [(1, past // HBM_TILE_ROWS - n, n, r) for r in range(0, HBM_TILE_ROWS, dil)]
        else:
            pieces.append((2, past // dil - BAND, BAND, 0))
    return pieces


def _cache_views(cache):
    depth, bsz, past, nh, hd = cache.shape
    assert all(d == 1 or HBM_TILE_ROWS % d == 0 or d % HBM_TILE_ROWS == 0 for d in DILATIONS)
    big = max(DILATIONS)
    hm = cache.transpose(0, 1, 3, 2, 4)
    return (hm, hm.reshape(depth, bsz, nh, past // HBM_TILE_ROWS, HBM_TILE_ROWS, hd),
            hm.reshape(depth, bsz, nh, past // big, big, hd))


def _sample_row(layer, p_ref, cos, sin, k_win, v_win, st_ref, mk_ref, mv_ref, lb_all, ng_ref, z_ref, ko_ref, vo_ref,
                so_ref):
    q_all = _rope(p_ref[G_QA:G_QA + N_HEADS_ATTN, :], cos, sin) * ATTN_SCALE
    k_all = _rope(p_ref[G_KA:G_KA + N_HEADS_ATTN, :], cos, sin)
    v_all = p_ref[G_VA:G_VA + N_HEADS_ATTN, :]
    ko_ref[...] = k_all
    vo_ref[...] = v_all

    for h in range(N_HEADS_ATTN):
        q = q_all[h:h + 1]
        s_new = jnp.sum(q * k_all[h:h + 1], axis=-1, keepdims=True)
        s = jnp.sum(k_win[h] * q, axis=-1, keepdims=True)
        m = jnp.maximum(jnp.max(s, axis=0, keepdims=True), s_new)
        p = jnp.exp(s - m)
        p_new = jnp.exp(s_new - m) * len(DILATIONS)
        den = jnp.sum(p, axis=0, keepdims=True) + p_new
        num = jnp.sum(p * v_win[h], axis=0, keepdims=True) + p_new * v_all[h:h + 1]
        z_ref[h:h + 1, :] = (num / den) * _silu(p_ref[G_GA + h:G_GA + h + 1, :])

    for h in range(N_HEADS_HGRN):
        lanes = slice(h * HEAD_DIM, (h + 1) * HEAD_DIM)
        lb = _lower_bound(lb_all[:, lanes], layer)
        f = lb + (1.0 - lb) * _sigmoid(p_ref[G_FB + h:G_FB + h + 1, :])
        f_col = _column(f)
        q_col = _column(p_ref[G_QB + h:G_QB + h + 1, :])
        s_new = f_col * st_ref[h] + (1.0 - f_col) * p_ref[G_IB + h:G_IB + h + 1, :]
        so_ref[h] = s_new
        o = jnp.sum(s_new * q_col, axis=0, keepdims=True)
        z_ref[N_HEADS_ATTN + h:N_HEADS_ATTN + h + 1, :] = _rms_gate(
            o, ng_ref[:, lanes], p_ref[G_GB + h:G_GB + h + 1, :])

    fold = mk_ref.shape[1] // N_HEADS_MEM
    unfold = lambda a: sum(a[i * N_HEADS_MEM:(i + 1) * N_HEADS_MEM] for i in range(fold))
    q_m = p_ref[G_QM:G_QM + N_HEADS_MEM, :] * ATTN_SCALE
    s = jnp.sum(mk_ref[...] * jnp.concatenate([q_m] * fold, axis=0)[None], axis=-1, keepdims=True)
    m = jnp.max(s, axis=0)
    m = functools.reduce(jnp.maximum, [m[i * N_HEADS_MEM:(i + 1) * N_HEADS_MEM] for i in range(fold)])
    p = jnp.exp(s - jnp.concatenate([m] * fold, axis=0)[None])
    o = unfold(jnp.sum(p * mv_ref[...], axis=0)) / unfold(jnp.sum(p, axis=0))
    row = N_HEADS_ATTN + N_HEADS_HGRN
    z_ref[row:row + N_HEADS_MEM, :] = o * _silu(p_ref[G_GM:G_GM + N_HEADS_MEM, :])


def _sample_kernel(layer, past, p_ref, cos_ref, sin_ref, k0_hbm, k1_hbm, k2_hbm, v0_hbm, v1_hbm, v2_hbm,
                   st_ref, mk_ref, mv_ref, lb_ref, ng_ref, z_ref, ko_ref, vo_ref, so_ref, kbuf, vbuf, sem):
    step = pl.program_id(0)
    slot = step % 2
    n_rows = p_ref.shape[0]
    pieces = _window_pieces(past)

    def window_copies(stp, sl):
        out = []
        for rr in range(n_rows):
            for ci, (views, buf) in enumerate((((k0_hbm, k1_hbm, k2_hbm), kbuf), ((v0_hbm, v1_hbm, v2_hbm), vbuf))):
                off = 0
                for pi, (vi, start, count, res) in enumerate(pieces):
                    view, row = views[vi], stp * n_rows + rr
                    src = (view.at[layer, row, :, pl.ds(start, count), :] if res is None
                           else view.at[layer, row, :, pl.ds(start, count), res, :])
                    out.append(pltpu.make_async_copy(src, buf.at[sl, rr, :, pl.ds(off, count), :],
                                                     sem.at[sl, rr, ci, pi]))
                    off += count
        return out

    @pl.when(step == 0)
    def _():
        for c in window_copies(step, slot):
            c.start()

    @pl.when(step + 1 < pl.num_programs(0))
    def _():
        for c in window_copies(step + 1, 1 - slot):
            c.start()

    for c in window_copies(step, slot):
        c.wait()

    cos, sin, lb_all = cos_ref[...], sin_ref[...], lb_ref[...]
    for rr in range(n_rows):
        _sample_row(layer, p_ref.at[rr], cos, sin, kbuf.at[slot, rr], vbuf.at[slot, rr], st_ref.at[rr], mk_ref.at[rr],
                    mv_ref.at[rr], lb_all, ng_ref, z_ref.at[rr], ko_ref.at[rr], vo_ref.at[rr], so_ref.at[rr])


def _sample_mixers(proj_rows, cos, sin, win_k, win_v, state, mem_k, mem_v, lb_raw, norm_g, layer):
    bsz, past = win_k.shape[1:3]
    rps = SAMPLE_ROWS
    assert all(past % d == 0 and past >= w for w, d in zip(WINDOWS, DILATIONS)) and bsz % rps == 0
    assert all(w // d == BAND for w, d in zip(WINDOWS, DILATIONS)) and past % HBM_TILE_ROWS == 0
    n_keys = BAND * len(DILATIONS)
    n_pieces = len(_window_pieces(past))
    any_spec = pl.BlockSpec(memory_space=pl.ANY)
    vec = lambda n: pl.BlockSpec((n, W_HGRN), lambda b: (0, 0))
    heads = lambda n: pl.BlockSpec((rps, n, HEAD_DIM), lambda b: (b, 0, 0))
    fold = HBM_TILE_ROWS // N_HEADS_MEM
    mem_k, mem_v = (a.reshape(a.shape[0], bsz, N_MEM // fold, fold * N_HEADS_MEM, HEAD_DIM) for a in (mem_k, mem_v))
    mem_spec = pl.BlockSpec((None, rps, N_MEM // fold, fold * N_HEADS_MEM, HEAD_DIM), lambda b: (layer, b, 0, 0, 0))
    return pl.pallas_call(
        functools.partial(_sample_kernel, layer, past),
        grid=(bsz // rps,),
        in_specs=[heads(N_GROUPS), pl.BlockSpec((1, HEAD_DIM), lambda b: (0, 0)),
                  pl.BlockSpec((1, HEAD_DIM), lambda b: (0, 0)), *([any_spec] * 6),
                  pl.BlockSpec((None, rps, N_HEADS_HGRN, HEAD_DIM, HEAD_DIM), lambda b: (layer, b, 0, 0, 0)),
                  mem_spec, mem_spec, vec(DEPTH + 1), vec(1)],
        out_specs=[heads(MIX_WIDTH // HEAD_DIM), heads(N_HEADS_ATTN), heads(N_HEADS_ATTN),
                   pl.BlockSpec((rps, N_HEADS_HGRN, HEAD_DIM, HEAD_DIM), lambda b: (b, 0, 0, 0))],
        out_shape=[jax.ShapeDtypeStruct((bsz, MIX_WIDTH // HEAD_DIM, HEAD_DIM), F32),
                   jax.ShapeDtypeStruct((bsz, N_HEADS_ATTN, HEAD_DIM), F32),
                   jax.ShapeDtypeStruct((bsz, N_HEADS_ATTN, HEAD_DIM), F32),
                   jax.ShapeDtypeStruct((bsz, N_HEADS_HGRN, HEAD_DIM, HEAD_DIM), F32)],
        scratch_shapes=[pltpu.VMEM((2, rps, N_HEADS_ATTN, n_keys, HEAD_DIM), F32),
                        pltpu.VMEM((2, rps, N_HEADS_ATTN, n_keys, HEAD_DIM), F32),
                        pltpu.SemaphoreType.DMA((2, rps, 2, n_pieces))],
        compiler_params=_params(("arbitrary",)),
        name="sample_mixers",
    )(proj_rows, cos, sin, *_cache_views(win_k), *_cache_views(win_v), state, mem_k, mem_v, lb_raw, norm_g)


def _rope_tables(pos):
    half = HEAD_DIM // 2
    inv_freq = 1.0 / (ROPE_THETA ** (np.arange(half, dtype=np.float64) / half))
    ang = np.asarray(pos, np.float64)[:, None] * inv_freq[None, :]
    cos, sin = np.cos(ang), np.sin(ang)
    return (jnp.asarray(np.concatenate([cos, cos], axis=-1), F32),
            jnp.asarray(np.concatenate([-sin, sin], axis=-1), F32))


def kernel(x_prompt, x_sample, cache_win_k, cache_win_v, state_hgrn, cache_mem_k, cache_mem_v, mem_prompt,
           w_in, w_mem_kv, hgrn_lb_raw, hgrn_norm_g, w_out, ln_g, ln_b):
    bsz, seq, _ = x_prompt.shape
    dbsz, n_new, _ = x_sample.shape
    assert n_new == 1 and seq % (BAND * max(DILATIONS)) == 0
    cos_p, sin_p = _rope_tables(np.arange(seq))
    cos_s, sin_s = _rope_tables(PAST_LEN + np.arange(n_new))

    hp = x_prompt.reshape(bsz * seq, D_MODEL)
    hs = x_sample.reshape(dbsz * n_new, D_MODEL)
    mem = mem_prompt.reshape(bsz * N_MEM, D_MODEL)
    outs = [[] for _ in range(8)]
    for layer in range(DEPTH):
        lb_raw = hgrn_lb_raw
        norm_g = hgrn_norm_g[layer][None]
        lg, lbias = ln_g[layer][None], ln_b[layer][None]

        proj, proj_s = _project(hp, hs, w_in[layer], PROJ_ROWS, PROJ_FIRST_COLS, PROJ_COLS)
        za, k1, v1 = _prompt_attention(proj, cos_p, sin_p, bsz, seq)
        zh, s1 = _prompt_hgrn(proj, lb_raw, norm_g, layer, bsz, seq)
        zm, mk1, mv1 = _prompt_mem(proj, mem, w_mem_kv[layer], bsz, seq)
        hp, w_out_b = _merge(za, zh, zm, hp, w_out[layer], lg, lbias, MERGE_ROWS)

        proj_s = proj_s.transpose(1, 0, 2)
        zs, k2, v2, s2 = _sample_mixers(proj_s, cos_s, sin_s, cache_win_k, cache_win_v, state_hgrn,
                                        cache_mem_k, cache_mem_v, lb_raw, norm_g, layer)
        zs = zs.transpose(1, 0, 2)
        hs = _merge(zs[:N_HEADS_ATTN], zs[N_HEADS_ATTN:N_HEADS_ATTN + N_HEADS_HGRN],
                    zs[N_HEADS_ATTN + N_HEADS_HGRN:], hs, w_out_b, lg, lbias, dbsz)

        new = (k1.transpose(0, 2, 1, 3), v1.transpose(0, 2, 1, 3), s1, mk1, mv1,
               k2.reshape(dbsz, n_new, N_HEADS_ATTN, HEAD_DIM), v2.reshape(dbsz, n_new, N_HEADS_ATTN, HEAD_DIM),
               s2.astype(state_hgrn.dtype))
        for acc, val in zip(outs, new):
            acc.append(val)

    return (hp.reshape(bsz, seq, D_MODEL), hs.reshape(dbsz, n_new, D_MODEL), *[jnp.stack(o) for o in outs])
```

```python
import functools

import jax
import jax.numpy as jnp
import numpy as np
from jax import lax
from jax.experimental import pallas as pl
from jax.experimental.pallas import tpu as pltpu

F32 = jnp.float32
BF16 = jnp.bfloat16

D_MODEL = 2048
DEPTH = 1
PAST_LEN = 8192
HEAD_DIM = 128
N_HEADS_ATTN = 6
N_HEADS_HGRN = 6
N_HEADS_MEM = 4
W_ATTN = N_HEADS_ATTN * HEAD_DIM
W_HGRN = N_HEADS_HGRN * HEAD_DIM
W_MEM = N_HEADS_MEM * HEAD_DIM
MIX_WIDTH = W_ATTN + W_HGRN + W_MEM
WINDOWS = (128, 512, 2048)
DILATIONS = (1, 4, 16)
N_MEM = 256
ROPE_THETA = 10000.0
LN_EPS = 1e-5
RMS_EPS = 1e-6
NEG_INF = -1e30
DEEPNORM_ALPHA = (2 * DEPTH) ** 0.25
ATTN_SCALE = HEAD_DIM ** -0.5
ATTN_SCALE_LOG2 = ATTN_SCALE * 1.4426950408889634

G_QA, G_KA, G_VA, G_GA = 0, 6, 12, 18
G_QB, G_FB, G_IB, G_GB = 24, 30, 36, 42
G_QM, G_GM = 48, 52
N_GROUPS = 56

LANES = 128
BAND = 128
ATTN_GROUP = 8
HGRN_CHUNK = 64
HGRN_SUB = 16
HGRN_HEADS_PER_STEP = 3
HGRN_PREP_ROWS = 256
HGRN_SCAN_CHUNKS = 32
HGRN_SAFE_LOG_RANGE = 70.0
VMEM_LIMIT = 48 * 1024 * 1024
PROJ_ROWS, PROJ_COLS, PROJ_FIRST_COLS = 1024, 1024, 512
MERGE_ROWS = 512
MERGE_CAST_ROWS = 256


def _params(sem, vmem=VMEM_LIMIT):
    return pltpu.CompilerParams(dimension_semantics=sem, vmem_limit_bytes=vmem)


def _sigmoid(x):
    return 1.0 / (1.0 + jnp.exp(-x))


def _silu(x):
    return x * _sigmoid(x)


def _dot_nt(a, b):
    return lax.dot_general(a, b, (((1,), (1,)), ((), ())), preferred_element_type=F32)


def _store_slabs(ref, acc):
    for c in range(ref.shape[0]):
        ref[c] = acc[:, c * LANES:(c + 1) * LANES]


def _proj_first_kernel(x_ref, xs_ref, w_ref, o_ref, os_ref, wb_ref, xb_ref):
    tm = x_ref.shape[0]

    @pl.when(pl.program_id(0) == 0)
    def _():
        xb_ref[:tm, :] = x_ref[...].astype(BF16)
        xb_ref[tm:, :] = xs_ref[...].astype(BF16)

    wb_ref[...] = w_ref[...].astype(BF16)
    acc = jnp.dot(xb_ref[...], wb_ref[...], preferred_element_type=F32)
    _store_slabs(o_ref, acc[:tm])
    _store_slabs(os_ref, acc[tm:])


def _proj_rest_kernel(x_ref, w_ref, first_ref, o_hbm, xb_ref, obuf, sem_out, sem_first):
    i, j = pl.program_id(0), pl.program_id(1)
    n_col = pl.num_programs(1)
    step = i * n_col + j
    last = pl.num_programs(0) * n_col - 1
    slot = step % 2
    tm = x_ref.shape[0]
    n_slab = obuf.shape[1]

    def tile_copy(stp):
        row_tile, col_tile = stp // n_col + 1, stp % n_col
        return pltpu.make_async_copy(
            obuf.at[stp % 2], o_hbm.at[pl.ds(col_tile * n_slab, n_slab), pl.ds(row_tile * tm, tm), :],
            sem_out.at[stp % 2])

    first_copy = pltpu.make_async_copy(first_ref, o_hbm.at[pl.ds(j * n_slab, n_slab), pl.ds(0, tm), :], sem_first.at[0])

    @pl.when(i == 0)
    def _():
        first_copy.start()

    @pl.when(j == 0)
    def _():
        xb_ref[...] = x_ref[...].astype(BF16)

    @pl.when(step >= 2)
    def _():
        tile_copy(step - 2).wait()

    _store_slabs(obuf.at[slot], jnp.dot(xb_ref[...], w_ref[...], preferred_element_type=F32))
    tile_copy(step).start()

    @pl.when(i == 0)
    def _():
        first_copy.wait()

    @pl.when(step == last)
    def _():
        tile_copy(step - 1).wait()
        tile_copy(step).wait()


def _project(x, xs, w, tm, tn_first, tn):
    m, k = x.shape
    ms = xs.shape[0]
    n = w.shape[1]
    assert (m // tm - 1) * (n // tn) >= 2
    proj_first, proj_s, w_bf16 = pl.pallas_call(
        _proj_first_kernel,
        grid=(n // tn_first,),
        in_specs=[pl.BlockSpec((tm, k), lambda j: (0, 0)),
                  pl.BlockSpec((ms, k), lambda j: (0, 0)),
                  pl.BlockSpec((k, tn_first), lambda j: (0, j))],
        out_specs=[pl.BlockSpec((tn_first // LANES, tm, LANES), lambda j: (j, 0, 0)),
                   pl.BlockSpec((tn_first // LANES, ms, LANES), lambda j: (j, 0, 0)),
                   pl.BlockSpec((k, tn_first), lambda j: (0, j))],
        out_shape=[jax.ShapeDtypeStruct((n // LANES, tm, LANES), F32),
                   jax.ShapeDtypeStruct((n // LANES, ms, LANES), F32),
                   jax.ShapeDtypeStruct((k, n), BF16)],
        scratch_shapes=[pltpu.VMEM((tm + ms, k), BF16)],
        compiler_params=_params(("arbitrary",)),
        name="in_proj_first",
    )(x, xs, w)
    proj = pl.pallas_call(
        _proj_rest_kernel,
        grid=(m // tm - 1, n // tn),
        in_specs=[pl.BlockSpec((tm, k), lambda i, j: (i + 1, 0)),
                  pl.BlockSpec((k, tn), lambda i, j: (0, j)),
                  pl.BlockSpec((tn // LANES, tm, LANES), lambda i, j: (jnp.where(i == 0, j, n // tn - 1), 0, 0))],
        out_specs=pl.BlockSpec(memory_space=pl.ANY),
        out_shape=jax.ShapeDtypeStruct((n // LANES, m, LANES), F32),
        scratch_shapes=[pltpu.VMEM((tm, k), BF16), pltpu.VMEM((2, tn // LANES, tm, LANES), F32),
                        pltpu.SemaphoreType.DMA((2,)), pltpu.SemaphoreType.DMA((1,))],
        compiler_params=_params(("arbitrary", "arbitrary")),
        name="in_proj",
    )(x, w_bf16, proj_first)
    return proj, proj_s


def _rope(x, cos, sin_signed):
    return x * cos + pltpu.roll(x, HEAD_DIM // 2, 1) * sin_signed


def _bias_from_count(count):
    return jnp.where(count > 1.5, 1.0, jnp.where(count > 0.5, 0.0, NEG_INF)).astype(F32)


def _block_deltas(block_gap):
    qi = lax.broadcasted_iota(jnp.int32, (BAND, BAND), 0)
    ki = lax.broadcasted_iota(jnp.int32, (BAND, BAND), 1)
    return block_gap * BAND + qi - ki


def _near_bias(block_gap):
    d = _block_deltas(block_gap)
    return _bias_from_count(((d >= 0) & (d <= BAND)).astype(F32))


def _class_bias(block_gap, far_step):
    d = _block_deltas(block_gap)
    near = (d >= 0) & (d <= BAND)
    far = (d >= 0) & ((d & (far_step - 1)) == 0)
    return _bias_from_count(near.astype(F32) + far.astype(F32))


def _softmax_stage(scores, floors=None):
    out = []
    for idx, s in enumerate(scores):
        m = jnp.broadcast_to(jnp.max(s, axis=-1, keepdims=True), (BAND, HEAD_DIM))
        if floors is not None:
            m = jnp.maximum(m, floors[idx])
        m_wide = jnp.concatenate([m] * (s.shape[1] // HEAD_DIM), axis=1)
        out.append((m, jnp.exp2(s - m_wide).astype(BF16)))
    return out


def _attn_kernel(q_ref, k_ref, v_ref, g_ref, cos_ref, sin_ref, z_ref, ko_ref, vo_ref,
                 qs_ref, qb_ref, kb_ref, vb_ref, qc_ref, kc_ref, vc_ref, acc_ref, m_ref, l_ref):
    seq = q_ref.shape[0]
    d_mid, d_far = DILATIONS[1], DILATIONS[2]
    far_step = d_far // d_mid
    cls_rows = seq // d_mid
    cos = cos_ref[...]
    sin = sin_ref[...]
    qs_ref[...] = _rope(q_ref[...], cos, sin) * ATTN_SCALE_LOG2
    ko_ref[...] = _rope(k_ref[...], cos, sin)
    vo_ref[...] = v_ref[...]
    qb_ref[...] = qs_ref[...].astype(BF16)
    kb_ref[...] = ko_ref[...].astype(BF16)
    ones = jnp.ones((seq, HEAD_DIM), BF16)
    vb_ref[:, :HEAD_DIM] = v_ref[...].astype(BF16)
    vb_ref[:, HEAD_DIM:] = ones
    vc_ref[:, HEAD_DIM:] = ones
    for r in range(d_mid):
        cls, dst = pl.ds(r, cls_rows, stride=d_mid), pl.ds(r * cls_rows, cls_rows)
        qc_ref[dst, :] = qs_ref[cls, :].astype(BF16)
        kc_ref[dst, :] = ko_ref[cls, :].astype(BF16)
        vc_ref[dst, :HEAD_DIM] = v_ref[cls, :].astype(BF16)

    n_cls_blk = cls_rows // BAND
    cls_bias = [_class_bias(gap, far_step) for gap in range(min(n_cls_blk, 3))]
    items = [(r, n) for r in range(d_mid) for n in range(n_cls_blk)]
    for g0 in range(0, len(items), ATTN_GROUP):
        group = items[g0:g0 + ATTN_GROUP]
        keys = [pl.ds(r * cls_rows, (n + 1) * BAND) for r, n in group]
        scores = [_dot_nt(qc_ref[pl.ds(r * cls_rows + n * BAND, BAND), :], kc_ref[kr, :])
                  + jnp.concatenate([cls_bias[min(n - nk, 2)] for nk in range(n + 1)], axis=1)
                  for (r, n), kr in zip(group, keys)]
        for (r, n), kr, (m, p) in zip(group, keys, _softmax_stage(scores)):
            rows = pl.ds(r + d_mid * BAND * n, BAND, stride=d_mid)
            pv = jnp.dot(p, vc_ref[kr, :], preferred_element_type=F32)
            acc_ref[rows, :] = pv[:, :HEAD_DIM]
            l_ref[rows, :] = pv[:, HEAD_DIM:]
            m_ref[rows, :] = m

    near_bias = [_near_bias(0), jnp.concatenate([_near_bias(1), _near_bias(0)], axis=1)]
    items = list(range(seq // BAND))
    for g0 in range(0, len(items), ATTN_GROUP):
        group = items[g0:g0 + ATTN_GROUP]
        rows = [pl.ds(n * BAND, BAND) for n in group]
        keys = [pl.ds(max(n - 1, 0) * BAND, BAND * min(n + 1, 2)) for n in group]
        scores = [_dot_nt(qb_ref[rw, :], kb_ref[kr, :]) + near_bias[min(n, 1)]
                  for n, rw, kr in zip(group, rows, keys)]
        probs = _softmax_stage(scores, floors=[m_ref[rw, :] for rw in rows])
        for rw, kr, (m, p) in zip(rows, keys, probs):
            w = jnp.exp2(m_ref[rw, :] - m)
            pv = jnp.dot(p, vb_ref[kr, :], preferred_element_type=F32)
            num = pv[:, :HEAD_DIM] + w * acc_ref[rw, :]
            den = pv[:, HEAD_DIM:] + w * l_ref[rw, :]
            z_ref[rw, :] = ((num / den) * _silu(g_ref[rw, :])).astype(z_ref.dtype)


def _prompt_attention(proj, cos, sin, bsz, seq):
    def col(g0):
        return pl.BlockSpec((None, seq, HEAD_DIM), lambda b, h: (g0 + h, b, 0))

    table = pl.BlockSpec((seq, HEAD_DIM), lambda b, h: (0, 0))
    kv_out = pl.BlockSpec((None, None, seq, HEAD_DIM), lambda b, h: (b, h, 0, 0))
    d_near, d_mid, d_far = DILATIONS
    far_step = d_far // d_mid
    assert d_near == 1 and d_far % d_mid == 0 and far_step & (far_step - 1) == 0
    assert all(w // d == BAND for w, d in zip(WINDOWS, DILATIONS)) and seq % (d_mid * BAND) == 0
    assert seq // d_mid <= far_step * BAND
    return pl.pallas_call(
        _attn_kernel,
        grid=(bsz, N_HEADS_ATTN),
        in_specs=[col(G_QA), col(G_KA), col(G_VA), col(G_GA), table, table],
        out_specs=[pl.BlockSpec((None, seq, HEAD_DIM), lambda b, h: (h, b, 0)), kv_out, kv_out],
        out_shape=[jax.ShapeDtypeStruct((N_HEADS_ATTN, bsz * seq, HEAD_DIM), BF16),
                   jax.ShapeDtypeStruct((bsz, N_HEADS_ATTN, seq, HEAD_DIM), F32),
                   jax.ShapeDtypeStruct((bsz, N_HEADS_ATTN, seq, HEAD_DIM), F32)],
        scratch_shapes=[pltpu.VMEM((seq, HEAD_DIM), F32)]
                       + [pltpu.VMEM((seq, HEAD_DIM), BF16), pltpu.VMEM((seq, HEAD_DIM), BF16),
                          pltpu.VMEM((seq, 2 * HEAD_DIM), BF16)] * 2
                       + [pltpu.VMEM((seq, HEAD_DIM), F32)] * 3,
        compiler_params=_params(("parallel", "parallel")),
        name="prompt_attn",
    )(proj, proj, proj, proj, cos, sin)


def _lower_bound(lb_raw, layer):
    e = jnp.exp(lb_raw - jnp.max(lb_raw, axis=0, keepdims=True))
    sm = e / jnp.sum(e, axis=0, keepdims=True)
    return jnp.sum(sm[:layer + 1], axis=0, keepdims=True)


def _split2(x):
    hi = x.astype(BF16)
    return hi, (x - hi.astype(F32)).astype(BF16)


def _rms_gate(o, norm_g, gate):
    o = o * lax.rsqrt(jnp.mean(o * o, axis=-1, keepdims=True) + RMS_EPS)
    return o * norm_g * _silu(gate)


def _hgrn_gates(fb, lb, tril):
    f = lb + (1.0 - lb) * _sigmoid(fb)
    g = jnp.log(f)
    b = sum(jnp.dot(tril, piece, preferred_element_type=F32) for piece in _split2(g))
    return 1.0 - f, b


def _hgrn_fast_prepare(q_ref, f_ref, lb, qh_ref, kh_ref, el_ref):
    n_heads, seq, _ = q_ref.shape
    blk = HGRN_PREP_ROWS
    tril = _tril_ones(blk)

    def body(bi, carry):
        r0 = pl.multiple_of(bi * blk, blk)
        fb = jnp.concatenate([f_ref[h, pl.ds(r0, blk), :] for h in range(n_heads)], axis=1)
        f = lb + (1.0 - lb) * _sigmoid(fb)
        b_blk = sum(jnp.dot(tril, piece, preferred_element_type=F32) for piece in _split2(jnp.log(f)))
        kk = 1.0 - f
        for c0 in range(0, blk, HGRN_CHUNK):
            rs = slice(c0, c0 + HGRN_CHUNK)
            b = b_blk[rs] - b_blk[c0 - 1:c0] if c0 else b_blk[rs]
            rows = pl.ds(r0 + c0, HGRN_CHUNK)
            q = jnp.concatenate([q_ref[h, rows, :] for h in range(n_heads)], axis=1)
            qh_ref[rows, :] = (q * jnp.exp(b)).astype(BF16)
            kh_ref[rows, :] = (kk[rs] * jnp.exp(-b)).astype(BF16)
            el_ref[pl.ds(bi * (blk // HGRN_CHUNK) + c0 // HGRN_CHUNK, 1), :] = jnp.exp(b[HGRN_CHUNK - 1:])
        return carry

    lax.fori_loop(0, seq // blk, body, 0, unroll=4)


def _hgrn_fast_scan(i_ref, g_ref, norm_g, qh_ref, kh_ref, el_ref, st_ref, z_ref):
    n_heads, seq, _ = i_ref.shape
    c = HGRN_CHUNK
    heads = range(n_heads)
    chunks = range(HGRN_SCAN_CHUNKS)
    lanes = [slice(h * HEAD_DIM, (h + 1) * HEAD_DIM) for h in heads]
    causal = lax.broadcasted_iota(jnp.int32, (c, c), 0) >= lax.broadcasted_iota(jnp.int32, (c, c), 1)
    tn = (((0,), (0,)), ((), ()))

    def body(ti, carry):
        rows = [pl.ds(pl.multiple_of((ti * HGRN_SCAN_CHUNKS + k) * c, c), c) for k in chunks]
        qh = [[qh_ref[rows[k], lanes[h]] for h in heads] for k in chunks]
        kh = [[kh_ref[rows[k], lanes[h]] for h in heads] for k in chunks]
        vb = [[i_ref[h, rows[k], :].astype(BF16) for h in heads] for k in chunks]
        att = [[jnp.where(causal, _dot_nt(qh[k][h], kh[k][h]), 0.0).astype(BF16) for h in heads] for k in chunks]
        ds = [[lax.dot_general(vb[k][h], kh[k][h], tn, preferred_element_type=F32) for h in heads] for k in chunks]
        st = [st_ref[h] for h in heads]
        o = []
        for k in chunks:
            el = el_ref[pl.ds(ti * HGRN_SCAN_CHUNKS + k, 1), :]
            o.append([_dot_nt(qh[k][h], st[h].astype(BF16)) for h in heads])
            st = [(st[h] + ds[k][h]) * el[:, lanes[h]] for h in heads]
        for h in heads:
            st_ref[h] = st[h]
        for k in chunks:
            for h in heads:
                o_kh = o[k][h] + jnp.dot(att[k][h], vb[k][h], preferred_element_type=F32)
                z_ref[h, rows[k], :] = _rms_gate(o_kh, norm_g[:, lanes[h]], g_ref[h, rows[k], :]).astype(z_ref.dtype)
        return carry

    lax.fori_loop(0, seq // (c * HGRN_SCAN_CHUNKS), body, 0)


def _hgrn_chunk(q, fb, v, lb, st, tril):
    c = q.shape[0]
    kk, b = _hgrn_gates(fb, lb, tril)
    o = _dot_nt((q * jnp.exp(b)).astype(BF16), st.astype(BF16))

    s_idx = lax.broadcasted_iota(jnp.int32, (c, 1), 0)
    lane = lax.broadcasted_iota(jnp.int32, (HGRN_SUB, c), 1)
    row = lax.broadcasted_iota(jnp.int32, (HGRN_SUB, c), 0)
    att_rows = []
    for i0 in range(0, c, HGRN_SUB):
        qi = q[i0:i0 + HGRN_SUB]
        bi = b[i0:i0 + HGRN_SUB]
        if i0 > 0:
            bref = b[i0 - 1:i0]
            kt = jnp.where(s_idx < i0, kk * jnp.exp(jnp.minimum(bref - b, 0.0)), 0.0)
            att = _dot_nt((qi * jnp.exp(bi - bref)).astype(BF16), kt.astype(BF16))
        else:
            att = jnp.zeros((HGRN_SUB, c), F32)
        for j in range(HGRN_SUB):
            s = i0 + j
            e = jnp.exp(jnp.minimum(bi - b[s:s + 1], 0.0))
            colv = jnp.sum(qi * kk[s:s + 1] * e, axis=-1, keepdims=True)
            att = jnp.where((lane == s) & (row >= j), colv, att)
        att_rows.append(att)
    att = jnp.concatenate(att_rows, axis=0)
    vb = v.astype(BF16)
    o = o + jnp.dot(att.astype(BF16), vb, preferred_element_type=F32)

    b_last = b[c - 1:c]
    kd = (kk * jnp.exp(b_last - b)).astype(BF16)
    st_new = st * jnp.exp(b_last) + lax.dot_general(vb, kd, (((0,), (0,)), ((), ())),
                                                    preferred_element_type=F32)
    return o, st_new


def _tril_ones(c):
    return (lax.broadcasted_iota(jnp.int32, (c, c), 0) >= lax.broadcasted_iota(jnp.int32, (c, c), 1)
            ).astype(BF16)


def _hgrn_kernel(layer, q_ref, f_ref, i_ref, g_ref, lb_ref, ng_ref, z_ref, s_ref, st_ref, qh_ref, kh_ref, el_ref):
    n_heads, seq, _ = q_ref.shape
    lb = _lower_bound(lb_ref[...], layer)
    norm_g = ng_ref[...]
    st_ref[...] = jnp.zeros_like(st_ref)

    def fast():
        _hgrn_fast_prepare(q_ref, f_ref, lb, qh_ref, kh_ref, el_ref)
        _hgrn_fast_scan(i_ref, g_ref, norm_g, qh_ref, kh_ref, el_ref, st_ref, z_ref)

    def safe():
        tril = _tril_ones(HGRN_CHUNK)

        def body(ci, carry):
            rows = pl.ds(pl.multiple_of(ci * HGRN_CHUNK, HGRN_CHUNK), HGRN_CHUNK)
            for h in range(n_heads):
                lanes = slice(h * HEAD_DIM, (h + 1) * HEAD_DIM)
                o, st = _hgrn_chunk(q_ref[h, rows, :], f_ref[h, rows, :], i_ref[h, rows, :], lb[:, lanes],
                                    st_ref[h], tril)
                st_ref[h] = st
                z_ref[h, rows, :] = _rms_gate(o, norm_g[:, lanes], g_ref[h, rows, :]).astype(z_ref.dtype)
            return carry

        lax.fori_loop(0, seq // HGRN_CHUNK, body, 0)

    fast_ok = HGRN_CHUNK * -jnp.log(jnp.min(lb)) <= HGRN_SAFE_LOG_RANGE
    lax.cond(fast_ok, fast, safe)
    for h in range(n_heads):
        s_ref[h] = st_ref[h].T


def _prompt_hgrn(proj, lb_raw, norm_g, layer, bsz, seq):
    hps = HGRN_HEADS_PER_STEP
    assert N_HEADS_HGRN % hps == 0 and all(g % hps == 0 for g in (G_QB, G_FB, G_IB, G_GB))
    assert seq % (HGRN_CHUNK * HGRN_SCAN_CHUNKS) == 0 and seq % HGRN_PREP_ROWS == 0 and HGRN_PREP_ROWS % HGRN_CHUNK == 0

    def cols(g0):
        return pl.BlockSpec((hps, seq, HEAD_DIM), lambda b, h: (g0 // hps + h, b, 0))

    return pl.pallas_call(
        functools.partial(_hgrn_kernel, layer),
        grid=(bsz, N_HEADS_HGRN // hps),
        in_specs=[cols(G_QB), cols(G_FB), cols(G_IB), cols(G_GB),
                  pl.BlockSpec((DEPTH + 1, hps * HEAD_DIM), lambda b, h: (0, h)),
                  pl.BlockSpec((1, hps * HEAD_DIM), lambda b, h: (0, h))],
        out_specs=[pl.BlockSpec((hps, seq, HEAD_DIM), lambda b, h: (h, b, 0)),
                   pl.BlockSpec((None, hps, HEAD_DIM, HEAD_DIM), lambda b, h: (b, h, 0, 0))],
        out_shape=[jax.ShapeDtypeStruct((N_HEADS_HGRN, bsz * seq, HEAD_DIM), BF16),
                   jax.ShapeDtypeStruct((bsz, N_HEADS_HGRN, HEAD_DIM, HEAD_DIM), F32)],
        scratch_shapes=[pltpu.VMEM((hps, HEAD_DIM, HEAD_DIM), F32),
                        pltpu.VMEM((seq, hps * HEAD_DIM), BF16),
                        pltpu.VMEM((seq, hps * HEAD_DIM), BF16),
                        pltpu.VMEM((seq // HGRN_CHUNK, hps * HEAD_DIM), F32)],
        compiler_params=_params(("parallel", "parallel")),
        name="prompt_hgrn",
    )(proj, proj, proj, proj, lb_raw, norm_g)


MEM_ROWS = 256
MEM_GROUP = 4


def _mem_kernel(q_ref, g_ref, mem_ref, w_ref, z_ref, mk_ref, mv_ref):
    seq = q_ref.shape[1]
    kv = jnp.dot(mem_ref[...].astype(BF16), w_ref[...].astype(BF16), preferred_element_type=F32)
    ones = jnp.ones((N_MEM, HEAD_DIM), BF16)
    mk_b, mv_b = [], []
    for h in range(N_HEADS_MEM):
        mk = kv[:, h * HEAD_DIM:(h + 1) * HEAD_DIM]
        mv = kv[:, W_MEM + h * HEAD_DIM:W_MEM + (h + 1) * HEAD_DIM]
        mk_ref[:, h, :] = mk
        mv_ref[:, h, :] = mv
        mk_b.append(mk.astype(BF16))
        mv_b.append(jnp.concatenate([mv.astype(BF16), ones], axis=1))

    items = [(h, pl.ds(n * MEM_ROWS, MEM_ROWS)) for h in range(N_HEADS_MEM) for n in range(seq // MEM_ROWS)]
    for g0 in range(0, len(items), MEM_GROUP):
        group = items[g0:g0 + MEM_GROUP]
        scores = [_dot_nt((q_ref[h, rows, :] * ATTN_SCALE_LOG2).astype(BF16), mk_b[h]) for h, rows in group]
        probs = [jnp.exp2(s - jnp.max(s, axis=-1, keepdims=True)).astype(BF16) for s in scores]
        for (h, rows), p in zip(group, probs):
            pv = jnp.dot(p, mv_b[h], preferred_element_type=F32)
            o = pv[:, :HEAD_DIM] / pv[:, HEAD_DIM:]
            z_ref[h, rows, :] = (o * _silu(g_ref[h, rows, :])).astype(z_ref.dtype)


def _prompt_mem(proj, mem, w_kv, bsz, seq):
    assert G_QM % N_HEADS_MEM == 0 and G_GM % N_HEADS_MEM == 0

    def cols(g0):
        return pl.BlockSpec((N_HEADS_MEM, seq, HEAD_DIM), lambda b: (g0 // N_HEADS_MEM, b, 0))

    kv_out = pl.BlockSpec((None, N_MEM, N_HEADS_MEM, HEAD_DIM), lambda b: (b, 0, 0, 0))
    kv_shape = jax.ShapeDtypeStruct((bsz, N_MEM, N_HEADS_MEM, HEAD_DIM), F32)
    return pl.pallas_call(
        _mem_kernel,
        grid=(bsz,),
        in_specs=[cols(G_QM), cols(G_GM), pl.BlockSpec((N_MEM, D_MODEL), lambda b: (b, 0)),
                  pl.BlockSpec((D_MODEL, 2 * W_MEM), lambda b: (0, 0))],
        out_specs=[pl.BlockSpec((N_HEADS_MEM, seq, HEAD_DIM), lambda b: (0, b, 0)), kv_out, kv_out],
        out_shape=[jax.ShapeDtypeStruct((N_HEADS_MEM, bsz * seq, HEAD_DIM), BF16), kv_shape, kv_shape],
        compiler_params=_params(("parallel",)),
        name="prompt_mem",
    )(proj, proj, mem, w_kv)


def _merge_kernel(cast_w, za_ref, zh_ref, zm_ref, x_ref, w_ref, lg_ref, lb_ref, o_ref, *rest):
    if cast_w:
        wb_ref, z_ref, stage, sem = rest
        n_rows = stage.shape[1]
        n_chunks = w_ref.shape[0] // n_rows

        def chunk_copy(c):
            return pltpu.make_async_copy(w_ref.at[pl.ds(c * n_rows, n_rows), :], stage.at[c % 2], sem.at[c % 2])

        @pl.when(pl.program_id(0) == 0)
        def _():
            chunk_copy(0).start()
            for c in range(n_chunks):
                if c + 1 < n_chunks:
                    chunk_copy(c + 1).start()
                chunk_copy(c).wait()
                wb_ref[c * n_rows:(c + 1) * n_rows, :] = stage[c % 2].astype(BF16)
    else:
        wb_ref, (z_ref,) = w_ref, rest

    c0 = 0
    for ref in (za_ref, zh_ref, zm_ref):
        for c in range(ref.shape[0]):
            z_ref[:, (c0 + c) * LANES:(c0 + c + 1) * LANES] = ref[c].astype(BF16)
        c0 += ref.shape[0]
    y = jnp.dot(z_ref[...], wb_ref[...], preferred_element_type=F32)
    r = DEEPNORM_ALPHA * x_ref[...] + y
    mu = jnp.mean(r, axis=-1, keepdims=True)
    d = r - mu
    var = jnp.mean(d * d, axis=-1, keepdims=True)
    o_ref[...] = d * lax.rsqrt(var + LN_EPS) * lg_ref[...] + lb_ref[...]


def _merge(za, zh, zm, x, w_out, ln_g, ln_b, tm):
    m = x.shape[0]
    cast_w = w_out.dtype != BF16

    def slab(a):
        return pl.BlockSpec((a.shape[0], tm, LANES), lambda i: (0, i, 0))

    const = lambda shape: pl.BlockSpec(shape, lambda i: (0, 0))
    y_spec, y_shape = pl.BlockSpec((tm, D_MODEL), lambda i: (i, 0)), jax.ShapeDtypeStruct((m, D_MODEL), F32)
    w_block = const((MIX_WIDTH, D_MODEL))
    z_scratch = pltpu.VMEM((tm, MIX_WIDTH), BF16)
    if cast_w:
        w_spec = pl.BlockSpec(memory_space=pl.ANY)
        out_specs, out_shape = [y_spec, w_block], [y_shape, jax.ShapeDtypeStruct((MIX_WIDTH, D_MODEL), BF16)]
        scratch = [z_scratch, pltpu.VMEM((2, MERGE_CAST_ROWS, D_MODEL), F32), pltpu.SemaphoreType.DMA((2,))]
    else:
        w_spec, out_specs, out_shape, scratch = w_block, y_spec, y_shape, [z_scratch]
    return pl.pallas_call(
        functools.partial(_merge_kernel, cast_w),
        grid=(m // tm,),
        in_specs=[slab(za), slab(zh), slab(zm), pl.BlockSpec((tm, D_MODEL), lambda i: (i, 0)),
                  w_spec, const((1, D_MODEL)), const((1, D_MODEL))],
        out_specs=out_specs,
        out_shape=out_shape,
        scratch_shapes=scratch,
        compiler_params=_params(("arbitrary",)),
        name="merge",
    )(za, zh, zm, x, w_out, ln_g, ln_b)


COLUMN_PIECES = 3


def _columns(row_a, row_b):
    pieces = []
    for row in (row_a, row_b):
        rest = row
        for _ in range(COLUMN_PIECES):
            piece = rest.astype(BF16).astype(F32)
            pieces.append(piece)
            rest = rest - piece
    n_pad = HBM_TILE_ROWS - len(pieces)
    tile = jnp.concatenate(pieces + [jnp.zeros((n_pad, HEAD_DIM), F32)], axis=0).astype(BF16)
    j = lax.broadcasted_iota(jnp.int32, (HBM_TILE_ROWS, 2 * HEAD_DIM), 0)
    lane = lax.broadcasted_iota(jnp.int32, (HBM_TILE_ROWS, 2 * HEAD_DIM), 1)
    first = (j < COLUMN_PIECES) & (lane < HEAD_DIM)
    second = (j >= COLUMN_PIECES) & (j < 2 * COLUMN_PIECES) & (lane >= HEAD_DIM)
    select = (first | second).astype(BF16)
    cols = lax.dot_general(tile, select, (((0,), (0,)), ((), ())), preferred_element_type=F32)
    return cols[:, :HEAD_DIM], cols[:, HEAD_DIM:]


HBM_TILE_ROWS = 8
SAMPLE_ROWS = 4


def _window_pieces(past):
    pieces = []
    for dil in DILATIONS:
        if dil == 1:
            pieces.append((0, past - BAND, BAND, None))
        elif dil < HBM_TILE_ROWS:
            n = BAND * dil // HBM_TILE_ROWS
            pieces += [(1, past // HBM_TILE_ROWS - n, n, r) for r in range(0, HBM_TILE_ROWS, dil)]
        else:
            pieces.append((2, past // dil - BAND, BAND, 0))
    return pieces


def _cache_views(cache):
    depth, bsz, past, nh, hd = cache.shape
    assert all(d == 1 or HBM_TILE_ROWS % d == 0 or d % HBM_TILE_ROWS == 0 for d in DILATIONS)
    big = max(DILATIONS)
    hm = cache.transpose(0, 1, 3, 2, 4)
    return (hm, hm.reshape(depth, bsz, nh, past // HBM_TILE_ROWS, HBM_TILE_ROWS, hd),
            hm.reshape(depth, bsz, nh, past // big, big, hd))


def _sample_row(layer, p_ref, cos, sin, k_win, v_win, st_ref, mk_ref, mv_ref, lb_all, ng_ref, z_ref, ko_ref, vo_ref,
                so_ref):
    q_all = _rope(p_ref[G_QA:G_QA + N_HEADS_ATTN, :], cos, sin) * ATTN_SCALE
    k_all = _rope(p_ref[G_KA:G_KA + N_HEADS_ATTN, :], cos, sin)
    v_all = p_ref[G_VA:G_VA + N_HEADS_ATTN, :]
    ko_ref[...] = k_all
    vo_ref[...] = v_all

    for h in range(N_HEADS_ATTN):
        q = q_all[h:h + 1]
        s_new = jnp.sum(q * k_all[h:h + 1], axis=-1, keepdims=True)
        s = jnp.sum(k_win[h] * q, axis=-1, keepdims=True)
        m = jnp.maximum(jnp.max(s, axis=0, keepdims=True), s_new)
        p = jnp.exp(s - m)
        p_new = jnp.exp(s_new - m) * len(DILATIONS)
        den = jnp.sum(p, axis=0, keepdims=True) + p_new
        num = jnp.sum(p * v_win[h], axis=0, keepdims=True) + p_new * v_all[h:h + 1]
        z_ref[h:h + 1, :] = (num / den) * _silu(p_ref[G_GA + h:G_GA + h + 1, :])

    for h in range(N_HEADS_HGRN):
        lanes = slice(h * HEAD_DIM, (h + 1) * HEAD_DIM)
        lb = _lower_bound(lb_all[:, lanes], layer)
        f = lb + (1.0 - lb) * _sigmoid(p_ref[G_FB + h:G_FB + h + 1, :])
        f_col, q_col = _columns(f, p_ref[G_QB + h:G_QB + h + 1, :])
        s_new = f_col * st_ref[h] + (1.0 - f_col) * p_ref[G_IB + h:G_IB + h + 1, :]
        so_ref[h] = s_new
        o = jnp.sum(s_new * q_col, axis=0, keepdims=True)
        z_ref[N_HEADS_ATTN + h:N_HEADS_ATTN + h + 1, :] = _rms_gate(
            o, ng_ref[:, lanes], p_ref[G_GB + h:G_GB + h + 1, :])

    fold = mk_ref.shape[1] // N_HEADS_MEM
    unfold = lambda a: sum(a[i * N_HEADS_MEM:(i + 1) * N_HEADS_MEM] for i in range(fold))
    q_m = p_ref[G_QM:G_QM + N_HEADS_MEM, :] * ATTN_SCALE
    s = jnp.sum(mk_ref[...] * jnp.concatenate([q_m] * fold, axis=0)[None], axis=-1, keepdims=True)
    m = jnp.max(s, axis=0)
    m = functools.reduce(jnp.maximum, [m[i * N_HEADS_MEM:(i + 1) * N_HEADS_MEM] for i in range(fold)])
    p = jnp.exp(s - jnp.concatenate([m] * fold, axis=0)[None])
    o = unfold(jnp.sum(p * mv_ref[...], axis=0)) / unfold(jnp.sum(p, axis=0))
    row = N_HEADS_ATTN + N_HEADS_HGRN
    z_ref[row:row + N_HEADS_MEM, :] = o * _silu(p_ref[G_GM:G_GM + N_HEADS_MEM, :])


def _sample_kernel(layer, past, p_ref, cos_ref, sin_ref, k0_hbm, k1_hbm, k2_hbm, v0_hbm, v1_hbm, v2_hbm,
                   st_ref, mk_ref, mv_ref, lb_ref, ng_ref, z_ref, ko_ref, vo_ref, so_ref, kbuf, vbuf, sem):
    step = pl.program_id(0)
    slot = step % 2
    n_rows = p_ref.shape[0]
    pieces = _window_pieces(past)

    def window_copies(stp, sl):
        out = []
        for rr in range(n_rows):
            for ci, (views, buf) in enumerate((((k0_hbm, k1_hbm, k2_hbm), kbuf), ((v0_hbm, v1_hbm, v2_hbm), vbuf))):
                off = 0
                for pi, (vi, start, count, res) in enumerate(pieces):
                    view, row = views[vi], stp * n_rows + rr
                    src = (view.at[layer, row, :, pl.ds(start, count), :] if res is None
                           else view.at[layer, row, :, pl.ds(start, count), res, :])
                    out.append(pltpu.make_async_copy(src, buf.at[sl, rr, :, pl.ds(off, count), :],
                                                     sem.at[sl, rr, ci, pi]))
                    off += count
        return out

    @pl.when(step == 0)
    def _():
        for c in window_copies(step, slot):
            c.start()

    @pl.when(step + 1 < pl.num_programs(0))
    def _():
        for c in window_copies(step + 1, 1 - slot):
            c.start()

    for c in window_copies(step, slot):
        c.wait()

    cos, sin, lb_all = cos_ref[...], sin_ref[...], lb_ref[...]
    for rr in range(n_rows):
        _sample_row(layer, p_ref.at[rr], cos, sin, kbuf.at[slot, rr], vbuf.at[slot, rr], st_ref.at[rr], mk_ref.at[rr],
                    mv_ref.at[rr], lb_all, ng_ref, z_ref.at[rr], ko_ref.at[rr], vo_ref.at[rr], so_ref.at[rr])


def _sample_mixers(proj_rows, cos, sin, win_k, win_v, state, mem_k, mem_v, lb_raw, norm_g, layer):
    bsz, past = win_k.shape[1:3]
    rps = SAMPLE_ROWS
    assert all(past % d == 0 and past >= w for w, d in zip(WINDOWS, DILATIONS)) and bsz % rps == 0
    assert all(w // d == BAND for w, d in zip(WINDOWS, DILATIONS)) and past % HBM_TILE_ROWS == 0
    n_keys = BAND * len(DILATIONS)
    n_pieces = len(_window_pieces(past))
    any_spec = pl.BlockSpec(memory_space=pl.ANY)
    vec = lambda n: pl.BlockSpec((n, W_HGRN), lambda b: (0, 0))
    heads = lambda n: pl.BlockSpec((rps, n, HEAD_DIM), lambda b: (b, 0, 0))
    fold = HBM_TILE_ROWS // N_HEADS_MEM
    mem_k, mem_v = (a.reshape(a.shape[0], bsz, N_MEM // fold, fold * N_HEADS_MEM, HEAD_DIM) for a in (mem_k, mem_v))
    mem_spec = pl.BlockSpec((None, rps, N_MEM // fold, fold * N_HEADS_MEM, HEAD_DIM), lambda b: (layer, b, 0, 0, 0))
    return pl.pallas_call(
        functools.partial(_sample_kernel, layer, past),
        grid=(bsz // rps,),
        in_specs=[heads(N_GROUPS), pl.BlockSpec((1, HEAD_DIM), lambda b: (0, 0)),
                  pl.BlockSpec((1, HEAD_DIM), lambda b: (0, 0)), *([any_spec] * 6),
                  pl.BlockSpec((None, rps, N_HEADS_HGRN, HEAD_DIM, HEAD_DIM), lambda b: (layer, b, 0, 0, 0)),
                  mem_spec, mem_spec, vec(DEPTH + 1), vec(1)],
        out_specs=[heads(MIX_WIDTH // HEAD_DIM), heads(N_HEADS_ATTN), heads(N_HEADS_ATTN),
                   pl.BlockSpec((rps, N_HEADS_HGRN, HEAD_DIM, HEAD_DIM), lambda b: (b, 0, 0, 0))],
        out_shape=[jax.ShapeDtypeStruct((bsz, MIX_WIDTH // HEAD_DIM, HEAD_DIM), F32),
                   jax.ShapeDtypeStruct((bsz, N_HEADS_ATTN, HEAD_DIM), F32),
                   jax.ShapeDtypeStruct((bsz, N_HEADS_ATTN, HEAD_DIM), F32),
                   jax.ShapeDtypeStruct((bsz, N_HEADS_HGRN, HEAD_DIM, HEAD_DIM), F32)],
        scratch_shapes=[pltpu.VMEM((2, rps, N_HEADS_ATTN, n_keys, HEAD_DIM), F32),
                        pltpu.VMEM((2, rps, N_HEADS_ATTN, n_keys, HEAD_DIM), F32),
                        pltpu.SemaphoreType.DMA((2, rps, 2, n_pieces))],
        compiler_params=_params(("arbitrary",)),
        name="sample_mixers",
    )(proj_rows, cos, sin, *_cache_views(win_k), *_cache_views(win_v), state, mem_k, mem_v, lb_raw, norm_g)


def _rope_tables(pos):
    half = HEAD_DIM // 2
    inv_freq = 1.0 / (ROPE_THETA ** (np.arange(half, dtype=np.float64) / half))
    ang = np.asarray(pos, np.float64)[:, None] * inv_freq[None, :]
    cos, sin = np.cos(ang), np.sin(ang)
    return (jnp.asarray(np.concatenate([cos, cos], axis=-1), F32),
            jnp.asarray(np.concatenate([-sin, sin], axis=-1), F32))


def kernel(x_prompt, x_sample, cache_win_k, cache_win_v, state_hgrn, cache_mem_k, cache_mem_v, mem_prompt,
           w_in, w_mem_kv, hgrn_lb_raw, hgrn_norm_g, w_out, ln_g, ln_b):
    bsz, seq, _ = x_prompt.shape
    dbsz, n_new, _ = x_sample.shape
    assert n_new == 1 and seq % (BAND * max(DILATIONS)) == 0
    cos_p, sin_p = _rope_tables(np.arange(seq))
    cos_s, sin_s = _rope_tables(PAST_LEN + np.arange(n_new))

    hp = x_prompt.reshape(bsz * seq, D_MODEL)
    hs = x_sample.reshape(dbsz * n_new, D_MODEL)
    mem = mem_prompt.reshape(bsz * N_MEM, D_MODEL)
    outs = [[] for _ in range(8)]
    for layer in range(DEPTH):
        lb_raw = hgrn_lb_raw
        norm_g = hgrn_norm_g[layer][None]
        lg, lbias = ln_g[layer][None], ln_b[layer][None]

        proj, proj_s = _project(hp, hs, w_in[layer], PROJ_ROWS, PROJ_FIRST_COLS, PROJ_COLS)
        za, k1, v1 = _prompt_attention(proj, cos_p, sin_p, bsz, seq)
        zh, s1 = _prompt_hgrn(proj, lb_raw, norm_g, layer, bsz, seq)
        zm, mk1, mv1 = _prompt_mem(proj, mem, w_mem_kv[layer], bsz, seq)
        hp, w_out_b = _merge(za, zh, zm, hp, w_out[layer], lg, lbias, MERGE_ROWS)

        proj_s = proj_s.transpose(1, 0, 2)
        zs, k2, v2, s2 = _sample_mixers(proj_s, cos_s, sin_s, cache_win_k, cache_win_v, state_hgrn,
                                        cache_mem_k, cache_mem_v, lb_raw, norm_g, layer)
        zs = zs.transpose(1, 0, 2)
        hs = _merge(zs[:N_HEADS_ATTN], zs[N_HEADS_ATTN:N_HEADS_ATTN + N_HEADS_HGRN],
                    zs[N_HEADS_ATTN + N_HEADS_HGRN:], hs, w_out_b, lg, lbias, dbsz)

        new = (k1.transpose(0, 2, 1, 3), v1.transpose(0, 2, 1, 3), s1, mk1, mv1,
               k2.reshape(dbsz, n_new, N_HEADS_ATTN, HEAD_DIM), v2.reshape(dbsz, n_new, N_HEADS_ATTN, HEAD_DIM),
               s2.astype(state_hgrn.dtype))
        for acc, val in zip(outs, new):
            acc.append(val)

    return (hp.reshape(bsz, seq, D_MODEL), hs.reshape(dbsz, n_new, D_MODEL), *[jnp.stack(o) for o in outs])
```

```python
import functools

import jax
import jax.numpy as jnp
import numpy as np
from jax import lax
from jax.experimental import pallas as pl
from jax.experimental.pallas import tpu as pltpu

F32 = jnp.float32
BF16 = jnp.bfloat16

D_MODEL = 2048
DEPTH = 1
PAST_LEN = 8192
HEAD_DIM = 128
N_HEADS_ATTN = 6
N_HEADS_HGRN = 6
N_HEADS_MEM = 4
W_ATTN = N_HEADS_ATTN * HEAD_DIM
W_HGRN = N_HEADS_HGRN * HEAD_DIM
W_MEM = N_HEADS_MEM * HEAD_DIM
MIX_WIDTH = W_ATTN + W_HGRN + W_MEM
WINDOWS = (128, 512, 2048)
DILATIONS = (1, 4, 16)
N_MEM = 256
ROPE_THETA = 10000.0
LN_EPS = 1e-5
RMS_EPS = 1e-6
NEG_INF = -1e30
DEEPNORM_ALPHA = (2 * DEPTH) ** 0.25
ATTN_SCALE = HEAD_DIM ** -0.5
ATTN_SCALE_LOG2 = ATTN_SCALE * 1.4426950408889634

G_QA, G_KA, G_VA, G_GA = 0, 6, 12, 18
G_QB, G_FB, G_IB, G_GB = 24, 30, 36, 42
G_QM, G_GM = 48, 52
N_GROUPS = 56

LANES = 128
BAND = 128
ATTN_GROUP = 8
HGRN_CHUNK = 64
HGRN_SUB = 16
HGRN_HEADS_PER_STEP = 3
HGRN_PREP_ROWS = 128
HGRN_SCAN_CHUNKS = 32
HGRN_SAFE_LOG_RANGE = 70.0
VMEM_LIMIT = 48 * 1024 * 1024
PROJ_ROWS, PROJ_COLS, PROJ_FIRST_COLS = 1024, 1024, 512
MERGE_ROWS = 512
MERGE_CAST_ROWS = 256


def _params(sem, vmem=VMEM_LIMIT):
    return pltpu.CompilerParams(dimension_semantics=sem, vmem_limit_bytes=vmem)


def _sigmoid(x):
    return 1.0 / (1.0 + jnp.exp(-x))


def _silu(x):
    return x * _sigmoid(x)


def _dot_nt(a, b):
    return lax.dot_general(a, b, (((1,), (1,)), ((), ())), preferred_element_type=F32)


def _store_slabs(ref, acc):
    for c in range(ref.shape[0]):
        ref[c] = acc[:, c * LANES:(c + 1) * LANES]


def _proj_first_kernel(x_ref, xs_ref, w_ref, o_ref, os_ref, wb_ref, xb_ref):
    tm = x_ref.shape[0]

    @pl.when(pl.program_id(0) == 0)
    def _():
        xb_ref[:tm, :] = x_ref[...].astype(BF16)
        xb_ref[tm:, :] = xs_ref[...].astype(BF16)

    wb_ref[...] = w_ref[...].astype(BF16)
    acc = jnp.dot(xb_ref[...], wb_ref[...], preferred_element_type=F32)
    _store_slabs(o_ref, acc[:tm])
    _store_slabs(os_ref, acc[tm:])


def _proj_rest_kernel(x_ref, w_ref, first_ref, o_hbm, xb_ref, obuf, sem_out, sem_first):
    i, j = pl.program_id(0), pl.program_id(1)
    n_col = pl.num_programs(1)
    step = i * n_col + j
    last = pl.num_programs(0) * n_col - 1
    slot = step % 2
    tm = x_ref.shape[0]
    n_slab = obuf.shape[1]

    def tile_copy(stp):
        row_tile, col_tile = stp // n_col + 1, stp % n_col
        return pltpu.make_async_copy(
            obuf.at[stp % 2], o_hbm.at[pl.ds(col_tile * n_slab, n_slab), pl.ds(row_tile * tm, tm), :],
            sem_out.at[stp % 2])

    first_copy = pltpu.make_async_copy(first_ref, o_hbm.at[pl.ds(j * n_slab, n_slab), pl.ds(0, tm), :], sem_first.at[0])

    @pl.when(i == 0)
    def _():
        first_copy.start()

    @pl.when(j == 0)
    def _():
        xb_ref[...] = x_ref[...].astype(BF16)

    @pl.when(step >= 2)
    def _():
        tile_copy(step - 2).wait()

    _store_slabs(obuf.at[slot], jnp.dot(xb_ref[...], w_ref[...], preferred_element_type=F32))
    tile_copy(step).start()

    @pl.when(i == 0)
    def _():
        first_copy.wait()

    @pl.when(step == last)
    def _():
        tile_copy(step - 1).wait()
        tile_copy(step).wait()


def _project(x, xs, w, tm, tn_first, tn):
    m, k = x.shape
    ms = xs.shape[0]
    n = w.shape[1]
    assert (m // tm - 1) * (n // tn) >= 2
    proj_first, proj_s, w_bf16 = pl.pallas_call(
        _proj_first_kernel,
        grid=(n // tn_first,),
        in_specs=[pl.BlockSpec((tm, k), lambda j: (0, 0)),
                  pl.BlockSpec((ms, k), lambda j: (0, 0)),
                  pl.BlockSpec((k, tn_first), lambda j: (0, j))],
        out_specs=[pl.BlockSpec((tn_first // LANES, tm, LANES), lambda j: (j, 0, 0)),
                   pl.BlockSpec((tn_first // LANES, ms, LANES), lambda j: (j, 0, 0)),
                   pl.BlockSpec((k, tn_first), lambda j: (0, j))],
        out_shape=[jax.ShapeDtypeStruct((n // LANES, tm, LANES), F32),
                   jax.ShapeDtypeStruct((n // LANES, ms, LANES), F32),
                   jax.ShapeDtypeStruct((k, n), BF16)],
        scratch_shapes=[pltpu.VMEM((tm + ms, k), BF16)],
        compiler_params=_params(("arbitrary",)),
        name="in_proj_first",
    )(x, xs, w)
    proj = pl.pallas_call(
        _proj_rest_kernel,
        grid=(m // tm - 1, n // tn),
        in_specs=[pl.BlockSpec((tm, k), lambda i, j: (i + 1, 0)),
                  pl.BlockSpec((k, tn), lambda i, j: (0, j)),
                  pl.BlockSpec((tn // LANES, tm, LANES), lambda i, j: (jnp.where(i == 0, j, n // tn - 1), 0, 0))],
        out_specs=pl.BlockSpec(memory_space=pl.ANY),
        out_shape=jax.ShapeDtypeStruct((n // LANES, m, LANES), F32),
        scratch_shapes=[pltpu.VMEM((tm, k), BF16), pltpu.VMEM((2, tn // LANES, tm, LANES), F32),
                        pltpu.SemaphoreType.DMA((2,)), pltpu.SemaphoreType.DMA((1,))],
        compiler_params=_params(("arbitrary", "arbitrary")),
        name="in_proj",
    )(x, w_bf16, proj_first)
    return proj, proj_s


def _rope(x, cos, sin_signed):
    return x * cos + pltpu.roll(x, HEAD_DIM // 2, 1) * sin_signed


def _bias_from_count(count):
    return jnp.where(count > 1.5, 1.0, jnp.where(count > 0.5, 0.0, NEG_INF)).astype(F32)


def _block_deltas(block_gap):
    qi = lax.broadcasted_iota(jnp.int32, (BAND, BAND), 0)
    ki = lax.broadcasted_iota(jnp.int32, (BAND, BAND), 1)
    return block_gap * BAND + qi - ki


def _near_bias(block_gap):
    d = _block_deltas(block_gap)
    return _bias_from_count(((d >= 0) & (d <= BAND)).astype(F32))


def _class_bias(block_gap, far_step):
    d = _block_deltas(block_gap)
    near = (d >= 0) & (d <= BAND)
    far = (d >= 0) & ((d & (far_step - 1)) == 0)
    return _bias_from_count(near.astype(F32) + far.astype(F32))


def _softmax_stage(scores, floors=None):
    out = []
    for idx, s in enumerate(scores):
        m = jnp.broadcast_to(jnp.max(s, axis=-1, keepdims=True), (BAND, HEAD_DIM))
        if floors is not None:
            m = jnp.maximum(m, floors[idx])
        m_wide = jnp.concatenate([m] * (s.shape[1] // HEAD_DIM), axis=1)
        out.append((m, jnp.exp2(s - m_wide).astype(BF16)))
    return out


def _attn_kernel(q_ref, k_ref, v_ref, g_ref, cos_ref, sin_ref, z_ref, ko_ref, vo_ref,
                 qs_ref, qb_ref, kb_ref, vb_ref, qc_ref, kc_ref, vc_ref, acc_ref, m_ref, l_ref):
    seq = q_ref.shape[0]
    d_mid, d_far = DILATIONS[1], DILATIONS[2]
    far_step = d_far // d_mid
    cls_rows = seq // d_mid
    cos = cos_ref[...]
    sin = sin_ref[...]
    qs_ref[...] = _rope(q_ref[...], cos, sin) * ATTN_SCALE_LOG2
    ko_ref[...] = _rope(k_ref[...], cos, sin)
    vo_ref[...] = v_ref[...]
    qb_ref[...] = qs_ref[...].astype(BF16)
    kb_ref[...] = ko_ref[...].astype(BF16)
    ones = jnp.ones((seq, HEAD_DIM), BF16)
    vb_ref[:, :HEAD_DIM] = v_ref[...].astype(BF16)
    vb_ref[:, HEAD_DIM:] = ones
    vc_ref[:, HEAD_DIM:] = ones
    for r in range(d_mid):
        cls, dst = pl.ds(r, cls_rows, stride=d_mid), pl.ds(r * cls_rows, cls_rows)
        qc_ref[dst, :] = qs_ref[cls, :].astype(BF16)
        kc_ref[dst, :] = ko_ref[cls, :].astype(BF16)
        vc_ref[dst, :HEAD_DIM] = v_ref[cls, :].astype(BF16)

    n_cls_blk = cls_rows // BAND
    cls_bias = [_class_bias(gap, far_step) for gap in range(min(n_cls_blk, 3))]
    items = [(r, n) for r in range(d_mid) for n in range(n_cls_blk)]
    for g0 in range(0, len(items), ATTN_GROUP):
        group = items[g0:g0 + ATTN_GROUP]
        keys = [pl.ds(r * cls_rows, (n + 1) * BAND) for r, n in group]
        scores = [_dot_nt(qc_ref[pl.ds(r * cls_rows + n * BAND, BAND), :], kc_ref[kr, :])
                  + jnp.concatenate([cls_bias[min(n - nk, 2)] for nk in range(n + 1)], axis=1)
                  for (r, n), kr in zip(group, keys)]
        for (r, n), kr, (m, p) in zip(group, keys, _softmax_stage(scores)):
            rows = pl.ds(r + d_mid * BAND * n, BAND, stride=d_mid)
            pv = jnp.dot(p, vc_ref[kr, :], preferred_element_type=F32)
            acc_ref[rows, :] = pv[:, :HEAD_DIM]
            l_ref[rows, :] = pv[:, HEAD_DIM:]
            m_ref[rows, :] = m

    near_bias = [_near_bias(0), jnp.concatenate([_near_bias(1), _near_bias(0)], axis=1)]
    items = list(range(seq // BAND))
    for g0 in range(0, len(items), ATTN_GROUP):
        group = items[g0:g0 + ATTN_GROUP]
        rows = [pl.ds(n * BAND, BAND) for n in group]
        keys = [pl.ds(max(n - 1, 0) * BAND, BAND * min(n + 1, 2)) for n in group]
        scores = [_dot_nt(qb_ref[rw, :], kb_ref[kr, :]) + near_bias[min(n, 1)]
                  for n, rw, kr in zip(group, rows, keys)]
        probs = _softmax_stage(scores, floors=[m_ref[rw, :] for rw in rows])
        for rw, kr, (m, p) in zip(rows, keys, probs):
            w = jnp.exp2(m_ref[rw, :] - m)
            pv = jnp.dot(p, vb_ref[kr, :], preferred_element_type=F32)
            num = pv[:, :HEAD_DIM] + w * acc_ref[rw, :]
            den = pv[:, HEAD_DIM:] + w * l_ref[rw, :]
            z_ref[rw, :] = ((num / den) * _silu(g_ref[rw, :])).astype(z_ref.dtype)


def _prompt_attention(proj, cos, sin, bsz, seq):
    def col(g0):
        return pl.BlockSpec((None, seq, HEAD_DIM), lambda b, h: (g0 + h, b, 0))

    table = pl.BlockSpec((seq, HEAD_DIM), lambda b, h: (0, 0))
    kv_out = pl.BlockSpec((None, None, seq, HEAD_DIM), lambda b, h: (b, h, 0, 0))
    d_near, d_mid, d_far = DILATIONS
    far_step = d_far // d_mid
    assert d_near == 1 and d_far % d_mid == 0 and far_step & (far_step - 1) == 0
    assert all(w // d == BAND for w, d in zip(WINDOWS, DILATIONS)) and seq % (d_mid * BAND) == 0
    assert seq // d_mid <= far_step * BAND
    return pl.pallas_call(
        _attn_kernel,
        grid=(bsz, N_HEADS_ATTN),
        in_specs=[col(G_QA), col(G_KA), col(G_VA), col(G_GA), table, table],
        out_specs=[pl.BlockSpec((None, seq, HEAD_DIM), lambda b, h: (h, b, 0)), kv_out, kv_out],
        out_shape=[jax.ShapeDtypeStruct((N_HEADS_ATTN, bsz * seq, HEAD_DIM), BF16),
                   jax.ShapeDtypeStruct((bsz, N_HEADS_ATTN, seq, HEAD_DIM), F32),
                   jax.ShapeDtypeStruct((bsz, N_HEADS_ATTN, seq, HEAD_DIM), F32)],
        scratch_shapes=[pltpu.VMEM((seq, HEAD_DIM), F32)]
                       + [pltpu.VMEM((seq, HEAD_DIM), BF16), pltpu.VMEM((seq, HEAD_DIM), BF16),
                          pltpu.VMEM((seq, 2 * HEAD_DIM), BF16)] * 2
                       + [pltpu.VMEM((seq, HEAD_DIM), F32)] * 3,
        compiler_params=_params(("parallel", "parallel")),
        name="prompt_attn",
    )(proj, proj, proj, proj, cos, sin)


def _lower_bound(lb_raw, layer):
    e = jnp.exp(lb_raw - jnp.max(lb_raw, axis=0, keepdims=True))
    sm = e / jnp.sum(e, axis=0, keepdims=True)
    return jnp.sum(sm[:layer + 1], axis=0, keepdims=True)


def _split2(x):
    hi = x.astype(BF16)
    return hi, (x - hi.astype(F32)).astype(BF16)


def _rms_gate(o, norm_g, gate):
    o = o * lax.rsqrt(jnp.mean(o * o, axis=-1, keepdims=True) + RMS_EPS)
    return o * norm_g * _silu(gate)


def _hgrn_gates(fb, lb, tril):
    f = lb + (1.0 - lb) * _sigmoid(fb)
    g = jnp.log(f)
    b = sum(jnp.dot(tril, piece, preferred_element_type=F32) for piece in _split2(g))
    return 1.0 - f, b


def _hgrn_fast_prepare(q_ref, f_ref, lb, qh_ref, kh_ref, el_ref):
    n_heads, seq, _ = q_ref.shape
    blk = HGRN_PREP_ROWS
    tril = _tril_ones(blk)

    def body(bi, carry):
        r0 = pl.multiple_of(bi * blk, blk)
        fb = jnp.concatenate([f_ref[h, pl.ds(r0, blk), :] for h in range(n_heads)], axis=1)
        f = lb + (1.0 - lb) * _sigmoid(fb)
        b_blk = sum(jnp.dot(tril, piece, preferred_element_type=F32) for piece in _split2(jnp.log(f)))
        kk = 1.0 - f
        for c0 in range(0, blk, HGRN_CHUNK):
            rs = slice(c0, c0 + HGRN_CHUNK)
            b = b_blk[rs] - b_blk[c0 - 1:c0] if c0 else b_blk[rs]
            rows = pl.ds(r0 + c0, HGRN_CHUNK)
            q = jnp.concatenate([q_ref[h, rows, :] for h in range(n_heads)], axis=1)
            qh_ref[rows, :] = (q * jnp.exp(b)).astype(BF16)
            kh_ref[rows, :] = (kk[rs] * jnp.exp(-b)).astype(BF16)
            el_ref[pl.ds(bi * (blk // HGRN_CHUNK) + c0 // HGRN_CHUNK, 1), :] = jnp.exp(b[HGRN_CHUNK - 1:])
        return carry

    lax.fori_loop(0, seq // blk, body, 0, unroll=4)


def _hgrn_fast_scan(i_ref, g_ref, norm_g, qh_ref, kh_ref, el_ref, st_ref, z_ref):
    n_heads, seq, _ = i_ref.shape
    c = HGRN_CHUNK
    heads = range(n_heads)
    chunks = range(HGRN_SCAN_CHUNKS)
    lanes = [slice(h * HEAD_DIM, (h + 1) * HEAD_DIM) for h in heads]
    causal = lax.broadcasted_iota(jnp.int32, (c, c), 0) >= lax.broadcasted_iota(jnp.int32, (c, c), 1)
    tn = (((0,), (0,)), ((), ()))

    def body(ti, carry):
        rows = [pl.ds(pl.multiple_of((ti * HGRN_SCAN_CHUNKS + k) * c, c), c) for k in chunks]
        qh = [[qh_ref[rows[k], lanes[h]] for h in heads] for k in chunks]
        kh = [[kh_ref[rows[k], lanes[h]] for h in heads] for k in chunks]
        vb = [[i_ref[h, rows[k], :].astype(BF16) for h in heads] for k in chunks]
        att = [[jnp.where(causal, _dot_nt(qh[k][h], kh[k][h]), 0.0).astype(BF16) for h in heads] for k in chunks]
        ds = [[lax.dot_general(vb[k][h], kh[k][h], tn, preferred_element_type=F32) for h in heads] for k in chunks]
        st = [st_ref[h] for h in heads]
        o = []
        for k in chunks:
            el = el_ref[pl.ds(ti * HGRN_SCAN_CHUNKS + k, 1), :]
            o.append([_dot_nt(qh[k][h], st[h].astype(BF16)) for h in heads])
            st = [(st[h] + ds[k][h]) * el[:, lanes[h]] for h in heads]
        for h in heads:
            st_ref[h] = st[h]
        intra = [[jnp.dot(att[k][h], vb[k][h], preferred_element_type=F32) for h in heads] for k in chunks]
        for k in chunks:
            for h in heads:
                z_ref[h, rows[k], :] = _rms_gate(o[k][h] + intra[k][h], norm_g[:, lanes[h]],
                                                 g_ref[h, rows[k], :]).astype(z_ref.dtype)
        return carry

    lax.fori_loop(0, seq // (c * HGRN_SCAN_CHUNKS), body, 0)


def _hgrn_chunk(q, fb, v, lb, st, tril):
    c = q.shape[0]
    kk, b = _hgrn_gates(fb, lb, tril)
    o = _dot_nt((q * jnp.exp(b)).astype(BF16), st.astype(BF16))

    s_idx = lax.broadcasted_iota(jnp.int32, (c, 1), 0)
    lane = lax.broadcasted_iota(jnp.int32, (HGRN_SUB, c), 1)
    row = lax.broadcasted_iota(jnp.int32, (HGRN_SUB, c), 0)
    att_rows = []
    for i0 in range(0, c, HGRN_SUB):
        qi = q[i0:i0 + HGRN_SUB]
        bi = b[i0:i0 + HGRN_SUB]
        if i0 > 0:
            bref = b[i0 - 1:i0]
            kt = jnp.where(s_idx < i0, kk * jnp.exp(jnp.minimum(bref - b, 0.0)), 0.0)
            att = _dot_nt((qi * jnp.exp(bi - bref)).astype(BF16), kt.astype(BF16))
        else:
            att = jnp.zeros((HGRN_SUB, c), F32)
        for j in range(HGRN_SUB):
            s = i0 + j
            e = jnp.exp(jnp.minimum(bi - b[s:s + 1], 0.0))
            colv = jnp.sum(qi * kk[s:s + 1] * e, axis=-1, keepdims=True)
            att = jnp.where((lane == s) & (row >= j), colv, att)
        att_rows.append(att)
    att = jnp.concatenate(att_rows, axis=0)
    vb = v.astype(BF16)
    o = o + jnp.dot(att.astype(BF16), vb, preferred_element_type=F32)

    b_last = b[c - 1:c]
    kd = (kk * jnp.exp(b_last - b)).astype(BF16)
    st_new = st * jnp.exp(b_last) + lax.dot_general(vb, kd, (((0,), (0,)), ((), ())),
                                                    preferred_element_type=F32)
    return o, st_new


def _tril_ones(c):
    return (lax.broadcasted_iota(jnp.int32, (c, c), 0) >= lax.broadcasted_iota(jnp.int32, (c, c), 1)
            ).astype(BF16)


def _hgrn_kernel(layer, q_ref, f_ref, i_ref, g_ref, lb_ref, ng_ref, z_ref, s_ref, st_ref, qh_ref, kh_ref, el_ref):
    n_heads, seq, _ = q_ref.shape
    lb = _lower_bound(lb_ref[...], layer)
    norm_g = ng_ref[...]
    st_ref[...] = jnp.zeros_like(st_ref)

    def fast():
        _hgrn_fast_prepare(q_ref, f_ref, lb, qh_ref, kh_ref, el_ref)
        _hgrn_fast_scan(i_ref, g_ref, norm_g, qh_ref, kh_ref, el_ref, st_ref, z_ref)

    def safe():
        tril = _tril_ones(HGRN_CHUNK)

        def body(ci, carry):
            rows = pl.ds(pl.multiple_of(ci * HGRN_CHUNK, HGRN_CHUNK), HGRN_CHUNK)
            for h in range(n_heads):
                lanes = slice(h * HEAD_DIM, (h + 1) * HEAD_DIM)
                o, st = _hgrn_chunk(q_ref[h, rows, :], f_ref[h, rows, :], i_ref[h, rows, :], lb[:, lanes],
                                    st_ref[h], tril)
                st_ref[h] = st
                z_ref[h, rows, :] = _rms_gate(o, norm_g[:, lanes], g_ref[h, rows, :]).astype(z_ref.dtype)
            return carry

        lax.fori_loop(0, seq // HGRN_CHUNK, body, 0)

    fast_ok = HGRN_CHUNK * -jnp.log(jnp.min(lb)) <= HGRN_SAFE_LOG_RANGE
    lax.cond(fast_ok, fast, safe)
    for h in range(n_heads):
        s_ref[h] = st_ref[h].T


def _prompt_hgrn(proj, lb_raw, norm_g, layer, bsz, seq):
    hps = HGRN_HEADS_PER_STEP
    assert N_HEADS_HGRN % hps == 0 and all(g % hps == 0 for g in (G_QB, G_FB, G_IB, G_GB))
    assert seq % (HGRN_CHUNK * HGRN_SCAN_CHUNKS) == 0 and seq % HGRN_PREP_ROWS == 0 and HGRN_PREP_ROWS % HGRN_CHUNK == 0

    def cols(g0):
        return pl.BlockSpec((hps, seq, HEAD_DIM), lambda b, h: (g0 // hps + h, b, 0))

    return pl.pallas_call(
        functools.partial(_hgrn_kernel, layer),
        grid=(bsz, N_HEADS_HGRN // hps),
        in_specs=[cols(G_QB), cols(G_FB), cols(G_IB), cols(G_GB),
                  pl.BlockSpec((DEPTH + 1, hps * HEAD_DIM), lambda b, h: (0, h)),
                  pl.BlockSpec((1, hps * HEAD_DIM), lambda b, h: (0, h))],
        out_specs=[pl.BlockSpec((hps, seq, HEAD_DIM), lambda b, h: (h, b, 0)),
                   pl.BlockSpec((None, hps, HEAD_DIM, HEAD_DIM), lambda b, h: (b, h, 0, 0))],
        out_shape=[jax.ShapeDtypeStruct((N_HEADS_HGRN, bsz * seq, HEAD_DIM), BF16),
                   jax.ShapeDtypeStruct((bsz, N_HEADS_HGRN, HEAD_DIM, HEAD_DIM), F32)],
        scratch_shapes=[pltpu.VMEM((hps, HEAD_DIM, HEAD_DIM), F32),
                        pltpu.VMEM((seq, hps * HEAD_DIM), BF16),
                        pltpu.VMEM((seq, hps * HEAD_DIM), BF16),
                        pltpu.VMEM((seq // HGRN_CHUNK, hps * HEAD_DIM), F32)],
        compiler_params=_params(("parallel", "parallel")),
        name="prompt_hgrn",
    )(proj, proj, proj, proj, lb_raw, norm_g)


MEM_ROWS = 256
MEM_GROUP = 4


def _mem_kernel(q_ref, g_ref, mem_ref, w_ref, z_ref, mk_ref, mv_ref, wb_ref):
    seq = q_ref.shape[1]

    @pl.when(pl.program_id(0) == 0)
    def _():
        wb_ref[...] = w_ref[...].astype(BF16)

    kv = jnp.dot(mem_ref[...].astype(BF16), wb_ref[...], preferred_element_type=F32)
    ones = jnp.ones((N_MEM, HEAD_DIM), BF16)
    mk_b, mv_b = [], []
    for h in range(N_HEADS_MEM):
        mk = kv[:, h * HEAD_DIM:(h + 1) * HEAD_DIM]
        mv = kv[:, W_MEM + h * HEAD_DIM:W_MEM + (h + 1) * HEAD_DIM]
        mk_ref[:, h, :] = mk
        mv_ref[:, h, :] = mv
        mk_b.append(mk.astype(BF16))
        mv_b.append(jnp.concatenate([mv.astype(BF16), ones], axis=1))

    items = [(h, pl.ds(n * MEM_ROWS, MEM_ROWS)) for h in range(N_HEADS_MEM) for n in range(seq // MEM_ROWS)]
    for g0 in range(0, len(items), MEM_GROUP):
        group = items[g0:g0 + MEM_GROUP]
        scores = [_dot_nt((q_ref[h, rows, :] * ATTN_SCALE_LOG2).astype(BF16), mk_b[h]) for h, rows in group]
        probs = [jnp.exp2(s - jnp.max(s, axis=-1, keepdims=True)).astype(BF16) for s in scores]
        for (h, rows), p in zip(group, probs):
            pv = jnp.dot(p, mv_b[h], preferred_element_type=F32)
            o = pv[:, :HEAD_DIM] / pv[:, HEAD_DIM:]
            z_ref[h, rows, :] = (o * _silu(g_ref[h, rows, :])).astype(z_ref.dtype)


def _prompt_mem(proj, mem, w_kv, bsz, seq):
    assert G_QM % N_HEADS_MEM == 0 and G_GM % N_HEADS_MEM == 0

    def cols(g0):
        return pl.BlockSpec((N_HEADS_MEM, seq, HEAD_DIM), lambda b: (g0 // N_HEADS_MEM, b, 0))

    kv_out = pl.BlockSpec((None, N_MEM, N_HEADS_MEM, HEAD_DIM), lambda b: (b, 0, 0, 0))
    kv_shape = jax.ShapeDtypeStruct((bsz, N_MEM, N_HEADS_MEM, HEAD_DIM), F32)
    return pl.pallas_call(
        _mem_kernel,
        grid=(bsz,),
        in_specs=[cols(G_QM), cols(G_GM), pl.BlockSpec((N_MEM, D_MODEL), lambda b: (b, 0)),
                  pl.BlockSpec((D_MODEL, 2 * W_MEM), lambda b: (0, 0))],
        out_specs=[pl.BlockSpec((N_HEADS_MEM, seq, HEAD_DIM), lambda b: (0, b, 0)), kv_out, kv_out],
        out_shape=[jax.ShapeDtypeStruct((N_HEADS_MEM, bsz * seq, HEAD_DIM), BF16), kv_shape, kv_shape],
        scratch_shapes=[pltpu.VMEM((D_MODEL, 2 * W_MEM), BF16)],
        compiler_params=_params(("arbitrary",)),
        name="prompt_mem",
    )(proj, proj, mem, w_kv)


def _merge_kernel(cast_w, za_ref, zh_ref, zm_ref, x_ref, w_ref, lg_ref, lb_ref, o_ref, *rest):
    if cast_w:
        wb_ref, z_ref, stage, sem = rest
        n_rows = stage.shape[1]
        n_chunks = w_ref.shape[0] // n_rows

        def chunk_copy(c):
            return pltpu.make_async_copy(w_ref.at[pl.ds(c * n_rows, n_rows), :], stage.at[c % 2], sem.at[c % 2])

        @pl.when(pl.program_id(0) == 0)
        def _():
            chunk_copy(0).start()
            for c in range(n_chunks):
                if c + 1 < n_chunks:
                    chunk_copy(c + 1).start()
                chunk_copy(c).wait()
                wb_ref[c * n_rows:(c + 1) * n_rows, :] = stage[c % 2].astype(BF16)
    else:
        wb_ref, (z_ref,) = w_ref, rest

    c0 = 0
    for ref in (za_ref, zh_ref, zm_ref):
        for c in range(ref.shape[0]):
            z_ref[:, (c0 + c) * LANES:(c0 + c + 1) * LANES] = ref[c].astype(BF16)
        c0 += ref.shape[0]
    y = jnp.dot(z_ref[...], wb_ref[...], preferred_element_type=F32)
    r = DEEPNORM_ALPHA * x_ref[...] + y
    mu = jnp.mean(r, axis=-1, keepdims=True)
    d = r - mu
    var = jnp.mean(d * d, axis=-1, keepdims=True)
    o_ref[...] = d * lax.rsqrt(var + LN_EPS) * lg_ref[...] + lb_ref[...]


def _merge(za, zh, zm, x, w_out, ln_g, ln_b, tm):
    m = x.shape[0]
    cast_w = w_out.dtype != BF16

    def slab(a):
        return pl.BlockSpec((a.shape[0], tm, LANES), lambda i: (0, i, 0))

    const = lambda shape: pl.BlockSpec(shape, lambda i: (0, 0))
    y_spec, y_shape = pl.BlockSpec((tm, D_MODEL), lambda i: (i, 0)), jax.ShapeDtypeStruct((m, D_MODEL), F32)
    w_block = const((MIX_WIDTH, D_MODEL))
    z_scratch = pltpu.VMEM((tm, MIX_WIDTH), BF16)
    if cast_w:
        w_spec = pl.BlockSpec(memory_space=pl.ANY)
        out_specs, out_shape = [y_spec, w_block], [y_shape, jax.ShapeDtypeStruct((MIX_WIDTH, D_MODEL), BF16)]
        scratch = [z_scratch, pltpu.VMEM((2, MERGE_CAST_ROWS, D_MODEL), F32), pltpu.SemaphoreType.DMA((2,))]
    else:
        w_spec, out_specs, out_shape, scratch = w_block, y_spec, y_shape, [z_scratch]
    return pl.pallas_call(
        functools.partial(_merge_kernel, cast_w),
        grid=(m // tm,),
        in_specs=[slab(za), slab(zh), slab(zm), pl.BlockSpec((tm, D_MODEL), lambda i: (i, 0)),
                  w_spec, const((1, D_MODEL)), const((1, D_MODEL))],
        out_specs=out_specs,
        out_shape=out_shape,
        scratch_shapes=scratch,
        compiler_params=_params(("arbitrary",)),
        name="merge",
    )(za, zh, zm, x, w_out, ln_g, ln_b)


def _column(row):
    return jnp.broadcast_to(row, (HEAD_DIM, HEAD_DIM)).T


HBM_TILE_ROWS = 8
SAMPLE_ROWS = 4


def _window_pieces(past):
    pieces = []
    for dil in DILATIONS:
        if dil == 1:
            pieces.append((0, past - BAND, BAND, None))
        elif dil < HBM_TILE_ROWS:
            n = BAND * dil // HBM_TILE_ROWS
            pieces += [(1, past // HBM_TILE_ROWS - n, n, r) for r in range(0, HBM_TILE_ROWS, dil)]
        else:
            pieces.append((2, past // dil - BAND, BAND, 0))
    return pieces


def _cache_views(cache):
    depth, bsz, past, nh, hd = cache.shape
    assert all(d == 1 or HBM_TILE_ROWS % d == 0 or d % HBM_TILE_ROWS == 0 for d in DILATIONS)
    big = max(DILATIONS)
    hm = cache.transpose(0, 1, 3, 2, 4)
    return (hm, hm.reshape(depth, bsz, nh, past // HBM_TILE_ROWS, HBM_TILE_ROWS, hd),
            hm.reshape(depth, bsz, nh, past // big, big, hd))


def _sample_row(layer, p_ref, cos, sin, k_win, v_win, st_ref, mk_ref, mv_ref, lb_all, ng_ref, z_ref, ko_ref, vo_ref,
                so_ref):
    q_all = _rope(p_ref[G_QA:G_QA + N_HEADS_ATTN, :], cos, sin) * ATTN_SCALE
    k_all = _rope(p_ref[G_KA:G_KA + N_HEADS_ATTN, :], cos, sin)
    v_all = p_ref[G_VA:G_VA + N_HEADS_ATTN, :]
    ko_ref[...] = k_all
    vo_ref[...] = v_all

    for h in range(N_HEADS_ATTN):
        q = q_all[h:h + 1]
        s_new = jnp.sum(q * k_all[h:h + 1], axis=-1, keepdims=True)
        s = jnp.sum(k_win[h] * q, axis=-1, keepdims=True)
        m = jnp.maximum(jnp.max(s, axis=0, keepdims=True), s_new)
        p = jnp.exp(s - m)
        p_new = jnp.exp(s_new - m) * len(DILATIONS)
        den = jnp.sum(p, axis=0, keepdims=True) + p_new
        num = jnp.sum(p * v_win[h], axis=0, keepdims=True) + p_new * v_all[h:h + 1]
        z_ref[h:h + 1, :] = (num / den) * _silu(p_ref[G_GA + h:G_GA + h + 1, :])

    for h in range(N_HEADS_HGRN):
        lanes = slice(h * HEAD_DIM, (h + 1) * HEAD_DIM)
        lb = _lower_bound(lb_all[:, lanes], layer)
        f = lb + (1.0 - lb) * _sigmoid(p_ref[G_FB + h:G_FB + h + 1, :])
        f_col = _column(f)
        q_col = _column(p_ref[G_QB + h:G_QB + h + 1, :])
        s_new = f_col * st_ref[h] + (1.0 - f_col) * p_ref[G_IB + h:G_IB + h + 1, :]
        so_ref[h] = s_new
        o = jnp.sum(s_new * q_col, axis=0, keepdims=True)
        z_ref[N_HEADS_ATTN + h:N_HEADS_ATTN + h + 1, :] = _rms_gate(
            o, ng_ref[:, lanes], p_ref[G_GB + h:G_GB + h + 1, :])

    fold = mk_ref.shape[1] // N_HEADS_MEM
    unfold = lambda a: sum(a[i * N_HEADS_MEM:(i + 1) * N_HEADS_MEM] for i in range(fold))
    q_m = p_ref[G_QM:G_QM + N_HEADS_MEM, :] * ATTN_SCALE
    s = jnp.sum(mk_ref[...] * jnp.concatenate([q_m] * fold, axis=0)[None], axis=-1, keepdims=True)
    m = jnp.max(s, axis=0)
    m = functools.reduce(jnp.maximum, [m[i * N_HEADS_MEM:(i + 1) * N_HEADS_MEM] for i in range(fold)])
    p = jnp.exp(s - jnp.concatenate([m] * fold, axis=0)[None])
    o = unfold(jnp.sum(p * mv_ref[...], axis=0)) / unfold(jnp.sum(p, axis=0))
    row = N_HEADS_ATTN + N_HEADS_HGRN
    z_ref[row:row + N_HEADS_MEM, :] = o * _silu(p_ref[G_GM:G_GM + N_HEADS_MEM, :])


def _sample_kernel(layer, past, p_ref, cos_ref, sin_ref, k0_hbm, k1_hbm, k2_hbm, v0_hbm, v1_hbm, v2_hbm,
                   st_ref, mk_ref, mv_ref, lb_ref, ng_ref, z_ref, ko_ref, vo_ref, so_ref, kbuf, vbuf, sem):
    step = pl.program_id(0)
    slot = step % 2
    n_rows = p_ref.shape[0]
    pieces = _window_pieces(past)

    def window_copies(stp, sl):
        out = []
        for rr in range(n_rows):
            for ci, (views, buf) in enumerate((((k0_hbm, k1_hbm, k2_hbm), kbuf), ((v0_hbm, v1_hbm, v2_hbm), vbuf))):
                off = 0
                for pi, (vi, start, count, res) in enumerate(pieces):
                    view, row = views[vi], stp * n_rows + rr
                    src = (view.at[layer, row, :, pl.ds(start, count), :] if res is None
                           else view.at[layer, row, :, pl.ds(start, count), res, :])
                    out.append(pltpu.make_async_copy(src, buf.at[sl, rr, :, pl.ds(off, count), :],
                                                     sem.at[sl, rr, ci, pi]))
                    off += count
        return out

    @pl.when(step == 0)
    def _():
        for c in window_copies(step, slot):
            c.start()

    @pl.when(step + 1 < pl.num_programs(0))
    def _():
        for c in window_copies(step + 1, 1 - slot):
            c.start()

    for c in window_copies(step, slot):
        c.wait()

    cos, sin, lb_all = cos_ref[...], sin_ref[...], lb_ref[...]
    for rr in range(n_rows):
        _sample_row(layer, p_ref.at[rr], cos, sin, kbuf.at[slot, rr], vbuf.at[slot, rr], st_ref.at[rr], mk_ref.at[rr],
                    mv_ref.at[rr], lb_all, ng_ref, z_ref.at[rr], ko_ref.at[rr], vo_ref.at[rr], so_ref.at[rr])


def _sample_mixers(proj_rows, cos, sin, win_k, win_v, state, mem_k, mem_v, lb_raw, norm_g, layer):
    bsz, past = win_k.shape[1:3]
    rps = SAMPLE_ROWS
    assert all(past % d == 0 and past >= w for w, d in zip(WINDOWS, DILATIONS)) and bsz % rps == 0
    assert all(w // d == BAND for w, d in zip(WINDOWS, DILATIONS)) and past % HBM_TILE_ROWS == 0
    n_keys = BAND * len(DILATIONS)
    n_pieces = len(_window_pieces(past))
    any_spec = pl.BlockSpec(memory_space=pl.ANY)
    vec = lambda n: pl.BlockSpec((n, W_HGRN), lambda b: (0, 0))
    heads = lambda n: pl.BlockSpec((rps, n, HEAD_DIM), lambda b: (b, 0, 0))
    fold = HBM_TILE_ROWS // N_HEADS_MEM
    mem_k, mem_v = (a.reshape(a.shape[0], bsz, N_MEM // fold, fold * N_HEADS_MEM, HEAD_DIM) for a in (mem_k, mem_v))
    mem_spec = pl.BlockSpec((None, rps, N_MEM // fold, fold * N_HEADS_MEM, HEAD_DIM), lambda b: (layer, b, 0, 0, 0))
    return pl.pallas_call(
        functools.partial(_sample_kernel, layer, past),
        grid=(bsz // rps,),
        in_specs=[heads(N_GROUPS), pl.BlockSpec((1, HEAD_DIM), lambda b: (0, 0)),
                  pl.BlockSpec((1, HEAD_DIM), lambda b: (0, 0)), *([any_spec] * 6),
                  pl.BlockSpec((None, rps, N_HEADS_HGRN, HEAD_DIM, HEAD_DIM), lambda b: (layer, b, 0, 0, 0)),
                  mem_spec, mem_spec, vec(DEPTH + 1), vec(1)],
        out_specs=[heads(MIX_WIDTH // HEAD_DIM), heads(N_HEADS_ATTN), heads(N_HEADS_ATTN),
                   pl.BlockSpec((rps, N_HEADS_HGRN, HEAD_DIM, HEAD_DIM), lambda b: (b, 0, 0, 0))],
        out_shape=[jax.ShapeDtypeStruct((bsz, MIX_WIDTH // HEAD_DIM, HEAD_DIM), F32),
                   jax.ShapeDtypeStruct((bsz, N_HEADS_ATTN, HEAD_DIM), F32),
                   jax.ShapeDtypeStruct((bsz, N_HEADS_ATTN, HEAD_DIM), F32),
                   jax.ShapeDtypeStruct((bsz, N_HEADS_HGRN, HEAD_DIM, HEAD_DIM), F32)],
        scratch_shapes=[pltpu.VMEM((2, rps, N_HEADS_ATTN, n_keys, HEAD_DIM), F32),
                        pltpu.VMEM((2, rps, N_HEADS_ATTN, n_keys, HEAD_DIM), F32),
                        pltpu.SemaphoreType.DMA((2, rps, 2, n_pieces))],
        compiler_params=_params(("arbitrary",)),
        name="sample_mixers",
    )(proj_rows, cos, sin, *_cache_views(win_k), *_cache_views(win_v), state, mem_k, mem_v, lb_raw, norm_g)


def _rope_tables(pos):
    half = HEAD_DIM // 2
    inv_freq = 1.0 / (ROPE_THETA ** (np.arange(half, dtype=np.float64) / half))
    ang = np.asarray(pos, np.float64)[:, None] * inv_freq[None, :]
    cos, sin = np.cos(ang), np.sin(ang)
    return (jnp.asarray(np.concatenate([cos, cos], axis=-1), F32),
            jnp.asarray(np.concatenate([-sin, sin], axis=-1), F32))


def kernel(x_prompt, x_sample, cache_win_k, cache_win_v, state_hgrn, cache_mem_k, cache_mem_v, mem_prompt,
           w_in, w_mem_kv, hgrn_lb_raw, hgrn_norm_g, w_out, ln_g, ln_b):
    bsz, seq, _ = x_prompt.shape
    dbsz, n_new, _ = x_sample.shape
    assert n_new == 1 and seq % (BAND * max(DILATIONS)) == 0
    cos_p, sin_p = _rope_tables(np.arange(seq))
    cos_s, sin_s = _rope_tables(PAST_LEN + np.arange(n_new))

    hp = x_prompt.reshape(bsz * seq, D_MODEL)
    hs = x_sample.reshape(dbsz * n_new, D_MODEL)
    mem = mem_prompt.reshape(bsz * N_MEM, D_MODEL)
    outs = [[] for _ in range(8)]
    for layer in range(DEPTH):
        lb_raw = hgrn_lb_raw
        norm_g = hgrn_norm_g[layer][None]
        lg, lbias = ln_g[layer][None], ln_b[layer][None]

        proj, proj_s = _project(hp, hs, w_in[layer], PROJ_ROWS, PROJ_FIRST_COLS, PROJ_COLS)
        za, k1, v1 = _prompt_attention(proj, cos_p, sin_p, bsz, seq)
        zh, s1 = _prompt_hgrn(proj, lb_raw, norm_g, layer, bsz, seq)
        zm, mk1, mv1 = _prompt_mem(proj, mem, w_mem_kv[layer], bsz, seq)
        hp, w_out_b = _merge(za, zh, zm, hp, w_out[layer], lg, lbias, MERGE_ROWS)

        proj_s = proj_s.transpose(1, 0, 2)
        zs, k2, v2, s2 = _sample_mixers(proj_s, cos_s, sin_s, cache_win_k, cache_win_v, state_hgrn,
                                        cache_mem_k, cache_mem_v, lb_raw, norm_g, layer)
        zs = zs.transpose(1, 0, 2)
        hs = _merge(zs[:N_HEADS_ATTN], zs[N_HEADS_ATTN:N_HEADS_ATTN + N_HEADS_HGRN],
                    zs[N_HEADS_ATTN + N_HEADS_HGRN:], hs, w_out_b, lg, lbias, dbsz)

        new = (k1.transpose(0, 2, 1, 3), v1.transpose(0, 2, 1, 3), s1, mk1, mv1,
               k2.reshape(dbsz, n_new, N_HEADS_ATTN, HEAD_DIM), v2.reshape(dbsz, n_new, N_HEADS_ATTN, HEAD_DIM),
               s2.astype(state_hgrn.dtype))
        for acc, val in zip(outs, new):
            acc.append(val)

    return (hp.reshape(bsz, seq, D_MODEL), hs.reshape(dbsz, n_new, D_MODEL), *[jnp.stack(o) for o in outs])
```

```python
import functools

import jax
import jax.numpy as jnp
import numpy as np
from jax import lax
from jax.experimental import pallas as pl
from jax.experimental.pallas import tpu as pltpu

F32 = jnp.float32
BF16 = jnp.bfloat16

D_MODEL = 2048
DEPTH = 1
PAST_LEN = 8192
HEAD_DIM = 128
N_HEADS_ATTN = 6
N_HEADS_HGRN = 6
N_HEADS_MEM = 4
W_ATTN = N_HEADS_ATTN * HEAD_DIM
W_HGRN = N_HEADS_HGRN * HEAD_DIM
W_MEM = N_HEADS_MEM * HEAD_DIM
MIX_WIDTH = W_ATTN + W_HGRN + W_MEM
WINDOWS = (128, 512, 2048)
DILATIONS = (1, 4, 16)
N_MEM = 256
ROPE_THETA = 10000.0
LN_EPS = 1e-5
RMS_EPS = 1e-6
NEG_INF = -1e30
DEEPNORM_ALPHA = (2 * DEPTH) ** 0.25
ATTN_SCALE = HEAD_DIM ** -0.5
ATTN_SCALE_LOG2 = ATTN_SCALE * 1.4426950408889634

G_QA, G_KA, G_VA, G_GA = 0, 6, 12, 18
G_QB, G_FB, G_IB, G_GB = 24, 30, 36, 42
G_QM, G_GM = 48, 52
N_GROUPS = 56

LANES = 128
BAND = 128
ATTN_GROUP = 8
HGRN_CHUNK = 64
HGRN_SUB = 16
HGRN_HEADS_PER_STEP = 3
HGRN_PREP_ROWS = 128
HGRN_SCAN_CHUNKS = 32
HGRN_SAFE_LOG_RANGE = 70.0
VMEM_LIMIT = 48 * 1024 * 1024
PROJ_ROWS, PROJ_COLS, PROJ_FIRST_COLS = 1024, 1024, 512
MERGE_ROWS = 512
MERGE_CAST_ROWS = 256


def _params(sem, vmem=VMEM_LIMIT):
    return pltpu.CompilerParams(dimension_semantics=sem, vmem_limit_bytes=vmem)


def _sigmoid(x):
    return 1.0 / (1.0 + jnp.exp(-x))


def _silu(x):
    return x * _sigmoid(x)


def _dot_nt(a, b):
    return lax.dot_general(a, b, (((1,), (1,)), ((), ())), preferred_element_type=F32)


def _store_slabs(ref, acc):
    for c in range(ref.shape[0]):
        ref[c] = acc[:, c * LANES:(c + 1) * LANES]


def _proj_first_kernel(x_ref, xs_ref, w_ref, o_ref, os_ref, wb_ref, xb_ref):
    tm = x_ref.shape[0]

    @pl.when(pl.program_id(0) == 0)
    def _():
        xb_ref[:tm, :] = x_ref[...].astype(BF16)
        xb_ref[tm:, :] = xs_ref[...].astype(BF16)

    wb_ref[...] = w_ref[...].astype(BF16)
    acc = jnp.dot(xb_ref[...], wb_ref[...], preferred_element_type=F32)
    _store_slabs(o_ref, acc[:tm])
    _store_slabs(os_ref, acc[tm:])


def _proj_rest_kernel(x_ref, w_ref, first_ref, o_hbm, xb_ref, obuf, sem_out, sem_first):
    i, j = pl.program_id(0), pl.program_id(1)
    n_col = pl.num_programs(1)
    step = i * n_col + j
    last = pl.num_programs(0) * n_col - 1
    slot = step % 2
    tm = x_ref.shape[0]
    n_slab = obuf.shape[1]

    def tile_copy(stp):
        row_tile, col_tile = stp // n_col + 1, stp % n_col
        return pltpu.make_async_copy(
            obuf.at[stp % 2], o_hbm.at[pl.ds(col_tile * n_slab, n_slab), pl.ds(row_tile * tm, tm), :],
            sem_out.at[stp % 2])

    first_copy = pltpu.make_async_copy(first_ref, o_hbm.at[pl.ds(j * n_slab, n_slab), pl.ds(0, tm), :], sem_first.at[0])

    @pl.when(i == 0)
    def _():
        first_copy.start()

    @pl.when(j == 0)
    def _():
        xb_ref[...] = x_ref[...].astype(BF16)

    @pl.when(step >= 2)
    def _():
        tile_copy(step - 2).wait()

    _store_slabs(obuf.at[slot], jnp.dot(xb_ref[...], w_ref[...], preferred_element_type=F32))
    tile_copy(step).start()

    @pl.when(i == 0)
    def _():
        first_copy.wait()

    @pl.when(step == last)
    def _():
        tile_copy(step - 1).wait()
        tile_copy(step).wait()


def _project(x, xs, w, tm, tn_first, tn):
    m, k = x.shape
    ms = xs.shape[0]
    n = w.shape[1]
    assert (m // tm - 1) * (n // tn) >= 2
    proj_first, proj_s, w_bf16 = pl.pallas_call(
        _proj_first_kernel,
        grid=(n // tn_first,),
        in_specs=[pl.BlockSpec((tm, k), lambda j: (0, 0)),
                  pl.BlockSpec((ms, k), lambda j: (0, 0)),
                  pl.BlockSpec((k, tn_first), lambda j: (0, j))],
        out_specs=[pl.BlockSpec((tn_first // LANES, tm, LANES), lambda j: (j, 0, 0)),
                   pl.BlockSpec((tn_first // LANES, ms, LANES), lambda j: (j, 0, 0)),
                   pl.BlockSpec((k, tn_first), lambda j: (0, j))],
        out_shape=[jax.ShapeDtypeStruct((n // LANES, tm, LANES), F32),
                   jax.ShapeDtypeStruct((n // LANES, ms, LANES), F32),
                   jax.ShapeDtypeStruct((k, n), BF16)],
        scratch_shapes=[pltpu.VMEM((tm + ms, k), BF16)],
        compiler_params=_params(("arbitrary",)),
        name="in_proj_first",
    )(x, xs, w)
    proj = pl.pallas_call(
        _proj_rest_kernel,
        grid=(m // tm - 1, n // tn),
        in_specs=[pl.BlockSpec((tm, k), lambda i, j: (i + 1, 0)),
                  pl.BlockSpec((k, tn), lambda i, j: (0, j)),
                  pl.BlockSpec((tn // LANES, tm, LANES), lambda i, j: (jnp.where(i == 0, j, n // tn - 1), 0, 0))],
        out_specs=pl.BlockSpec(memory_space=pl.ANY),
        out_shape=jax.ShapeDtypeStruct((n // LANES, m, LANES), F32),
        scratch_shapes=[pltpu.VMEM((tm, k), BF16), pltpu.VMEM((2, tn // LANES, tm, LANES), F32),
                        pltpu.SemaphoreType.DMA((2,)), pltpu.SemaphoreType.DMA((1,))],
        compiler_params=_params(("arbitrary", "arbitrary")),
        name="in_proj",
    )(x, w_bf16, proj_first)
    return proj, proj_s


def _rope(x, cos, sin_signed):
    return x * cos + pltpu.roll(x, HEAD_DIM // 2, 1) * sin_signed


def _bias_from_count(count):
    return jnp.where(count > 1.5, 1.0, jnp.where(count > 0.5, 0.0, NEG_INF)).astype(F32)


def _block_deltas(block_gap):
    qi = lax.broadcasted_iota(jnp.int32, (BAND, BAND), 0)
    ki = lax.broadcasted_iota(jnp.int32, (BAND, BAND), 1)
    return block_gap * BAND + qi - ki


def _near_bias(block_gap):
    d = _block_deltas(block_gap)
    return _bias_from_count(((d >= 0) & (d <= BAND)).astype(F32))


def _class_bias(block_gap, far_step):
    d = _block_deltas(block_gap)
    near = (d >= 0) & (d <= BAND)
    far = (d >= 0) & ((d & (far_step - 1)) == 0)
    return _bias_from_count(near.astype(F32) + far.astype(F32))


def _softmax_stage(scores, floors=None):
    out = []
    for idx, s in enumerate(scores):
        m = jnp.broadcast_to(jnp.max(s, axis=-1, keepdims=True), (BAND, HEAD_DIM))
        if floors is not None:
            m = jnp.maximum(m, floors[idx])
        m_wide = jnp.concatenate([m] * (s.shape[1] // HEAD_DIM), axis=1)
        out.append((m, jnp.exp2(s - m_wide).astype(BF16)))
    return out


def _attn_kernel(q_ref, k_ref, v_ref, g_ref, cos_ref, sin_ref, z_ref, ko_ref, vo_ref,
                 qs_ref, qb_ref, kb_ref, vb_ref, qc_ref, kc_ref, vc_ref, acc_ref, m_ref, l_ref):
    seq = q_ref.shape[0]
    d_mid, d_far = DILATIONS[1], DILATIONS[2]
    far_step = d_far // d_mid
    cls_rows = seq // d_mid
    cos = cos_ref[...]
    sin = sin_ref[...]
    qs_ref[...] = _rope(q_ref[...], cos, sin) * ATTN_SCALE_LOG2
    ko_ref[...] = _rope(k_ref[...], cos, sin)
    vo_ref[...] = v_ref[...]
    qb_ref[...] = qs_ref[...].astype(BF16)
    kb_ref[...] = ko_ref[...].astype(BF16)
    ones = jnp.ones((seq, HEAD_DIM), BF16)
    vb_ref[:, :HEAD_DIM] = v_ref[...].astype(BF16)
    vb_ref[:, HEAD_DIM:] = ones
    vc_ref[:, HEAD_DIM:] = ones
    for r in range(d_mid):
        cls, dst = pl.ds(r, cls_rows, stride=d_mid), pl.ds(r * cls_rows, cls_rows)
        qc_ref[dst, :] = qs_ref[cls, :].astype(BF16)
        kc_ref[dst, :] = ko_ref[cls, :].astype(BF16)
        vc_ref[dst, :HEAD_DIM] = v_ref[cls, :].astype(BF16)

    n_cls_blk = cls_rows // BAND
    cls_bias = [_class_bias(gap, far_step) for gap in range(min(n_cls_blk, 3))]
    items = [(r, n) for r in range(d_mid) for n in range(n_cls_blk)]
    for g0 in range(0, len(items), ATTN_GROUP):
        group = items[g0:g0 + ATTN_GROUP]
        keys = [pl.ds(r * cls_rows, (n + 1) * BAND) for r, n in group]
        scores = [_dot_nt(qc_ref[pl.ds(r * cls_rows + n * BAND, BAND), :], kc_ref[kr, :])
                  + jnp.concatenate([cls_bias[min(n - nk, 2)] for nk in range(n + 1)], axis=1)
                  for (r, n), kr in zip(group, keys)]
        for (r, n), kr, (m, p) in zip(group, keys, _softmax_stage(scores)):
            rows = pl.ds(r + d_mid * BAND * n, BAND, stride=d_mid)
            pv = jnp.dot(p, vc_ref[kr, :], preferred_element_type=F32)
            acc_ref[rows, :] = pv[:, :HEAD_DIM]
            l_ref[rows, :] = pv[:, HEAD_DIM:]
            m_ref[rows, :] = m

    near_bias = [_near_bias(0), jnp.concatenate([_near_bias(1), _near_bias(0)], axis=1)]
    items = list(range(seq // BAND))
    for g0 in range(0, len(items), ATTN_GROUP):
        group = items[g0:g0 + ATTN_GROUP]
        rows = [pl.ds(n * BAND, BAND) for n in group]
        keys = [pl.ds(max(n - 1, 0) * BAND, BAND * min(n + 1, 2)) for n in group]
        scores = [_dot_nt(qb_ref[rw, :], kb_ref[kr, :]) + near_bias[min(n, 1)]
                  for n, rw, kr in zip(group, rows, keys)]
        probs = _softmax_stage(scores, floors=[m_ref[rw, :] for rw in rows])
        for rw, kr, (m, p) in zip(rows, keys, probs):
            w = jnp.exp2(m_ref[rw, :] - m)
            pv = jnp.dot(p, vb_ref[kr, :], preferred_element_type=F32)
            num = pv[:, :HEAD_DIM] + w * acc_ref[rw, :]
            den = pv[:, HEAD_DIM:] + w * l_ref[rw, :]
            z_ref[rw, :] = ((num / den) * _silu(g_ref[rw, :])).astype(z_ref.dtype)


def _prompt_attention(proj, cos, sin, bsz, seq):
    def col(g0):
        return pl.BlockSpec((None, seq, HEAD_DIM), lambda b, h: (g0 + h, b, 0))

    table = pl.BlockSpec((seq, HEAD_DIM), lambda b, h: (0, 0))
    kv_out = pl.BlockSpec((None, None, seq, HEAD_DIM), lambda b, h: (b, h, 0, 0))
    d_near, d_mid, d_far = DILATIONS
    far_step = d_far // d_mid
    assert d_near == 1 and d_far % d_mid == 0 and far_step & (far_step - 1) == 0
    assert all(w // d == BAND for w, d in zip(WINDOWS, DILATIONS)) and seq % (d_mid * BAND) == 0
    assert seq // d_mid <= far_step * BAND
    return pl.pallas_call(
        _attn_kernel,
        grid=(bsz, N_HEADS_ATTN),
        in_specs=[col(G_QA), col(G_KA), col(G_VA), col(G_GA), table, table],
        out_specs=[pl.BlockSpec((None, seq, HEAD_DIM), lambda b, h: (h, b, 0)), kv_out, kv_out],
        out_shape=[jax.ShapeDtypeStruct((N_HEADS_ATTN, bsz * seq, HEAD_DIM), BF16),
                   jax.ShapeDtypeStruct((bsz, N_HEADS_ATTN, seq, HEAD_DIM), F32),
                   jax.ShapeDtypeStruct((bsz, N_HEADS_ATTN, seq, HEAD_DIM), F32)],
        scratch_shapes=[pltpu.VMEM((seq, HEAD_DIM), F32)]
                       + [pltpu.VMEM((seq, HEAD_DIM), BF16), pltpu.VMEM((seq, HEAD_DIM), BF16),
                          pltpu.VMEM((seq, 2 * HEAD_DIM), BF16)] * 2
                       + [pltpu.VMEM((seq, HEAD_DIM), F32)] * 3,
        compiler_params=_params(("parallel", "parallel")),
        name="prompt_attn",
    )(proj, proj, proj, proj, cos, sin)


def _lower_bound(lb_raw, layer):
    e = jnp.exp(lb_raw - jnp.max(lb_raw, axis=0, keepdims=True))
    sm = e / jnp.sum(e, axis=0, keepdims=True)
    return jnp.sum(sm[:layer + 1], axis=0, keepdims=True)


def _split2(x):
    hi = x.astype(BF16)
    return hi, (x - hi.astype(F32)).astype(BF16)


def _rms_gate(o, norm_g, gate):
    o = o * lax.rsqrt(jnp.mean(o * o, axis=-1, keepdims=True) + RMS_EPS)
    return o * norm_g * _silu(gate)


def _hgrn_gates(fb, lb, tril):
    f = lb + (1.0 - lb) * _sigmoid(fb)
    g = jnp.log(f)
    b = sum(jnp.dot(tril, piece, preferred_element_type=F32) for piece in _split2(g))
    return 1.0 - f, b


def _hgrn_fast_prepare(q_ref, f_ref, lb, qh_ref, kh_ref, el_ref):
    n_heads, seq, _ = q_ref.shape
    blk = HGRN_PREP_ROWS
    tril = _tril_ones(blk)

    def body(bi, carry):
        r0 = pl.multiple_of(bi * blk, blk)
        fb = jnp.concatenate([f_ref[h, pl.ds(r0, blk), :] for h in range(n_heads)], axis=1)
        f = lb + (1.0 - lb) * _sigmoid(fb)
        b_blk = sum(jnp.dot(tril, piece, preferred_element_type=F32) for piece in _split2(jnp.log(f)))
        kk = 1.0 - f
        for c0 in range(0, blk, HGRN_CHUNK):
            rs = slice(c0, c0 + HGRN_CHUNK)
            b = b_blk[rs] - b_blk[c0 - 1:c0] if c0 else b_blk[rs]
            rows = pl.ds(r0 + c0, HGRN_CHUNK)
            q = jnp.concatenate([q_ref[h, rows, :] for h in range(n_heads)], axis=1)
            qh_ref[rows, :] = (q * jnp.exp(b)).astype(BF16)
            kh_ref[rows, :] = (kk[rs] * jnp.exp(-b)).astype(BF16)
            el_ref[pl.ds(bi * (blk // HGRN_CHUNK) + c0 // HGRN_CHUNK, 1), :] = jnp.exp(b[HGRN_CHUNK - 1:])
        return carry

    lax.fori_loop(0, seq // blk, body, 0, unroll=4)


def _hgrn_fast_scan(i_ref, g_ref, norm_g, qh_ref, kh_ref, el_ref, st_ref, z_ref):
    n_heads, seq, _ = i_ref.shape
    c = HGRN_CHUNK
    heads = range(n_heads)
    chunks = range(HGRN_SCAN_CHUNKS)
    lanes = [slice(h * HEAD_DIM, (h + 1) * HEAD_DIM) for h in heads]
    causal = lax.broadcasted_iota(jnp.int32, (c, c), 0) >= lax.broadcasted_iota(jnp.int32, (c, c), 1)
    tn = (((0,), (0,)), ((), ()))

    def body(ti, carry):
        rows = [pl.ds(pl.multiple_of((ti * HGRN_SCAN_CHUNKS + k) * c, c), c) for k in chunks]
        qh = [[qh_ref[rows[k], lanes[h]] for h in heads] for k in chunks]
        kh = [[kh_ref[rows[k], lanes[h]] for h in heads] for k in chunks]
        vb = [[i_ref[h, rows[k], :].astype(BF16) for h in heads] for k in chunks]
        att = [[jnp.where(causal, _dot_nt(qh[k][h], kh[k][h]), 0.0).astype(BF16) for h in heads] for k in chunks]
        ds = [[lax.dot_general(vb[k][h], kh[k][h], tn, preferred_element_type=F32) for h in heads] for k in chunks]
        st = [st_ref[h] for h in heads]
        o = []
        for k in chunks:
            el = el_ref[pl.ds(ti * HGRN_SCAN_CHUNKS + k, 1), :]
            o.append([_dot_nt(qh[k][h], st[h].astype(BF16)) for h in heads])
            st = [(st[h] + ds[k][h]) * el[:, lanes[h]] for h in heads]
        for h in heads:
            st_ref[h] = st[h]
        intra = [[jnp.dot(att[k][h], vb[k][h], preferred_element_type=F32) for h in heads] for k in chunks]
        for k in chunks:
            for h in heads:
                z_ref[h, rows[k], :] = _rms_gate(o[k][h] + intra[k][h], norm_g[:, lanes[h]],
                                                 g_ref[h, rows[k], :]).astype(z_ref.dtype)
        return carry

    lax.fori_loop(0, seq // (c * HGRN_SCAN_CHUNKS), body, 0)


def _hgrn_chunk(q, fb, v, lb, st, tril):
    c = q.shape[0]
    kk, b = _hgrn_gates(fb, lb, tril)
    o = _dot_nt((q * jnp.exp(b)).astype(BF16), st.astype(BF16))

    s_idx = lax.broadcasted_iota(jnp.int32, (c, 1), 0)
    lane = lax.broadcasted_iota(jnp.int32, (HGRN_SUB, c), 1)
    row = lax.broadcasted_iota(jnp.int32, (HGRN_SUB, c), 0)
    att_rows = []
    for i0 in range(0, c, HGRN_SUB):
        qi = q[i0:i0 + HGRN_SUB]
        bi = b[i0:i0 + HGRN_SUB]
        if i0 > 0:
            bref = b[i0 - 1:i0]
            kt = jnp.where(s_idx < i0, kk * jnp.exp(jnp.minimum(bref - b, 0.0)), 0.0)
            att = _dot_nt((qi * jnp.exp(bi - bref)).astype(BF16), kt.astype(BF16))
        else:
            att = jnp.zeros((HGRN_SUB, c), F32)
        for j in range(HGRN_SUB):
            s = i0 + j
            e = jnp.exp(jnp.minimum(bi - b[s:s + 1], 0.0))
            colv = jnp.sum(qi * kk[s:s + 1] * e, axis=-1, keepdims=True)
            att = jnp.where((lane == s) & (row >= j), colv, att)
        att_rows.append(att)
    att = jnp.concatenate(att_rows, axis=0)
    vb = v.astype(BF16)
    o = o + jnp.dot(att.astype(BF16), vb, preferred_element_type=F32)

    b_last = b[c - 1:c]
    kd = (kk * jnp.exp(b_last - b)).astype(BF16)
    st_new = st * jnp.exp(b_last) + lax.dot_general(vb, kd, (((0,), (0,)), ((), ())),
                                                    preferred_element_type=F32)
    return o, st_new


def _tril_ones(c):
    return (lax.broadcasted_iota(jnp.int32, (c, c), 0) >= lax.broadcasted_iota(jnp.int32, (c, c), 1)
            ).astype(BF16)


def _hgrn_kernel(layer, q_ref, f_ref, i_ref, g_ref, lb_ref, ng_ref, z_ref, s_ref, st_ref, qh_ref, kh_ref, el_ref):
    n_heads, seq, _ = q_ref.shape
    lb = _lower_bound(lb_ref[...], layer)
    norm_g = ng_ref[...]
    st_ref[...] = jnp.zeros_like(st_ref)

    def fast():
        _hgrn_fast_prepare(q_ref, f_ref, lb, qh_ref, kh_ref, el_ref)
        _hgrn_fast_scan(i_ref, g_ref, norm_g, qh_ref, kh_ref, el_ref, st_ref, z_ref)

    def safe():
        tril = _tril_ones(HGRN_CHUNK)

        def body(ci, carry):
            rows = pl.ds(pl.multiple_of(ci * HGRN_CHUNK, HGRN_CHUNK), HGRN_CHUNK)
            for h in range(n_heads):
                lanes = slice(h * HEAD_DIM, (h + 1) * HEAD_DIM)
                o, st = _hgrn_chunk(q_ref[h, rows, :], f_ref[h, rows, :], i_ref[h, rows, :], lb[:, lanes],
                                    st_ref[h], tril)
                st_ref[h] = st
                z_ref[h, rows, :] = _rms_gate(o, norm_g[:, lanes], g_ref[h, rows, :]).astype(z_ref.dtype)
            return carry

        lax.fori_loop(0, seq // HGRN_CHUNK, body, 0)

    fast_ok = HGRN_CHUNK * -jnp.log(jnp.min(lb)) <= HGRN_SAFE_LOG_RANGE
    lax.cond(fast_ok, fast, safe)
    for h in range(n_heads):
        s_ref[h] = st_ref[h].T


def _prompt_hgrn(proj, lb_raw, norm_g, layer, bsz, seq):
    hps = HGRN_HEADS_PER_STEP
    assert N_HEADS_HGRN % hps == 0 and all(g % hps == 0 for g in (G_QB, G_FB, G_IB, G_GB))
    assert seq % (HGRN_CHUNK * HGRN_SCAN_CHUNKS) == 0 and seq % HGRN_PREP_ROWS == 0 and HGRN_PREP_ROWS % HGRN_CHUNK == 0

    def cols(g0):
        return pl.BlockSpec((hps, seq, HEAD_DIM), lambda b, h: (g0 // hps + h, b, 0))

    return pl.pallas_call(
        functools.partial(_hgrn_kernel, layer),
        grid=(bsz, N_HEADS_HGRN // hps),
        in_specs=[cols(G_QB), cols(G_FB), cols(G_IB), cols(G_GB),
                  pl.BlockSpec((DEPTH + 1, hps * HEAD_DIM), lambda b, h: (0, h)),
                  pl.BlockSpec((1, hps * HEAD_DIM), lambda b, h: (0, h))],
        out_specs=[pl.BlockSpec((hps, seq, HEAD_DIM), lambda b, h: (h, b, 0)),
                   pl.BlockSpec((None, hps, HEAD_DIM, HEAD_DIM), lambda b, h: (b, h, 0, 0))],
        out_shape=[jax.ShapeDtypeStruct((N_HEADS_HGRN, bsz * seq, HEAD_DIM), BF16),
                   jax.ShapeDtypeStruct((bsz, N_HEADS_HGRN, HEAD_DIM, HEAD_DIM), F32)],
        scratch_shapes=[pltpu.VMEM((hps, HEAD_DIM, HEAD_DIM), F32),
                        pltpu.VMEM((seq, hps * HEAD_DIM), BF16),
                        pltpu.VMEM((seq, hps * HEAD_DIM), BF16),
                        pltpu.VMEM((seq // HGRN_CHUNK, hps * HEAD_DIM), F32)],
        compiler_params=_params(("parallel", "parallel")),
        name="prompt_hgrn",
    )(proj, proj, proj, proj, lb_raw, norm_g)


MEM_ROWS = 256
MEM_GROUP = 4


def _mem_kernel(q_ref, g_ref, mem_ref, w_ref, z_ref, mk_ref, mv_ref):
    seq = q_ref.shape[1]
    kv = jnp.dot(mem_ref[...].astype(BF16), w_ref[...].astype(BF16), preferred_element_type=F32)
    ones = jnp.ones((N_MEM, HEAD_DIM), BF16)
    mk_b, mv_b = [], []
    for h in range(N_HEADS_MEM):
        mk = kv[:, h * HEAD_DIM:(h + 1) * HEAD_DIM]
        mv = kv[:, W_MEM + h * HEAD_DIM:W_MEM + (h + 1) * HEAD_DIM]
        mk_ref[:, h, :] = mk
        mv_ref[:, h, :] = mv
        mk_b.append(mk.astype(BF16))
        mv_b.append(jnp.concatenate([mv.astype(BF16), ones], axis=1))

    items = [(h, pl.ds(n * MEM_ROWS, MEM_ROWS)) for h in range(N_HEADS_MEM) for n in range(seq // MEM_ROWS)]
    for g0 in range(0, len(items), MEM_GROUP):
        group = items[g0:g0 + MEM_GROUP]
        scores = [_dot_nt((q_ref[h, rows, :] * ATTN_SCALE_LOG2).astype(BF16), mk_b[h]) for h, rows in group]
        probs = [jnp.exp2(s - jnp.max(s, axis=-1, keepdims=True)).astype(BF16) for s in scores]
        for (h, rows), p in zip(group, probs):
            pv = jnp.dot(p, mv_b[h], preferred_element_type=F32)
            o = pv[:, :HEAD_DIM] / pv[:, HEAD_DIM:]
            z_ref[h, rows, :] = (o * _silu(g_ref[h, rows, :])).astype(z_ref.dtype)


def _prompt_mem(proj, mem, w_kv, bsz, seq):
    assert G_QM % N_HEADS_MEM == 0 and G_GM % N_HEADS_MEM == 0

    def cols(g0):
        return pl.BlockSpec((N_HEADS_MEM, seq, HEAD_DIM), lambda b: (g0 // N_HEADS_MEM, b, 0))

    kv_out = pl.BlockSpec((None, N_MEM, N_HEADS_MEM, HEAD_DIM), lambda b: (b, 0, 0, 0))
    kv_shape = jax.ShapeDtypeStruct((bsz, N_MEM, N_HEADS_MEM, HEAD_DIM), F32)
    return pl.pallas_call(
        _mem_kernel,
        grid=(bsz,),
        in_specs=[cols(G_QM), cols(G_GM), pl.BlockSpec((N_MEM, D_MODEL), lambda b: (b, 0)),
                  pl.BlockSpec((D_MODEL, 2 * W_MEM), lambda b: (0, 0))],
        out_specs=[pl.BlockSpec((N_HEADS_MEM, seq, HEAD_DIM), lambda b: (0, b, 0)), kv_out, kv_out],
        out_shape=[jax.ShapeDtypeStruct((N_HEADS_MEM, bsz * seq, HEAD_DIM), BF16), kv_shape, kv_shape],
        compiler_params=_params(("parallel",)),
        name="prompt_mem",
    )(proj, proj, mem, w_kv)


def _merge_kernel(cast_w, za_ref, zh_ref, zm_ref, x_ref, w_ref, lg_ref, lb_ref, o_ref, *rest):
    if cast_w:
        wb_ref, z_ref, stage, sem = rest
        n_rows = stage.shape[1]
        n_chunks = w_ref.shape[0] // n_rows

        def chunk_copy(c):
            return pltpu.make_async_copy(w_ref.at[pl.ds(c * n_rows, n_rows), :], stage.at[c % 2], sem.at[c % 2])

        @pl.when(pl.program_id(0) == 0)
        def _():
            chunk_copy(0).start()
            for c in range(n_chunks):
                if c + 1 < n_chunks:
                    chunk_copy(c + 1).start()
                chunk_copy(c).wait()
                wb_ref[c * n_rows:(c + 1) * n_rows, :] = stage[c % 2].astype(BF16)
    else:
        wb_ref, (z_ref,) = w_ref, rest

    c0 = 0
    for ref in (za_ref, zh_ref, zm_ref):
        for c in range(ref.shape[0]):
            z_ref[:, (c0 + c) * LANES:(c0 + c + 1) * LANES] = ref[c].astype(BF16)
        c0 += ref.shape[0]
    y = jnp.dot(z_ref[...], wb_ref[...], preferred_element_type=F32)
    r = DEEPNORM_ALPHA * x_ref[...] + y
    mu = jnp.mean(r, axis=-1, keepdims=True)
    d = r - mu
    var = jnp.mean(d * d, axis=-1, keepdims=True)
    o_ref[...] = d * lax.rsqrt(var + LN_EPS) * lg_ref[...] + lb_ref[...]


def _merge(za, zh, zm, x, w_out, ln_g, ln_b, tm):
    m = x.shape[0]
    cast_w = w_out.dtype != BF16

    def slab(a):
        return pl.BlockSpec((a.shape[0], tm, LANES), lambda i: (0, i, 0))

    const = lambda shape: pl.BlockSpec(shape, lambda i: (0, 0))
    y_spec, y_shape = pl.BlockSpec((tm, D_MODEL), lambda i: (i, 0)), jax.ShapeDtypeStruct((m, D_MODEL), F32)
    w_block = const((MIX_WIDTH, D_MODEL))
    z_scratch = pltpu.VMEM((tm, MIX_WIDTH), BF16)
    if cast_w:
        w_spec = pl.BlockSpec(memory_space=pl.ANY)
        out_specs, out_shape = [y_spec, w_block], [y_shape, jax.ShapeDtypeStruct((MIX_WIDTH, D_MODEL), BF16)]
        scratch = [z_scratch, pltpu.VMEM((2, MERGE_CAST_ROWS, D_MODEL), F32), pltpu.SemaphoreType.DMA((2,))]
    else:
        w_spec, out_specs, out_shape, scratch = w_block, y_spec, y_shape, [z_scratch]
    return pl.pallas_call(
        functools.partial(_merge_kernel, cast_w),
        grid=(m // tm,),
        in_specs=[slab(za), slab(zh), slab(zm), pl.BlockSpec((tm, D_MODEL), lambda i: (i, 0)),
                  w_spec, const((1, D_MODEL)), const((1, D_MODEL))],
        out_specs=out_specs,
        out_shape=out_shape,
        scratch_shapes=scratch,
        compiler_params=_params(("arbitrary",)),
        name="merge",
    )(za, zh, zm, x, w_out, ln_g, ln_b)


def _column(row):
    return jnp.broadcast_to(row, (HEAD_DIM, HEAD_DIM)).T


HBM_TILE_ROWS = 8
SAMPLE_ROWS = 4


def _window_pieces(past):
    pieces = []
    for dil in DILATIONS:
        if dil == 1:
            pieces.append((0, past - BAND, BAND, None))
        elif dil < HBM_TILE_ROWS:
            n = BAND * dil // HBM_TILE_ROWS
            pieces += [(1, past // HBM_TILE_ROWS - n, n, r) for r in range(0, HBM_TILE_ROWS, dil)]
        else:
            pieces.append((2, past // dil - BAND, BAND, 0))
    return pieces


def _cache_views(cache):
    depth, bsz, past, nh, hd = cache.shape
    assert all(d == 1 or HBM_TILE_ROWS % d == 0 or d % HBM_TILE_ROWS == 0 for d in DILATIONS)
    big = max(DILATIONS)
    hm = cache.transpose(0, 1, 3, 2, 4)
    return (hm, hm.reshape(depth, bsz, nh, past // HBM_TILE_ROWS, HBM_TILE_ROWS, hd),
            hm.reshape(depth, bsz, nh, past // big, big, hd))


def _sample_row(layer, p_ref, cos, sin, k_win, v_win, st_ref, mk_ref, mv_ref, lb_all, ng_ref, z_ref, ko_ref, vo_ref,
                so_ref):
    q_all = _rope(p_ref[G_QA:G_QA + N_HEADS_ATTN, :], cos, sin) * ATTN_SCALE
    k_all = _rope(p_ref[G_KA:G_KA + N_HEADS_ATTN, :], cos, sin)
    v_all = p_ref[G_VA:G_VA + N_HEADS_ATTN, :]
    ko_ref[...] = k_all
    vo_ref[...] = v_all

    for h in range(N_HEADS_ATTN):
        q = q_all[h:h + 1]
        s_new = jnp.sum(q * k_all[h:h + 1], axis=-1, keepdims=True)
        s = jnp.sum(k_win[h] * q, axis=-1, keepdims=True)
        m = jnp.maximum(jnp.max(s, axis=0, keepdims=True), s_new)
        p = jnp.exp(s - m)
        p_new = jnp.exp(s_new - m) * len(DILATIONS)
        den = jnp.sum(p, axis=0, keepdims=True) + p_new
        num = jnp.sum(p * v_win[h], axis=0, keepdims=True) + p_new * v_all[h:h + 1]
        z_ref[h:h + 1, :] = (num / den) * _silu(p_ref[G_GA + h:G_GA + h + 1, :])

    for h in range(N_HEADS_HGRN):
        lanes = slice(h * HEAD_DIM, (h + 1) * HEAD_DIM)
        lb = _lower_bound(lb_all[:, lanes], layer)
        f = lb + (1.0 - lb) * _sigmoid(p_ref[G_FB + h:G_FB + h + 1, :])
        f_col = _column(f)
        q_col = _column(p_ref[G_QB + h:G_QB + h + 1, :])
        s_new = f_col * st_ref[h] + (1.0 - f_col) * p_ref[G_IB + h:G_IB + h + 1, :]
        so_ref[h] = s_new
        o = jnp.sum(s_new * q_col, axis=0, keepdims=True)
        z_ref[N_HEADS_ATTN + h:N_HEADS_ATTN + h + 1, :] = _rms_gate(
            o, ng_ref[:, lanes], p_ref[G_GB + h:G_GB + h + 1, :])

    fold = mk_ref.shape[1] // N_HEADS_MEM
    unfold = lambda a: sum(a[i * N_HEADS_MEM:(i + 1) * N_HEADS_MEM] for i in range(fold))
    q_m = p_ref[G_QM:G_QM + N_HEADS_MEM, :] * ATTN_SCALE
    s = jnp.sum(mk_ref[...] * jnp.concatenate([q_m] * fold, axis=0)[None], axis=-1, keepdims=True)
    m = jnp.max(s, axis=0)
    m = functools.reduce(jnp.maximum, [m[i * N_HEADS_MEM:(i + 1) * N_HEADS_MEM] for i in range(fold)])
    p = jnp.exp(s - jnp.concatenate([m] * fold, axis=0)[None])
    o = unfold(jnp.sum(p * mv_ref[...], axis=0)) / unfold(jnp.sum(p, axis=0))
    row = N_HEADS_ATTN + N_HEADS_HGRN
    z_ref[row:row + N_HEADS_MEM, :] = o * _silu(p_ref[G_GM:G_GM + N_HEADS_MEM, :])


def _sample_kernel(layer, past, p_ref, cos_ref, sin_ref, k0_hbm, k1_hbm, k2_hbm, v0_hbm, v1_hbm, v2_hbm,
                   st_ref, mk_ref, mv_ref, lb_ref, ng_ref, z_ref, ko_ref, vo_ref, so_ref, kbuf, vbuf, sem):
    step = pl.program_id(0)
    slot = step % 2
    n_rows = p_ref.shape[0]
    pieces = _window_pieces(past)

    def window_copies(stp, sl):
        out = []
        for rr in range(n_rows):
            for ci, (views, buf) in enumerate((((k0_hbm, k1_hbm, k2_hbm), kbuf), ((v0_hbm, v1_hbm, v2_hbm), vbuf))):
                off = 0
                for pi, (vi, start, count, res) in enumerate(pieces):
                    view, row = views[vi], stp * n_rows + rr
                    src = (view.at[layer, row, :, pl.ds(start, count), :] if res is None
                           else view.at[layer, row, :, pl.ds(start, count), res, :])
                    out.append(pltpu.make_async_copy(src, buf.at[sl, rr, :, pl.ds(off, count), :],
                                                     sem.at[sl, rr, ci, pi]))
                    off += count
        return out

    @pl.when(step == 0)
    def _():
        for c in window_copies(step, slot):
            c.start()

    @pl.when(step + 1 < pl.num_programs(0))
    def _():
        for c in window_copies(step + 1, 1 - slot):
            c.start()

    for c in window_copies(step, slot):
        c.wait()

    cos, sin, lb_all = cos_ref[...], sin_ref[...], lb_ref[...]
    for rr in range(n_rows):
        _sample_row(layer, p_ref.at[rr], cos, sin, kbuf.at[slot, rr], vbuf.at[slot, rr], st_ref.at[rr], mk_ref.at[rr],
                    mv_ref.at[rr], lb_all, ng_ref, z_ref.at[rr], ko_ref.at[rr], vo_ref.at[rr], so_ref.at[rr])


def _sample_mixers(proj_rows, cos, sin, win_k, win_v, state, mem_k, mem_v, lb_raw, norm_g, layer):
    bsz, past = win_k.shape[1:3]
    rps = SAMPLE_ROWS
    assert all(past % d == 0 and past >= w for w, d in zip(WINDOWS, DILATIONS)) and bsz % rps == 0
    assert all(w // d == BAND for w, d in zip(WINDOWS, DILATIONS)) and past % HBM_TILE_ROWS == 0
    n_keys = BAND * len(DILATIONS)
    n_pieces = len(_window_pieces(past))
    any_spec = pl.BlockSpec(memory_space=pl.ANY)
    vec = lambda n: pl.BlockSpec((n, W_HGRN), lambda b: (0, 0))
    heads = lambda n: pl.BlockSpec((rps, n, HEAD_DIM), lambda b: (b, 0, 0))
    fold = HBM_TILE_ROWS // N_HEADS_MEM
    mem_k, mem_v = (a.reshape(a.shape[0], bsz, N_MEM // fold, fold * N_HEADS_MEM, HEAD_DIM) for a in (mem_k, mem_v))
    mem_spec = pl.BlockSpec((None, rps, N_MEM // fold, fold * N_HEADS_MEM, HEAD_DIM), lambda b: (layer, b, 0, 0, 0))
    return pl.pallas_call(
        functools.partial(_sample_kernel, layer, past),
        grid=(bsz // rps,),
        in_specs=[heads(N_GROUPS), pl.BlockSpec((1, HEAD_DIM), lambda b: (0, 0)),
                  pl.BlockSpec((1, HEAD_DIM), lambda b: (0, 0)), *([any_spec] * 6),
                  pl.BlockSpec((None, rps, N_HEADS_HGRN, HEAD_DIM, HEAD_DIM), lambda b: (layer, b, 0, 0, 0)),
                  mem_spec, mem_spec, vec(DEPTH + 1), vec(1)],
        out_specs=[heads(MIX_WIDTH // HEAD_DIM), heads(N_HEADS_ATTN), heads(N_HEADS_ATTN),
                   pl.BlockSpec((rps, N_HEADS_HGRN, HEAD_DIM, HEAD_DIM), lambda b: (b, 0, 0, 0))],
        out_shape=[jax.ShapeDtypeStruct((bsz, MIX_WIDTH // HEAD_DIM, HEAD_DIM), F32),
                   jax.ShapeDtypeStruct((bsz, N_HEADS_ATTN, HEAD_DIM), F32),
                   jax.ShapeDtypeStruct((bsz, N_HEADS_ATTN, HEAD_DIM), F32),
                   jax.ShapeDtypeStruct((bsz, N_HEADS_HGRN, HEAD_DIM, HEAD_DIM), F32)],
        scratch_shapes=[pltpu.VMEM((2, rps, N_HEADS_ATTN, n_keys, HEAD_DIM), F32),
                        pltpu.VMEM((2, rps, N_HEADS_ATTN, n_keys, HEAD_DIM), F32),
                        pltpu.SemaphoreType.DMA((2, rps, 2, n_pieces))],
        compiler_params=_params(("arbitrary",)),
        name="sample_mixers",
    )(proj_rows, cos, sin, *_cache_views(win_k), *_cache_views(win_v), state, mem_k, mem_v, lb_raw, norm_g)


def _rope_tables(pos):
    half = HEAD_DIM // 2
    inv_freq = 1.0 / (ROPE_THETA ** (np.arange(half, dtype=np.float64) / half))
    ang = np.asarray(pos, np.float64)[:, None] * inv_freq[None, :]
    cos, sin = np.cos(ang), np.sin(ang)
    return (jnp.asarray(np.concatenate([cos, cos], axis=-1), F32),
            jnp.asarray(np.concatenate([-sin, sin], axis=-1), F32))


def kernel(x_prompt, x_sample, cache_win_k, cache_win_v, state_hgrn, cache_mem_k, cache_mem_v, mem_prompt,
           w_in, w_mem_kv, hgrn_lb_raw, hgrn_norm_g, w_out, ln_g, ln_b):
    bsz, seq, _ = x_prompt.shape
    dbsz, n_new, _ = x_sample.shape
    assert n_new == 1 and seq % (BAND * max(DILATIONS)) == 0
    cos_p, sin_p = _rope_tables(np.arange(seq))
    cos_s, sin_s = _rope_tables(PAST_LEN + np.arange(n_new))

    hp = x_prompt.reshape(bsz * seq, D_MODEL)
    hs = x_sample.reshape(dbsz * n_new, D_MODEL)
    mem = mem_prompt.reshape(bsz * N_MEM, D_MODEL)
    outs = [[] for _ in range(8)]
    for layer in range(DEPTH):
        lb_raw = hgrn_lb_raw
        norm_g = hgrn_norm_g[layer][None]
        lg, lbias = ln_g[layer][None], ln_b[layer][None]

        proj, proj_s = _project(hp, hs, w_in[layer], PROJ_ROWS, PROJ_FIRST_COLS, PROJ_COLS)
        za, k1, v1 = _prompt_attention(proj, cos_p, sin_p, bsz, seq)
        zh, s1 = _prompt_hgrn(proj, lb_raw, norm_g, layer, bsz, seq)
        zm, mk1, mv1 = _prompt_mem(proj, mem, w_mem_kv[layer], bsz, seq)
        hp, w_out_b = _merge(za, zh, zm, hp, w_out[layer], lg, lbias, MERGE_ROWS)

        proj_s = proj_s.transpose(1, 0, 2)
        zs, k2, v2, s2 = _sample_mixers(proj_s, cos_s, sin_s, cache_win_k, cache_win_v, state_hgrn,
                                        cache_mem_k, cache_mem_v, lb_raw, norm_g, layer)
        zs = zs.transpose(1, 0, 2)
        hs = _merge(zs[:N_HEADS_ATTN], zs[N_HEADS_ATTN:N_HEADS_ATTN + N_HEADS_HGRN],
                    zs[N_HEADS_ATTN + N_HEADS_HGRN:], hs, w_out_b, lg, lbias, dbsz)

        new = (k1.transpose(0, 2, 1, 3), v1.transpose(0, 2, 1, 3), s1, mk1, mv1,
               k2.reshape(dbsz, n_new, N_HEADS_ATTN, HEAD_DIM), v2.reshape(dbsz, n_new, N_HEADS_ATTN, HEAD_DIM),
               s2.astype(state_hgrn.dtype))
        for acc, val in zip(outs, new):
            acc.append(val)

    return (hp.reshape(bsz, seq, D_MODEL), hs.reshape(dbsz, n_new, D_MODEL), *[jnp.stack(o) for o in outs])
```

```python
import functools

import jax
import jax.numpy as jnp
import numpy as np
from jax import lax
from jax.experimental import pallas as pl
from jax.experimental.pallas import tpu as pltpu

F32 = jnp.float32
BF16 = jnp.bfloat16

D_MODEL = 2048
DEPTH = 1
PAST_LEN = 8192
HEAD_DIM = 128
N_HEADS_ATTN = 6
N_HEADS_HGRN = 6
N_HEADS_MEM = 4
W_ATTN = N_HEADS_ATTN * HEAD_DIM
W_HGRN = N_HEADS_HGRN * HEAD_DIM
W_MEM = N_HEADS_MEM * HEAD_DIM
MIX_WIDTH = W_ATTN + W_HGRN + W_MEM
WINDOWS = (128, 512, 2048)
DILATIONS = (1, 4, 16)
N_MEM = 256
ROPE_THETA = 10000.0
LN_EPS = 1e-5
RMS_EPS = 1e-6
NEG_INF = -1e30
DEEPNORM_ALPHA = (2 * DEPTH) ** 0.25
ATTN_SCALE = HEAD_DIM ** -0.5
ATTN_SCALE_LOG2 = ATTN_SCALE * 1.4426950408889634

G_QA, G_KA, G_VA, G_GA = 0, 6, 12, 18
G_QB, G_FB, G_IB, G_GB = 24, 30, 36, 42
G_QM, G_GM = 48, 52
N_GROUPS = 56

LANES = 128
BAND = 128
ATTN_GROUP = 8
HGRN_CHUNK = 64
HGRN_SUB = 16
HGRN_HEADS_PER_STEP = 3
HGRN_PREP_ROWS = 128
HGRN_SCAN_CHUNKS = 32
HGRN_SAFE_LOG_RANGE = 70.0
VMEM_LIMIT = 48 * 1024 * 1024
PROJ_ROWS, PROJ_COLS, PROJ_FIRST_COLS = 1024, 1024, 512
MERGE_ROWS = 512
MERGE_CAST_ROWS = 256


def _params(sem, vmem=VMEM_LIMIT):
    return pltpu.CompilerParams(dimension_semantics=sem, vmem_limit_bytes=vmem)


def _sigmoid(x):
    return 1.0 / (1.0 + jnp.exp(-x))


def _silu(x):
    return x * _sigmoid(x)


def _dot_nt(a, b):
    return lax.dot_general(a, b, (((1,), (1,)), ((), ())), preferred_element_type=F32)


def _store_slabs(ref, acc):
    for c in range(ref.shape[0]):
        ref[c] = acc[:, c * LANES:(c + 1) * LANES]


def _proj_first_kernel(x_ref, xs_ref, w_ref, o_ref, os_ref, wb_ref, xb_ref):
    tm = x_ref.shape[0]

    @pl.when(pl.program_id(0) == 0)
    def _():
        xb_ref[:tm, :] = x_ref[...].astype(BF16)
        xb_ref[tm:, :] = xs_ref[...].astype(BF16)

    wb_ref[...] = w_ref[...].astype(BF16)
    acc = jnp.dot(xb_ref[...], wb_ref[...], preferred_element_type=F32)
    _store_slabs(o_ref, acc[:tm])
    _store_slabs(os_ref, acc[tm:])


def _proj_rest_kernel(x_ref, w_ref, first_ref, o_hbm, xb_ref, obuf, sem_out, sem_first):
    i, j = pl.program_id(0), pl.program_id(1)
    n_col = pl.num_programs(1)
    step = i * n_col + j
    last = pl.num_programs(0) * n_col - 1
    slot = step % 2
    tm = x_ref.shape[0]
    n_slab = obuf.shape[1]

    def tile_copy(stp):
        row_tile, col_tile = stp // n_col + 1, stp % n_col
        return pltpu.make_async_copy(
            obuf.at[stp % 2], o_hbm.at[pl.ds(col_tile * n_slab, n_slab), pl.ds(row_tile * tm, tm), :],
            sem_out.at[stp % 2])

    first_copy = pltpu.make_async_copy(first_ref, o_hbm.at[pl.ds(j * n_slab, n_slab), pl.ds(0, tm), :], sem_first.at[0])

    @pl.when(i == 0)
    def _():
        first_copy.start()

    @pl.when(j == 0)
    def _():
        xb_ref[...] = x_ref[...].astype(BF16)

    @pl.when(step >= 2)
    def _():
        tile_copy(step - 2).wait()

    _store_slabs(obuf.at[slot], jnp.dot(xb_ref[...], w_ref[...], preferred_element_type=F32))
    tile_copy(step).start()

    @pl.when(i == 0)
    def _():
        first_copy.wait()

    @pl.when(step == last)
    def _():
        tile_copy(step - 1).wait()
        tile_copy(step).wait()


def _project(x, xs, w, tm, tn_first, tn):
    m, k = x.shape
    ms = xs.shape[0]
    n = w.shape[1]
    assert (m // tm - 1) * (n // tn) >= 2
    proj_first, proj_s, w_bf16 = pl.pallas_call(
        _proj_first_kernel,
        grid=(n // tn_first,),
        in_specs=[pl.BlockSpec((tm, k), lambda j: (0, 0)),
                  pl.BlockSpec((ms, k), lambda j: (0, 0)),
                  pl.BlockSpec((k, tn_first), lambda j: (0, j))],
        out_specs=[pl.BlockSpec((tn_first // LANES, tm, LANES), lambda j: (j, 0, 0)),
                   pl.BlockSpec((tn_first // LANES, ms, LANES), lambda j: (j, 0, 0)),
                   pl.BlockSpec((k, tn_first), lambda j: (0, j))],
        out_shape=[jax.ShapeDtypeStruct((n // LANES, tm, LANES), F32),
                   jax.ShapeDtypeStruct((n // LANES, ms, LANES), F32),
                   jax.ShapeDtypeStruct((k, n), BF16)],
        scratch_shapes=[pltpu.VMEM((tm + ms, k), BF16)],
        compiler_params=_params(("arbitrary",)),
        name="in_proj_first",
    )(x, xs, w)
    proj = pl.pallas_call(
        _proj_rest_kernel,
        grid=(m // tm - 1, n // tn),
        in_specs=[pl.BlockSpec((tm, k), lambda i, j: (i + 1, 0)),
                  pl.BlockSpec((k, tn), lambda i, j: (0, j)),
                  pl.BlockSpec((tn // LANES, tm, LANES), lambda i, j: (jnp.where(i == 0, j, n // tn - 1), 0, 0))],
        out_specs=pl.BlockSpec(memory_space=pl.ANY),
        out_shape=jax.ShapeDtypeStruct((n // LANES, m, LANES), F32),
        scratch_shapes=[pltpu.VMEM((tm, k), BF16), pltpu.VMEM((2, tn // LANES, tm, LANES), F32),
                        pltpu.SemaphoreType.DMA((2,)), pltpu.SemaphoreType.DMA((1,))],
        compiler_params=_params(("arbitrary", "arbitrary")),
        name="in_proj",
    )(x, w_bf16, proj_first)
    return proj, proj_s


def _rope(x, cos, sin_signed):
    return x * cos + pltpu.roll(x, HEAD_DIM // 2, 1) * sin_signed


def _bias_from_count(count):
    return jnp.where(count > 1.5, 1.0, jnp.where(count > 0.5, 0.0, NEG_INF)).astype(F32)


def _block_deltas(block_gap):
    qi = lax.broadcasted_iota(jnp.int32, (BAND, BAND), 0)
    ki = lax.broadcasted_iota(jnp.int32, (BAND, BAND), 1)
    return block_gap * BAND + qi - ki


def _near_bias(block_gap):
    d = _block_deltas(block_gap)
    return _bias_from_count(((d >= 0) & (d <= BAND)).astype(F32))


def _class_bias(block_gap, far_step):
    d = _block_deltas(block_gap)
    near = (d >= 0) & (d <= BAND)
    far = (d >= 0) & ((d & (far_step - 1)) == 0)
    return _bias_from_count(near.astype(F32) + far.astype(F32))


def _softmax_stage(scores, floors=None):
    out = []
    for idx, s in enumerate(scores):
        m = jnp.broadcast_to(jnp.max(s, axis=-1, keepdims=True), (BAND, HEAD_DIM))
        if floors is not None:
            m = jnp.maximum(m, floors[idx])
        m_wide = jnp.concatenate([m] * (s.shape[1] // HEAD_DIM), axis=1)
        out.append((m, jnp.exp2(s - m_wide).astype(BF16)))
    return out


def _attn_kernel(q_ref, k_ref, v_ref, g_ref, cos_ref, sin_ref, z_ref, ko_ref, vo_ref,
                 qs_ref, qb_ref, kb_ref, vb_ref, qc_ref, kc_ref, vc_ref, acc_ref, m_ref, l_ref):
    seq = q_ref.shape[0]
    d_mid, d_far = DILATIONS[1], DILATIONS[2]
    far_step = d_far // d_mid
    cls_rows = seq // d_mid
    cos = cos_ref[...]
    sin = sin_ref[...]
    qs_ref[...] = _rope(q_ref[...], cos, sin) * ATTN_SCALE_LOG2
    ko_ref[...] = _rope(k_ref[...], cos, sin)
    vo_ref[...] = v_ref[...]
    qb_ref[...] = qs_ref[...].astype(BF16)
    kb_ref[...] = ko_ref[...].astype(BF16)
    ones = jnp.ones((seq, HEAD_DIM), BF16)
    vb_ref[:, :HEAD_DIM] = v_ref[...].astype(BF16)
    vb_ref[:, HEAD_DIM:] = ones
    vc_ref[:, HEAD_DIM:] = ones
    for r in range(d_mid):
        cls, dst = pl.ds(r, cls_rows, stride=d_mid), pl.ds(r * cls_rows, cls_rows)
        qc_ref[dst, :] = qs_ref[cls, :].astype(BF16)
        kc_ref[dst, :] = ko_ref[cls, :].astype(BF16)
        vc_ref[dst, :HEAD_DIM] = v_ref[cls, :].astype(BF16)

    n_cls_blk = cls_rows // BAND
    cls_bias = [_class_bias(gap, far_step) for gap in range(min(n_cls_blk, 3))]
    items = [(r, n) for r in range(d_mid) for n in range(n_cls_blk)]
    for g0 in range(0, len(items), ATTN_GROUP):
        group = items[g0:g0 + ATTN_GROUP]
        keys = [pl.ds(r * cls_rows, (n + 1) * BAND) for r, n in group]
        scores = [_dot_nt(qc_ref[pl.ds(r * cls_rows + n * BAND, BAND), :], kc_ref[kr, :])
                  + jnp.concatenate([cls_bias[min(n - nk, 2)] for nk in range(n + 1)], axis=1)
                  for (r, n), kr in zip(group, keys)]
        for (r, n), kr, (m, p) in zip(group, keys, _softmax_stage(scores)):
            rows = pl.ds(r + d_mid * BAND * n, BAND, stride=d_mid)
            pv = jnp.dot(p, vc_ref[kr, :], preferred_element_type=F32)
            acc_ref[rows, :] = pv[:, :HEAD_DIM]
            l_ref[rows, :] = pv[:, HEAD_DIM:]
            m_ref[rows, :] = m

    near_bias = [_near_bias(0), jnp.concatenate([_near_bias(1), _near_bias(0)], axis=1)]
    items = list(range(seq // BAND))
    for g0 in range(0, len(items), ATTN_GROUP):
        group = items[g0:g0 + ATTN_GROUP]
        rows = [pl.ds(n * BAND, BAND) for n in group]
        keys = [pl.ds(max(n - 1, 0) * BAND, BAND * min(n + 1, 2)) for n in group]
        scores = [_dot_nt(qb_ref[rw, :], kb_ref[kr, :]) + near_bias[min(n, 1)]
                  for n, rw, kr in zip(group, rows, keys)]
        probs = _softmax_stage(scores, floors=[m_ref[rw, :] for rw in rows])
        for rw, kr, (m, p) in zip(rows, keys, probs):
            w = jnp.exp2(m_ref[rw, :] - m)
            pv = jnp.dot(p, vb_ref[kr, :], preferred_element_type=F32)
            num = pv[:, :HEAD_DIM] + w * acc_ref[rw, :]
            den = pv[:, HEAD_DIM:] + w * l_ref[rw, :]
            z_ref[rw, :] = ((num / den) * _silu(g_ref[rw, :])).astype(z_ref.dtype)


def _prompt_attention(proj, cos, sin, bsz, seq):
    def col(g0):
        return pl.BlockSpec((None, seq, HEAD_DIM), lambda b, h: (g0 + h, b, 0))

    table = pl.BlockSpec((seq, HEAD_DIM), lambda b, h: (0, 0))
    kv_out = pl.BlockSpec((None, None, seq, HEAD_DIM), lambda b, h: (b, h, 0, 0))
    d_near, d_mid, d_far = DILATIONS
    far_step = d_far // d_mid
    assert d_near == 1 and d_far % d_mid == 0 and far_step & (far_step - 1) == 0
    assert all(w // d == BAND for w, d in zip(WINDOWS, DILATIONS)) and seq % (d_mid * BAND) == 0
    assert seq // d_mid <= far_step * BAND
    return pl.pallas_call(
        _attn_kernel,
        grid=(bsz, N_HEADS_ATTN),
        in_specs=[col(G_QA), col(G_KA), col(G_VA), col(G_GA), table, table],
        out_specs=[pl.BlockSpec((None, seq, HEAD_DIM), lambda b, h: (h, b, 0)), kv_out, kv_out],
        out_shape=[jax.ShapeDtypeStruct((N_HEADS_ATTN, bsz * seq, HEAD_DIM), BF16),
                   jax.ShapeDtypeStruct((bsz, N_HEADS_ATTN, seq, HEAD_DIM), F32),
                   jax.ShapeDtypeStruct((bsz, N_HEADS_ATTN, seq, HEAD_DIM), F32)],
        scratch_shapes=[pltpu.VMEM((seq, HEAD_DIM), F32)]
                       + [pltpu.VMEM((seq, HEAD_DIM), BF16), pltpu.VMEM((seq, HEAD_DIM), BF16),
                          pltpu.VMEM((seq, 2 * HEAD_DIM), BF16)] * 2
                       + [pltpu.VMEM((seq, HEAD_DIM), F32)] * 3,
        compiler_params=_params(("parallel", "parallel")),
        name="prompt_attn",
    )(proj, proj, proj, proj, cos, sin)


def _lower_bound(lb_raw, layer):
    e = jnp.exp(lb_raw - jnp.max(lb_raw, axis=0, keepdims=True))
    sm = e / jnp.sum(e, axis=0, keepdims=True)
    return jnp.sum(sm[:layer + 1], axis=0, keepdims=True)


def _split2(x):
    hi = x.astype(BF16)
    return hi, (x - hi.astype(F32)).astype(BF16)


def _rms_gate(o, norm_g, gate):
    o = o * lax.rsqrt(jnp.mean(o * o, axis=-1, keepdims=True) + RMS_EPS)
    return o * norm_g * _silu(gate)


def _hgrn_gates(fb, lb, tril):
    f = lb + (1.0 - lb) * _sigmoid(fb)
    g = jnp.log(f)
    b = sum(jnp.dot(tril, piece, preferred_element_type=F32) for piece in _split2(g))
    return 1.0 - f, b


def _hgrn_fast_prepare(q_ref, f_ref, lb, qh_ref, kh_ref, el_ref):
    n_heads, seq, _ = q_ref.shape
    blk = HGRN_PREP_ROWS
    tril = _tril_ones(blk)

    def body(bi, carry):
        r0 = pl.multiple_of(bi * blk, blk)
        fb = jnp.concatenate([f_ref[h, pl.ds(r0, blk), :] for h in range(n_heads)], axis=1)
        f = lb + (1.0 - lb) * _sigmoid(fb)
        b_blk = sum(jnp.dot(tril, piece, preferred_element_type=F32) for piece in _split2(jnp.log(f)))
        kk = 1.0 - f
        for c0 in range(0, blk, HGRN_CHUNK):
            rs = slice(c0, c0 + HGRN_CHUNK)
            b = b_blk[rs] - b_blk[c0 - 1:c0] if c0 else b_blk[rs]
            rows = pl.ds(r0 + c0, HGRN_CHUNK)
            q = jnp.concatenate([q_ref[h, rows, :] for h in range(n_heads)], axis=1)
            qh_ref[rows, :] = (q * jnp.exp(b)).astype(BF16)
            kh_ref[rows, :] = (kk[rs] * jnp.exp(-b)).astype(BF16)
            el_ref[pl.ds(bi * (blk // HGRN_CHUNK) + c0 // HGRN_CHUNK, 1), :] = jnp.exp(b[HGRN_CHUNK - 1:])
        return carry

    lax.fori_loop(0, seq // blk, body, 0, unroll=4)


def _hgrn_fast_scan(i_ref, g_ref, norm_g, qh_ref, kh_ref, el_ref, st_ref, z_ref):
    n_heads, seq, _ = i_ref.shape
    c = HGRN_CHUNK
    heads = range(n_heads)
    chunks = range(HGRN_SCAN_CHUNKS)
    lanes = [slice(h * HEAD_DIM, (h + 1) * HEAD_DIM) for h in heads]
    causal = lax.broadcasted_iota(jnp.int32, (c, c), 0) >= lax.broadcasted_iota(jnp.int32, (c, c), 1)
    tn = (((0,), (0,)), ((), ()))

    def body(ti, carry):
        rows = [pl.ds(pl.multiple_of((ti * HGRN_SCAN_CHUNKS + k) * c, c), c) for k in chunks]
        qh = [[qh_ref[rows[k], lanes[h]] for h in heads] for k in chunks]
        kh = [[kh_ref[rows[k], lanes[h]] for h in heads] for k in chunks]
        vb = [[i_ref[h, rows[k], :].astype(BF16) for h in heads] for k in chunks]
        att = [[jnp.where(causal, _dot_nt(qh[k][h], kh[k][h]), 0.0).astype(BF16) for h in heads] for k in chunks]
        ds = [[lax.dot_general(vb[k][h], kh[k][h], tn, preferred_element_type=F32) for h in heads] for k in chunks]
        st = [st_ref[h] for h in heads]
        o = []
        for k in chunks:
            el = el_ref[pl.ds(ti * HGRN_SCAN_CHUNKS + k, 1), :]
            o.append([_dot_nt(qh[k][h], st[h].astype(BF16)) for h in heads])
            st = [(st[h] + ds[k][h]) * el[:, lanes[h]] for h in heads]
        for h in heads:
            st_ref[h] = st[h]
        intra = [[jnp.dot(att[k][h], vb[k][h], preferred_element_type=F32) for h in heads] for k in chunks]
        for k in chunks:
            for h in heads:
                z_ref[h, rows[k], :] = _rms_gate(o[k][h] + intra[k][h], norm_g[:, lanes[h]],
                                                 g_ref[h, rows[k], :]).astype(z_ref.dtype)
        return carry

    lax.fori_loop(0, seq // (c * HGRN_SCAN_CHUNKS), body, 0)


def _hgrn_chunk(q, fb, v, lb, st, tril):
    c = q.shape[0]
    kk, b = _hgrn_gates(fb, lb, tril)
    o = _dot_nt((q * jnp.exp(b)).astype(BF16), st.astype(BF16))

    s_idx = lax.broadcasted_iota(jnp.int32, (c, 1), 0)
    lane = lax.broadcasted_iota(jnp.int32, (HGRN_SUB, c), 1)
    row = lax.broadcasted_iota(jnp.int32, (HGRN_SUB, c), 0)
    att_rows = []
    for i0 in range(0, c, HGRN_SUB):
        qi = q[i0:i0 + HGRN_SUB]
        bi = b[i0:i0 + HGRN_SUB]
        if i0 > 0:
            bref = b[i0 - 1:i0]
            kt = jnp.where(s_idx < i0, kk * jnp.exp(jnp.minimum(bref - b, 0.0)), 0.0)
            att = _dot_nt((qi * jnp.exp(bi - bref)).astype(BF16), kt.astype(BF16))
        else:
            att = jnp.zeros((HGRN_SUB, c), F32)
        for j in range(HGRN_SUB):
            s = i0 + j
            e = jnp.exp(jnp.minimum(bi - b[s:s + 1], 0.0))
            colv = jnp.sum(qi * kk[s:s + 1] * e, axis=-1, keepdims=True)
            att = jnp.where((lane == s) & (row >= j), colv, att)
        att_rows.append(att)
    att = jnp.concatenate(att_rows, axis=0)
    vb = v.astype(BF16)
    o = o + jnp.dot(att.astype(BF16), vb, preferred_element_type=F32)

    b_last = b[c - 1:c]
    kd = (kk * jnp.exp(b_last - b)).astype(BF16)
    st_new = st * jnp.exp(b_last) + lax.dot_general(vb, kd, (((0,), (0,)), ((), ())),
                                                    preferred_element_type=F32)
    return o, st_new


def _tril_ones(c):
    return (lax.broadcasted_iota(jnp.int32, (c, c), 0) >= lax.broadcasted_iota(jnp.int32, (c, c), 1)
            ).astype(BF16)


def _hgrn_kernel(layer, q_ref, f_ref, i_ref, g_ref, lb_ref, ng_ref, z_ref, s_ref, st_ref, qh_ref, kh_ref, el_ref):
    n_heads, seq, _ = q_ref.shape
    lb = _lower_bound(lb_ref[...], layer)
    norm_g = ng_ref[...]
    st_ref[...] = jnp.zeros_like(st_ref)

    def fast():
        _hgrn_fast_prepare(q_ref, f_ref, lb, qh_ref, kh_ref, el_ref)
        _hgrn_fast_scan(i_ref, g_ref, norm_g, qh_ref, kh_ref, el_ref, st_ref, z_ref)

    def safe():
        tril = _tril_ones(HGRN_CHUNK)

        def body(ci, carry):
            rows = pl.ds(pl.multiple_of(ci * HGRN_CHUNK, HGRN_CHUNK), HGRN_CHUNK)
            for h in range(n_heads):
                lanes = slice(h * HEAD_DIM, (h + 1) * HEAD_DIM)
                o, st = _hgrn_chunk(q_ref[h, rows, :], f_ref[h, rows, :], i_ref[h, rows, :], lb[:, lanes],
                                    st_ref[h], tril)
                st_ref[h] = st
                z_ref[h, rows, :] = _rms_gate(o, norm_g[:, lanes], g_ref[h, rows, :]).astype(z_ref.dtype)
            return carry

        lax.fori_loop(0, seq // HGRN_CHUNK, body, 0)

    fast_ok = HGRN_CHUNK * -jnp.log(jnp.min(lb)) <= HGRN_SAFE_LOG_RANGE
    lax.cond(fast_ok, fast, safe)
    for h in range(n_heads):
        s_ref[h] = st_ref[h].T


def _prompt_hgrn(proj, lb_raw, norm_g, layer, bsz, seq):
    hps = HGRN_HEADS_PER_STEP
    assert N_HEADS_HGRN % hps == 0 and all(g % hps == 0 for g in (G_QB, G_FB, G_IB, G_GB))
    assert seq % (HGRN_CHUNK * HGRN_SCAN_CHUNKS) == 0 and seq % HGRN_PREP_ROWS == 0 and HGRN_PREP_ROWS % HGRN_CHUNK == 0

    def cols(g0):
        return pl.BlockSpec((hps, seq, HEAD_DIM), lambda b, h: (g0 // hps + h, b, 0))

    return pl.pallas_call(
        functools.partial(_hgrn_kernel, layer),
        grid=(bsz, N_HEADS_HGRN // hps),
        in_specs=[cols(G_QB), cols(G_FB), cols(G_IB), cols(G_GB),
                  pl.BlockSpec((DEPTH + 1, hps * HEAD_DIM), lambda b, h: (0, h)),
                  pl.BlockSpec((1, hps * HEAD_DIM), lambda b, h: (0, h))],
        out_specs=[pl.BlockSpec((hps, seq, HEAD_DIM), lambda b, h: (h, b, 0)),
                   pl.BlockSpec((None, hps, HEAD_DIM, HEAD_DIM), lambda b, h: (b, h, 0, 0))],
        out_shape=[jax.ShapeDtypeStruct((N_HEADS_HGRN, bsz * seq, HEAD_DIM), BF16),
                   jax.ShapeDtypeStruct((bsz, N_HEADS_HGRN, HEAD_DIM, HEAD_DIM), F32)],
        scratch_shapes=[pltpu.VMEM((hps, HEAD_DIM, HEAD_DIM), F32),
                        pltpu.VMEM((seq, hps * HEAD_DIM), BF16),
                        pltpu.VMEM((seq, hps * HEAD_DIM), BF16),
                        pltpu.VMEM((seq // HGRN_CHUNK, hps * HEAD_DIM), F32)],
        compiler_params=_params(("parallel", "parallel")),
        name="prompt_hgrn",
    )(proj, proj, proj, proj, lb_raw, norm_g)


MEM_ROWS = 256
MEM_GROUP = 4


def _mem_kernel(q_ref, g_ref, mem_ref, w_ref, z_ref, mk_ref, mv_ref):
    seq = q_ref.shape[1]
    kv = jnp.dot(mem_ref[...].astype(BF16), w_ref[...].astype(BF16), preferred_element_type=F32)
    ones = jnp.ones((N_MEM, HEAD_DIM), BF16)
    mk_b, mv_b = [], []
    for h in range(N_HEADS_MEM):
        mk = kv[:, h * HEAD_DIM:(h + 1) * HEAD_DIM]
        mv = kv[:, W_MEM + h * HEAD_DIM:W_MEM + (h + 1) * HEAD_DIM]
        mk_ref[:, h, :] = mk
        mv_ref[:, h, :] = mv
        mk_b.append(mk.astype(BF16))
        mv_b.append(jnp.concatenate([mv.astype(BF16), ones], axis=1))

    items = [(h, pl.ds(n * MEM_ROWS, MEM_ROWS)) for h in range(N_HEADS_MEM) for n in range(seq // MEM_ROWS)]
    for g0 in range(0, len(items), MEM_GROUP):
        group = items[g0:g0 + MEM_GROUP]
        scores = [_dot_nt((q_ref[h, rows, :] * ATTN_SCALE_LOG2).astype(BF16), mk_b[h]) for h, rows in group]
        probs = [jnp.exp2(s - jnp.max(s, axis=-1, keepdims=True)).astype(BF16) for s in scores]
        for (h, rows), p in zip(group, probs):
            pv = jnp.dot(p, mv_b[h], preferred_element_type=F32)
            o = pv[:, :HEAD_DIM] / pv[:, HEAD_DIM:]
            z_ref[h, rows, :] = (o * _silu(g_ref[h, rows, :])).astype(z_ref.dtype)


def _prompt_mem(proj, mem, w_kv, bsz, seq):
    assert G_QM % N_HEADS_MEM == 0 and G_GM % N_HEADS_MEM == 0

    def cols(g0):
        return pl.BlockSpec((N_HEADS_MEM, seq, HEAD_DIM), lambda b: (g0 // N_HEADS_MEM, b, 0))

    kv_out = pl.BlockSpec((None, N_MEM, N_HEADS_MEM, HEAD_DIM), lambda b: (b, 0, 0, 0))
    kv_shape = jax.ShapeDtypeStruct((bsz, N_MEM, N_HEADS_MEM, HEAD_DIM), F32)
    return pl.pallas_call(
        _mem_kernel,
        grid=(bsz,),
        in_specs=[cols(G_QM), cols(G_GM), pl.BlockSpec((N_MEM, D_MODEL), lambda b: (b, 0)),
                  pl.BlockSpec((D_MODEL, 2 * W_MEM), lambda b: (0, 0))],
        out_specs=[pl.BlockSpec((N_HEADS_MEM, seq, HEAD_DIM), lambda b: (0, b, 0)), kv_out, kv_out],
        out_shape=[jax.ShapeDtypeStruct((N_HEADS_MEM, bsz * seq, HEAD_DIM), BF16), kv_shape, kv_shape],
        compiler_params=_params(("parallel",)),
        name="prompt_mem",
    )(proj, proj, mem, w_kv)


def _merge_kernel(cast_w, za_ref, zh_ref, zm_ref, x_ref, w_ref, lg_ref, lb_ref, o_ref, *rest):
    if cast_w:
        wb_ref, z_ref, stage, sem = rest
        n_rows = stage.shape[1]
        n_chunks = w_ref.shape[0] // n_rows

        def chunk_copy(c):
            return pltpu.make_async_copy(w_ref.at[pl.ds(c * n_rows, n_rows), :], stage.at[c % 2], sem.at[c % 2])

        @pl.when(pl.program_id(0) == 0)
        def _():
            chunk_copy(0).start()
            for c in range(n_chunks):
                if c + 1 < n_chunks:
                    chunk_copy(c + 1).start()
                chunk_copy(c).wait()
                wb_ref[c * n_rows:(c + 1) * n_rows, :] = stage[c % 2].astype(BF16)
    else:
        wb_ref, (z_ref,) = w_ref, rest

    c0 = 0
    for ref in (za_ref, zh_ref, zm_ref):
        for c in range(ref.shape[0]):
            z_ref[:, (c0 + c) * LANES:(c0 + c + 1) * LANES] = ref[c].astype(BF16)
        c0 += ref.shape[0]
    y = jnp.dot(z_ref[...], wb_ref[...], preferred_element_type=F32)
    r = DEEPNORM_ALPHA * x_ref[...] + y
    mu = jnp.mean(r, axis=-1, keepdims=True)
    d = r - mu
    var = jnp.mean(d * d, axis=-1, keepdims=True)
    o_ref[...] = d * lax.rsqrt(var + LN_EPS) * lg_ref[...] + lb_ref[...]


def _merge(za, zh, zm, x, w_out, ln_g, ln_b, tm):
    m = x.shape[0]
    cast_w = w_out.dtype != BF16

    def slab(a):
        return pl.BlockSpec((a.shape[0], tm, LANES), lambda i: (0, i, 0))

    const = lambda shape: pl.BlockSpec(shape, lambda i: (0, 0))
    y_spec, y_shape = pl.BlockSpec((tm, D_MODEL), lambda i: (i, 0)), jax.ShapeDtypeStruct((m, D_MODEL), F32)
    w_block = const((MIX_WIDTH, D_MODEL))
    z_scratch = pltpu.VMEM((tm, MIX_WIDTH), BF16)
    if cast_w:
        w_spec = pl.BlockSpec(memory_space=pl.ANY)
        out_specs, out_shape = [y_spec, w_block], [y_shape, jax.ShapeDtypeStruct((MIX_WIDTH, D_MODEL), BF16)]
        scratch = [z_scratch, pltpu.VMEM((2, MERGE_CAST_ROWS, D_MODEL), F32), pltpu.SemaphoreType.DMA((2,))]
    else:
        w_spec, out_specs, out_shape, scratch = w_block, y_spec, y_shape, [z_scratch]
    return pl.pallas_call(
        functools.partial(_merge_kernel, cast_w),
        grid=(m // tm,),
        in_specs=[slab(za), slab(zh), slab(zm), pl.BlockSpec((tm, D_MODEL), lambda i: (i, 0)),
                  w_spec, const((1, D_MODEL)), const((1, D_MODEL))],
        out_specs=out_specs,
        out_shape=out_shape,
        scratch_shapes=scratch,
        compiler_params=_params(("arbitrary",)),
        name="merge",
    )(za, zh, zm, x, w_out, ln_g, ln_b)


def _column(row):
    return jnp.broadcast_to(row, (HEAD_DIM, HEAD_DIM)).T


HBM_TILE_ROWS = 8
SAMPLE_ROWS = 4


def _window_pieces(past):
    pieces = []
    for dil in DILATIONS:
        if dil == 1:
            pieces.append((0, past - BAND, BAND, None))
        elif dil < HBM_TILE_ROWS:
            n = BAND * dil // HBM_TILE_ROWS
            pieces += [(1, past // HBM_TILE_ROWS - n, n, r) for r in range(0, HBM_TILE_ROWS, dil)]
        else:
            pieces.append((2, past // dil - BAND, BAND, 0))
    return pieces


def _cache_views(cache):
    depth, bsz, past, nh, hd = cache.shape
    assert all(d == 1 or HBM_TILE_ROWS % d == 0 or d % HBM_TILE_ROWS == 0 for d in DILATIONS)
    big = max(DILATIONS)
    hm = cache.transpose(0, 1, 3, 2, 4)
    return (hm, hm.reshape(depth, bsz, nh, past // HBM_TILE_ROWS, HBM_TILE_ROWS, hd),
            hm.reshape(depth, bsz, nh, past // big, big, hd))


def _sample_row(layer, p_ref, cos, sin, k_win, v_win, st_ref, mk_ref, mv_ref, lb_all, ng_ref, z_ref, ko_ref, vo_ref,
                so_ref):
    q_all = _rope(p_ref[G_QA:G_QA + N_HEADS_ATTN, :], cos, sin) * ATTN_SCALE
    k_all = _rope(p_ref[G_KA:G_KA + N_HEADS_ATTN, :], cos, sin)
    v_all = p_ref[G_VA:G_VA + N_HEADS_ATTN, :]
    ko_ref[...] = k_all
    vo_ref[...] = v_all

    for h in range(N_HEADS_ATTN):
        q = q_all[h:h + 1]
        s_new = jnp.sum(q * k_all[h:h + 1], axis=-1, keepdims=True)
        s = jnp.sum(k_win[h] * q, axis=-1, keepdims=True)
        m = jnp.maximum(jnp.max(s, axis=0, keepdims=True), s_new)
        p = jnp.exp(s - m)
        p_new = jnp.exp(s_new - m) * len(DILATIONS)
        den = jnp.sum(p, axis=0, keepdims=True) + p_new
        num = jnp.sum(p * v_win[h], axis=0, keepdims=True) + p_new * v_all[h:h + 1]
        z_ref[h:h + 1, :] = (num / den) * _silu(p_ref[G_GA + h:G_GA + h + 1, :])

    for h in range(N_HEADS_HGRN):
        lanes = slice(h * HEAD_DIM, (h + 1) * HEAD_DIM)
        lb = _lower_bound(lb_all[:, lanes], layer)
        f = lb + (1.0 - lb) * _sigmoid(p_ref[G_FB + h:G_FB + h + 1, :])
        f_col = _column(f)
        q_col = _column(p_ref[G_QB + h:G_QB + h + 1, :])
        s_new = f_col * st_ref[h] + (1.0 - f_col) * p_ref[G_IB + h:G_IB + h + 1, :]
        so_ref[h] = s_new
        o = jnp.sum(s_new * q_col, axis=0, keepdims=True)
        z_ref[N_HEADS_ATTN + h:N_HEADS_ATTN + h + 1, :] = _rms_gate(
            o, ng_ref[:, lanes], p_ref[G_GB + h:G_GB + h + 1, :])

    fold = mk_ref.shape[1] // N_HEADS_MEM
    unfold = lambda a: sum(a[i * N_HEADS_MEM:(i + 1) * N_HEADS_MEM] for i in range(fold))
    q_m = p_ref[G_QM:G_QM + N_HEADS_MEM, :] * ATTN_SCALE
    s = jnp.sum(mk_ref[...] * jnp.concatenate([q_m] * fold, axis=0)[None], axis=-1, keepdims=True)
    m = jnp.max(s, axis=0)
    m = functools.reduce(jnp.maximum, [m[i * N_HEADS_MEM:(i + 1) * N_HEADS_MEM] for i in range(fold)])
    p = jnp.exp(s - jnp.concatenate([m] * fold, axis=0)[None])
    o = unfold(jnp.sum(p * mv_ref[...], axis=0)) / unfold(jnp.sum(p, axis=0))
    row = N_HEADS_ATTN + N_HEADS_HGRN
    z_ref[row:row + N_HEADS_MEM, :] = o * _silu(p_ref[G_GM:G_GM + N_HEADS_MEM, :])


def _sample_kernel(layer, past, p_ref, cos_ref, sin_ref, k0_hbm, k1_hbm, k2_hbm, v0_hbm, v1_hbm, v2_hbm,
                   st_ref, mk_ref, mv_ref, lb_ref, ng_ref, z_ref, ko_ref, vo_ref, so_ref, kbuf, vbuf, sem):
    step = pl.program_id(0)
    slot = step % 2
    n_rows = p_ref.shape[0]
    pieces = _window_pieces(past)

    def window_copies(stp, sl):
        out = []
        for rr in range(n_rows):
            for ci, (views, buf) in enumerate((((k0_hbm, k1_hbm, k2_hbm), kbuf), ((v0_hbm, v1_hbm, v2_hbm), vbuf))):
                off = 0
                for pi, (vi, start, count, res) in enumerate(pieces):
                    view, row = views[vi], stp * n_rows + rr
                    src = (view.at[layer, row, :, pl.ds(start, count), :] if res is None
                           else view.at[layer, row, :, pl.ds(start, count), res, :])
                    out.append(pltpu.make_async_copy(src, buf.at[sl, rr, :, pl.ds(off, count), :],
                                                     sem.at[sl, rr, ci, pi]))
                    off += count
        return out

    @pl.when(step == 0)
    def _():
        for n, c in enumerate(window_copies(step, slot)):
            c.start(priority=n % 2)

    @pl.when(step + 1 < pl.num_programs(0))
    def _():
        for n, c in enumerate(window_copies(step + 1, 1 - slot)):
            c.start(priority=n % 2)

    for c in window_copies(step, slot):
        c.wait()

    cos, sin, lb_all = cos_ref[...], sin_ref[...], lb_ref[...]
    for rr in range(n_rows):
        _sample_row(layer, p_ref.at[rr], cos, sin, kbuf.at[slot, rr], vbuf.at[slot, rr], st_ref.at[rr], mk_ref.at[rr],
                    mv_ref.at[rr], lb_all, ng_ref, z_ref.at[rr], ko_ref.at[rr], vo_ref.at[rr], so_ref.at[rr])


def _sample_mixers(proj_rows, cos, sin, win_k, win_v, state, mem_k, mem_v, lb_raw, norm_g, layer):
    bsz, past = win_k.shape[1:3]
    rps = SAMPLE_ROWS
    assert all(past % d == 0 and past >= w for w, d in zip(WINDOWS, DILATIONS)) and bsz % rps == 0
    assert all(w // d == BAND for w, d in zip(WINDOWS, DILATIONS)) and past % HBM_TILE_ROWS == 0
    n_keys = BAND * len(DILATIONS)
    n_pieces = len(_window_pieces(past))
    any_spec = pl.BlockSpec(memory_space=pl.ANY)
    vec = lambda n: pl.BlockSpec((n, W_HGRN), lambda b: (0, 0))
    heads = lambda n: pl.BlockSpec((rps, n, HEAD_DIM), lambda b: (b, 0, 0))
    fold = HBM_TILE_ROWS // N_HEADS_MEM
    mem_k, mem_v = (a.reshape(a.shape[0], bsz, N_MEM // fold, fold * N_HEADS_MEM, HEAD_DIM) for a in (mem_k, mem_v))
    mem_spec = pl.BlockSpec((None, rps, N_MEM // fold, fold * N_HEADS_MEM, HEAD_DIM), lambda b: (layer, b, 0, 0, 0))
    return pl.pallas_call(
        functools.partial(_sample_kernel, layer, past),
        grid=(bsz // rps,),
        in_specs=[heads(N_GROUPS), pl.BlockSpec((1, HEAD_DIM), lambda b: (0, 0)),
                  pl.BlockSpec((1, HEAD_DIM), lambda b: (0, 0)), *([any_spec] * 6),
                  pl.BlockSpec((None, rps, N_HEADS_HGRN, HEAD_DIM, HEAD_DIM), lambda b: (layer, b, 0, 0, 0)),
                  mem_spec, mem_spec, vec(DEPTH + 1), vec(1)],
        out_specs=[heads(MIX_WIDTH // HEAD_DIM), heads(N_HEADS_ATTN), heads(N_HEADS_ATTN),
                   pl.BlockSpec((rps, N_HEADS_HGRN, HEAD_DIM, HEAD_DIM), lambda b: (b, 0, 0, 0))],
        out_shape=[jax.ShapeDtypeStruct((bsz, MIX_WIDTH // HEAD_DIM, HEAD_DIM), F32),
                   jax.ShapeDtypeStruct((bsz, N_HEADS_ATTN, HEAD_DIM), F32),
                   jax.ShapeDtypeStruct((bsz, N_HEADS_ATTN, HEAD_DIM), F32),
                   jax.ShapeDtypeStruct((bsz, N_HEADS_HGRN, HEAD_DIM, HEAD_DIM), F32)],
        scratch_shapes=[pltpu.VMEM((2, rps, N_HEADS_ATTN, n_keys, HEAD_DIM), F32),
                        pltpu.VMEM((2, rps, N_HEADS_ATTN, n_keys, HEAD_DIM), F32),
                        pltpu.SemaphoreType.DMA((2, rps, 2, n_pieces))],
        compiler_params=_params(("arbitrary",)),
        name="sample_mixers",
    )(proj_rows, cos, sin, *_cache_views(win_k), *_cache_views(win_v), state, mem_k, mem_v, lb_raw, norm_g)


def _rope_tables(pos):
    half = HEAD_DIM // 2
    inv_freq = 1.0 / (ROPE_THETA ** (np.arange(half, dtype=np.float64) / half))
    ang = np.asarray(pos, np.float64)[:, None] * inv_freq[None, :]
    cos, sin = np.cos(ang), np.sin(ang)
    return (jnp.asarray(np.concatenate([cos, cos], axis=-1), F32),
            jnp.asarray(np.concatenate([-sin, sin], axis=-1), F32))


def kernel(x_prompt, x_sample, cache_win_k, cache_win_v, state_hgrn, cache_mem_k, cache_mem_v, mem_prompt,
           w_in, w_mem_kv, hgrn_lb_raw, hgrn_norm_g, w_out, ln_g, ln_b):
    bsz, seq, _ = x_prompt.shape
    dbsz, n_new, _ = x_sample.shape
    assert n_new == 1 and seq % (BAND * max(DILATIONS)) == 0
    cos_p, sin_p = _rope_tables(np.arange(seq))
    cos_s, sin_s = _rope_tables(PAST_LEN + np.arange(n_new))

    hp = x_prompt.reshape(bsz * seq, D_MODEL)
    hs = x_sample.reshape(dbsz * n_new, D_MODEL)
    mem = mem_prompt.reshape(bsz * N_MEM, D_MODEL)
    outs = [[] for _ in range(8)]
    for layer in range(DEPTH):
        lb_raw = hgrn_lb_raw
        norm_g = hgrn_norm_g[layer][None]
        lg, lbias = ln_g[layer][None], ln_b[layer][None]

        proj, proj_s = _project(hp, hs, w_in[layer], PROJ_ROWS, PROJ_FIRST_COLS, PROJ_COLS)
        za, k1, v1 = _prompt_attention(proj, cos_p, sin_p, bsz, seq)
        zh, s1 = _prompt_hgrn(proj, lb_raw, norm_g, layer, bsz, seq)
        zm, mk1, mv1 = _prompt_mem(proj, mem, w_mem_kv[layer], bsz, seq)
        hp, w_out_b = _merge(za, zh, zm, hp, w_out[layer], lg, lbias, MERGE_ROWS)

        proj_s = proj_s.transpose(1, 0, 2)
        zs, k2, v2, s2 = _sample_mixers(proj_s, cos_s, sin_s, cache_win_k, cache_win_v, state_hgrn,
                                        cache_mem_k, cache_mem_v, lb_raw, norm_g, layer)
        zs = zs.transpose(1, 0, 2)
        hs = _merge(zs[:N_HEADS_ATTN], zs[N_HEADS_ATTN:N_HEADS_ATTN + N_HEADS_HGRN],
                    zs[N_HEADS_ATTN + N_HEADS_HGRN:], hs, w_out_b, lg, lbias, dbsz)

        new = (k1.transpose(0, 2, 1, 3), v1.transpose(0, 2, 1, 3), s1, mk1, mv1,
               k2.reshape(dbsz, n_new, N_HEADS_ATTN, HEAD_DIM), v2.reshape(dbsz, n_new, N_HEADS_ATTN, HEAD_DIM),
               s2.astype(state_hgrn.dtype))
        for acc, val in zip(outs, new):
            acc.append(val)

    return (hp.reshape(bsz, seq, D_MODEL), hs.reshape(dbsz, n_new, D_MODEL), *[jnp.stack(o) for o in outs])
```
